```python
import math
import jax, jax.numpy as jnp
from jax import lax
import numpy as np

D_MODEL = 1024
BATCH = 4
SEQ = 4096
DEPTH = 1

HEAD_DIM = 128
H_DELTA = D_MODEL // HEAD_DIM
DK_DELTA = HEAD_DIM
DV_DELTA = HEAD_DIM
CONV_WIDTH = 4
CHUNK = 64
H_DIFF = D_MODEL // HEAD_DIM
DH_DIFF = HEAD_DIM // 2
DV_DIFF = HEAD_DIM
Q_BLOCK = 128
N_BUCKETS = 32
MAX_DISTANCE = 128
N_EXPERTS = 32
TOP_K = 4
D_FF = D_MODEL
SWIGLU_LIMIT = 7.0
SWIGLU_ALPHA = 1.702
EXPERT_BLOCK = 256
EPS = 1e-6

W_DELTA_QK = H_DELTA * DK_DELTA
W_DELTA_V = H_DELTA * DV_DELTA
W_DIFF_QK = H_DIFF * 2 * DH_DIFF
W_DIFF_V = H_DIFF * DV_DIFF
CONV_CH = 2 * W_DELTA_QK + W_DELTA_V
IN_SPLITS = (W_DELTA_QK, W_DELTA_QK, W_DELTA_V, W_DELTA_V, H_DELTA, H_DELTA,
             W_DIFF_QK, W_DIFF_QK, W_DIFF_V, D_MODEL, D_MODEL)
D_IN = sum(IN_SPLITS)

kernel_name = 'hybrid_gdn_diffattn_moe_block'


def rms_norm(x, gain):
    xf = x.astype(jnp.float32)
    y = xf * lax.rsqrt(jnp.mean(xf * xf, axis=-1, keepdims=True) + EPS)
    return (y * gain.astype(jnp.float32)).astype(x.dtype)


def l2_normalize(x):
    xf = x.astype(jnp.float32)
    return xf * lax.rsqrt(jnp.sum(xf * xf, axis=-1, keepdims=True) + EPS)


def causal_depthwise_conv(x, w):
    k_width, ch = w.shape
    return lax.conv_general_dilated(
        x, w[:, None, :].astype(x.dtype), window_strides=(1,), padding=[(k_width - 1, 0)],
        dimension_numbers=('NWC', 'WIO', 'NWC'), feature_group_count=ch)


def gated_delta_rule(q, k, v, beta, g):
    bsz, seq, nh, dk = q.shape
    dv = v.shape[-1]
    n_chunks = seq // CHUNK

    def chunks(t):
        t = jnp.moveaxis(t.astype(jnp.float32), 2, 1)
        return t.reshape(t.shape[:2] + (n_chunks, CHUNK) + t.shape[3:])

    qc, kc, vc, bc = chunks(q), chunks(k), chunks(v), chunks(beta)
    gc = jnp.cumsum(chunks(g), axis=-1)
    incl = jnp.tril(jnp.ones((CHUNK, CHUNK), bool))
    strict = jnp.tril(jnp.ones((CHUNK, CHUNK), bool), k=-1)
    decay = jnp.exp(jnp.where(incl, gc[..., :, None] - gc[..., None, :], -jnp.inf))
    kb = kc * bc[..., None]
    a_mat = jnp.where(strict, jnp.einsum('bhnid,bhnjd->bhnij', kb, kc) * decay, 0.0)
    eye = jnp.eye(CHUNK, dtype=jnp.float32)
    t_mat = lax.linalg.triangular_solve(eye + a_mat, jnp.broadcast_to(eye, a_mat.shape),
                                        left_side=True, lower=True, unit_diagonal=True)
    u = jnp.einsum('bhnij,bhnje->bhnie', t_mat, vc * bc[..., None])
    w = jnp.einsum('bhnij,bhnjd->bhnid', t_mat, kb * jnp.exp(gc)[..., None])
    qk = jnp.einsum('bhnid,bhnjd->bhnij', qc, kc) * decay
    g_last = gc[..., -1]
    k_to_end = kc * jnp.exp(g_last[..., None] - gc)[..., None]
    q_dec = qc * jnp.exp(gc)[..., None]

    def step(state, xs):
        u_i, w_i, qk_i, q_i, k_i, gl_i = xs
        v_new = u_i - jnp.einsum('bhcd,bhde->bhce', w_i, state)
        o_i = jnp.einsum('bhcd,bhde->bhce', q_i, state) + jnp.einsum('bhij,bhje->bhie', qk_i, v_new)
        state = state * jnp.exp(gl_i)[..., None, None] + jnp.einsum('bhcd,bhce->bhde', k_i, v_new)
        return state, o_i

    xs = tuple(jnp.moveaxis(t, 2, 0) for t in (u, w, qk, q_dec, k_to_end, g_last))
    _, o = lax.scan(step, jnp.zeros((bsz, nh, dk, dv), jnp.float32), xs)
    o = jnp.moveaxis(o, 0, 2).reshape(bsz, nh, seq, dv)
    return jnp.moveaxis(o, 1, 2)


def t5_bucket(rel):
    n = jnp.maximum(-rel, 0)
    max_exact = N_BUCKETS // 2
    nf = jnp.maximum(n, 1).astype(jnp.float32)
    large = max_exact + (jnp.log(nf / max_exact) / math.log(MAX_DISTANCE / max_exact)
                         * (N_BUCKETS - max_exact)).astype(jnp.int32)
    large = jnp.minimum(large, N_BUCKETS - 1)
    return jnp.where(n < max_exact, n, large)


def diff_attention(q, k, v, lam, rel_bias):
    bsz, seq, nh, _, dh = q.shape
    dv = v.shape[-1]
    n_qb = seq // Q_BLOCK
    scale = dh ** -0.5
    qb = jnp.moveaxis(q.reshape(bsz, n_qb, Q_BLOCK, nh, 2, dh), 1, 0)
    k_pos = jnp.arange(seq, dtype=jnp.int32)

    def block(args):
        q_blk, start = args
        q_pos = start + jnp.arange(Q_BLOCK, dtype=jnp.int32)
        rel = k_pos[None, :] - q_pos[:, None]
        bias = jnp.transpose(rel_bias[t5_bucket(rel)], (2, 0, 1)).astype(jnp.float32)
        logits = jnp.einsum('bqhmd,bkhmd->bhmqk', q_blk, k).astype(jnp.float32) * scale
        logits = logits + bias[None, :, None]
        logits = jnp.where(rel[None, None, None] <= 0, logits, -jnp.inf)
        p = jax.nn.softmax(logits, axis=-1)
        attn = p[:, :, 0] - lam * p[:, :, 1]
        return jnp.einsum('bhqk,bkhe->bqhe', attn.astype(v.dtype), v)

    starts = jnp.arange(n_qb, dtype=jnp.int32) * Q_BLOCK
    out = lax.map(block, (qb, starts))
    return jnp.moveaxis(out, 0, 1).reshape(bsz, seq, nh, dv)


def clamped_swiglu(hid):
    glu, lin = hid[..., ::2], hid[..., 1::2]
    glu = jnp.minimum(glu, SWIGLU_LIMIT)
    lin = jnp.clip(lin, -SWIGLU_LIMIT, SWIGLU_LIMIT)
    return glu * jax.nn.sigmoid(SWIGLU_ALPHA * glu) * (lin + 1.0)


def moe_ffn(h, w_router, b_router, w_up, b_up, w_down, b_down):
    n_tok, d = h.shape
    logits = (h @ w_router + b_router).astype(jnp.float32)
    top_val, top_idx = lax.top_k(logits, TOP_K)
    top_w = jax.nn.softmax(top_val, axis=-1)
    n_assign = n_tok * TOP_K
    flat_e = top_idx.reshape(-1).astype(jnp.int32)
    flat_tok = jnp.repeat(jnp.arange(n_tok, dtype=jnp.int32), TOP_K)
    flat_w = top_w.reshape(-1)
    sorted_e, order = lax.sort((flat_e, jnp.arange(n_assign, dtype=jnp.int32)), num_keys=1, is_stable=True)
    counts = jnp.bincount(flat_e, length=N_EXPERTS)
    padded = (counts + EXPERT_BLOCK - 1) // EXPERT_BLOCK * EXPERT_BLOCK
    padded_end = jnp.cumsum(padded)
    padded_start = padded_end - padded
    start = jnp.cumsum(counts) - counts
    dest = padded_start[sorted_e] + jnp.arange(n_assign, dtype=jnp.int32) - start[sorted_e]
    n_blocks = -(-n_assign // EXPERT_BLOCK) + N_EXPERTS
    n_rows = n_blocks * EXPERT_BLOCK
    row_tok = jnp.full((n_rows,), n_tok, jnp.int32).at[dest].set(flat_tok[order])
    row_w = jnp.zeros((n_rows,), jnp.float32).at[dest].set(flat_w[order])
    block_e = jnp.minimum(jnp.searchsorted(padded_end, jnp.arange(n_blocks, dtype=jnp.int32) * EXPERT_BLOCK,
                                           side='right'), N_EXPERTS - 1)
    h_pad = jnp.concatenate([h, jnp.zeros((1, d), h.dtype)], axis=0)
    xs = h_pad[row_tok].reshape(n_blocks, EXPERT_BLOCK, d)

    def expert_block(args):
        xb, e = args
        hid = xb @ w_up[e] + b_up[e]
        return clamped_swiglu(hid) @ w_down[e] + b_down[e]

    ys = lax.map(expert_block, (xs, block_e)).reshape(n_rows, d)
    out = jnp.zeros((n_tok + 1, d), h.dtype).at[row_tok].add(ys * row_w[:, None].astype(h.dtype))
    return out[:n_tok]


def setup_inputs(seed: int = 0) -> dict:
    key = jax.random.key(seed)
    ks = jax.random.split(key, 24)

    def nrm(k, shape, scale):
        return scale * jax.random.normal(k, shape, jnp.float32)

    dt = jnp.exp(jax.random.uniform(ks[5], (DEPTH, H_DELTA), jnp.float32, math.log(1e-3), math.log(1e-1)))
    return {
        'x': nrm(ks[0], (BATCH, SEQ, D_MODEL), 1.0),
        'g_mix': 1.0 + nrm(ks[1], (DEPTH, D_MODEL), 0.02),
        'w_in': nrm(ks[2], (DEPTH, D_MODEL, D_IN), D_MODEL ** -0.5),
        'b_gate': nrm(ks[3], (DEPTH, 2, D_MODEL), 0.1),
        'conv_w': nrm(ks[4], (DEPTH, CONV_WIDTH, CONV_CH), CONV_WIDTH ** -0.5),
        'a_log': jnp.log(jax.random.uniform(ks[6], (DEPTH, H_DELTA), jnp.float32, 1.0, 16.0)),
        'dt_bias': dt + jnp.log(-jnp.expm1(-dt)),
        'g_delta_out': 1.0 + nrm(ks[7], (DEPTH, DV_DELTA), 0.02),
        'q_norm': 1.0 + nrm(ks[8], (DEPTH, DH_DIFF), 0.02),
        'k_norm': 1.0 + nrm(ks[9], (DEPTH, DH_DIFF), 0.02),
        'lambda_q1': nrm(ks[10], (DEPTH, DH_DIFF), 0.1),
        'lambda_k1': nrm(ks[11], (DEPTH, DH_DIFF), 0.1),
        'lambda_q2': nrm(ks[12], (DEPTH, DH_DIFF), 0.1),
        'lambda_k2': nrm(ks[13], (DEPTH, DH_DIFF), 0.1),
        'g_subln': 1.0 + nrm(ks[14], (DEPTH, DV_DIFF), 0.02),
        'rel_bias': nrm(ks[15], (N_BUCKETS, H_DIFF), 0.5),
        'w_o': nrm(ks[16], (DEPTH, D_MODEL, D_MODEL), D_MODEL ** -0.5),
        'g_ffn': 1.0 + nrm(ks[17], (DEPTH, D_MODEL), 0.02),
        'w_router': nrm(ks[18], (DEPTH, D_MODEL, N_EXPERTS), D_MODEL ** -0.5),
        'b_router': nrm(ks[19], (DEPTH, N_EXPERTS), 0.01),
        'w_up': nrm(ks[20], (DEPTH, N_EXPERTS, D_MODEL, 2 * D_FF), D_MODEL ** -0.5),
        'b_up': nrm(ks[21], (DEPTH, N_EXPERTS, 2 * D_FF), 0.02),
        'w_down': nrm(ks[22], (DEPTH, N_EXPERTS, D_FF, D_MODEL), D_FF ** -0.5),
        'b_down': nrm(ks[23], (DEPTH, N_EXPERTS, D_MODEL), 0.02),
    }


def reference(x, g_mix, w_in, b_gate, conv_w, a_log, dt_bias, g_delta_out, q_norm, k_norm,
              lambda_q1, lambda_k1, lambda_q2, lambda_k2, g_subln, rel_bias, w_o, g_ffn,
              w_router, b_router, w_up, b_up, w_down, b_down):
    bsz, seq, d = x.shape
    split_points = np.cumsum(IN_SPLITS)[:-1].tolist()
    for l in range(DEPTH):
        h = rms_norm(x, g_mix[l])
        proj = h @ w_in[l]
        (qa, ka, va, za, beta_in, a_in, qd, kd, vd, gate_a, gate_b) = jnp.split(proj, split_points, axis=-1)

        qkv = jax.nn.silu(causal_depthwise_conv(jnp.concatenate([qa, ka, va], axis=-1), conv_w[l]))
        qa, ka, va = jnp.split(qkv, [W_DELTA_QK, 2 * W_DELTA_QK], axis=-1)
        qa = l2_normalize(qa.reshape(bsz, seq, H_DELTA, DK_DELTA)) * (DK_DELTA ** -0.5)
        ka = l2_normalize(ka.reshape(bsz, seq, H_DELTA, DK_DELTA))
        va = va.reshape(bsz, seq, H_DELTA, DV_DELTA)
        beta = jax.nn.sigmoid(beta_in.astype(jnp.float32))
        g = -jnp.exp(a_log[l].astype(jnp.float32)) * jax.nn.softplus(
            a_in.astype(jnp.float32) + dt_bias[l].astype(jnp.float32))
        oa = gated_delta_rule(qa, ka, va, beta, g)
        oa = rms_norm(oa, g_delta_out[l]) * jax.nn.silu(za.reshape(bsz, seq, H_DELTA, DV_DELTA).astype(jnp.float32))
        oa = oa.reshape(bsz, seq, W_DELTA_V).astype(x.dtype)

        qd = rms_norm(qd.reshape(bsz, seq, H_DIFF, 2, DH_DIFF), q_norm[l])
        kd = rms_norm(kd.reshape(bsz, seq, H_DIFF, 2, DH_DIFF), k_norm[l])
        vd = vd.reshape(bsz, seq, H_DIFF, DV_DIFF)
        lam_init = 0.8 - 0.6 * math.exp(-0.3 * l)
        lam = (jnp.exp(jnp.sum(lambda_q1[l].astype(jnp.float32) * lambda_k1[l].astype(jnp.float32)))
               - jnp.exp(jnp.sum(lambda_q2[l].astype(jnp.float32) * lambda_k2[l].astype(jnp.float32)))
               + lam_init)
        od = diff_attention(qd, kd, vd, lam, rel_bias)
        od = (rms_norm(od, g_subln[l]) * (1.0 - lam_init)).reshape(bsz, seq, W_DIFF_V).astype(x.dtype)

        mix = jax.nn.sigmoid(gate_a + b_gate[l, 0]) * oa + jax.nn.sigmoid(gate_b + b_gate[l, 1]) * od
        x = x + mix @ w_o[l]

        h2 = rms_norm(x, g_ffn[l]).reshape(bsz * seq, d)
        x = x + moe_ffn(h2, w_router[l], b_router[l], w_up[l], b_up[l], w_down[l], b_down[l]).reshape(bsz, seq, d)
    return x
```

```python
import functools
import math

import jax
import jax.numpy as jnp
from jax import lax
from jax.experimental import pallas as pl
from jax.experimental.pallas import tpu as pltpu

F32 = jnp.float32
BF16 = jnp.bfloat16

HEAD_DIM = 128
DH_DIFF = HEAD_DIM // 2
CONV_WIDTH = 4
CHUNK = 64
N_BUCKETS = 32
MAX_DISTANCE = 128
N_EXPERTS = 32
TOP_K = 4
TOP_K_SHIFT = 2
SWIGLU_LIMIT = 7.0
SWIGLU_ALPHA = 1.702
EPS = 1e-6
NEG_BIG = -1e30

LANES = 128
SUBLANES = 8
VMEM_LIMIT = 56 * 1024 * 1024

PROJ_TM = 1024
PROJ_TN = 1024
GDN_TB = 256
GDN_HG = 2
ATT_BQ = 256
ATT_BK = 256
MIX_TM = 512
MOE_RB = 256
DISP_TD = 256
COMB_TC = 128


def _cparams(sem):
    return pltpu.CompilerParams(dimension_semantics=sem, vmem_limit_bytes=VMEM_LIMIT)


def _bdot(a, b):
    return jnp.dot(a.astype(BF16), b.astype(BF16), preferred_element_type=F32)


def _bdot_nt(a, b):
    return lax.dot_general(a.astype(BF16), b.astype(BF16), (((1,), (1,)), ((), ())),
                           preferred_element_type=F32)


def _bdot_tn(a, b):
    return lax.dot_general(a.astype(BF16), b.astype(BF16), (((0,), (0,)), ((), ())),
                           preferred_element_type=F32)


def _fdot(a, b):
    return jnp.dot(a, b, preferred_element_type=F32, precision=lax.Precision.HIGHEST)


def _proj_kernel(x_ref, g_ref, w_ref, aux_ref, o_ref, h_ref, *, n_plain_a, n_norm, n_plain_b):
    j = pl.program_id(1)

    @pl.when(j == 0)
    def _():
        x = x_ref[...]
        ms = jnp.mean(x * x, axis=-1, keepdims=True)
        h_ref[...] = (x * lax.rsqrt(ms + EPS) * g_ref[...]).astype(BF16)

    acc = jnp.dot(h_ref[...], w_ref[...], preferred_element_type=F32)
    first_norm = n_plain_a
    first_plain_b = n_plain_a + n_norm
    first_gate = first_plain_b + n_plain_b

    @pl.when(jnp.logical_or(j < first_norm, jnp.logical_and(j >= first_plain_b, j < first_gate)))
    def _():
        o_ref[...] = acc.astype(o_ref.dtype)

    @pl.when(jnp.logical_and(j >= first_norm, j < first_plain_b))
    def _():
        lane = lax.broadcasted_iota(jnp.int32, (1, LANES), 1)
        lo = lane < DH_DIFF
        for c in range(PROJ_TN // LANES):
            sl = slice(c * LANES, (c + 1) * LANES)
            y = acc[:, sl]
            y2 = y * y
            s_lo = jnp.sum(jnp.where(lo, y2, 0.0), axis=-1, keepdims=True)
            s_hi = jnp.sum(jnp.where(lo, 0.0, y2), axis=-1, keepdims=True)
            r = jnp.where(lo, lax.rsqrt(s_lo / DH_DIFF + EPS), lax.rsqrt(s_hi / DH_DIFF + EPS))
            o_ref[:, sl] = (y * r * aux_ref[:, sl]).astype(o_ref.dtype)

    @pl.when(j >= first_gate)
    def _():
        o_ref[...] = jax.nn.sigmoid(acc + aux_ref[...]).astype(o_ref.dtype)


def _input_projection(x2d, g_mix, w_cat, aux, *, n_plain_a, n_norm, n_plain_b):
    t, d = x2d.shape
    n = w_cat.shape[1]
    kern = functools.partial(_proj_kernel, n_plain_a=n_plain_a, n_norm=n_norm, n_plain_b=n_plain_b)
    return pl.pallas_call(
        kern,
        grid=(t // PROJ_TM, n // PROJ_TN),
        in_specs=[
            pl.BlockSpec((PROJ_TM, d), lambda i, j: (i, 0)),
            pl.BlockSpec((1, d), lambda i, j: (0, 0)),
            pl.BlockSpec((d, PROJ_TN), lambda i, j: (0, j)),
            pl.BlockSpec((1, PROJ_TN), lambda i, j: (0, j)),
        ],
        out_specs=pl.BlockSpec((PROJ_TM, PROJ_TN), lambda i, j: (i, j)),
        out_shape=jax.ShapeDtypeStruct((t, n), BF16),
        scratch_shapes=[pltpu.VMEM((PROJ_TM, d), BF16)],
        compiler_params=_cparams(("parallel", "arbitrary")),
        name="input_projection",
    )(x2d, g_mix, w_cat, aux)


def _small_proj_kernel(x_ref, g_ref, w_ref, wt_ref, alog_ref, dtb_ref, alog_t_ref, dtb_t_ref,
                       o_ref, ot_ref, *, n_heads):
    x = x_ref[...]
    ms = jnp.mean(x * x, axis=-1, keepdims=True)
    h = (x * lax.rsqrt(ms + EPS) * g_ref[...]).astype(BF16)

    def finish(acc, idx, alog, dtb):
        beta = jax.nn.sigmoid(acc)
        z = acc + dtb
        softplus = jnp.maximum(z, 0.0) + jnp.log1p(jnp.exp(-jnp.abs(z)))
        gdec = -jnp.exp(alog) * softplus
        return jnp.where(idx < n_heads, beta, jnp.where(idx < 2 * n_heads, gdec, 0.0))

    acc = jnp.dot(h, w_ref[...], preferred_element_type=F32)
    lane = lax.broadcasted_iota(jnp.int32, acc.shape, 1)
    o_ref[...] = finish(acc, lane, alog_ref[...], dtb_ref[...])
    acc_t = lax.dot_general(wt_ref[...], h, (((1,), (1,)), ((), ())),
                            preferred_element_type=F32)
    sub = lax.broadcasted_iota(jnp.int32, acc_t.shape, 0)
    ot_ref[...] = finish(acc_t, sub, alog_t_ref[...], dtb_t_ref[...])


def _small_projection(x2d, g_mix, w_small, w_small_t, alog, dtb, alog_t, dtb_t, n_heads):
    t, d = x2d.shape
    rows_t = w_small_t.shape[0]
    tm = PROJ_TM
    full = lambda shape: pl.BlockSpec(shape, lambda i: (0, 0))
    return pl.pallas_call(
        functools.partial(_small_proj_kernel, n_heads=n_heads),
        grid=(t // tm,),
        in_specs=[
            pl.BlockSpec((tm, d), lambda i: (i, 0)),
            full((1, d)), full((d, LANES)), full((rows_t, d)),
            full((1, LANES)), full((1, LANES)), full((rows_t, 1)), full((rows_t, 1)),
        ],
        out_specs=[pl.BlockSpec((tm, LANES), lambda i: (i, 0)),
                   pl.BlockSpec((rows_t, tm), lambda i: (0, i))],
        out_shape=[jax.ShapeDtypeStruct((t, LANES), F32),
                   jax.ShapeDtypeStruct((rows_t, t), F32)],
        compiler_params=_cparams(("parallel",)),
        name="beta_decay_projection",
    )(x2d, g_mix, w_small, w_small_t, alog, dtb, alog_t, dtb_t)


def _gdn_kernel(q_ref, k_ref, v_ref, z_ref, sm_ref, smt_ref, cwq_ref, cwk_ref, cwv_ref, gout_ref,
                o_ref, state_ref, qp_ref, kp_ref, vp_ref, vn_ref, *, n_heads):
    hg = pl.program_id(1)
    s = pl.program_id(2)
    tb = GDN_TB
    pad = SUBLANES
    width = GDN_HG * HEAD_DIM

    @pl.when(s == 0)
    def _():
        state_ref[...] = jnp.zeros_like(state_ref)
        for p_ref in (qp_ref, kp_ref, vp_ref):
            p_ref[0:pad, :] = jnp.zeros((pad, width), F32)

    def conv_silu(x_ref, p_ref, cw_ref):
        p_ref[pad:pad + tb, :] = x_ref[...].astype(F32)
        acc = cw_ref[CONV_WIDTH - 1:CONV_WIDTH, :] * p_ref[pad:pad + tb, :]
        for jj in range(CONV_WIDTH - 1):
            off = pad - (CONV_WIDTH - 1) + jj
            acc = acc + cw_ref[jj:jj + 1, :] * p_ref[off:off + tb, :]
        p_ref[0:pad, :] = p_ref[tb:tb + pad, :]
        return acc * jax.nn.sigmoid(acc)

    q_all = conv_silu(q_ref, qp_ref, cwq_ref)
    k_all = conv_silu(k_ref, kp_ref, cwk_ref)
    v_all = conv_silu(v_ref, vp_ref, cwv_ref)

    r = lax.broadcasted_iota(jnp.int32, (tb, tb), 0)
    c = lax.broadcasted_iota(jnp.int32, (tb, tb), 1)
    shift = int(math.log2(CHUNK))
    same = (r >> shift) == (c >> shift)
    incl = jnp.logical_and(same, c <= r)
    strict = jnp.logical_and(same, c < r)
    eye = (r == c).astype(F32)

    small = sm_ref[...]
    small_t = smt_ref[...]
    lane = lax.broadcasted_iota(jnp.int32, small.shape, 1)
    sub = lax.broadcasted_iota(jnp.int32, small_t.shape, 0)
    gcum = _fdot(incl.astype(F32), small)
    gtot = _fdot(same.astype(F32), small)
    gcum_t = _fdot(small_t, jnp.logical_and(same, r <= c).astype(F32))

    for hh in range(GDN_HG):
        head = hg * GDN_HG + hh
        hs = slice(hh * HEAD_DIM, (hh + 1) * HEAD_DIM)
        q = q_all[:, hs]
        k = k_all[:, hs]
        v = v_all[:, hs]
        q = q * lax.rsqrt(jnp.sum(q * q, axis=-1, keepdims=True) + EPS) * (HEAD_DIM ** -0.5)
        k = k * lax.rsqrt(jnp.sum(k * k, axis=-1, keepdims=True) + EPS)

        beta = jnp.sum(jnp.where(lane == head, small, 0.0), axis=-1, keepdims=True)
        gsel = lane == head + n_heads
        gc = jnp.sum(jnp.where(gsel, gcum, 0.0), axis=-1, keepdims=True)
        gl = jnp.sum(jnp.where(gsel, gtot, 0.0), axis=-1, keepdims=True)
        gc_row = jnp.sum(jnp.where(sub == head + n_heads, gcum_t, 0.0), axis=0, keepdims=True)

        decay = jnp.where(incl, jnp.exp(jnp.minimum(gc - gc_row, 0.0)), 0.0)
        kb = k * beta
        a_neg = jnp.where(strict, -(_bdot_nt(kb, k) * decay), 0.0)
        tmat = eye + a_neg
        pw = a_neg
        for _ in range(int(math.log2(CHUNK)) - 1):
            pw = _bdot(pw, pw)
            tmat = tmat + _bdot(tmat, pw)
        egc = jnp.exp(gc)
        uw = _bdot(tmat, jnp.concatenate([v * beta, kb * egc], axis=1))
        u = uw[:, :HEAD_DIM]
        w = uw[:, HEAD_DIM:]
        qk = jnp.where(incl, _bdot_nt(q, k) * decay, 0.0)
        q_dec = q * egc
        k_end = k * jnp.exp(gl - gc)

        vn_ref[hh] = jnp.zeros((tb, HEAD_DIM), F32)
        outs = []
        for ci in range(tb // CHUNK):
            cs = slice(ci * CHUNK, (ci + 1) * CHUNK)
            st = state_ref[hh]
            ws_qs = _bdot(jnp.concatenate([w[cs], q_dec[cs]], axis=0), st)
            v_new = u[cs] - ws_qs[:CHUNK]
            vn_ref[hh, cs, :] = v_new
            outs.append(ws_qs[CHUNK:] + _bdot(qk[cs], vn_ref[hh]))
            g_last = gl[ci * CHUNK:ci * CHUNK + 1, :]
            state_ref[hh] = st * jnp.exp(g_last) + _bdot_tn(k_end[cs], v_new)
        o = jnp.concatenate(outs, axis=0)
        o = o * lax.rsqrt(jnp.mean(o * o, axis=-1, keepdims=True) + EPS) * gout_ref[...]
        zz = z_ref[:, hs].astype(F32)
        o_ref[:, hs] = (o * (zz * jax.nn.sigmoid(zz))).astype(o_ref.dtype)


def _gated_delta(big, small, small_t, conv_w, g_out, bsz, seq, n_heads, d_model):
    t = bsz * seq
    tb = GDN_TB
    ns = seq // tb
    width = GDN_HG * HEAD_DIM
    nhg = n_heads // GDN_HG
    blocks_per_group = d_model // width
    rows_t = small_t.shape[0]

    def colspec(group):
        return pl.BlockSpec((tb, width), lambda b, h, s: (b * ns + s, group * blocks_per_group + h))

    def cwspec(group):
        return pl.BlockSpec((CONV_WIDTH, width), lambda b, h, s: (0, group * blocks_per_group + h))

    return pl.pallas_call(
        functools.partial(_gdn_kernel, n_heads=n_heads),
        grid=(bsz, nhg, ns),
        in_specs=[
            colspec(0), colspec(1), colspec(2), colspec(3),
            pl.BlockSpec((tb, LANES), lambda b, h, s: (b * ns + s, 0)),
            pl.BlockSpec((rows_t, tb), lambda b, h, s: (0, b * ns + s)),
            cwspec(0), cwspec(1), cwspec(2),
            pl.BlockSpec((1, HEAD_DIM), lambda b, h, s: (0, 0)),
        ],
        out_specs=pl.BlockSpec((tb, width), lambda b, h, s: (b * ns + s, h)),
        out_shape=jax.ShapeDtypeStruct((t, d_model), BF16),
        scratch_shapes=[
            pltpu.VMEM((GDN_HG, HEAD_DIM, HEAD_DIM), F32),
            pltpu.VMEM((tb + SUBLANES, width), F32),
            pltpu.VMEM((tb + SUBLANES, width), F32),
            pltpu.VMEM((tb + SUBLANES, width), F32),
            pltpu.VMEM((GDN_HG, tb, HEAD_DIM), F32),
        ],
        compiler_params=_cparams(("parallel", "parallel", "arbitrary")),
        name="gated_delta",
    )(big, big, big, big, small, small_t, conv_w, conv_w, conv_w, g_out)


def _t5_bucket(n):
    max_exact = N_BUCKETS // 2
    nf = jnp.maximum(n, 1).astype(F32)
    large = max_exact + (jnp.log(nf / max_exact) / math.log(MAX_DISTANCE / max_exact)
                         * (N_BUCKETS - max_exact)).astype(jnp.int32)
    large = jnp.minimum(large, N_BUCKETS - 1)
    return jnp.where(n < max_exact, n, large)


def _attn_kernel(rb_ref, q_ref, k_ref, v_ref, lam_ref, gsub_ref, o_ref,
                 bias_ref, m_ref, l_ref, acc_ref, *, lam_init):
    h = pl.program_id(0)
    b = pl.program_id(1)
    qi = pl.program_id(2)
    bq, bk = ATT_BQ, ATT_BK

    @pl.when(jnp.logical_and(b == 0, qi == 0))
    def _():
        i = lax.broadcasted_iota(jnp.int32, (bq, bk), 0)
        jj = lax.broadcasted_iota(jnp.int32, (bq, bk), 1)
        far = rb_ref[N_BUCKETS - 1, h]
        for slot in range(2):
            n = i - jj + slot * bk
            bucket = _t5_bucket(jnp.maximum(n, 0))
            bias = jnp.zeros((bq, bk), F32)
            for cc in range(N_BUCKETS):
                bias = jnp.where(bucket == cc, rb_ref[cc, h] - far, bias)
            if slot == 0:
                bias = jnp.where(n >= 0, bias, NEG_BIG)
            bias_ref[slot] = bias

    m_ref[...] = jnp.full(m_ref.shape, NEG_BIG, F32)
    l_ref[...] = jnp.zeros(l_ref.shape, F32)
    acc_ref[...] = jnp.zeros(acc_ref.shape, F32)

    q = q_ref[...]
    lane = lax.broadcasted_iota(jnp.int32, q.shape, 1)
    zero = jnp.zeros_like(q)
    q_maps = (jnp.where(lane < DH_DIFF, q, zero), jnp.where(lane < DH_DIFF, zero, q))

    def tile(j, bias):
        ks = pl.multiple_of(j * bk, bk)
        k = k_ref[pl.ds(ks, bk), :]
        v = v_ref[pl.ds(ks, bk), :]
        for mp in range(2):
            sc = lax.dot_general(q_maps[mp], k, (((1,), (1,)), ((), ())), preferred_element_type=F32)
            if bias is not None:
                sc = sc + bias
            m_old = m_ref[mp]
            m_new = jnp.maximum(m_old, jnp.max(sc, axis=-1, keepdims=True))
            p = jnp.exp(sc - m_new)
            alpha = jnp.exp(m_old - m_new)
            l_ref[mp] = alpha * l_ref[mp] + jnp.sum(p, axis=-1, keepdims=True)
            acc_ref[mp] = alpha * acc_ref[mp] + jnp.dot(p.astype(BF16), v, preferred_element_type=F32)
            m_ref[mp] = m_new

    def far_body(j, carry):
        tile(j, None)
        return carry

    lax.fori_loop(0, jnp.maximum(qi - 1, 0), far_body, 0)

    @pl.when(qi >= 1)
    def _():
        tile(qi - 1, bias_ref[1])

    tile(qi, bias_ref[0])

    lam_p = lam_ref[...]
    s1 = jnp.sum(lam_p[0:1] * lam_p[1:2], axis=-1, keepdims=True)
    s2 = jnp.sum(lam_p[2:3] * lam_p[3:4], axis=-1, keepdims=True)
    lam = jnp.exp(s1) - jnp.exp(s2) + lam_init
    o = acc_ref[0] / l_ref[0] - lam * (acc_ref[1] / l_ref[1])
    o = o * lax.rsqrt(jnp.mean(o * o, axis=-1, keepdims=True) + EPS) * gsub_ref[...]
    o_ref[...] = (o * (1.0 - lam_init)).astype(o_ref.dtype)


def _diff_attention(big, rel_bias, lam_params, g_subln, bsz, seq, n_heads, d_model, lam_init):
    t = bsz * seq
    nq = seq // ATT_BQ
    qcol = 4 * d_model // HEAD_DIM
    per = d_model // HEAD_DIM
    return pl.pallas_call(
        functools.partial(_attn_kernel, lam_init=lam_init),
        grid=(n_heads, bsz, nq),
        in_specs=[
            pl.BlockSpec(memory_space=pltpu.SMEM),
            pl.BlockSpec((ATT_BQ, HEAD_DIM), lambda h, b, i: (b * nq + i, qcol + h)),
            pl.BlockSpec((seq, HEAD_DIM), lambda h, b, i: (b, qcol + per + h)),
            pl.BlockSpec((seq, HEAD_DIM), lambda h, b, i: (b, qcol + 2 * per + h)),
            pl.BlockSpec((4, DH_DIFF), lambda h, b, i: (0, 0)),
            pl.BlockSpec((1, HEAD_DIM), lambda h, b, i: (0, 0)),
        ],
        out_specs=pl.BlockSpec((ATT_BQ, HEAD_DIM), lambda h, b, i: (b * nq + i, h)),
        out_shape=jax.ShapeDtypeStruct((t, d_model), BF16),
        scratch_shapes=[
            pltpu.VMEM((2, ATT_BQ, ATT_BK), F32),
            pltpu.VMEM((2, ATT_BQ, 1), F32),
            pltpu.VMEM((2, ATT_BQ, 1), F32),
            pltpu.VMEM((2, ATT_BQ, HEAD_DIM), F32),
        ],
        compiler_params=_cparams(("arbitrary", "arbitrary", "arbitrary")),
        name="diff_attention",
    )(rel_bias, big, big, big, lam_params, g_subln)


def _mix_kernel(ga_ref, gb_ref, oa_ref, od_ref, x_ref, wo_ref, gffn_ref, wr_ref, br_ref,
                x1_ref, h2_ref, topi_ref, topw_ref, rank_ref, cnt_ref, carry_ref):
    i = pl.program_id(0)
    tm = MIX_TM

    @pl.when(i == 0)
    def _():
        carry_ref[...] = jnp.zeros_like(carry_ref)

    mix = (ga_ref[...].astype(F32) * oa_ref[...].astype(F32)
           + gb_ref[...].astype(F32) * od_ref[...].astype(F32))
    x1 = x_ref[...] + jnp.dot(mix.astype(BF16), wo_ref[...], preferred_element_type=F32)
    x1_ref[...] = x1
    h2 = x1 * lax.rsqrt(jnp.mean(x1 * x1, axis=-1, keepdims=True) + EPS) * gffn_ref[...]
    h2_ref[...] = h2

    logits = lax.dot_general(wr_ref[...], h2, (((1,), (1,)), ((), ())),
                             preferred_element_type=F32, precision=lax.Precision.HIGHEST) + br_ref[...]
    eidx = lax.broadcasted_iota(jnp.int32, logits.shape, 0).astype(F32)
    vals, hots = [], []
    cur = logits
    for kk in range(TOP_K):
        mx = jnp.max(cur, axis=0, keepdims=True)
        idx = jnp.min(jnp.where(cur == mx, eidx, float(N_EXPERTS)), axis=0, keepdims=True)
        hot = eidx == idx
        vals.append(mx)
        hots.append(hot)
        topi_ref[kk:kk + 1, :] = idx.astype(jnp.int32)
        cur = jnp.where(hot, -jnp.inf, cur)
    exps = [jnp.exp(vv - vals[0]) for vv in vals]
    denom = exps[0] + exps[1] + exps[2] + exps[3]
    for kk in range(TOP_K):
        topw_ref[kk:kk + 1, :] = exps[kk] / denom

    sel = hots[0]
    for kk in range(1, TOP_K):
        sel = jnp.logical_or(sel, hots[kk])
    sel_f = sel.astype(F32)
    r = lax.broadcasted_iota(jnp.int32, (tm, tm), 0)
    c = lax.broadcasted_iota(jnp.int32, (tm, tm), 1)
    before = _bdot(sel_f, (r < c).astype(F32)) + carry_ref[...]
    for kk in range(TOP_K):
        rank_ref[kk:kk + 1, :] = jnp.sum(jnp.where(hots[kk], before, 0.0), axis=0,
                                         keepdims=True).astype(jnp.int32)
    carry_ref[...] = carry_ref[...] + jnp.sum(sel_f, axis=-1, keepdims=True)
    cnt_ref[...] = carry_ref[...].astype(jnp.int32)


def _mix_project_route(big, oa, od, x2d, w_o, g_ffn, w_r_t, b_r, d_model):
    t = x2d.shape[0]
    tm = MIX_TM
    gate_blk = 7
    full = lambda shape: pl.BlockSpec(shape, lambda i: (0, 0))
    row = lambda: pl.BlockSpec((tm, d_model), lambda i: (i, 0))
    krow = lambda: pl.BlockSpec((TOP_K, tm), lambda i: (0, i))
    return pl.pallas_call(
        _mix_kernel,
        grid=(t // tm,),
        in_specs=[
            pl.BlockSpec((tm, d_model), lambda i: (i, gate_blk)),
            pl.BlockSpec((tm, d_model), lambda i: (i, gate_blk + 1)),
            row(), row(), row(),
            full((d_model, d_model)), full((1, d_model)), full((N_EXPERTS, d_model)), full((N_EXPERTS, 1)),
        ],
        out_specs=[row(), row(), krow(), krow(), krow(), full((N_EXPERTS, 1))],
        out_shape=[
            jax.ShapeDtypeStruct((t, d_model), F32),
            jax.ShapeDtypeStruct((t, d_model), F32),
            jax.ShapeDtypeStruct((TOP_K, t), jnp.int32),
            jax.ShapeDtypeStruct((TOP_K, t), F32),
            jax.ShapeDtypeStruct((TOP_K, t), jnp.int32),
            jax.ShapeDtypeStruct((N_EXPERTS, 1), jnp.int32),
        ],
        scratch_shapes=[pltpu.VMEM((N_EXPERTS, 1), F32)],
        compiler_params=_cparams(("arbitrary",)),
        name="merge_outproj_route",
    )(big, big, oa, od, x2d, w_o, g_ffn, w_r_t, b_r)


def _row_copy(src_ref, src_row, dst_ref, dst_row, sem):
    return pltpu.make_async_copy(src_ref.at[pl.ds(src_row, 1)], dst_ref.at[pl.ds(dst_row, 1)], sem)


def _dispatch_kernel(dest_ref, h2_ref, xs_in_ref, xs_ref, sem):
    del xs_in_ref
    i = pl.program_id(0)
    n = DISP_TD * TOP_K

    def start(a, carry):
        _row_copy(h2_ref, i * DISP_TD + lax.shift_right_logical(a, TOP_K_SHIFT), xs_ref, dest_ref[a],
                  sem).start()
        return carry

    lax.fori_loop(0, n, start, 0)

    def wait(a, carry):
        _row_copy(h2_ref, 0, xs_ref, 0, sem).wait()
        return carry

    lax.fori_loop(0, n, wait, 0)


def _dispatch(dest_flat, h2, n_rows):
    t, d = h2.shape
    xs0 = jnp.zeros((n_rows, d), F32)
    return pl.pallas_call(
        _dispatch_kernel,
        grid=(t // DISP_TD,),
        in_specs=[
            pl.BlockSpec((DISP_TD * TOP_K,), lambda i: (i,), memory_space=pltpu.SMEM),
            pl.BlockSpec(memory_space=pl.ANY),
            pl.BlockSpec(memory_space=pl.ANY),
        ],
        out_specs=pl.BlockSpec(memory_space=pl.ANY),
        out_shape=jax.ShapeDtypeStruct((n_rows, d), F32),
        scratch_shapes=[pltpu.SemaphoreType.DMA],
        input_output_aliases={2: 0},
        compiler_params=_cparams(("arbitrary",)),
        name="moe_dispatch",
    )(dest_flat, h2, xs0)


def _expert_kernel(be_ref, nu_ref, xs_ref, wup_ref, bup_ref, wdn_ref, bdn_ref, ys_ref):
    i = pl.program_id(0)
    d_ff = wdn_ref.shape[1]

    @pl.when(i < nu_ref[0])
    def _():
        hid = jnp.dot(xs_ref[...].astype(BF16), wup_ref[0], preferred_element_type=F32) + bup_ref[0]
        glu = jnp.minimum(hid[:, :d_ff], SWIGLU_LIMIT)
        lin = jnp.clip(hid[:, d_ff:], -SWIGLU_LIMIT, SWIGLU_LIMIT)
        act = glu * jax.nn.sigmoid(SWIGLU_ALPHA * glu) * (lin + 1.0)
        ys_ref[...] = jnp.dot(act.astype(BF16), wdn_ref[0], preferred_element_type=F32) + bdn_ref[0]

    @pl.when(i >= nu_ref[0])
    def _():
        ys_ref[...] = jnp.zeros_like(ys_ref)


def _experts(block_e, n_used, xs, w_up, b_up, w_down, b_down):
    n_rows, d = xs.shape
    nb = n_rows // MOE_RB
    two_ff = w_up.shape[2]
    d_ff = w_down.shape[1]
    grid_spec = pltpu.PrefetchScalarGridSpec(
        num_scalar_prefetch=2,
        grid=(nb,),
        in_specs=[
            pl.BlockSpec((MOE_RB, d), lambda i, be, nu: (jnp.minimum(i, nu[0] - 1), 0)),
            pl.BlockSpec((1, d, two_ff), lambda i, be, nu: (be[i], 0, 0)),
            pl.BlockSpec((1, 1, two_ff), lambda i, be, nu: (be[i], 0, 0)),
            pl.BlockSpec((1, d_ff, d), lambda i, be, nu: (be[i], 0, 0)),
            pl.BlockSpec((1, 1, d), lambda i, be, nu: (be[i], 0, 0)),
        ],
        out_specs=pl.BlockSpec((MOE_RB, d), lambda i, be, nu: (i, 0)),
    )
    return pl.pallas_call(
        _expert_kernel,
        grid_spec=grid_spec,
        out_shape=jax.ShapeDtypeStruct((n_rows, d), F32),
        compiler_params=_cparams(("arbitrary",)),
        name="moe_experts",
    )(block_e, n_used, xs, w_up, b_up, w_down, b_down)


def _combine_kernel(dest_ref, x1_ref, w_ref, ys_ref, o_ref, buf_ref, sem):
    tc = COMB_TC
    n = tc * TOP_K

    def start(a, carry):
        _row_copy(ys_ref, dest_ref[a], buf_ref.at[a & (TOP_K - 1)],
                  lax.shift_right_logical(a, TOP_K_SHIFT), sem).start()
        return carry

    lax.fori_loop(0, n, start, 0)

    def wait(a, carry):
        _row_copy(ys_ref, 0, buf_ref.at[0], 0, sem).wait()
        return carry

    lax.fori_loop(0, n, wait, 0)

    w = w_ref[...]
    out = x1_ref[...]
    for kk in range(TOP_K):
        out = out + w[:, kk:kk + 1] * buf_ref[kk]
    o_ref[...] = out


def _combine(dest_flat, x1, w_tok, ys):
    t, d = x1.shape
    tc = COMB_TC
    return pl.pallas_call(
        _combine_kernel,
        grid=(t // tc,),
        in_specs=[
            pl.BlockSpec((tc * TOP_K,), lambda i: (i,), memory_space=pltpu.SMEM),
            pl.BlockSpec((tc, d), lambda i: (i, 0)),
            pl.BlockSpec((tc, TOP_K), lambda i: (i, 0)),
            pl.BlockSpec(memory_space=pl.ANY),
        ],
        out_specs=pl.BlockSpec((tc, d), lambda i: (i, 0)),
        out_shape=jax.ShapeDtypeStruct((t, d), F32),
        scratch_shapes=[pltpu.VMEM((TOP_K, tc, d), F32), pltpu.SemaphoreType.DMA],
        compiler_params=_cparams(("arbitrary",)),
        name="moe_combine",
    )(dest_flat, x1, w_tok, ys)


def _moe(x1, h2, topi, topw, rank, counts, w_up, b_up, w_down, b_down):
    t, d = x1.shape
    n_assign = t * TOP_K
    nb = -(-n_assign // MOE_RB) + N_EXPERTS
    n_rows = nb * MOE_RB
    counts = counts[:, 0]
    padded = (counts + MOE_RB - 1) // MOE_RB * MOE_RB
    padded_end = jnp.cumsum(padded)
    padded_start = padded_end - padded
    dest = (jnp.take(padded_start, topi) + rank).astype(jnp.int32)
    dest_flat = dest.T.reshape(-1)
    n_used = (padded_end[-1] // MOE_RB).astype(jnp.int32)
    blk = jnp.arange(nb, dtype=jnp.int32)
    blk = jnp.minimum(blk, n_used - 1)
    block_e = jnp.minimum(jnp.searchsorted(padded_end, blk * MOE_RB, side='right'),
                          N_EXPERTS - 1).astype(jnp.int32)

    xs = _dispatch(dest_flat, h2, n_rows)
    ys = _experts(block_e, n_used.reshape(1), xs, w_up, b_up, w_down, b_down)
    return _combine(dest_flat, x1, topw.T, ys)


def kernel(x, g_mix, w_in, b_gate, conv_w, a_log, dt_bias, g_delta_out, q_norm, k_norm, lambda_q1, lambda_k1, lambda_q2, lambda_k2, g_subln, rel_bias, w_o, g_ffn, w_router, b_router, w_up, b_up, w_down, b_down):
    bsz, seq, d = x.shape
    depth = g_mix.shape[0]
    n_heads = d // HEAD_DIM
    t = bsz * seq
    d_ff = w_down.shape[2]
    assert d % PROJ_TN == 0 and t % PROJ_TM == 0 and seq % GDN_TB == 0 and seq % ATT_BQ == 0
    assert t % MIX_TM == 0 and t % DISP_TD == 0 and t % COMB_TC == 0 and n_heads % GDN_HG == 0
    assert 2 * n_heads <= 2 * SUBLANES

    x2d = x.reshape(t, d)
    for l in range(depth):
        wl = w_in[l]
        c0 = 4 * d
        c1 = c0 + 2 * n_heads
        w_cat = jnp.concatenate([wl[:, :c0], wl[:, c1:]], axis=1).astype(BF16)
        w_small = jnp.pad(wl[:, c0:c1], ((0, 0), (0, LANES - 2 * n_heads)))
        qk_gain = jnp.concatenate([jnp.tile(q_norm[l] * (DH_DIFF ** -0.5), 2 * n_heads),
                                   jnp.tile(k_norm[l], 2 * n_heads)])
        aux = jnp.concatenate([jnp.zeros((4 * d,), F32), qk_gain, jnp.zeros((d,), F32),
                               b_gate[l].reshape(-1)]).reshape(1, -1)
        big = _input_projection(x2d, g_mix[l].reshape(1, d), w_cat, aux,
                                n_plain_a=4 * d // PROJ_TN, n_norm=2 * d // PROJ_TN,
                                n_plain_b=d // PROJ_TN)

        head_pad = jnp.zeros((LANES - 2 * n_heads,), F32)
        alog = jnp.concatenate([jnp.zeros((n_heads,), F32), a_log[l], head_pad])
        dtb = jnp.concatenate([jnp.zeros((n_heads,), F32), dt_bias[l], head_pad])
        rows_t = 2 * n_heads
        small, small_t = _small_projection(
            x2d, g_mix[l].reshape(1, d), w_small.astype(BF16), w_small[:, :rows_t].T.astype(BF16),
            alog.reshape(1, LANES), dtb.reshape(1, LANES),
            alog[:rows_t].reshape(rows_t, 1), dtb[:rows_t].reshape(rows_t, 1), n_heads)

        oa = _gated_delta(big, small, small_t, conv_w[l], g_delta_out[l].reshape(1, HEAD_DIM),
                          bsz, seq, n_heads, d)

        lam_init = 0.8 - 0.6 * math.exp(-0.3 * l)
        lam_params = jnp.stack([lambda_q1[l], lambda_k1[l], lambda_q2[l], lambda_k2[l]])
        od = _diff_attention(big, rel_bias, lam_params, g_subln[l].reshape(1, HEAD_DIM),
                             bsz, seq, n_heads, d, lam_init)

        x1, h2, topi, topw, rank, counts = _mix_project_route(
            big, oa, od, x2d, w_o[l].astype(BF16), g_ffn[l].reshape(1, d),
            w_router[l].T, b_router[l].reshape(N_EXPERTS, 1), d)

        w_up_l = jnp.concatenate([w_up[l][:, :, 0::2], w_up[l][:, :, 1::2]], axis=-1).astype(BF16)
        b_up_l = jnp.concatenate([b_up[l][:, 0::2], b_up[l][:, 1::2]], axis=-1).reshape(N_EXPERTS, 1, 2 * d_ff)
        x2d = _moe(x1, h2, topi, topw, rank, counts, w_up_l, b_up_l,
                   w_down[l].astype(BF16), b_down[l].reshape(N_EXPERTS, 1, d))
    return x2d.reshape(bsz, seq, d)
```

```python
import functools
import math

import jax
import jax.numpy as jnp
from jax import lax
from jax.experimental import pallas as pl
from jax.experimental.pallas import tpu as pltpu

F32 = jnp.float32
BF16 = jnp.bfloat16

HEAD_DIM = 128
DH_DIFF = HEAD_DIM // 2
CONV_WIDTH = 4
CHUNK = 64
N_BUCKETS = 32
MAX_DISTANCE = 128
N_EXPERTS = 32
TOP_K = 4
TOP_K_SHIFT = 2
SWIGLU_LIMIT = 7.0
SWIGLU_ALPHA = 1.702
EPS = 1e-6
NEG_BIG = -1e30

LANES = 128
SUBLANES = 8
VMEM_LIMIT = 56 * 1024 * 1024

PROJ_TM = 1024
PROJ_TN = 1024
GDN_TB = 256
GDN_HG = 2
ATT_BQ = 256
ATT_BK = 256
MIX_TM = 512
MOE_RB = 256
COMB_TC = 512


def _cparams(sem):
    return pltpu.CompilerParams(dimension_semantics=sem, vmem_limit_bytes=VMEM_LIMIT)


def _bdot(a, b):
    return jnp.dot(a.astype(BF16), b.astype(BF16), preferred_element_type=F32)


def _bdot_nt(a, b):
    return lax.dot_general(a.astype(BF16), b.astype(BF16), (((1,), (1,)), ((), ())),
                           preferred_element_type=F32)


def _bdot_tn(a, b):
    return lax.dot_general(a.astype(BF16), b.astype(BF16), (((0,), (0,)), ((), ())),
                           preferred_element_type=F32)


def _fdot(a, b):
    return jnp.dot(a, b, preferred_element_type=F32, precision=lax.Precision.HIGHEST)


def _proj_kernel(x_ref, g_ref, w_ref, aux_ref, o_ref, h_ref, *, n_plain_a, n_norm, n_plain_b):
    j = pl.program_id(1)

    @pl.when(j == 0)
    def _():
        x = x_ref[...]
        ms = jnp.mean(x * x, axis=-1, keepdims=True)
        h_ref[...] = (x * lax.rsqrt(ms + EPS) * g_ref[...]).astype(BF16)

    acc = jnp.dot(h_ref[...], w_ref[...], preferred_element_type=F32)
    first_norm = n_plain_a
    first_plain_b = n_plain_a + n_norm
    first_gate = first_plain_b + n_plain_b

    @pl.when(jnp.logical_or(j < first_norm, jnp.logical_and(j >= first_plain_b, j < first_gate)))
    def _():
        o_ref[...] = acc.astype(o_ref.dtype)

    @pl.when(jnp.logical_and(j >= first_norm, j < first_plain_b))
    def _():
        lane = lax.broadcasted_iota(jnp.int32, (1, LANES), 1)
        lo = lane < DH_DIFF
        for c in range(PROJ_TN // LANES):
            sl = slice(c * LANES, (c + 1) * LANES)
            y = acc[:, sl]
            y2 = y * y
            s_lo = jnp.sum(jnp.where(lo, y2, 0.0), axis=-1, keepdims=True)
            s_hi = jnp.sum(jnp.where(lo, 0.0, y2), axis=-1, keepdims=True)
            r = jnp.where(lo, lax.rsqrt(s_lo / DH_DIFF + EPS), lax.rsqrt(s_hi / DH_DIFF + EPS))
            o_ref[:, sl] = (y * r * aux_ref[:, sl]).astype(o_ref.dtype)

    @pl.when(j >= first_gate)
    def _():
        o_ref[...] = jax.nn.sigmoid(acc + aux_ref[...]).astype(o_ref.dtype)


def _input_projection(x2d, g_mix, w_cat, aux, *, n_plain_a, n_norm, n_plain_b):
    t, d = x2d.shape
    n = w_cat.shape[1]
    kern = functools.partial(_proj_kernel, n_plain_a=n_plain_a, n_norm=n_norm, n_plain_b=n_plain_b)
    return pl.pallas_call(
        kern,
        grid=(t // PROJ_TM, n // PROJ_TN),
        in_specs=[
            pl.BlockSpec((PROJ_TM, d), lambda i, j: (i, 0)),
            pl.BlockSpec((1, d), lambda i, j: (0, 0)),
            pl.BlockSpec((d, PROJ_TN), lambda i, j: (0, j)),
            pl.BlockSpec((1, PROJ_TN), lambda i, j: (0, j)),
        ],
        out_specs=pl.BlockSpec((PROJ_TM, PROJ_TN), lambda i, j: (i, j)),
        out_shape=jax.ShapeDtypeStruct((t, n), BF16),
        scratch_shapes=[pltpu.VMEM((PROJ_TM, d), BF16)],
        compiler_params=_cparams(("parallel", "arbitrary")),
        name="input_projection",
    )(x2d, g_mix, w_cat, aux)


def _small_proj_kernel(x_ref, g_ref, w_ref, wt_ref, alog_ref, dtb_ref, alog_t_ref, dtb_t_ref,
                       o_ref, ot_ref, *, n_heads):
    x = x_ref[...]
    ms = jnp.mean(x * x, axis=-1, keepdims=True)
    h = (x * lax.rsqrt(ms + EPS) * g_ref[...]).astype(BF16)

    def finish(acc, idx, alog, dtb):
        beta = jax.nn.sigmoid(acc)
        z = acc + dtb
        softplus = jnp.maximum(z, 0.0) + jnp.log1p(jnp.exp(-jnp.abs(z)))
        gdec = -jnp.exp(alog) * softplus
        return jnp.where(idx < n_heads, beta, jnp.where(idx < 2 * n_heads, gdec, 0.0))

    acc = jnp.dot(h, w_ref[...], preferred_element_type=F32)
    lane = lax.broadcasted_iota(jnp.int32, acc.shape, 1)
    o_ref[...] = finish(acc, lane, alog_ref[...], dtb_ref[...])
    acc_t = lax.dot_general(wt_ref[...], h, (((1,), (1,)), ((), ())),
                            preferred_element_type=F32)
    sub = lax.broadcasted_iota(jnp.int32, acc_t.shape, 0)
    ot_ref[...] = finish(acc_t, sub, alog_t_ref[...], dtb_t_ref[...])


def _small_projection(x2d, g_mix, w_small, w_small_t, alog, dtb, alog_t, dtb_t, n_heads):
    t, d = x2d.shape
    rows_t = w_small_t.shape[0]
    tm = PROJ_TM
    full = lambda shape: pl.BlockSpec(shape, lambda i: (0, 0))
    return pl.pallas_call(
        functools.partial(_small_proj_kernel, n_heads=n_heads),
        grid=(t // tm,),
        in_specs=[
            pl.BlockSpec((tm, d), lambda i: (i, 0)),
            full((1, d)), full((d, LANES)), full((rows_t, d)),
            full((1, LANES)), full((1, LANES)), full((rows_t, 1)), full((rows_t, 1)),
        ],
        out_specs=[pl.BlockSpec((tm, LANES), lambda i: (i, 0)),
                   pl.BlockSpec((rows_t, tm), lambda i: (0, i))],
        out_shape=[jax.ShapeDtypeStruct((t, LANES), F32),
                   jax.ShapeDtypeStruct((rows_t, t), F32)],
        compiler_params=_cparams(("parallel",)),
        name="beta_decay_projection",
    )(x2d, g_mix, w_small, w_small_t, alog, dtb, alog_t, dtb_t)


def _gdn_kernel(q_ref, k_ref, v_ref, z_ref, sm_ref, smt_ref, cwq_ref, cwk_ref, cwv_ref, gout_ref,
                o_ref, state_ref, qp_ref, kp_ref, vp_ref, vn_ref, *, n_heads):
    hg = pl.program_id(1)
    s = pl.program_id(2)
    tb = GDN_TB
    pad = SUBLANES
    width = GDN_HG * HEAD_DIM

    @pl.when(s == 0)
    def _():
        state_ref[...] = jnp.zeros_like(state_ref)
        for p_ref in (qp_ref, kp_ref, vp_ref):
            p_ref[0:pad, :] = jnp.zeros((pad, width), F32)

    def conv_silu(x_ref, p_ref, cw_ref):
        p_ref[pad:pad + tb, :] = x_ref[...].astype(F32)
        acc = cw_ref[CONV_WIDTH - 1:CONV_WIDTH, :] * p_ref[pad:pad + tb, :]
        for jj in range(CONV_WIDTH - 1):
            off = pad - (CONV_WIDTH - 1) + jj
            acc = acc + cw_ref[jj:jj + 1, :] * p_ref[off:off + tb, :]
        p_ref[0:pad, :] = p_ref[tb:tb + pad, :]
        return acc * jax.nn.sigmoid(acc)

    q_all = conv_silu(q_ref, qp_ref, cwq_ref)
    k_all = conv_silu(k_ref, kp_ref, cwk_ref)
    v_all = conv_silu(v_ref, vp_ref, cwv_ref)

    r = lax.broadcasted_iota(jnp.int32, (tb, tb), 0)
    c = lax.broadcasted_iota(jnp.int32, (tb, tb), 1)
    shift = int(math.log2(CHUNK))
    same = (r >> shift) == (c >> shift)
    incl = jnp.logical_and(same, c <= r)
    strict = jnp.logical_and(same, c < r)
    eye = (r == c).astype(F32)

    small = sm_ref[...]
    small_t = smt_ref[...]
    lane = lax.broadcasted_iota(jnp.int32, small.shape, 1)
    sub = lax.broadcasted_iota(jnp.int32, small_t.shape, 0)
    gcum = _fdot(incl.astype(F32), small)
    gtot = _fdot(same.astype(F32), small)
    gcum_t = _fdot(small_t, jnp.logical_and(same, r <= c).astype(F32))

    for hh in range(GDN_HG):
        head = hg * GDN_HG + hh
        hs = slice(hh * HEAD_DIM, (hh + 1) * HEAD_DIM)
        q = q_all[:, hs]
        k = k_all[:, hs]
        v = v_all[:, hs]
        q = q * lax.rsqrt(jnp.sum(q * q, axis=-1, keepdims=True) + EPS) * (HEAD_DIM ** -0.5)
        k = k * lax.rsqrt(jnp.sum(k * k, axis=-1, keepdims=True) + EPS)

        beta = jnp.sum(jnp.where(lane == head, small, 0.0), axis=-1, keepdims=True)
        gsel = lane == head + n_heads
        gc = jnp.sum(jnp.where(gsel, gcum, 0.0), axis=-1, keepdims=True)
        gl = jnp.sum(jnp.where(gsel, gtot, 0.0), axis=-1, keepdims=True)
        gc_row = jnp.sum(jnp.where(sub == head + n_heads, gcum_t, 0.0), axis=0, keepdims=True)

        decay = jnp.where(incl, jnp.exp(jnp.minimum(gc - gc_row, 0.0)), 0.0)
        kb = k * beta
        a_neg = jnp.where(strict, -(_bdot_nt(kb, k) * decay), 0.0)
        tmat = eye + a_neg
        pw = a_neg
        for _ in range(int(math.log2(CHUNK)) - 1):
            pw = _bdot(pw, pw)
            tmat = tmat + _bdot(tmat, pw)
        egc = jnp.exp(gc)
        uw = _bdot(tmat, jnp.concatenate([v * beta, kb * egc], axis=1))
        u = uw[:, :HEAD_DIM]
        w = uw[:, HEAD_DIM:]
        qk = jnp.where(incl, _bdot_nt(q, k) * decay, 0.0)
        q_dec = q * egc
        k_end = k * jnp.exp(gl - gc)

        vn_ref[hh] = jnp.zeros((tb, HEAD_DIM), F32)
        outs = []
        for ci in range(tb // CHUNK):
            cs = slice(ci * CHUNK, (ci + 1) * CHUNK)
            st = state_ref[hh]
            ws_qs = _bdot(jnp.concatenate([w[cs], q_dec[cs]], axis=0), st)
            v_new = u[cs] - ws_qs[:CHUNK]
            vn_ref[hh, cs, :] = v_new
            outs.append(ws_qs[CHUNK:] + _bdot(qk[cs], vn_ref[hh]))
            g_last = gl[ci * CHUNK:ci * CHUNK + 1, :]
            state_ref[hh] = st * jnp.exp(g_last) + _bdot_tn(k_end[cs], v_new)
        o = jnp.concatenate(outs, axis=0)
        o = o * lax.rsqrt(jnp.mean(o * o, axis=-1, keepdims=True) + EPS) * gout_ref[...]
        zz = z_ref[:, hs].astype(F32)
        o_ref[:, hs] = (o * (zz * jax.nn.sigmoid(zz))).astype(o_ref.dtype)


def _gated_delta(big, small, small_t, conv_w, g_out, bsz, seq, n_heads, d_model):
    t = bsz * seq
    tb = GDN_TB
    ns = seq // tb
    width = GDN_HG * HEAD_DIM
    nhg = n_heads // GDN_HG
    blocks_per_group = d_model // width
    rows_t = small_t.shape[0]

    def colspec(group):
        return pl.BlockSpec((tb, width), lambda b, h, s: (b * ns + s, group * blocks_per_group + h))

    def cwspec(group):
        return pl.BlockSpec((CONV_WIDTH, width), lambda b, h, s: (0, group * blocks_per_group + h))

    return pl.pallas_call(
        functools.partial(_gdn_kernel, n_heads=n_heads),
        grid=(bsz, nhg, ns),
        in_specs=[
            colspec(0), colspec(1), colspec(2), colspec(3),
            pl.BlockSpec((tb, LANES), lambda b, h, s: (b * ns + s, 0)),
            pl.BlockSpec((rows_t, tb), lambda b, h, s: (0, b * ns + s)),
            cwspec(0), cwspec(1), cwspec(2),
            pl.BlockSpec((1, HEAD_DIM), lambda b, h, s: (0, 0)),
        ],
        out_specs=pl.BlockSpec((tb, width), lambda b, h, s: (b * ns + s, h)),
        out_shape=jax.ShapeDtypeStruct((t, d_model), BF16),
        scratch_shapes=[
            pltpu.VMEM((GDN_HG, HEAD_DIM, HEAD_DIM), F32),
            pltpu.VMEM((tb + SUBLANES, width), F32),
            pltpu.VMEM((tb + SUBLANES, width), F32),
            pltpu.VMEM((tb + SUBLANES, width), F32),
            pltpu.VMEM((GDN_HG, tb, HEAD_DIM), F32),
        ],
        compiler_params=_cparams(("parallel", "parallel", "arbitrary")),
        name="gated_delta",
    )(big, big, big, big, small, small_t, conv_w, conv_w, conv_w, g_out)


def _t5_bucket(n):
    max_exact = N_BUCKETS // 2
    nf = jnp.maximum(n, 1).astype(F32)
    large = max_exact + (jnp.log(nf / max_exact) / math.log(MAX_DISTANCE / max_exact)
                         * (N_BUCKETS - max_exact)).astype(jnp.int32)
    large = jnp.minimum(large, N_BUCKETS - 1)
    return jnp.where(n < max_exact, n, large)


def _attn_kernel(rb_ref, q_ref, k_ref, v_ref, lam_ref, gsub_ref, o_ref,
                 bias_ref, m_ref, l_ref, acc_ref, *, lam_init):
    h = pl.program_id(0)
    b = pl.program_id(1)
    qi = pl.program_id(2)
    bq, bk = ATT_BQ, ATT_BK

    @pl.when(jnp.logical_and(b == 0, qi == 0))
    def _():
        i = lax.broadcasted_iota(jnp.int32, (bq, bk), 0)
        jj = lax.broadcasted_iota(jnp.int32, (bq, bk), 1)
        far = rb_ref[N_BUCKETS - 1, h]
        for slot in range(2):
            n = i - jj + slot * bk
            bucket = _t5_bucket(jnp.maximum(n, 0))
            bias = jnp.zeros((bq, bk), F32)
            for cc in range(N_BUCKETS):
                bias = jnp.where(bucket == cc, rb_ref[cc, h] - far, bias)
            if slot == 0:
                bias = jnp.where(n >= 0, bias, NEG_BIG)
            bias_ref[slot] = bias

    m_ref[...] = jnp.full(m_ref.shape, NEG_BIG, F32)
    l_ref[...] = jnp.zeros(l_ref.shape, F32)
    acc_ref[...] = jnp.zeros(acc_ref.shape, F32)

    q = q_ref[...]
    lane = lax.broadcasted_iota(jnp.int32, q.shape, 1)
    zero = jnp.zeros_like(q)
    q_maps = (jnp.where(lane < DH_DIFF, q, zero), jnp.where(lane < DH_DIFF, zero, q))

    def tile(j, bias):
        ks = pl.multiple_of(j * bk, bk)
        k = k_ref[pl.ds(ks, bk), :]
        v = v_ref[pl.ds(ks, bk), :]
        for mp in range(2):
            sc = lax.dot_general(q_maps[mp], k, (((1,), (1,)), ((), ())), preferred_element_type=F32)
            if bias is not None:
                sc = sc + bias
            m_old = m_ref[mp]
            m_new = jnp.maximum(m_old, jnp.max(sc, axis=-1, keepdims=True))
            p = jnp.exp(sc - m_new)
            alpha = jnp.exp(m_old - m_new)
            l_ref[mp] = alpha * l_ref[mp] + jnp.sum(p, axis=-1, keepdims=True)
            acc_ref[mp] = alpha * acc_ref[mp] + jnp.dot(p.astype(BF16), v, preferred_element_type=F32)
            m_ref[mp] = m_new

    def far_body(j, carry):
        tile(j, None)
        return carry

    lax.fori_loop(0, jnp.maximum(qi - 1, 0), far_body, 0)

    @pl.when(qi >= 1)
    def _():
        tile(qi - 1, bias_ref[1])

    tile(qi, bias_ref[0])

    lam_p = lam_ref[...]
    s1 = jnp.sum(lam_p[0:1] * lam_p[1:2], axis=-1, keepdims=True)
    s2 = jnp.sum(lam_p[2:3] * lam_p[3:4], axis=-1, keepdims=True)
    lam = jnp.exp(s1) - jnp.exp(s2) + lam_init
    o = acc_ref[0] / l_ref[0] - lam * (acc_ref[1] / l_ref[1])
    o = o * lax.rsqrt(jnp.mean(o * o, axis=-1, keepdims=True) + EPS) * gsub_ref[...]
    o_ref[...] = (o * (1.0 - lam_init)).astype(o_ref.dtype)


def _diff_attention(big, rel_bias, lam_params, g_subln, bsz, seq, n_heads, d_model, lam_init):
    t = bsz * seq
    nq = seq // ATT_BQ
    qcol = 4 * d_model // HEAD_DIM
    per = d_model // HEAD_DIM
    return pl.pallas_call(
        functools.partial(_attn_kernel, lam_init=lam_init),
        grid=(n_heads, bsz, nq),
        in_specs=[
            pl.BlockSpec(memory_space=pltpu.SMEM),
            pl.BlockSpec((ATT_BQ, HEAD_DIM), lambda h, b, i: (b * nq + i, qcol + h)),
            pl.BlockSpec((seq, HEAD_DIM), lambda h, b, i: (b, qcol + per + h)),
            pl.BlockSpec((seq, HEAD_DIM), lambda h, b, i: (b, qcol + 2 * per + h)),
            pl.BlockSpec((4, DH_DIFF), lambda h, b, i: (0, 0)),
            pl.BlockSpec((1, HEAD_DIM), lambda h, b, i: (0, 0)),
        ],
        out_specs=pl.BlockSpec((ATT_BQ, HEAD_DIM), lambda h, b, i: (b * nq + i, h)),
        out_shape=jax.ShapeDtypeStruct((t, d_model), BF16),
        scratch_shapes=[
            pltpu.VMEM((2, ATT_BQ, ATT_BK), F32),
            pltpu.VMEM((2, ATT_BQ, 1), F32),
            pltpu.VMEM((2, ATT_BQ, 1), F32),
            pltpu.VMEM((2, ATT_BQ, HEAD_DIM), F32),
        ],
        compiler_params=_cparams(("arbitrary", "arbitrary", "arbitrary")),
        name="diff_attention",
    )(rel_bias, big, big, big, lam_params, g_subln)


def _mix_kernel(ga_ref, gb_ref, oa_ref, od_ref, x_ref, wo_ref, gffn_ref, wr_ref, br_ref,
                x1_ref, h2_ref, topi_ref, topw_ref, rank_ref, cnt_ref, carry_ref):
    i = pl.program_id(0)
    tm = MIX_TM

    @pl.when(i == 0)
    def _():
        carry_ref[...] = jnp.zeros_like(carry_ref)

    mix = (ga_ref[...].astype(F32) * oa_ref[...].astype(F32)
           + gb_ref[...].astype(F32) * od_ref[...].astype(F32))
    x1 = x_ref[...] + jnp.dot(mix.astype(BF16), wo_ref[...], preferred_element_type=F32)
    x1_ref[...] = x1
    h2 = x1 * lax.rsqrt(jnp.mean(x1 * x1, axis=-1, keepdims=True) + EPS) * gffn_ref[...]
    h2_ref[...] = h2

    logits = lax.dot_general(wr_ref[...], h2, (((1,), (1,)), ((), ())),
                             preferred_element_type=F32, precision=lax.Precision.HIGHEST) + br_ref[...]
    eidx = lax.broadcasted_iota(jnp.int32, logits.shape, 0).astype(F32)
    vals, hots = [], []
    cur = logits
    for kk in range(TOP_K):
        mx = jnp.max(cur, axis=0, keepdims=True)
        idx = jnp.min(jnp.where(cur == mx, eidx, float(N_EXPERTS)), axis=0, keepdims=True)
        hot = eidx == idx
        vals.append(mx)
        hots.append(hot)
        topi_ref[kk:kk + 1, :] = idx.astype(jnp.int32)
        cur = jnp.where(hot, -jnp.inf, cur)
    exps = [jnp.exp(vv - vals[0]) for vv in vals]
    denom = exps[0] + exps[1] + exps[2] + exps[3]
    for kk in range(TOP_K):
        topw_ref[kk:kk + 1, :] = exps[kk] / denom

    sel = hots[0]
    for kk in range(1, TOP_K):
        sel = jnp.logical_or(sel, hots[kk])
    sel_f = sel.astype(F32)
    r = lax.broadcasted_iota(jnp.int32, (tm, tm), 0)
    c = lax.broadcasted_iota(jnp.int32, (tm, tm), 1)
    before = _bdot(sel_f, (r < c).astype(F32)) + carry_ref[...]
    for kk in range(TOP_K):
        rank_ref[kk:kk + 1, :] = jnp.sum(jnp.where(hots[kk], before, 0.0), axis=0,
                                         keepdims=True).astype(jnp.int32)
    carry_ref[...] = carry_ref[...] + jnp.sum(sel_f, axis=-1, keepdims=True)
    cnt_ref[...] = carry_ref[...].astype(jnp.int32)


def _mix_project_route(big, oa, od, x2d, w_o, g_ffn, w_r_t, b_r, d_model):
    t = x2d.shape[0]
    tm = MIX_TM
    gate_blk = 7
    full = lambda shape: pl.BlockSpec(shape, lambda i: (0, 0))
    row = lambda: pl.BlockSpec((tm, d_model), lambda i: (i, 0))
    krow = lambda: pl.BlockSpec((TOP_K, tm), lambda i: (0, i))
    return pl.pallas_call(
        _mix_kernel,
        grid=(t // tm,),
        in_specs=[
            pl.BlockSpec((tm, d_model), lambda i: (i, gate_blk)),
            pl.BlockSpec((tm, d_model), lambda i: (i, gate_blk + 1)),
            row(), row(), row(),
            full((d_model, d_model)), full((1, d_model)), full((N_EXPERTS, d_model)), full((N_EXPERTS, 1)),
        ],
        out_specs=[row(), row(), krow(), krow(), krow(), full((N_EXPERTS, 1))],
        out_shape=[
            jax.ShapeDtypeStruct((t, d_model), F32),
            jax.ShapeDtypeStruct((t, d_model), F32),
            jax.ShapeDtypeStruct((TOP_K, t), jnp.int32),
            jax.ShapeDtypeStruct((TOP_K, t), F32),
            jax.ShapeDtypeStruct((TOP_K, t), jnp.int32),
            jax.ShapeDtypeStruct((N_EXPERTS, 1), jnp.int32),
        ],
        scratch_shapes=[pltpu.VMEM((N_EXPERTS, 1), F32)],
        compiler_params=_cparams(("arbitrary",)),
        name="merge_outproj_route",
    )(big, big, oa, od, x2d, w_o, g_ffn, w_r_t, b_r)


def _row_copy(src_ref, src_row, dst_ref, dst_row, sem):
    return pltpu.make_async_copy(src_ref.at[pl.ds(src_row, 1)], dst_ref.at[pl.ds(dst_row, 1)], sem)


def _expert_kernel(be_ref, nu_ref, src_ref, dst_ref, src_next_ref, h2_ref, wup_ref, bup_ref, wdn_ref,
                   bdn_ref, y_ref, xbuf, ybuf, wup_bf, wdn_bf, gsem, ssem):
    i = pl.program_id(0)
    n_used = nu_ref[0]
    slot = i & 1
    rb = MOE_RB
    d_ff = wdn_ref.shape[1]

    def gather(idx_ref, s):
        def body(a, carry):
            _row_copy(h2_ref, idx_ref[a], xbuf.at[s], a, gsem.at[s]).start()
            return carry
        lax.fori_loop(0, rb, body, 0, unroll=8)

    @pl.when(i == 0)
    def _():
        gather(src_ref, 0)

    @pl.when(i + 1 < n_used)
    def _():
        gather(src_next_ref, 1 - slot)

    @pl.when(i < n_used)
    def _():
        @pl.when(jnp.logical_or(i == 0, be_ref[i] != be_ref[jnp.maximum(i - 1, 0)]))
        def _():
            rr = lax.broadcasted_iota(jnp.int32, (2 * LANES, 2 * LANES), 0)
            cc = lax.broadcasted_iota(jnp.int32, (2 * LANES, 2 * LANES), 1)
            pick = jnp.where(cc < LANES, 2 * cc, 2 * (cc - LANES) + 1)
            perm = (rr == pick).astype(BF16)
            for g in range(wup_ref.shape[2] // (2 * LANES)):
                cs = slice(g * 2 * LANES, (g + 1) * 2 * LANES)
                wup_bf[:, cs] = jnp.dot(wup_ref[0, :, cs].astype(BF16), perm,
                                        preferred_element_type=F32).astype(BF16)
            wdn_bf[...] = wdn_ref[0].astype(BF16)

        pltpu.make_async_copy(h2_ref.at[pl.ds(0, rb)], xbuf.at[slot], gsem.at[slot]).wait()
        hid = jnp.dot(xbuf[slot].astype(BF16), wup_bf[...], preferred_element_type=F32) + bup_ref[0]
        acts = []
        for g in range(hid.shape[1] // (2 * LANES)):
            glu = jnp.minimum(hid[:, g * 2 * LANES:g * 2 * LANES + LANES], SWIGLU_LIMIT)
            lin = jnp.clip(hid[:, g * 2 * LANES + LANES:(g + 1) * 2 * LANES], -SWIGLU_LIMIT, SWIGLU_LIMIT)
            acts.append(glu * jax.nn.sigmoid(SWIGLU_ALPHA * glu) * (lin + 1.0))
        act = jnp.concatenate(acts, axis=1)
        assert act.shape[1] == d_ff
        ybuf[slot] = jnp.dot(act.astype(BF16), wdn_bf[...], preferred_element_type=F32) + bdn_ref[0]

        @pl.when(i >= 1)
        def _():
            pltpu.make_async_copy(ybuf.at[1 - slot], y_ref.at[pl.ds(0, rb)], ssem.at[1 - slot]).wait()

        def scatter(a, carry):
            _row_copy(ybuf.at[slot], a, y_ref, dst_ref[a], ssem.at[slot]).start()
            return carry
        lax.fori_loop(0, rb, scatter, 0, unroll=8)

        @pl.when(i == n_used - 1)
        def _():
            pltpu.make_async_copy(ybuf.at[slot], y_ref.at[pl.ds(0, rb)], ssem.at[slot]).wait()

    @pl.when(i >= n_used)
    def _():
        ybuf[slot] = jnp.zeros((rb, ybuf.shape[2]), F32)
        first = pl.multiple_of(dst_ref[0], rb)
        fill = pltpu.make_async_copy(ybuf.at[slot], y_ref.at[pl.ds(first, rb)], ssem.at[slot])
        fill.start()
        fill.wait()


def _experts(block_e, n_used, src_tok, dst_row, h2, w_up, b_up, w_down, b_down):
    n_rows = src_tok.shape[0]
    d = h2.shape[1]
    nb = n_rows // MOE_RB
    two_ff = w_up.shape[2]
    d_ff = w_down.shape[1]
    smem_rows = lambda fn: pl.BlockSpec((MOE_RB,), fn, memory_space=pltpu.SMEM)
    grid_spec = pltpu.PrefetchScalarGridSpec(
        num_scalar_prefetch=2,
        grid=(nb,),
        in_specs=[
            smem_rows(lambda i, be, nu: (i,)),
            smem_rows(lambda i, be, nu: (i,)),
            smem_rows(lambda i, be, nu: (jnp.minimum(i + 1, nb - 1),)),
            pl.BlockSpec(memory_space=pl.ANY),
            pl.BlockSpec((1, d, two_ff), lambda i, be, nu: (be[i], 0, 0)),
            pl.BlockSpec((1, 1, two_ff), lambda i, be, nu: (be[i], 0, 0)),
            pl.BlockSpec((1, d_ff, d), lambda i, be, nu: (be[i], 0, 0)),
            pl.BlockSpec((1, 1, d), lambda i, be, nu: (be[i], 0, 0)),
        ],
        out_specs=pl.BlockSpec(memory_space=pl.ANY),
        scratch_shapes=[
            pltpu.VMEM((2, MOE_RB, d), F32),
            pltpu.VMEM((2, MOE_RB, d), F32),
            pltpu.VMEM((d, two_ff), BF16),
            pltpu.VMEM((d_ff, d), BF16),
            pltpu.SemaphoreType.DMA((2,)),
            pltpu.SemaphoreType.DMA((2,)),
        ],
    )
    return pl.pallas_call(
        _expert_kernel,
        grid_spec=grid_spec,
        out_shape=jax.ShapeDtypeStruct((n_rows, d), F32),
        compiler_params=_cparams(("arbitrary",)),
        name="moe_experts",
    )(block_e, n_used, src_tok, dst_row, src_tok, h2, w_up, b_up, w_down, b_down)


def _combine_kernel(x1_ref, w_ref, y0_ref, y1_ref, y2_ref, y3_ref, o_ref):
    w = w_ref[...]
    out = x1_ref[...]
    for kk, y_ref in enumerate((y0_ref, y1_ref, y2_ref, y3_ref)):
        out = out + w[:, kk:kk + 1] * y_ref[...]
    o_ref[...] = out


def _combine(x1, w_tok, y_slots):
    t, d = x1.shape
    tc = COMB_TC
    nt = t // tc
    yspec = lambda kk: pl.BlockSpec((tc, d), lambda i: (kk * nt + i, 0))
    return pl.pallas_call(
        _combine_kernel,
        grid=(nt,),
        in_specs=[
            pl.BlockSpec((tc, d), lambda i: (i, 0)),
            pl.BlockSpec((tc, TOP_K), lambda i: (i, 0)),
            yspec(0), yspec(1), yspec(2), yspec(3),
        ],
        out_specs=pl.BlockSpec((tc, d), lambda i: (i, 0)),
        out_shape=jax.ShapeDtypeStruct((t, d), F32),
        compiler_params=_cparams(("parallel",)),
        name="moe_combine",
    )(x1, w_tok, y_slots, y_slots, y_slots, y_slots)


def _moe(x1, h2, topi, topw, rank, counts, w_up, b_up, w_down, b_down):
    t, d = x1.shape
    n_assign = t * TOP_K
    nb = -(-n_assign // MOE_RB) + N_EXPERTS
    n_rows = nb * MOE_RB
    counts = counts[:, 0]
    padded = (counts + MOE_RB - 1) // MOE_RB * MOE_RB
    padded_end = jnp.cumsum(padded)
    padded_start = padded_end - padded
    expert_ids = jnp.arange(N_EXPERTS, dtype=jnp.int32)[:, None, None]
    start_of = jnp.sum(jnp.where(topi[None] == expert_ids, padded_start[:, None, None], 0), axis=0)
    dest = (start_of + rank).astype(jnp.int32)
    n_used = (padded_end[-1] // MOE_RB).astype(jnp.int32)
    blk = jnp.minimum(jnp.arange(nb, dtype=jnp.int32), n_used - 1)
    block_e = jnp.minimum(jnp.sum(padded_end[None, :] <= (blk * MOE_RB)[:, None], axis=1),
                          N_EXPERTS - 1).astype(jnp.int32)
    slot_of = jnp.full((n_rows,), -1, jnp.int32).at[dest.reshape(-1)].set(
        jnp.arange(n_assign, dtype=jnp.int32), unique_indices=True)
    is_pad = slot_of < 0
    src_tok = jnp.where(is_pad, 0, slot_of % t)
    dst_row = jnp.where(is_pad, n_assign + jnp.cumsum(is_pad.astype(jnp.int32)) - 1, slot_of)

    y_slots = _experts(block_e, n_used.reshape(1), src_tok, dst_row, h2, w_up, b_up, w_down, b_down)
    return _combine(x1, topw.T, y_slots)


def kernel(x, g_mix, w_in, b_gate, conv_w, a_log, dt_bias, g_delta_out, q_norm, k_norm, lambda_q1, lambda_k1, lambda_q2, lambda_k2, g_subln, rel_bias, w_o, g_ffn, w_router, b_router, w_up, b_up, w_down, b_down):
    bsz, seq, d = x.shape
    depth = g_mix.shape[0]
    n_heads = d // HEAD_DIM
    t = bsz * seq
    d_ff = w_down.shape[2]
    assert d % PROJ_TN == 0 and t % PROJ_TM == 0 and seq % GDN_TB == 0 and seq % ATT_BQ == 0
    assert t % MIX_TM == 0 and t % COMB_TC == 0 and n_heads % GDN_HG == 0
    assert (t * TOP_K) % MOE_RB == 0
    assert 2 * n_heads <= 2 * SUBLANES

    x2d = x.reshape(t, d)
    for l in range(depth):
        wl = w_in[l]
        c0 = 4 * d
        c1 = c0 + 2 * n_heads
        w_cat = jnp.concatenate([wl[:, :c0], wl[:, c1:]], axis=1).astype(BF16)
        w_small = jnp.pad(wl[:, c0:c1], ((0, 0), (0, LANES - 2 * n_heads)))
        qk_gain = jnp.concatenate([jnp.tile(q_norm[l] * (DH_DIFF ** -0.5), 2 * n_heads),
                                   jnp.tile(k_norm[l], 2 * n_heads)])
        aux = jnp.concatenate([jnp.zeros((4 * d,), F32), qk_gain, jnp.zeros((d,), F32),
                               b_gate[l].reshape(-1)]).reshape(1, -1)
        big = _input_projection(x2d, g_mix[l].reshape(1, d), w_cat, aux,
                                n_plain_a=4 * d // PROJ_TN, n_norm=2 * d // PROJ_TN,
                                n_plain_b=d // PROJ_TN)

        head_pad = jnp.zeros((LANES - 2 * n_heads,), F32)
        alog = jnp.concatenate([jnp.zeros((n_heads,), F32), a_log[l], head_pad])
        dtb = jnp.concatenate([jnp.zeros((n_heads,), F32), dt_bias[l], head_pad])
        rows_t = 2 * n_heads
        small, small_t = _small_projection(
            x2d, g_mix[l].reshape(1, d), w_small.astype(BF16), w_small[:, :rows_t].T.astype(BF16),
            alog.reshape(1, LANES), dtb.reshape(1, LANES),
            alog[:rows_t].reshape(rows_t, 1), dtb[:rows_t].reshape(rows_t, 1), n_heads)

        oa = _gated_delta(big, small, small_t, conv_w[l], g_delta_out[l].reshape(1, HEAD_DIM),
                          bsz, seq, n_heads, d)

        lam_init = 0.8 - 0.6 * math.exp(-0.3 * l)
        lam_params = jnp.stack([lambda_q1[l], lambda_k1[l], lambda_q2[l], lambda_k2[l]])
        od = _diff_attention(big, rel_bias, lam_params, g_subln[l].reshape(1, HEAD_DIM),
                             bsz, seq, n_heads, d, lam_init)

        x1, h2, topi, topw, rank, counts = _mix_project_route(
            big, oa, od, x2d, w_o[l].astype(BF16), g_ffn[l].reshape(1, d),
            w_router[l].T, b_router[l].reshape(N_EXPERTS, 1), d)

        b_up_l = b_up[l].reshape(N_EXPERTS, 2 * d_ff // (2 * LANES), LANES, 2)
        b_up_l = jnp.swapaxes(b_up_l, 2, 3).reshape(N_EXPERTS, 1, 2 * d_ff)
        x2d = _moe(x1, h2, topi, topw, rank, counts, w_up[l], b_up_l,
                   w_down[l], b_down[l].reshape(N_EXPERTS, 1, d))
    return x2d.reshape(bsz, seq, d)
```

```python
import functools
import math

import jax
import jax.numpy as jnp
from jax import lax
from jax.experimental import pallas as pl
from jax.experimental.pallas import tpu as pltpu

F32 = jnp.float32
BF16 = jnp.bfloat16

HEAD_DIM = 128
DH_DIFF = HEAD_DIM // 2
CONV_WIDTH = 4
CHUNK = 64
N_BUCKETS = 32
MAX_DISTANCE = 128
N_EXPERTS = 32
TOP_K = 4
TOP_K_SHIFT = 2
SWIGLU_LIMIT = 7.0
SWIGLU_ALPHA = 1.702
EPS = 1e-6
NEG_BIG = -1e30

LANES = 128
SUBLANES = 8
VMEM_LIMIT = 56 * 1024 * 1024

PROJ_TM = 1024
PROJ_TN = 1024
GDN_TB = 256
GDN_HG = 4
ATT_BQ = 512
ATT_BK = 512
MIX_TM = 512
MOE_RB = 256
COMB_TC = 512


def _cparams(sem):
    return pltpu.CompilerParams(dimension_semantics=sem, vmem_limit_bytes=VMEM_LIMIT)


def _bdot(a, b):
    return jnp.dot(a.astype(BF16), b.astype(BF16), preferred_element_type=F32)


def _bdot_nt(a, b):
    return lax.dot_general(a.astype(BF16), b.astype(BF16), (((1,), (1,)), ((), ())),
                           preferred_element_type=F32)


def _bdot_tn(a, b):
    return lax.dot_general(a.astype(BF16), b.astype(BF16), (((0,), (0,)), ((), ())),
                           preferred_element_type=F32)


def _fdot(a, b):
    return jnp.dot(a, b, preferred_element_type=F32, precision=lax.Precision.HIGHEST)


def _proj_kernel(x_ref, g_ref, w_ref, aux_ref, o_ref, h_ref, *, n_plain_a, n_norm, n_plain_b):
    j = pl.program_id(1)

    @pl.when(j == 0)
    def _():
        x = x_ref[...]
        ms = jnp.mean(x * x, axis=-1, keepdims=True)
        h_ref[...] = (x * lax.rsqrt(ms + EPS) * g_ref[...]).astype(BF16)

    acc = jnp.dot(h_ref[...], w_ref[...], preferred_element_type=F32)
    first_norm = n_plain_a
    first_plain_b = n_plain_a + n_norm
    first_gate = first_plain_b + n_plain_b

    @pl.when(jnp.logical_or(j < first_norm, jnp.logical_and(j >= first_plain_b, j < first_gate)))
    def _():
        o_ref[...] = acc.astype(o_ref.dtype)

    @pl.when(jnp.logical_and(j >= first_norm, j < first_plain_b))
    def _():
        lane = lax.broadcasted_iota(jnp.int32, (1, LANES), 1)
        lo = lane < DH_DIFF
        for c in range(PROJ_TN // LANES):
            sl = slice(c * LANES, (c + 1) * LANES)
            y = acc[:, sl]
            y2 = y * y
            s_lo = jnp.sum(jnp.where(lo, y2, 0.0), axis=-1, keepdims=True)
            s_hi = jnp.sum(jnp.where(lo, 0.0, y2), axis=-1, keepdims=True)
            r = jnp.where(lo, lax.rsqrt(s_lo / DH_DIFF + EPS), lax.rsqrt(s_hi / DH_DIFF + EPS))
            o_ref[:, sl] = (y * r * aux_ref[:, sl]).astype(o_ref.dtype)

    @pl.when(j >= first_gate)
    def _():
        o_ref[...] = jax.nn.sigmoid(acc + aux_ref[...]).astype(o_ref.dtype)


def _input_projection(x2d, g_mix, w_cat, aux, *, n_plain_a, n_norm, n_plain_b):
    t, d = x2d.shape
    n = w_cat.shape[1]
    kern = functools.partial(_proj_kernel, n_plain_a=n_plain_a, n_norm=n_norm, n_plain_b=n_plain_b)
    return pl.pallas_call(
        kern,
        grid=(t // PROJ_TM, n // PROJ_TN),
        in_specs=[
            pl.BlockSpec((PROJ_TM, d), lambda i, j: (i, 0)),
            pl.BlockSpec((1, d), lambda i, j: (0, 0)),
            pl.BlockSpec((d, PROJ_TN), lambda i, j: (0, j)),
            pl.BlockSpec((1, PROJ_TN), lambda i, j: (0, j)),
        ],
        out_specs=pl.BlockSpec((PROJ_TM, PROJ_TN), lambda i, j: (i, j)),
        out_shape=jax.ShapeDtypeStruct((t, n), BF16),
        scratch_shapes=[pltpu.VMEM((PROJ_TM, d), BF16)],
        compiler_params=_cparams(("parallel", "arbitrary")),
        name="input_projection",
    )(x2d, g_mix, w_cat, aux)


def _small_proj_kernel(x_ref, g_ref, w_ref, wt_ref, alog_ref, dtb_ref, alog_t_ref, dtb_t_ref,
                       o_ref, ot_ref, *, n_heads):
    x = x_ref[...]
    ms = jnp.mean(x * x, axis=-1, keepdims=True)
    h = (x * lax.rsqrt(ms + EPS) * g_ref[...]).astype(BF16)

    def finish(acc, idx, alog, dtb):
        beta = jax.nn.sigmoid(acc)
        z = acc + dtb
        softplus = jnp.maximum(z, 0.0) + jnp.log1p(jnp.exp(-jnp.abs(z)))
        gdec = -jnp.exp(alog) * softplus
        return jnp.where(idx < n_heads, beta, jnp.where(idx < 2 * n_heads, gdec, 0.0))

    acc = jnp.dot(h, w_ref[...], preferred_element_type=F32)
    lane = lax.broadcasted_iota(jnp.int32, acc.shape, 1)
    o_ref[...] = finish(acc, lane, alog_ref[...], dtb_ref[...])
    acc_t = lax.dot_general(wt_ref[...], h, (((1,), (1,)), ((), ())),
                            preferred_element_type=F32)
    sub = lax.broadcasted_iota(jnp.int32, acc_t.shape, 0)
    ot_ref[...] = finish(acc_t, sub, alog_t_ref[...], dtb_t_ref[...])


def _small_projection(x2d, g_mix, w_small, w_small_t, alog, dtb, alog_t, dtb_t, n_heads):
    t, d = x2d.shape
    rows_t = w_small_t.shape[0]
    tm = PROJ_TM
    full = lambda shape: pl.BlockSpec(shape, lambda i: (0, 0))
    return pl.pallas_call(
        functools.partial(_small_proj_kernel, n_heads=n_heads),
        grid=(t // tm,),
        in_specs=[
            pl.BlockSpec((tm, d), lambda i: (i, 0)),
            full((1, d)), full((d, LANES)), full((rows_t, d)),
            full((1, LANES)), full((1, LANES)), full((rows_t, 1)), full((rows_t, 1)),
        ],
        out_specs=[pl.BlockSpec((tm, LANES), lambda i: (i, 0)),
                   pl.BlockSpec((rows_t, tm), lambda i: (0, i))],
        out_shape=[jax.ShapeDtypeStruct((t, LANES), F32),
                   jax.ShapeDtypeStruct((rows_t, t), F32)],
        compiler_params=_cparams(("parallel",)),
        name="beta_decay_projection",
    )(x2d, g_mix, w_small, w_small_t, alog, dtb, alog_t, dtb_t)


def _gdn_kernel(q_ref, k_ref, v_ref, z_ref, sm_ref, smt_ref, cwq_ref, cwk_ref, cwv_ref, gout_ref,
                o_ref, state_ref, qp_ref, kp_ref, vp_ref, vn_ref, *, n_heads):
    hg = pl.program_id(1)
    s = pl.program_id(2)
    tb = GDN_TB
    pad = SUBLANES
    width = GDN_HG * HEAD_DIM

    @pl.when(s == 0)
    def _():
        state_ref[...] = jnp.zeros_like(state_ref)
        for p_ref in (qp_ref, kp_ref, vp_ref):
            p_ref[0:pad, :] = jnp.zeros((pad, width), F32)

    def conv_silu(x_ref, p_ref, cw_ref):
        p_ref[pad:pad + tb, :] = x_ref[...].astype(F32)
        acc = cw_ref[CONV_WIDTH - 1:CONV_WIDTH, :] * p_ref[pad:pad + tb, :]
        for jj in range(CONV_WIDTH - 1):
            off = pad - (CONV_WIDTH - 1) + jj
            acc = acc + cw_ref[jj:jj + 1, :] * p_ref[off:off + tb, :]
        p_ref[0:pad, :] = p_ref[tb:tb + pad, :]
        return acc * jax.nn.sigmoid(acc)

    q_all = conv_silu(q_ref, qp_ref, cwq_ref)
    k_all = conv_silu(k_ref, kp_ref, cwk_ref)
    v_all = conv_silu(v_ref, vp_ref, cwv_ref)

    r = lax.broadcasted_iota(jnp.int32, (tb, tb), 0)
    c = lax.broadcasted_iota(jnp.int32, (tb, tb), 1)
    shift = int(math.log2(CHUNK))
    same = (r >> shift) == (c >> shift)
    incl = jnp.logical_and(same, c <= r)
    strict = jnp.logical_and(same, c < r)
    eye = (r == c).astype(F32)

    small = sm_ref[...]
    small_t = smt_ref[...]
    lane = lax.broadcasted_iota(jnp.int32, small.shape, 1)
    sub = lax.broadcasted_iota(jnp.int32, small_t.shape, 0)
    gcum = _fdot(incl.astype(F32), small)
    gtot = _fdot(same.astype(F32), small)
    gcum_t = _fdot(small_t, jnp.logical_and(same, r <= c).astype(F32))

    heads = range(GDN_HG)
    hsl = [slice(hh * HEAD_DIM, (hh + 1) * HEAD_DIM) for hh in heads]
    qs = [q_all[:, hs] for hs in hsl]
    ks = [k_all[:, hs] for hs in hsl]
    vs = [v_all[:, hs] for hs in hsl]
    qs = [q * lax.rsqrt(jnp.sum(q * q, axis=-1, keepdims=True) + EPS) * (HEAD_DIM ** -0.5) for q in qs]
    ks = [k * lax.rsqrt(jnp.sum(k * k, axis=-1, keepdims=True) + EPS) for k in ks]

    def col_of(arr, idx):
        return jnp.sum(jnp.where(lane == idx, arr, 0.0), axis=-1, keepdims=True)

    head_ids = [hg * GDN_HG + hh for hh in heads]
    betas = [col_of(small, hd) for hd in head_ids]
    gcs = [col_of(gcum, hd + n_heads) for hd in head_ids]
    gls = [col_of(gtot, hd + n_heads) for hd in head_ids]
    gc_rows = [jnp.sum(jnp.where(sub == hd + n_heads, gcum_t, 0.0), axis=0, keepdims=True)
               for hd in head_ids]

    decays = [jnp.where(incl, jnp.exp(jnp.minimum(gc - gr, 0.0)), 0.0) for gc, gr in zip(gcs, gc_rows)]
    kbs = [k * b for k, b in zip(ks, betas)]
    kks = [_bdot_nt(kb, k) for kb, k in zip(kbs, ks)]
    pws = [jnp.where(strict, -(kk * dc), 0.0) for kk, dc in zip(kks, decays)]
    tmats = [eye + pw for pw in pws]
    for _ in range(int(math.log2(CHUNK)) - 1):
        pws = [_bdot(pw, pw) for pw in pws]
        tmats = [tm + _bdot(tm, pw) for tm, pw in zip(tmats, pws)]
    egcs = [jnp.exp(gc) for gc in gcs]
    uws = [_bdot(tm, jnp.concatenate([v * b, kb * eg], axis=1))
           for tm, v, b, kb, eg in zip(tmats, vs, betas, kbs, egcs)]
    us = [uw[:, :HEAD_DIM] for uw in uws]
    ws = [uw[:, HEAD_DIM:] for uw in uws]
    qkm = [_bdot_nt(q, k) for q, k in zip(qs, ks)]
    qkm = [jnp.where(incl, x * dc, 0.0) for x, dc in zip(qkm, decays)]
    q_decs = [q * eg for q, eg in zip(qs, egcs)]
    k_ends = [k * jnp.exp(gl - gc) for k, gl, gc in zip(ks, gls, gcs)]

    for hh in heads:
        vn_ref[hh] = jnp.zeros((tb, HEAD_DIM), F32)
    outs = [[] for _ in heads]
    for ci in range(tb // CHUNK):
        cs = slice(ci * CHUNK, (ci + 1) * CHUNK)
        sts = [state_ref[hh] for hh in heads]
        ws_qs = [_bdot(jnp.concatenate([ws[hh][cs], q_decs[hh][cs]], axis=0), sts[hh]) for hh in heads]
        v_news = [us[hh][cs] - ws_qs[hh][:CHUNK] for hh in heads]
        for hh in heads:
            vn_ref[hh, cs, :] = v_news[hh]
        intra = [_bdot(qkm[hh][cs], vn_ref[hh]) for hh in heads]
        upd = [_bdot_tn(k_ends[hh][cs], v_news[hh]) for hh in heads]
        for hh in heads:
            outs[hh].append(ws_qs[hh][CHUNK:] + intra[hh])
            g_last = gls[hh][ci * CHUNK:ci * CHUNK + 1, :]
            state_ref[hh] = sts[hh] * jnp.exp(g_last) + upd[hh]
    for hh in heads:
        o = jnp.concatenate(outs[hh], axis=0)
        o = o * lax.rsqrt(jnp.mean(o * o, axis=-1, keepdims=True) + EPS) * gout_ref[...]
        zz = z_ref[:, hsl[hh]].astype(F32)
        o_ref[:, hsl[hh]] = (o * (zz * jax.nn.sigmoid(zz))).astype(o_ref.dtype)


def _gated_delta(big, small, small_t, conv_w, g_out, bsz, seq, n_heads, d_model):
    t = bsz * seq
    tb = GDN_TB
    ns = seq // tb
    width = GDN_HG * HEAD_DIM
    nhg = n_heads // GDN_HG
    blocks_per_group = d_model // width
    rows_t = small_t.shape[0]

    def colspec(group):
        return pl.BlockSpec((tb, width), lambda b, h, s: (b * ns + s, group * blocks_per_group + h))

    def cwspec(group):
        return pl.BlockSpec((CONV_WIDTH, width), lambda b, h, s: (0, group * blocks_per_group + h))

    return pl.pallas_call(
        functools.partial(_gdn_kernel, n_heads=n_heads),
        grid=(bsz, nhg, ns),
        in_specs=[
            colspec(0), colspec(1), colspec(2), colspec(3),
            pl.BlockSpec((tb, LANES), lambda b, h, s: (b * ns + s, 0)),
            pl.BlockSpec((rows_t, tb), lambda b, h, s: (0, b * ns + s)),
            cwspec(0), cwspec(1), cwspec(2),
            pl.BlockSpec((1, HEAD_DIM), lambda b, h, s: (0, 0)),
        ],
        out_specs=pl.BlockSpec((tb, width), lambda b, h, s: (b * ns + s, h)),
        out_shape=jax.ShapeDtypeStruct((t, d_model), BF16),
        scratch_shapes=[
            pltpu.VMEM((GDN_HG, HEAD_DIM, HEAD_DIM), F32),
            pltpu.VMEM((tb + SUBLANES, width), F32),
            pltpu.VMEM((tb + SUBLANES, width), F32),
            pltpu.VMEM((tb + SUBLANES, width), F32),
            pltpu.VMEM((GDN_HG, tb, HEAD_DIM), F32),
        ],
        compiler_params=_cparams(("parallel", "parallel", "arbitrary")),
        name="gated_delta",
    )(big, big, big, big, small, small_t, conv_w, conv_w, conv_w, g_out)


def _t5_bucket(n):
    max_exact = N_BUCKETS // 2
    nf = jnp.maximum(n, 1).astype(F32)
    large = max_exact + (jnp.log(nf / max_exact) / math.log(MAX_DISTANCE / max_exact)
                         * (N_BUCKETS - max_exact)).astype(jnp.int32)
    large = jnp.minimum(large, N_BUCKETS - 1)
    return jnp.where(n < max_exact, n, large)


def _attn_kernel(rb_ref, q_ref, k_ref, v_ref, lam_ref, gsub_ref, o_ref,
                 bias_ref, m_ref, acc_ref, sa_ref, sb_ref, *, lam_init):
    h = pl.program_id(0)
    b = pl.program_id(1)
    qi = pl.program_id(2)
    bq, bk = ATT_BQ, ATT_BK

    @pl.when(jnp.logical_and(b == 0, qi == 0))
    def _():
        i = lax.broadcasted_iota(jnp.int32, (bq, bk), 0)
        jj = lax.broadcasted_iota(jnp.int32, (bq, bk), 1)
        far = rb_ref[N_BUCKETS - 1, h]
        bias_ref[2] = jnp.zeros((bq, bk), F32)
        for slot in range(2):
            n = i - jj + slot * bk
            bucket = _t5_bucket(jnp.maximum(n, 0))
            bias = jnp.zeros((bq, bk), F32)
            for cc in range(N_BUCKETS):
                bias = jnp.where(bucket == cc, rb_ref[cc, h] - far, bias)
            if slot == 0:
                bias = jnp.where(n >= 0, bias, NEG_BIG)
            bias_ref[slot] = bias

    m_ref[...] = jnp.full(m_ref.shape, NEG_BIG, F32)
    acc_ref[...] = jnp.zeros(acc_ref.shape, F32)

    q = q_ref[...]
    lane = lax.broadcasted_iota(jnp.int32, q.shape, 1)
    zero = jnp.zeros_like(q)
    qs = jnp.concatenate([jnp.where(lane < DH_DIFF, q, zero), jnp.where(lane < DH_DIFF, zero, q)], axis=0)
    ones_col = (lax.broadcasted_iota(jnp.int32, (bk, HEAD_DIM), 1) == 0).astype(BF16)

    def scores(j):
        ks = pl.multiple_of(j * bk, bk)
        return lax.dot_general(qs, k_ref[pl.ds(ks, bk), :], (((1,), (1,)), ((), ())),
                               preferred_element_type=F32)

    def absorb(j, sc_ref):
        ks = pl.multiple_of(j * bk, bk)
        v_ext = jnp.concatenate([v_ref[pl.ds(ks, bk), :], ones_col], axis=1)
        bias = bias_ref[jnp.minimum(qi - j, 2)]
        sc = jnp.concatenate([sc_ref[0:bq, :] + bias, sc_ref[bq:2 * bq, :] + bias], axis=0)
        m_old = m_ref[...]
        m_new = jnp.maximum(m_old, jnp.max(sc, axis=-1, keepdims=True))
        p = jnp.exp(sc - m_new)
        acc_ref[...] = (jnp.exp(m_old - m_new) * acc_ref[...]
                        + jnp.dot(p.astype(BF16), v_ext, preferred_element_type=F32))
        m_ref[...] = m_new

    n_tiles = qi + 1
    sa_ref[...] = scores(0)

    def pair_body(jj, carry):
        j0 = 2 * jj
        sb_ref[...] = scores(j0 + 1)
        absorb(j0, sa_ref)
        sa_ref[...] = scores(jnp.minimum(j0 + 2, qi))
        absorb(j0 + 1, sb_ref)
        return carry

    lax.fori_loop(0, n_tiles // 2, pair_body, 0)

    @pl.when(n_tiles % 2 == 1)
    def _():
        absorb(qi, sa_ref)

    lam_p = lam_ref[...]
    s1 = jnp.sum(lam_p[0:1] * lam_p[1:2], axis=-1, keepdims=True)
    s2 = jnp.sum(lam_p[2:3] * lam_p[3:4], axis=-1, keepdims=True)
    lam = jnp.exp(s1) - jnp.exp(s2) + lam_init
    acc = acc_ref[...]
    num = acc[:, :HEAD_DIM]
    den = acc[:, HEAD_DIM:HEAD_DIM + 1]
    o = num[:bq] / den[:bq] - lam * (num[bq:] / den[bq:])
    o = o * lax.rsqrt(jnp.mean(o * o, axis=-1, keepdims=True) + EPS) * gsub_ref[...]
    o_ref[...] = (o * (1.0 - lam_init)).astype(o_ref.dtype)


def _diff_attention(big, rel_bias, lam_params, g_subln, bsz, seq, n_heads, d_model, lam_init):
    t = bsz * seq
    nq = seq // ATT_BQ
    qcol = 4 * d_model // HEAD_DIM
    per = d_model // HEAD_DIM
    return pl.pallas_call(
        functools.partial(_attn_kernel, lam_init=lam_init),
        grid=(n_heads, bsz, nq),
        in_specs=[
            pl.BlockSpec(memory_space=pltpu.SMEM),
            pl.BlockSpec((ATT_BQ, HEAD_DIM), lambda h, b, i: (b * nq + i, qcol + h)),
            pl.BlockSpec((seq, HEAD_DIM), lambda h, b, i: (b, qcol + per + h)),
            pl.BlockSpec((seq, HEAD_DIM), lambda h, b, i: (b, qcol + 2 * per + h)),
            pl.BlockSpec((4, DH_DIFF), lambda h, b, i: (0, 0)),
            pl.BlockSpec((1, HEAD_DIM), lambda h, b, i: (0, 0)),
        ],
        out_specs=pl.BlockSpec((ATT_BQ, HEAD_DIM), lambda h, b, i: (b * nq + i, h)),
        out_shape=jax.ShapeDtypeStruct((t, d_model), BF16),
        scratch_shapes=[
            pltpu.VMEM((3, ATT_BQ, ATT_BK), F32),
            pltpu.VMEM((2 * ATT_BQ, 1), F32),
            pltpu.VMEM((2 * ATT_BQ, 2 * HEAD_DIM), F32),
            pltpu.VMEM((2 * ATT_BQ, ATT_BK), F32),
            pltpu.VMEM((2 * ATT_BQ, ATT_BK), F32),
        ],
        compiler_params=_cparams(("arbitrary", "arbitrary", "arbitrary")),
        name="diff_attention",
    )(rel_bias, big, big, big, lam_params, g_subln)


def _mix_kernel(ga_ref, gb_ref, oa_ref, od_ref, x_ref, wo_ref, gffn_ref, wr_ref, br_ref,
                x1_ref, h2_ref, topi_ref, topw_ref, rank_ref, cnt_ref, carry_ref):
    i = pl.program_id(0)
    tm = MIX_TM

    @pl.when(i == 0)
    def _():
        carry_ref[...] = jnp.zeros_like(carry_ref)

    mix = (ga_ref[...].astype(F32) * oa_ref[...].astype(F32)
           + gb_ref[...].astype(F32) * od_ref[...].astype(F32))
    x1 = x_ref[...] + jnp.dot(mix.astype(BF16), wo_ref[...], preferred_element_type=F32)
    x1_ref[...] = x1
    h2 = x1 * lax.rsqrt(jnp.mean(x1 * x1, axis=-1, keepdims=True) + EPS) * gffn_ref[...]
    h2_ref[...] = h2

    logits = lax.dot_general(wr_ref[...], h2, (((1,), (1,)), ((), ())),
                             preferred_element_type=F32, precision=lax.Precision.HIGHEST) + br_ref[...]
    eidx = lax.broadcasted_iota(jnp.int32, logits.shape, 0).astype(F32)
    vals, hots = [], []
    cur = logits
    for kk in range(TOP_K):
        mx = jnp.max(cur, axis=0, keepdims=True)
        idx = jnp.min(jnp.where(cur == mx, eidx, float(N_EXPERTS)), axis=0, keepdims=True)
        hot = eidx == idx
        vals.append(mx)
        hots.append(hot)
        topi_ref[kk:kk + 1, :] = idx.astype(jnp.int32)
        cur = jnp.where(hot, -jnp.inf, cur)
    exps = [jnp.exp(vv - vals[0]) for vv in vals]
    denom = exps[0] + exps[1] + exps[2] + exps[3]
    for kk in range(TOP_K):
        topw_ref[kk:kk + 1, :] = exps[kk] / denom

    sel = hots[0]
    for kk in range(1, TOP_K):
        sel = jnp.logical_or(sel, hots[kk])
    sel_f = sel.astype(F32)
    r = lax.broadcasted_iota(jnp.int32, (tm, tm), 0)
    c = lax.broadcasted_iota(jnp.int32, (tm, tm), 1)
    before = _bdot(sel_f, (r < c).astype(F32)) + carry_ref[...]
    for kk in range(TOP_K):
        rank_ref[kk:kk + 1, :] = jnp.sum(jnp.where(hots[kk], before, 0.0), axis=0,
                                         keepdims=True).astype(jnp.int32)
    carry_ref[...] = carry_ref[...] + jnp.sum(sel_f, axis=-1, keepdims=True)
    cnt_ref[...] = carry_ref[...].astype(jnp.int32)


def _mix_project_route(big, oa, od, x2d, w_o, g_ffn, w_r_t, b_r, d_model):
    t = x2d.shape[0]
    tm = MIX_TM
    gate_blk = 7
    full = lambda shape: pl.BlockSpec(shape, lambda i: (0, 0))
    row = lambda: pl.BlockSpec((tm, d_model), lambda i: (i, 0))
    krow = lambda: pl.BlockSpec((TOP_K, tm), lambda i: (0, i))
    return pl.pallas_call(
        _mix_kernel,
        grid=(t // tm,),
        in_specs=[
            pl.BlockSpec((tm, d_model), lambda i: (i, gate_blk)),
            pl.BlockSpec((tm, d_model), lambda i: (i, gate_blk + 1)),
            row(), row(), row(),
            full((d_model, d_model)), full((1, d_model)), full((N_EXPERTS, d_model)), full((N_EXPERTS, 1)),
        ],
        out_specs=[row(), row(), krow(), krow(), krow(), full((N_EXPERTS, 1))],
        out_shape=[
            jax.ShapeDtypeStruct((t, d_model), F32),
            jax.ShapeDtypeStruct((t, d_model), F32),
            jax.ShapeDtypeStruct((TOP_K, t), jnp.int32),
            jax.ShapeDtypeStruct((TOP_K, t), F32),
            jax.ShapeDtypeStruct((TOP_K, t), jnp.int32),
            jax.ShapeDtypeStruct((N_EXPERTS, 1), jnp.int32),
        ],
        scratch_shapes=[pltpu.VMEM((N_EXPERTS, 1), F32)],
        compiler_params=_cparams(("arbitrary",)),
        name="merge_outproj_route",
    )(big, big, oa, od, x2d, w_o, g_ffn, w_r_t, b_r)


def _row_copy(src_ref, src_row, dst_ref, dst_row, sem):
    return pltpu.make_async_copy(src_ref.at[pl.ds(src_row, 1)], dst_ref.at[pl.ds(dst_row, 1)], sem)


def _expert_kernel(be_ref, src_ref, src_next_ref, dst_prev_ref, dst_ref, h2_ref, wup_ref, bup_ref,
                   wdn_ref, bdn_ref, y_ref, x0, x1, y0, y1, wup_bf, wdn_bf, gsem, ssem):
    i = pl.program_id(0)
    last = pl.num_programs(0) - 1
    rb = MOE_RB
    d_ff = wdn_ref.shape[1]

    def gather_wait(x_ref, s):
        pltpu.make_async_copy(h2_ref.at[pl.ds(0, rb)], x_ref, gsem.at[s]).wait()

    def scatter_wait(yb_ref, s):
        pltpu.make_async_copy(yb_ref, y_ref.at[pl.ds(0, rb)], ssem.at[s]).wait()

    @pl.when(i == 0)
    def _():
        for a in range(rb):
            _row_copy(h2_ref, src_ref[a], x0, a, gsem.at[0]).start()

    @pl.when(jnp.logical_or(i == 0, be_ref[i] != be_ref[jnp.maximum(i - 1, 0)]))
    def _():
        rr = lax.broadcasted_iota(jnp.int32, (2 * LANES, 2 * LANES), 0)
        cc = lax.broadcasted_iota(jnp.int32, (2 * LANES, 2 * LANES), 1)
        pick = jnp.where(cc < LANES, 2 * cc, 2 * (cc - LANES) + 1)
        perm = (rr == pick).astype(BF16)
        for g in range(wup_ref.shape[2] // (2 * LANES)):
            cs = slice(g * 2 * LANES, (g + 1) * 2 * LANES)
            wup_bf[:, cs] = jnp.dot(wup_ref[0, :, cs].astype(BF16), perm,
                                    preferred_element_type=F32).astype(BF16)
        wdn_bf[...] = wdn_ref[0].astype(BF16)

    @pl.when(i == 0)
    def _():
        y1[...] = jnp.zeros(y1.shape, F32)

    def block(xa, ya, xb, yb, s):
        gather_wait(xa, s)

        @pl.when(i >= 1)
        def _():
            scatter_wait(ya, s)

        for a in range(rb):
            _row_copy(h2_ref, src_next_ref[a], xb, a, gsem.at[1 - s]).start()
            _row_copy(yb, a, y_ref, dst_prev_ref[a], ssem.at[1 - s]).start()
        hid = jnp.dot(xa[...].astype(BF16), wup_bf[...], preferred_element_type=F32) + bup_ref[0]
        acts = []
        for g in range(hid.shape[1] // (2 * LANES)):
            glu = jnp.minimum(hid[:, g * 2 * LANES:g * 2 * LANES + LANES], SWIGLU_LIMIT)
            lin = jnp.clip(hid[:, g * 2 * LANES + LANES:(g + 1) * 2 * LANES], -SWIGLU_LIMIT, SWIGLU_LIMIT)
            acts.append(glu * jax.nn.sigmoid(SWIGLU_ALPHA * glu) * (lin + 1.0))
        act = jnp.concatenate(acts, axis=1)
        assert act.shape[1] == d_ff
        ya[...] = jnp.dot(act.astype(BF16), wdn_bf[...], preferred_element_type=F32) + bdn_ref[0]

        @pl.when(i == last)
        def _():
            for a in range(rb):
                _row_copy(ya, a, y_ref, dst_ref[a], ssem.at[s]).start()
            scatter_wait(yb, 1 - s)
            scatter_wait(ya, s)
            gather_wait(xb, 1 - s)

    @pl.when((i & 1) == 1)
    def _():
        block(x1, y1, x0, y0, 1)

    @pl.when((i & 1) == 0)
    def _():
        block(x0, y0, x1, y1, 0)


def _experts(block_e, src_tok, dst_row, h2, w_up, b_up, w_down, b_down):
    n_rows = src_tok.shape[0]
    d = h2.shape[1]
    nb = n_rows // MOE_RB
    two_ff = w_up.shape[2]
    d_ff = w_down.shape[1]
    dst_ext = jnp.concatenate([n_rows + jnp.arange(MOE_RB, dtype=jnp.int32), dst_row])
    smem_rows = lambda fn: pl.BlockSpec((MOE_RB,), fn, memory_space=pltpu.SMEM)
    grid_spec = pltpu.PrefetchScalarGridSpec(
        num_scalar_prefetch=1,
        grid=(nb,),
        in_specs=[
            smem_rows(lambda i, be: (i,)),
            smem_rows(lambda i, be: (jnp.minimum(i + 1, nb - 1),)),
            smem_rows(lambda i, be: (i,)),
            smem_rows(lambda i, be: (i + 1,)),
            pl.BlockSpec(memory_space=pl.ANY),
            pl.BlockSpec((1, d, two_ff), lambda i, be: (be[i], 0, 0)),
            pl.BlockSpec((1, 1, two_ff), lambda i, be: (be[i], 0, 0)),
            pl.BlockSpec((1, d_ff, d), lambda i, be: (be[i], 0, 0)),
            pl.BlockSpec((1, 1, d), lambda i, be: (be[i], 0, 0)),
        ],
        out_specs=pl.BlockSpec(memory_space=pl.ANY),
        scratch_shapes=[
            pltpu.VMEM((MOE_RB, d), F32), pltpu.VMEM((MOE_RB, d), F32),
            pltpu.VMEM((MOE_RB, d), F32), pltpu.VMEM((MOE_RB, d), F32),
            pltpu.VMEM((d, two_ff), BF16),
            pltpu.VMEM((d_ff, d), BF16),
            pltpu.SemaphoreType.DMA((2,)),
            pltpu.SemaphoreType.DMA((2,)),
        ],
    )
    return pl.pallas_call(
        _expert_kernel,
        grid_spec=grid_spec,
        out_shape=jax.ShapeDtypeStruct((n_rows + MOE_RB, d), F32),
        compiler_params=_cparams(("arbitrary",)),
        name="moe_experts",
    )(block_e, src_tok, src_tok, dst_ext, dst_ext, h2, w_up, b_up, w_down, b_down)


def _combine_kernel(x1_ref, w_ref, y0_ref, y1_ref, y2_ref, y3_ref, o_ref):
    w = w_ref[...]
    out = x1_ref[...]
    for kk, y_ref in enumerate((y0_ref, y1_ref, y2_ref, y3_ref)):
        out = out + w[:, kk:kk + 1] * y_ref[...]
    o_ref[...] = out


def _combine(x1, w_tok, y_slots):
    t, d = x1.shape
    tc = COMB_TC
    nt = t // tc
    yspec = lambda kk: pl.BlockSpec((tc, d), lambda i: (kk * nt + i, 0))
    return pl.pallas_call(
        _combine_kernel,
        grid=(nt,),
        in_specs=[
            pl.BlockSpec((tc, d), lambda i: (i, 0)),
            pl.BlockSpec((tc, TOP_K), lambda i: (i, 0)),
            yspec(0), yspec(1), yspec(2), yspec(3),
        ],
        out_specs=pl.BlockSpec((tc, d), lambda i: (i, 0)),
        out_shape=jax.ShapeDtypeStruct((t, d), F32),
        compiler_params=_cparams(("parallel",)),
        name="moe_combine",
    )(x1, w_tok, y_slots, y_slots, y_slots, y_slots)


def _moe(x1, h2, topi, topw, rank, counts, w_up, b_up, w_down, b_down):
    t, d = x1.shape
    n_assign = t * TOP_K
    nb = -(-n_assign // MOE_RB) + N_EXPERTS
    n_rows = nb * MOE_RB
    counts = counts[:, 0]
    padded = (counts + MOE_RB - 1) // MOE_RB * MOE_RB
    padded_end = jnp.cumsum(padded)
    padded_start = padded_end - padded
    expert_ids = jnp.arange(N_EXPERTS, dtype=jnp.int32)[:, None, None]
    start_of = jnp.sum(jnp.where(topi[None] == expert_ids, padded_start[:, None, None], 0), axis=0)
    dest = (start_of + rank).astype(jnp.int32)
    n_used = (padded_end[-1] // MOE_RB).astype(jnp.int32)
    blk = jnp.minimum(jnp.arange(nb, dtype=jnp.int32), n_used - 1)
    block_e = jnp.minimum(jnp.sum(padded_end[None, :] <= (blk * MOE_RB)[:, None], axis=1),
                          N_EXPERTS - 1).astype(jnp.int32)
    slot_of = jnp.full((n_rows,), -1, jnp.int32).at[dest.reshape(-1)].set(
        jnp.arange(n_assign, dtype=jnp.int32), unique_indices=True)
    is_pad = slot_of < 0
    src_tok = jnp.where(is_pad, 0, slot_of % t)
    dst_row = jnp.where(is_pad, n_assign + jnp.cumsum(is_pad.astype(jnp.int32)) - 1, slot_of)

    y_slots = _experts(block_e, src_tok, dst_row, h2, w_up, b_up, w_down, b_down)
    return _combine(x1, topw.T, y_slots)


def kernel(x, g_mix, w_in, b_gate, conv_w, a_log, dt_bias, g_delta_out, q_norm, k_norm, lambda_q1, lambda_k1, lambda_q2, lambda_k2, g_subln, rel_bias, w_o, g_ffn, w_router, b_router, w_up, b_up, w_down, b_down):
    bsz, seq, d = x.shape
    depth = g_mix.shape[0]
    n_heads = d // HEAD_DIM
    t = bsz * seq
    d_ff = w_down.shape[2]
    assert d % PROJ_TN == 0 and t % PROJ_TM == 0 and seq % GDN_TB == 0 and seq % ATT_BQ == 0
    assert t % MIX_TM == 0 and t % COMB_TC == 0 and n_heads % GDN_HG == 0
    assert (t * TOP_K) % MOE_RB == 0
    assert 2 * n_heads <= 2 * SUBLANES

    x2d = x.reshape(t, d)
    for l in range(depth):
        wl = w_in[l]
        c0 = 4 * d
        c1 = c0 + 2 * n_heads
        w_cat = jnp.concatenate([wl[:, :c0], wl[:, c1:]], axis=1).astype(BF16)
        w_small = jnp.pad(wl[:, c0:c1], ((0, 0), (0, LANES - 2 * n_heads)))
        qk_gain = jnp.concatenate([jnp.tile(q_norm[l] * (DH_DIFF ** -0.5), 2 * n_heads),
                                   jnp.tile(k_norm[l], 2 * n_heads)])
        aux = jnp.concatenate([jnp.zeros((4 * d,), F32), qk_gain, jnp.zeros((d,), F32),
                               b_gate[l].reshape(-1)]).reshape(1, -1)
        big = _input_projection(x2d, g_mix[l].reshape(1, d), w_cat, aux,
                                n_plain_a=4 * d // PROJ_TN, n_norm=2 * d // PROJ_TN,
                                n_plain_b=d // PROJ_TN)

        head_pad = jnp.zeros((LANES - 2 * n_heads,), F32)
        alog = jnp.concatenate([jnp.zeros((n_heads,), F32), a_log[l], head_pad])
        dtb = jnp.concatenate([jnp.zeros((n_heads,), F32), dt_bias[l], head_pad])
        rows_t = 2 * n_heads
        small, small_t = _small_projection(
            x2d, g_mix[l].reshape(1, d), w_small.astype(BF16), w_small[:, :rows_t].T.astype(BF16),
            alog.reshape(1, LANES), dtb.reshape(1, LANES),
            alog[:rows_t].reshape(rows_t, 1), dtb[:rows_t].reshape(rows_t, 1), n_heads)

        oa = _gated_delta(big, small, small_t, conv_w[l], g_delta_out[l].reshape(1, HEAD_DIM),
                          bsz, seq, n_heads, d)

        lam_init = 0.8 - 0.6 * math.exp(-0.3 * l)
        lam_params = jnp.stack([lambda_q1[l], lambda_k1[l], lambda_q2[l], lambda_k2[l]])
        od = _diff_attention(big, rel_bias, lam_params, g_subln[l].reshape(1, HEAD_DIM),
                             bsz, seq, n_heads, d, lam_init)

        x1, h2, topi, topw, rank, counts = _mix_project_route(
            big, oa, od, x2d, w_o[l].astype(BF16), g_ffn[l].reshape(1, d),
            w_router[l].T, b_router[l].reshape(N_EXPERTS, 1), d)

        b_up_l = b_up[l].reshape(N_EXPERTS, 2 * d_ff // (2 * LANES), LANES, 2)
        b_up_l = jnp.swapaxes(b_up_l, 2, 3).reshape(N_EXPERTS, 1, 2 * d_ff)
        x2d = _moe(x1, h2, topi, topw, rank, counts, w_up[l], b_up_l,
                   w_down[l], b_down[l].reshape(N_EXPERTS, 1, d))
    return x2d.reshape(bsz, seq, d)
```

```python
import functools
import math

import jax
import jax.numpy as jnp
from jax import lax
from jax.experimental import pallas as pl
from jax.experimental.pallas import tpu as pltpu

F32 = jnp.float32
BF16 = jnp.bfloat16

HEAD_DIM = 128
DH_DIFF = HEAD_DIM // 2
CONV_WIDTH = 4
CHUNK = 64
N_BUCKETS = 32
MAX_DISTANCE = 128
N_EXPERTS = 32
TOP_K = 4
TOP_K_SHIFT = 2
SWIGLU_LIMIT = 7.0
SWIGLU_ALPHA = 1.702
EPS = 1e-6
NEG_BIG = -1e30

LANES = 128
SUBLANES = 8
VMEM_LIMIT = 56 * 1024 * 1024

PROJ_TM = 1024
PROJ_TN = 1024
GDN_TB = 256
GDN_HG = 4
ATT_BQ = 512
ATT_BK = 512
MIX_TM = 512
MOE_RB = 256
COMB_TC = 512


def _cparams(sem):
    return pltpu.CompilerParams(dimension_semantics=sem, vmem_limit_bytes=VMEM_LIMIT)


def _bdot(a, b):
    return jnp.dot(a.astype(BF16), b.astype(BF16), preferred_element_type=F32)


def _bdot_nt(a, b):
    return lax.dot_general(a.astype(BF16), b.astype(BF16), (((1,), (1,)), ((), ())),
                           preferred_element_type=F32)


def _bdot_tn(a, b):
    return lax.dot_general(a.astype(BF16), b.astype(BF16), (((0,), (0,)), ((), ())),
                           preferred_element_type=F32)


def _fdot(a, b):
    return jnp.dot(a, b, preferred_element_type=F32, precision=lax.Precision.HIGHEST)


def _proj_kernel(x_ref, g_ref, w_ref, aux_ref, o_ref, h_ref, *, n_plain_a, n_norm, n_plain_b):
    j = pl.program_id(1)

    @pl.when(j == 0)
    def _():
        x = x_ref[...]
        ms = jnp.mean(x * x, axis=-1, keepdims=True)
        h_ref[...] = (x * lax.rsqrt(ms + EPS) * g_ref[...]).astype(BF16)

    acc = jnp.dot(h_ref[...], w_ref[...], preferred_element_type=F32)
    first_norm = n_plain_a
    first_plain_b = n_plain_a + n_norm
    first_gate = first_plain_b + n_plain_b

    @pl.when(jnp.logical_or(j < first_norm, jnp.logical_and(j >= first_plain_b, j < first_gate)))
    def _():
        o_ref[...] = acc.astype(o_ref.dtype)

    @pl.when(jnp.logical_and(j >= first_norm, j < first_plain_b))
    def _():
        lane = lax.broadcasted_iota(jnp.int32, (1, LANES), 1)
        lo = lane < DH_DIFF
        for c in range(PROJ_TN // LANES):
            sl = slice(c * LANES, (c + 1) * LANES)
            y = acc[:, sl]
            y2 = y * y
            s_lo = jnp.sum(jnp.where(lo, y2, 0.0), axis=-1, keepdims=True)
            s_hi = jnp.sum(jnp.where(lo, 0.0, y2), axis=-1, keepdims=True)
            r = jnp.where(lo, lax.rsqrt(s_lo / DH_DIFF + EPS), lax.rsqrt(s_hi / DH_DIFF + EPS))
            o_ref[:, sl] = (y * r * aux_ref[:, sl]).astype(o_ref.dtype)

    @pl.when(j >= first_gate)
    def _():
        o_ref[...] = jax.nn.sigmoid(acc + aux_ref[...]).astype(o_ref.dtype)


def _input_projection(x2d, g_mix, w_cat, aux, *, n_plain_a, n_norm, n_plain_b):
    t, d = x2d.shape
    n = w_cat.shape[1]
    kern = functools.partial(_proj_kernel, n_plain_a=n_plain_a, n_norm=n_norm, n_plain_b=n_plain_b)
    return pl.pallas_call(
        kern,
        grid=(t // PROJ_TM, n // PROJ_TN),
        in_specs=[
            pl.BlockSpec((PROJ_TM, d), lambda i, j: (i, 0)),
            pl.BlockSpec((1, d), lambda i, j: (0, 0)),
            pl.BlockSpec((d, PROJ_TN), lambda i, j: (0, j)),
            pl.BlockSpec((1, PROJ_TN), lambda i, j: (0, j)),
        ],
        out_specs=pl.BlockSpec((PROJ_TM, PROJ_TN), lambda i, j: (i, j)),
        out_shape=jax.ShapeDtypeStruct((t, n), BF16),
        scratch_shapes=[pltpu.VMEM((PROJ_TM, d), BF16)],
        compiler_params=_cparams(("parallel", "arbitrary")),
        name="input_projection",
    )(x2d, g_mix, w_cat, aux)


def _small_proj_kernel(x_ref, g_ref, w_ref, wt_ref, alog_ref, dtb_ref, alog_t_ref, dtb_t_ref,
                       o_ref, ot_ref, *, n_heads):
    x = x_ref[...]
    ms = jnp.mean(x * x, axis=-1, keepdims=True)
    h = (x * lax.rsqrt(ms + EPS) * g_ref[...]).astype(BF16)

    def finish(acc, idx, alog, dtb):
        beta = jax.nn.sigmoid(acc)
        z = acc + dtb
        softplus = jnp.maximum(z, 0.0) + jnp.log1p(jnp.exp(-jnp.abs(z)))
        gdec = -jnp.exp(alog) * softplus
        return jnp.where(idx < n_heads, beta, jnp.where(idx < 2 * n_heads, gdec, 0.0))

    acc = jnp.dot(h, w_ref[...], preferred_element_type=F32)
    lane = lax.broadcasted_iota(jnp.int32, acc.shape, 1)
    o_ref[...] = finish(acc, lane, alog_ref[...], dtb_ref[...])
    acc_t = lax.dot_general(wt_ref[...], h, (((1,), (1,)), ((), ())),
                            preferred_element_type=F32)
    sub = lax.broadcasted_iota(jnp.int32, acc_t.shape, 0)
    ot_ref[...] = finish(acc_t, sub, alog_t_ref[...], dtb_t_ref[...])


def _small_projection(x2d, g_mix, w_small, w_small_t, alog, dtb, alog_t, dtb_t, n_heads):
    t, d = x2d.shape
    rows_t = w_small_t.shape[0]
    tm = PROJ_TM
    full = lambda shape: pl.BlockSpec(shape, lambda i: (0, 0))
    return pl.pallas_call(
        functools.partial(_small_proj_kernel, n_heads=n_heads),
        grid=(t // tm,),
        in_specs=[
            pl.BlockSpec((tm, d), lambda i: (i, 0)),
            full((1, d)), full((d, LANES)), full((rows_t, d)),
            full((1, LANES)), full((1, LANES)), full((rows_t, 1)), full((rows_t, 1)),
        ],
        out_specs=[pl.BlockSpec((tm, LANES), lambda i: (i, 0)),
                   pl.BlockSpec((rows_t, tm), lambda i: (0, i))],
        out_shape=[jax.ShapeDtypeStruct((t, LANES), F32),
                   jax.ShapeDtypeStruct((rows_t, t), F32)],
        compiler_params=_cparams(("parallel",)),
        name="beta_decay_projection",
    )(x2d, g_mix, w_small, w_small_t, alog, dtb, alog_t, dtb_t)


def _gdn_kernel(q_ref, k_ref, v_ref, z_ref, sm_ref, smt_ref, cwq_ref, cwk_ref, cwv_ref, gout_ref,
                o_ref, state_ref, qp_ref, kp_ref, vp_ref, vn_ref, *, n_heads):
    hg = pl.program_id(1)
    s = pl.program_id(2)
    tb = GDN_TB
    pad = SUBLANES
    width = GDN_HG * HEAD_DIM

    @pl.when(s == 0)
    def _():
        state_ref[...] = jnp.zeros_like(state_ref)
        for p_ref in (qp_ref, kp_ref, vp_ref):
            p_ref[0:pad, :] = jnp.zeros((pad, width), F32)

    def conv_silu(x_ref, p_ref, cw_ref):
        p_ref[pad:pad + tb, :] = x_ref[...].astype(F32)
        acc = cw_ref[CONV_WIDTH - 1:CONV_WIDTH, :] * p_ref[pad:pad + tb, :]
        for jj in range(CONV_WIDTH - 1):
            off = pad - (CONV_WIDTH - 1) + jj
            acc = acc + cw_ref[jj:jj + 1, :] * p_ref[off:off + tb, :]
        p_ref[0:pad, :] = p_ref[tb:tb + pad, :]
        return acc * jax.nn.sigmoid(acc)

    q_all = conv_silu(q_ref, qp_ref, cwq_ref)
    k_all = conv_silu(k_ref, kp_ref, cwk_ref)
    v_all = conv_silu(v_ref, vp_ref, cwv_ref)

    r = lax.broadcasted_iota(jnp.int32, (tb, tb), 0)
    c = lax.broadcasted_iota(jnp.int32, (tb, tb), 1)
    shift = int(math.log2(CHUNK))
    same = (r >> shift) == (c >> shift)
    incl = jnp.logical_and(same, c <= r)
    strict = jnp.logical_and(same, c < r)
    eye = (r == c).astype(F32)

    small = sm_ref[...]
    small_t = smt_ref[...]
    lane = lax.broadcasted_iota(jnp.int32, small.shape, 1)
    sub = lax.broadcasted_iota(jnp.int32, small_t.shape, 0)
    gcum = _fdot(incl.astype(F32), small)
    gtot = _fdot(same.astype(F32), small)
    gcum_t = _fdot(small_t, jnp.logical_and(same, r <= c).astype(F32))

    heads = range(GDN_HG)
    hsl = [slice(hh * HEAD_DIM, (hh + 1) * HEAD_DIM) for hh in heads]
    qs = [q_all[:, hs] for hs in hsl]
    ks = [k_all[:, hs] for hs in hsl]
    vs = [v_all[:, hs] for hs in hsl]
    qs = [q * lax.rsqrt(jnp.sum(q * q, axis=-1, keepdims=True) + EPS) * (HEAD_DIM ** -0.5) for q in qs]
    ks = [k * lax.rsqrt(jnp.sum(k * k, axis=-1, keepdims=True) + EPS) for k in ks]

    def col_of(arr, idx):
        return jnp.sum(jnp.where(lane == idx, arr, 0.0), axis=-1, keepdims=True)

    head_ids = [hg * GDN_HG + hh for hh in heads]
    betas = [col_of(small, hd) for hd in head_ids]
    gcs = [col_of(gcum, hd + n_heads) for hd in head_ids]
    gls = [col_of(gtot, hd + n_heads) for hd in head_ids]
    gc_rows = [jnp.sum(jnp.where(sub == hd + n_heads, gcum_t, 0.0), axis=0, keepdims=True)
               for hd in head_ids]

    decays = [jnp.where(incl, jnp.exp(jnp.minimum(gc - gr, 0.0)), 0.0) for gc, gr in zip(gcs, gc_rows)]
    kbs = [k * b for k, b in zip(ks, betas)]
    kks = [_bdot_nt(kb, k) for kb, k in zip(kbs, ks)]
    pws = [jnp.where(strict, -(kk * dc), 0.0) for kk, dc in zip(kks, decays)]
    tmats = [eye + pw for pw in pws]
    for _ in range(int(math.log2(CHUNK)) - 1):
        pws = [_bdot(pw, pw) for pw in pws]
        tmats = [tm + _bdot(tm, pw) for tm, pw in zip(tmats, pws)]
    egcs = [jnp.exp(gc) for gc in gcs]
    uws = [_bdot(tm, jnp.concatenate([v * b, kb * eg], axis=1))
           for tm, v, b, kb, eg in zip(tmats, vs, betas, kbs, egcs)]
    us = [uw[:, :HEAD_DIM] for uw in uws]
    ws = [uw[:, HEAD_DIM:] for uw in uws]
    qkm = [_bdot_nt(q, k) for q, k in zip(qs, ks)]
    qkm = [jnp.where(incl, x * dc, 0.0) for x, dc in zip(qkm, decays)]
    q_decs = [q * eg for q, eg in zip(qs, egcs)]
    k_ends = [k * jnp.exp(gl - gc) for k, gl, gc in zip(ks, gls, gcs)]

    for hh in heads:
        vn_ref[hh] = jnp.zeros((tb, HEAD_DIM), F32)
    outs = [[] for _ in heads]
    for ci in range(tb // CHUNK):
        cs = slice(ci * CHUNK, (ci + 1) * CHUNK)
        sts = [state_ref[hh] for hh in heads]
        ws_qs = [_bdot(jnp.concatenate([ws[hh][cs], q_decs[hh][cs]], axis=0), sts[hh]) for hh in heads]
        v_news = [us[hh][cs] - ws_qs[hh][:CHUNK] for hh in heads]
        for hh in heads:
            vn_ref[hh, cs, :] = v_news[hh]
        intra = [_bdot(qkm[hh][cs], vn_ref[hh]) for hh in heads]
        upd = [_bdot_tn(k_ends[hh][cs], v_news[hh]) for hh in heads]
        for hh in heads:
            outs[hh].append(ws_qs[hh][CHUNK:] + intra[hh])
            g_last = gls[hh][ci * CHUNK:ci * CHUNK + 1, :]
            state_ref[hh] = sts[hh] * jnp.exp(g_last) + upd[hh]
    for hh in heads:
        o = jnp.concatenate(outs[hh], axis=0)
        o = o * lax.rsqrt(jnp.mean(o * o, axis=-1, keepdims=True) + EPS) * gout_ref[...]
        zz = z_ref[:, hsl[hh]].astype(F32)
        o_ref[:, hsl[hh]] = (o * (zz * jax.nn.sigmoid(zz))).astype(o_ref.dtype)


def _gated_delta(big, small, small_t, conv_w, g_out, bsz, seq, n_heads, d_model):
    t = bsz * seq
    tb = GDN_TB
    ns = seq // tb
    width = GDN_HG * HEAD_DIM
    nhg = n_heads // GDN_HG
    blocks_per_group = d_model // width
    rows_t = small_t.shape[0]

    def colspec(group):
        return pl.BlockSpec((tb, width), lambda b, h, s: (b * ns + s, group * blocks_per_group + h))

    def cwspec(group):
        return pl.BlockSpec((CONV_WIDTH, width), lambda b, h, s: (0, group * blocks_per_group + h))

    return pl.pallas_call(
        functools.partial(_gdn_kernel, n_heads=n_heads),
        grid=(bsz, nhg, ns),
        in_specs=[
            colspec(0), colspec(1), colspec(2), colspec(3),
            pl.BlockSpec((tb, LANES), lambda b, h, s: (b * ns + s, 0)),
            pl.BlockSpec((rows_t, tb), lambda b, h, s: (0, b * ns + s)),
            cwspec(0), cwspec(1), cwspec(2),
            pl.BlockSpec((1, HEAD_DIM), lambda b, h, s: (0, 0)),
        ],
        out_specs=pl.BlockSpec((tb, width), lambda b, h, s: (b * ns + s, h)),
        out_shape=jax.ShapeDtypeStruct((t, d_model), BF16),
        scratch_shapes=[
            pltpu.VMEM((GDN_HG, HEAD_DIM, HEAD_DIM), F32),
            pltpu.VMEM((tb + SUBLANES, width), F32),
            pltpu.VMEM((tb + SUBLANES, width), F32),
            pltpu.VMEM((tb + SUBLANES, width), F32),
            pltpu.VMEM((GDN_HG, tb, HEAD_DIM), F32),
        ],
        compiler_params=_cparams(("parallel", "parallel", "arbitrary")),
        name="gated_delta",
    )(big, big, big, big, small, small_t, conv_w, conv_w, conv_w, g_out)


def _t5_bucket(n):
    max_exact = N_BUCKETS // 2
    nf = jnp.maximum(n, 1).astype(F32)
    large = max_exact + (jnp.log(nf / max_exact) / math.log(MAX_DISTANCE / max_exact)
                         * (N_BUCKETS - max_exact)).astype(jnp.int32)
    large = jnp.minimum(large, N_BUCKETS - 1)
    return jnp.where(n < max_exact, n, large)


def _attn_kernel(rb_ref, q_ref, k_ref, v_ref, lam_ref, gsub_ref, o_ref,
                 bias_ref, m_ref, acc_ref, sa_ref, sb_ref, *, lam_init):
    h = pl.program_id(0)
    b = pl.program_id(1)
    qi = pl.program_id(2)
    bq, bk = ATT_BQ, ATT_BK

    @pl.when(jnp.logical_and(b == 0, qi == 0))
    def _():
        i = lax.broadcasted_iota(jnp.int32, (bq, bk), 0)
        jj = lax.broadcasted_iota(jnp.int32, (bq, bk), 1)
        far = rb_ref[N_BUCKETS - 1, h]
        bias_ref[2] = jnp.zeros((bq, bk), F32)
        for slot in range(2):
            n = i - jj + slot * bk
            bucket = _t5_bucket(jnp.maximum(n, 0))
            bias = jnp.zeros((bq, bk), F32)
            for cc in range(N_BUCKETS):
                bias = jnp.where(bucket == cc, rb_ref[cc, h] - far, bias)
            if slot == 0:
                bias = jnp.where(n >= 0, bias, NEG_BIG)
            bias_ref[slot] = bias

    m_ref[...] = jnp.full(m_ref.shape, NEG_BIG, F32)
    acc_ref[...] = jnp.zeros(acc_ref.shape, F32)

    q = q_ref[...]
    lane = lax.broadcasted_iota(jnp.int32, q.shape, 1)
    zero = jnp.zeros_like(q)
    qs = jnp.concatenate([jnp.where(lane < DH_DIFF, q, zero), jnp.where(lane < DH_DIFF, zero, q)], axis=0)
    ones_col = (lax.broadcasted_iota(jnp.int32, (bk, HEAD_DIM), 1) == 0).astype(BF16)

    def scores(j):
        ks = pl.multiple_of(j * bk, bk)
        return lax.dot_general(qs, k_ref[pl.ds(ks, bk), :], (((1,), (1,)), ((), ())),
                               preferred_element_type=F32)

    def absorb(j, sc_ref):
        ks = pl.multiple_of(j * bk, bk)
        v_ext = jnp.concatenate([v_ref[pl.ds(ks, bk), :], ones_col], axis=1)
        bias = bias_ref[jnp.minimum(qi - j, 2)]
        sc = jnp.concatenate([sc_ref[0:bq, :] + bias, sc_ref[bq:2 * bq, :] + bias], axis=0)
        m_old = m_ref[...]
        m_new = jnp.maximum(m_old, jnp.max(sc, axis=-1, keepdims=True))
        p = jnp.exp(sc - m_new)
        acc_ref[...] = (jnp.exp(m_old - m_new) * acc_ref[...]
                        + jnp.dot(p.astype(BF16), v_ext, preferred_element_type=F32))
        m_ref[...] = m_new

    n_tiles = qi + 1
    sa_ref[...] = scores(0)

    def pair_body(jj, carry):
        j0 = 2 * jj
        sb_ref[...] = scores(j0 + 1)
        absorb(j0, sa_ref)
        sa_ref[...] = scores(jnp.minimum(j0 + 2, qi))
        absorb(j0 + 1, sb_ref)
        return carry

    lax.fori_loop(0, n_tiles // 2, pair_body, 0)

    @pl.when(n_tiles % 2 == 1)
    def _():
        absorb(qi, sa_ref)

    lam_p = lam_ref[...]
    s1 = jnp.sum(lam_p[0:1] * lam_p[1:2], axis=-1, keepdims=True)
    s2 = jnp.sum(lam_p[2:3] * lam_p[3:4], axis=-1, keepdims=True)
    lam = jnp.exp(s1) - jnp.exp(s2) + lam_init
    acc = acc_ref[...]
    num = acc[:, :HEAD_DIM]
    den = acc[:, HEAD_DIM:HEAD_DIM + 1]
    o = num[:bq] / den[:bq] - lam * (num[bq:] / den[bq:])
    o = o * lax.rsqrt(jnp.mean(o * o, axis=-1, keepdims=True) + EPS) * gsub_ref[...]
    o_ref[...] = (o * (1.0 - lam_init)).astype(o_ref.dtype)


def _diff_attention(big, rel_bias, lam_params, g_subln, bsz, seq, n_heads, d_model, lam_init):
    t = bsz * seq
    nq = seq // ATT_BQ
    qcol = 4 * d_model // HEAD_DIM
    per = d_model // HEAD_DIM
    return pl.pallas_call(
        functools.partial(_attn_kernel, lam_init=lam_init),
        grid=(n_heads, bsz, nq),
        in_specs=[
            pl.BlockSpec(memory_space=pltpu.SMEM),
            pl.BlockSpec((ATT_BQ, HEAD_DIM), lambda h, b, i: (b * nq + i, qcol + h)),
            pl.BlockSpec((seq, HEAD_DIM), lambda h, b, i: (b, qcol + per + h)),
            pl.BlockSpec((seq, HEAD_DIM), lambda h, b, i: (b, qcol + 2 * per + h)),
            pl.BlockSpec((4, DH_DIFF), lambda h, b, i: (0, 0)),
            pl.BlockSpec((1, HEAD_DIM), lambda h, b, i: (0, 0)),
        ],
        out_specs=pl.BlockSpec((ATT_BQ, HEAD_DIM), lambda h, b, i: (b * nq + i, h)),
        out_shape=jax.ShapeDtypeStruct((t, d_model), BF16),
        scratch_shapes=[
            pltpu.VMEM((3, ATT_BQ, ATT_BK), F32),
            pltpu.VMEM((2 * ATT_BQ, 1), F32),
            pltpu.VMEM((2 * ATT_BQ, 2 * HEAD_DIM), F32),
            pltpu.VMEM((2 * ATT_BQ, ATT_BK), F32),
            pltpu.VMEM((2 * ATT_BQ, ATT_BK), F32),
        ],
        compiler_params=_cparams(("arbitrary", "arbitrary", "arbitrary")),
        name="diff_attention",
    )(rel_bias, big, big, big, lam_params, g_subln)


def _mix_kernel(ga_ref, gb_ref, oa_ref, od_ref, x_ref, wo_ref, gffn_ref, wr_ref, br_ref,
                x1_ref, h2_ref, topi_ref, topw_ref, rank_ref, cnt_ref, carry_ref):
    i = pl.program_id(0)
    tm = MIX_TM

    @pl.when(i == 0)
    def _():
        carry_ref[...] = jnp.zeros_like(carry_ref)

    mix = (ga_ref[...].astype(F32) * oa_ref[...].astype(F32)
           + gb_ref[...].astype(F32) * od_ref[...].astype(F32))
    x1 = x_ref[...] + jnp.dot(mix.astype(BF16), wo_ref[...], preferred_element_type=F32)
    x1_ref[...] = x1
    h2 = x1 * lax.rsqrt(jnp.mean(x1 * x1, axis=-1, keepdims=True) + EPS) * gffn_ref[...]
    for cb in range(h2.shape[1] // LANES):
        h2_ref[:, cb, :] = h2[:, cb * LANES:(cb + 1) * LANES]

    logits = lax.dot_general(wr_ref[...], h2, (((1,), (1,)), ((), ())),
                             preferred_element_type=F32, precision=lax.Precision.HIGHEST) + br_ref[...]
    eidx = lax.broadcasted_iota(jnp.int32, logits.shape, 0).astype(F32)
    vals, hots = [], []
    cur = logits
    for kk in range(TOP_K):
        mx = jnp.max(cur, axis=0, keepdims=True)
        idx = jnp.min(jnp.where(cur == mx, eidx, float(N_EXPERTS)), axis=0, keepdims=True)
        hot = eidx == idx
        vals.append(mx)
        hots.append(hot)
        topi_ref[kk:kk + 1, :] = idx.astype(jnp.int32)
        cur = jnp.where(hot, -jnp.inf, cur)
    exps = [jnp.exp(vv - vals[0]) for vv in vals]
    denom = exps[0] + exps[1] + exps[2] + exps[3]
    for kk in range(TOP_K):
        topw_ref[kk:kk + 1, :] = exps[kk] / denom

    sel = hots[0]
    for kk in range(1, TOP_K):
        sel = jnp.logical_or(sel, hots[kk])
    sel_f = sel.astype(F32)
    r = lax.broadcasted_iota(jnp.int32, (tm, tm), 0)
    c = lax.broadcasted_iota(jnp.int32, (tm, tm), 1)
    before = _bdot(sel_f, (r < c).astype(F32)) + carry_ref[...]
    for kk in range(TOP_K):
        rank_ref[kk:kk + 1, :] = jnp.sum(jnp.where(hots[kk], before, 0.0), axis=0,
                                         keepdims=True).astype(jnp.int32)
    carry_ref[...] = carry_ref[...] + jnp.sum(sel_f, axis=-1, keepdims=True)
    cnt_ref[...] = carry_ref[...].astype(jnp.int32)


def _mix_project_route(big, oa, od, x2d, w_o, g_ffn, w_r_t, b_r, d_model):
    t = x2d.shape[0]
    tm = MIX_TM
    gate_blk = 7
    full = lambda shape: pl.BlockSpec(shape, lambda i: (0, 0))
    row = lambda: pl.BlockSpec((tm, d_model), lambda i: (i, 0))
    krow = lambda: pl.BlockSpec((TOP_K, tm), lambda i: (0, i))
    return pl.pallas_call(
        _mix_kernel,
        grid=(t // tm,),
        in_specs=[
            pl.BlockSpec((tm, d_model), lambda i: (i, gate_blk)),
            pl.BlockSpec((tm, d_model), lambda i: (i, gate_blk + 1)),
            row(), row(), row(),
            full((d_model, d_model)), full((1, d_model)), full((N_EXPERTS, d_model)), full((N_EXPERTS, 1)),
        ],
        out_specs=[row(), pl.BlockSpec((tm, d_model // LANES, LANES), lambda i: (i, 0, 0)),
                   krow(), krow(), krow(), full((N_EXPERTS, 1))],
        out_shape=[
            jax.ShapeDtypeStruct((t, d_model), F32),
            jax.ShapeDtypeStruct((t, d_model // LANES, LANES), F32),
            jax.ShapeDtypeStruct((TOP_K, t), jnp.int32),
            jax.ShapeDtypeStruct((TOP_K, t), F32),
            jax.ShapeDtypeStruct((TOP_K, t), jnp.int32),
            jax.ShapeDtypeStruct((N_EXPERTS, 1), jnp.int32),
        ],
        scratch_shapes=[pltpu.VMEM((N_EXPERTS, 1), F32)],
        compiler_params=_cparams(("arbitrary",)),
        name="merge_outproj_route",
    )(big, big, oa, od, x2d, w_o, g_ffn, w_r_t, b_r)


def _row_copy(src_ref, src_row, dst_ref, dst_row, sem):
    return pltpu.make_async_copy(src_ref.at[pl.ds(src_row, 1)], dst_ref.at[pl.ds(dst_row, 1)], sem)


def _expert_kernel(be_ref, src_ref, src_next_ref, dst_prev_ref, dst_ref, h2_ref, wup_ref, bup_ref,
                   wdn_ref, bdn_ref, y_ref, x0, x1, y0, y1, wup_bf, wdn_bf, gsem, ssem):
    i = pl.program_id(0)
    last = pl.num_programs(0) - 1
    rb = MOE_RB
    d_ff = wdn_ref.shape[1]

    def gather_wait(x_ref, s):
        pltpu.make_async_copy(h2_ref.at[pl.ds(0, rb)], x_ref, gsem.at[s]).wait()

    def scatter_wait(yb_ref, s):
        pltpu.make_async_copy(yb_ref, y_ref.at[pl.ds(0, rb)], ssem.at[s]).wait()

    @pl.when(i == 0)
    def _():
        for a in range(rb):
            _row_copy(h2_ref, src_ref[a], x0, a, gsem.at[0]).start()

    @pl.when(jnp.logical_or(i == 0, be_ref[i] != be_ref[jnp.maximum(i - 1, 0)]))
    def _():
        rr = lax.broadcasted_iota(jnp.int32, (2 * LANES, 2 * LANES), 0)
        cc = lax.broadcasted_iota(jnp.int32, (2 * LANES, 2 * LANES), 1)
        pick = jnp.where(cc < LANES, 2 * cc, 2 * (cc - LANES) + 1)
        perm = (rr == pick).astype(BF16)
        for g in range(wup_ref.shape[2] // (2 * LANES)):
            cs = slice(g * 2 * LANES, (g + 1) * 2 * LANES)
            wup_bf[:, cs] = jnp.dot(wup_ref[0, :, cs].astype(BF16), perm,
                                    preferred_element_type=F32).astype(BF16)
        wdn_bf[...] = wdn_ref[0].astype(BF16)

    @pl.when(i == 0)
    def _():
        y1[...] = jnp.zeros(y1.shape, F32)

    def block(xa, ya, xb, yb, s):
        gather_wait(xa, s)

        @pl.when(i >= 1)
        def _():
            scatter_wait(ya, s)

        for a in range(rb):
            _row_copy(h2_ref, src_next_ref[a], xb, a, gsem.at[1 - s]).start()
            _row_copy(yb, a, y_ref, dst_prev_ref[a], ssem.at[1 - s]).start()
        n_cb = xa.shape[1]
        x = jnp.concatenate([xa[:, cb, :] for cb in range(n_cb)], axis=1)
        hid = jnp.dot(x.astype(BF16), wup_bf[...], preferred_element_type=F32) + bup_ref[0]
        acts = []
        for g in range(hid.shape[1] // (2 * LANES)):
            glu = jnp.minimum(hid[:, g * 2 * LANES:g * 2 * LANES + LANES], SWIGLU_LIMIT)
            lin = jnp.clip(hid[:, g * 2 * LANES + LANES:(g + 1) * 2 * LANES], -SWIGLU_LIMIT, SWIGLU_LIMIT)
            acts.append(glu * jax.nn.sigmoid(SWIGLU_ALPHA * glu) * (lin + 1.0))
        act = jnp.concatenate(acts, axis=1)
        assert act.shape[1] == d_ff
        y = jnp.dot(act.astype(BF16), wdn_bf[...], preferred_element_type=F32) + bdn_ref[0]
        for cb in range(n_cb):
            ya[:, cb, :] = y[:, cb * LANES:(cb + 1) * LANES]

        @pl.when(i == last)
        def _():
            for a in range(rb):
                _row_copy(ya, a, y_ref, dst_ref[a], ssem.at[s]).start()
            scatter_wait(yb, 1 - s)
            scatter_wait(ya, s)
            gather_wait(xb, 1 - s)

    @pl.when((i & 1) == 1)
    def _():
        block(x1, y1, x0, y0, 1)

    @pl.when((i & 1) == 0)
    def _():
        block(x0, y0, x1, y1, 0)


def _experts(block_e, src_tok, dst_row, h2, w_up, b_up, w_down, b_down):
    n_rows = src_tok.shape[0]
    d = w_up.shape[1]
    nb = n_rows // MOE_RB
    two_ff = w_up.shape[2]
    d_ff = w_down.shape[1]
    dst_ext = jnp.concatenate([n_rows + jnp.arange(MOE_RB, dtype=jnp.int32), dst_row])
    smem_rows = lambda fn: pl.BlockSpec((MOE_RB,), fn, memory_space=pltpu.SMEM)
    grid_spec = pltpu.PrefetchScalarGridSpec(
        num_scalar_prefetch=1,
        grid=(nb,),
        in_specs=[
            smem_rows(lambda i, be: (i,)),
            smem_rows(lambda i, be: (jnp.minimum(i + 1, nb - 1),)),
            smem_rows(lambda i, be: (i,)),
            smem_rows(lambda i, be: (i + 1,)),
            pl.BlockSpec(memory_space=pl.ANY),
            pl.BlockSpec((1, d, two_ff), lambda i, be: (be[i], 0, 0)),
            pl.BlockSpec((1, 1, two_ff), lambda i, be: (be[i], 0, 0)),
            pl.BlockSpec((1, d_ff, d), lambda i, be: (be[i], 0, 0)),
            pl.BlockSpec((1, 1, d), lambda i, be: (be[i], 0, 0)),
        ],
        out_specs=pl.BlockSpec(memory_space=pl.ANY),
        scratch_shapes=[
            pltpu.VMEM((MOE_RB,) + h2.shape[1:], F32), pltpu.VMEM((MOE_RB,) + h2.shape[1:], F32),
            pltpu.VMEM((MOE_RB,) + h2.shape[1:], F32), pltpu.VMEM((MOE_RB,) + h2.shape[1:], F32),
            pltpu.VMEM((d, two_ff), BF16),
            pltpu.VMEM((d_ff, d), BF16),
            pltpu.SemaphoreType.DMA((2,)),
            pltpu.SemaphoreType.DMA((2,)),
        ],
    )
    return pl.pallas_call(
        _expert_kernel,
        grid_spec=grid_spec,
        out_shape=jax.ShapeDtypeStruct((n_rows + MOE_RB,) + h2.shape[1:], F32),
        compiler_params=_cparams(("arbitrary",)),
        name="moe_experts",
    )(block_e, src_tok, src_tok, dst_ext, dst_ext, h2, w_up, b_up, w_down, b_down)


def _combine_kernel(x1_ref, w_ref, y0_ref, y1_ref, y2_ref, y3_ref, o_ref):
    w = w_ref[...]
    for cb in range(y0_ref.shape[1]):
        cs = slice(cb * LANES, (cb + 1) * LANES)
        out = x1_ref[:, cs]
        for kk, y_ref in enumerate((y0_ref, y1_ref, y2_ref, y3_ref)):
            out = out + w[:, kk:kk + 1] * y_ref[:, cb, :]
        o_ref[:, cs] = out


def _combine(x1, w_tok, y_slots):
    t, d = x1.shape
    tc = COMB_TC
    nt = t // tc
    yspec = lambda kk: pl.BlockSpec((tc,) + y_slots.shape[1:], lambda i: (kk * nt + i, 0, 0))
    return pl.pallas_call(
        _combine_kernel,
        grid=(nt,),
        in_specs=[
            pl.BlockSpec((tc, d), lambda i: (i, 0)),
            pl.BlockSpec((tc, TOP_K), lambda i: (i, 0)),
            yspec(0), yspec(1), yspec(2), yspec(3),
        ],
        out_specs=pl.BlockSpec((tc, d), lambda i: (i, 0)),
        out_shape=jax.ShapeDtypeStruct((t, d), F32),
        compiler_params=_cparams(("parallel",)),
        name="moe_combine",
    )(x1, w_tok, y_slots, y_slots, y_slots, y_slots)


def _moe(x1, h2, topi, topw, rank, counts, w_up, b_up, w_down, b_down):
    t, d = x1.shape
    n_assign = t * TOP_K
    nb = -(-n_assign // MOE_RB) + N_EXPERTS
    n_rows = nb * MOE_RB
    counts = counts[:, 0]
    padded = (counts + MOE_RB - 1) // MOE_RB * MOE_RB
    padded_end = jnp.cumsum(padded)
    padded_start = padded_end - padded
    expert_ids = jnp.arange(N_EXPERTS, dtype=jnp.int32)[:, None, None]
    start_of = jnp.sum(jnp.where(topi[None] == expert_ids, padded_start[:, None, None], 0), axis=0)
    dest = (start_of + rank).astype(jnp.int32)
    n_used = (padded_end[-1] // MOE_RB).astype(jnp.int32)
    blk = jnp.minimum(jnp.arange(nb, dtype=jnp.int32), n_used - 1)
    block_e = jnp.minimum(jnp.sum(padded_end[None, :] <= (blk * MOE_RB)[:, None], axis=1),
                          N_EXPERTS - 1).astype(jnp.int32)
    slot_of = jnp.full((n_rows,), -1, jnp.int32).at[dest.reshape(-1)].set(
        jnp.arange(n_assign, dtype=jnp.int32), unique_indices=True)
    is_pad = slot_of < 0
    src_tok = jnp.where(is_pad, 0, slot_of % t)
    dst_row = jnp.where(is_pad, n_assign + jnp.cumsum(is_pad.astype(jnp.int32)) - 1, slot_of)

    y_slots = _experts(block_e, src_tok, dst_row, h2, w_up, b_up, w_down, b_down)
    return _combine(x1, topw.T, y_slots)


def kernel(x, g_mix, w_in, b_gate, conv_w, a_log, dt_bias, g_delta_out, q_norm, k_norm, lambda_q1, lambda_k1, lambda_q2, lambda_k2, g_subln, rel_bias, w_o, g_ffn, w_router, b_router, w_up, b_up, w_down, b_down):
    bsz, seq, d = x.shape
    depth = g_mix.shape[0]
    n_heads = d // HEAD_DIM
    t = bsz * seq
    d_ff = w_down.shape[2]
    assert d % PROJ_TN == 0 and t % PROJ_TM == 0 and seq % GDN_TB == 0 and seq % ATT_BQ == 0
    assert t % MIX_TM == 0 and t % COMB_TC == 0 and n_heads % GDN_HG == 0
    assert (t * TOP_K) % MOE_RB == 0
    assert 2 * n_heads <= 2 * SUBLANES

    x2d = x.reshape(t, d)
    for l in range(depth):
        wl = w_in[l]
        c0 = 4 * d
        c1 = c0 + 2 * n_heads
        w_cat = jnp.concatenate([wl[:, :c0], wl[:, c1:]], axis=1).astype(BF16)
        w_small = jnp.pad(wl[:, c0:c1], ((0, 0), (0, LANES - 2 * n_heads)))
        qk_gain = jnp.concatenate([jnp.tile(q_norm[l] * (DH_DIFF ** -0.5), 2 * n_heads),
                                   jnp.tile(k_norm[l], 2 * n_heads)])
        aux = jnp.concatenate([jnp.zeros((4 * d,), F32), qk_gain, jnp.zeros((d,), F32),
                               b_gate[l].reshape(-1)]).reshape(1, -1)
        big = _input_projection(x2d, g_mix[l].reshape(1, d), w_cat, aux,
                                n_plain_a=4 * d // PROJ_TN, n_norm=2 * d // PROJ_TN,
                                n_plain_b=d // PROJ_TN)

        head_pad = jnp.zeros((LANES - 2 * n_heads,), F32)
        alog = jnp.concatenate([jnp.zeros((n_heads,), F32), a_log[l], head_pad])
        dtb = jnp.concatenate([jnp.zeros((n_heads,), F32), dt_bias[l], head_pad])
        rows_t = 2 * n_heads
        small, small_t = _small_projection(
            x2d, g_mix[l].reshape(1, d), w_small.astype(BF16), w_small[:, :rows_t].T.astype(BF16),
            alog.reshape(1, LANES), dtb.reshape(1, LANES),
            alog[:rows_t].reshape(rows_t, 1), dtb[:rows_t].reshape(rows_t, 1), n_heads)

        oa = _gated_delta(big, small, small_t, conv_w[l], g_delta_out[l].reshape(1, HEAD_DIM),
                          bsz, seq, n_heads, d)

        lam_init = 0.8 - 0.6 * math.exp(-0.3 * l)
        lam_params = jnp.stack([lambda_q1[l], lambda_k1[l], lambda_q2[l], lambda_k2[l]])
        od = _diff_attention(big, rel_bias, lam_params, g_subln[l].reshape(1, HEAD_DIM),
                             bsz, seq, n_heads, d, lam_init)

        x1, h2, topi, topw, rank, counts = _mix_project_route(
            big, oa, od, x2d, w_o[l].astype(BF16), g_ffn[l].reshape(1, d),
            w_router[l].T, b_router[l].reshape(N_EXPERTS, 1), d)

        b_up_l = b_up[l].reshape(N_EXPERTS, 2 * d_ff // (2 * LANES), LANES, 2)
        b_up_l = jnp.swapaxes(b_up_l, 2, 3).reshape(N_EXPERTS, 1, 2 * d_ff)
        x2d = _moe(x1, h2, topi, topw, rank, counts, w_up[l], b_up_l,
                   w_down[l], b_down[l].reshape(N_EXPERTS, 1, d))
    return x2d.reshape(bsz, seq, d)
```

```python
import functools
import math

import jax
import jax.numpy as jnp
from jax import lax
from jax.experimental import pallas as pl
from jax.experimental.pallas import tpu as pltpu

F32 = jnp.float32
BF16 = jnp.bfloat16

HEAD_DIM = 128
DH_DIFF = HEAD_DIM // 2
CONV_WIDTH = 4
CHUNK = 64
N_BUCKETS = 32
MAX_DISTANCE = 128
N_EXPERTS = 32
TOP_K = 4
TOP_K_SHIFT = 2
SWIGLU_LIMIT = 7.0
SWIGLU_ALPHA = 1.702
EPS = 1e-6
NEG_BIG = -1e30

LANES = 128
SUBLANES = 8
VMEM_LIMIT = 56 * 1024 * 1024

PROJ_TM = 2048
PROJ_TN = 1024
PROJ_CHUNK = 256
GDN_TB = 256
GDN_HG = 4
ATT_BQ = 512
ATT_BK = 512
MIX_TM = 512
MOE_RB = 256
COMB_TC = 512


def _cparams(sem):
    return pltpu.CompilerParams(dimension_semantics=sem, vmem_limit_bytes=VMEM_LIMIT)


def _bdot(a, b):
    return jnp.dot(a.astype(BF16), b.astype(BF16), preferred_element_type=F32)


def _bdot_nt(a, b):
    return lax.dot_general(a.astype(BF16), b.astype(BF16), (((1,), (1,)), ((), ())),
                           preferred_element_type=F32)


def _bdot_tn(a, b):
    return lax.dot_general(a.astype(BF16), b.astype(BF16), (((0,), (0,)), ((), ())),
                           preferred_element_type=F32)


def _fdot(a, b):
    return jnp.dot(a, b, preferred_element_type=F32, precision=lax.Precision.HIGHEST)


def _proj_kernel(x_ref, g_ref, w_ref, aux_ref, o_ref, h_ref, *, mode):
    @pl.when(pl.program_id(1) == 0)
    def _():
        x = x_ref[...]
        ms = jnp.mean(x * x, axis=-1, keepdims=True)
        h_ref[...] = (x * lax.rsqrt(ms + EPS) * g_ref[...]).astype(BF16)

    h = h_ref[...]
    lo = lax.broadcasted_iota(jnp.int32, (1, LANES), 1) < DH_DIFF
    for c in range(PROJ_TN // PROJ_CHUNK):
        cs = slice(c * PROJ_CHUNK, (c + 1) * PROJ_CHUNK)
        acc = jnp.dot(h, w_ref[:, cs], preferred_element_type=F32)
        if mode == "plain":
            o_ref[:, cs] = acc.astype(o_ref.dtype)
        elif mode == "gate":
            o_ref[:, cs] = jax.nn.sigmoid(acc + aux_ref[:, cs]).astype(o_ref.dtype)
        else:
            for g in range(PROJ_CHUNK // LANES):
                sl = slice(c * PROJ_CHUNK + g * LANES, c * PROJ_CHUNK + (g + 1) * LANES)
                y = acc[:, g * LANES:(g + 1) * LANES]
                y2 = y * y
                s_lo = jnp.sum(jnp.where(lo, y2, 0.0), axis=-1, keepdims=True)
                s_hi = jnp.sum(jnp.where(lo, 0.0, y2), axis=-1, keepdims=True)
                r = jnp.where(lo, lax.rsqrt(s_lo / DH_DIFF + EPS), lax.rsqrt(s_hi / DH_DIFF + EPS))
                o_ref[:, sl] = (y * r * aux_ref[:, sl]).astype(o_ref.dtype)


def _input_projection(x2d, g_mix, w, aux, mode):
    t, d = x2d.shape
    n = w.shape[1]
    return pl.pallas_call(
        functools.partial(_proj_kernel, mode=mode),
        grid=(t // PROJ_TM, n // PROJ_TN),
        in_specs=[
            pl.BlockSpec((PROJ_TM, d), lambda i, j: (i, 0)),
            pl.BlockSpec((1, d), lambda i, j: (0, 0)),
            pl.BlockSpec((d, PROJ_TN), lambda i, j: (0, j)),
            pl.BlockSpec((1, PROJ_TN), lambda i, j: (0, j)),
        ],
        out_specs=pl.BlockSpec((PROJ_TM, PROJ_TN), lambda i, j: (i, j)),
        out_shape=jax.ShapeDtypeStruct((t, n), BF16),
        scratch_shapes=[pltpu.VMEM((PROJ_TM, d), BF16)],
        compiler_params=_cparams(("parallel", "arbitrary")),
        name="input_projection_" + mode,
    )(x2d, g_mix, w, aux)


def _small_proj_kernel(x_ref, g_ref, w_ref, wt_ref, alog_ref, dtb_ref, alog_t_ref, dtb_t_ref,
                       o_ref, ot_ref, *, n_heads):
    x = x_ref[...]
    ms = jnp.mean(x * x, axis=-1, keepdims=True)
    h = (x * lax.rsqrt(ms + EPS) * g_ref[...]).astype(BF16)

    def finish(acc, idx, alog, dtb):
        beta = jax.nn.sigmoid(acc)
        z = acc + dtb
        softplus = jnp.maximum(z, 0.0) + jnp.log1p(jnp.exp(-jnp.abs(z)))
        gdec = -jnp.exp(alog) * softplus
        return jnp.where(idx < n_heads, beta, jnp.where(idx < 2 * n_heads, gdec, 0.0))

    acc = jnp.dot(h, w_ref[...], preferred_element_type=F32)
    lane = lax.broadcasted_iota(jnp.int32, acc.shape, 1)
    o_ref[...] = finish(acc, lane, alog_ref[...], dtb_ref[...])
    acc_t = lax.dot_general(wt_ref[...], h, (((1,), (1,)), ((), ())),
                            preferred_element_type=F32)
    sub = lax.broadcasted_iota(jnp.int32, acc_t.shape, 0)
    ot_ref[...] = finish(acc_t, sub, alog_t_ref[...], dtb_t_ref[...])


def _small_projection(x2d, g_mix, w_small, w_small_t, alog, dtb, alog_t, dtb_t, n_heads):
    t, d = x2d.shape
    rows_t = w_small_t.shape[0]
    tm = PROJ_TM
    full = lambda shape: pl.BlockSpec(shape, lambda i: (0, 0))
    return pl.pallas_call(
        functools.partial(_small_proj_kernel, n_heads=n_heads),
        grid=(t // tm,),
        in_specs=[
            pl.BlockSpec((tm, d), lambda i: (i, 0)),
            full((1, d)), full((d, LANES)), full((rows_t, d)),
            full((1, LANES)), full((1, LANES)), full((rows_t, 1)), full((rows_t, 1)),
        ],
        out_specs=[pl.BlockSpec((tm, LANES), lambda i: (i, 0)),
                   pl.BlockSpec((rows_t, tm), lambda i: (0, i))],
        out_shape=[jax.ShapeDtypeStruct((t, LANES), F32),
                   jax.ShapeDtypeStruct((rows_t, t), F32)],
        compiler_params=_cparams(("parallel",)),
        name="beta_decay_projection",
    )(x2d, g_mix, w_small, w_small_t, alog, dtb, alog_t, dtb_t)


def _gdn_kernel(q_ref, k_ref, v_ref, z_ref, sm_ref, smt_ref, cwq_ref, cwk_ref, cwv_ref, gout_ref,
                o_ref, state_ref, qp_ref, kp_ref, vp_ref, vn_ref, *, n_heads):
    hg = pl.program_id(1)
    s = pl.program_id(2)
    tb = GDN_TB
    pad = SUBLANES
    width = GDN_HG * HEAD_DIM

    @pl.when(s == 0)
    def _():
        state_ref[...] = jnp.zeros_like(state_ref)
        for p_ref in (qp_ref, kp_ref, vp_ref):
            p_ref[0:pad, :] = jnp.zeros((pad, width), F32)

    def conv_silu(x_ref, p_ref, cw_ref):
        p_ref[pad:pad + tb, :] = x_ref[...].astype(F32)
        acc = cw_ref[CONV_WIDTH - 1:CONV_WIDTH, :] * p_ref[pad:pad + tb, :]
        for jj in range(CONV_WIDTH - 1):
            off = pad - (CONV_WIDTH - 1) + jj
            acc = acc + cw_ref[jj:jj + 1, :] * p_ref[off:off + tb, :]
        p_ref[0:pad, :] = p_ref[tb:tb + pad, :]
        return acc * jax.nn.sigmoid(acc)

    q_all = conv_silu(q_ref, qp_ref, cwq_ref)
    k_all = conv_silu(k_ref, kp_ref, cwk_ref)
    v_all = conv_silu(v_ref, vp_ref, cwv_ref)

    r = lax.broadcasted_iota(jnp.int32, (tb, tb), 0)
    c = lax.broadcasted_iota(jnp.int32, (tb, tb), 1)
    shift = int(math.log2(CHUNK))
    same = (r >> shift) == (c >> shift)
    incl = jnp.logical_and(same, c <= r)
    strict = jnp.logical_and(same, c < r)
    eye = (r == c).astype(F32)

    small = sm_ref[...]
    small_t = smt_ref[...]
    lane = lax.broadcasted_iota(jnp.int32, small.shape, 1)
    sub = lax.broadcasted_iota(jnp.int32, small_t.shape, 0)
    gcum = _fdot(incl.astype(F32), small)
    gtot = _fdot(same.astype(F32), small)
    gcum_t = _fdot(small_t, jnp.logical_and(same, r <= c).astype(F32))

    heads = range(GDN_HG)
    hsl = [slice(hh * HEAD_DIM, (hh + 1) * HEAD_DIM) for hh in heads]
    qs = [q_all[:, hs] for hs in hsl]
    ks = [k_all[:, hs] for hs in hsl]
    vs = [v_all[:, hs] for hs in hsl]
    qs = [q * lax.rsqrt(jnp.sum(q * q, axis=-1, keepdims=True) + EPS) * (HEAD_DIM ** -0.5) for q in qs]
    ks = [k * lax.rsqrt(jnp.sum(k * k, axis=-1, keepdims=True) + EPS) for k in ks]

    def col_of(arr, idx):
        return jnp.sum(jnp.where(lane == idx, arr, 0.0), axis=-1, keepdims=True)

    head_ids = [hg * GDN_HG + hh for hh in heads]
    betas = [col_of(small, hd) for hd in head_ids]
    gcs = [col_of(gcum, hd + n_heads) for hd in head_ids]
    gls = [col_of(gtot, hd + n_heads) for hd in head_ids]
    gc_rows = [jnp.sum(jnp.where(sub == hd + n_heads, gcum_t, 0.0), axis=0, keepdims=True)
               for hd in head_ids]

    decays = [jnp.where(incl, jnp.exp(jnp.minimum(gc - gr, 0.0)), 0.0) for gc, gr in zip(gcs, gc_rows)]
    kbs = [k * b for k, b in zip(ks, betas)]
    kks = [_bdot_nt(kb, k) for kb, k in zip(kbs, ks)]
    pws = [jnp.where(strict, -(kk * dc), 0.0) for kk, dc in zip(kks, decays)]
    tmats = [eye + pw for pw in pws]
    for _ in range(int(math.log2(CHUNK)) - 1):
        pws = [_bdot(pw, pw) for pw in pws]
        tmats = [tm + _bdot(tm, pw) for tm, pw in zip(tmats, pws)]
    egcs = [jnp.exp(gc) for gc in gcs]
    uws = [_bdot(tm, jnp.concatenate([v * b, kb * eg], axis=1))
           for tm, v, b, kb, eg in zip(tmats, vs, betas, kbs, egcs)]
    us = [uw[:, :HEAD_DIM] for uw in uws]
    ws = [uw[:, HEAD_DIM:] for uw in uws]
    qkm = [_bdot_nt(q, k) for q, k in zip(qs, ks)]
    qkm = [jnp.where(incl, x * dc, 0.0) for x, dc in zip(qkm, decays)]
    q_decs = [q * eg for q, eg in zip(qs, egcs)]
    k_ends = [k * jnp.exp(gl - gc) for k, gl, gc in zip(ks, gls, gcs)]

    for hh in heads:
        vn_ref[hh] = jnp.zeros((tb, HEAD_DIM), F32)
    outs = [[] for _ in heads]
    for ci in range(tb // CHUNK):
        cs = slice(ci * CHUNK, (ci + 1) * CHUNK)
        sts = [state_ref[hh] for hh in heads]
        ws_qs = [_bdot(jnp.concatenate([ws[hh][cs], q_decs[hh][cs]], axis=0), sts[hh]) for hh in heads]
        v_news = [us[hh][cs] - ws_qs[hh][:CHUNK] for hh in heads]
        for hh in heads:
            vn_ref[hh, cs, :] = v_news[hh]
        intra = [_bdot(qkm[hh][cs], vn_ref[hh]) for hh in heads]
        upd = [_bdot_tn(k_ends[hh][cs], v_news[hh]) for hh in heads]
        for hh in heads:
            outs[hh].append(ws_qs[hh][CHUNK:] + intra[hh])
            g_last = gls[hh][ci * CHUNK:ci * CHUNK + 1, :]
            state_ref[hh] = sts[hh] * jnp.exp(g_last) + upd[hh]
    for hh in heads:
        o = jnp.concatenate(outs[hh], axis=0)
        o = o * lax.rsqrt(jnp.mean(o * o, axis=-1, keepdims=True) + EPS) * gout_ref[...]
        zz = z_ref[:, hsl[hh]].astype(F32)
        o_ref[:, hsl[hh]] = (o * (zz * jax.nn.sigmoid(zz))).astype(o_ref.dtype)


def _gated_delta(big, small, small_t, conv_w, g_out, bsz, seq, n_heads, d_model):
    t = bsz * seq
    tb = GDN_TB
    ns = seq // tb
    width = GDN_HG * HEAD_DIM
    nhg = n_heads // GDN_HG
    blocks_per_group = d_model // width
    rows_t = small_t.shape[0]

    def colspec(group):
        return pl.BlockSpec((tb, width), lambda b, h, s: (b * ns + s, group * blocks_per_group + h))

    def cwspec(group):
        return pl.BlockSpec((CONV_WIDTH, width), lambda b, h, s: (0, group * blocks_per_group + h))

    return pl.pallas_call(
        functools.partial(_gdn_kernel, n_heads=n_heads),
        grid=(bsz, nhg, ns),
        in_specs=[
            colspec(0), colspec(1), colspec(2), colspec(3),
            pl.BlockSpec((tb, LANES), lambda b, h, s: (b * ns + s, 0)),
            pl.BlockSpec((rows_t, tb), lambda b, h, s: (0, b * ns + s)),
            cwspec(0), cwspec(1), cwspec(2),
            pl.BlockSpec((1, HEAD_DIM), lambda b, h, s: (0, 0)),
        ],
        out_specs=pl.BlockSpec((tb, width), lambda b, h, s: (b * ns + s, h)),
        out_shape=jax.ShapeDtypeStruct((t, d_model), BF16),
        scratch_shapes=[
            pltpu.VMEM((GDN_HG, HEAD_DIM, HEAD_DIM), F32),
            pltpu.VMEM((tb + SUBLANES, width), F32),
            pltpu.VMEM((tb + SUBLANES, width), F32),
            pltpu.VMEM((tb + SUBLANES, width), F32),
            pltpu.VMEM((GDN_HG, tb, HEAD_DIM), F32),
        ],
        compiler_params=_cparams(("parallel", "parallel", "arbitrary")),
        name="gated_delta",
    )(big, big, big, big, small, small_t, conv_w, conv_w, conv_w, g_out)


def _t5_bucket(n):
    max_exact = N_BUCKETS // 2
    nf = jnp.maximum(n, 1).astype(F32)
    large = max_exact + (jnp.log(nf / max_exact) / math.log(MAX_DISTANCE / max_exact)
                         * (N_BUCKETS - max_exact)).astype(jnp.int32)
    large = jnp.minimum(large, N_BUCKETS - 1)
    return jnp.where(n < max_exact, n, large)


def _attn_kernel(rb_ref, q_ref, k_ref, v_ref, lam_ref, gsub_ref, o_ref,
                 bias_ref, m_ref, acc_ref, sa_ref, sb_ref, *, lam_init):
    h = pl.program_id(0)
    b = pl.program_id(1)
    qi = pl.program_id(2)
    bq, bk = ATT_BQ, ATT_BK

    @pl.when(jnp.logical_and(b == 0, qi == 0))
    def _():
        i = lax.broadcasted_iota(jnp.int32, (bq, bk), 0)
        jj = lax.broadcasted_iota(jnp.int32, (bq, bk), 1)
        far = rb_ref[N_BUCKETS - 1, h]
        bias_ref[2] = jnp.zeros((bq, bk), F32)
        for slot in range(2):
            n = i - jj + slot * bk
            bucket = _t5_bucket(jnp.maximum(n, 0))
            bias = jnp.zeros((bq, bk), F32)
            for cc in range(N_BUCKETS):
                bias = jnp.where(bucket == cc, rb_ref[cc, h] - far, bias)
            if slot == 0:
                bias = jnp.where(n >= 0, bias, NEG_BIG)
            bias_ref[slot] = bias

    m_ref[...] = jnp.full(m_ref.shape, NEG_BIG, F32)
    acc_ref[...] = jnp.zeros(acc_ref.shape, F32)

    q = q_ref[...]
    lane = lax.broadcasted_iota(jnp.int32, q.shape, 1)
    zero = jnp.zeros_like(q)
    qs = jnp.concatenate([jnp.where(lane < DH_DIFF, q, zero), jnp.where(lane < DH_DIFF, zero, q)], axis=0)
    ones_col = (lax.broadcasted_iota(jnp.int32, (bk, HEAD_DIM), 1) == 0).astype(BF16)

    def scores(j):
        ks = pl.multiple_of(j * bk, bk)
        return lax.dot_general(qs, k_ref[pl.ds(ks, bk), :], (((1,), (1,)), ((), ())),
                               preferred_element_type=F32)

    def absorb(j, sc_ref):
        ks = pl.multiple_of(j * bk, bk)
        v_ext = jnp.concatenate([v_ref[pl.ds(ks, bk), :], ones_col], axis=1)
        bias = bias_ref[jnp.minimum(qi - j, 2)]
        sc = jnp.concatenate([sc_ref[0:bq, :] + bias, sc_ref[bq:2 * bq, :] + bias], axis=0)
        m_old = m_ref[...]
        m_new = jnp.maximum(m_old, jnp.max(sc, axis=-1, keepdims=True))
        p = jnp.exp(sc - m_new)
        acc_ref[...] = (jnp.exp(m_old - m_new) * acc_ref[...]
                        + jnp.dot(p.astype(BF16), v_ext, preferred_element_type=F32))
        m_ref[...] = m_new

    n_tiles = qi + 1
    sa_ref[...] = scores(0)

    def pair_body(jj, carry):
        j0 = 2 * jj
        sb_ref[...] = scores(j0 + 1)
        absorb(j0, sa_ref)
        sa_ref[...] = scores(jnp.minimum(j0 + 2, qi))
        absorb(j0 + 1, sb_ref)
        return carry

    lax.fori_loop(0, n_tiles // 2, pair_body, 0)

    @pl.when(n_tiles % 2 == 1)
    def _():
        absorb(qi, sa_ref)

    lam_p = lam_ref[...]
    s1 = jnp.sum(lam_p[0:1] * lam_p[1:2], axis=-1, keepdims=True)
    s2 = jnp.sum(lam_p[2:3] * lam_p[3:4], axis=-1, keepdims=True)
    lam = jnp.exp(s1) - jnp.exp(s2) + lam_init
    acc = acc_ref[...]
    num = acc[:, :HEAD_DIM]
    den = acc[:, HEAD_DIM:HEAD_DIM + 1]
    o = num[:bq] / den[:bq] - lam * (num[bq:] / den[bq:])
    o = o * lax.rsqrt(jnp.mean(o * o, axis=-1, keepdims=True) + EPS) * gsub_ref[...]
    o_ref[...] = (o * (1.0 - lam_init)).astype(o_ref.dtype)


def _diff_attention(proj_qk, proj_plain, rel_bias, lam_params, g_subln, bsz, seq, n_heads, d_model,
                    lam_init):
    t = bsz * seq
    nq = seq // ATT_BQ
    per = d_model // HEAD_DIM
    vcol = 4 * per
    return pl.pallas_call(
        functools.partial(_attn_kernel, lam_init=lam_init),
        grid=(n_heads, bsz, nq),
        in_specs=[
            pl.BlockSpec(memory_space=pltpu.SMEM),
            pl.BlockSpec((ATT_BQ, HEAD_DIM), lambda h, b, i: (b * nq + i, h)),
            pl.BlockSpec((seq, HEAD_DIM), lambda h, b, i: (b, per + h)),
            pl.BlockSpec((seq, HEAD_DIM), lambda h, b, i: (b, vcol + h)),
            pl.BlockSpec((4, DH_DIFF), lambda h, b, i: (0, 0)),
            pl.BlockSpec((1, HEAD_DIM), lambda h, b, i: (0, 0)),
        ],
        out_specs=pl.BlockSpec((ATT_BQ, HEAD_DIM), lambda h, b, i: (b * nq + i, h)),
        out_shape=jax.ShapeDtypeStruct((t, d_model), BF16),
        scratch_shapes=[
            pltpu.VMEM((3, ATT_BQ, ATT_BK), F32),
            pltpu.VMEM((2 * ATT_BQ, 1), F32),
            pltpu.VMEM((2 * ATT_BQ, 2 * HEAD_DIM), F32),
            pltpu.VMEM((2 * ATT_BQ, ATT_BK), F32),
            pltpu.VMEM((2 * ATT_BQ, ATT_BK), F32),
        ],
        compiler_params=_cparams(("arbitrary", "arbitrary", "arbitrary")),
        name="diff_attention",
    )(rel_bias, proj_qk, proj_qk, proj_plain, lam_params, g_subln)


def _mix_kernel(ga_ref, gb_ref, oa_ref, od_ref, x_ref, wo_ref, gffn_ref, wr_ref, br_ref,
                x1_ref, h2_ref, topi_ref, topw_ref, rank_ref, cnt_ref, carry_ref):
    i = pl.program_id(0)
    tm = MIX_TM

    @pl.when(i == 0)
    def _():
        carry_ref[...] = jnp.zeros_like(carry_ref)

    mix = (ga_ref[...].astype(F32) * oa_ref[...].astype(F32)
           + gb_ref[...].astype(F32) * od_ref[...].astype(F32))
    x1 = x_ref[...] + jnp.dot(mix.astype(BF16), wo_ref[...], preferred_element_type=F32)
    x1_ref[...] = x1
    h2 = x1 * lax.rsqrt(jnp.mean(x1 * x1, axis=-1, keepdims=True) + EPS) * gffn_ref[...]
    for cb in range(h2.shape[1] // LANES):
        h2_ref[:, cb, :] = h2[:, cb * LANES:(cb + 1) * LANES]

    logits = lax.dot_general(wr_ref[...], h2, (((1,), (1,)), ((), ())),
                             preferred_element_type=F32, precision=lax.Precision.HIGHEST) + br_ref[...]
    eidx = lax.broadcasted_iota(jnp.int32, logits.shape, 0).astype(F32)
    vals, hots = [], []
    cur = logits
    for kk in range(TOP_K):
        mx = jnp.max(cur, axis=0, keepdims=True)
        idx = jnp.min(jnp.where(cur == mx, eidx, float(N_EXPERTS)), axis=0, keepdims=True)
        hot = eidx == idx
        vals.append(mx)
        hots.append(hot)
        topi_ref[kk:kk + 1, :] = idx.astype(jnp.int32)
        cur = jnp.where(hot, -jnp.inf, cur)
    exps = [jnp.exp(vv - vals[0]) for vv in vals]
    denom = exps[0] + exps[1] + exps[2] + exps[3]
    for kk in range(TOP_K):
        topw_ref[kk:kk + 1, :] = exps[kk] / denom

    sel = hots[0]
    for kk in range(1, TOP_K):
        sel = jnp.logical_or(sel, hots[kk])
    sel_f = sel.astype(F32)
    r = lax.broadcasted_iota(jnp.int32, (tm, tm), 0)
    c = lax.broadcasted_iota(jnp.int32, (tm, tm), 1)
    before = _bdot(sel_f, (r < c).astype(F32)) + carry_ref[...]
    for kk in range(TOP_K):
        rank_ref[kk:kk + 1, :] = jnp.sum(jnp.where(hots[kk], before, 0.0), axis=0,
                                         keepdims=True).astype(jnp.int32)
    carry_ref[...] = carry_ref[...] + jnp.sum(sel_f, axis=-1, keepdims=True)
    cnt_ref[...] = carry_ref[...].astype(jnp.int32)


def _mix_project_route(proj_gate, oa, od, x2d, w_o, g_ffn, w_r_t, b_r, d_model):
    t = x2d.shape[0]
    tm = MIX_TM
    full = lambda shape: pl.BlockSpec(shape, lambda i: (0, 0))
    row = lambda: pl.BlockSpec((tm, d_model), lambda i: (i, 0))
    krow = lambda: pl.BlockSpec((TOP_K, tm), lambda i: (0, i))
    return pl.pallas_call(
        _mix_kernel,
        grid=(t // tm,),
        in_specs=[
            pl.BlockSpec((tm, d_model), lambda i: (i, 0)),
            pl.BlockSpec((tm, d_model), lambda i: (i, 1)),
            row(), row(), row(),
            full((d_model, d_model)), full((1, d_model)), full((N_EXPERTS, d_model)), full((N_EXPERTS, 1)),
        ],
        out_specs=[row(), pl.BlockSpec((tm, d_model // LANES, LANES), lambda i: (i, 0, 0)),
                   krow(), krow(), krow(), full((N_EXPERTS, 1))],
        out_shape=[
            jax.ShapeDtypeStruct((t, d_model), F32),
            jax.ShapeDtypeStruct((t, d_model // LANES, LANES), F32),
            jax.ShapeDtypeStruct((TOP_K, t), jnp.int32),
            jax.ShapeDtypeStruct((TOP_K, t), F32),
            jax.ShapeDtypeStruct((TOP_K, t), jnp.int32),
            jax.ShapeDtypeStruct((N_EXPERTS, 1), jnp.int32),
        ],
        scratch_shapes=[pltpu.VMEM((N_EXPERTS, 1), F32)],
        compiler_params=_cparams(("arbitrary",)),
        name="merge_outproj_route",
    )(proj_gate, proj_gate, oa, od, x2d, w_o, g_ffn, w_r_t, b_r)


def _row_copy(src_ref, src_row, dst_ref, dst_row, sem):
    return pltpu.make_async_copy(src_ref.at[pl.ds(src_row, 1)], dst_ref.at[pl.ds(dst_row, 1)], sem)


def _expert_kernel(be_ref, src_ref, src_next_ref, dst_prev_ref, dst_ref, h2_ref, wup_ref, bup_ref,
                   wdn_ref, bdn_ref, y_ref, x0, x1, y0, y1, wup_bf, wdn_bf, gsem, ssem):
    i = pl.program_id(0)
    last = pl.num_programs(0) - 1
    rb = MOE_RB
    d_ff = wdn_ref.shape[1]

    def gather_wait(x_ref, s):
        pltpu.make_async_copy(h2_ref.at[pl.ds(0, rb)], x_ref, gsem.at[s]).wait()

    def scatter_wait(yb_ref, s):
        pltpu.make_async_copy(yb_ref, y_ref.at[pl.ds(0, rb)], ssem.at[s]).wait()

    @pl.when(i == 0)
    def _():
        for a in range(rb):
            _row_copy(h2_ref, src_ref[a], x0, a, gsem.at[0]).start()

    @pl.when(jnp.logical_or(i == 0, be_ref[i] != be_ref[jnp.maximum(i - 1, 0)]))
    def _():
        rr = lax.broadcasted_iota(jnp.int32, (2 * LANES, 2 * LANES), 0)
        cc = lax.broadcasted_iota(jnp.int32, (2 * LANES, 2 * LANES), 1)
        pick = jnp.where(cc < LANES, 2 * cc, 2 * (cc - LANES) + 1)
        perm = (rr == pick).astype(BF16)
        for g in range(wup_ref.shape[2] // (2 * LANES)):
            cs = slice(g * 2 * LANES, (g + 1) * 2 * LANES)
            wup_bf[:, cs] = jnp.dot(wup_ref[0, :, cs].astype(BF16), perm,
                                    preferred_element_type=F32).astype(BF16)
        wdn_bf[...] = wdn_ref[0].astype(BF16)

    @pl.when(i == 0)
    def _():
        y1[...] = jnp.zeros(y1.shape, F32)

    def block(xa, ya, xb, yb, s):
        gather_wait(xa, s)

        @pl.when(i >= 1)
        def _():
            scatter_wait(ya, s)

        for a in range(rb):
            _row_copy(h2_ref, src_next_ref[a], xb, a, gsem.at[1 - s]).start(priority=a % 2)
            _row_copy(yb, a, y_ref, dst_prev_ref[a], ssem.at[1 - s]).start(priority=a % 2)
        n_cb = xa.shape[1]
        x = jnp.concatenate([xa[:, cb, :] for cb in range(n_cb)], axis=1)
        hid = jnp.dot(x.astype(BF16), wup_bf[...], preferred_element_type=F32) + bup_ref[0]
        acts = []
        for g in range(hid.shape[1] // (2 * LANES)):
            glu = jnp.minimum(hid[:, g * 2 * LANES:g * 2 * LANES + LANES], SWIGLU_LIMIT)
            lin = jnp.clip(hid[:, g * 2 * LANES + LANES:(g + 1) * 2 * LANES], -SWIGLU_LIMIT, SWIGLU_LIMIT)
            acts.append(glu * jax.nn.sigmoid(SWIGLU_ALPHA * glu) * (lin + 1.0))
        act = jnp.concatenate(acts, axis=1)
        assert act.shape[1] == d_ff
        y = jnp.dot(act.astype(BF16), wdn_bf[...], preferred_element_type=F32) + bdn_ref[0]
        for cb in range(n_cb):
            ya[:, cb, :] = y[:, cb * LANES:(cb + 1) * LANES]

        @pl.when(i == last)
        def _():
            for a in range(rb):
                _row_copy(ya, a, y_ref, dst_ref[a], ssem.at[s]).start()
            scatter_wait(yb, 1 - s)
            scatter_wait(ya, s)
            gather_wait(xb, 1 - s)

    @pl.when((i & 1) == 1)
    def _():
        block(x1, y1, x0, y0, 1)

    @pl.when((i & 1) == 0)
    def _():
        block(x0, y0, x1, y1, 0)


def _experts(block_e, src_tok, dst_row, h2, w_up, b_up, w_down, b_down):
    n_rows = src_tok.shape[0]
    d = w_up.shape[1]
    nb = n_rows // MOE_RB
    two_ff = w_up.shape[2]
    d_ff = w_down.shape[1]
    dst_ext = jnp.concatenate([n_rows + jnp.arange(MOE_RB, dtype=jnp.int32), dst_row])
    smem_rows = lambda fn: pl.BlockSpec((MOE_RB,), fn, memory_space=pltpu.SMEM)
    grid_spec = pltpu.PrefetchScalarGridSpec(
        num_scalar_prefetch=1,
        grid=(nb,),
        in_specs=[
            smem_rows(lambda i, be: (i,)),
            smem_rows(lambda i, be: (jnp.minimum(i + 1, nb - 1),)),
            smem_rows(lambda i, be: (i,)),
            smem_rows(lambda i, be: (i + 1,)),
            pl.BlockSpec(memory_space=pl.ANY),
            pl.BlockSpec((1, d, two_ff), lambda i, be: (be[i], 0, 0)),
            pl.BlockSpec((1, 1, two_ff), lambda i, be: (be[i], 0, 0)),
            pl.BlockSpec((1, d_ff, d), lambda i, be: (be[i], 0, 0)),
            pl.BlockSpec((1, 1, d), lambda i, be: (be[i], 0, 0)),
        ],
        out_specs=pl.BlockSpec(memory_space=pl.ANY),
        scratch_shapes=[
            pltpu.VMEM((MOE_RB,) + h2.shape[1:], F32), pltpu.VMEM((MOE_RB,) + h2.shape[1:], F32),
            pltpu.VMEM((MOE_RB,) + h2.shape[1:], F32), pltpu.VMEM((MOE_RB,) + h2.shape[1:], F32),
            pltpu.VMEM((d, two_ff), BF16),
            pltpu.VMEM((d_ff, d), BF16),
            pltpu.SemaphoreType.DMA((2,)),
            pltpu.SemaphoreType.DMA((2,)),
        ],
    )
    return pl.pallas_call(
        _expert_kernel,
        grid_spec=grid_spec,
        out_shape=jax.ShapeDtypeStruct((n_rows + MOE_RB,) + h2.shape[1:], F32),
        compiler_params=_cparams(("arbitrary",)),
        name="moe_experts",
    )(block_e, src_tok, src_tok, dst_ext, dst_ext, h2, w_up, b_up, w_down, b_down)


def _combine_kernel(x1_ref, w_ref, y0_ref, y1_ref, y2_ref, y3_ref, o_ref):
    w = w_ref[...]
    for cb in range(y0_ref.shape[1]):
        cs = slice(cb * LANES, (cb + 1) * LANES)
        out = x1_ref[:, cs]
        for kk, y_ref in enumerate((y0_ref, y1_ref, y2_ref, y3_ref)):
            out = out + w[:, kk:kk + 1] * y_ref[:, cb, :]
        o_ref[:, cs] = out


def _combine(x1, w_tok, y_slots):
    t, d = x1.shape
    tc = COMB_TC
    nt = t // tc
    yspec = lambda kk: pl.BlockSpec((tc,) + y_slots.shape[1:], lambda i: (kk * nt + i, 0, 0))
    return pl.pallas_call(
        _combine_kernel,
        grid=(nt,),
        in_specs=[
            pl.BlockSpec((tc, d), lambda i: (i, 0)),
            pl.BlockSpec((tc, TOP_K), lambda i: (i, 0)),
            yspec(0), yspec(1), yspec(2), yspec(3),
        ],
        out_specs=pl.BlockSpec((tc, d), lambda i: (i, 0)),
        out_shape=jax.ShapeDtypeStruct((t, d), F32),
        compiler_params=_cparams(("parallel",)),
        name="moe_combine",
    )(x1, w_tok, y_slots, y_slots, y_slots, y_slots)


def _moe(x1, h2, topi, topw, rank, counts, w_up, b_up, w_down, b_down):
    t, d = x1.shape
    n_assign = t * TOP_K
    nb = -(-n_assign // MOE_RB) + N_EXPERTS
    n_rows = nb * MOE_RB
    counts = counts[:, 0]
    padded = (counts + MOE_RB - 1) // MOE_RB * MOE_RB
    padded_end = jnp.cumsum(padded)
    padded_start = padded_end - padded
    expert_ids = jnp.arange(N_EXPERTS, dtype=jnp.int32)[:, None, None]
    start_of = jnp.sum(jnp.where(topi[None] == expert_ids, padded_start[:, None, None], 0), axis=0)
    dest = (start_of + rank).astype(jnp.int32)
    n_used = (padded_end[-1] // MOE_RB).astype(jnp.int32)
    blk = jnp.minimum(jnp.arange(nb, dtype=jnp.int32), n_used - 1)
    block_e = jnp.minimum(jnp.sum(padded_end[None, :] <= (blk * MOE_RB)[:, None], axis=1),
                          N_EXPERTS - 1).astype(jnp.int32)
    slot_of = jnp.full((n_rows,), -1, jnp.int32).at[dest.reshape(-1)].set(
        jnp.arange(n_assign, dtype=jnp.int32), unique_indices=True)
    is_pad = slot_of < 0
    src_tok = jnp.where(is_pad, 0, slot_of % t)
    dst_row = jnp.where(is_pad, n_assign + jnp.cumsum(is_pad.astype(jnp.int32)) - 1, slot_of)

    y_slots = _experts(block_e, src_tok, dst_row, h2, w_up, b_up, w_down, b_down)
    return _combine(x1, topw.T, y_slots)


def kernel(x, g_mix, w_in, b_gate, conv_w, a_log, dt_bias, g_delta_out, q_norm, k_norm, lambda_q1, lambda_k1, lambda_q2, lambda_k2, g_subln, rel_bias, w_o, g_ffn, w_router, b_router, w_up, b_up, w_down, b_down):
    bsz, seq, d = x.shape
    depth = g_mix.shape[0]
    n_heads = d // HEAD_DIM
    t = bsz * seq
    d_ff = w_down.shape[2]
    assert d % PROJ_TN == 0 and t % PROJ_TM == 0 and seq % GDN_TB == 0 and seq % ATT_BQ == 0
    assert t % MIX_TM == 0 and t % COMB_TC == 0 and n_heads % GDN_HG == 0
    assert (t * TOP_K) % MOE_RB == 0
    assert 2 * n_heads <= 2 * SUBLANES

    x2d = x.reshape(t, d)
    for l in range(depth):
        wl = w_in[l]
        c0 = 4 * d
        c1 = c0 + 2 * n_heads
        c2 = c1 + 2 * d
        c3 = c2 + d
        w_small = jnp.pad(wl[:, c0:c1], ((0, 0), (0, LANES - 2 * n_heads)))
        gm = g_mix[l].reshape(1, d)
        w_plain = jnp.concatenate([wl[:, :c0], wl[:, c2:c3]], axis=1).astype(BF16)
        proj_plain = _input_projection(x2d, gm, w_plain, jnp.zeros((1, 5 * d), F32), "plain")
        qk_gain = jnp.concatenate([jnp.tile(q_norm[l] * (DH_DIFF ** -0.5), 2 * n_heads),
                                   jnp.tile(k_norm[l], 2 * n_heads)]).reshape(1, 2 * d)
        proj_qk = _input_projection(x2d, gm, wl[:, c1:c2].astype(BF16), qk_gain, "qknorm")
        proj_gate = _input_projection(x2d, gm, wl[:, c3:].astype(BF16), b_gate[l].reshape(1, 2 * d), "gate")

        head_pad = jnp.zeros((LANES - 2 * n_heads,), F32)
        alog = jnp.concatenate([jnp.zeros((n_heads,), F32), a_log[l], head_pad])
        dtb = jnp.concatenate([jnp.zeros((n_heads,), F32), dt_bias[l], head_pad])
        rows_t = 2 * n_heads
        small, small_t = _small_projection(
            x2d, g_mix[l].reshape(1, d), w_small.astype(BF16), w_small[:, :rows_t].T.astype(BF16),
            alog.reshape(1, LANES), dtb.reshape(1, LANES),
            alog[:rows_t].reshape(rows_t, 1), dtb[:rows_t].reshape(rows_t, 1), n_heads)

        oa = _gated_delta(proj_plain, small, small_t, conv_w[l], g_delta_out[l].reshape(1, HEAD_DIM),
                          bsz, seq, n_heads, d)

        lam_init = 0.8 - 0.6 * math.exp(-0.3 * l)
        lam_params = jnp.stack([lambda_q1[l], lambda_k1[l], lambda_q2[l], lambda_k2[l]])
        od = _diff_attention(proj_qk, proj_plain, rel_bias, lam_params, g_subln[l].reshape(1, HEAD_DIM),
                             bsz, seq, n_heads, d, lam_init)

        x1, h2, topi, topw, rank, counts = _mix_project_route(
            proj_gate, oa, od, x2d, w_o[l].astype(BF16), g_ffn[l].reshape(1, d),
            w_router[l].T, b_router[l].reshape(N_EXPERTS, 1), d)

        b_up_l = b_up[l].reshape(N_EXPERTS, 2 * d_ff // (2 * LANES), LANES, 2)
        b_up_l = jnp.swapaxes(b_up_l, 2, 3).reshape(N_EXPERTS, 1, 2 * d_ff)
        x2d = _moe(x1, h2, topi, topw, rank, counts, w_up[l], b_up_l,
                   w_down[l], b_down[l].reshape(N_EXPERTS, 1, d))
    return x2d.reshape(bsz, seq, d)
```

```python
import functools
import math

import jax
import jax.numpy as jnp
from jax import lax
from jax.experimental import pallas as pl
from jax.experimental.pallas import tpu as pltpu

F32 = jnp.float32
BF16 = jnp.bfloat16

HEAD_DIM = 128
DH_DIFF = HEAD_DIM // 2
CONV_WIDTH = 4
CHUNK = 64
N_BUCKETS = 32
MAX_DISTANCE = 128
N_EXPERTS = 32
TOP_K = 4
TOP_K_SHIFT = 2
SWIGLU_LIMIT = 7.0
SWIGLU_ALPHA = 1.702
EPS = 1e-6
NEG_BIG = -1e30

LANES = 128
SUBLANES = 8
VMEM_LIMIT = 56 * 1024 * 1024

PROJ_TM = 2048
PROJ_TN = 1024
PROJ_CHUNK = 256
GDN_TB = 256
GDN_HG = 4
ATT_BQ = 512
ATT_BK = 512
MIX_TM = 512
MOE_RB = 256
COMB_TC = 512


def _cparams(sem):
    return pltpu.CompilerParams(dimension_semantics=sem, vmem_limit_bytes=VMEM_LIMIT)


def _bdot(a, b):
    return jnp.dot(a.astype(BF16), b.astype(BF16), preferred_element_type=F32)


def _bdot_nt(a, b):
    return lax.dot_general(a.astype(BF16), b.astype(BF16), (((1,), (1,)), ((), ())),
                           preferred_element_type=F32)


def _bdot_tn(a, b):
    return lax.dot_general(a.astype(BF16), b.astype(BF16), (((0,), (0,)), ((), ())),
                           preferred_element_type=F32)


def _proj_kernel(x_ref, g_ref, w_ref, aux_ref, o_ref, h_ref, *, mode):
    @pl.when(pl.program_id(1) == 0)
    def _():
        x = x_ref[...]
        ms = jnp.mean(x * x, axis=-1, keepdims=True)
        h_ref[...] = (x * lax.rsqrt(ms + EPS) * g_ref[...]).astype(BF16)

    h = h_ref[...]
    lo = lax.broadcasted_iota(jnp.int32, (1, LANES), 1) < DH_DIFF
    for c in range(PROJ_TN // PROJ_CHUNK):
        cs = slice(c * PROJ_CHUNK, (c + 1) * PROJ_CHUNK)
        acc = jnp.dot(h, w_ref[:, cs], preferred_element_type=F32)
        if mode == "plain":
            o_ref[:, cs] = acc.astype(o_ref.dtype)
        elif mode == "gate":
            o_ref[:, cs] = jax.nn.sigmoid(acc + aux_ref[:, cs]).astype(o_ref.dtype)
        else:
            for g in range(PROJ_CHUNK // LANES):
                sl = slice(c * PROJ_CHUNK + g * LANES, c * PROJ_CHUNK + (g + 1) * LANES)
                y = acc[:, g * LANES:(g + 1) * LANES]
                y2 = y * y
                s_lo = jnp.sum(jnp.where(lo, y2, 0.0), axis=-1, keepdims=True)
                s_hi = jnp.sum(jnp.where(lo, 0.0, y2), axis=-1, keepdims=True)
                r = jnp.where(lo, lax.rsqrt(s_lo / DH_DIFF + EPS), lax.rsqrt(s_hi / DH_DIFF + EPS))
                o_ref[:, sl] = (y * r * aux_ref[:, sl]).astype(o_ref.dtype)


def _input_projection(x2d, g_mix, w, aux, mode):
    t, d = x2d.shape
    n = w.shape[1]
    return pl.pallas_call(
        functools.partial(_proj_kernel, mode=mode),
        grid=(t // PROJ_TM, n // PROJ_TN),
        in_specs=[
            pl.BlockSpec((PROJ_TM, d), lambda i, j: (i, 0)),
            pl.BlockSpec((1, d), lambda i, j: (0, 0)),
            pl.BlockSpec((d, PROJ_TN), lambda i, j: (0, j)),
            pl.BlockSpec((1, PROJ_TN), lambda i, j: (0, j)),
        ],
        out_specs=pl.BlockSpec((PROJ_TM, PROJ_TN), lambda i, j: (i, j)),
        out_shape=jax.ShapeDtypeStruct((t, n), BF16),
        scratch_shapes=[pltpu.VMEM((PROJ_TM, d), BF16)],
        compiler_params=_cparams(("parallel", "arbitrary")),
        name="input_projection_" + mode,
    )(x2d, g_mix, w, aux)


def _small_proj_kernel(x_ref, g_ref, w_ref, wt_ref, alog_ref, dtb_ref, alog_t_ref, dtb_t_ref,
                       o_ref, ot_ref, *, n_heads):
    x = x_ref[...]
    ms = jnp.mean(x * x, axis=-1, keepdims=True)
    h = (x * lax.rsqrt(ms + EPS) * g_ref[...]).astype(BF16)

    def finish(acc, idx, alog, dtb):
        beta = jax.nn.sigmoid(acc)
        z = acc + dtb
        softplus = jnp.maximum(z, 0.0) + jnp.log1p(jnp.exp(-jnp.abs(z)))
        gdec = -jnp.exp(alog) * softplus
        return jnp.where(idx < n_heads, beta, jnp.where(idx < 2 * n_heads, gdec, 0.0))

    acc = jnp.dot(h, w_ref[...], preferred_element_type=F32)
    lane = lax.broadcasted_iota(jnp.int32, acc.shape, 1)
    o_ref[...] = finish(acc, lane, alog_ref[...], dtb_ref[...])
    acc_t = lax.dot_general(wt_ref[...], h, (((1,), (1,)), ((), ())),
                            preferred_element_type=F32)
    sub = lax.broadcasted_iota(jnp.int32, acc_t.shape, 0)
    ot_ref[...] = finish(acc_t, sub, alog_t_ref[...], dtb_t_ref[...])


def _small_projection(x2d, g_mix, w_small, w_small_t, alog, dtb, alog_t, dtb_t, n_heads):
    t, d = x2d.shape
    rows_t = w_small_t.shape[0]
    tm = PROJ_TM
    full = lambda shape: pl.BlockSpec(shape, lambda i: (0, 0))
    return pl.pallas_call(
        functools.partial(_small_proj_kernel, n_heads=n_heads),
        grid=(t // tm,),
        in_specs=[
            pl.BlockSpec((tm, d), lambda i: (i, 0)),
            full((1, d)), full((d, LANES)), full((rows_t, d)),
            full((1, LANES)), full((1, LANES)), full((rows_t, 1)), full((rows_t, 1)),
        ],
        out_specs=[pl.BlockSpec((tm, LANES), lambda i: (i, 0)),
                   pl.BlockSpec((rows_t, tm), lambda i: (0, i))],
        out_shape=[jax.ShapeDtypeStruct((t, LANES), F32),
                   jax.ShapeDtypeStruct((rows_t, t), F32)],
        compiler_params=_cparams(("parallel",)),
        name="beta_decay_projection",
    )(x2d, g_mix, w_small, w_small_t, alog, dtb, alog_t, dtb_t)


def _gdn_kernel(q_ref, k_ref, v_ref, z_ref, sm_ref, smt_ref, cwq_ref, cwk_ref, cwv_ref, gout_ref,
                o_ref, state_ref, qp_ref, kp_ref, vp_ref, vn_ref, *, n_heads):
    hg = pl.program_id(1)
    s = pl.program_id(2)
    tb = GDN_TB
    pad = SUBLANES
    width = GDN_HG * HEAD_DIM

    @pl.when(s == 0)
    def _():
        state_ref[...] = jnp.zeros_like(state_ref)
        for p_ref in (qp_ref, kp_ref, vp_ref):
            p_ref[0:pad, :] = jnp.zeros((pad, width), F32)

    def conv_silu(x_ref, p_ref, cw_ref):
        p_ref[pad:pad + tb, :] = x_ref[...].astype(F32)
        acc = cw_ref[CONV_WIDTH - 1:CONV_WIDTH, :] * p_ref[pad:pad + tb, :]
        for jj in range(CONV_WIDTH - 1):
            off = pad - (CONV_WIDTH - 1) + jj
            acc = acc + cw_ref[jj:jj + 1, :] * p_ref[off:off + tb, :]
        p_ref[0:pad, :] = p_ref[tb:tb + pad, :]
        return acc * jax.nn.sigmoid(acc)

    q_all = conv_silu(q_ref, qp_ref, cwq_ref)
    k_all = conv_silu(k_ref, kp_ref, cwk_ref)
    v_all = conv_silu(v_ref, vp_ref, cwv_ref)

    r = lax.broadcasted_iota(jnp.int32, (tb, tb), 0)
    c = lax.broadcasted_iota(jnp.int32, (tb, tb), 1)
    shift = int(math.log2(CHUNK))
    same = (r >> shift) == (c >> shift)
    incl = jnp.logical_and(same, c <= r)
    strict = jnp.logical_and(same, c < r)

    small = sm_ref[...]
    small_t = smt_ref[...]
    lane = lax.broadcasted_iota(jnp.int32, small.shape, 1)
    def split3(a):
        hi = a.astype(BF16)
        r1 = a - hi.astype(F32)
        mid = r1.astype(BF16)
        lo = (r1 - mid.astype(F32)).astype(BF16)
        return hi.astype(F32), mid.astype(F32), lo.astype(F32)

    part = 2 * n_heads
    s_hi, s_mid, s_lo = split3(small)
    small3 = jnp.where(lane < part, s_hi,
                       jnp.where(lane < 2 * part, pltpu.roll(s_mid, part, 1),
                                 jnp.where(lane < 3 * part, pltpu.roll(s_lo, 2 * part, 1), 0.0)))
    both = _bdot(jnp.concatenate([incl.astype(F32), same.astype(F32)], axis=0), small3)
    gcum = both[:tb]
    gtot = both[tb:]
    gcum_t = _bdot(jnp.concatenate(split3(small_t), axis=0),
                   jnp.logical_and(same, r <= c).astype(F32))
    sub3 = lax.broadcasted_iota(jnp.int32, gcum_t.shape, 0)

    heads = range(GDN_HG)
    hsl = [slice(hh * HEAD_DIM, (hh + 1) * HEAD_DIM) for hh in heads]
    qs = [q_all[:, hs] for hs in hsl]
    ks = [k_all[:, hs] for hs in hsl]
    vs = [v_all[:, hs] for hs in hsl]
    qs = [q * lax.rsqrt(jnp.sum(q * q, axis=-1, keepdims=True) + EPS) * (HEAD_DIM ** -0.5) for q in qs]
    ks = [k * lax.rsqrt(jnp.sum(k * k, axis=-1, keepdims=True) + EPS) for k in ks]

    def col_of(arr, idx):
        return jnp.sum(jnp.where(lane == idx, arr, 0.0), axis=-1, keepdims=True)

    def terms_of(pos, idx):
        return jnp.logical_or(pos == idx, jnp.logical_or(pos == idx + part, pos == idx + 2 * part))

    head_ids = [hg * GDN_HG + hh for hh in heads]
    betas = [col_of(small, hd) for hd in head_ids]
    gcs = [jnp.sum(jnp.where(terms_of(lane, hd + n_heads), gcum, 0.0), axis=-1, keepdims=True)
           for hd in head_ids]
    gls = [jnp.sum(jnp.where(terms_of(lane, hd + n_heads), gtot, 0.0), axis=-1, keepdims=True)
           for hd in head_ids]
    gc_rows = [jnp.sum(jnp.where(terms_of(sub3, hd + n_heads), gcum_t, 0.0), axis=0, keepdims=True)
               for hd in head_ids]

    decays = [jnp.where(incl, jnp.exp(jnp.minimum(gc - gr, 0.0)), 0.0) for gc, gr in zip(gcs, gc_rows)]
    kbs = [k * b for k, b in zip(ks, betas)]
    kks = [_bdot_nt(kb, k) for kb, k in zip(kbs, ks)]
    pws = [jnp.where(strict, -(kk * dc), 0.0) for kk, dc in zip(kks, decays)]
    n_chunks = tb // CHUNK
    cat_row = lax.broadcasted_iota(jnp.int32, (CHUNK, tb), 0)
    cat_lane = lax.broadcasted_iota(jnp.int32, (CHUNK, tb), 1)
    lane_chunk = cat_lane >> shift

    def block_diag(m_cat):
        return jnp.concatenate([jnp.where(lane_chunk == ci, m_cat, 0.0) for ci in range(n_chunks)], axis=0)

    def cat_of(m_bd):
        out = m_bd[0:CHUNK]
        for ci in range(1, n_chunks):
            out = out + m_bd[ci * CHUNK:(ci + 1) * CHUNK]
        return out

    pcats = [cat_of(pw) for pw in pws]
    eye_cat = ((cat_lane & (CHUNK - 1)) == cat_row).astype(F32)
    tcats = [eye_cat + pc for pc in pcats]
    pcats = [_bdot(pc, block_diag(pc)) for pc in pcats]
    n_levels = int(math.log2(CHUNK))
    for lev in range(1, n_levels):
        bds = [block_diag(pc) for pc in pcats]
        if lev < n_levels - 1:
            prods = [_bdot(jnp.concatenate([pc, tc], axis=0), bd) for pc, tc, bd in zip(pcats, tcats, bds)]
            pcats = [pr[:CHUNK] for pr in prods]
            tcats = [tc + pr[CHUNK:] for tc, pr in zip(tcats, prods)]
        else:
            tcats = [tc + _bdot(tc, bd) for tc, bd in zip(tcats, bds)]
    tmats = [block_diag(tc) for tc in tcats]
    egcs = [jnp.exp(gc) for gc in gcs]
    uws = [_bdot(tm, jnp.concatenate([v * b, kb * eg], axis=1))
           for tm, v, b, kb, eg in zip(tmats, vs, betas, kbs, egcs)]
    us = [uw[:, :HEAD_DIM] for uw in uws]
    ws = [uw[:, HEAD_DIM:] for uw in uws]
    qkm = [_bdot_nt(q, k) for q, k in zip(qs, ks)]
    qkm = [jnp.where(incl, x * dc, 0.0) for x, dc in zip(qkm, decays)]
    q_decs = [q * eg for q, eg in zip(qs, egcs)]
    k_ends = [k * jnp.exp(gl - gc) for k, gl, gc in zip(ks, gls, gcs)]

    for hh in heads:
        vn_ref[hh] = jnp.zeros((tb, HEAD_DIM), F32)
    outs = [[] for _ in heads]
    for ci in range(tb // CHUNK):
        cs = slice(ci * CHUNK, (ci + 1) * CHUNK)
        sts = [state_ref[hh] for hh in heads]
        ws_qs = [_bdot(jnp.concatenate([ws[hh][cs], q_decs[hh][cs]], axis=0), sts[hh]) for hh in heads]
        v_news = [us[hh][cs] - ws_qs[hh][:CHUNK] for hh in heads]
        for hh in heads:
            vn_ref[hh, cs, :] = v_news[hh]
        intra = [_bdot(qkm[hh][cs], vn_ref[hh]) for hh in heads]
        upd = [_bdot_tn(k_ends[hh][cs], v_news[hh]) for hh in heads]
        for hh in heads:
            outs[hh].append(ws_qs[hh][CHUNK:] + intra[hh])
            g_last = gls[hh][ci * CHUNK:ci * CHUNK + 1, :]
            state_ref[hh] = sts[hh] * jnp.exp(g_last) + upd[hh]
    for hh in heads:
        o = jnp.concatenate(outs[hh], axis=0)
        o = o * lax.rsqrt(jnp.mean(o * o, axis=-1, keepdims=True) + EPS) * gout_ref[...]
        zz = z_ref[:, hsl[hh]].astype(F32)
        o_ref[:, hsl[hh]] = (o * (zz * jax.nn.sigmoid(zz))).astype(o_ref.dtype)


def _gated_delta(big, small, small_t, conv_w, g_out, bsz, seq, n_heads, d_model):
    t = bsz * seq
    tb = GDN_TB
    ns = seq // tb
    width = GDN_HG * HEAD_DIM
    nhg = n_heads // GDN_HG
    blocks_per_group = d_model // width
    rows_t = small_t.shape[0]

    def colspec(group):
        return pl.BlockSpec((tb, width), lambda b, h, s: (b * ns + s, group * blocks_per_group + h))

    def cwspec(group):
        return pl.BlockSpec((CONV_WIDTH, width), lambda b, h, s: (0, group * blocks_per_group + h))

    return pl.pallas_call(
        functools.partial(_gdn_kernel, n_heads=n_heads),
        grid=(bsz, nhg, ns),
        in_specs=[
            colspec(0), colspec(1), colspec(2), colspec(3),
            pl.BlockSpec((tb, LANES), lambda b, h, s: (b * ns + s, 0)),
            pl.BlockSpec((rows_t, tb), lambda b, h, s: (0, b * ns + s)),
            cwspec(0), cwspec(1), cwspec(2),
            pl.BlockSpec((1, HEAD_DIM), lambda b, h, s: (0, 0)),
        ],
        out_specs=pl.BlockSpec((tb, width), lambda b, h, s: (b * ns + s, h)),
        out_shape=jax.ShapeDtypeStruct((t, d_model), BF16),
        scratch_shapes=[
            pltpu.VMEM((GDN_HG, HEAD_DIM, HEAD_DIM), F32),
            pltpu.VMEM((tb + SUBLANES, width), F32),
            pltpu.VMEM((tb + SUBLANES, width), F32),
            pltpu.VMEM((tb + SUBLANES, width), F32),
            pltpu.VMEM((GDN_HG, tb, HEAD_DIM), F32),
        ],
        compiler_params=_cparams(("parallel", "parallel", "arbitrary")),
        name="gated_delta",
    )(big, big, big, big, small, small_t, conv_w, conv_w, conv_w, g_out)


def _t5_bucket(n):
    max_exact = N_BUCKETS // 2
    nf = jnp.maximum(n, 1).astype(F32)
    large = max_exact + (jnp.log(nf / max_exact) / math.log(MAX_DISTANCE / max_exact)
                         * (N_BUCKETS - max_exact)).astype(jnp.int32)
    large = jnp.minimum(large, N_BUCKETS - 1)
    return jnp.where(n < max_exact, n, large)


def _attn_kernel(rb_ref, q_ref, k_ref, v_ref, lam_ref, gsub_ref, o_ref,
                 bias_ref, m_ref, acc_ref, sa_ref, sb_ref, *, lam_init):
    h = pl.program_id(0)
    b = pl.program_id(1)
    qi = pl.program_id(2)
    bq, bk = ATT_BQ, ATT_BK

    @pl.when(jnp.logical_and(b == 0, qi == 0))
    def _():
        i = lax.broadcasted_iota(jnp.int32, (bq, bk), 0)
        jj = lax.broadcasted_iota(jnp.int32, (bq, bk), 1)
        far = rb_ref[N_BUCKETS - 1, h]
        bias_ref[2] = jnp.zeros((bq, bk), F32)
        for slot in range(2):
            n = i - jj + slot * bk
            bucket = _t5_bucket(jnp.maximum(n, 0))
            bias = jnp.zeros((bq, bk), F32)
            for cc in range(N_BUCKETS):
                bias = jnp.where(bucket == cc, rb_ref[cc, h] - far, bias)
            if slot == 0:
                bias = jnp.where(n >= 0, bias, NEG_BIG)
            bias_ref[slot] = bias

    m_ref[...] = jnp.full(m_ref.shape, NEG_BIG, F32)
    acc_ref[...] = jnp.zeros(acc_ref.shape, F32)

    q = q_ref[...]
    lane = lax.broadcasted_iota(jnp.int32, q.shape, 1)
    zero = jnp.zeros_like(q)
    qs = jnp.concatenate([jnp.where(lane < DH_DIFF, q, zero), jnp.where(lane < DH_DIFF, zero, q)], axis=0)
    ones_col = (lax.broadcasted_iota(jnp.int32, (bk, HEAD_DIM), 1) == 0).astype(BF16)

    def scores(j):
        ks = pl.multiple_of(j * bk, bk)
        return lax.dot_general(qs, k_ref[pl.ds(ks, bk), :], (((1,), (1,)), ((), ())),
                               preferred_element_type=F32)

    def absorb(j, sc_ref, biased=True):
        ks = pl.multiple_of(j * bk, bk)
        v_ext = jnp.concatenate([v_ref[pl.ds(ks, bk), :], ones_col], axis=1)
        if biased:
            bias = bias_ref[jnp.minimum(qi - j, 2)]
            sc = jnp.concatenate([sc_ref[0:bq, :] + bias, sc_ref[bq:2 * bq, :] + bias], axis=0)
        else:
            sc = sc_ref[...]
        m_old = m_ref[...]
        m_new = jnp.maximum(m_old, jnp.max(sc, axis=-1, keepdims=True))
        p = jnp.exp(sc - m_new)
        acc_ref[...] = (jnp.exp(m_old - m_new) * acc_ref[...]
                        + jnp.dot(p.astype(BF16), v_ext, preferred_element_type=F32))
        m_ref[...] = m_new

    n_tiles = qi + 1
    sa_ref[...] = scores(0)

    def pair_body(jj, carry, biased):
        j0 = 2 * jj
        sb_ref[...] = scores(j0 + 1)
        absorb(j0, sa_ref, biased)
        sa_ref[...] = scores(jnp.minimum(j0 + 2, qi))
        absorb(j0 + 1, sb_ref, biased)
        return carry

    n_far_pairs = jnp.maximum(qi - 1, 0) // 2
    lax.fori_loop(0, n_far_pairs, functools.partial(pair_body, biased=False), 0)
    lax.fori_loop(n_far_pairs, n_tiles // 2, functools.partial(pair_body, biased=True), 0)

    @pl.when(n_tiles % 2 == 1)
    def _():
        absorb(qi, sa_ref)

    lam_p = lam_ref[...]
    s1 = jnp.sum(lam_p[0:1] * lam_p[1:2], axis=-1, keepdims=True)
    s2 = jnp.sum(lam_p[2:3] * lam_p[3:4], axis=-1, keepdims=True)
    lam = jnp.exp(s1) - jnp.exp(s2) + lam_init
    acc = acc_ref[...]
    num = acc[:, :HEAD_DIM]
    den = acc[:, HEAD_DIM:HEAD_DIM + 1]
    o = num[:bq] / den[:bq] - lam * (num[bq:] / den[bq:])
    o = o * lax.rsqrt(jnp.mean(o * o, axis=-1, keepdims=True) + EPS) * gsub_ref[...]
    o_ref[...] = (o * (1.0 - lam_init)).astype(o_ref.dtype)


def _diff_attention(proj_qk, proj_plain, rel_bias, lam_params, g_subln, bsz, seq, n_heads, d_model,
                    lam_init):
    t = bsz * seq
    nq = seq // ATT_BQ
    per = d_model // HEAD_DIM
    vcol = 4 * per
    return pl.pallas_call(
        functools.partial(_attn_kernel, lam_init=lam_init),
        grid=(n_heads, bsz, nq),
        in_specs=[
            pl.BlockSpec(memory_space=pltpu.SMEM),
            pl.BlockSpec((ATT_BQ, HEAD_DIM), lambda h, b, i: (b * nq + i, h)),
            pl.BlockSpec((seq, HEAD_DIM), lambda h, b, i: (b, per + h)),
            pl.BlockSpec((seq, HEAD_DIM), lambda h, b, i: (b, vcol + h)),
            pl.BlockSpec((4, DH_DIFF), lambda h, b, i: (0, 0)),
            pl.BlockSpec((1, HEAD_DIM), lambda h, b, i: (0, 0)),
        ],
        out_specs=pl.BlockSpec((ATT_BQ, HEAD_DIM), lambda h, b, i: (b * nq + i, h)),
        out_shape=jax.ShapeDtypeStruct((t, d_model), BF16),
        scratch_shapes=[
            pltpu.VMEM((3, ATT_BQ, ATT_BK), F32),
            pltpu.VMEM((2 * ATT_BQ, 1), F32),
            pltpu.VMEM((2 * ATT_BQ, 2 * HEAD_DIM), F32),
            pltpu.VMEM((2 * ATT_BQ, ATT_BK), F32),
            pltpu.VMEM((2 * ATT_BQ, ATT_BK), F32),
        ],
        compiler_params=_cparams(("arbitrary", "arbitrary", "arbitrary")),
        name="diff_attention",
    )(rel_bias, proj_qk, proj_qk, proj_plain, lam_params, g_subln)


def _mix_kernel(ga_ref, gb_ref, oa_ref, od_ref, x_ref, wo_ref, gffn_ref, wr_ref, br_ref,
                x1_ref, h2_ref, topi_ref, topw_ref, rank_ref, cnt_ref, carry_ref):
    i = pl.program_id(0)
    tm = MIX_TM

    @pl.when(i == 0)
    def _():
        carry_ref[...] = jnp.zeros_like(carry_ref)

    mix = (ga_ref[...].astype(F32) * oa_ref[...].astype(F32)
           + gb_ref[...].astype(F32) * od_ref[...].astype(F32))
    x1 = x_ref[...] + jnp.dot(mix.astype(BF16), wo_ref[...], preferred_element_type=F32)
    x1_ref[...] = x1
    h2 = x1 * lax.rsqrt(jnp.mean(x1 * x1, axis=-1, keepdims=True) + EPS) * gffn_ref[...]
    h2_ref[...] = h2

    logits = lax.dot_general(wr_ref[...], h2, (((1,), (1,)), ((), ())),
                             preferred_element_type=F32, precision=lax.Precision.HIGHEST) + br_ref[...]
    eidx = lax.broadcasted_iota(jnp.int32, logits.shape, 0).astype(F32)
    vals, hots = [], []
    cur = logits
    for kk in range(TOP_K):
        mx = jnp.max(cur, axis=0, keepdims=True)
        idx = jnp.min(jnp.where(cur == mx, eidx, float(N_EXPERTS)), axis=0, keepdims=True)
        hot = eidx == idx
        vals.append(mx)
        hots.append(hot)
        topi_ref[kk:kk + 1, :] = idx.astype(jnp.int32)
        cur = jnp.where(hot, -jnp.inf, cur)
    exps = [jnp.exp(vv - vals[0]) for vv in vals]
    denom = exps[0] + exps[1] + exps[2] + exps[3]
    for kk in range(TOP_K):
        topw_ref[kk:kk + 1, :] = exps[kk] / denom

    sel = hots[0]
    for kk in range(1, TOP_K):
        sel = jnp.logical_or(sel, hots[kk])
    sel_f = sel.astype(F32)
    r = lax.broadcasted_iota(jnp.int32, (tm, tm), 0)
    c = lax.broadcasted_iota(jnp.int32, (tm, tm), 1)
    before = _bdot(sel_f, (r < c).astype(F32)) + carry_ref[...]
    for kk in range(TOP_K):
        rank_ref[kk:kk + 1, :] = jnp.sum(jnp.where(hots[kk], before, 0.0), axis=0,
                                         keepdims=True).astype(jnp.int32)
    carry_ref[...] = carry_ref[...] + jnp.sum(sel_f, axis=-1, keepdims=True)
    cnt_ref[...] = carry_ref[...].astype(jnp.int32)


def _mix_project_route(proj_gate, oa, od, x2d, w_o, g_ffn, w_r_t, b_r, d_model):
    t = x2d.shape[0]
    tm = MIX_TM
    full = lambda shape: pl.BlockSpec(shape, lambda i: (0, 0))
    row = lambda: pl.BlockSpec((tm, d_model), lambda i: (i, 0))
    krow = lambda: pl.BlockSpec((TOP_K, tm), lambda i: (0, i))
    return pl.pallas_call(
        _mix_kernel,
        grid=(t // tm,),
        in_specs=[
            pl.BlockSpec((tm, d_model), lambda i: (i, 0)),
            pl.BlockSpec((tm, d_model), lambda i: (i, 1)),
            row(), row(), row(),
            full((d_model, d_model)), full((1, d_model)), full((N_EXPERTS, d_model)), full((N_EXPERTS, 1)),
        ],
        out_specs=[row(), row(), krow(), krow(), krow(), full((N_EXPERTS, 1))],
        out_shape=[
            jax.ShapeDtypeStruct((t, d_model), F32),
            jax.ShapeDtypeStruct((t, d_model), F32),
            jax.ShapeDtypeStruct((TOP_K, t), jnp.int32),
            jax.ShapeDtypeStruct((TOP_K, t), F32),
            jax.ShapeDtypeStruct((TOP_K, t), jnp.int32),
            jax.ShapeDtypeStruct((N_EXPERTS, 1), jnp.int32),
        ],
        scratch_shapes=[pltpu.VMEM((N_EXPERTS, 1), F32)],
        compiler_params=_cparams(("arbitrary",)),
        name="merge_outproj_route",
    )(proj_gate, proj_gate, oa, od, x2d, w_o, g_ffn, w_r_t, b_r)


def _row_copy(src_ref, src_row, dst_ref, dst_row, sem):
    return pltpu.make_async_copy(src_ref.at[pl.ds(src_row, 1)], dst_ref.at[pl.ds(dst_row, 1)], sem)


def _expert_kernel(be_ref, src_ref, src_next_ref, dst_prev_ref, dst_ref, h2_ref, wup_ref, bup_ref,
                   wdn_ref, bdn_ref, y_ref, x0, x1, y0, y1, wup_bf, wdn_bf, gsem, ssem):
    i = pl.program_id(0)
    last = pl.num_programs(0) - 1
    rb = MOE_RB
    d_ff = wdn_ref.shape[1]

    def gather_wait(x_ref, s):
        pltpu.make_async_copy(h2_ref.at[pl.ds(0, rb)], x_ref, gsem.at[s]).wait()

    def scatter_wait(yb_ref, s):
        pltpu.make_async_copy(yb_ref, y_ref.at[pl.ds(0, rb)], ssem.at[s]).wait()

    @pl.when(i == 0)
    def _():
        for a in range(rb):
            _row_copy(h2_ref, src_ref[a], x0, a, gsem.at[0]).start()

    @pl.when(jnp.logical_or(i == 0, be_ref[i] != be_ref[jnp.maximum(i - 1, 0)]))
    def _():
        rr = lax.broadcasted_iota(jnp.int32, (2 * LANES, 2 * LANES), 0)
        cc = lax.broadcasted_iota(jnp.int32, (2 * LANES, 2 * LANES), 1)
        pick = jnp.where(cc < LANES, 2 * cc, 2 * (cc - LANES) + 1)
        perm = (rr == pick).astype(BF16)
        for g in range(wup_ref.shape[2] // (2 * LANES)):
            cs = slice(g * 2 * LANES, (g + 1) * 2 * LANES)
            wup_bf[:, cs] = jnp.dot(wup_ref[0, :, cs].astype(BF16), perm,
                                    preferred_element_type=F32).astype(BF16)
        wdn_bf[...] = wdn_ref[0].astype(BF16)

    @pl.when(i == 0)
    def _():
        y1[...] = jnp.zeros(y1.shape, F32)

    def block(xa, ya, xb, yb, s):
        gather_wait(xa, s)

        @pl.when(i >= 1)
        def _():
            scatter_wait(ya, s)

        for a in range(rb):
            _row_copy(h2_ref, src_next_ref[a], xb, a, gsem.at[1 - s]).start()
            _row_copy(yb, a, y_ref, dst_prev_ref[a], ssem.at[1 - s]).start()
        hid = jnp.dot(xa[...].astype(BF16), wup_bf[...], preferred_element_type=F32) + bup_ref[0]
        acts = []
        for g in range(hid.shape[1] // (2 * LANES)):
            glu = jnp.minimum(hid[:, g * 2 * LANES:g * 2 * LANES + LANES], SWIGLU_LIMIT)
            lin = jnp.clip(hid[:, g * 2 * LANES + LANES:(g + 1) * 2 * LANES], -SWIGLU_LIMIT, SWIGLU_LIMIT)
            acts.append(glu * jax.nn.sigmoid(SWIGLU_ALPHA * glu) * (lin + 1.0))
        act = jnp.concatenate(acts, axis=1)
        assert act.shape[1] == d_ff
        ya[...] = jnp.dot(act.astype(BF16), wdn_bf[...], preferred_element_type=F32) + bdn_ref[0]

        @pl.when(i == last)
        def _():
            for a in range(rb):
                _row_copy(ya, a, y_ref, dst_ref[a], ssem.at[s]).start()
            scatter_wait(yb, 1 - s)
            scatter_wait(ya, s)
            gather_wait(xb, 1 - s)

    @pl.when((i & 1) == 1)
    def _():
        block(x1, y1, x0, y0, 1)

    @pl.when((i & 1) == 0)
    def _():
        block(x0, y0, x1, y1, 0)


def _experts(block_e, src_tok, dst_row, h2, w_up, b_up, w_down, b_down):
    n_rows = src_tok.shape[0]
    d = w_up.shape[1]
    nb = n_rows // MOE_RB
    two_ff = w_up.shape[2]
    d_ff = w_down.shape[1]
    dst_ext = jnp.concatenate([n_rows + jnp.arange(MOE_RB, dtype=jnp.int32), dst_row])
    smem_rows = lambda fn: pl.BlockSpec((MOE_RB,), fn, memory_space=pltpu.SMEM)
    grid_spec = pltpu.PrefetchScalarGridSpec(
        num_scalar_prefetch=1,
        grid=(nb,),
        in_specs=[
            smem_rows(lambda i, be: (i,)),
            smem_rows(lambda i, be: (jnp.minimum(i + 1, nb - 1),)),
            smem_rows(lambda i, be: (i,)),
            smem_rows(lambda i, be: (i + 1,)),
            pl.BlockSpec(memory_space=pl.ANY),
            pl.BlockSpec((1, d, two_ff), lambda i, be: (be[i], 0, 0)),
            pl.BlockSpec((1, 1, two_ff), lambda i, be: (be[i], 0, 0)),
            pl.BlockSpec((1, d_ff, d), lambda i, be: (be[i], 0, 0)),
            pl.BlockSpec((1, 1, d), lambda i, be: (be[i], 0, 0)),
        ],
        out_specs=pl.BlockSpec(memory_space=pl.ANY),
        scratch_shapes=[
            pltpu.VMEM((MOE_RB,) + h2.shape[1:], F32), pltpu.VMEM((MOE_RB,) + h2.shape[1:], F32),
            pltpu.VMEM((MOE_RB,) + h2.shape[1:], F32), pltpu.VMEM((MOE_RB,) + h2.shape[1:], F32),
            pltpu.VMEM((d, two_ff), BF16),
            pltpu.VMEM((d_ff, d), BF16),
            pltpu.SemaphoreType.DMA((2,)),
            pltpu.SemaphoreType.DMA((2,)),
        ],
    )
    return pl.pallas_call(
        _expert_kernel,
        grid_spec=grid_spec,
        out_shape=jax.ShapeDtypeStruct((n_rows + MOE_RB,) + h2.shape[1:], F32),
        compiler_params=_cparams(("arbitrary",)),
        name="moe_experts",
    )(block_e, src_tok, src_tok, dst_ext, dst_ext, h2, w_up, b_up, w_down, b_down)


def _combine_kernel(x1_ref, w_ref, y0_ref, y1_ref, y2_ref, y3_ref, o_ref):
    w = w_ref[...]
    out = x1_ref[...]
    for kk, y_ref in enumerate((y0_ref, y1_ref, y2_ref, y3_ref)):
        out = out + w[:, kk:kk + 1] * y_ref[...]
    o_ref[...] = out


def _combine(x1, w_tok, y_slots):
    t, d = x1.shape
    tc = COMB_TC
    nt = t // tc
    yspec = lambda kk: pl.BlockSpec((tc, d), lambda i: (kk * nt + i, 0))
    return pl.pallas_call(
        _combine_kernel,
        grid=(nt,),
        in_specs=[
            pl.BlockSpec((tc, d), lambda i: (i, 0)),
            pl.BlockSpec((tc, TOP_K), lambda i: (i, 0)),
            yspec(0), yspec(1), yspec(2), yspec(3),
        ],
        out_specs=pl.BlockSpec((tc, d), lambda i: (i, 0)),
        out_shape=jax.ShapeDtypeStruct((t, d), F32),
        compiler_params=_cparams(("parallel",)),
        name="moe_combine",
    )(x1, w_tok, y_slots, y_slots, y_slots, y_slots)


def _moe(x1, h2, topi, topw, rank, counts, w_up, b_up, w_down, b_down):
    t, d = x1.shape
    n_assign = t * TOP_K
    nb = -(-n_assign // MOE_RB) + N_EXPERTS
    n_rows = nb * MOE_RB
    counts = counts[:, 0]
    padded = (counts + MOE_RB - 1) // MOE_RB * MOE_RB
    padded_end = jnp.cumsum(padded)
    padded_start = padded_end - padded
    expert_ids = jnp.arange(N_EXPERTS, dtype=jnp.int32)[:, None, None]
    start_of = jnp.sum(jnp.where(topi[None] == expert_ids, padded_start[:, None, None], 0), axis=0)
    dest = (start_of + rank).astype(jnp.int32)
    n_used = (padded_end[-1] // MOE_RB).astype(jnp.int32)
    blk = jnp.minimum(jnp.arange(nb, dtype=jnp.int32), n_used - 1)
    block_e = jnp.minimum(jnp.sum(padded_end[None, :] <= (blk * MOE_RB)[:, None], axis=1),
                          N_EXPERTS - 1).astype(jnp.int32)
    slot_of = jnp.full((n_rows,), -1, jnp.int32).at[dest.reshape(-1)].set(
        jnp.arange(n_assign, dtype=jnp.int32), unique_indices=True)
    is_pad = slot_of < 0
    src_tok = jnp.where(is_pad, 0, slot_of % t)
    dst_row = jnp.where(is_pad, n_assign + jnp.cumsum(is_pad.astype(jnp.int32)) - 1, slot_of)

    y_slots = _experts(block_e, src_tok, dst_row, h2, w_up, b_up, w_down, b_down)
    return _combine(x1, topw.T, y_slots)


def kernel(x, g_mix, w_in, b_gate, conv_w, a_log, dt_bias, g_delta_out, q_norm, k_norm, lambda_q1, lambda_k1, lambda_q2, lambda_k2, g_subln, rel_bias, w_o, g_ffn, w_router, b_router, w_up, b_up, w_down, b_down):
    bsz, seq, d = x.shape
    depth = g_mix.shape[0]
    n_heads = d // HEAD_DIM
    t = bsz * seq
    d_ff = w_down.shape[2]
    assert d % PROJ_TN == 0 and t % PROJ_TM == 0 and seq % GDN_TB == 0 and seq % ATT_BQ == 0
    assert t % MIX_TM == 0 and t % COMB_TC == 0 and n_heads % GDN_HG == 0
    assert (t * TOP_K) % MOE_RB == 0
    assert 2 * n_heads <= 2 * SUBLANES

    x2d = x.reshape(t, d)
    for l in range(depth):
        wl = w_in[l]
        c0 = 4 * d
        c1 = c0 + 2 * n_heads
        c2 = c1 + 2 * d
        c3 = c2 + d
        w_small = jnp.pad(wl[:, c0:c1], ((0, 0), (0, LANES - 2 * n_heads)))
        gm = g_mix[l].reshape(1, d)
        w_plain = jnp.concatenate([wl[:, :c0], wl[:, c2:c3]], axis=1).astype(BF16)
        proj_plain = _input_projection(x2d, gm, w_plain, jnp.zeros((1, 5 * d), F32), "plain")
        qk_gain = jnp.concatenate([jnp.tile(q_norm[l] * (DH_DIFF ** -0.5), 2 * n_heads),
                                   jnp.tile(k_norm[l], 2 * n_heads)]).reshape(1, 2 * d)
        proj_qk = _input_projection(x2d, gm, wl[:, c1:c2].astype(BF16), qk_gain, "qknorm")
        proj_gate = _input_projection(x2d, gm, wl[:, c3:].astype(BF16), b_gate[l].reshape(1, 2 * d), "gate")

        head_pad = jnp.zeros((LANES - 2 * n_heads,), F32)
        alog = jnp.concatenate([jnp.zeros((n_heads,), F32), a_log[l], head_pad])
        dtb = jnp.concatenate([jnp.zeros((n_heads,), F32), dt_bias[l], head_pad])
        rows_t = 2 * n_heads
        small, small_t = _small_projection(
            x2d, g_mix[l].reshape(1, d), w_small.astype(BF16), w_small[:, :rows_t].T.astype(BF16),
            alog.reshape(1, LANES), dtb.reshape(1, LANES),
            alog[:rows_t].reshape(rows_t, 1), dtb[:rows_t].reshape(rows_t, 1), n_heads)

        oa = _gated_delta(proj_plain, small, small_t, conv_w[l], g_delta_out[l].reshape(1, HEAD_DIM),
                          bsz, seq, n_heads, d)

        lam_init = 0.8 - 0.6 * math.exp(-0.3 * l)
        lam_params = jnp.stack([lambda_q1[l], lambda_k1[l], lambda_q2[l], lambda_k2[l]])
        od = _diff_attention(proj_qk, proj_plain, rel_bias, lam_params, g_subln[l].reshape(1, HEAD_DIM),
                             bsz, seq, n_heads, d, lam_init)

        x1, h2, topi, topw, rank, counts = _mix_project_route(
            proj_gate, oa, od, x2d, w_o[l].astype(BF16), g_ffn[l].reshape(1, d),
            w_router[l].T, b_router[l].reshape(N_EXPERTS, 1), d)

        b_up_l = b_up[l].reshape(N_EXPERTS, 2 * d_ff // (2 * LANES), LANES, 2)
        b_up_l = jnp.swapaxes(b_up_l, 2, 3).reshape(N_EXPERTS, 1, 2 * d_ff)
        x2d = _moe(x1, h2, topi, topw, rank, counts, w_up[l], b_up_l,
                   w_down[l], b_down[l].reshape(N_EXPERTS, 1, d))
    return x2d.reshape(bsz, seq, d)
```

```python
import functools
import math

import jax
import jax.numpy as jnp
from jax import lax
from jax.experimental import pallas as pl
from jax.experimental.pallas import tpu as pltpu
from jax.experimental.pallas import tpu_sc as plsc

F32 = jnp.float32
BF16 = jnp.bfloat16

HEAD_DIM = 128
DH_DIFF = HEAD_DIM // 2
CONV_WIDTH = 4
CHUNK = 64
N_BUCKETS = 32
MAX_DISTANCE = 128
N_EXPERTS = 32
TOP_K = 4
TOP_K_SHIFT = 2
SWIGLU_LIMIT = 7.0
SWIGLU_ALPHA = 1.702
EPS = 1e-6
NEG_BIG = -1e30

LANES = 128
SUBLANES = 8
VMEM_LIMIT = 56 * 1024 * 1024
SC_CORES = 2
SC_SUBCORES = 16
SC_GATHER_ROWS = 32

PROJ_TM = 2048
PROJ_TN = 1024
PROJ_CHUNK = 256
GDN_TB = 256
GDN_HG = 4
ATT_BQ = 512
ATT_BK = 512
MIX_TM = 512
MOE_RB = 256
COMB_TC = 512


def _cparams(sem):
    return pltpu.CompilerParams(dimension_semantics=sem, vmem_limit_bytes=VMEM_LIMIT)


def _bdot(a, b):
    return jnp.dot(a.astype(BF16), b.astype(BF16), preferred_element_type=F32)


def _bdot_nt(a, b):
    return lax.dot_general(a.astype(BF16), b.astype(BF16), (((1,), (1,)), ((), ())),
                           preferred_element_type=F32)


def _bdot_tn(a, b):
    return lax.dot_general(a.astype(BF16), b.astype(BF16), (((0,), (0,)), ((), ())),
                           preferred_element_type=F32)


def _proj_kernel(x_ref, g_ref, w_ref, aux_ref, o_ref, h_ref, *, mode):
    @pl.when(pl.program_id(1) == 0)
    def _():
        x = x_ref[...]
        ms = jnp.mean(x * x, axis=-1, keepdims=True)
        h_ref[...] = (x * lax.rsqrt(ms + EPS) * g_ref[...]).astype(BF16)

    h = h_ref[...]
    lo = lax.broadcasted_iota(jnp.int32, (1, LANES), 1) < DH_DIFF
    for c in range(PROJ_TN // PROJ_CHUNK):
        cs = slice(c * PROJ_CHUNK, (c + 1) * PROJ_CHUNK)
        acc = jnp.dot(h, w_ref[:, cs], preferred_element_type=F32)
        if mode == "plain":
            o_ref[:, cs] = acc.astype(o_ref.dtype)
        elif mode == "gate":
            o_ref[:, cs] = jax.nn.sigmoid(acc + aux_ref[:, cs]).astype(o_ref.dtype)
        else:
            for g in range(PROJ_CHUNK // LANES):
                sl = slice(c * PROJ_CHUNK + g * LANES, c * PROJ_CHUNK + (g + 1) * LANES)
                y = acc[:, g * LANES:(g + 1) * LANES]
                y2 = y * y
                s_lo = jnp.sum(jnp.where(lo, y2, 0.0), axis=-1, keepdims=True)
                s_hi = jnp.sum(jnp.where(lo, 0.0, y2), axis=-1, keepdims=True)
                r = jnp.where(lo, lax.rsqrt(s_lo / DH_DIFF + EPS), lax.rsqrt(s_hi / DH_DIFF + EPS))
                o_ref[:, sl] = (y * r * aux_ref[:, sl]).astype(o_ref.dtype)


def _input_projection(x2d, g_mix, w, aux, mode):
    t, d = x2d.shape
    n = w.shape[1]
    return pl.pallas_call(
        functools.partial(_proj_kernel, mode=mode),
        grid=(t // PROJ_TM, n // PROJ_TN),
        in_specs=[
            pl.BlockSpec((PROJ_TM, d), lambda i, j: (i, 0)),
            pl.BlockSpec((1, d), lambda i, j: (0, 0)),
            pl.BlockSpec((d, PROJ_TN), lambda i, j: (0, j)),
            pl.BlockSpec((1, PROJ_TN), lambda i, j: (0, j)),
        ],
        out_specs=pl.BlockSpec((PROJ_TM, PROJ_TN), lambda i, j: (i, j)),
        out_shape=jax.ShapeDtypeStruct((t, n), BF16),
        scratch_shapes=[pltpu.VMEM((PROJ_TM, d), BF16)],
        compiler_params=_cparams(("parallel", "arbitrary")),
        name="input_projection_" + mode,
    )(x2d, g_mix, w, aux)


def _small_proj_kernel(x_ref, g_ref, w_ref, wt_ref, alog_ref, dtb_ref, alog_t_ref, dtb_t_ref,
                       o_ref, ot_ref, *, n_heads):
    x = x_ref[...]
    ms = jnp.mean(x * x, axis=-1, keepdims=True)
    h = (x * lax.rsqrt(ms + EPS) * g_ref[...]).astype(BF16)

    def finish(acc, idx, alog, dtb):
        beta = jax.nn.sigmoid(acc)
        z = acc + dtb
        softplus = jnp.maximum(z, 0.0) + jnp.log1p(jnp.exp(-jnp.abs(z)))
        gdec = -jnp.exp(alog) * softplus
        return jnp.where(idx < n_heads, beta, jnp.where(idx < 2 * n_heads, gdec, 0.0))

    acc = jnp.dot(h, w_ref[...], preferred_element_type=F32)
    lane = lax.broadcasted_iota(jnp.int32, acc.shape, 1)
    o_ref[...] = finish(acc, lane, alog_ref[...], dtb_ref[...])
    acc_t = lax.dot_general(wt_ref[...], h, (((1,), (1,)), ((), ())),
                            preferred_element_type=F32)
    sub = lax.broadcasted_iota(jnp.int32, acc_t.shape, 0)
    ot_ref[...] = finish(acc_t, sub, alog_t_ref[...], dtb_t_ref[...])


def _small_projection(x2d, g_mix, w_small, w_small_t, alog, dtb, alog_t, dtb_t, n_heads):
    t, d = x2d.shape
    rows_t = w_small_t.shape[0]
    tm = PROJ_TM
    full = lambda shape: pl.BlockSpec(shape, lambda i: (0, 0))
    return pl.pallas_call(
        functools.partial(_small_proj_kernel, n_heads=n_heads),
        grid=(t // tm,),
        in_specs=[
            pl.BlockSpec((tm, d), lambda i: (i, 0)),
            full((1, d)), full((d, LANES)), full((rows_t, d)),
            full((1, LANES)), full((1, LANES)), full((rows_t, 1)), full((rows_t, 1)),
        ],
        out_specs=[pl.BlockSpec((tm, LANES), lambda i: (i, 0)),
                   pl.BlockSpec((rows_t, tm), lambda i: (0, i))],
        out_shape=[jax.ShapeDtypeStruct((t, LANES), F32),
                   jax.ShapeDtypeStruct((rows_t, t), F32)],
        compiler_params=_cparams(("parallel",)),
        name="beta_decay_projection",
    )(x2d, g_mix, w_small, w_small_t, alog, dtb, alog_t, dtb_t)


def _gdn_kernel(q_ref, k_ref, v_ref, z_ref, sm_ref, smt_ref, cwq_ref, cwk_ref, cwv_ref, gout_ref,
                o_ref, state_ref, qp_ref, kp_ref, vp_ref, vn_ref, *, n_heads):
    hg = pl.program_id(1)
    s = pl.program_id(2)
    tb = GDN_TB
    pad = SUBLANES
    width = GDN_HG * HEAD_DIM

    @pl.when(s == 0)
    def _():
        state_ref[...] = jnp.zeros_like(state_ref)
        for p_ref in (qp_ref, kp_ref, vp_ref):
            p_ref[0:pad, :] = jnp.zeros((pad, width), F32)

    def conv_silu(x_ref, p_ref, cw_ref):
        p_ref[pad:pad + tb, :] = x_ref[...].astype(F32)
        acc = cw_ref[CONV_WIDTH - 1:CONV_WIDTH, :] * p_ref[pad:pad + tb, :]
        for jj in range(CONV_WIDTH - 1):
            off = pad - (CONV_WIDTH - 1) + jj
            acc = acc + cw_ref[jj:jj + 1, :] * p_ref[off:off + tb, :]
        p_ref[0:pad, :] = p_ref[tb:tb + pad, :]
        return acc * jax.nn.sigmoid(acc)

    q_all = conv_silu(q_ref, qp_ref, cwq_ref)
    k_all = conv_silu(k_ref, kp_ref, cwk_ref)
    v_all = conv_silu(v_ref, vp_ref, cwv_ref)

    r = lax.broadcasted_iota(jnp.int32, (tb, tb), 0)
    c = lax.broadcasted_iota(jnp.int32, (tb, tb), 1)
    shift = int(math.log2(CHUNK))
    same = (r >> shift) == (c >> shift)
    incl = jnp.logical_and(same, c <= r)
    strict = jnp.logical_and(same, c < r)

    small = sm_ref[...]
    small_t = smt_ref[...]
    lane = lax.broadcasted_iota(jnp.int32, small.shape, 1)
    def split3(a):
        hi = a.astype(BF16)
        r1 = a - hi.astype(F32)
        mid = r1.astype(BF16)
        lo = (r1 - mid.astype(F32)).astype(BF16)
        return hi.astype(F32), mid.astype(F32), lo.astype(F32)

    part = 2 * n_heads
    s_hi, s_mid, s_lo = split3(small)
    small3 = jnp.where(lane < part, s_hi,
                       jnp.where(lane < 2 * part, pltpu.roll(s_mid, part, 1),
                                 jnp.where(lane < 3 * part, pltpu.roll(s_lo, 2 * part, 1), 0.0)))
    both = _bdot(jnp.concatenate([incl.astype(F32), same.astype(F32)], axis=0), small3)
    gcum = both[:tb]
    gtot = both[tb:]
    gcum_t = _bdot(jnp.concatenate(split3(small_t), axis=0),
                   jnp.logical_and(same, r <= c).astype(F32))
    sub3 = lax.broadcasted_iota(jnp.int32, gcum_t.shape, 0)

    heads = range(GDN_HG)
    hsl = [slice(hh * HEAD_DIM, (hh + 1) * HEAD_DIM) for hh in heads]
    qs = [q_all[:, hs] for hs in hsl]
    ks = [k_all[:, hs] for hs in hsl]
    vs = [v_all[:, hs] for hs in hsl]
    qs = [q * lax.rsqrt(jnp.sum(q * q, axis=-1, keepdims=True) + EPS) * (HEAD_DIM ** -0.5) for q in qs]
    ks = [k * lax.rsqrt(jnp.sum(k * k, axis=-1, keepdims=True) + EPS) for k in ks]

    def col_of(arr, idx):
        return jnp.sum(jnp.where(lane == idx, arr, 0.0), axis=-1, keepdims=True)

    def terms_of(pos, idx):
        return jnp.logical_or(pos == idx, jnp.logical_or(pos == idx + part, pos == idx + 2 * part))

    head_ids = [hg * GDN_HG + hh for hh in heads]
    betas = [col_of(small, hd) for hd in head_ids]
    gcs = [jnp.sum(jnp.where(terms_of(lane, hd + n_heads), gcum, 0.0), axis=-1, keepdims=True)
           for hd in head_ids]
    gls = [jnp.sum(jnp.where(terms_of(lane, hd + n_heads), gtot, 0.0), axis=-1, keepdims=True)
           for hd in head_ids]
    gc_rows = [jnp.sum(jnp.where(terms_of(sub3, hd + n_heads), gcum_t, 0.0), axis=0, keepdims=True)
               for hd in head_ids]

    decays = [jnp.where(incl, jnp.exp(jnp.minimum(gc - gr, 0.0)), 0.0) for gc, gr in zip(gcs, gc_rows)]
    kbs = [k * b for k, b in zip(ks, betas)]
    kks = [_bdot_nt(kb, k) for kb, k in zip(kbs, ks)]
    pws = [jnp.where(strict, -(kk * dc), 0.0) for kk, dc in zip(kks, decays)]
    n_chunks = tb // CHUNK
    cat_row = lax.broadcasted_iota(jnp.int32, (CHUNK, tb), 0)
    cat_lane = lax.broadcasted_iota(jnp.int32, (CHUNK, tb), 1)
    lane_chunk = cat_lane >> shift

    def block_diag(m_cat):
        return jnp.concatenate([jnp.where(lane_chunk == ci, m_cat, 0.0) for ci in range(n_chunks)], axis=0)

    def cat_of(m_bd):
        out = m_bd[0:CHUNK]
        for ci in range(1, n_chunks):
            out = out + m_bd[ci * CHUNK:(ci + 1) * CHUNK]
        return out

    pcats = [cat_of(pw) for pw in pws]
    eye_cat = ((cat_lane & (CHUNK - 1)) == cat_row).astype(F32)
    tcats = [eye_cat + pc for pc in pcats]
    pcats = [_bdot(pc, block_diag(pc)) for pc in pcats]
    n_levels = int(math.log2(CHUNK))
    for lev in range(1, n_levels):
        bds = [block_diag(pc) for pc in pcats]
        if lev < n_levels - 1:
            prods = [_bdot(jnp.concatenate([pc, tc], axis=0), bd) for pc, tc, bd in zip(pcats, tcats, bds)]
            pcats = [pr[:CHUNK] for pr in prods]
            tcats = [tc + pr[CHUNK:] for tc, pr in zip(tcats, prods)]
        else:
            tcats = [tc + _bdot(tc, bd) for tc, bd in zip(tcats, bds)]
    tmats = [block_diag(tc) for tc in tcats]
    egcs = [jnp.exp(gc) for gc in gcs]
    uws = [_bdot(tm, jnp.concatenate([v * b, kb * eg], axis=1))
           for tm, v, b, kb, eg in zip(tmats, vs, betas, kbs, egcs)]
    us = [uw[:, :HEAD_DIM] for uw in uws]
    ws = [uw[:, HEAD_DIM:] for uw in uws]
    qkm = [_bdot_nt(q, k) for q, k in zip(qs, ks)]
    qkm = [jnp.where(incl, x * dc, 0.0) for x, dc in zip(qkm, decays)]
    q_decs = [q * eg for q, eg in zip(qs, egcs)]
    k_ends = [k * jnp.exp(gl - gc) for k, gl, gc in zip(ks, gls, gcs)]

    for hh in heads:
        vn_ref[hh] = jnp.zeros((tb, HEAD_DIM), F32)
    outs = [[] for _ in heads]
    for ci in range(tb // CHUNK):
        cs = slice(ci * CHUNK, (ci + 1) * CHUNK)
        sts = [state_ref[hh] for hh in heads]
        ws_qs = [_bdot(jnp.concatenate([ws[hh][cs], q_decs[hh][cs]], axis=0), sts[hh]) for hh in heads]
        v_news = [us[hh][cs] - ws_qs[hh][:CHUNK] for hh in heads]
        for hh in heads:
            vn_ref[hh, cs, :] = v_news[hh]
        intra = [_bdot(qkm[hh][cs], vn_ref[hh]) for hh in heads]
        upd = [_bdot_tn(k_ends[hh][cs], v_news[hh]) for hh in heads]
        for hh in heads:
            outs[hh].append(ws_qs[hh][CHUNK:] + intra[hh])
            g_last = gls[hh][ci * CHUNK:ci * CHUNK + 1, :]
            state_ref[hh] = sts[hh] * jnp.exp(g_last) + upd[hh]
    for hh in heads:
        o = jnp.concatenate(outs[hh], axis=0)
        o = o * lax.rsqrt(jnp.mean(o * o, axis=-1, keepdims=True) + EPS) * gout_ref[...]
        zz = z_ref[:, hsl[hh]].astype(F32)
        o_ref[:, hsl[hh]] = (o * (zz * jax.nn.sigmoid(zz))).astype(o_ref.dtype)


def _gated_delta(big, small, small_t, conv_w, g_out, bsz, seq, n_heads, d_model):
    t = bsz * seq
    tb = GDN_TB
    ns = seq // tb
    width = GDN_HG * HEAD_DIM
    nhg = n_heads // GDN_HG
    blocks_per_group = d_model // width
    rows_t = small_t.shape[0]

    def colspec(group):
        return pl.BlockSpec((tb, width), lambda b, h, s: (b * ns + s, group * blocks_per_group + h))

    def cwspec(group):
        return pl.BlockSpec((CONV_WIDTH, width), lambda b, h, s: (0, group * blocks_per_group + h))

    return pl.pallas_call(
        functools.partial(_gdn_kernel, n_heads=n_heads),
        grid=(bsz, nhg, ns),
        in_specs=[
            colspec(0), colspec(1), colspec(2), colspec(3),
            pl.BlockSpec((tb, LANES), lambda b, h, s: (b * ns + s, 0)),
            pl.BlockSpec((rows_t, tb), lambda b, h, s: (0, b * ns + s)),
            cwspec(0), cwspec(1), cwspec(2),
            pl.BlockSpec((1, HEAD_DIM), lambda b, h, s: (0, 0)),
        ],
        out_specs=pl.BlockSpec((tb, width), lambda b, h, s: (b * ns + s, h)),
        out_shape=jax.ShapeDtypeStruct((t, d_model), BF16),
        scratch_shapes=[
            pltpu.VMEM((GDN_HG, HEAD_DIM, HEAD_DIM), F32),
            pltpu.VMEM((tb + SUBLANES, width), F32),
            pltpu.VMEM((tb + SUBLANES, width), F32),
            pltpu.VMEM((tb + SUBLANES, width), F32),
            pltpu.VMEM((GDN_HG, tb, HEAD_DIM), F32),
        ],
        compiler_params=_cparams(("parallel", "parallel", "arbitrary")),
        name="gated_delta",
    )(big, big, big, big, small, small_t, conv_w, conv_w, conv_w, g_out)


def _t5_bucket(n):
    max_exact = N_BUCKETS // 2
    nf = jnp.maximum(n, 1).astype(F32)
    large = max_exact + (jnp.log(nf / max_exact) / math.log(MAX_DISTANCE / max_exact)
                         * (N_BUCKETS - max_exact)).astype(jnp.int32)
    large = jnp.minimum(large, N_BUCKETS - 1)
    return jnp.where(n < max_exact, n, large)


def _attn_kernel(rb_ref, q_ref, k_ref, v_ref, lam_ref, gsub_ref, o_ref,
                 bias_ref, m_ref, acc_ref, sa_ref, sb_ref, *, lam_init):
    h = pl.program_id(0)
    b = pl.program_id(1)
    qi = pl.program_id(2)
    bq, bk = ATT_BQ, ATT_BK

    @pl.when(jnp.logical_and(b == 0, qi == 0))
    def _():
        i = lax.broadcasted_iota(jnp.int32, (bq, bk), 0)
        jj = lax.broadcasted_iota(jnp.int32, (bq, bk), 1)
        far = rb_ref[N_BUCKETS - 1, h]
        bias_ref[2] = jnp.zeros((bq, bk), F32)
        for slot in range(2):
            n = i - jj + slot * bk
            bucket = _t5_bucket(jnp.maximum(n, 0))
            bias = jnp.zeros((bq, bk), F32)
            for cc in range(N_BUCKETS):
                bias = jnp.where(bucket == cc, rb_ref[cc, h] - far, bias)
            if slot == 0:
                bias = jnp.where(n >= 0, bias, NEG_BIG)
            bias_ref[slot] = bias

    m_ref[...] = jnp.full(m_ref.shape, NEG_BIG, F32)
    acc_ref[...] = jnp.zeros(acc_ref.shape, F32)

    q = q_ref[...]
    lane = lax.broadcasted_iota(jnp.int32, q.shape, 1)
    zero = jnp.zeros_like(q)
    qs = jnp.concatenate([jnp.where(lane < DH_DIFF, q, zero), jnp.where(lane < DH_DIFF, zero, q)], axis=0)
    ones_col = (lax.broadcasted_iota(jnp.int32, (bk, HEAD_DIM), 1) == 0).astype(BF16)

    def scores(j):
        ks = pl.multiple_of(j * bk, bk)
        return lax.dot_general(qs, k_ref[pl.ds(ks, bk), :], (((1,), (1,)), ((), ())),
                               preferred_element_type=F32)

    def absorb(j, sc_ref, biased=True):
        ks = pl.multiple_of(j * bk, bk)
        v_ext = jnp.concatenate([v_ref[pl.ds(ks, bk), :], ones_col], axis=1)
        if biased:
            bias = bias_ref[jnp.minimum(qi - j, 2)]
            sc = jnp.concatenate([sc_ref[0:bq, :] + bias, sc_ref[bq:2 * bq, :] + bias], axis=0)
        else:
            sc = sc_ref[...]
        m_old = m_ref[...]
        m_new = jnp.maximum(m_old, jnp.max(sc, axis=-1, keepdims=True))
        p = jnp.exp(sc - m_new)
        acc_ref[...] = (jnp.exp(m_old - m_new) * acc_ref[...]
                        + jnp.dot(p.astype(BF16), v_ext, preferred_element_type=F32))
        m_ref[...] = m_new

    n_tiles = qi + 1
    sa_ref[...] = scores(0)

    def pair_body(jj, carry, biased):
        j0 = 2 * jj
        sb_ref[...] = scores(j0 + 1)
        absorb(j0, sa_ref, biased)
        sa_ref[...] = scores(jnp.minimum(j0 + 2, qi))
        absorb(j0 + 1, sb_ref, biased)
        return carry

    n_far_pairs = jnp.maximum(qi - 1, 0) // 2
    lax.fori_loop(0, n_far_pairs, functools.partial(pair_body, biased=False), 0)
    lax.fori_loop(n_far_pairs, n_tiles // 2, functools.partial(pair_body, biased=True), 0)

    @pl.when(n_tiles % 2 == 1)
    def _():
        absorb(qi, sa_ref)

    lam_p = lam_ref[...]
    s1 = jnp.sum(lam_p[0:1] * lam_p[1:2], axis=-1, keepdims=True)
    s2 = jnp.sum(lam_p[2:3] * lam_p[3:4], axis=-1, keepdims=True)
    lam = jnp.exp(s1) - jnp.exp(s2) + lam_init
    acc = acc_ref[...]
    num = acc[:, :HEAD_DIM]
    den = acc[:, HEAD_DIM:HEAD_DIM + 1]
    o = num[:bq] / den[:bq] - lam * (num[bq:] / den[bq:])
    o = o * lax.rsqrt(jnp.mean(o * o, axis=-1, keepdims=True) + EPS) * gsub_ref[...]
    o_ref[...] = (o * (1.0 - lam_init)).astype(o_ref.dtype)


def _diff_attention(proj_qk, proj_plain, rel_bias, lam_params, g_subln, bsz, seq, n_heads, d_model,
                    lam_init):
    t = bsz * seq
    nq = seq // ATT_BQ
    per = d_model // HEAD_DIM
    vcol = 4 * per
    return pl.pallas_call(
        functools.partial(_attn_kernel, lam_init=lam_init),
        grid=(n_heads, bsz, nq),
        in_specs=[
            pl.BlockSpec(memory_space=pltpu.SMEM),
            pl.BlockSpec((ATT_BQ, HEAD_DIM), lambda h, b, i: (b * nq + i, h)),
            pl.BlockSpec((seq, HEAD_DIM), lambda h, b, i: (b, per + h)),
            pl.BlockSpec((seq, HEAD_DIM), lambda h, b, i: (b, vcol + h)),
            pl.BlockSpec((4, DH_DIFF), lambda h, b, i: (0, 0)),
            pl.BlockSpec((1, HEAD_DIM), lambda h, b, i: (0, 0)),
        ],
        out_specs=pl.BlockSpec((ATT_BQ, HEAD_DIM), lambda h, b, i: (b * nq + i, h)),
        out_shape=jax.ShapeDtypeStruct((t, d_model), BF16),
        scratch_shapes=[
            pltpu.VMEM((3, ATT_BQ, ATT_BK), F32),
            pltpu.VMEM((2 * ATT_BQ, 1), F32),
            pltpu.VMEM((2 * ATT_BQ, 2 * HEAD_DIM), F32),
            pltpu.VMEM((2 * ATT_BQ, ATT_BK), F32),
            pltpu.VMEM((2 * ATT_BQ, ATT_BK), F32),
        ],
        compiler_params=_cparams(("arbitrary", "arbitrary", "arbitrary")),
        name="diff_attention",
    )(rel_bias, proj_qk, proj_qk, proj_plain, lam_params, g_subln)


def _mix_kernel(ga_ref, gb_ref, oa_ref, od_ref, x_ref, wo_ref, gffn_ref, wr_ref, br_ref,
                x1_ref, h2_ref, topi_ref, topw_ref, rank_ref, cnt_ref, carry_ref):
    i = pl.program_id(0)
    tm = MIX_TM

    @pl.when(i == 0)
    def _():
        carry_ref[...] = jnp.zeros_like(carry_ref)

    mix = (ga_ref[...].astype(F32) * oa_ref[...].astype(F32)
           + gb_ref[...].astype(F32) * od_ref[...].astype(F32))
    x1 = x_ref[...] + jnp.dot(mix.astype(BF16), wo_ref[...], preferred_element_type=F32)
    x1_ref[...] = x1
    h2 = x1 * lax.rsqrt(jnp.mean(x1 * x1, axis=-1, keepdims=True) + EPS) * gffn_ref[...]
    h2_ref[...] = h2

    logits = lax.dot_general(wr_ref[...], h2, (((1,), (1,)), ((), ())),
                             preferred_element_type=F32, precision=lax.Precision.HIGHEST) + br_ref[...]
    eidx = lax.broadcasted_iota(jnp.int32, logits.shape, 0).astype(F32)
    vals, hots = [], []
    cur = logits
    for kk in range(TOP_K):
        mx = jnp.max(cur, axis=0, keepdims=True)
        idx = jnp.min(jnp.where(cur == mx, eidx, float(N_EXPERTS)), axis=0, keepdims=True)
        hot = eidx == idx
        vals.append(mx)
        hots.append(hot)
        topi_ref[kk:kk + 1, :] = idx.astype(jnp.int32)
        cur = jnp.where(hot, -jnp.inf, cur)
    exps = [jnp.exp(vv - vals[0]) for vv in vals]
    denom = exps[0] + exps[1] + exps[2] + exps[3]
    for kk in range(TOP_K):
        topw_ref[kk:kk + 1, :] = exps[kk] / denom

    sel = hots[0]
    for kk in range(1, TOP_K):
        sel = jnp.logical_or(sel, hots[kk])
    sel_f = sel.astype(F32)
    r = lax.broadcasted_iota(jnp.int32, (tm, tm), 0)
    c = lax.broadcasted_iota(jnp.int32, (tm, tm), 1)
    before = _bdot(sel_f, (r < c).astype(F32)) + carry_ref[...]
    for kk in range(TOP_K):
        rank_ref[kk:kk + 1, :] = jnp.sum(jnp.where(hots[kk], before, 0.0), axis=0,
                                         keepdims=True).astype(jnp.int32)
    carry_ref[...] = carry_ref[...] + jnp.sum(sel_f, axis=-1, keepdims=True)
    cnt_ref[...] = carry_ref[...].astype(jnp.int32)


def _mix_project_route(proj_gate, oa, od, x2d, w_o, g_ffn, w_r_t, b_r, d_model):
    t = x2d.shape[0]
    tm = MIX_TM
    full = lambda shape: pl.BlockSpec(shape, lambda i: (0, 0))
    row = lambda: pl.BlockSpec((tm, d_model), lambda i: (i, 0))
    krow = lambda: pl.BlockSpec((TOP_K, tm), lambda i: (0, i))
    return pl.pallas_call(
        _mix_kernel,
        grid=(t // tm,),
        in_specs=[
            pl.BlockSpec((tm, d_model), lambda i: (i, 0)),
            pl.BlockSpec((tm, d_model), lambda i: (i, 1)),
            row(), row(), row(),
            full((d_model, d_model)), full((1, d_model)), full((N_EXPERTS, d_model)), full((N_EXPERTS, 1)),
        ],
        out_specs=[row(), row(), krow(), krow(), krow(), full((N_EXPERTS, 1))],
        out_shape=[
            jax.ShapeDtypeStruct((t, d_model), F32),
            jax.ShapeDtypeStruct((t, d_model), F32),
            jax.ShapeDtypeStruct((TOP_K, t), jnp.int32),
            jax.ShapeDtypeStruct((TOP_K, t), F32),
            jax.ShapeDtypeStruct((TOP_K, t), jnp.int32),
            jax.ShapeDtypeStruct((N_EXPERTS, 1), jnp.int32),
        ],
        scratch_shapes=[pltpu.VMEM((N_EXPERTS, 1), F32)],
        compiler_params=_cparams(("arbitrary",)),
        name="merge_outproj_route",
    )(proj_gate, proj_gate, oa, od, x2d, w_o, g_ffn, w_r_t, b_r)


def _row_copy(src_ref, src_row, dst_ref, dst_row, sem):
    return pltpu.make_async_copy(src_ref.at[pl.ds(src_row, 1)], dst_ref.at[pl.ds(dst_row, 1)], sem)


def _expert_kernel(be_ref, src_ref, src_next_ref, h2_ref, wup_ref, bup_ref, wdn_ref, bdn_ref, y_ref,
                   x0, x1, wup_bf, wdn_bf, gsem):
    i = pl.program_id(0)
    last = pl.num_programs(0) - 1
    rb = MOE_RB
    d_ff = wdn_ref.shape[1]

    def gather_wait(x_ref, s):
        pltpu.make_async_copy(h2_ref.at[pl.ds(0, rb)], x_ref, gsem.at[s]).wait()

    @pl.when(i == 0)
    def _():
        for a in range(rb):
            _row_copy(h2_ref, src_ref[a], x0, a, gsem.at[0]).start()

    @pl.when(jnp.logical_or(i == 0, be_ref[i] != be_ref[jnp.maximum(i - 1, 0)]))
    def _():
        rr = lax.broadcasted_iota(jnp.int32, (2 * LANES, 2 * LANES), 0)
        cc = lax.broadcasted_iota(jnp.int32, (2 * LANES, 2 * LANES), 1)
        pick = jnp.where(cc < LANES, 2 * cc, 2 * (cc - LANES) + 1)
        perm = (rr == pick).astype(BF16)
        for g in range(wup_ref.shape[2] // (2 * LANES)):
            cs = slice(g * 2 * LANES, (g + 1) * 2 * LANES)
            wup_bf[:, cs] = jnp.dot(wup_ref[0, :, cs].astype(BF16), perm,
                                    preferred_element_type=F32).astype(BF16)
        wdn_bf[...] = wdn_ref[0].astype(BF16)

    def block(xa, xb, s):
        gather_wait(xa, s)
        for a in range(rb):
            _row_copy(h2_ref, src_next_ref[a], xb, a, gsem.at[1 - s]).start()
        hid = jnp.dot(xa[...].astype(BF16), wup_bf[...], preferred_element_type=F32) + bup_ref[0]
        acts = []
        for g in range(hid.shape[1] // (2 * LANES)):
            glu = jnp.minimum(hid[:, g * 2 * LANES:g * 2 * LANES + LANES], SWIGLU_LIMIT)
            lin = jnp.clip(hid[:, g * 2 * LANES + LANES:(g + 1) * 2 * LANES], -SWIGLU_LIMIT, SWIGLU_LIMIT)
            acts.append(glu * jax.nn.sigmoid(SWIGLU_ALPHA * glu) * (lin + 1.0))
        act = jnp.concatenate(acts, axis=1)
        assert act.shape[1] == d_ff
        y_ref[...] = jnp.dot(act.astype(BF16), wdn_bf[...], preferred_element_type=F32) + bdn_ref[0]

        @pl.when(i == last)
        def _():
            gather_wait(xb, 1 - s)

    @pl.when((i & 1) == 1)
    def _():
        block(x1, x0, 1)

    @pl.when((i & 1) == 0)
    def _():
        block(x0, x1, 0)


def _experts(block_e, src_tok, h2, w_up, b_up, w_down, b_down):
    n_rows = src_tok.shape[0]
    d = w_up.shape[1]
    nb = n_rows // MOE_RB
    two_ff = w_up.shape[2]
    d_ff = w_down.shape[1]
    smem_rows = lambda fn: pl.BlockSpec((MOE_RB,), fn, memory_space=pltpu.SMEM)
    grid_spec = pltpu.PrefetchScalarGridSpec(
        num_scalar_prefetch=1,
        grid=(nb,),
        in_specs=[
            smem_rows(lambda i, be: (i,)),
            smem_rows(lambda i, be: (jnp.minimum(i + 1, nb - 1),)),
            pl.BlockSpec(memory_space=pl.ANY),
            pl.BlockSpec((1, d, two_ff), lambda i, be: (be[i], 0, 0)),
            pl.BlockSpec((1, 1, two_ff), lambda i, be: (be[i], 0, 0)),
            pl.BlockSpec((1, d_ff, d), lambda i, be: (be[i], 0, 0)),
            pl.BlockSpec((1, 1, d), lambda i, be: (be[i], 0, 0)),
        ],
        out_specs=pl.BlockSpec((MOE_RB, d), lambda i, be: (i, 0)),
        scratch_shapes=[
            pltpu.VMEM((MOE_RB, d), F32), pltpu.VMEM((MOE_RB, d), F32),
            pltpu.VMEM((d, two_ff), BF16),
            pltpu.VMEM((d_ff, d), BF16),
            pltpu.SemaphoreType.DMA((2,)),
        ],
    )
    return pl.pallas_call(
        _expert_kernel,
        grid_spec=grid_spec,
        out_shape=jax.ShapeDtypeStruct((n_rows, d), F32),
        compiler_params=_cparams(("arbitrary",)),
        name="moe_experts",
    )(block_e, src_tok, src_tok, h2, w_up, b_up, w_down, b_down)


def _sc_gather_rows(table, idx):
    n_idx = idx.shape[0]
    d = table.shape[1]
    n_workers = SC_CORES * SC_SUBCORES
    per_worker = n_idx // n_workers
    n_chunks = per_worker // SC_GATHER_ROWS
    assert n_idx % n_workers == 0 and per_worker % SC_GATHER_ROWS == 0
    mesh = plsc.VectorSubcoreMesh(core_axis_name="c", subcore_axis_name="s",
                                  num_cores=SC_CORES, num_subcores=SC_SUBCORES)

    def body(table_hbm, idx_hbm, out_hbm, idx_v, rows_v, sem):
        wid = lax.axis_index("s") * SC_CORES + lax.axis_index("c")
        base = wid * per_worker
        pltpu.sync_copy(idx_hbm.at[pl.ds(base, per_worker)], idx_v)

        @pl.loop(0, n_chunks)
        def _(ci):
            off = pl.multiple_of(ci * SC_GATHER_ROWS, SC_GATHER_ROWS)
            pltpu.async_copy(table_hbm.at[idx_v.at[pl.ds(off, SC_GATHER_ROWS)]], rows_v, sem).wait()
            pltpu.sync_copy(rows_v, out_hbm.at[pl.ds(base + off, SC_GATHER_ROWS)])

    return pl.kernel(
        body,
        out_type=jax.ShapeDtypeStruct((n_idx, d), table.dtype),
        mesh=mesh,
        scratch_types=[
            pltpu.VMEM((per_worker,), jnp.int32),
            pltpu.VMEM((SC_GATHER_ROWS, d), table.dtype),
            pltpu.SemaphoreType.DMA,
        ],
        name="moe_slot_gather",
    )(table, idx)


def _combine_kernel(x1_ref, w_ref, y0_ref, y1_ref, y2_ref, y3_ref, o_ref):
    w = w_ref[...]
    out = x1_ref[...]
    for kk, y_ref in enumerate((y0_ref, y1_ref, y2_ref, y3_ref)):
        out = out + w[:, kk:kk + 1] * y_ref[...]
    o_ref[...] = out


def _combine(x1, w_tok, y_slots):
    t, d = x1.shape
    tc = COMB_TC
    nt = t // tc
    yspec = lambda kk: pl.BlockSpec((tc, d), lambda i: (kk * nt + i, 0))
    return pl.pallas_call(
        _combine_kernel,
        grid=(nt,),
        in_specs=[
            pl.BlockSpec((tc, d), lambda i: (i, 0)),
            pl.BlockSpec((tc, TOP_K), lambda i: (i, 0)),
            yspec(0), yspec(1), yspec(2), yspec(3),
        ],
        out_specs=pl.BlockSpec((tc, d), lambda i: (i, 0)),
        out_shape=jax.ShapeDtypeStruct((t, d), F32),
        compiler_params=_cparams(("parallel",)),
        name="moe_combine",
    )(x1, w_tok, y_slots, y_slots, y_slots, y_slots)


def _moe(x1, h2, topi, topw, rank, counts, w_up, b_up, w_down, b_down):
    t, d = x1.shape
    n_assign = t * TOP_K
    nb = -(-n_assign // MOE_RB) + N_EXPERTS
    n_rows = nb * MOE_RB
    counts = counts[:, 0]
    padded = (counts + MOE_RB - 1) // MOE_RB * MOE_RB
    padded_end = jnp.cumsum(padded)
    padded_start = padded_end - padded
    expert_ids = jnp.arange(N_EXPERTS, dtype=jnp.int32)[:, None, None]
    start_of = jnp.sum(jnp.where(topi[None] == expert_ids, padded_start[:, None, None], 0), axis=0)
    dest = (start_of + rank).astype(jnp.int32)
    n_used = (padded_end[-1] // MOE_RB).astype(jnp.int32)
    blk = jnp.minimum(jnp.arange(nb, dtype=jnp.int32), n_used - 1)
    block_e = jnp.minimum(jnp.sum(padded_end[None, :] <= (blk * MOE_RB)[:, None], axis=1),
                          N_EXPERTS - 1).astype(jnp.int32)
    slot_of = jnp.full((n_rows,), -1, jnp.int32).at[dest.reshape(-1)].set(
        jnp.arange(n_assign, dtype=jnp.int32), unique_indices=True)
    src_tok = jnp.where(slot_of < 0, 0, slot_of % t)

    y_rows = _experts(block_e, src_tok, h2, w_up, b_up, w_down, b_down)
    y_slots = _sc_gather_rows(y_rows, dest.reshape(-1))
    return _combine(x1, topw.T, y_slots)


def kernel(x, g_mix, w_in, b_gate, conv_w, a_log, dt_bias, g_delta_out, q_norm, k_norm, lambda_q1, lambda_k1, lambda_q2, lambda_k2, g_subln, rel_bias, w_o, g_ffn, w_router, b_router, w_up, b_up, w_down, b_down):
    bsz, seq, d = x.shape
    depth = g_mix.shape[0]
    n_heads = d // HEAD_DIM
    t = bsz * seq
    d_ff = w_down.shape[2]
    assert d % PROJ_TN == 0 and t % PROJ_TM == 0 and seq % GDN_TB == 0 and seq % ATT_BQ == 0
    assert t % MIX_TM == 0 and t % COMB_TC == 0 and n_heads % GDN_HG == 0
    assert (t * TOP_K) % MOE_RB == 0
    assert 2 * n_heads <= 2 * SUBLANES

    x2d = x.reshape(t, d)
    for l in range(depth):
        wl = w_in[l]
        c0 = 4 * d
        c1 = c0 + 2 * n_heads
        c2 = c1 + 2 * d
        c3 = c2 + d
        w_small = jnp.pad(wl[:, c0:c1], ((0, 0), (0, LANES - 2 * n_heads)))
        gm = g_mix[l].reshape(1, d)
        w_plain = jnp.concatenate([wl[:, :c0], wl[:, c2:c3]], axis=1).astype(BF16)
        proj_plain = _input_projection(x2d, gm, w_plain, jnp.zeros((1, 5 * d), F32), "plain")
        qk_gain = jnp.concatenate([jnp.tile(q_norm[l] * (DH_DIFF ** -0.5), 2 * n_heads),
                                   jnp.tile(k_norm[l], 2 * n_heads)]).reshape(1, 2 * d)
        proj_qk = _input_projection(x2d, gm, wl[:, c1:c2].astype(BF16), qk_gain, "qknorm")
        proj_gate = _input_projection(x2d, gm, wl[:, c3:].astype(BF16), b_gate[l].reshape(1, 2 * d), "gate")

        head_pad = jnp.zeros((LANES - 2 * n_heads,), F32)
        alog = jnp.concatenate([jnp.zeros((n_heads,), F32), a_log[l], head_pad])
        dtb = jnp.concatenate([jnp.zeros((n_heads,), F32), dt_bias[l], head_pad])
        rows_t = 2 * n_heads
        small, small_t = _small_projection(
            x2d, g_mix[l].reshape(1, d), w_small.astype(BF16), w_small[:, :rows_t].T.astype(BF16),
            alog.reshape(1, LANES), dtb.reshape(1, LANES),
            alog[:rows_t].reshape(rows_t, 1), dtb[:rows_t].reshape(rows_t, 1), n_heads)

        oa = _gated_delta(proj_plain, small, small_t, conv_w[l], g_delta_out[l].reshape(1, HEAD_DIM),
                          bsz, seq, n_heads, d)

        lam_init = 0.8 - 0.6 * math.exp(-0.3 * l)
        lam_params = jnp.stack([lambda_q1[l], lambda_k1[l], lambda_q2[l], lambda_k2[l]])
        od = _diff_attention(proj_qk, proj_plain, rel_bias, lam_params, g_subln[l].reshape(1, HEAD_DIM),
                             bsz, seq, n_heads, d, lam_init)

        x1, h2, topi, topw, rank, counts = _mix_project_route(
            proj_gate, oa, od, x2d, w_o[l].astype(BF16), g_ffn[l].reshape(1, d),
            w_router[l].T, b_router[l].reshape(N_EXPERTS, 1), d)

        b_up_l = b_up[l].reshape(N_EXPERTS, 2 * d_ff // (2 * LANES), LANES, 2)
        b_up_l = jnp.swapaxes(b_up_l, 2, 3).reshape(N_EXPERTS, 1, 2 * d_ff)
        x2d = _moe(x1, h2, topi, topw, rank, counts, w_up[l], b_up_l,
                   w_down[l], b_down[l].reshape(N_EXPERTS, 1, d))
    return x2d.reshape(bsz, seq, d)
```

```python
import functools
import math

import jax
import jax.numpy as jnp
from jax import lax
from jax.experimental import pallas as pl
from jax.experimental.pallas import tpu as pltpu
from jax.experimental.pallas import tpu_sc as plsc

F32 = jnp.float32
BF16 = jnp.bfloat16

HEAD_DIM = 128
DH_DIFF = HEAD_DIM // 2
CONV_WIDTH = 4
CHUNK = 64
N_BUCKETS = 32
MAX_DISTANCE = 128
N_EXPERTS = 32
TOP_K = 4
TOP_K_SHIFT = 2
SWIGLU_LIMIT = 7.0
SWIGLU_ALPHA = 1.702
EPS = 1e-6
NEG_BIG = -1e30

LANES = 128
SUBLANES = 8
VMEM_LIMIT = 56 * 1024 * 1024
SC_CORES = 2
SC_SUBCORES = 16
SC_GATHER_ROWS = 32

PROJ_TM = 2048
PROJ_TN = 1024
PROJ_CHUNK = 256
GDN_TB = 256
GDN_HG = 4
ATT_BQ = 512
ATT_BK = 512
MIX_TM = 512
MOE_RB = 256
COMB_TC = 512


def _cparams(sem):
    return pltpu.CompilerParams(dimension_semantics=sem, vmem_limit_bytes=VMEM_LIMIT)


def _bdot(a, b):
    return jnp.dot(a.astype(BF16), b.astype(BF16), preferred_element_type=F32)


def _bdot_nt(a, b):
    return lax.dot_general(a.astype(BF16), b.astype(BF16), (((1,), (1,)), ((), ())),
                           preferred_element_type=F32)


def _bdot_tn(a, b):
    return lax.dot_general(a.astype(BF16), b.astype(BF16), (((0,), (0,)), ((), ())),
                           preferred_element_type=F32)


def _proj_kernel(x_ref, g_ref, w_ref, aux_ref, o_ref, h_ref, *, mode):
    @pl.when(pl.program_id(1) == 0)
    def _():
        x = x_ref[...]
        ms = jnp.mean(x * x, axis=-1, keepdims=True)
        h_ref[...] = (x * lax.rsqrt(ms + EPS) * g_ref[...]).astype(BF16)

    h = h_ref[...]
    lo = lax.broadcasted_iota(jnp.int32, (1, LANES), 1) < DH_DIFF
    for c in range(PROJ_TN // PROJ_CHUNK):
        cs = slice(c * PROJ_CHUNK, (c + 1) * PROJ_CHUNK)
        acc = jnp.dot(h, w_ref[:, cs], preferred_element_type=F32)
        if mode == "plain":
            o_ref[:, cs] = acc.astype(o_ref.dtype)
        elif mode == "gate":
            o_ref[:, cs] = jax.nn.sigmoid(acc + aux_ref[:, cs]).astype(o_ref.dtype)
        else:
            for g in range(PROJ_CHUNK // LANES):
                sl = slice(c * PROJ_CHUNK + g * LANES, c * PROJ_CHUNK + (g + 1) * LANES)
                y = acc[:, g * LANES:(g + 1) * LANES]
                y2 = y * y
                s_lo = jnp.sum(jnp.where(lo, y2, 0.0), axis=-1, keepdims=True)
                s_hi = jnp.sum(jnp.where(lo, 0.0, y2), axis=-1, keepdims=True)
                r = jnp.where(lo, lax.rsqrt(s_lo / DH_DIFF + EPS), lax.rsqrt(s_hi / DH_DIFF + EPS))
                o_ref[:, sl] = (y * r * aux_ref[:, sl]).astype(o_ref.dtype)


def _input_projection(x2d, g_mix, w, aux, mode):
    t, d = x2d.shape
    n = w.shape[1]
    return pl.pallas_call(
        functools.partial(_proj_kernel, mode=mode),
        grid=(t // PROJ_TM, n // PROJ_TN),
        in_specs=[
            pl.BlockSpec((PROJ_TM, d), lambda i, j: (i, 0)),
            pl.BlockSpec((1, d), lambda i, j: (0, 0)),
            pl.BlockSpec((d, PROJ_TN), lambda i, j: (0, j)),
            pl.BlockSpec((1, PROJ_TN), lambda i, j: (0, j)),
        ],
        out_specs=pl.BlockSpec((PROJ_TM, PROJ_TN), lambda i, j: (i, j)),
        out_shape=jax.ShapeDtypeStruct((t, n), BF16),
        scratch_shapes=[pltpu.VMEM((PROJ_TM, d), BF16)],
        compiler_params=_cparams(("parallel", "arbitrary")),
        name="input_projection_" + mode,
    )(x2d, g_mix, w, aux)


def _small_proj_kernel(x_ref, g_ref, w_ref, wt_ref, alog_ref, dtb_ref, alog_t_ref, dtb_t_ref,
                       o_ref, ot_ref, *, n_heads):
    x = x_ref[...]
    ms = jnp.mean(x * x, axis=-1, keepdims=True)
    h = (x * lax.rsqrt(ms + EPS) * g_ref[...]).astype(BF16)

    def finish(acc, idx, alog, dtb):
        beta = jax.nn.sigmoid(acc)
        z = acc + dtb
        softplus = jnp.maximum(z, 0.0) + jnp.log1p(jnp.exp(-jnp.abs(z)))
        gdec = -jnp.exp(alog) * softplus
        return jnp.where(idx < n_heads, beta, jnp.where(idx < 2 * n_heads, gdec, 0.0))

    acc = jnp.dot(h, w_ref[...], preferred_element_type=F32)
    lane = lax.broadcasted_iota(jnp.int32, acc.shape, 1)
    o_ref[...] = finish(acc, lane, alog_ref[...], dtb_ref[...])
    acc_t = lax.dot_general(wt_ref[...], h, (((1,), (1,)), ((), ())),
                            preferred_element_type=F32)
    sub = lax.broadcasted_iota(jnp.int32, acc_t.shape, 0)
    ot_ref[...] = finish(acc_t, sub, alog_t_ref[...], dtb_t_ref[...])


def _small_projection(x2d, g_mix, w_small, w_small_t, alog, dtb, alog_t, dtb_t, n_heads):
    t, d = x2d.shape
    rows_t = w_small_t.shape[0]
    tm = PROJ_TM
    full = lambda shape: pl.BlockSpec(shape, lambda i: (0, 0))
    return pl.pallas_call(
        functools.partial(_small_proj_kernel, n_heads=n_heads),
        grid=(t // tm,),
        in_specs=[
            pl.BlockSpec((tm, d), lambda i: (i, 0)),
            full((1, d)), full((d, LANES)), full((rows_t, d)),
            full((1, LANES)), full((1, LANES)), full((rows_t, 1)), full((rows_t, 1)),
        ],
        out_specs=[pl.BlockSpec((tm, LANES), lambda i: (i, 0)),
                   pl.BlockSpec((rows_t, tm), lambda i: (0, i))],
        out_shape=[jax.ShapeDtypeStruct((t, LANES), F32),
                   jax.ShapeDtypeStruct((rows_t, t), F32)],
        compiler_params=_cparams(("parallel",)),
        name="beta_decay_projection",
    )(x2d, g_mix, w_small, w_small_t, alog, dtb, alog_t, dtb_t)


def _gdn_kernel(q_ref, k_ref, v_ref, z_ref, sm_ref, smt_ref, cwq_ref, cwk_ref, cwv_ref, gout_ref,
                o_ref, state_ref, qp_ref, kp_ref, vp_ref, vn_ref, *, n_heads):
    hg = pl.program_id(1)
    s = pl.program_id(2)
    tb = GDN_TB
    pad = SUBLANES
    width = GDN_HG * HEAD_DIM

    @pl.when(s == 0)
    def _():
        state_ref[...] = jnp.zeros_like(state_ref)
        for p_ref in (qp_ref, kp_ref, vp_ref):
            p_ref[0:pad, :] = jnp.zeros((pad, width), F32)

    def conv_silu(x_ref, p_ref, cw_ref):
        p_ref[pad:pad + tb, :] = x_ref[...].astype(F32)
        acc = cw_ref[CONV_WIDTH - 1:CONV_WIDTH, :] * p_ref[pad:pad + tb, :]
        for jj in range(CONV_WIDTH - 1):
            off = pad - (CONV_WIDTH - 1) + jj
            acc = acc + cw_ref[jj:jj + 1, :] * p_ref[off:off + tb, :]
        p_ref[0:pad, :] = p_ref[tb:tb + pad, :]
        return acc * jax.nn.sigmoid(acc)

    q_all = conv_silu(q_ref, qp_ref, cwq_ref)
    k_all = conv_silu(k_ref, kp_ref, cwk_ref)
    v_all = conv_silu(v_ref, vp_ref, cwv_ref)

    r = lax.broadcasted_iota(jnp.int32, (tb, tb), 0)
    c = lax.broadcasted_iota(jnp.int32, (tb, tb), 1)
    shift = int(math.log2(CHUNK))
    same = (r >> shift) == (c >> shift)
    incl = jnp.logical_and(same, c <= r)
    strict = jnp.logical_and(same, c < r)

    small = sm_ref[...]
    small_t = smt_ref[...]
    lane = lax.broadcasted_iota(jnp.int32, small.shape, 1)
    def split3(a):
        hi = a.astype(BF16)
        r1 = a - hi.astype(F32)
        mid = r1.astype(BF16)
        lo = (r1 - mid.astype(F32)).astype(BF16)
        return hi.astype(F32), mid.astype(F32), lo.astype(F32)

    part = 2 * n_heads
    s_hi, s_mid, s_lo = split3(small)
    small3 = jnp.where(lane < part, s_hi,
                       jnp.where(lane < 2 * part, pltpu.roll(s_mid, part, 1),
                                 jnp.where(lane < 3 * part, pltpu.roll(s_lo, 2 * part, 1), 0.0)))
    both = _bdot(jnp.concatenate([incl.astype(F32), same.astype(F32)], axis=0), small3)
    gcum = both[:tb]
    gtot = both[tb:]
    gcum_t = _bdot(jnp.concatenate(split3(small_t), axis=0),
                   jnp.logical_and(same, r <= c).astype(F32))
    sub3 = lax.broadcasted_iota(jnp.int32, gcum_t.shape, 0)

    heads = range(GDN_HG)
    hsl = [slice(hh * HEAD_DIM, (hh + 1) * HEAD_DIM) for hh in heads]
    qs = [q_all[:, hs] for hs in hsl]
    ks = [k_all[:, hs] for hs in hsl]
    vs = [v_all[:, hs] for hs in hsl]
    qs = [q * lax.rsqrt(jnp.sum(q * q, axis=-1, keepdims=True) + EPS) * (HEAD_DIM ** -0.5) for q in qs]
    ks = [k * lax.rsqrt(jnp.sum(k * k, axis=-1, keepdims=True) + EPS) for k in ks]

    def col_of(arr, idx):
        return jnp.sum(jnp.where(lane == idx, arr, 0.0), axis=-1, keepdims=True)

    def terms_of(pos, idx):
        return jnp.logical_or(pos == idx, jnp.logical_or(pos == idx + part, pos == idx + 2 * part))

    head_ids = [hg * GDN_HG + hh for hh in heads]
    betas = [col_of(small, hd) for hd in head_ids]
    gcs = [jnp.sum(jnp.where(terms_of(lane, hd + n_heads), gcum, 0.0), axis=-1, keepdims=True)
           for hd in head_ids]
    gls = [jnp.sum(jnp.where(terms_of(lane, hd + n_heads), gtot, 0.0), axis=-1, keepdims=True)
           for hd in head_ids]
    gc_rows = [jnp.sum(jnp.where(terms_of(sub3, hd + n_heads), gcum_t, 0.0), axis=0, keepdims=True)
               for hd in head_ids]

    decays = [jnp.where(incl, jnp.exp(jnp.minimum(gc - gr, 0.0)), 0.0) for gc, gr in zip(gcs, gc_rows)]
    kbs = [k * b for k, b in zip(ks, betas)]
    kks = [_bdot_nt(kb, k) for kb, k in zip(kbs, ks)]
    pws = [jnp.where(strict, -(kk * dc), 0.0) for kk, dc in zip(kks, decays)]
    n_chunks = tb // CHUNK
    cat_row = lax.broadcasted_iota(jnp.int32, (CHUNK, tb), 0)
    cat_lane = lax.broadcasted_iota(jnp.int32, (CHUNK, tb), 1)
    lane_chunk = cat_lane >> shift

    def block_diag(m_cat):
        return jnp.concatenate([jnp.where(lane_chunk == ci, m_cat, 0.0) for ci in range(n_chunks)], axis=0)

    def cat_of(m_bd):
        out = m_bd[0:CHUNK]
        for ci in range(1, n_chunks):
            out = out + m_bd[ci * CHUNK:(ci + 1) * CHUNK]
        return out

    pcats = [cat_of(pw) for pw in pws]
    eye_cat = ((cat_lane & (CHUNK - 1)) == cat_row).astype(F32)
    tcats = [eye_cat + pc for pc in pcats]
    pcats = [_bdot(pc, block_diag(pc)) for pc in pcats]
    n_levels = int(math.log2(CHUNK))
    for lev in range(1, n_levels):
        bds = [block_diag(pc) for pc in pcats]
        if lev < n_levels - 1:
            prods = [_bdot(jnp.concatenate([pc, tc], axis=0), bd) for pc, tc, bd in zip(pcats, tcats, bds)]
            pcats = [pr[:CHUNK] for pr in prods]
            tcats = [tc + pr[CHUNK:] for tc, pr in zip(tcats, prods)]
        else:
            tcats = [tc + _bdot(tc, bd) for tc, bd in zip(tcats, bds)]
    tmats = [block_diag(tc) for tc in tcats]
    egcs = [jnp.exp(gc) for gc in gcs]
    uws = [_bdot(tm, jnp.concatenate([v * b, kb * eg], axis=1))
           for tm, v, b, kb, eg in zip(tmats, vs, betas, kbs, egcs)]
    us = [uw[:, :HEAD_DIM] for uw in uws]
    ws = [uw[:, HEAD_DIM:] for uw in uws]
    qkm = [_bdot_nt(q, k) for q, k in zip(qs, ks)]
    qkm = [jnp.where(incl, x * dc, 0.0) for x, dc in zip(qkm, decays)]
    q_decs = [q * eg for q, eg in zip(qs, egcs)]
    k_ends = [k * jnp.exp(gl - gc) for k, gl, gc in zip(ks, gls, gcs)]

    for hh in heads:
        vn_ref[hh] = jnp.zeros((tb, HEAD_DIM), F32)
    outs = [[] for _ in heads]
    for ci in range(tb // CHUNK):
        cs = slice(ci * CHUNK, (ci + 1) * CHUNK)
        sts = [state_ref[hh] for hh in heads]
        ws_qs = [_bdot(jnp.concatenate([ws[hh][cs], q_decs[hh][cs]], axis=0), sts[hh]) for hh in heads]
        v_news = [us[hh][cs] - ws_qs[hh][:CHUNK] for hh in heads]
        for hh in heads:
            vn_ref[hh, cs, :] = v_news[hh]
        intra = [_bdot(qkm[hh][cs], vn_ref[hh]) for hh in heads]
        upd = [_bdot_tn(k_ends[hh][cs], v_news[hh]) for hh in heads]
        for hh in heads:
            outs[hh].append(ws_qs[hh][CHUNK:] + intra[hh])
            g_last = gls[hh][ci * CHUNK:ci * CHUNK + 1, :]
            state_ref[hh] = sts[hh] * jnp.exp(g_last) + upd[hh]
    for hh in heads:
        o = jnp.concatenate(outs[hh], axis=0)
        o = o * lax.rsqrt(jnp.mean(o * o, axis=-1, keepdims=True) + EPS) * gout_ref[...]
        zz = z_ref[:, hsl[hh]].astype(F32)
        o_ref[:, hsl[hh]] = (o * (zz * jax.nn.sigmoid(zz))).astype(o_ref.dtype)


def _gated_delta(big, small, small_t, conv_w, g_out, bsz, seq, n_heads, d_model):
    t = bsz * seq
    tb = GDN_TB
    ns = seq // tb
    width = GDN_HG * HEAD_DIM
    nhg = n_heads // GDN_HG
    blocks_per_group = d_model // width
    rows_t = small_t.shape[0]

    def colspec(group):
        return pl.BlockSpec((tb, width), lambda b, h, s: (b * ns + s, group * blocks_per_group + h))

    def cwspec(group):
        return pl.BlockSpec((CONV_WIDTH, width), lambda b, h, s: (0, group * blocks_per_group + h))

    return pl.pallas_call(
        functools.partial(_gdn_kernel, n_heads=n_heads),
        grid=(bsz, nhg, ns),
        in_specs=[
            colspec(0), colspec(1), colspec(2), colspec(3),
            pl.BlockSpec((tb, LANES), lambda b, h, s: (b * ns + s, 0)),
            pl.BlockSpec((rows_t, tb), lambda b, h, s: (0, b * ns + s)),
            cwspec(0), cwspec(1), cwspec(2),
            pl.BlockSpec((1, HEAD_DIM), lambda b, h, s: (0, 0)),
        ],
        out_specs=pl.BlockSpec((tb, width), lambda b, h, s: (b * ns + s, h)),
        out_shape=jax.ShapeDtypeStruct((t, d_model), BF16),
        scratch_shapes=[
            pltpu.VMEM((GDN_HG, HEAD_DIM, HEAD_DIM), F32),
            pltpu.VMEM((tb + SUBLANES, width), F32),
            pltpu.VMEM((tb + SUBLANES, width), F32),
            pltpu.VMEM((tb + SUBLANES, width), F32),
            pltpu.VMEM((GDN_HG, tb, HEAD_DIM), F32),
        ],
        compiler_params=_cparams(("parallel", "parallel", "arbitrary")),
        name="gated_delta",
    )(big, big, big, big, small, small_t, conv_w, conv_w, conv_w, g_out)


def _t5_bucket(n):
    max_exact = N_BUCKETS // 2
    nf = jnp.maximum(n, 1).astype(F32)
    large = max_exact + (jnp.log(nf / max_exact) / math.log(MAX_DISTANCE / max_exact)
                         * (N_BUCKETS - max_exact)).astype(jnp.int32)
    large = jnp.minimum(large, N_BUCKETS - 1)
    return jnp.where(n < max_exact, n, large)


def _attn_kernel(rb_ref, q_ref, k_ref, v_ref, lam_ref, gsub_ref, o_ref,
                 bias_ref, m_ref, acc_ref, sa_ref, sb_ref, *, lam_init):
    h = pl.program_id(0)
    b = pl.program_id(1)
    qi = pl.program_id(2)
    bq, bk = ATT_BQ, ATT_BK

    @pl.when(jnp.logical_and(b == 0, qi == 0))
    def _():
        i = lax.broadcasted_iota(jnp.int32, (bq, bk), 0)
        jj = lax.broadcasted_iota(jnp.int32, (bq, bk), 1)
        far = rb_ref[N_BUCKETS - 1, h]
        bias_ref[2] = jnp.zeros((bq, bk), F32)
        for slot in range(2):
            n = i - jj + slot * bk
            bucket = _t5_bucket(jnp.maximum(n, 0))
            bias = jnp.zeros((bq, bk), F32)
            for cc in range(N_BUCKETS):
                bias = jnp.where(bucket == cc, rb_ref[cc, h] - far, bias)
            if slot == 0:
                bias = jnp.where(n >= 0, bias, NEG_BIG)
            bias_ref[slot] = bias

    m_ref[...] = jnp.full(m_ref.shape, NEG_BIG, F32)
    acc_ref[...] = jnp.zeros(acc_ref.shape, F32)

    q = q_ref[...]
    lane = lax.broadcasted_iota(jnp.int32, q.shape, 1)
    zero = jnp.zeros_like(q)
    qs = jnp.concatenate([jnp.where(lane < DH_DIFF, q, zero), jnp.where(lane < DH_DIFF, zero, q)], axis=0)
    ones_col = (lax.broadcasted_iota(jnp.int32, (bk, HEAD_DIM), 1) == 0).astype(BF16)

    def scores(j):
        ks = pl.multiple_of(j * bk, bk)
        return lax.dot_general(qs, k_ref[pl.ds(ks, bk), :], (((1,), (1,)), ((), ())),
                               preferred_element_type=F32)

    def absorb(j, sc_ref, biased=True):
        ks = pl.multiple_of(j * bk, bk)
        v_ext = jnp.concatenate([v_ref[pl.ds(ks, bk), :], ones_col], axis=1)
        if biased:
            bias = bias_ref[jnp.minimum(qi - j, 2)]
            sc = jnp.concatenate([sc_ref[0:bq, :] + bias, sc_ref[bq:2 * bq, :] + bias], axis=0)
        else:
            sc = sc_ref[...]
        m_old = m_ref[...]
        m_new = jnp.maximum(m_old, jnp.max(sc, axis=-1, keepdims=True))
        p = jnp.exp(sc - m_new)
        acc_ref[...] = (jnp.exp(m_old - m_new) * acc_ref[...]
                        + jnp.dot(p.astype(BF16), v_ext, preferred_element_type=F32))
        m_ref[...] = m_new

    n_tiles = qi + 1
    sa_ref[...] = scores(0)

    def pair_body(jj, carry, biased):
        j0 = 2 * jj
        sb_ref[...] = scores(j0 + 1)
        absorb(j0, sa_ref, biased)
        sa_ref[...] = scores(jnp.minimum(j0 + 2, qi))
        absorb(j0 + 1, sb_ref, biased)
        return carry

    n_far_pairs = jnp.maximum(qi - 1, 0) // 2
    lax.fori_loop(0, n_far_pairs, functools.partial(pair_body, biased=False), 0)
    lax.fori_loop(n_far_pairs, n_tiles // 2, functools.partial(pair_body, biased=True), 0)

    @pl.when(n_tiles % 2 == 1)
    def _():
        absorb(qi, sa_ref)

    lam_p = lam_ref[...]
    s1 = jnp.sum(lam_p[0:1] * lam_p[1:2], axis=-1, keepdims=True)
    s2 = jnp.sum(lam_p[2:3] * lam_p[3:4], axis=-1, keepdims=True)
    lam = jnp.exp(s1) - jnp.exp(s2) + lam_init
    acc = acc_ref[...]
    num = acc[:, :HEAD_DIM]
    den = acc[:, HEAD_DIM:HEAD_DIM + 1]
    o = num[:bq] / den[:bq] - lam * (num[bq:] / den[bq:])
    o = o * lax.rsqrt(jnp.mean(o * o, axis=-1, keepdims=True) + EPS) * gsub_ref[...]
    o_ref[...] = (o * (1.0 - lam_init)).astype(o_ref.dtype)


def _diff_attention(proj_qk, proj_plain, rel_bias, lam_params, g_subln, bsz, seq, n_heads, d_model,
                    lam_init):
    t = bsz * seq
    nq = seq // ATT_BQ
    per = d_model // HEAD_DIM
    vcol = 4 * per
    return pl.pallas_call(
        functools.partial(_attn_kernel, lam_init=lam_init),
        grid=(n_heads, bsz, nq),
        in_specs=[
            pl.BlockSpec(memory_space=pltpu.SMEM),
            pl.BlockSpec((ATT_BQ, HEAD_DIM), lambda h, b, i: (b * nq + i, h)),
            pl.BlockSpec((seq, HEAD_DIM), lambda h, b, i: (b, per + h)),
            pl.BlockSpec((seq, HEAD_DIM), lambda h, b, i: (b, vcol + h)),
            pl.BlockSpec((4, DH_DIFF), lambda h, b, i: (0, 0)),
            pl.BlockSpec((1, HEAD_DIM), lambda h, b, i: (0, 0)),
        ],
        out_specs=pl.BlockSpec((ATT_BQ, HEAD_DIM), lambda h, b, i: (b * nq + i, h)),
        out_shape=jax.ShapeDtypeStruct((t, d_model), BF16),
        scratch_shapes=[
            pltpu.VMEM((3, ATT_BQ, ATT_BK), F32),
            pltpu.VMEM((2 * ATT_BQ, 1), F32),
            pltpu.VMEM((2 * ATT_BQ, 2 * HEAD_DIM), F32),
            pltpu.VMEM((2 * ATT_BQ, ATT_BK), F32),
            pltpu.VMEM((2 * ATT_BQ, ATT_BK), F32),
        ],
        compiler_params=_cparams(("arbitrary", "arbitrary", "arbitrary")),
        name="diff_attention",
    )(rel_bias, proj_qk, proj_qk, proj_plain, lam_params, g_subln)


def _mix_kernel(ga_ref, gb_ref, oa_ref, od_ref, x_ref, wo_ref, gffn_ref, wr_ref, br_ref,
                x1_ref, h2_ref, topi_ref, topw_ref, rank_ref, cnt_ref, carry_ref):
    i = pl.program_id(0)
    tm = MIX_TM

    @pl.when(i == 0)
    def _():
        carry_ref[...] = jnp.zeros_like(carry_ref)

    mix = (ga_ref[...].astype(F32) * oa_ref[...].astype(F32)
           + gb_ref[...].astype(F32) * od_ref[...].astype(F32))
    x1 = x_ref[...] + jnp.dot(mix.astype(BF16), wo_ref[...], preferred_element_type=F32)
    x1_ref[...] = x1
    h2 = x1 * lax.rsqrt(jnp.mean(x1 * x1, axis=-1, keepdims=True) + EPS) * gffn_ref[...]
    h2_ref[...] = h2

    logits = lax.dot_general(wr_ref[...], h2, (((1,), (1,)), ((), ())),
                             preferred_element_type=F32, precision=lax.Precision.HIGHEST) + br_ref[...]
    eidx = lax.broadcasted_iota(jnp.int32, logits.shape, 0).astype(F32)
    vals, hots = [], []
    cur = logits
    for kk in range(TOP_K):
        mx = jnp.max(cur, axis=0, keepdims=True)
        idx = jnp.min(jnp.where(cur == mx, eidx, float(N_EXPERTS)), axis=0, keepdims=True)
        hot = eidx == idx
        vals.append(mx)
        hots.append(hot)
        topi_ref[kk:kk + 1, :] = idx.astype(jnp.int32)
        cur = jnp.where(hot, -jnp.inf, cur)
    exps = [jnp.exp(vv - vals[0]) for vv in vals]
    denom = exps[0] + exps[1] + exps[2] + exps[3]
    for kk in range(TOP_K):
        topw_ref[kk:kk + 1, :] = exps[kk] / denom

    sel = hots[0]
    for kk in range(1, TOP_K):
        sel = jnp.logical_or(sel, hots[kk])
    sel_f = sel.astype(F32)
    r = lax.broadcasted_iota(jnp.int32, (tm, tm), 0)
    c = lax.broadcasted_iota(jnp.int32, (tm, tm), 1)
    before = _bdot(sel_f, (r < c).astype(F32)) + carry_ref[...]
    for kk in range(TOP_K):
        rank_ref[kk:kk + 1, :] = jnp.sum(jnp.where(hots[kk], before, 0.0), axis=0,
                                         keepdims=True).astype(jnp.int32)
    carry_ref[...] = carry_ref[...] + jnp.sum(sel_f, axis=-1, keepdims=True)
    cnt_ref[...] = carry_ref[...].astype(jnp.int32)


def _mix_project_route(proj_gate, oa, od, x2d, w_o, g_ffn, w_r_t, b_r, d_model):
    t = x2d.shape[0]
    tm = MIX_TM
    full = lambda shape: pl.BlockSpec(shape, lambda i: (0, 0))
    row = lambda: pl.BlockSpec((tm, d_model), lambda i: (i, 0))
    krow = lambda: pl.BlockSpec((TOP_K, tm), lambda i: (0, i))
    return pl.pallas_call(
        _mix_kernel,
        grid=(t // tm,),
        in_specs=[
            pl.BlockSpec((tm, d_model), lambda i: (i, 0)),
            pl.BlockSpec((tm, d_model), lambda i: (i, 1)),
            row(), row(), row(),
            full((d_model, d_model)), full((1, d_model)), full((N_EXPERTS, d_model)), full((N_EXPERTS, 1)),
        ],
        out_specs=[row(), row(), krow(), krow(), krow(), full((N_EXPERTS, 1))],
        out_shape=[
            jax.ShapeDtypeStruct((t, d_model), F32),
            jax.ShapeDtypeStruct((t, d_model), F32),
            jax.ShapeDtypeStruct((TOP_K, t), jnp.int32),
            jax.ShapeDtypeStruct((TOP_K, t), F32),
            jax.ShapeDtypeStruct((TOP_K, t), jnp.int32),
            jax.ShapeDtypeStruct((N_EXPERTS, 1), jnp.int32),
        ],
        scratch_shapes=[pltpu.VMEM((N_EXPERTS, 1), F32)],
        compiler_params=_cparams(("arbitrary",)),
        name="merge_outproj_route",
    )(proj_gate, proj_gate, oa, od, x2d, w_o, g_ffn, w_r_t, b_r)


def _row_copy(src_ref, src_row, dst_ref, dst_row, sem):
    return pltpu.make_async_copy(src_ref.at[pl.ds(src_row, 1)], dst_ref.at[pl.ds(dst_row, 1)], sem)


def _expert_kernel(be_ref, src_ref, src_next_ref, src_ahead_ref, h2_ref, wup_ref, bup_ref, wdn_ref,
                   bdn_ref, y_ref, x0, x1, x2, wup_bf, wdn_bf, gsem):
    i = pl.program_id(0)
    last = pl.num_programs(0) - 1
    rb = MOE_RB
    d_ff = wdn_ref.shape[1]

    def gather_wait(x_ref, s):
        pltpu.make_async_copy(h2_ref.at[pl.ds(0, rb)], x_ref, gsem.at[s]).wait()

    @pl.when(i == 0)
    def _():
        def first_two(a, carry):
            _row_copy(h2_ref, src_ref[a], x0, a, gsem.at[0]).start()
            _row_copy(h2_ref, src_next_ref[a], x1, a, gsem.at[1]).start()
            return carry
        lax.fori_loop(0, rb, first_two, 0, unroll=8)

    @pl.when(jnp.logical_or(i == 0, be_ref[i] != be_ref[jnp.maximum(i - 1, 0)]))
    def _():
        rr = lax.broadcasted_iota(jnp.int32, (2 * LANES, 2 * LANES), 0)
        cc = lax.broadcasted_iota(jnp.int32, (2 * LANES, 2 * LANES), 1)
        pick = jnp.where(cc < LANES, 2 * cc, 2 * (cc - LANES) + 1)
        perm = (rr == pick).astype(BF16)
        for g in range(wup_ref.shape[2] // (2 * LANES)):
            cs = slice(g * 2 * LANES, (g + 1) * 2 * LANES)
            wup_bf[:, cs] = jnp.dot(wup_ref[0, :, cs].astype(BF16), perm,
                                    preferred_element_type=F32).astype(BF16)
        wdn_bf[...] = wdn_ref[0].astype(BF16)

    def block(xa, xb, s, s_ahead, s_other):
        gather_wait(xa, s)
        for a in range(rb):
            _row_copy(h2_ref, src_ahead_ref[a], xb, a, gsem.at[s_ahead]).start()
        hid = jnp.dot(xa[...].astype(BF16), wup_bf[...], preferred_element_type=F32) + bup_ref[0]
        acts = []
        for g in range(hid.shape[1] // (2 * LANES)):
            glu = jnp.minimum(hid[:, g * 2 * LANES:g * 2 * LANES + LANES], SWIGLU_LIMIT)
            lin = jnp.clip(hid[:, g * 2 * LANES + LANES:(g + 1) * 2 * LANES], -SWIGLU_LIMIT, SWIGLU_LIMIT)
            acts.append(glu * jax.nn.sigmoid(SWIGLU_ALPHA * glu) * (lin + 1.0))
        act = jnp.concatenate(acts, axis=1)
        assert act.shape[1] == d_ff
        y_ref[...] = jnp.dot(act.astype(BF16), wdn_bf[...], preferred_element_type=F32) + bdn_ref[0]

        @pl.when(i == last)
        def _():
            gather_wait(xb, s_ahead)
            gather_wait(xb, s_other)

    bufs = (x0, x1, x2)
    for v in range(3):
        @pl.when(i % 3 == v)
        def _(v=v):
            block(bufs[v], bufs[(v + 2) % 3], v, (v + 2) % 3, (v + 1) % 3)


def _experts(block_e, src_tok, h2, w_up, b_up, w_down, b_down):
    n_rows = src_tok.shape[0]
    d = w_up.shape[1]
    nb = n_rows // MOE_RB
    two_ff = w_up.shape[2]
    d_ff = w_down.shape[1]
    smem_rows = lambda fn: pl.BlockSpec((MOE_RB,), fn, memory_space=pltpu.SMEM)
    grid_spec = pltpu.PrefetchScalarGridSpec(
        num_scalar_prefetch=1,
        grid=(nb,),
        in_specs=[
            smem_rows(lambda i, be: (i,)),
            smem_rows(lambda i, be: (jnp.minimum(i + 1, nb - 1),)),
            smem_rows(lambda i, be: (jnp.minimum(i + 2, nb - 1),)),
            pl.BlockSpec(memory_space=pl.ANY),
            pl.BlockSpec((1, d, two_ff), lambda i, be: (be[i], 0, 0)),
            pl.BlockSpec((1, 1, two_ff), lambda i, be: (be[i], 0, 0)),
            pl.BlockSpec((1, d_ff, d), lambda i, be: (be[i], 0, 0)),
            pl.BlockSpec((1, 1, d), lambda i, be: (be[i], 0, 0)),
        ],
        out_specs=pl.BlockSpec((MOE_RB, d), lambda i, be: (i, 0)),
        scratch_shapes=[
            pltpu.VMEM((MOE_RB, d), F32), pltpu.VMEM((MOE_RB, d), F32), pltpu.VMEM((MOE_RB, d), F32),
            pltpu.VMEM((d, two_ff), BF16),
            pltpu.VMEM((d_ff, d), BF16),
            pltpu.SemaphoreType.DMA((3,)),
        ],
    )
    return pl.pallas_call(
        _expert_kernel,
        grid_spec=grid_spec,
        out_shape=jax.ShapeDtypeStruct((n_rows, d), F32),
        compiler_params=_cparams(("arbitrary",)),
        name="moe_experts",
    )(block_e, src_tok, src_tok, src_tok, h2, w_up, b_up, w_down, b_down)


def _sc_gather_rows(table, idx):
    n_idx = idx.shape[0]
    d = table.shape[1]
    n_workers = SC_CORES * SC_SUBCORES
    per_worker = n_idx // n_workers
    n_chunks = per_worker // SC_GATHER_ROWS
    assert n_idx % n_workers == 0 and per_worker % SC_GATHER_ROWS == 0
    mesh = plsc.VectorSubcoreMesh(core_axis_name="c", subcore_axis_name="s",
                                  num_cores=SC_CORES, num_subcores=SC_SUBCORES)

    assert n_chunks % 2 == 0

    def body(table_hbm, idx_hbm, out_hbm, idx_v, rows_a, rows_b, sem_a, sem_b):
        wid = lax.axis_index("s") * SC_CORES + lax.axis_index("c")
        base = wid * per_worker
        pltpu.sync_copy(idx_hbm.at[pl.ds(base, per_worker)], idx_v)

        def gather(ci, rows_v, sem):
            off = pl.multiple_of(ci * SC_GATHER_ROWS, SC_GATHER_ROWS)
            return pltpu.make_async_copy(table_hbm.at[idx_v.at[pl.ds(off, SC_GATHER_ROWS)]], rows_v, sem)

        def put(ci, rows_v):
            off = pl.multiple_of(ci * SC_GATHER_ROWS, SC_GATHER_ROWS)
            pltpu.sync_copy(rows_v, out_hbm.at[pl.ds(base + off, SC_GATHER_ROWS)])

        gather(0, rows_a, sem_a).start()

        @pl.loop(0, n_chunks, step=2)
        def _(ci):
            gather(ci + 1, rows_b, sem_b).start()
            gather(ci, rows_a, sem_a).wait()
            put(ci, rows_a)
            nxt = jnp.minimum(ci + 2, n_chunks - 1)
            gather(nxt, rows_a, sem_a).start()
            gather(ci + 1, rows_b, sem_b).wait()
            put(ci + 1, rows_b)

        gather(n_chunks - 1, rows_a, sem_a).wait()

    return pl.kernel(
        body,
        out_type=jax.ShapeDtypeStruct((n_idx, d), table.dtype),
        mesh=mesh,
        scratch_types=[
            pltpu.VMEM((per_worker,), jnp.int32),
            pltpu.VMEM((SC_GATHER_ROWS, d), table.dtype),
            pltpu.VMEM((SC_GATHER_ROWS, d), table.dtype),
            pltpu.SemaphoreType.DMA,
            pltpu.SemaphoreType.DMA,
        ],
        name="moe_slot_gather",
    )(table, idx)


def _combine_kernel(x1_ref, w_ref, y0_ref, y1_ref, y2_ref, y3_ref, o_ref):
    w = w_ref[...]
    out = x1_ref[...]
    for kk, y_ref in enumerate((y0_ref, y1_ref, y2_ref, y3_ref)):
        out = out + w[:, kk:kk + 1] * y_ref[...]
    o_ref[...] = out


def _combine(x1, w_tok, y_slots):
    t, d = x1.shape
    tc = COMB_TC
    nt = t // tc
    yspec = lambda kk: pl.BlockSpec((tc, d), lambda i: (kk * nt + i, 0))
    return pl.pallas_call(
        _combine_kernel,
        grid=(nt,),
        in_specs=[
            pl.BlockSpec((tc, d), lambda i: (i, 0)),
            pl.BlockSpec((tc, TOP_K), lambda i: (i, 0)),
            yspec(0), yspec(1), yspec(2), yspec(3),
        ],
        out_specs=pl.BlockSpec((tc, d), lambda i: (i, 0)),
        out_shape=jax.ShapeDtypeStruct((t, d), F32),
        compiler_params=_cparams(("parallel",)),
        name="moe_combine",
    )(x1, w_tok, y_slots, y_slots, y_slots, y_slots)


def _moe(x1, h2, topi, topw, rank, counts, w_up, b_up, w_down, b_down):
    t, d = x1.shape
    n_assign = t * TOP_K
    nb = -(-n_assign // MOE_RB) + N_EXPERTS
    n_rows = nb * MOE_RB
    counts = counts[:, 0]
    padded = (counts + MOE_RB - 1) // MOE_RB * MOE_RB
    padded_end = jnp.cumsum(padded)
    padded_start = padded_end - padded
    expert_ids = jnp.arange(N_EXPERTS, dtype=jnp.int32)[:, None, None]
    start_of = jnp.sum(jnp.where(topi[None] == expert_ids, padded_start[:, None, None], 0), axis=0)
    dest = (start_of + rank).astype(jnp.int32)
    n_used = (padded_end[-1] // MOE_RB).astype(jnp.int32)
    blk = jnp.minimum(jnp.arange(nb, dtype=jnp.int32), n_used - 1)
    block_e = jnp.minimum(jnp.sum(padded_end[None, :] <= (blk * MOE_RB)[:, None], axis=1),
                          N_EXPERTS - 1).astype(jnp.int32)
    slot_of = jnp.full((n_rows,), -1, jnp.int32).at[dest.reshape(-1)].set(
        jnp.arange(n_assign, dtype=jnp.int32), unique_indices=True)
    src_tok = jnp.where(slot_of < 0, 0, slot_of % t)

    y_rows = _experts(block_e, src_tok, h2, w_up, b_up, w_down, b_down)
    y_slots = _sc_gather_rows(y_rows, dest.reshape(-1))
    return _combine(x1, topw.T, y_slots)


def kernel(x, g_mix, w_in, b_gate, conv_w, a_log, dt_bias, g_delta_out, q_norm, k_norm, lambda_q1, lambda_k1, lambda_q2, lambda_k2, g_subln, rel_bias, w_o, g_ffn, w_router, b_router, w_up, b_up, w_down, b_down):
    bsz, seq, d = x.shape
    depth = g_mix.shape[0]
    n_heads = d // HEAD_DIM
    t = bsz * seq
    d_ff = w_down.shape[2]
    assert d % PROJ_TN == 0 and t % PROJ_TM == 0 and seq % GDN_TB == 0 and seq % ATT_BQ == 0
    assert t % MIX_TM == 0 and t % COMB_TC == 0 and n_heads % GDN_HG == 0
    assert (t * TOP_K) % MOE_RB == 0
    assert 2 * n_heads <= 2 * SUBLANES

    x2d = x.reshape(t, d)
    for l in range(depth):
        wl = w_in[l]
        c0 = 4 * d
        c1 = c0 + 2 * n_heads
        c2 = c1 + 2 * d
        c3 = c2 + d
        w_small = jnp.pad(wl[:, c0:c1], ((0, 0), (0, LANES - 2 * n_heads)))
        gm = g_mix[l].reshape(1, d)
        w_plain = jnp.concatenate([wl[:, :c0], wl[:, c2:c3]], axis=1).astype(BF16)
        proj_plain = _input_projection(x2d, gm, w_plain, jnp.zeros((1, 5 * d), F32), "plain")
        qk_gain = jnp.concatenate([jnp.tile(q_norm[l] * (DH_DIFF ** -0.5), 2 * n_heads),
                                   jnp.tile(k_norm[l], 2 * n_heads)]).reshape(1, 2 * d)
        proj_qk = _input_projection(x2d, gm, wl[:, c1:c2].astype(BF16), qk_gain, "qknorm")
        proj_gate = _input_projection(x2d, gm, wl[:, c3:].astype(BF16), b_gate[l].reshape(1, 2 * d), "gate")

        head_pad = jnp.zeros((LANES - 2 * n_heads,), F32)
        alog = jnp.concatenate([jnp.zeros((n_heads,), F32), a_log[l], head_pad])
        dtb = jnp.concatenate([jnp.zeros((n_heads,), F32), dt_bias[l], head_pad])
        rows_t = 2 * n_heads
        small, small_t = _small_projection(
            x2d, g_mix[l].reshape(1, d), w_small.astype(BF16), w_small[:, :rows_t].T.astype(BF16),
            alog.reshape(1, LANES), dtb.reshape(1, LANES),
            alog[:rows_t].reshape(rows_t, 1), dtb[:rows_t].reshape(rows_t, 1), n_heads)

        oa = _gated_delta(proj_plain, small, small_t, conv_w[l], g_delta_out[l].reshape(1, HEAD_DIM),
                          bsz, seq, n_heads, d)

        lam_init = 0.8 - 0.6 * math.exp(-0.3 * l)
        lam_params = jnp.stack([lambda_q1[l], lambda_k1[l], lambda_q2[l], lambda_k2[l]])
        od = _diff_attention(proj_qk, proj_plain, rel_bias, lam_params, g_subln[l].reshape(1, HEAD_DIM),
                             bsz, seq, n_heads, d, lam_init)

        x1, h2, topi, topw, rank, counts = _mix_project_route(
            proj_gate, oa, od, x2d, w_o[l].astype(BF16), g_ffn[l].reshape(1, d),
            w_router[l].T, b_router[l].reshape(N_EXPERTS, 1), d)

        b_up_l = b_up[l].reshape(N_EXPERTS, 2 * d_ff // (2 * LANES), LANES, 2)
        b_up_l = jnp.swapaxes(b_up_l, 2, 3).reshape(N_EXPERTS, 1, 2 * d_ff)
        x2d = _moe(x1, h2, topi, topw, rank, counts, w_up[l], b_up_l,
                   w_down[l], b_down[l].reshape(N_EXPERTS, 1, d))
    return x2d.reshape(bsz, seq, d)
```

```python
import functools
import math

import jax
import jax.numpy as jnp
from jax import lax
from jax.experimental import pallas as pl
from jax.experimental.pallas import tpu as pltpu
from jax.experimental.pallas import tpu_sc as plsc

F32 = jnp.float32
BF16 = jnp.bfloat16

HEAD_DIM = 128
DH_DIFF = HEAD_DIM // 2
CONV_WIDTH = 4
CHUNK = 64
N_BUCKETS = 32
MAX_DISTANCE = 128
N_EXPERTS = 32
TOP_K = 4
TOP_K_SHIFT = 2
SWIGLU_LIMIT = 7.0
SWIGLU_ALPHA = 1.702
EPS = 1e-6
NEG_BIG = -1e30

LANES = 128
SUBLANES = 8
VMEM_LIMIT = 56 * 1024 * 1024
SC_CORES = 2
SC_SUBCORES = 16
SC_LANES = 16
SC_GATHER_ROWS = 32
SC_SCAN_CHUNK = 4096

PROJ_TM = 2048
PROJ_TN = 1024
PROJ_CHUNK = 256
GDN_TB = 256
GDN_HG = 4
ATT_BQ = 512
ATT_BK = 512
MIX_TM = 512
MOE_RB = 256
COMB_TC = 512


def _cparams(sem):
    return pltpu.CompilerParams(dimension_semantics=sem, vmem_limit_bytes=VMEM_LIMIT)


def _bdot(a, b):
    return jnp.dot(a.astype(BF16), b.astype(BF16), preferred_element_type=F32)


def _bdot_nt(a, b):
    return lax.dot_general(a.astype(BF16), b.astype(BF16), (((1,), (1,)), ((), ())),
                           preferred_element_type=F32)


def _bdot_tn(a, b):
    return lax.dot_general(a.astype(BF16), b.astype(BF16), (((0,), (0,)), ((), ())),
                           preferred_element_type=F32)


def _proj_kernel(x_ref, g_ref, w_ref, aux_ref, o_ref, h_ref, *, mode):
    @pl.when(pl.program_id(1) == 0)
    def _():
        x = x_ref[...]
        ms = jnp.mean(x * x, axis=-1, keepdims=True)
        h_ref[...] = (x * lax.rsqrt(ms + EPS) * g_ref[...]).astype(BF16)

    h = h_ref[...]
    lo = lax.broadcasted_iota(jnp.int32, (1, LANES), 1) < DH_DIFF
    for c in range(PROJ_TN // PROJ_CHUNK):
        cs = slice(c * PROJ_CHUNK, (c + 1) * PROJ_CHUNK)
        acc = jnp.dot(h, w_ref[:, cs], preferred_element_type=F32)
        if mode == "plain":
            o_ref[:, cs] = acc.astype(o_ref.dtype)
        elif mode == "gate":
            o_ref[:, cs] = jax.nn.sigmoid(acc + aux_ref[:, cs]).astype(o_ref.dtype)
        else:
            for g in range(PROJ_CHUNK // LANES):
                sl = slice(c * PROJ_CHUNK + g * LANES, c * PROJ_CHUNK + (g + 1) * LANES)
                y = acc[:, g * LANES:(g + 1) * LANES]
                y2 = y * y
                s_lo = jnp.sum(jnp.where(lo, y2, 0.0), axis=-1, keepdims=True)
                s_hi = jnp.sum(jnp.where(lo, 0.0, y2), axis=-1, keepdims=True)
                r = jnp.where(lo, lax.rsqrt(s_lo / DH_DIFF + EPS), lax.rsqrt(s_hi / DH_DIFF + EPS))
                o_ref[:, sl] = (y * r * aux_ref[:, sl]).astype(o_ref.dtype)


def _input_projection(x2d, g_mix, w, aux, mode):
    t, d = x2d.shape
    n = w.shape[1]
    return pl.pallas_call(
        functools.partial(_proj_kernel, mode=mode),
        grid=(t // PROJ_TM, n // PROJ_TN),
        in_specs=[
            pl.BlockSpec((PROJ_TM, d), lambda i, j: (i, 0)),
            pl.BlockSpec((1, d), lambda i, j: (0, 0)),
            pl.BlockSpec((d, PROJ_TN), lambda i, j: (0, j)),
            pl.BlockSpec((1, PROJ_TN), lambda i, j: (0, j)),
        ],
        out_specs=pl.BlockSpec((PROJ_TM, PROJ_TN), lambda i, j: (i, j)),
        out_shape=jax.ShapeDtypeStruct((t, n), BF16),
        scratch_shapes=[pltpu.VMEM((PROJ_TM, d), BF16)],
        compiler_params=_cparams(("parallel", "arbitrary")),
        name="input_projection_" + mode,
    )(x2d, g_mix, w, aux)


def _small_proj_kernel(x_ref, g_ref, w_ref, wt_ref, alog_ref, dtb_ref, alog_t_ref, dtb_t_ref,
                       o_ref, ot_ref, *, n_heads):
    x = x_ref[...]
    ms = jnp.mean(x * x, axis=-1, keepdims=True)
    h = (x * lax.rsqrt(ms + EPS) * g_ref[...]).astype(BF16)

    def finish(acc, idx, alog, dtb):
        beta = jax.nn.sigmoid(acc)
        z = acc + dtb
        softplus = jnp.maximum(z, 0.0) + jnp.log1p(jnp.exp(-jnp.abs(z)))
        gdec = -jnp.exp(alog) * softplus
        return jnp.where(idx < n_heads, beta, jnp.where(idx < 2 * n_heads, gdec, 0.0))

    acc = jnp.dot(h, w_ref[...], preferred_element_type=F32)
    lane = lax.broadcasted_iota(jnp.int32, acc.shape, 1)
    o_ref[...] = finish(acc, lane, alog_ref[...], dtb_ref[...])
    acc_t = lax.dot_general(wt_ref[...], h, (((1,), (1,)), ((), ())),
                            preferred_element_type=F32)
    sub = lax.broadcasted_iota(jnp.int32, acc_t.shape, 0)
    ot_ref[...] = finish(acc_t, sub, alog_t_ref[...], dtb_t_ref[...])


def _small_projection(x2d, g_mix, w_small, w_small_t, alog, dtb, alog_t, dtb_t, n_heads):
    t, d = x2d.shape
    rows_t = w_small_t.shape[0]
    tm = PROJ_TM
    full = lambda shape: pl.BlockSpec(shape, lambda i: (0, 0))
    return pl.pallas_call(
        functools.partial(_small_proj_kernel, n_heads=n_heads),
        grid=(t // tm,),
        in_specs=[
            pl.BlockSpec((tm, d), lambda i: (i, 0)),
            full((1, d)), full((d, LANES)), full((rows_t, d)),
            full((1, LANES)), full((1, LANES)), full((rows_t, 1)), full((rows_t, 1)),
        ],
        out_specs=[pl.BlockSpec((tm, LANES), lambda i: (i, 0)),
                   pl.BlockSpec((rows_t, tm), lambda i: (0, i))],
        out_shape=[jax.ShapeDtypeStruct((t, LANES), F32),
                   jax.ShapeDtypeStruct((rows_t, t), F32)],
        compiler_params=_cparams(("parallel",)),
        name="beta_decay_projection",
    )(x2d, g_mix, w_small, w_small_t, alog, dtb, alog_t, dtb_t)


def _gdn_kernel(q_ref, k_ref, v_ref, z_ref, sm_ref, smt_ref, cwq_ref, cwk_ref, cwv_ref, gout_ref,
                o_ref, state_ref, qp_ref, kp_ref, vp_ref, vn_ref, *, n_heads):
    hg = pl.program_id(1)
    s = pl.program_id(2)
    tb = GDN_TB
    pad = SUBLANES
    width = GDN_HG * HEAD_DIM

    @pl.when(s == 0)
    def _():
        state_ref[...] = jnp.zeros_like(state_ref)
        for p_ref in (qp_ref, kp_ref, vp_ref):
            p_ref[0:pad, :] = jnp.zeros((pad, width), F32)

    def conv_silu(x_ref, p_ref, cw_ref):
        p_ref[pad:pad + tb, :] = x_ref[...].astype(F32)
        acc = cw_ref[CONV_WIDTH - 1:CONV_WIDTH, :] * p_ref[pad:pad + tb, :]
        for jj in range(CONV_WIDTH - 1):
            off = pad - (CONV_WIDTH - 1) + jj
            acc = acc + cw_ref[jj:jj + 1, :] * p_ref[off:off + tb, :]
        p_ref[0:pad, :] = p_ref[tb:tb + pad, :]
        return acc * jax.nn.sigmoid(acc)

    q_all = conv_silu(q_ref, qp_ref, cwq_ref)
    k_all = conv_silu(k_ref, kp_ref, cwk_ref)
    v_all = conv_silu(v_ref, vp_ref, cwv_ref)

    r = lax.broadcasted_iota(jnp.int32, (tb, tb), 0)
    c = lax.broadcasted_iota(jnp.int32, (tb, tb), 1)
    shift = int(math.log2(CHUNK))
    same = (r >> shift) == (c >> shift)
    incl = jnp.logical_and(same, c <= r)
    strict = jnp.logical_and(same, c < r)

    small = sm_ref[...]
    small_t = smt_ref[...]
    lane = lax.broadcasted_iota(jnp.int32, small.shape, 1)
    def split3(a):
        hi = a.astype(BF16)
        r1 = a - hi.astype(F32)
        mid = r1.astype(BF16)
        lo = (r1 - mid.astype(F32)).astype(BF16)
        return hi.astype(F32), mid.astype(F32), lo.astype(F32)

    part = 2 * n_heads
    s_hi, s_mid, s_lo = split3(small)
    small3 = jnp.where(lane < part, s_hi,
                       jnp.where(lane < 2 * part, pltpu.roll(s_mid, part, 1),
                                 jnp.where(lane < 3 * part, pltpu.roll(s_lo, 2 * part, 1), 0.0)))
    both = _bdot(jnp.concatenate([incl.astype(F32), same.astype(F32)], axis=0), small3)
    gcum = both[:tb]
    gtot = both[tb:]
    gcum_t = _bdot(jnp.concatenate(split3(small_t), axis=0),
                   jnp.logical_and(same, r <= c).astype(F32))
    sub3 = lax.broadcasted_iota(jnp.int32, gcum_t.shape, 0)

    heads = range(GDN_HG)
    hsl = [slice(hh * HEAD_DIM, (hh + 1) * HEAD_DIM) for hh in heads]
    qs = [q_all[:, hs] for hs in hsl]
    ks = [k_all[:, hs] for hs in hsl]
    vs = [v_all[:, hs] for hs in hsl]
    qs = [q * lax.rsqrt(jnp.sum(q * q, axis=-1, keepdims=True) + EPS) * (HEAD_DIM ** -0.5) for q in qs]
    ks = [k * lax.rsqrt(jnp.sum(k * k, axis=-1, keepdims=True) + EPS) for k in ks]

    def col_of(arr, idx):
        return jnp.sum(jnp.where(lane == idx, arr, 0.0), axis=-1, keepdims=True)

    def terms_of(pos, idx):
        return jnp.logical_or(pos == idx, jnp.logical_or(pos == idx + part, pos == idx + 2 * part))

    head_ids = [hg * GDN_HG + hh for hh in heads]
    betas = [col_of(small, hd) for hd in head_ids]
    gcs = [jnp.sum(jnp.where(terms_of(lane, hd + n_heads), gcum, 0.0), axis=-1, keepdims=True)
           for hd in head_ids]
    gls = [jnp.sum(jnp.where(terms_of(lane, hd + n_heads), gtot, 0.0), axis=-1, keepdims=True)
           for hd in head_ids]
    gc_rows = [jnp.sum(jnp.where(terms_of(sub3, hd + n_heads), gcum_t, 0.0), axis=0, keepdims=True)
               for hd in head_ids]

    decays = [jnp.where(incl, jnp.exp(jnp.minimum(gc - gr, 0.0)), 0.0) for gc, gr in zip(gcs, gc_rows)]
    kbs = [k * b for k, b in zip(ks, betas)]
    kks = [_bdot_nt(kb, k) for kb, k in zip(kbs, ks)]
    pws = [jnp.where(strict, -(kk * dc), 0.0) for kk, dc in zip(kks, decays)]
    n_chunks = tb // CHUNK
    cat_row = lax.broadcasted_iota(jnp.int32, (CHUNK, tb), 0)
    cat_lane = lax.broadcasted_iota(jnp.int32, (CHUNK, tb), 1)
    lane_chunk = cat_lane >> shift

    def block_diag(m_cat):
        return jnp.concatenate([jnp.where(lane_chunk == ci, m_cat, 0.0) for ci in range(n_chunks)], axis=0)

    def cat_of(m_bd):
        out = m_bd[0:CHUNK]
        for ci in range(1, n_chunks):
            out = out + m_bd[ci * CHUNK:(ci + 1) * CHUNK]
        return out

    pcats = [cat_of(pw) for pw in pws]
    eye_cat = ((cat_lane & (CHUNK - 1)) == cat_row).astype(F32)
    tcats = [eye_cat + pc for pc in pcats]
    pcats = [_bdot(pc, block_diag(pc)) for pc in pcats]
    n_levels = int(math.log2(CHUNK))
    for lev in range(1, n_levels):
        bds = [block_diag(pc) for pc in pcats]
        if lev < n_levels - 1:
            prods = [_bdot(jnp.concatenate([pc, tc], axis=0), bd) for pc, tc, bd in zip(pcats, tcats, bds)]
            pcats = [pr[:CHUNK] for pr in prods]
            tcats = [tc + pr[CHUNK:] for tc, pr in zip(tcats, prods)]
        else:
            tcats = [tc + _bdot(tc, bd) for tc, bd in zip(tcats, bds)]
    tmats = [block_diag(tc) for tc in tcats]
    egcs = [jnp.exp(gc) for gc in gcs]
    uws = [_bdot(tm, jnp.concatenate([v * b, kb * eg], axis=1))
           for tm, v, b, kb, eg in zip(tmats, vs, betas, kbs, egcs)]
    us = [uw[:, :HEAD_DIM] for uw in uws]
    ws = [uw[:, HEAD_DIM:] for uw in uws]
    qkm = [_bdot_nt(q, k) for q, k in zip(qs, ks)]
    qkm = [jnp.where(incl, x * dc, 0.0) for x, dc in zip(qkm, decays)]
    q_decs = [q * eg for q, eg in zip(qs, egcs)]
    k_ends = [k * jnp.exp(gl - gc) for k, gl, gc in zip(ks, gls, gcs)]

    for hh in heads:
        vn_ref[hh] = jnp.zeros((tb, HEAD_DIM), F32)
    outs = [[] for _ in heads]
    for ci in range(tb // CHUNK):
        cs = slice(ci * CHUNK, (ci + 1) * CHUNK)
        sts = [state_ref[hh] for hh in heads]
        ws_qs = [_bdot(jnp.concatenate([ws[hh][cs], q_decs[hh][cs]], axis=0), sts[hh]) for hh in heads]
        v_news = [us[hh][cs] - ws_qs[hh][:CHUNK] for hh in heads]
        for hh in heads:
            vn_ref[hh, cs, :] = v_news[hh]
        intra = [_bdot(qkm[hh][cs], vn_ref[hh]) for hh in heads]
        upd = [_bdot_tn(k_ends[hh][cs], v_news[hh]) for hh in heads]
        for hh in heads:
            outs[hh].append(ws_qs[hh][CHUNK:] + intra[hh])
            g_last = gls[hh][ci * CHUNK:ci * CHUNK + 1, :]
            state_ref[hh] = sts[hh] * jnp.exp(g_last) + upd[hh]
    for hh in heads:
        o = jnp.concatenate(outs[hh], axis=0)
        o = o * lax.rsqrt(jnp.mean(o * o, axis=-1, keepdims=True) + EPS) * gout_ref[...]
        zz = z_ref[:, hsl[hh]].astype(F32)
        o_ref[:, hsl[hh]] = (o * (zz * jax.nn.sigmoid(zz))).astype(o_ref.dtype)


def _gated_delta(big, small, small_t, conv_w, g_out, bsz, seq, n_heads, d_model):
    t = bsz * seq
    tb = GDN_TB
    ns = seq // tb
    width = GDN_HG * HEAD_DIM
    nhg = n_heads // GDN_HG
    blocks_per_group = d_model // width
    rows_t = small_t.shape[0]

    def colspec(group):
        return pl.BlockSpec((tb, width), lambda b, h, s: (b * ns + s, group * blocks_per_group + h))

    def cwspec(group):
        return pl.BlockSpec((CONV_WIDTH, width), lambda b, h, s: (0, group * blocks_per_group + h))

    return pl.pallas_call(
        functools.partial(_gdn_kernel, n_heads=n_heads),
        grid=(bsz, nhg, ns),
        in_specs=[
            colspec(0), colspec(1), colspec(2), colspec(3),
            pl.BlockSpec((tb, LANES), lambda b, h, s: (b * ns + s, 0)),
            pl.BlockSpec((rows_t, tb), lambda b, h, s: (0, b * ns + s)),
            cwspec(0), cwspec(1), cwspec(2),
            pl.BlockSpec((1, HEAD_DIM), lambda b, h, s: (0, 0)),
        ],
        out_specs=pl.BlockSpec((tb, width), lambda b, h, s: (b * ns + s, h)),
        out_shape=jax.ShapeDtypeStruct((t, d_model), BF16),
        scratch_shapes=[
            pltpu.VMEM((GDN_HG, HEAD_DIM, HEAD_DIM), F32),
            pltpu.VMEM((tb + SUBLANES, width), F32),
            pltpu.VMEM((tb + SUBLANES, width), F32),
            pltpu.VMEM((tb + SUBLANES, width), F32),
            pltpu.VMEM((GDN_HG, tb, HEAD_DIM), F32),
        ],
        compiler_params=_cparams(("parallel", "parallel", "arbitrary")),
        name="gated_delta",
    )(big, big, big, big, small, small_t, conv_w, conv_w, conv_w, g_out)


def _t5_bucket(n):
    max_exact = N_BUCKETS // 2
    nf = jnp.maximum(n, 1).astype(F32)
    large = max_exact + (jnp.log(nf / max_exact) / math.log(MAX_DISTANCE / max_exact)
                         * (N_BUCKETS - max_exact)).astype(jnp.int32)
    large = jnp.minimum(large, N_BUCKETS - 1)
    return jnp.where(n < max_exact, n, large)


def _attn_kernel(rb_ref, q_ref, k_ref, v_ref, lam_ref, gsub_ref, o_ref,
                 bias_ref, m_ref, acc_ref, sa_ref, sb_ref, *, lam_init):
    h = pl.program_id(0)
    b = pl.program_id(1)
    qi = pl.program_id(2)
    bq, bk = ATT_BQ, ATT_BK

    @pl.when(jnp.logical_and(b == 0, qi == 0))
    def _():
        i = lax.broadcasted_iota(jnp.int32, (bq, bk), 0)
        jj = lax.broadcasted_iota(jnp.int32, (bq, bk), 1)
        far = rb_ref[N_BUCKETS - 1, h]
        bias_ref[2] = jnp.zeros((bq, bk), F32)
        for slot in range(2):
            n = i - jj + slot * bk
            bucket = _t5_bucket(jnp.maximum(n, 0))
            bias = jnp.zeros((bq, bk), F32)
            for cc in range(N_BUCKETS):
                bias = jnp.where(bucket == cc, rb_ref[cc, h] - far, bias)
            if slot == 0:
                bias = jnp.where(n >= 0, bias, NEG_BIG)
            bias_ref[slot] = bias

    m_ref[...] = jnp.full(m_ref.shape, NEG_BIG, F32)
    acc_ref[...] = jnp.zeros(acc_ref.shape, F32)

    q = q_ref[...]
    lane = lax.broadcasted_iota(jnp.int32, q.shape, 1)
    zero = jnp.zeros_like(q)
    qs = jnp.concatenate([jnp.where(lane < DH_DIFF, q, zero), jnp.where(lane < DH_DIFF, zero, q)], axis=0)
    ones_col = (lax.broadcasted_iota(jnp.int32, (bk, HEAD_DIM), 1) == 0).astype(BF16)

    def scores(j):
        ks = pl.multiple_of(j * bk, bk)
        return lax.dot_general(qs, k_ref[pl.ds(ks, bk), :], (((1,), (1,)), ((), ())),
                               preferred_element_type=F32)

    def absorb(j, sc_ref, biased=True):
        ks = pl.multiple_of(j * bk, bk)
        v_ext = jnp.concatenate([v_ref[pl.ds(ks, bk), :], ones_col], axis=1)
        if biased:
            bias = bias_ref[jnp.minimum(qi - j, 2)]
            sc = jnp.concatenate([sc_ref[0:bq, :] + bias, sc_ref[bq:2 * bq, :] + bias], axis=0)
        else:
            sc = sc_ref[...]
        m_old = m_ref[...]
        m_new = jnp.maximum(m_old, jnp.max(sc, axis=-1, keepdims=True))
        p = jnp.exp(sc - m_new)
        acc_ref[...] = (jnp.exp(m_old - m_new) * acc_ref[...]
                        + jnp.dot(p.astype(BF16), v_ext, preferred_element_type=F32))
        m_ref[...] = m_new

    n_tiles = qi + 1
    sa_ref[...] = scores(0)

    def pair_body(jj, carry, biased):
        j0 = 2 * jj
        sb_ref[...] = scores(j0 + 1)
        absorb(j0, sa_ref, biased)
        sa_ref[...] = scores(jnp.minimum(j0 + 2, qi))
        absorb(j0 + 1, sb_ref, biased)
        return carry

    n_far_pairs = jnp.maximum(qi - 1, 0) // 2
    lax.fori_loop(0, n_far_pairs, functools.partial(pair_body, biased=False), 0)
    lax.fori_loop(n_far_pairs, n_tiles // 2, functools.partial(pair_body, biased=True), 0)

    @pl.when(n_tiles % 2 == 1)
    def _():
        absorb(qi, sa_ref)

    lam_p = lam_ref[...]
    s1 = jnp.sum(lam_p[0:1] * lam_p[1:2], axis=-1, keepdims=True)
    s2 = jnp.sum(lam_p[2:3] * lam_p[3:4], axis=-1, keepdims=True)
    lam = jnp.exp(s1) - jnp.exp(s2) + lam_init
    acc = acc_ref[...]
    num = acc[:, :HEAD_DIM]
    den = acc[:, HEAD_DIM:HEAD_DIM + 1]
    o = num[:bq] / den[:bq] - lam * (num[bq:] / den[bq:])
    o = o * lax.rsqrt(jnp.mean(o * o, axis=-1, keepdims=True) + EPS) * gsub_ref[...]
    o_ref[...] = (o * (1.0 - lam_init)).astype(o_ref.dtype)


def _diff_attention(proj_qk, proj_plain, rel_bias, lam_params, g_subln, bsz, seq, n_heads, d_model,
                    lam_init):
    t = bsz * seq
    nq = seq // ATT_BQ
    per = d_model // HEAD_DIM
    vcol = 4 * per
    return pl.pallas_call(
        functools.partial(_attn_kernel, lam_init=lam_init),
        grid=(n_heads, bsz, nq),
        in_specs=[
            pl.BlockSpec(memory_space=pltpu.SMEM),
            pl.BlockSpec((ATT_BQ, HEAD_DIM), lambda h, b, i: (b * nq + i, h)),
            pl.BlockSpec((seq, HEAD_DIM), lambda h, b, i: (b, per + h)),
            pl.BlockSpec((seq, HEAD_DIM), lambda h, b, i: (b, vcol + h)),
            pl.BlockSpec((4, DH_DIFF), lambda h, b, i: (0, 0)),
            pl.BlockSpec((1, HEAD_DIM), lambda h, b, i: (0, 0)),
        ],
        out_specs=pl.BlockSpec((ATT_BQ, HEAD_DIM), lambda h, b, i: (b * nq + i, h)),
        out_shape=jax.ShapeDtypeStruct((t, d_model), BF16),
        scratch_shapes=[
            pltpu.VMEM((3, ATT_BQ, ATT_BK), F32),
            pltpu.VMEM((2 * ATT_BQ, 1), F32),
            pltpu.VMEM((2 * ATT_BQ, 2 * HEAD_DIM), F32),
            pltpu.VMEM((2 * ATT_BQ, ATT_BK), F32),
            pltpu.VMEM((2 * ATT_BQ, ATT_BK), F32),
        ],
        compiler_params=_cparams(("arbitrary", "arbitrary", "arbitrary")),
        name="diff_attention",
    )(rel_bias, proj_qk, proj_qk, proj_plain, lam_params, g_subln)


def _mix_kernel(ga_ref, gb_ref, oa_ref, od_ref, x_ref, wo_ref, gffn_ref, wr_ref, br_ref,
                x1_ref, h2_ref, topi_ref, topw_ref, rank_ref, cnt_ref, carry_ref):
    i = pl.program_id(0)
    tm = MIX_TM

    @pl.when(i == 0)
    def _():
        carry_ref[...] = jnp.zeros_like(carry_ref)

    mix = (ga_ref[...].astype(F32) * oa_ref[...].astype(F32)
           + gb_ref[...].astype(F32) * od_ref[...].astype(F32))
    x1 = x_ref[...] + jnp.dot(mix.astype(BF16), wo_ref[...], preferred_element_type=F32)
    x1_ref[...] = x1
    h2 = x1 * lax.rsqrt(jnp.mean(x1 * x1, axis=-1, keepdims=True) + EPS) * gffn_ref[...]
    h2_ref[...] = h2

    logits = lax.dot_general(wr_ref[...], h2, (((1,), (1,)), ((), ())),
                             preferred_element_type=F32, precision=lax.Precision.HIGHEST) + br_ref[...]
    eidx = lax.broadcasted_iota(jnp.int32, logits.shape, 0).astype(F32)
    vals, hots = [], []
    cur = logits
    for kk in range(TOP_K):
        mx = jnp.max(cur, axis=0, keepdims=True)
        idx = jnp.min(jnp.where(cur == mx, eidx, float(N_EXPERTS)), axis=0, keepdims=True)
        hot = eidx == idx
        vals.append(mx)
        hots.append(hot)
        topi_ref[kk:kk + 1, :] = idx.astype(jnp.int32)
        cur = jnp.where(hot, -jnp.inf, cur)
    exps = [jnp.exp(vv - vals[0]) for vv in vals]
    denom = exps[0] + exps[1] + exps[2] + exps[3]
    for kk in range(TOP_K):
        topw_ref[kk:kk + 1, :] = exps[kk] / denom

    sel = hots[0]
    for kk in range(1, TOP_K):
        sel = jnp.logical_or(sel, hots[kk])
    sel_f = sel.astype(F32)
    r = lax.broadcasted_iota(jnp.int32, (tm, tm), 0)
    c = lax.broadcasted_iota(jnp.int32, (tm, tm), 1)
    before = _bdot(sel_f, (r < c).astype(F32)) + carry_ref[...]
    for kk in range(TOP_K):
        rank_ref[kk:kk + 1, :] = jnp.sum(jnp.where(hots[kk], before, 0.0), axis=0,
                                         keepdims=True).astype(jnp.int32)
    carry_ref[...] = carry_ref[...] + jnp.sum(sel_f, axis=-1, keepdims=True)
    cnt_ref[...] = carry_ref[...].astype(jnp.int32)


def _mix_project_route(proj_gate, oa, od, x2d, w_o, g_ffn, w_r_t, b_r, d_model):
    t = x2d.shape[0]
    tm = MIX_TM
    full = lambda shape: pl.BlockSpec(shape, lambda i: (0, 0))
    row = lambda: pl.BlockSpec((tm, d_model), lambda i: (i, 0))
    krow = lambda: pl.BlockSpec((TOP_K, tm), lambda i: (0, i))
    return pl.pallas_call(
        _mix_kernel,
        grid=(t // tm,),
        in_specs=[
            pl.BlockSpec((tm, d_model), lambda i: (i, 0)),
            pl.BlockSpec((tm, d_model), lambda i: (i, 1)),
            row(), row(), row(),
            full((d_model, d_model)), full((1, d_model)), full((N_EXPERTS, d_model)), full((N_EXPERTS, 1)),
        ],
        out_specs=[row(), row(), krow(), krow(), krow(), full((N_EXPERTS, 1))],
        out_shape=[
            jax.ShapeDtypeStruct((t, d_model), F32),
            jax.ShapeDtypeStruct((t, d_model), F32),
            jax.ShapeDtypeStruct((TOP_K, t), jnp.int32),
            jax.ShapeDtypeStruct((TOP_K, t), F32),
            jax.ShapeDtypeStruct((TOP_K, t), jnp.int32),
            jax.ShapeDtypeStruct((N_EXPERTS, 1), jnp.int32),
        ],
        scratch_shapes=[pltpu.VMEM((N_EXPERTS, 1), F32)],
        compiler_params=_cparams(("arbitrary",)),
        name="merge_outproj_route",
    )(proj_gate, proj_gate, oa, od, x2d, w_o, g_ffn, w_r_t, b_r)


def _row_copy(src_ref, src_row, dst_ref, dst_row, sem):
    return pltpu.make_async_copy(src_ref.at[pl.ds(src_row, 1)], dst_ref.at[pl.ds(dst_row, 1)], sem)


def _expert_kernel(be_ref, src_ref, src_next_ref, src_ahead_ref, h2_ref, wup_ref, bup_ref, wdn_ref,
                   bdn_ref, y_ref, x0, x1, x2, wup_bf, wdn_bf, gsem):
    i = pl.program_id(0)
    last = pl.num_programs(0) - 1
    rb = MOE_RB
    d_ff = wdn_ref.shape[1]

    def gather_wait(x_ref, s):
        pltpu.make_async_copy(h2_ref.at[pl.ds(0, rb)], x_ref, gsem.at[s]).wait()

    @pl.when(i == 0)
    def _():
        def first_two(a, carry):
            _row_copy(h2_ref, src_ref[a], x0, a, gsem.at[0]).start()
            _row_copy(h2_ref, src_next_ref[a], x1, a, gsem.at[1]).start()
            return carry
        lax.fori_loop(0, rb, first_two, 0, unroll=8)

    @pl.when(jnp.logical_or(i == 0, be_ref[i] != be_ref[jnp.maximum(i - 1, 0)]))
    def _():
        rr = lax.broadcasted_iota(jnp.int32, (2 * LANES, 2 * LANES), 0)
        cc = lax.broadcasted_iota(jnp.int32, (2 * LANES, 2 * LANES), 1)
        pick = jnp.where(cc < LANES, 2 * cc, 2 * (cc - LANES) + 1)
        perm = (rr == pick).astype(BF16)
        for g in range(wup_ref.shape[2] // (2 * LANES)):
            cs = slice(g * 2 * LANES, (g + 1) * 2 * LANES)
            wup_bf[:, cs] = jnp.dot(wup_ref[0, :, cs].astype(BF16), perm,
                                    preferred_element_type=F32).astype(BF16)
        wdn_bf[...] = wdn_ref[0].astype(BF16)

    def block(xa, xb, s, s_ahead, s_other):
        gather_wait(xa, s)
        for a in range(rb):
            _row_copy(h2_ref, src_ahead_ref[a], xb, a, gsem.at[s_ahead]).start()
        hid = jnp.dot(xa[...].astype(BF16), wup_bf[...], preferred_element_type=F32) + bup_ref[0]
        acts = []
        for g in range(hid.shape[1] // (2 * LANES)):
            glu = jnp.minimum(hid[:, g * 2 * LANES:g * 2 * LANES + LANES], SWIGLU_LIMIT)
            lin = jnp.clip(hid[:, g * 2 * LANES + LANES:(g + 1) * 2 * LANES], -SWIGLU_LIMIT, SWIGLU_LIMIT)
            acts.append(glu * jax.nn.sigmoid(SWIGLU_ALPHA * glu) * (lin + 1.0))
        act = jnp.concatenate(acts, axis=1)
        assert act.shape[1] == d_ff
        y_ref[...] = jnp.dot(act.astype(BF16), wdn_bf[...], preferred_element_type=F32) + bdn_ref[0]

        @pl.when(i == last)
        def _():
            gather_wait(xb, s_ahead)
            gather_wait(xb, s_other)

    bufs = (x0, x1, x2)
    for v in range(3):
        @pl.when(i % 3 == v)
        def _(v=v):
            block(bufs[v], bufs[(v + 2) % 3], v, (v + 2) % 3, (v + 1) % 3)


def _experts(block_e, src_tok, h2, w_up, b_up, w_down, b_down):
    n_rows = src_tok.shape[0]
    d = w_up.shape[1]
    nb = n_rows // MOE_RB
    two_ff = w_up.shape[2]
    d_ff = w_down.shape[1]
    smem_rows = lambda fn: pl.BlockSpec((MOE_RB,), fn, memory_space=pltpu.SMEM)
    grid_spec = pltpu.PrefetchScalarGridSpec(
        num_scalar_prefetch=1,
        grid=(nb,),
        in_specs=[
            smem_rows(lambda i, be: (i,)),
            smem_rows(lambda i, be: (jnp.minimum(i + 1, nb - 1),)),
            smem_rows(lambda i, be: (jnp.minimum(i + 2, nb - 1),)),
            pl.BlockSpec(memory_space=pl.ANY),
            pl.BlockSpec((1, d, two_ff), lambda i, be: (be[i], 0, 0)),
            pl.BlockSpec((1, 1, two_ff), lambda i, be: (be[i], 0, 0)),
            pl.BlockSpec((1, d_ff, d), lambda i, be: (be[i], 0, 0)),
            pl.BlockSpec((1, 1, d), lambda i, be: (be[i], 0, 0)),
        ],
        out_specs=pl.BlockSpec((MOE_RB, d), lambda i, be: (i, 0)),
        scratch_shapes=[
            pltpu.VMEM((MOE_RB, d), F32), pltpu.VMEM((MOE_RB, d), F32), pltpu.VMEM((MOE_RB, d), F32),
            pltpu.VMEM((d, two_ff), BF16),
            pltpu.VMEM((d_ff, d), BF16),
            pltpu.SemaphoreType.DMA((3,)),
        ],
    )
    return pl.pallas_call(
        _expert_kernel,
        grid_spec=grid_spec,
        out_shape=jax.ShapeDtypeStruct((n_rows, d), F32),
        compiler_params=_cparams(("arbitrary",)),
        name="moe_experts",
    )(block_e, src_tok, src_tok, src_tok, h2, w_up, b_up, w_down, b_down)


def _sc_invert_slots(dest_flat, n_rows):
    n_assign = dest_flat.shape[0]
    n_workers = SC_CORES * SC_SUBCORES
    rows_per_w = n_rows // n_workers
    chunk = SC_SCAN_CHUNK
    assert n_rows % n_workers == 0 and rows_per_w % SC_LANES == 0 and n_assign % chunk == 0
    mesh = plsc.VectorSubcoreMesh(core_axis_name="c", subcore_axis_name="s",
                                  num_cores=SC_CORES, num_subcores=SC_SUBCORES)

    def body(dest_hbm, out_hbm, dest_v, map_v):
        wid = lax.axis_index("s") * SC_CORES + lax.axis_index("c")
        base = wid * rows_per_w
        lanes = lax.broadcasted_iota(jnp.int32, (SC_LANES,), 0)

        @pl.loop(0, rows_per_w, step=SC_LANES)
        def _(r0):
            map_v[pl.ds(r0, SC_LANES)] = jnp.full((SC_LANES,), -1, jnp.int32)

        @pl.loop(0, n_assign // chunk)
        def _(ci):
            pltpu.sync_copy(dest_hbm.at[pl.ds(ci * chunk, chunk)], dest_v)

            @pl.loop(0, chunk, step=SC_LANES)
            def _(j):
                local = dest_v[pl.ds(j, SC_LANES)] - base
                mine = jnp.logical_and(local >= 0, local < rows_per_w)
                plsc.store_scatter(map_v, [jnp.where(mine, local, 0)], ci * chunk + j + lanes, mask=mine)

        pltpu.sync_copy(map_v, out_hbm.at[pl.ds(base, rows_per_w)])

    return pl.kernel(
        body,
        out_type=jax.ShapeDtypeStruct((n_rows,), jnp.int32),
        mesh=mesh,
        scratch_types=[pltpu.VMEM((chunk,), jnp.int32), pltpu.VMEM((rows_per_w,), jnp.int32)],
        compiler_params=pltpu.CompilerParams(needs_layout_passes=False),
        name="moe_slot_inverse",
    )(dest_flat)


def _sc_gather_rows(table, idx):
    n_idx = idx.shape[0]
    d = table.shape[1]
    n_workers = SC_CORES * SC_SUBCORES
    per_worker = n_idx // n_workers
    n_chunks = per_worker // SC_GATHER_ROWS
    assert n_idx % n_workers == 0 and per_worker % SC_GATHER_ROWS == 0
    mesh = plsc.VectorSubcoreMesh(core_axis_name="c", subcore_axis_name="s",
                                  num_cores=SC_CORES, num_subcores=SC_SUBCORES)

    assert n_chunks % 2 == 0

    def body(table_hbm, idx_hbm, out_hbm, idx_v, rows_a, rows_b, sem_a, sem_b):
        wid = lax.axis_index("s") * SC_CORES + lax.axis_index("c")
        base = wid * per_worker
        pltpu.sync_copy(idx_hbm.at[pl.ds(base, per_worker)], idx_v)

        def gather(ci, rows_v, sem):
            off = pl.multiple_of(ci * SC_GATHER_ROWS, SC_GATHER_ROWS)
            return pltpu.make_async_copy(table_hbm.at[idx_v.at[pl.ds(off, SC_GATHER_ROWS)]], rows_v, sem)

        def put(ci, rows_v):
            off = pl.multiple_of(ci * SC_GATHER_ROWS, SC_GATHER_ROWS)
            pltpu.sync_copy(rows_v, out_hbm.at[pl.ds(base + off, SC_GATHER_ROWS)])

        gather(0, rows_a, sem_a).start()

        @pl.loop(0, n_chunks, step=2)
        def _(ci):
            gather(ci + 1, rows_b, sem_b).start()
            gather(ci, rows_a, sem_a).wait()
            put(ci, rows_a)
            nxt = jnp.minimum(ci + 2, n_chunks - 1)
            gather(nxt, rows_a, sem_a).start()
            gather(ci + 1, rows_b, sem_b).wait()
            put(ci + 1, rows_b)

        gather(n_chunks - 1, rows_a, sem_a).wait()

    return pl.kernel(
        body,
        out_type=jax.ShapeDtypeStruct((n_idx, d), table.dtype),
        mesh=mesh,
        scratch_types=[
            pltpu.VMEM((per_worker,), jnp.int32),
            pltpu.VMEM((SC_GATHER_ROWS, d), table.dtype),
            pltpu.VMEM((SC_GATHER_ROWS, d), table.dtype),
            pltpu.SemaphoreType.DMA,
            pltpu.SemaphoreType.DMA,
        ],
        name="moe_slot_gather",
    )(table, idx)


def _combine_kernel(x1_ref, w_ref, y0_ref, y1_ref, y2_ref, y3_ref, o_ref):
    w = w_ref[...]
    out = x1_ref[...]
    for kk, y_ref in enumerate((y0_ref, y1_ref, y2_ref, y3_ref)):
        out = out + w[:, kk:kk + 1] * y_ref[...]
    o_ref[...] = out


def _combine(x1, w_tok, y_slots):
    t, d = x1.shape
    tc = COMB_TC
    nt = t // tc
    yspec = lambda kk: pl.BlockSpec((tc, d), lambda i: (kk * nt + i, 0))
    return pl.pallas_call(
        _combine_kernel,
        grid=(nt,),
        in_specs=[
            pl.BlockSpec((tc, d), lambda i: (i, 0)),
            pl.BlockSpec((tc, TOP_K), lambda i: (i, 0)),
            yspec(0), yspec(1), yspec(2), yspec(3),
        ],
        out_specs=pl.BlockSpec((tc, d), lambda i: (i, 0)),
        out_shape=jax.ShapeDtypeStruct((t, d), F32),
        compiler_params=_cparams(("parallel",)),
        name="moe_combine",
    )(x1, w_tok, y_slots, y_slots, y_slots, y_slots)


def _moe(x1, h2, topi, topw, rank, counts, w_up, b_up, w_down, b_down):
    t, d = x1.shape
    n_assign = t * TOP_K
    nb = -(-n_assign // MOE_RB) + N_EXPERTS
    n_rows = nb * MOE_RB
    counts = counts[:, 0]
    padded = (counts + MOE_RB - 1) // MOE_RB * MOE_RB
    padded_end = jnp.cumsum(padded)
    padded_start = padded_end - padded
    expert_ids = jnp.arange(N_EXPERTS, dtype=jnp.int32)[:, None, None]
    start_of = jnp.sum(jnp.where(topi[None] == expert_ids, padded_start[:, None, None], 0), axis=0)
    dest = (start_of + rank).astype(jnp.int32)
    n_used = (padded_end[-1] // MOE_RB).astype(jnp.int32)
    blk = jnp.minimum(jnp.arange(nb, dtype=jnp.int32), n_used - 1)
    block_e = jnp.minimum(jnp.sum(padded_end[None, :] <= (blk * MOE_RB)[:, None], axis=1),
                          N_EXPERTS - 1).astype(jnp.int32)
    slot_of = _sc_invert_slots(dest.reshape(-1), n_rows)
    src_tok = jnp.where(slot_of < 0, 0, slot_of % t)

    y_rows = _experts(block_e, src_tok, h2, w_up, b_up, w_down, b_down)
    y_slots = _sc_gather_rows(y_rows, dest.reshape(-1))
    return _combine(x1, topw.T, y_slots)


def kernel(x, g_mix, w_in, b_gate, conv_w, a_log, dt_bias, g_delta_out, q_norm, k_norm, lambda_q1, lambda_k1, lambda_q2, lambda_k2, g_subln, rel_bias, w_o, g_ffn, w_router, b_router, w_up, b_up, w_down, b_down):
    bsz, seq, d = x.shape
    depth = g_mix.shape[0]
    n_heads = d // HEAD_DIM
    t = bsz * seq
    d_ff = w_down.shape[2]
    assert d % PROJ_TN == 0 and t % PROJ_TM == 0 and seq % GDN_TB == 0 and seq % ATT_BQ == 0
    assert t % MIX_TM == 0 and t % COMB_TC == 0 and n_heads % GDN_HG == 0
    assert (t * TOP_K) % MOE_RB == 0
    assert 2 * n_heads <= 2 * SUBLANES

    x2d = x.reshape(t, d)
    for l in range(depth):
        wl = w_in[l]
        c0 = 4 * d
        c1 = c0 + 2 * n_heads
        c2 = c1 + 2 * d
        c3 = c2 + d
        w_small = jnp.pad(wl[:, c0:c1], ((0, 0), (0, LANES - 2 * n_heads)))
        gm = g_mix[l].reshape(1, d)
        w_plain = jnp.concatenate([wl[:, :c0], wl[:, c2:c3]], axis=1).astype(BF16)
        proj_plain = _input_projection(x2d, gm, w_plain, jnp.zeros((1, 5 * d), F32), "plain")
        qk_gain = jnp.concatenate([jnp.tile(q_norm[l] * (DH_DIFF ** -0.5), 2 * n_heads),
                                   jnp.tile(k_norm[l], 2 * n_heads)]).reshape(1, 2 * d)
        proj_qk = _input_projection(x2d, gm, wl[:, c1:c2].astype(BF16), qk_gain, "qknorm")
        proj_gate = _input_projection(x2d, gm, wl[:, c3:].astype(BF16), b_gate[l].reshape(1, 2 * d), "gate")

        head_pad = jnp.zeros((LANES - 2 * n_heads,), F32)
        alog = jnp.concatenate([jnp.zeros((n_heads,), F32), a_log[l], head_pad])
        dtb = jnp.concatenate([jnp.zeros((n_heads,), F32), dt_bias[l], head_pad])
        rows_t = 2 * n_heads
        small, small_t = _small_projection(
            x2d, g_mix[l].reshape(1, d), w_small.astype(BF16), w_small[:, :rows_t].T.astype(BF16),
            alog.reshape(1, LANES), dtb.reshape(1, LANES),
            alog[:rows_t].reshape(rows_t, 1), dtb[:rows_t].reshape(rows_t, 1), n_heads)

        oa = _gated_delta(proj_plain, small, small_t, conv_w[l], g_delta_out[l].reshape(1, HEAD_DIM),
                          bsz, seq, n_heads, d)

        lam_init = 0.8 - 0.6 * math.exp(-0.3 * l)
        lam_params = jnp.stack([lambda_q1[l], lambda_k1[l], lambda_q2[l], lambda_k2[l]])
        od = _diff_attention(proj_qk, proj_plain, rel_bias, lam_params, g_subln[l].reshape(1, HEAD_DIM),
                             bsz, seq, n_heads, d, lam_init)

        x1, h2, topi, topw, rank, counts = _mix_project_route(
            proj_gate, oa, od, x2d, w_o[l].astype(BF16), g_ffn[l].reshape(1, d),
            w_router[l].T, b_router[l].reshape(N_EXPERTS, 1), d)

        b_up_l = b_up[l].reshape(N_EXPERTS, 2 * d_ff // (2 * LANES), LANES, 2)
        b_up_l = jnp.swapaxes(b_up_l, 2, 3).reshape(N_EXPERTS, 1, 2 * d_ff)
        x2d = _moe(x1, h2, topi, topw, rank, counts, w_up[l], b_up_l,
                   w_down[l], b_down[l].reshape(N_EXPERTS, 1, d))
    return x2d.reshape(bsz, seq, d)
```

```python
import functools
import math

import jax
import jax.numpy as jnp
from jax import lax
from jax.experimental import pallas as pl
from jax.experimental.pallas import tpu as pltpu
from jax.experimental.pallas import tpu_sc as plsc

F32 = jnp.float32
BF16 = jnp.bfloat16

HEAD_DIM = 128
DH_DIFF = HEAD_DIM // 2
CONV_WIDTH = 4
CHUNK = 64
N_BUCKETS = 32
MAX_DISTANCE = 128
N_EXPERTS = 32
TOP_K = 4
TOP_K_SHIFT = 2
SWIGLU_LIMIT = 7.0
SWIGLU_ALPHA = 1.702
EPS = 1e-6
NEG_BIG = -1e30

LANES = 128
SUBLANES = 8
VMEM_LIMIT = 56 * 1024 * 1024
SC_CORES = 2
SC_SUBCORES = 16
SC_LANES = 16
SC_GATHER_ROWS = 64
SC_SCAN_CHUNK = 4096

PROJ_TM = 2048
PROJ_TN = 1024
PROJ_CHUNK = 256
GDN_TB = 256
GDN_HG = 4
ATT_BQ = 512
ATT_BK = 512
MIX_TM = 512
MOE_RB = 256
COMB_TC = 512


def _cparams(sem):
    return pltpu.CompilerParams(dimension_semantics=sem, vmem_limit_bytes=VMEM_LIMIT)


def _bdot(a, b):
    return jnp.dot(a.astype(BF16), b.astype(BF16), preferred_element_type=F32)


def _bdot_nt(a, b):
    return lax.dot_general(a.astype(BF16), b.astype(BF16), (((1,), (1,)), ((), ())),
                           preferred_element_type=F32)


def _bdot_tn(a, b):
    return lax.dot_general(a.astype(BF16), b.astype(BF16), (((0,), (0,)), ((), ())),
                           preferred_element_type=F32)


def _proj_kernel(x_ref, g_ref, w_ref, aux_ref, o_ref, h_ref, *, mode):
    @pl.when(pl.program_id(1) == 0)
    def _():
        x = x_ref[...]
        ms = jnp.mean(x * x, axis=-1, keepdims=True)
        h_ref[...] = (x * lax.rsqrt(ms + EPS) * g_ref[...]).astype(BF16)

    h = h_ref[...]
    lo = lax.broadcasted_iota(jnp.int32, (1, LANES), 1) < DH_DIFF
    for c in range(PROJ_TN // PROJ_CHUNK):
        cs = slice(c * PROJ_CHUNK, (c + 1) * PROJ_CHUNK)
        acc = jnp.dot(h, w_ref[:, cs], preferred_element_type=F32)
        if mode == "plain":
            o_ref[:, cs] = acc.astype(o_ref.dtype)
        elif mode == "gate":
            o_ref[:, cs] = jax.nn.sigmoid(acc + aux_ref[:, cs]).astype(o_ref.dtype)
        else:
            for g in range(PROJ_CHUNK // LANES):
                sl = slice(c * PROJ_CHUNK + g * LANES, c * PROJ_CHUNK + (g + 1) * LANES)
                y = acc[:, g * LANES:(g + 1) * LANES]
                y2 = y * y
                s_lo = jnp.sum(jnp.where(lo, y2, 0.0), axis=-1, keepdims=True)
                s_hi = jnp.sum(jnp.where(lo, 0.0, y2), axis=-1, keepdims=True)
                r = jnp.where(lo, lax.rsqrt(s_lo / DH_DIFF + EPS), lax.rsqrt(s_hi / DH_DIFF + EPS))
                o_ref[:, sl] = (y * r * aux_ref[:, sl]).astype(o_ref.dtype)


def _input_projection(x2d, g_mix, w, aux, mode):
    t, d = x2d.shape
    n = w.shape[1]
    return pl.pallas_call(
        functools.partial(_proj_kernel, mode=mode),
        grid=(t // PROJ_TM, n // PROJ_TN),
        in_specs=[
            pl.BlockSpec((PROJ_TM, d), lambda i, j: (i, 0)),
            pl.BlockSpec((1, d), lambda i, j: (0, 0)),
            pl.BlockSpec((d, PROJ_TN), lambda i, j: (0, j)),
            pl.BlockSpec((1, PROJ_TN), lambda i, j: (0, j)),
        ],
        out_specs=pl.BlockSpec((PROJ_TM, PROJ_TN), lambda i, j: (i, j)),
        out_shape=jax.ShapeDtypeStruct((t, n), BF16),
        scratch_shapes=[pltpu.VMEM((PROJ_TM, d), BF16)],
        compiler_params=_cparams(("parallel", "arbitrary")),
        name="input_projection_" + mode,
    )(x2d, g_mix, w, aux)


def _small_proj_kernel(x_ref, g_ref, w_ref, wt_ref, alog_ref, dtb_ref, alog_t_ref, dtb_t_ref,
                       o_ref, ot_ref, *, n_heads):
    x = x_ref[...]
    ms = jnp.mean(x * x, axis=-1, keepdims=True)
    h = (x * lax.rsqrt(ms + EPS) * g_ref[...]).astype(BF16)

    def finish(acc, idx, alog, dtb):
        beta = jax.nn.sigmoid(acc)
        z = acc + dtb
        softplus = jnp.maximum(z, 0.0) + jnp.log1p(jnp.exp(-jnp.abs(z)))
        gdec = -jnp.exp(alog) * softplus
        return jnp.where(idx < n_heads, beta, jnp.where(idx < 2 * n_heads, gdec, 0.0))

    acc = jnp.dot(h, w_ref[...], preferred_element_type=F32)
    lane = lax.broadcasted_iota(jnp.int32, acc.shape, 1)
    o_ref[...] = finish(acc, lane, alog_ref[...], dtb_ref[...])
    acc_t = lax.dot_general(wt_ref[...], h, (((1,), (1,)), ((), ())),
                            preferred_element_type=F32)
    sub = lax.broadcasted_iota(jnp.int32, acc_t.shape, 0)
    ot_ref[...] = finish(acc_t, sub, alog_t_ref[...], dtb_t_ref[...])


def _small_projection(x2d, g_mix, w_small, w_small_t, alog, dtb, alog_t, dtb_t, n_heads):
    t, d = x2d.shape
    rows_t = w_small_t.shape[0]
    tm = PROJ_TM
    full = lambda shape: pl.BlockSpec(shape, lambda i: (0, 0))
    return pl.pallas_call(
        functools.partial(_small_proj_kernel, n_heads=n_heads),
        grid=(t // tm,),
        in_specs=[
            pl.BlockSpec((tm, d), lambda i: (i, 0)),
            full((1, d)), full((d, LANES)), full((rows_t, d)),
            full((1, LANES)), full((1, LANES)), full((rows_t, 1)), full((rows_t, 1)),
        ],
        out_specs=[pl.BlockSpec((tm, LANES), lambda i: (i, 0)),
                   pl.BlockSpec((rows_t, tm), lambda i: (0, i))],
        out_shape=[jax.ShapeDtypeStruct((t, LANES), F32),
                   jax.ShapeDtypeStruct((rows_t, t), F32)],
        compiler_params=_cparams(("parallel",)),
        name="beta_decay_projection",
    )(x2d, g_mix, w_small, w_small_t, alog, dtb, alog_t, dtb_t)


def _gdn_kernel(q_ref, k_ref, v_ref, z_ref, sm_ref, smt_ref, cwq_ref, cwk_ref, cwv_ref, gout_ref,
                o_ref, state_ref, qp_ref, kp_ref, vp_ref, vn_ref, *, n_heads):
    hg = pl.program_id(1)
    s = pl.program_id(2)
    tb = GDN_TB
    pad = SUBLANES
    width = GDN_HG * HEAD_DIM

    @pl.when(s == 0)
    def _():
        state_ref[...] = jnp.zeros_like(state_ref)
        for p_ref in (qp_ref, kp_ref, vp_ref):
            p_ref[0:pad, :] = jnp.zeros((pad, width), F32)

    def conv_silu(x_ref, p_ref, cw_ref):
        p_ref[pad:pad + tb, :] = x_ref[...].astype(F32)
        acc = cw_ref[CONV_WIDTH - 1:CONV_WIDTH, :] * p_ref[pad:pad + tb, :]
        for jj in range(CONV_WIDTH - 1):
            off = pad - (CONV_WIDTH - 1) + jj
            acc = acc + cw_ref[jj:jj + 1, :] * p_ref[off:off + tb, :]
        p_ref[0:pad, :] = p_ref[tb:tb + pad, :]
        return acc * jax.nn.sigmoid(acc)

    q_all = conv_silu(q_ref, qp_ref, cwq_ref)
    k_all = conv_silu(k_ref, kp_ref, cwk_ref)
    v_all = conv_silu(v_ref, vp_ref, cwv_ref)

    r = lax.broadcasted_iota(jnp.int32, (tb, tb), 0)
    c = lax.broadcasted_iota(jnp.int32, (tb, tb), 1)
    shift = int(math.log2(CHUNK))
    same = (r >> shift) == (c >> shift)
    incl = jnp.logical_and(same, c <= r)
    strict = jnp.logical_and(same, c < r)

    small = sm_ref[...]
    small_t = smt_ref[...]
    lane = lax.broadcasted_iota(jnp.int32, small.shape, 1)
    def split3(a):
        hi = a.astype(BF16)
        r1 = a - hi.astype(F32)
        mid = r1.astype(BF16)
        lo = (r1 - mid.astype(F32)).astype(BF16)
        return hi.astype(F32), mid.astype(F32), lo.astype(F32)

    part = 2 * n_heads
    s_hi, s_mid, s_lo = split3(small)
    small3 = jnp.where(lane < part, s_hi,
                       jnp.where(lane < 2 * part, pltpu.roll(s_mid, part, 1),
                                 jnp.where(lane < 3 * part, pltpu.roll(s_lo, 2 * part, 1), 0.0)))
    both = _bdot(jnp.concatenate([incl.astype(F32), same.astype(F32)], axis=0), small3)
    gcum = both[:tb]
    gtot = both[tb:]
    gcum_t = _bdot(jnp.concatenate(split3(small_t), axis=0),
                   jnp.logical_and(same, r <= c).astype(F32))
    sub3 = lax.broadcasted_iota(jnp.int32, gcum_t.shape, 0)

    heads = range(GDN_HG)
    hsl = [slice(hh * HEAD_DIM, (hh + 1) * HEAD_DIM) for hh in heads]
    qs = [q_all[:, hs] for hs in hsl]
    ks = [k_all[:, hs] for hs in hsl]
    vs = [v_all[:, hs] for hs in hsl]
    qs = [q * lax.rsqrt(jnp.sum(q * q, axis=-1, keepdims=True) + EPS) * (HEAD_DIM ** -0.5) for q in qs]
    ks = [k * lax.rsqrt(jnp.sum(k * k, axis=-1, keepdims=True) + EPS) for k in ks]

    def col_of(arr, idx):
        return jnp.sum(jnp.where(lane == idx, arr, 0.0), axis=-1, keepdims=True)

    def terms_of(pos, idx):
        return jnp.logical_or(pos == idx, jnp.logical_or(pos == idx + part, pos == idx + 2 * part))

    head_ids = [hg * GDN_HG + hh for hh in heads]
    betas = [col_of(small, hd) for hd in head_ids]
    gcs = [jnp.sum(jnp.where(terms_of(lane, hd + n_heads), gcum, 0.0), axis=-1, keepdims=True)
           for hd in head_ids]
    gls = [jnp.sum(jnp.where(terms_of(lane, hd + n_heads), gtot, 0.0), axis=-1, keepdims=True)
           for hd in head_ids]
    gc_rows = [jnp.sum(jnp.where(terms_of(sub3, hd + n_heads), gcum_t, 0.0), axis=0, keepdims=True)
               for hd in head_ids]

    decays = [jnp.where(incl, jnp.exp(jnp.minimum(gc - gr, 0.0)), 0.0) for gc, gr in zip(gcs, gc_rows)]
    kbs = [k * b for k, b in zip(ks, betas)]
    kks = [_bdot_nt(kb, k) for kb, k in zip(kbs, ks)]
    pws = [jnp.where(strict, -(kk * dc), 0.0) for kk, dc in zip(kks, decays)]
    n_chunks = tb // CHUNK
    cat_row = lax.broadcasted_iota(jnp.int32, (CHUNK, tb), 0)
    cat_lane = lax.broadcasted_iota(jnp.int32, (CHUNK, tb), 1)
    lane_chunk = cat_lane >> shift

    def block_diag(m_cat):
        return jnp.concatenate([jnp.where(lane_chunk == ci, m_cat, 0.0) for ci in range(n_chunks)], axis=0)

    def cat_of(m_bd):
        out = m_bd[0:CHUNK]
        for ci in range(1, n_chunks):
            out = out + m_bd[ci * CHUNK:(ci + 1) * CHUNK]
        return out

    pcats = [cat_of(pw) for pw in pws]
    eye_cat = ((cat_lane & (CHUNK - 1)) == cat_row).astype(F32)
    tcats = [eye_cat + pc for pc in pcats]
    pcats = [_bdot(pc, block_diag(pc)) for pc in pcats]
    n_levels = int(math.log2(CHUNK))
    for lev in range(1, n_levels):
        bds = [block_diag(pc) for pc in pcats]
        if lev < n_levels - 1:
            prods = [_bdot(jnp.concatenate([pc, tc], axis=0), bd) for pc, tc, bd in zip(pcats, tcats, bds)]
            pcats = [pr[:CHUNK] for pr in prods]
            tcats = [tc + pr[CHUNK:] for tc, pr in zip(tcats, prods)]
        else:
            tcats = [tc + _bdot(tc, bd) for tc, bd in zip(tcats, bds)]
    tmats = [block_diag(tc) for tc in tcats]
    egcs = [jnp.exp(gc) for gc in gcs]
    uws = [_bdot(tm, jnp.concatenate([v * b, kb * eg], axis=1))
           for tm, v, b, kb, eg in zip(tmats, vs, betas, kbs, egcs)]
    us = [uw[:, :HEAD_DIM] for uw in uws]
    ws = [uw[:, HEAD_DIM:] for uw in uws]
    qkm = [_bdot_nt(q, k) for q, k in zip(qs, ks)]
    qkm = [jnp.where(incl, x * dc, 0.0) for x, dc in zip(qkm, decays)]
    q_decs = [q * eg for q, eg in zip(qs, egcs)]
    k_ends = [k * jnp.exp(gl - gc) for k, gl, gc in zip(ks, gls, gcs)]

    for hh in heads:
        vn_ref[hh] = jnp.zeros((tb, HEAD_DIM), F32)
    outs = [[] for _ in heads]
    for ci in range(tb // CHUNK):
        cs = slice(ci * CHUNK, (ci + 1) * CHUNK)
        sts = [state_ref[hh] for hh in heads]
        ws_qs = [_bdot(jnp.concatenate([ws[hh][cs], q_decs[hh][cs]], axis=0), sts[hh]) for hh in heads]
        v_news = [us[hh][cs] - ws_qs[hh][:CHUNK] for hh in heads]
        for hh in heads:
            vn_ref[hh, cs, :] = v_news[hh]
        intra = [_bdot(qkm[hh][cs], vn_ref[hh]) for hh in heads]
        upd = [_bdot_tn(k_ends[hh][cs], v_news[hh]) for hh in heads]
        for hh in heads:
            outs[hh].append(ws_qs[hh][CHUNK:] + intra[hh])
            g_last = gls[hh][ci * CHUNK:ci * CHUNK + 1, :]
            state_ref[hh] = sts[hh] * jnp.exp(g_last) + upd[hh]
    for hh in heads:
        o = jnp.concatenate(outs[hh], axis=0)
        o = o * lax.rsqrt(jnp.mean(o * o, axis=-1, keepdims=True) + EPS) * gout_ref[...]
        zz = z_ref[:, hsl[hh]].astype(F32)
        o_ref[:, hsl[hh]] = (o * (zz * jax.nn.sigmoid(zz))).astype(o_ref.dtype)


def _gated_delta(big, small, small_t, conv_w, g_out, bsz, seq, n_heads, d_model):
    t = bsz * seq
    tb = GDN_TB
    ns = seq // tb
    width = GDN_HG * HEAD_DIM
    nhg = n_heads // GDN_HG
    blocks_per_group = d_model // width
    rows_t = small_t.shape[0]

    def colspec(group):
        return pl.BlockSpec((tb, width), lambda b, h, s: (b * ns + s, group * blocks_per_group + h))

    def cwspec(group):
        return pl.BlockSpec((CONV_WIDTH, width), lambda b, h, s: (0, group * blocks_per_group + h))

    return pl.pallas_call(
        functools.partial(_gdn_kernel, n_heads=n_heads),
        grid=(bsz, nhg, ns),
        in_specs=[
            colspec(0), colspec(1), colspec(2), colspec(3),
            pl.BlockSpec((tb, LANES), lambda b, h, s: (b * ns + s, 0)),
            pl.BlockSpec((rows_t, tb), lambda b, h, s: (0, b * ns + s)),
            cwspec(0), cwspec(1), cwspec(2),
            pl.BlockSpec((1, HEAD_DIM), lambda b, h, s: (0, 0)),
        ],
        out_specs=pl.BlockSpec((tb, width), lambda b, h, s: (b * ns + s, h)),
        out_shape=jax.ShapeDtypeStruct((t, d_model), BF16),
        scratch_shapes=[
            pltpu.VMEM((GDN_HG, HEAD_DIM, HEAD_DIM), F32),
            pltpu.VMEM((tb + SUBLANES, width), F32),
            pltpu.VMEM((tb + SUBLANES, width), F32),
            pltpu.VMEM((tb + SUBLANES, width), F32),
            pltpu.VMEM((GDN_HG, tb, HEAD_DIM), F32),
        ],
        compiler_params=_cparams(("parallel", "parallel", "arbitrary")),
        name="gated_delta",
    )(big, big, big, big, small, small_t, conv_w, conv_w, conv_w, g_out)


def _t5_bucket(n):
    max_exact = N_BUCKETS // 2
    nf = jnp.maximum(n, 1).astype(F32)
    large = max_exact + (jnp.log(nf / max_exact) / math.log(MAX_DISTANCE / max_exact)
                         * (N_BUCKETS - max_exact)).astype(jnp.int32)
    large = jnp.minimum(large, N_BUCKETS - 1)
    return jnp.where(n < max_exact, n, large)


def _attn_kernel(rb_ref, q_ref, k_ref, v_ref, lam_ref, gsub_ref, o_ref,
                 bias_ref, m_ref, acc_ref, sa_ref, sb_ref, *, lam_init):
    h = pl.program_id(0)
    b = pl.program_id(1)
    qi = pl.program_id(2)
    bq, bk = ATT_BQ, ATT_BK

    @pl.when(jnp.logical_and(b == 0, qi == 0))
    def _():
        i = lax.broadcasted_iota(jnp.int32, (bq, bk), 0)
        jj = lax.broadcasted_iota(jnp.int32, (bq, bk), 1)
        far = rb_ref[N_BUCKETS - 1, h]
        bias_ref[2] = jnp.zeros((bq, bk), F32)
        for slot in range(2):
            n = i - jj + slot * bk
            bucket = _t5_bucket(jnp.maximum(n, 0))
            bias = jnp.zeros((bq, bk), F32)
            for cc in range(N_BUCKETS):
                bias = jnp.where(bucket == cc, rb_ref[cc, h] - far, bias)
            if slot == 0:
                bias = jnp.where(n >= 0, bias, NEG_BIG)
            bias_ref[slot] = bias

    m_ref[...] = jnp.full(m_ref.shape, NEG_BIG, F32)
    acc_ref[...] = jnp.zeros(acc_ref.shape, F32)

    q = q_ref[...]
    lane = lax.broadcasted_iota(jnp.int32, q.shape, 1)
    zero = jnp.zeros_like(q)
    qs = jnp.concatenate([jnp.where(lane < DH_DIFF, q, zero), jnp.where(lane < DH_DIFF, zero, q)], axis=0)
    ones_col = (lax.broadcasted_iota(jnp.int32, (bk, HEAD_DIM), 1) == 0).astype(BF16)

    def scores(j):
        ks = pl.multiple_of(j * bk, bk)
        return lax.dot_general(qs, k_ref[pl.ds(ks, bk), :], (((1,), (1,)), ((), ())),
                               preferred_element_type=F32)

    def absorb(j, sc_ref, biased=True):
        ks = pl.multiple_of(j * bk, bk)
        v_ext = jnp.concatenate([v_ref[pl.ds(ks, bk), :], ones_col], axis=1)
        if biased:
            bias = bias_ref[jnp.minimum(qi - j, 2)]
            sc = jnp.concatenate([sc_ref[0:bq, :] + bias, sc_ref[bq:2 * bq, :] + bias], axis=0)
        else:
            sc = sc_ref[...]
        m_old = m_ref[...]
        m_new = jnp.maximum(m_old, jnp.max(sc, axis=-1, keepdims=True))
        p = jnp.exp(sc - m_new)
        acc_ref[...] = (jnp.exp(m_old - m_new) * acc_ref[...]
                        + jnp.dot(p.astype(BF16), v_ext, preferred_element_type=F32))
        m_ref[...] = m_new

    n_tiles = qi + 1
    sa_ref[...] = scores(0)

    def pair_body(jj, carry, biased):
        j0 = 2 * jj
        sb_ref[...] = scores(j0 + 1)
        absorb(j0, sa_ref, biased)
        sa_ref[...] = scores(jnp.minimum(j0 + 2, qi))
        absorb(j0 + 1, sb_ref, biased)
        return carry

    n_far_pairs = jnp.maximum(qi - 1, 0) // 2
    lax.fori_loop(0, n_far_pairs, functools.partial(pair_body, biased=False), 0)
    lax.fori_loop(n_far_pairs, n_tiles // 2, functools.partial(pair_body, biased=True), 0)

    @pl.when(n_tiles % 2 == 1)
    def _():
        absorb(qi, sa_ref)

    lam_p = lam_ref[...]
    s1 = jnp.sum(lam_p[0:1] * lam_p[1:2], axis=-1, keepdims=True)
    s2 = jnp.sum(lam_p[2:3] * lam_p[3:4], axis=-1, keepdims=True)
    lam = jnp.exp(s1) - jnp.exp(s2) + lam_init
    acc = acc_ref[...]
    num = acc[:, :HEAD_DIM]
    den = acc[:, HEAD_DIM:HEAD_DIM + 1]
    o = num[:bq] / den[:bq] - lam * (num[bq:] / den[bq:])
    o = o * lax.rsqrt(jnp.mean(o * o, axis=-1, keepdims=True) + EPS) * gsub_ref[...]
    o_ref[...] = (o * (1.0 - lam_init)).astype(o_ref.dtype)


def _diff_attention(proj_qk, proj_plain, rel_bias, lam_params, g_subln, bsz, seq, n_heads, d_model,
                    lam_init):
    t = bsz * seq
    nq = seq // ATT_BQ
    per = d_model // HEAD_DIM
    vcol = 4 * per
    return pl.pallas_call(
        functools.partial(_attn_kernel, lam_init=lam_init),
        grid=(n_heads, bsz, nq),
        in_specs=[
            pl.BlockSpec(memory_space=pltpu.SMEM),
            pl.BlockSpec((ATT_BQ, HEAD_DIM), lambda h, b, i: (b * nq + i, h)),
            pl.BlockSpec((seq, HEAD_DIM), lambda h, b, i: (b, per + h)),
            pl.BlockSpec((seq, HEAD_DIM), lambda h, b, i: (b, vcol + h)),
            pl.BlockSpec((4, DH_DIFF), lambda h, b, i: (0, 0)),
            pl.BlockSpec((1, HEAD_DIM), lambda h, b, i: (0, 0)),
        ],
        out_specs=pl.BlockSpec((ATT_BQ, HEAD_DIM), lambda h, b, i: (b * nq + i, h)),
        out_shape=jax.ShapeDtypeStruct((t, d_model), BF16),
        scratch_shapes=[
            pltpu.VMEM((3, ATT_BQ, ATT_BK), F32),
            pltpu.VMEM((2 * ATT_BQ, 1), F32),
            pltpu.VMEM((2 * ATT_BQ, 2 * HEAD_DIM), F32),
            pltpu.VMEM((2 * ATT_BQ, ATT_BK), F32),
            pltpu.VMEM((2 * ATT_BQ, ATT_BK), F32),
        ],
        compiler_params=_cparams(("arbitrary", "arbitrary", "arbitrary")),
        name="diff_attention",
    )(rel_bias, proj_qk, proj_qk, proj_plain, lam_params, g_subln)


def _mix_kernel(ga_ref, gb_ref, oa_ref, od_ref, x_ref, wo_ref, gffn_ref, wr_ref, br_ref,
                x1_ref, h2_ref, topi_ref, topw_ref, rank_ref, cnt_ref, carry_ref):
    i = pl.program_id(0)
    tm = MIX_TM

    @pl.when(i == 0)
    def _():
        carry_ref[...] = jnp.zeros_like(carry_ref)

    mix = (ga_ref[...].astype(F32) * oa_ref[...].astype(F32)
           + gb_ref[...].astype(F32) * od_ref[...].astype(F32))
    x1 = x_ref[...] + jnp.dot(mix.astype(BF16), wo_ref[...], preferred_element_type=F32)
    x1_ref[...] = x1
    h2 = x1 * lax.rsqrt(jnp.mean(x1 * x1, axis=-1, keepdims=True) + EPS) * gffn_ref[...]
    h2_ref[...] = _pack_halves(h2)

    logits = lax.dot_general(wr_ref[...], h2, (((1,), (1,)), ((), ())),
                             preferred_element_type=F32, precision=lax.Precision.HIGHEST) + br_ref[...]
    eidx = lax.broadcasted_iota(jnp.int32, logits.shape, 0).astype(F32)
    vals, hots = [], []
    cur = logits
    for kk in range(TOP_K):
        mx = jnp.max(cur, axis=0, keepdims=True)
        idx = jnp.min(jnp.where(cur == mx, eidx, float(N_EXPERTS)), axis=0, keepdims=True)
        hot = eidx == idx
        vals.append(mx)
        hots.append(hot)
        topi_ref[kk:kk + 1, :] = idx.astype(jnp.int32)
        cur = jnp.where(hot, -jnp.inf, cur)
    exps = [jnp.exp(vv - vals[0]) for vv in vals]
    denom = exps[0] + exps[1] + exps[2] + exps[3]
    for kk in range(TOP_K):
        topw_ref[kk:kk + 1, :] = exps[kk] / denom

    sel = hots[0]
    for kk in range(1, TOP_K):
        sel = jnp.logical_or(sel, hots[kk])
    sel_f = sel.astype(F32)
    r = lax.broadcasted_iota(jnp.int32, (tm, tm), 0)
    c = lax.broadcasted_iota(jnp.int32, (tm, tm), 1)
    before = _bdot(sel_f, (r < c).astype(F32)) + carry_ref[...]
    for kk in range(TOP_K):
        rank_ref[kk:kk + 1, :] = jnp.sum(jnp.where(hots[kk], before, 0.0), axis=0,
                                         keepdims=True).astype(jnp.int32)
    carry_ref[...] = carry_ref[...] + jnp.sum(sel_f, axis=-1, keepdims=True)
    cnt_ref[...] = carry_ref[...].astype(jnp.int32)


def _mix_project_route(proj_gate, oa, od, x2d, w_o, g_ffn, w_r_t, b_r, d_model):
    t = x2d.shape[0]
    tm = MIX_TM
    full = lambda shape: pl.BlockSpec(shape, lambda i: (0, 0))
    row = lambda: pl.BlockSpec((tm, d_model), lambda i: (i, 0))
    krow = lambda: pl.BlockSpec((TOP_K, tm), lambda i: (0, i))
    return pl.pallas_call(
        _mix_kernel,
        grid=(t // tm,),
        in_specs=[
            pl.BlockSpec((tm, d_model), lambda i: (i, 0)),
            pl.BlockSpec((tm, d_model), lambda i: (i, 1)),
            row(), row(), row(),
            full((d_model, d_model)), full((1, d_model)), full((N_EXPERTS, d_model)), full((N_EXPERTS, 1)),
        ],
        out_specs=[row(), pl.BlockSpec((tm, d_model // 2), lambda i: (i, 0)),
                   krow(), krow(), krow(), full((N_EXPERTS, 1))],
        out_shape=[
            jax.ShapeDtypeStruct((t, d_model), F32),
            jax.ShapeDtypeStruct((t, d_model // 2), jnp.int32),
            jax.ShapeDtypeStruct((TOP_K, t), jnp.int32),
            jax.ShapeDtypeStruct((TOP_K, t), F32),
            jax.ShapeDtypeStruct((TOP_K, t), jnp.int32),
            jax.ShapeDtypeStruct((N_EXPERTS, 1), jnp.int32),
        ],
        scratch_shapes=[pltpu.VMEM((N_EXPERTS, 1), F32)],
        compiler_params=_cparams(("arbitrary",)),
        name="merge_outproj_route",
    )(proj_gate, proj_gate, oa, od, x2d, w_o, g_ffn, w_r_t, b_r)


def _pack_halves(x):
    half = x.shape[1] // 2
    bits = pltpu.bitcast(x.astype(BF16).astype(F32), jnp.int32)
    return bits[:, :half] | lax.shift_right_logical(bits[:, half:], 16)


def _unpack_halves(p):
    hi = pltpu.bitcast(p & jnp.int32(-65536), F32)
    lo = pltpu.bitcast(lax.shift_left(p, 16), F32)
    return jnp.concatenate([hi, lo], axis=1)


def _expert_kernel(be_ref, nu_ref, x_ref, wup_ref, bup_ref, wdn_ref, bdn_ref, y_ref, wup_bf, wdn_bf):
    i = pl.program_id(0)
    d_ff = wdn_ref.shape[1]

    @pl.when(jnp.logical_or(i == 0, be_ref[i] != be_ref[jnp.maximum(i - 1, 0)]))
    def _():
        rr = lax.broadcasted_iota(jnp.int32, (2 * LANES, 2 * LANES), 0)
        cc = lax.broadcasted_iota(jnp.int32, (2 * LANES, 2 * LANES), 1)
        pick = jnp.where(cc < LANES, 2 * cc, 2 * (cc - LANES) + 1)
        perm = (rr == pick).astype(BF16)
        for g in range(wup_ref.shape[2] // (2 * LANES)):
            cs = slice(g * 2 * LANES, (g + 1) * 2 * LANES)
            wup_bf[:, cs] = jnp.dot(wup_ref[0, :, cs].astype(BF16), perm,
                                    preferred_element_type=F32).astype(BF16)
        wdn_bf[...] = wdn_ref[0].astype(BF16)

    @pl.when(i < nu_ref[0])
    def _():
        x = _unpack_halves(x_ref[...])
        hid = jnp.dot(x.astype(BF16), wup_bf[...], preferred_element_type=F32) + bup_ref[0]
        acts = []
        for g in range(hid.shape[1] // (2 * LANES)):
            glu = jnp.minimum(hid[:, g * 2 * LANES:g * 2 * LANES + LANES], SWIGLU_LIMIT)
            lin = jnp.clip(hid[:, g * 2 * LANES + LANES:(g + 1) * 2 * LANES], -SWIGLU_LIMIT, SWIGLU_LIMIT)
            acts.append(glu * jax.nn.sigmoid(SWIGLU_ALPHA * glu) * (lin + 1.0))
        act = jnp.concatenate(acts, axis=1)
        assert act.shape[1] == d_ff
        y = jnp.dot(act.astype(BF16), wdn_bf[...], preferred_element_type=F32) + bdn_ref[0]
        y_ref[...] = _pack_halves(y)

    @pl.when(i >= nu_ref[0])
    def _():
        y_ref[...] = jnp.zeros(y_ref.shape, y_ref.dtype)


def _experts(block_e, n_used, xs, w_up, b_up, w_down, b_down):
    n_rows, half = xs.shape
    d = w_up.shape[1]
    nb = n_rows // MOE_RB
    two_ff = w_up.shape[2]
    d_ff = w_down.shape[1]
    grid_spec = pltpu.PrefetchScalarGridSpec(
        num_scalar_prefetch=2,
        grid=(nb,),
        in_specs=[
            pl.BlockSpec((MOE_RB, half), lambda i, be, nu: (jnp.minimum(i, nu[0] - 1), 0)),
            pl.BlockSpec((1, d, two_ff), lambda i, be, nu: (be[i], 0, 0)),
            pl.BlockSpec((1, 1, two_ff), lambda i, be, nu: (be[i], 0, 0)),
            pl.BlockSpec((1, d_ff, d), lambda i, be, nu: (be[i], 0, 0)),
            pl.BlockSpec((1, 1, d), lambda i, be, nu: (be[i], 0, 0)),
        ],
        out_specs=pl.BlockSpec((MOE_RB, half), lambda i, be, nu: (i, 0)),
        scratch_shapes=[pltpu.VMEM((d, two_ff), BF16), pltpu.VMEM((d_ff, d), BF16)],
    )
    return pl.pallas_call(
        _expert_kernel,
        grid_spec=grid_spec,
        out_shape=jax.ShapeDtypeStruct((n_rows, half), jnp.int32),
        compiler_params=_cparams(("arbitrary",)),
        name="moe_experts",
    )(block_e, n_used, xs, w_up, b_up, w_down, b_down)


def _sc_invert_slots(dest_flat, n_rows):
    n_assign = dest_flat.shape[0]
    n_workers = SC_CORES * SC_SUBCORES
    rows_per_w = n_rows // n_workers
    chunk = SC_SCAN_CHUNK
    assert n_rows % n_workers == 0 and rows_per_w % SC_LANES == 0 and n_assign % chunk == 0
    mesh = plsc.VectorSubcoreMesh(core_axis_name="c", subcore_axis_name="s",
                                  num_cores=SC_CORES, num_subcores=SC_SUBCORES)

    def body(dest_hbm, out_hbm, dest_v, map_v):
        wid = lax.axis_index("s") * SC_CORES + lax.axis_index("c")
        base = wid * rows_per_w
        lanes = lax.broadcasted_iota(jnp.int32, (SC_LANES,), 0)

        @pl.loop(0, rows_per_w, step=SC_LANES)
        def _(r0):
            map_v[pl.ds(r0, SC_LANES)] = jnp.full((SC_LANES,), -1, jnp.int32)

        @pl.loop(0, n_assign // chunk)
        def _(ci):
            pltpu.sync_copy(dest_hbm.at[pl.ds(ci * chunk, chunk)], dest_v)

            @pl.loop(0, chunk, step=SC_LANES)
            def _(j):
                local = dest_v[pl.ds(j, SC_LANES)] - base
                mine = jnp.logical_and(local >= 0, local < rows_per_w)
                plsc.store_scatter(map_v, [jnp.where(mine, local, 0)], ci * chunk + j + lanes, mask=mine)

        pltpu.sync_copy(map_v, out_hbm.at[pl.ds(base, rows_per_w)])

    return pl.kernel(
        body,
        out_type=jax.ShapeDtypeStruct((n_rows,), jnp.int32),
        mesh=mesh,
        scratch_types=[pltpu.VMEM((chunk,), jnp.int32), pltpu.VMEM((rows_per_w,), jnp.int32)],
        compiler_params=pltpu.CompilerParams(needs_layout_passes=False),
        name="moe_slot_inverse",
    )(dest_flat)


def _sc_gather_rows(table, idx):
    n_idx = idx.shape[0]
    d = table.shape[1]
    n_workers = SC_CORES * SC_SUBCORES
    per_worker = n_idx // n_workers
    n_chunks = per_worker // SC_GATHER_ROWS
    assert n_idx % n_workers == 0 and per_worker % SC_GATHER_ROWS == 0
    mesh = plsc.VectorSubcoreMesh(core_axis_name="c", subcore_axis_name="s",
                                  num_cores=SC_CORES, num_subcores=SC_SUBCORES)

    assert n_chunks % 2 == 0

    def body(table_hbm, idx_hbm, out_hbm, idx_v, rows_a, rows_b, sem_a, sem_b):
        wid = lax.axis_index("s") * SC_CORES + lax.axis_index("c")
        base = wid * per_worker
        pltpu.sync_copy(idx_hbm.at[pl.ds(base, per_worker)], idx_v)

        def gather(ci, rows_v, sem):
            off = pl.multiple_of(ci * SC_GATHER_ROWS, SC_GATHER_ROWS)
            return pltpu.make_async_copy(table_hbm.at[idx_v.at[pl.ds(off, SC_GATHER_ROWS)]], rows_v, sem)

        def put(ci, rows_v):
            off = pl.multiple_of(ci * SC_GATHER_ROWS, SC_GATHER_ROWS)
            pltpu.sync_copy(rows_v, out_hbm.at[pl.ds(base + off, SC_GATHER_ROWS)])

        gather(0, rows_a, sem_a).start()

        @pl.loop(0, n_chunks, step=2)
        def _(ci):
            gather(ci + 1, rows_b, sem_b).start()
            gather(ci, rows_a, sem_a).wait()
            put(ci, rows_a)
            nxt = jnp.minimum(ci + 2, n_chunks - 1)
            gather(nxt, rows_a, sem_a).start()
            gather(ci + 1, rows_b, sem_b).wait()
            put(ci + 1, rows_b)

        gather(n_chunks - 1, rows_a, sem_a).wait()

    return pl.kernel(
        body,
        out_type=jax.ShapeDtypeStruct((n_idx, d), table.dtype),
        mesh=mesh,
        scratch_types=[
            pltpu.VMEM((per_worker,), jnp.int32),
            pltpu.VMEM((SC_GATHER_ROWS, d), table.dtype),
            pltpu.VMEM((SC_GATHER_ROWS, d), table.dtype),
            pltpu.SemaphoreType.DMA,
            pltpu.SemaphoreType.DMA,
        ],
        name="moe_slot_gather",
    )(table, idx)


def _combine_kernel(x1_ref, w_ref, y0_ref, y1_ref, y2_ref, y3_ref, o_ref):
    w = w_ref[...]
    out = x1_ref[...]
    for kk, y_ref in enumerate((y0_ref, y1_ref, y2_ref, y3_ref)):
        out = out + w[:, kk:kk + 1] * _unpack_halves(y_ref[...])
    o_ref[...] = out


def _combine(x1, w_tok, y_slots):
    t, d = x1.shape
    tc = COMB_TC
    nt = t // tc
    yspec = lambda kk: pl.BlockSpec((tc, d // 2), lambda i: (kk * nt + i, 0))
    return pl.pallas_call(
        _combine_kernel,
        grid=(nt,),
        in_specs=[
            pl.BlockSpec((tc, d), lambda i: (i, 0)),
            pl.BlockSpec((tc, TOP_K), lambda i: (i, 0)),
            yspec(0), yspec(1), yspec(2), yspec(3),
        ],
        out_specs=pl.BlockSpec((tc, d), lambda i: (i, 0)),
        out_shape=jax.ShapeDtypeStruct((t, d), F32),
        compiler_params=_cparams(("parallel",)),
        name="moe_combine",
    )(x1, w_tok, y_slots, y_slots, y_slots, y_slots)


def _moe(x1, h2, topi, topw, rank, counts, w_up, b_up, w_down, b_down):
    t, d = x1.shape
    n_assign = t * TOP_K
    nb = -(-n_assign // MOE_RB) + N_EXPERTS
    n_rows = nb * MOE_RB
    counts = counts[:, 0]
    padded = (counts + MOE_RB - 1) // MOE_RB * MOE_RB
    padded_end = jnp.cumsum(padded)
    padded_start = padded_end - padded
    expert_ids = jnp.arange(N_EXPERTS, dtype=jnp.int32)[:, None, None]
    start_of = jnp.sum(jnp.where(topi[None] == expert_ids, padded_start[:, None, None], 0), axis=0)
    dest = (start_of + rank).astype(jnp.int32)
    n_used = (padded_end[-1] // MOE_RB).astype(jnp.int32)
    blk = jnp.minimum(jnp.arange(nb, dtype=jnp.int32), n_used - 1)
    block_e = jnp.minimum(jnp.sum(padded_end[None, :] <= (blk * MOE_RB)[:, None], axis=1),
                          N_EXPERTS - 1).astype(jnp.int32)
    slot_of = _sc_invert_slots(dest.reshape(-1), n_rows)
    src_tok = jnp.where(slot_of < 0, 0, slot_of % t)

    xs = _sc_gather_rows(h2, src_tok)
    y_rows = _experts(block_e, n_used.reshape(1), xs, w_up, b_up, w_down, b_down)
    y_slots = _sc_gather_rows(y_rows, dest.reshape(-1))
    return _combine(x1, topw.T, y_slots)


def kernel(x, g_mix, w_in, b_gate, conv_w, a_log, dt_bias, g_delta_out, q_norm, k_norm, lambda_q1, lambda_k1, lambda_q2, lambda_k2, g_subln, rel_bias, w_o, g_ffn, w_router, b_router, w_up, b_up, w_down, b_down):
    bsz, seq, d = x.shape
    depth = g_mix.shape[0]
    n_heads = d // HEAD_DIM
    t = bsz * seq
    d_ff = w_down.shape[2]
    assert d % PROJ_TN == 0 and t % PROJ_TM == 0 and seq % GDN_TB == 0 and seq % ATT_BQ == 0
    assert t % MIX_TM == 0 and t % COMB_TC == 0 and n_heads % GDN_HG == 0
    assert (t * TOP_K) % MOE_RB == 0
    assert 2 * n_heads <= 2 * SUBLANES

    x2d = x.reshape(t, d)
    for l in range(depth):
        wl = w_in[l]
        c0 = 4 * d
        c1 = c0 + 2 * n_heads
        c2 = c1 + 2 * d
        c3 = c2 + d
        w_small = jnp.pad(wl[:, c0:c1], ((0, 0), (0, LANES - 2 * n_heads)))
        gm = g_mix[l].reshape(1, d)
        w_plain = jnp.concatenate([wl[:, :c0], wl[:, c2:c3]], axis=1).astype(BF16)
        proj_plain = _input_projection(x2d, gm, w_plain, jnp.zeros((1, 5 * d), F32), "plain")
        qk_gain = jnp.concatenate([jnp.tile(q_norm[l] * (DH_DIFF ** -0.5), 2 * n_heads),
                                   jnp.tile(k_norm[l], 2 * n_heads)]).reshape(1, 2 * d)
        proj_qk = _input_projection(x2d, gm, wl[:, c1:c2].astype(BF16), qk_gain, "qknorm")
        proj_gate = _input_projection(x2d, gm, wl[:, c3:].astype(BF16), b_gate[l].reshape(1, 2 * d), "gate")

        head_pad = jnp.zeros((LANES - 2 * n_heads,), F32)
        alog = jnp.concatenate([jnp.zeros((n_heads,), F32), a_log[l], head_pad])
        dtb = jnp.concatenate([jnp.zeros((n_heads,), F32), dt_bias[l], head_pad])
        rows_t = 2 * n_heads
        small, small_t = _small_projection(
            x2d, g_mix[l].reshape(1, d), w_small.astype(BF16), w_small[:, :rows_t].T.astype(BF16),
            alog.reshape(1, LANES), dtb.reshape(1, LANES),
            alog[:rows_t].reshape(rows_t, 1), dtb[:rows_t].reshape(rows_t, 1), n_heads)

        oa = _gated_delta(proj_plain, small, small_t, conv_w[l], g_delta_out[l].reshape(1, HEAD_DIM),
                          bsz, seq, n_heads, d)

        lam_init = 0.8 - 0.6 * math.exp(-0.3 * l)
        lam_params = jnp.stack([lambda_q1[l], lambda_k1[l], lambda_q2[l], lambda_k2[l]])
        od = _diff_attention(proj_qk, proj_plain, rel_bias, lam_params, g_subln[l].reshape(1, HEAD_DIM),
                             bsz, seq, n_heads, d, lam_init)

        x1, h2, topi, topw, rank, counts = _mix_project_route(
            proj_gate, oa, od, x2d, w_o[l].astype(BF16), g_ffn[l].reshape(1, d),
            w_router[l].T, b_router[l].reshape(N_EXPERTS, 1), d)

        b_up_l = b_up[l].reshape(N_EXPERTS, 2 * d_ff // (2 * LANES), LANES, 2)
        b_up_l = jnp.swapaxes(b_up_l, 2, 3).reshape(N_EXPERTS, 1, 2 * d_ff)
        x2d = _moe(x1, h2, topi, topw, rank, counts, w_up[l], b_up_l,
                   w_down[l], b_down[l].reshape(N_EXPERTS, 1, d))
    return x2d.reshape(bsz, seq, d)
```

```python
import functools
import math

import jax
import jax.numpy as jnp
from jax import lax
from jax.experimental import pallas as pl
from jax.experimental.pallas import tpu as pltpu
from jax.experimental.pallas import tpu_sc as plsc

F32 = jnp.float32
BF16 = jnp.bfloat16

HEAD_DIM = 128
DH_DIFF = HEAD_DIM // 2
CONV_WIDTH = 4
CHUNK = 64
N_BUCKETS = 32
MAX_DISTANCE = 128
N_EXPERTS = 32
TOP_K = 4
TOP_K_SHIFT = 2
SWIGLU_LIMIT = 7.0
SWIGLU_ALPHA = 1.702
EPS = 1e-6
NEG_BIG = -1e30

LANES = 128
SUBLANES = 8
VMEM_LIMIT = 56 * 1024 * 1024
SC_CORES = 2
SC_SUBCORES = 16
SC_LANES = 16
SC_GATHER_ROWS = 64
SC_SCAN_CHUNK = 4096

PROJ_TM = 2048
PROJ_TN = 1024
PROJ_CHUNK = 256
GDN_TB = 256
GDN_HG = 4
ATT_BQ = 512
ATT_BK = 512
MIX_TM = 512
MOE_RB = 256
COMB_TC = 512


def _cparams(sem):
    return pltpu.CompilerParams(dimension_semantics=sem, vmem_limit_bytes=VMEM_LIMIT)


def _bdot(a, b):
    return jnp.dot(a.astype(BF16), b.astype(BF16), preferred_element_type=F32)


def _bdot_nt(a, b):
    return lax.dot_general(a.astype(BF16), b.astype(BF16), (((1,), (1,)), ((), ())),
                           preferred_element_type=F32)


def _bdot_tn(a, b):
    return lax.dot_general(a.astype(BF16), b.astype(BF16), (((0,), (0,)), ((), ())),
                           preferred_element_type=F32)


def _proj_kernel(x_ref, g_ref, w_ref, aux_ref, o_ref, h_ref, *, mode):
    @pl.when(pl.program_id(1) == 0)
    def _():
        x = x_ref[...]
        ms = jnp.mean(x * x, axis=-1, keepdims=True)
        h_ref[...] = (x * lax.rsqrt(ms + EPS) * g_ref[...]).astype(BF16)

    h = h_ref[...]
    lo = lax.broadcasted_iota(jnp.int32, (1, LANES), 1) < DH_DIFF
    for c in range(PROJ_TN // PROJ_CHUNK):
        cs = slice(c * PROJ_CHUNK, (c + 1) * PROJ_CHUNK)
        acc = jnp.dot(h, w_ref[:, cs], preferred_element_type=F32)
        if mode == "plain":
            o_ref[:, cs] = acc.astype(o_ref.dtype)
        elif mode == "gate":
            o_ref[:, cs] = jax.nn.sigmoid(acc + aux_ref[:, cs]).astype(o_ref.dtype)
        else:
            for g in range(PROJ_CHUNK // LANES):
                sl = slice(c * PROJ_CHUNK + g * LANES, c * PROJ_CHUNK + (g + 1) * LANES)
                y = acc[:, g * LANES:(g + 1) * LANES]
                y2 = y * y
                s_lo = jnp.sum(jnp.where(lo, y2, 0.0), axis=-1, keepdims=True)
                s_hi = jnp.sum(jnp.where(lo, 0.0, y2), axis=-1, keepdims=True)
                r = jnp.where(lo, lax.rsqrt(s_lo / DH_DIFF + EPS), lax.rsqrt(s_hi / DH_DIFF + EPS))
                o_ref[:, sl] = (y * r * aux_ref[:, sl]).astype(o_ref.dtype)


def _input_projection(x2d, g_mix, w, aux, mode):
    t, d = x2d.shape
    n = w.shape[1]
    return pl.pallas_call(
        functools.partial(_proj_kernel, mode=mode),
        grid=(t // PROJ_TM, n // PROJ_TN),
        in_specs=[
            pl.BlockSpec((PROJ_TM, d), lambda i, j: (i, 0)),
            pl.BlockSpec((1, d), lambda i, j: (0, 0)),
            pl.BlockSpec((d, PROJ_TN), lambda i, j: (0, j)),
            pl.BlockSpec((1, PROJ_TN), lambda i, j: (0, j)),
        ],
        out_specs=pl.BlockSpec((PROJ_TM, PROJ_TN), lambda i, j: (i, j)),
        out_shape=jax.ShapeDtypeStruct((t, n), BF16),
        scratch_shapes=[pltpu.VMEM((PROJ_TM, d), BF16)],
        compiler_params=_cparams(("parallel", "arbitrary")),
        name="input_projection_" + mode,
    )(x2d, g_mix, w, aux)


def _small_proj_kernel(x_ref, g_ref, w_ref, wt_ref, alog_ref, dtb_ref, alog_t_ref, dtb_t_ref,
                       o_ref, ot_ref, *, n_heads):
    x = x_ref[...]
    ms = jnp.mean(x * x, axis=-1, keepdims=True)
    h = (x * lax.rsqrt(ms + EPS) * g_ref[...]).astype(BF16)

    def finish(acc, idx, alog, dtb):
        beta = jax.nn.sigmoid(acc)
        z = acc + dtb
        softplus = jnp.maximum(z, 0.0) + jnp.log1p(jnp.exp(-jnp.abs(z)))
        gdec = -jnp.exp(alog) * softplus
        return jnp.where(idx < n_heads, beta, jnp.where(idx < 2 * n_heads, gdec, 0.0))

    acc = jnp.dot(h, w_ref[...], preferred_element_type=F32)
    lane = lax.broadcasted_iota(jnp.int32, acc.shape, 1)
    o_ref[...] = finish(acc, lane, alog_ref[...], dtb_ref[...])
    acc_t = lax.dot_general(wt_ref[...], h, (((1,), (1,)), ((), ())),
                            preferred_element_type=F32)
    sub = lax.broadcasted_iota(jnp.int32, acc_t.shape, 0)
    ot_ref[...] = finish(acc_t, sub, alog_t_ref[...], dtb_t_ref[...])


def _small_projection(x2d, g_mix, w_small, w_small_t, alog, dtb, alog_t, dtb_t, n_heads):
    t, d = x2d.shape
    rows_t = w_small_t.shape[0]
    tm = PROJ_TM
    full = lambda shape: pl.BlockSpec(shape, lambda i: (0, 0))
    return pl.pallas_call(
        functools.partial(_small_proj_kernel, n_heads=n_heads),
        grid=(t // tm,),
        in_specs=[
            pl.BlockSpec((tm, d), lambda i: (i, 0)),
            full((1, d)), full((d, LANES)), full((rows_t, d)),
            full((1, LANES)), full((1, LANES)), full((rows_t, 1)), full((rows_t, 1)),
        ],
        out_specs=[pl.BlockSpec((tm, LANES), lambda i: (i, 0)),
                   pl.BlockSpec((rows_t, tm), lambda i: (0, i))],
        out_shape=[jax.ShapeDtypeStruct((t, LANES), F32),
                   jax.ShapeDtypeStruct((rows_t, t), F32)],
        compiler_params=_cparams(("parallel",)),
        name="beta_decay_projection",
    )(x2d, g_mix, w_small, w_small_t, alog, dtb, alog_t, dtb_t)


def _gdn_kernel(q_ref, k_ref, v_ref, z_ref, sm_ref, smt_ref, cwq_ref, cwk_ref, cwv_ref, gout_ref,
                o_ref, state_ref, qp_ref, kp_ref, vp_ref, vn_ref, *, n_heads):
    hg = pl.program_id(1)
    s = pl.program_id(2)
    tb = GDN_TB
    pad = SUBLANES
    width = GDN_HG * HEAD_DIM

    @pl.when(s == 0)
    def _():
        state_ref[...] = jnp.zeros_like(state_ref)
        for p_ref in (qp_ref, kp_ref, vp_ref):
            p_ref[0:pad, :] = jnp.zeros((pad, width), F32)

    def conv_silu(x_ref, p_ref, cw_ref):
        p_ref[pad:pad + tb, :] = x_ref[...].astype(F32)
        acc = cw_ref[CONV_WIDTH - 1:CONV_WIDTH, :] * p_ref[pad:pad + tb, :]
        for jj in range(CONV_WIDTH - 1):
            off = pad - (CONV_WIDTH - 1) + jj
            acc = acc + cw_ref[jj:jj + 1, :] * p_ref[off:off + tb, :]
        p_ref[0:pad, :] = p_ref[tb:tb + pad, :]
        return acc * jax.nn.sigmoid(acc)

    q_all = conv_silu(q_ref, qp_ref, cwq_ref)
    k_all = conv_silu(k_ref, kp_ref, cwk_ref)
    v_all = conv_silu(v_ref, vp_ref, cwv_ref)

    r = lax.broadcasted_iota(jnp.int32, (tb, tb), 0)
    c = lax.broadcasted_iota(jnp.int32, (tb, tb), 1)
    shift = int(math.log2(CHUNK))
    same = (r >> shift) == (c >> shift)
    incl = jnp.logical_and(same, c <= r)
    strict = jnp.logical_and(same, c < r)

    small = sm_ref[...]
    small_t = smt_ref[...]
    lane = lax.broadcasted_iota(jnp.int32, small.shape, 1)
    def split3(a):
        hi = a.astype(BF16)
        r1 = a - hi.astype(F32)
        mid = r1.astype(BF16)
        lo = (r1 - mid.astype(F32)).astype(BF16)
        return hi.astype(F32), mid.astype(F32), lo.astype(F32)

    part = 2 * n_heads
    s_hi, s_mid, s_lo = split3(small)
    small3 = jnp.where(lane < part, s_hi,
                       jnp.where(lane < 2 * part, pltpu.roll(s_mid, part, 1),
                                 jnp.where(lane < 3 * part, pltpu.roll(s_lo, 2 * part, 1), 0.0)))
    both = _bdot(jnp.concatenate([incl.astype(F32), same.astype(F32)], axis=0), small3)
    gcum = both[:tb]
    gtot = both[tb:]
    gcum_t = _bdot(jnp.concatenate(split3(small_t), axis=0),
                   jnp.logical_and(same, r <= c).astype(F32))
    sub3 = lax.broadcasted_iota(jnp.int32, gcum_t.shape, 0)

    heads = range(GDN_HG)
    hsl = [slice(hh * HEAD_DIM, (hh + 1) * HEAD_DIM) for hh in heads]
    qs = [q_all[:, hs] for hs in hsl]
    ks = [k_all[:, hs] for hs in hsl]
    vs = [v_all[:, hs] for hs in hsl]
    qs = [q * lax.rsqrt(jnp.sum(q * q, axis=-1, keepdims=True) + EPS) * (HEAD_DIM ** -0.5) for q in qs]
    ks = [k * lax.rsqrt(jnp.sum(k * k, axis=-1, keepdims=True) + EPS) for k in ks]

    def col_of(arr, idx):
        return jnp.sum(jnp.where(lane == idx, arr, 0.0), axis=-1, keepdims=True)

    def terms_of(pos, idx):
        return jnp.logical_or(pos == idx, jnp.logical_or(pos == idx + part, pos == idx + 2 * part))

    head_ids = [hg * GDN_HG + hh for hh in heads]
    betas = [col_of(small, hd) for hd in head_ids]
    gcs = [jnp.sum(jnp.where(terms_of(lane, hd + n_heads), gcum, 0.0), axis=-1, keepdims=True)
           for hd in head_ids]
    gls = [jnp.sum(jnp.where(terms_of(lane, hd + n_heads), gtot, 0.0), axis=-1, keepdims=True)
           for hd in head_ids]
    gc_rows = [jnp.sum(jnp.where(terms_of(sub3, hd + n_heads), gcum_t, 0.0), axis=0, keepdims=True)
               for hd in head_ids]

    decays = [jnp.where(incl, jnp.exp(jnp.minimum(gc - gr, 0.0)), 0.0) for gc, gr in zip(gcs, gc_rows)]
    kbs = [k * b for k, b in zip(ks, betas)]
    kks = [_bdot_nt(kb, k) for kb, k in zip(kbs, ks)]
    pws = [jnp.where(strict, -(kk * dc), 0.0) for kk, dc in zip(kks, decays)]
    n_chunks = tb // CHUNK
    cat_row = lax.broadcasted_iota(jnp.int32, (CHUNK, tb), 0)
    cat_lane = lax.broadcasted_iota(jnp.int32, (CHUNK, tb), 1)
    lane_chunk = cat_lane >> shift

    def block_diag(m_cat):
        return jnp.concatenate([jnp.where(lane_chunk == ci, m_cat, 0.0) for ci in range(n_chunks)], axis=0)

    def cat_of(m_bd):
        out = m_bd[0:CHUNK]
        for ci in range(1, n_chunks):
            out = out + m_bd[ci * CHUNK:(ci + 1) * CHUNK]
        return out

    pcats = [cat_of(pw) for pw in pws]
    eye_cat = ((cat_lane & (CHUNK - 1)) == cat_row).astype(F32)
    tcats = [eye_cat + pc for pc in pcats]
    pcats = [_bdot(pc, block_diag(pc)) for pc in pcats]
    n_levels = int(math.log2(CHUNK))
    for lev in range(1, n_levels):
        bds = [block_diag(pc) for pc in pcats]
        if lev < n_levels - 1:
            prods = [_bdot(jnp.concatenate([pc, tc], axis=0), bd) for pc, tc, bd in zip(pcats, tcats, bds)]
            pcats = [pr[:CHUNK] for pr in prods]
            tcats = [tc + pr[CHUNK:] for tc, pr in zip(tcats, prods)]
        else:
            tcats = [tc + _bdot(tc, bd) for tc, bd in zip(tcats, bds)]
    tmats = [block_diag(tc) for tc in tcats]
    egcs = [jnp.exp(gc) for gc in gcs]
    uws = [_bdot(tm, jnp.concatenate([v * b, kb * eg], axis=1))
           for tm, v, b, kb, eg in zip(tmats, vs, betas, kbs, egcs)]
    us = [uw[:, :HEAD_DIM] for uw in uws]
    ws = [uw[:, HEAD_DIM:] for uw in uws]
    qkm = [_bdot_nt(q, k) for q, k in zip(qs, ks)]
    qkm = [jnp.where(incl, x * dc, 0.0) for x, dc in zip(qkm, decays)]
    q_decs = [q * eg for q, eg in zip(qs, egcs)]
    k_ends = [k * jnp.exp(gl - gc) for k, gl, gc in zip(ks, gls, gcs)]

    for hh in heads:
        vn_ref[hh] = jnp.zeros((tb, HEAD_DIM), F32)
    outs = [[] for _ in heads]
    for ci in range(tb // CHUNK):
        cs = slice(ci * CHUNK, (ci + 1) * CHUNK)
        sts = [state_ref[hh] for hh in heads]
        ws_qs = [_bdot(jnp.concatenate([ws[hh][cs], q_decs[hh][cs]], axis=0), sts[hh]) for hh in heads]
        v_news = [us[hh][cs] - ws_qs[hh][:CHUNK] for hh in heads]
        for hh in heads:
            vn_ref[hh, cs, :] = v_news[hh]
        intra = [_bdot(qkm[hh][cs], vn_ref[hh]) for hh in heads]
        upd = [_bdot_tn(k_ends[hh][cs], v_news[hh]) for hh in heads]
        for hh in heads:
            outs[hh].append(ws_qs[hh][CHUNK:] + intra[hh])
            g_last = gls[hh][ci * CHUNK:ci * CHUNK + 1, :]
            state_ref[hh] = sts[hh] * jnp.exp(g_last) + upd[hh]
    for hh in heads:
        o = jnp.concatenate(outs[hh], axis=0)
        o = o * lax.rsqrt(jnp.mean(o * o, axis=-1, keepdims=True) + EPS) * gout_ref[...]
        zz = z_ref[:, hsl[hh]].astype(F32)
        o_ref[:, hsl[hh]] = (o * (zz * jax.nn.sigmoid(zz))).astype(o_ref.dtype)


def _gated_delta(big, small, small_t, conv_w, g_out, bsz, seq, n_heads, d_model):
    t = bsz * seq
    tb = GDN_TB
    ns = seq // tb
    width = GDN_HG * HEAD_DIM
    nhg = n_heads // GDN_HG
    blocks_per_group = d_model // width
    rows_t = small_t.shape[0]

    def colspec(group):
        return pl.BlockSpec((tb, width), lambda b, h, s: (b * ns + s, group * blocks_per_group + h))

    def cwspec(group):
        return pl.BlockSpec((CONV_WIDTH, width), lambda b, h, s: (0, group * blocks_per_group + h))

    return pl.pallas_call(
        functools.partial(_gdn_kernel, n_heads=n_heads),
        grid=(bsz, nhg, ns),
        in_specs=[
            colspec(0), colspec(1), colspec(2), colspec(3),
            pl.BlockSpec((tb, LANES), lambda b, h, s: (b * ns + s, 0)),
            pl.BlockSpec((rows_t, tb), lambda b, h, s: (0, b * ns + s)),
            cwspec(0), cwspec(1), cwspec(2),
            pl.BlockSpec((1, HEAD_DIM), lambda b, h, s: (0, 0)),
        ],
        out_specs=pl.BlockSpec((tb, width), lambda b, h, s: (b * ns + s, h)),
        out_shape=jax.ShapeDtypeStruct((t, d_model), BF16),
        scratch_shapes=[
            pltpu.VMEM((GDN_HG, HEAD_DIM, HEAD_DIM), F32),
            pltpu.VMEM((tb + SUBLANES, width), F32),
            pltpu.VMEM((tb + SUBLANES, width), F32),
            pltpu.VMEM((tb + SUBLANES, width), F32),
            pltpu.VMEM((GDN_HG, tb, HEAD_DIM), F32),
        ],
        compiler_params=_cparams(("parallel", "parallel", "arbitrary")),
        name="gated_delta",
    )(big, big, big, big, small, small_t, conv_w, conv_w, conv_w, g_out)


def _t5_bucket(n):
    max_exact = N_BUCKETS // 2
    nf = jnp.maximum(n, 1).astype(F32)
    large = max_exact + (jnp.log(nf / max_exact) / math.log(MAX_DISTANCE / max_exact)
                         * (N_BUCKETS - max_exact)).astype(jnp.int32)
    large = jnp.minimum(large, N_BUCKETS - 1)
    return jnp.where(n < max_exact, n, large)


def _attn_kernel(rb_ref, q_ref, k_ref, v_ref, lam_ref, gsub_ref, o_ref,
                 bias_ref, m_ref, acc_ref, sa_ref, sb_ref, *, lam_init):
    h = pl.program_id(0)
    b = pl.program_id(1)
    qi = pl.program_id(2)
    bq, bk = ATT_BQ, ATT_BK

    @pl.when(jnp.logical_and(b == 0, qi == 0))
    def _():
        i = lax.broadcasted_iota(jnp.int32, (bq, bk), 0)
        jj = lax.broadcasted_iota(jnp.int32, (bq, bk), 1)
        far = rb_ref[N_BUCKETS - 1, h]
        bias_ref[2] = jnp.zeros((bq, bk), F32)
        for slot in range(2):
            n = i - jj + slot * bk
            bucket = _t5_bucket(jnp.maximum(n, 0))
            bias = jnp.zeros((bq, bk), F32)
            for cc in range(N_BUCKETS):
                bias = jnp.where(bucket == cc, rb_ref[cc, h] - far, bias)
            if slot == 0:
                bias = jnp.where(n >= 0, bias, NEG_BIG)
            bias_ref[slot] = bias

    m_ref[...] = jnp.full(m_ref.shape, NEG_BIG, F32)
    acc_ref[...] = jnp.zeros(acc_ref.shape, F32)

    q = q_ref[...]
    lane = lax.broadcasted_iota(jnp.int32, q.shape, 1)
    zero = jnp.zeros_like(q)
    qs = jnp.concatenate([jnp.where(lane < DH_DIFF, q, zero), jnp.where(lane < DH_DIFF, zero, q)], axis=0)
    ones_col = (lax.broadcasted_iota(jnp.int32, (bk, HEAD_DIM), 1) == 0).astype(BF16)

    def scores(j):
        ks = pl.multiple_of(j * bk, bk)
        return lax.dot_general(qs, k_ref[pl.ds(ks, bk), :], (((1,), (1,)), ((), ())),
                               preferred_element_type=F32)

    def absorb(j, sc_ref, biased=True):
        ks = pl.multiple_of(j * bk, bk)
        v_ext = jnp.concatenate([v_ref[pl.ds(ks, bk), :], ones_col], axis=1)
        if biased:
            bias = bias_ref[jnp.minimum(qi - j, 2)]
            sc = jnp.concatenate([sc_ref[0:bq, :] + bias, sc_ref[bq:2 * bq, :] + bias], axis=0)
        else:
            sc = sc_ref[...]
        m_old = m_ref[...]
        m_new = jnp.maximum(m_old, jnp.max(sc, axis=-1, keepdims=True))
        p = jnp.exp(sc - m_new)
        acc_ref[...] = (jnp.exp(m_old - m_new) * acc_ref[...]
                        + jnp.dot(p.astype(BF16), v_ext, preferred_element_type=F32))
        m_ref[...] = m_new

    n_tiles = qi + 1
    sa_ref[...] = scores(0)

    def pair_body(jj, carry, biased):
        j0 = 2 * jj
        sb_ref[...] = scores(j0 + 1)
        absorb(j0, sa_ref, biased)
        sa_ref[...] = scores(jnp.minimum(j0 + 2, qi))
        absorb(j0 + 1, sb_ref, biased)
        return carry

    n_far_pairs = jnp.maximum(qi - 1, 0) // 2
    lax.fori_loop(0, n_far_pairs, functools.partial(pair_body, biased=False), 0)
    lax.fori_loop(n_far_pairs, n_tiles // 2, functools.partial(pair_body, biased=True), 0)

    @pl.when(n_tiles % 2 == 1)
    def _():
        absorb(qi, sa_ref)

    lam_p = lam_ref[...]
    s1 = jnp.sum(lam_p[0:1] * lam_p[1:2], axis=-1, keepdims=True)
    s2 = jnp.sum(lam_p[2:3] * lam_p[3:4], axis=-1, keepdims=True)
    lam = jnp.exp(s1) - jnp.exp(s2) + lam_init
    acc = acc_ref[...]
    num = acc[:, :HEAD_DIM]
    den = acc[:, HEAD_DIM:HEAD_DIM + 1]
    o = num[:bq] / den[:bq] - lam * (num[bq:] / den[bq:])
    o = o * lax.rsqrt(jnp.mean(o * o, axis=-1, keepdims=True) + EPS) * gsub_ref[...]
    o_ref[...] = (o * (1.0 - lam_init)).astype(o_ref.dtype)


def _diff_attention(proj_qk, proj_plain, rel_bias, lam_params, g_subln, bsz, seq, n_heads, d_model,
                    lam_init):
    t = bsz * seq
    nq = seq // ATT_BQ
    per = d_model // HEAD_DIM
    vcol = 4 * per
    return pl.pallas_call(
        functools.partial(_attn_kernel, lam_init=lam_init),
        grid=(n_heads, bsz, nq),
        in_specs=[
            pl.BlockSpec(memory_space=pltpu.SMEM),
            pl.BlockSpec((ATT_BQ, HEAD_DIM), lambda h, b, i: (b * nq + i, h)),
            pl.BlockSpec((seq, HEAD_DIM), lambda h, b, i: (b, per + h)),
            pl.BlockSpec((seq, HEAD_DIM), lambda h, b, i: (b, vcol + h)),
            pl.BlockSpec((4, DH_DIFF), lambda h, b, i: (0, 0)),
            pl.BlockSpec((1, HEAD_DIM), lambda h, b, i: (0, 0)),
        ],
        out_specs=pl.BlockSpec((ATT_BQ, HEAD_DIM), lambda h, b, i: (b * nq + i, h)),
        out_shape=jax.ShapeDtypeStruct((t, d_model), BF16),
        scratch_shapes=[
            pltpu.VMEM((3, ATT_BQ, ATT_BK), F32),
            pltpu.VMEM((2 * ATT_BQ, 1), F32),
            pltpu.VMEM((2 * ATT_BQ, 2 * HEAD_DIM), F32),
            pltpu.VMEM((2 * ATT_BQ, ATT_BK), F32),
            pltpu.VMEM((2 * ATT_BQ, ATT_BK), F32),
        ],
        compiler_params=_cparams(("arbitrary", "arbitrary", "arbitrary")),
        name="diff_attention",
    )(rel_bias, proj_qk, proj_qk, proj_plain, lam_params, g_subln)


def _mix_kernel(ga_ref, gb_ref, oa_ref, od_ref, x_ref, wo_ref, gffn_ref, wr_ref, br_ref,
                x1_ref, h2_ref, topi_ref, topw_ref, rank_ref, cnt_ref, carry_ref):
    i = pl.program_id(0)
    tm = MIX_TM

    @pl.when(i == 0)
    def _():
        carry_ref[...] = jnp.zeros_like(carry_ref)

    mix = (ga_ref[...].astype(F32) * oa_ref[...].astype(F32)
           + gb_ref[...].astype(F32) * od_ref[...].astype(F32))
    x1 = x_ref[...] + jnp.dot(mix.astype(BF16), wo_ref[...], preferred_element_type=F32)
    x1_ref[...] = x1
    h2 = x1 * lax.rsqrt(jnp.mean(x1 * x1, axis=-1, keepdims=True) + EPS) * gffn_ref[...]
    h2_ref[...] = _pack_halves(h2)

    logits = lax.dot_general(wr_ref[...], h2, (((1,), (1,)), ((), ())),
                             preferred_element_type=F32, precision=lax.Precision.HIGHEST) + br_ref[...]
    eidx = lax.broadcasted_iota(jnp.int32, logits.shape, 0).astype(F32)
    vals, hots = [], []
    cur = logits
    for kk in range(TOP_K):
        mx = jnp.max(cur, axis=0, keepdims=True)
        idx = jnp.min(jnp.where(cur == mx, eidx, float(N_EXPERTS)), axis=0, keepdims=True)
        hot = eidx == idx
        vals.append(mx)
        hots.append(hot)
        topi_ref[kk:kk + 1, :] = idx.astype(jnp.int32)
        cur = jnp.where(hot, -jnp.inf, cur)
    exps = [jnp.exp(vv - vals[0]) for vv in vals]
    denom = exps[0] + exps[1] + exps[2] + exps[3]
    for kk in range(TOP_K):
        topw_ref[kk:kk + 1, :] = exps[kk] / denom

    sel = hots[0]
    for kk in range(1, TOP_K):
        sel = jnp.logical_or(sel, hots[kk])
    sel_f = sel.astype(F32)
    r = lax.broadcasted_iota(jnp.int32, (tm, tm), 0)
    c = lax.broadcasted_iota(jnp.int32, (tm, tm), 1)
    before = _bdot(sel_f, (r < c).astype(F32)) + carry_ref[...]
    for kk in range(TOP_K):
        rank_ref[kk:kk + 1, :] = jnp.sum(jnp.where(hots[kk], before, 0.0), axis=0,
                                         keepdims=True).astype(jnp.int32)
    carry_ref[...] = carry_ref[...] + jnp.sum(sel_f, axis=-1, keepdims=True)
    cnt_ref[...] = carry_ref[...].astype(jnp.int32)


def _mix_project_route(proj_gate, oa, od, x2d, w_o, g_ffn, w_r_t, b_r, d_model):
    t = x2d.shape[0]
    tm = MIX_TM
    full = lambda shape: pl.BlockSpec(shape, lambda i: (0, 0))
    row = lambda: pl.BlockSpec((tm, d_model), lambda i: (i, 0))
    krow = lambda: pl.BlockSpec((TOP_K, tm), lambda i: (0, i))
    return pl.pallas_call(
        _mix_kernel,
        grid=(t // tm,),
        in_specs=[
            pl.BlockSpec((tm, d_model), lambda i: (i, 0)),
            pl.BlockSpec((tm, d_model), lambda i: (i, 1)),
            row(), row(), row(),
            full((d_model, d_model)), full((1, d_model)), full((N_EXPERTS, d_model)), full((N_EXPERTS, 1)),
        ],
        out_specs=[row(), pl.BlockSpec((tm, d_model // 2), lambda i: (i, 0)),
                   krow(), krow(), krow(), full((N_EXPERTS, 1))],
        out_shape=[
            jax.ShapeDtypeStruct((t, d_model), F32),
            jax.ShapeDtypeStruct((t, d_model // 2), jnp.int32),
            jax.ShapeDtypeStruct((TOP_K, t), jnp.int32),
            jax.ShapeDtypeStruct((TOP_K, t), F32),
            jax.ShapeDtypeStruct((TOP_K, t), jnp.int32),
            jax.ShapeDtypeStruct((N_EXPERTS, 1), jnp.int32),
        ],
        scratch_shapes=[pltpu.VMEM((N_EXPERTS, 1), F32)],
        compiler_params=_cparams(("arbitrary",)),
        name="merge_outproj_route",
    )(proj_gate, proj_gate, oa, od, x2d, w_o, g_ffn, w_r_t, b_r)


def _pack_halves(x):
    half = x.shape[1] // 2
    bits = pltpu.bitcast(x.astype(BF16).astype(F32), jnp.int32)
    return bits[:, :half] | lax.shift_right_logical(bits[:, half:], 16)


def _unpack_halves(p):
    hi = pltpu.bitcast(p & jnp.int32(-65536), F32)
    lo = pltpu.bitcast(lax.shift_left(p, 16), F32)
    return jnp.concatenate([hi, lo], axis=1)


def _expert_kernel(be_ref, nu_ref, x_ref, wup_ref, bup_ref, wdn_ref, bdn_ref, y_ref, wup_bf, wdn_bf):
    i = pl.program_id(0)
    d_ff = wdn_ref.shape[1]

    @pl.when(jnp.logical_or(i == 0, be_ref[i] != be_ref[jnp.maximum(i - 1, 0)]))
    def _():
        rr = lax.broadcasted_iota(jnp.int32, (2 * LANES, 2 * LANES), 0)
        cc = lax.broadcasted_iota(jnp.int32, (2 * LANES, 2 * LANES), 1)
        pick = jnp.where(cc < LANES, 2 * cc, 2 * (cc - LANES) + 1)
        perm = (rr == pick).astype(BF16)
        for g in range(wup_ref.shape[2] // (2 * LANES)):
            cs = slice(g * 2 * LANES, (g + 1) * 2 * LANES)
            wup_bf[:, cs] = jnp.dot(wup_ref[0, :, cs].astype(BF16), perm,
                                    preferred_element_type=F32).astype(BF16)
        wdn_bf[...] = wdn_ref[0].astype(BF16)

    @pl.when(i < nu_ref[0])
    def _():
        x = _unpack_halves(x_ref[...])
        hid = jnp.dot(x.astype(BF16), wup_bf[...], preferred_element_type=F32) + bup_ref[0]
        acts = []
        for g in range(hid.shape[1] // (2 * LANES)):
            glu = jnp.minimum(hid[:, g * 2 * LANES:g * 2 * LANES + LANES], SWIGLU_LIMIT)
            lin = jnp.clip(hid[:, g * 2 * LANES + LANES:(g + 1) * 2 * LANES], -SWIGLU_LIMIT, SWIGLU_LIMIT)
            acts.append(glu * jax.nn.sigmoid(SWIGLU_ALPHA * glu) * (lin + 1.0))
        act = jnp.concatenate(acts, axis=1)
        assert act.shape[1] == d_ff
        y = jnp.dot(act.astype(BF16), wdn_bf[...], preferred_element_type=F32) + bdn_ref[0]
        y_ref[...] = _pack_halves(y)

    @pl.when(i >= nu_ref[0])
    def _():
        y_ref[...] = jnp.zeros(y_ref.shape, y_ref.dtype)


def _experts(block_e, n_used, xs, w_up, b_up, w_down, b_down):
    n_rows, half = xs.shape
    d = w_up.shape[1]
    nb = n_rows // MOE_RB
    two_ff = w_up.shape[2]
    d_ff = w_down.shape[1]
    grid_spec = pltpu.PrefetchScalarGridSpec(
        num_scalar_prefetch=2,
        grid=(nb,),
        in_specs=[
            pl.BlockSpec((MOE_RB, half), lambda i, be, nu: (jnp.minimum(i, nu[0] - 1), 0)),
            pl.BlockSpec((1, d, two_ff), lambda i, be, nu: (be[i], 0, 0)),
            pl.BlockSpec((1, 1, two_ff), lambda i, be, nu: (be[i], 0, 0)),
            pl.BlockSpec((1, d_ff, d), lambda i, be, nu: (be[i], 0, 0)),
            pl.BlockSpec((1, 1, d), lambda i, be, nu: (be[i], 0, 0)),
        ],
        out_specs=pl.BlockSpec((MOE_RB, half), lambda i, be, nu: (i, 0)),
        scratch_shapes=[pltpu.VMEM((d, two_ff), BF16), pltpu.VMEM((d_ff, d), BF16)],
    )
    return pl.pallas_call(
        _expert_kernel,
        grid_spec=grid_spec,
        out_shape=jax.ShapeDtypeStruct((n_rows, half), jnp.int32),
        compiler_params=_cparams(("arbitrary",)),
        name="moe_experts",
    )(block_e, n_used, xs, w_up, b_up, w_down, b_down)


def _sc_invert_slots(dest_flat, n_rows):
    n_assign = dest_flat.shape[0]
    n_workers = SC_CORES * SC_SUBCORES
    rows_per_w = n_rows // n_workers
    chunk = SC_SCAN_CHUNK
    assert n_rows % n_workers == 0 and rows_per_w % SC_LANES == 0 and n_assign % chunk == 0
    mesh = plsc.VectorSubcoreMesh(core_axis_name="c", subcore_axis_name="s",
                                  num_cores=SC_CORES, num_subcores=SC_SUBCORES)

    def body(dest_hbm, out_hbm, dest_v, map_v):
        wid = lax.axis_index("s") * SC_CORES + lax.axis_index("c")
        base = wid * rows_per_w
        lanes = lax.broadcasted_iota(jnp.int32, (SC_LANES,), 0)

        @pl.loop(0, rows_per_w, step=SC_LANES)
        def _(r0):
            map_v[pl.ds(r0, SC_LANES)] = jnp.full((SC_LANES,), -1, jnp.int32)

        @pl.loop(0, n_assign // chunk)
        def _(ci):
            pltpu.sync_copy(dest_hbm.at[pl.ds(ci * chunk, chunk)], dest_v)

            @pl.loop(0, chunk, step=SC_LANES)
            def _(j):
                local = dest_v[pl.ds(j, SC_LANES)] - base
                mine = jnp.logical_and(local >= 0, local < rows_per_w)
                plsc.store_scatter(map_v, [jnp.where(mine, local, 0)], ci * chunk + j + lanes, mask=mine)

        pltpu.sync_copy(map_v, out_hbm.at[pl.ds(base, rows_per_w)])

    return pl.kernel(
        body,
        out_type=jax.ShapeDtypeStruct((n_rows,), jnp.int32),
        mesh=mesh,
        scratch_types=[pltpu.VMEM((chunk,), jnp.int32), pltpu.VMEM((rows_per_w,), jnp.int32)],
        compiler_params=pltpu.CompilerParams(needs_layout_passes=False),
        name="moe_slot_inverse",
    )(dest_flat)


def _sc_gather_rows(table, idx):
    n_idx = idx.shape[0]
    d = table.shape[1]
    n_workers = SC_CORES * SC_SUBCORES
    per_worker = n_idx // n_workers
    n_chunks = per_worker // SC_GATHER_ROWS
    assert n_idx % n_workers == 0 and per_worker % SC_GATHER_ROWS == 0
    mesh = plsc.VectorSubcoreMesh(core_axis_name="c", subcore_axis_name="s",
                                  num_cores=SC_CORES, num_subcores=SC_SUBCORES)

    assert n_chunks % 2 == 0

    def body(table_hbm, idx_hbm, out_hbm, idx_v, rows_a, rows_b, sem_a, sem_b):
        wid = lax.axis_index("s") * SC_CORES + lax.axis_index("c")
        base = wid * per_worker
        pltpu.sync_copy(idx_hbm.at[pl.ds(base, per_worker)], idx_v)

        def gather(ci, rows_v, sem):
            off = pl.multiple_of(ci * SC_GATHER_ROWS, SC_GATHER_ROWS)
            return pltpu.make_async_copy(table_hbm.at[idx_v.at[pl.ds(off, SC_GATHER_ROWS)]], rows_v, sem)

        def put(ci, rows_v):
            off = pl.multiple_of(ci * SC_GATHER_ROWS, SC_GATHER_ROWS)
            pltpu.sync_copy(rows_v, out_hbm.at[pl.ds(base + off, SC_GATHER_ROWS)])

        gather(0, rows_a, sem_a).start()

        @pl.loop(0, n_chunks, step=2)
        def _(ci):
            gather(ci + 1, rows_b, sem_b).start()
            gather(ci, rows_a, sem_a).wait()
            put(ci, rows_a)
            nxt = jnp.minimum(ci + 2, n_chunks - 1)
            gather(nxt, rows_a, sem_a).start()
            gather(ci + 1, rows_b, sem_b).wait()
            put(ci + 1, rows_b)

        gather(n_chunks - 1, rows_a, sem_a).wait()

    return pl.kernel(
        body,
        out_type=jax.ShapeDtypeStruct((n_idx, d), table.dtype),
        mesh=mesh,
        scratch_types=[
            pltpu.VMEM((per_worker,), jnp.int32),
            pltpu.VMEM((SC_GATHER_ROWS, d), table.dtype),
            pltpu.VMEM((SC_GATHER_ROWS, d), table.dtype),
            pltpu.SemaphoreType.DMA,
            pltpu.SemaphoreType.DMA,
        ],
        name="moe_slot_gather",
    )(table, idx)


def _combine_kernel(x1_ref, w_ref, y0_ref, y1_ref, y2_ref, y3_ref, o_ref):
    w = w_ref[...]
    out = x1_ref[...]
    for kk, y_ref in enumerate((y0_ref, y1_ref, y2_ref, y3_ref)):
        out = out + w[:, kk:kk + 1] * _unpack_halves(y_ref[...])
    o_ref[...] = out


def _combine(x1, w_tok, y_slots):
    t, d = x1.shape
    tc = COMB_TC
    nt = t // tc
    yspec = lambda kk: pl.BlockSpec((tc, d // 2), lambda i: (kk * nt + i, 0))
    return pl.pallas_call(
        _combine_kernel,
        grid=(nt,),
        in_specs=[
            pl.BlockSpec((tc, d), lambda i: (i, 0)),
            pl.BlockSpec((tc, TOP_K), lambda i: (i, 0)),
            yspec(0), yspec(1), yspec(2), yspec(3),
        ],
        out_specs=pl.BlockSpec((tc, d), lambda i: (i, 0)),
        out_shape=jax.ShapeDtypeStruct((t, d), F32),
        compiler_params=_cparams(("parallel",)),
        name="moe_combine",
    )(x1, w_tok, y_slots, y_slots, y_slots, y_slots)


def _moe(x1, h2, topi, topw, rank, counts, w_up, b_up, w_down, b_down):
    t, d = x1.shape
    n_assign = t * TOP_K
    nb = -(-n_assign // MOE_RB) + N_EXPERTS
    n_rows = nb * MOE_RB
    counts = counts[:, 0]
    padded = (counts + MOE_RB - 1) // MOE_RB * MOE_RB
    padded_end = jnp.cumsum(padded)
    padded_start = padded_end - padded
    expert_ids = jnp.arange(N_EXPERTS, dtype=jnp.int32)[:, None, None]
    start_of = jnp.sum(jnp.where(topi[None] == expert_ids, padded_start[:, None, None], 0), axis=0)
    dest = (start_of + rank).astype(jnp.int32)
    n_used = (padded_end[-1] // MOE_RB).astype(jnp.int32)
    blk = jnp.minimum(jnp.arange(nb, dtype=jnp.int32), n_used - 1)
    block_e = jnp.minimum(jnp.sum(padded_end[None, :] <= (blk * MOE_RB)[:, None], axis=1),
                          N_EXPERTS - 1).astype(jnp.int32)
    slot_of = _sc_invert_slots(dest.reshape(-1), n_rows)
    src_tok = jnp.where(slot_of < 0, jnp.arange(n_rows, dtype=jnp.int32), slot_of) % t

    xs = _sc_gather_rows(h2, src_tok)
    y_rows = _experts(block_e, n_used.reshape(1), xs, w_up, b_up, w_down, b_down)
    y_slots = _sc_gather_rows(y_rows, dest.reshape(-1))
    return _combine(x1, topw.T, y_slots)


def kernel(x, g_mix, w_in, b_gate, conv_w, a_log, dt_bias, g_delta_out, q_norm, k_norm, lambda_q1, lambda_k1, lambda_q2, lambda_k2, g_subln, rel_bias, w_o, g_ffn, w_router, b_router, w_up, b_up, w_down, b_down):
    bsz, seq, d = x.shape
    depth = g_mix.shape[0]
    n_heads = d // HEAD_DIM
    t = bsz * seq
    d_ff = w_down.shape[2]
    assert d % PROJ_TN == 0 and t % PROJ_TM == 0 and seq % GDN_TB == 0 and seq % ATT_BQ == 0
    assert t % MIX_TM == 0 and t % COMB_TC == 0 and n_heads % GDN_HG == 0
    assert (t * TOP_K) % MOE_RB == 0
    assert 2 * n_heads <= 2 * SUBLANES

    x2d = x.reshape(t, d)
    for l in range(depth):
        wl = w_in[l]
        c0 = 4 * d
        c1 = c0 + 2 * n_heads
        c2 = c1 + 2 * d
        c3 = c2 + d
        w_small = jnp.pad(wl[:, c0:c1], ((0, 0), (0, LANES - 2 * n_heads)))
        gm = g_mix[l].reshape(1, d)
        w_plain = jnp.concatenate([wl[:, :c0], wl[:, c2:c3]], axis=1).astype(BF16)
        proj_plain = _input_projection(x2d, gm, w_plain, jnp.zeros((1, 5 * d), F32), "plain")
        qk_gain = jnp.concatenate([jnp.tile(q_norm[l] * (DH_DIFF ** -0.5), 2 * n_heads),
                                   jnp.tile(k_norm[l], 2 * n_heads)]).reshape(1, 2 * d)
        proj_qk = _input_projection(x2d, gm, wl[:, c1:c2].astype(BF16), qk_gain, "qknorm")
        proj_gate = _input_projection(x2d, gm, wl[:, c3:].astype(BF16), b_gate[l].reshape(1, 2 * d), "gate")

        head_pad = jnp.zeros((LANES - 2 * n_heads,), F32)
        alog = jnp.concatenate([jnp.zeros((n_heads,), F32), a_log[l], head_pad])
        dtb = jnp.concatenate([jnp.zeros((n_heads,), F32), dt_bias[l], head_pad])
        rows_t = 2 * n_heads
        small, small_t = _small_projection(
            x2d, g_mix[l].reshape(1, d), w_small.astype(BF16), w_small[:, :rows_t].T.astype(BF16),
            alog.reshape(1, LANES), dtb.reshape(1, LANES),
            alog[:rows_t].reshape(rows_t, 1), dtb[:rows_t].reshape(rows_t, 1), n_heads)

        oa = _gated_delta(proj_plain, small, small_t, conv_w[l], g_delta_out[l].reshape(1, HEAD_DIM),
                          bsz, seq, n_heads, d)

        lam_init = 0.8 - 0.6 * math.exp(-0.3 * l)
        lam_params = jnp.stack([lambda_q1[l], lambda_k1[l], lambda_q2[l], lambda_k2[l]])
        od = _diff_attention(proj_qk, proj_plain, rel_bias, lam_params, g_subln[l].reshape(1, HEAD_DIM),
                             bsz, seq, n_heads, d, lam_init)

        x1, h2, topi, topw, rank, counts = _mix_project_route(
            proj_gate, oa, od, x2d, w_o[l].astype(BF16), g_ffn[l].reshape(1, d),
            w_router[l].T, b_router[l].reshape(N_EXPERTS, 1), d)

        b_up_l = b_up[l].reshape(N_EXPERTS, 2 * d_ff // (2 * LANES), LANES, 2)
        b_up_l = jnp.swapaxes(b_up_l, 2, 3).reshape(N_EXPERTS, 1, 2 * d_ff)
        x2d = _moe(x1, h2, topi, topw, rank, counts, w_up[l], b_up_l,
                   w_down[l], b_down[l].reshape(N_EXPERTS, 1, d))
    return x2d.reshape(bsz, seq, d)
```

```python
import functools
import math

import jax
import jax.numpy as jnp
from jax import lax
from jax.experimental import pallas as pl
from jax.experimental.pallas import tpu as pltpu
from jax.experimental.pallas import tpu_sc as plsc

F32 = jnp.float32
BF16 = jnp.bfloat16

HEAD_DIM = 128
DH_DIFF = HEAD_DIM // 2
CONV_WIDTH = 4
CHUNK = 64
N_BUCKETS = 32
MAX_DISTANCE = 128
N_EXPERTS = 32
TOP_K = 4
TOP_K_SHIFT = 2
SWIGLU_LIMIT = 7.0
SWIGLU_ALPHA = 1.702
EPS = 1e-6
NEG_BIG = -1e30

LANES = 128
SUBLANES = 8
VMEM_LIMIT = 56 * 1024 * 1024
SC_CORES = 2
SC_SUBCORES = 16
SC_LANES = 16
SC_GATHER_ROWS = 64
SC_SCAN_CHUNK = 4096

PROJ_TM = 2048
PROJ_TN = 1024
PROJ_CHUNK = 256
GDN_TB = 256
GDN_HG = 4
ATT_BQ = 512
ATT_BK = 512
MIX_TM = 512
MOE_RB = 256
COMB_TC = 512


def _cparams(sem):
    return pltpu.CompilerParams(dimension_semantics=sem, vmem_limit_bytes=VMEM_LIMIT)


def _bdot(a, b):
    return jnp.dot(a.astype(BF16), b.astype(BF16), preferred_element_type=F32)


def _bdot_nt(a, b):
    return lax.dot_general(a.astype(BF16), b.astype(BF16), (((1,), (1,)), ((), ())),
                           preferred_element_type=F32)


def _bdot_tn(a, b):
    return lax.dot_general(a.astype(BF16), b.astype(BF16), (((0,), (0,)), ((), ())),
                           preferred_element_type=F32)


def _proj_kernel(x_ref, g_ref, w_ref, aux_ref, o_ref, h_ref, *, mode):
    @pl.when(pl.program_id(1) == 0)
    def _():
        x = x_ref[...]
        ms = jnp.mean(x * x, axis=-1, keepdims=True)
        h_ref[...] = (x * lax.rsqrt(ms + EPS) * g_ref[...]).astype(BF16)

    h = h_ref[...]
    lo = lax.broadcasted_iota(jnp.int32, (1, LANES), 1) < DH_DIFF
    for c in range(PROJ_TN // PROJ_CHUNK):
        cs = slice(c * PROJ_CHUNK, (c + 1) * PROJ_CHUNK)
        acc = jnp.dot(h, w_ref[:, cs], preferred_element_type=F32)
        if mode == "plain":
            o_ref[:, cs] = acc.astype(o_ref.dtype)
        elif mode == "gate":
            o_ref[:, cs] = jax.nn.sigmoid(acc + aux_ref[:, cs]).astype(o_ref.dtype)
        else:
            for g in range(PROJ_CHUNK // LANES):
                sl = slice(c * PROJ_CHUNK + g * LANES, c * PROJ_CHUNK + (g + 1) * LANES)
                y = acc[:, g * LANES:(g + 1) * LANES]
                y2 = y * y
                s_lo = jnp.sum(jnp.where(lo, y2, 0.0), axis=-1, keepdims=True)
                s_hi = jnp.sum(jnp.where(lo, 0.0, y2), axis=-1, keepdims=True)
                r = jnp.where(lo, lax.rsqrt(s_lo / DH_DIFF + EPS), lax.rsqrt(s_hi / DH_DIFF + EPS))
                o_ref[:, sl] = (y * r * aux_ref[:, sl]).astype(o_ref.dtype)


def _input_projection(x2d, g_mix, w, aux, mode):
    t, d = x2d.shape
    n = w.shape[1]
    return pl.pallas_call(
        functools.partial(_proj_kernel, mode=mode),
        grid=(t // PROJ_TM, n // PROJ_TN),
        in_specs=[
            pl.BlockSpec((PROJ_TM, d), lambda i, j: (i, 0)),
            pl.BlockSpec((1, d), lambda i, j: (0, 0)),
            pl.BlockSpec((d, PROJ_TN), lambda i, j: (0, j)),
            pl.BlockSpec((1, PROJ_TN), lambda i, j: (0, j)),
        ],
        out_specs=pl.BlockSpec((PROJ_TM, PROJ_TN), lambda i, j: (i, j)),
        out_shape=jax.ShapeDtypeStruct((t, n), BF16),
        scratch_shapes=[pltpu.VMEM((PROJ_TM, d), BF16)],
        compiler_params=_cparams(("parallel", "arbitrary")),
        name="input_projection_" + mode,
    )(x2d, g_mix, w, aux)


def _small_proj_kernel(x_ref, g_ref, w_ref, wt_ref, alog_ref, dtb_ref, alog_t_ref, dtb_t_ref,
                       o_ref, ot_ref, *, n_heads):
    x = x_ref[...]
    ms = jnp.mean(x * x, axis=-1, keepdims=True)
    h = (x * lax.rsqrt(ms + EPS) * g_ref[...]).astype(BF16)

    def finish(acc, idx, alog, dtb):
        beta = jax.nn.sigmoid(acc)
        z = acc + dtb
        softplus = jnp.maximum(z, 0.0) + jnp.log1p(jnp.exp(-jnp.abs(z)))
        gdec = -jnp.exp(alog) * softplus
        return jnp.where(idx < n_heads, beta, jnp.where(idx < 2 * n_heads, gdec, 0.0))

    acc = jnp.dot(h, w_ref[...], preferred_element_type=F32)
    lane = lax.broadcasted_iota(jnp.int32, acc.shape, 1)
    o_ref[...] = finish(acc, lane, alog_ref[...], dtb_ref[...])
    acc_t = lax.dot_general(wt_ref[...], h, (((1,), (1,)), ((), ())),
                            preferred_element_type=F32)
    sub = lax.broadcasted_iota(jnp.int32, acc_t.shape, 0)
    ot_ref[...] = finish(acc_t, sub, alog_t_ref[...], dtb_t_ref[...])


def _small_projection(x2d, g_mix, w_small, w_small_t, alog, dtb, alog_t, dtb_t, n_heads):
    t, d = x2d.shape
    rows_t = w_small_t.shape[0]
    tm = PROJ_TM
    full = lambda shape: pl.BlockSpec(shape, lambda i: (0, 0))
    return pl.pallas_call(
        functools.partial(_small_proj_kernel, n_heads=n_heads),
        grid=(t // tm,),
        in_specs=[
            pl.BlockSpec((tm, d), lambda i: (i, 0)),
            full((1, d)), full((d, LANES)), full((rows_t, d)),
            full((1, LANES)), full((1, LANES)), full((rows_t, 1)), full((rows_t, 1)),
        ],
        out_specs=[pl.BlockSpec((tm, LANES), lambda i: (i, 0)),
                   pl.BlockSpec((rows_t, tm), lambda i: (0, i))],
        out_shape=[jax.ShapeDtypeStruct((t, LANES), F32),
                   jax.ShapeDtypeStruct((rows_t, t), F32)],
        compiler_params=_cparams(("parallel",)),
        name="beta_decay_projection",
    )(x2d, g_mix, w_small, w_small_t, alog, dtb, alog_t, dtb_t)


def _gdn_kernel(q_ref, k_ref, v_ref, z_ref, sm_ref, smt_ref, cwq_ref, cwk_ref, cwv_ref, gout_ref,
                o_ref, state_ref, qp_ref, kp_ref, vp_ref, vn_ref, *, n_heads):
    hg = pl.program_id(1)
    s = pl.program_id(2)
    tb = GDN_TB
    pad = SUBLANES
    width = GDN_HG * HEAD_DIM

    @pl.when(s == 0)
    def _():
        state_ref[...] = jnp.zeros_like(state_ref)
        for p_ref in (qp_ref, kp_ref, vp_ref):
            p_ref[0:pad, :] = jnp.zeros((pad, width), F32)

    def conv_silu(x_ref, p_ref, cw_ref):
        p_ref[pad:pad + tb, :] = x_ref[...].astype(F32)
        acc = cw_ref[CONV_WIDTH - 1:CONV_WIDTH, :] * p_ref[pad:pad + tb, :]
        for jj in range(CONV_WIDTH - 1):
            off = pad - (CONV_WIDTH - 1) + jj
            acc = acc + cw_ref[jj:jj + 1, :] * p_ref[off:off + tb, :]
        p_ref[0:pad, :] = p_ref[tb:tb + pad, :]
        return acc * jax.nn.sigmoid(acc)

    q_all = conv_silu(q_ref, qp_ref, cwq_ref)
    k_all = conv_silu(k_ref, kp_ref, cwk_ref)
    v_all = conv_silu(v_ref, vp_ref, cwv_ref)

    r = lax.broadcasted_iota(jnp.int32, (tb, tb), 0)
    c = lax.broadcasted_iota(jnp.int32, (tb, tb), 1)
    shift = int(math.log2(CHUNK))
    same = (r >> shift) == (c >> shift)
    incl = jnp.logical_and(same, c <= r)
    strict = jnp.logical_and(same, c < r)

    small = sm_ref[...]
    small_t = smt_ref[...]
    lane = lax.broadcasted_iota(jnp.int32, small.shape, 1)
    def split3(a):
        hi = a.astype(BF16)
        r1 = a - hi.astype(F32)
        mid = r1.astype(BF16)
        lo = (r1 - mid.astype(F32)).astype(BF16)
        return hi.astype(F32), mid.astype(F32), lo.astype(F32)

    part = 2 * n_heads
    s_hi, s_mid, s_lo = split3(small)
    small3 = jnp.where(lane < part, s_hi,
                       jnp.where(lane < 2 * part, pltpu.roll(s_mid, part, 1),
                                 jnp.where(lane < 3 * part, pltpu.roll(s_lo, 2 * part, 1), 0.0)))
    both = _bdot(jnp.concatenate([incl.astype(F32), same.astype(F32)], axis=0), small3)
    gcum = both[:tb]
    gtot = both[tb:]
    gcum_t = _bdot(jnp.concatenate(split3(small_t), axis=0),
                   jnp.logical_and(same, r <= c).astype(F32))
    sub3 = lax.broadcasted_iota(jnp.int32, gcum_t.shape, 0)

    heads = range(GDN_HG)
    hsl = [slice(hh * HEAD_DIM, (hh + 1) * HEAD_DIM) for hh in heads]
    qs = [q_all[:, hs] for hs in hsl]
    ks = [k_all[:, hs] for hs in hsl]
    vs = [v_all[:, hs] for hs in hsl]
    qs = [q * lax.rsqrt(jnp.sum(q * q, axis=-1, keepdims=True) + EPS) * (HEAD_DIM ** -0.5) for q in qs]
    ks = [k * lax.rsqrt(jnp.sum(k * k, axis=-1, keepdims=True) + EPS) for k in ks]

    def col_of(arr, idx):
        return jnp.sum(jnp.where(lane == idx, arr, 0.0), axis=-1, keepdims=True)

    def terms_of(pos, idx):
        return jnp.logical_or(pos == idx, jnp.logical_or(pos == idx + part, pos == idx + 2 * part))

    head_ids = [hg * GDN_HG + hh for hh in heads]
    betas = [col_of(small, hd) for hd in head_ids]
    gcs = [jnp.sum(jnp.where(terms_of(lane, hd + n_heads), gcum, 0.0), axis=-1, keepdims=True)
           for hd in head_ids]
    gls = [jnp.sum(jnp.where(terms_of(lane, hd + n_heads), gtot, 0.0), axis=-1, keepdims=True)
           for hd in head_ids]
    gc_rows = [jnp.sum(jnp.where(terms_of(sub3, hd + n_heads), gcum_t, 0.0), axis=0, keepdims=True)
               for hd in head_ids]

    decays = [jnp.where(incl, jnp.exp(jnp.minimum(gc - gr, 0.0)), 0.0) for gc, gr in zip(gcs, gc_rows)]
    kbs = [k * b for k, b in zip(ks, betas)]
    kks = [_bdot_nt(kb, k) for kb, k in zip(kbs, ks)]
    pws = [jnp.where(strict, -(kk * dc), 0.0) for kk, dc in zip(kks, decays)]
    n_chunks = tb // CHUNK
    cat_row = lax.broadcasted_iota(jnp.int32, (CHUNK, tb), 0)
    cat_lane = lax.broadcasted_iota(jnp.int32, (CHUNK, tb), 1)
    lane_chunk = cat_lane >> shift

    def block_diag(m_cat):
        return jnp.concatenate([jnp.where(lane_chunk == ci, m_cat, 0.0) for ci in range(n_chunks)], axis=0)

    def cat_of(m_bd):
        out = m_bd[0:CHUNK]
        for ci in range(1, n_chunks):
            out = out + m_bd[ci * CHUNK:(ci + 1) * CHUNK]
        return out

    pcats = [cat_of(pw) for pw in pws]
    eye_cat = ((cat_lane & (CHUNK - 1)) == cat_row).astype(F32)
    tcats = [eye_cat + pc for pc in pcats]
    pcats = [_bdot(pc, block_diag(pc)) for pc in pcats]
    n_levels = int(math.log2(CHUNK))
    for lev in range(1, n_levels):
        bds = [block_diag(pc) for pc in pcats]
        if lev < n_levels - 1:
            prods = [_bdot(jnp.concatenate([pc, tc], axis=0), bd) for pc, tc, bd in zip(pcats, tcats, bds)]
            pcats = [pr[:CHUNK] for pr in prods]
            tcats = [tc + pr[CHUNK:] for tc, pr in zip(tcats, prods)]
        else:
            tcats = [tc + _bdot(tc, bd) for tc, bd in zip(tcats, bds)]
    tmats = [block_diag(tc) for tc in tcats]
    egcs = [jnp.exp(gc) for gc in gcs]
    uws = [_bdot(tm, jnp.concatenate([v * b, kb * eg], axis=1))
           for tm, v, b, kb, eg in zip(tmats, vs, betas, kbs, egcs)]
    us = [uw[:, :HEAD_DIM] for uw in uws]
    ws = [uw[:, HEAD_DIM:] for uw in uws]
    qkm = [_bdot_nt(q, k) for q, k in zip(qs, ks)]
    qkm = [jnp.where(incl, x * dc, 0.0) for x, dc in zip(qkm, decays)]
    q_decs = [q * eg for q, eg in zip(qs, egcs)]
    k_ends = [k * jnp.exp(gl - gc) for k, gl, gc in zip(ks, gls, gcs)]

    for hh in heads:
        vn_ref[hh] = jnp.zeros((tb, HEAD_DIM), F32)
    outs = [[] for _ in heads]
    for ci in range(tb // CHUNK):
        cs = slice(ci * CHUNK, (ci + 1) * CHUNK)
        sts = [state_ref[hh] for hh in heads]
        ws_qs = [_bdot(jnp.concatenate([ws[hh][cs], q_decs[hh][cs]], axis=0), sts[hh]) for hh in heads]
        v_news = [us[hh][cs] - ws_qs[hh][:CHUNK] for hh in heads]
        for hh in heads:
            vn_ref[hh, cs, :] = v_news[hh]
        intra = [_bdot(qkm[hh][cs], vn_ref[hh]) for hh in heads]
        upd = [_bdot_tn(k_ends[hh][cs], v_news[hh]) for hh in heads]
        for hh in heads:
            outs[hh].append(ws_qs[hh][CHUNK:] + intra[hh])
            g_last = gls[hh][ci * CHUNK:ci * CHUNK + 1, :]
            state_ref[hh] = sts[hh] * jnp.exp(g_last) + upd[hh]
    for hh in heads:
        o = jnp.concatenate(outs[hh], axis=0)
        o = o * lax.rsqrt(jnp.mean(o * o, axis=-1, keepdims=True) + EPS) * gout_ref[...]
        zz = z_ref[:, hsl[hh]].astype(F32)
        o_ref[:, hsl[hh]] = (o * (zz * jax.nn.sigmoid(zz))).astype(o_ref.dtype)


def _gated_delta(big, small, small_t, conv_w, g_out, bsz, seq, n_heads, d_model):
    t = bsz * seq
    tb = GDN_TB
    ns = seq // tb
    width = GDN_HG * HEAD_DIM
    nhg = n_heads // GDN_HG
    blocks_per_group = d_model // width
    rows_t = small_t.shape[0]

    def colspec(group):
        return pl.BlockSpec((tb, width), lambda b, h, s: (b * ns + s, group * blocks_per_group + h))

    def cwspec(group):
        return pl.BlockSpec((CONV_WIDTH, width), lambda b, h, s: (0, group * blocks_per_group + h))

    return pl.pallas_call(
        functools.partial(_gdn_kernel, n_heads=n_heads),
        grid=(bsz, nhg, ns),
        in_specs=[
            colspec(0), colspec(1), colspec(2), colspec(3),
            pl.BlockSpec((tb, LANES), lambda b, h, s: (b * ns + s, 0)),
            pl.BlockSpec((rows_t, tb), lambda b, h, s: (0, b * ns + s)),
            cwspec(0), cwspec(1), cwspec(2),
            pl.BlockSpec((1, HEAD_DIM), lambda b, h, s: (0, 0)),
        ],
        out_specs=pl.BlockSpec((tb, width), lambda b, h, s: (b * ns + s, h)),
        out_shape=jax.ShapeDtypeStruct((t, d_model), BF16),
        scratch_shapes=[
            pltpu.VMEM((GDN_HG, HEAD_DIM, HEAD_DIM), F32),
            pltpu.VMEM((tb + SUBLANES, width), F32),
            pltpu.VMEM((tb + SUBLANES, width), F32),
            pltpu.VMEM((tb + SUBLANES, width), F32),
            pltpu.VMEM((GDN_HG, tb, HEAD_DIM), F32),
        ],
        compiler_params=_cparams(("parallel", "parallel", "arbitrary")),
        name="gated_delta",
    )(big, big, big, big, small, small_t, conv_w, conv_w, conv_w, g_out)


def _t5_bucket(n):
    max_exact = N_BUCKETS // 2
    nf = jnp.maximum(n, 1).astype(F32)
    large = max_exact + (jnp.log(nf / max_exact) / math.log(MAX_DISTANCE / max_exact)
                         * (N_BUCKETS - max_exact)).astype(jnp.int32)
    large = jnp.minimum(large, N_BUCKETS - 1)
    return jnp.where(n < max_exact, n, large)


def _attn_kernel(tq_ref, tj_ref, rb_ref, q_ref, k_ref, v_ref, lam_ref, gsub_ref, o_ref,
                 bias_ref, m_ref, acc_ref, sa_ref, sb_ref, qs_ref, *, lam_init, n_tiles):
    h = pl.program_id(0)
    b = pl.program_id(1)
    bq, bk = ATT_BQ, ATT_BK

    @pl.when(b == 0)
    def _():
        i = lax.broadcasted_iota(jnp.int32, (bq, bk), 0)
        jj = lax.broadcasted_iota(jnp.int32, (bq, bk), 1)
        far = rb_ref[N_BUCKETS - 1, h]
        bias_ref[2] = jnp.zeros((bq, bk), F32)
        for slot in range(2):
            n = i - jj + slot * bk
            bucket = _t5_bucket(jnp.maximum(n, 0))
            bias = jnp.zeros((bq, bk), F32)
            for cc in range(N_BUCKETS):
                bias = jnp.where(bucket == cc, rb_ref[cc, h] - far, bias)
            if slot == 0:
                bias = jnp.where(n >= 0, bias, NEG_BIG)
            bias_ref[slot] = bias

    m_ref[...] = jnp.full(m_ref.shape, NEG_BIG, F32)
    acc_ref[...] = jnp.zeros(acc_ref.shape, F32)

    lane = lax.broadcasted_iota(jnp.int32, (bq, HEAD_DIM), 1)
    ones_col = (lax.broadcasted_iota(jnp.int32, (bk, HEAD_DIM), 1) == 0).astype(BF16)
    lam_p = lam_ref[...]
    s1 = jnp.sum(lam_p[0:1] * lam_p[1:2], axis=-1, keepdims=True)
    s2 = jnp.sum(lam_p[2:3] * lam_p[3:4], axis=-1, keepdims=True)
    lam = jnp.exp(s1) - jnp.exp(s2) + lam_init

    for qb in range(qs_ref.shape[0]):
        q = q_ref[qb * bq:(qb + 1) * bq, :]
        zero = jnp.zeros_like(q)
        qs_ref[qb, 0:bq, :] = jnp.where(lane < DH_DIFF, q, zero)
        qs_ref[qb, bq:2 * bq, :] = jnp.where(lane < DH_DIFF, zero, q)

    def scores(t):
        k = k_ref[pl.ds(pl.multiple_of(tj_ref[t] * bk, bk), bk), :]
        return lax.dot_general(qs_ref[tq_ref[t]], k, (((1,), (1,)), ((), ())),
                               preferred_element_type=F32)

    def absorb(t, sc_ref):
        qi = tq_ref[t]
        j = tj_ref[t]
        v_ext = jnp.concatenate([v_ref[pl.ds(pl.multiple_of(j * bk, bk), bk), :], ones_col], axis=1)
        bias = bias_ref[jnp.minimum(qi - j, 2)]
        sc = jnp.concatenate([sc_ref[0:bq, :] + bias, sc_ref[bq:2 * bq, :] + bias], axis=0)
        m_old = jnp.where(j == 0, NEG_BIG, m_ref[...])
        m_new = jnp.maximum(m_old, jnp.max(sc, axis=-1, keepdims=True))
        p = jnp.exp(sc - m_new)
        acc_ref[...] = (jnp.exp(m_old - m_new) * acc_ref[...]
                        + jnp.dot(p.astype(BF16), v_ext, preferred_element_type=F32))
        m_ref[...] = m_new

        @pl.when(j == qi)
        def _():
            acc = acc_ref[...]
            num = acc[:, :HEAD_DIM]
            den = acc[:, HEAD_DIM:HEAD_DIM + 1]
            o = num[:bq] / den[:bq] - lam * (num[bq:] / den[bq:])
            o = o * lax.rsqrt(jnp.mean(o * o, axis=-1, keepdims=True) + EPS) * gsub_ref[...]
            o_ref[pl.ds(pl.multiple_of(qi * bq, bq), bq), :] = (o * (1.0 - lam_init)).astype(o_ref.dtype)

    sa_ref[...] = scores(0)

    def pair_body(pp, carry):
        t0 = 2 * pp
        sb_ref[...] = scores(t0 + 1)
        absorb(t0, sa_ref)
        sa_ref[...] = scores(jnp.minimum(t0 + 2, n_tiles - 1))
        absorb(t0 + 1, sb_ref)
        return carry

    lax.fori_loop(0, n_tiles // 2, pair_body, 0)


def _diff_attention(proj_qk, proj_plain, rel_bias, lam_params, g_subln, bsz, seq, n_heads, d_model,
                    lam_init):
    t = bsz * seq
    nq = seq // ATT_BQ
    per = d_model // HEAD_DIM
    vcol = 4 * per
    tiles = [(qi, j) for qi in range(nq) for j in range(qi + 1)]
    assert len(tiles) % 2 == 0
    tile_q = jnp.array([qi for qi, _ in tiles], jnp.int32)
    tile_j = jnp.array([j for _, j in tiles], jnp.int32)
    grid_spec = pltpu.PrefetchScalarGridSpec(
        num_scalar_prefetch=2,
        grid=(n_heads, bsz),
        in_specs=[
            pl.BlockSpec(memory_space=pltpu.SMEM),
            pl.BlockSpec((seq, HEAD_DIM), lambda h, b, tq, tj: (b, h)),
            pl.BlockSpec((seq, HEAD_DIM), lambda h, b, tq, tj: (b, per + h)),
            pl.BlockSpec((seq, HEAD_DIM), lambda h, b, tq, tj: (b, vcol + h)),
            pl.BlockSpec((4, DH_DIFF), lambda h, b, tq, tj: (0, 0)),
            pl.BlockSpec((1, HEAD_DIM), lambda h, b, tq, tj: (0, 0)),
        ],
        out_specs=pl.BlockSpec((seq, HEAD_DIM), lambda h, b, tq, tj: (b, h)),
        scratch_shapes=[
            pltpu.VMEM((3, ATT_BQ, ATT_BK), F32),
            pltpu.VMEM((2 * ATT_BQ, 1), F32),
            pltpu.VMEM((2 * ATT_BQ, 2 * HEAD_DIM), F32),
            pltpu.VMEM((2 * ATT_BQ, ATT_BK), F32),
            pltpu.VMEM((2 * ATT_BQ, ATT_BK), F32),
            pltpu.VMEM((nq, 2 * ATT_BQ, HEAD_DIM), BF16),
        ],
    )
    return pl.pallas_call(
        functools.partial(_attn_kernel, lam_init=lam_init, n_tiles=len(tiles)),
        grid_spec=grid_spec,
        out_shape=jax.ShapeDtypeStruct((t, d_model), BF16),
        compiler_params=_cparams(("arbitrary", "arbitrary")),
        name="diff_attention",
    )(tile_q, tile_j, rel_bias, proj_qk, proj_qk, proj_plain, lam_params, g_subln)


def _mix_kernel(ga_ref, gb_ref, oa_ref, od_ref, x_ref, wo_ref, gffn_ref, wr_ref, br_ref,
                x1_ref, h2_ref, topi_ref, topw_ref, rank_ref, cnt_ref, carry_ref):
    i = pl.program_id(0)
    tm = MIX_TM

    @pl.when(i == 0)
    def _():
        carry_ref[...] = jnp.zeros_like(carry_ref)

    mix = (ga_ref[...].astype(F32) * oa_ref[...].astype(F32)
           + gb_ref[...].astype(F32) * od_ref[...].astype(F32))
    x1 = x_ref[...] + jnp.dot(mix.astype(BF16), wo_ref[...], preferred_element_type=F32)
    x1_ref[...] = x1
    h2 = x1 * lax.rsqrt(jnp.mean(x1 * x1, axis=-1, keepdims=True) + EPS) * gffn_ref[...]
    h2_ref[...] = _pack_halves(h2)

    logits = lax.dot_general(wr_ref[...], h2, (((1,), (1,)), ((), ())),
                             preferred_element_type=F32, precision=lax.Precision.HIGHEST) + br_ref[...]
    eidx = lax.broadcasted_iota(jnp.int32, logits.shape, 0).astype(F32)
    vals, hots = [], []
    cur = logits
    for kk in range(TOP_K):
        mx = jnp.max(cur, axis=0, keepdims=True)
        idx = jnp.min(jnp.where(cur == mx, eidx, float(N_EXPERTS)), axis=0, keepdims=True)
        hot = eidx == idx
        vals.append(mx)
        hots.append(hot)
        topi_ref[kk:kk + 1, :] = idx.astype(jnp.int32)
        cur = jnp.where(hot, -jnp.inf, cur)
    exps = [jnp.exp(vv - vals[0]) for vv in vals]
    denom = exps[0] + exps[1] + exps[2] + exps[3]
    for kk in range(TOP_K):
        topw_ref[kk:kk + 1, :] = exps[kk] / denom

    sel = hots[0]
    for kk in range(1, TOP_K):
        sel = jnp.logical_or(sel, hots[kk])
    sel_f = sel.astype(F32)
    r = lax.broadcasted_iota(jnp.int32, (tm, tm), 0)
    c = lax.broadcasted_iota(jnp.int32, (tm, tm), 1)
    before = _bdot(sel_f, (r < c).astype(F32)) + carry_ref[...]
    for kk in range(TOP_K):
        rank_ref[kk:kk + 1, :] = jnp.sum(jnp.where(hots[kk], before, 0.0), axis=0,
                                         keepdims=True).astype(jnp.int32)
    carry_ref[...] = carry_ref[...] + jnp.sum(sel_f, axis=-1, keepdims=True)
    cnt_ref[...] = carry_ref[...].astype(jnp.int32)


def _mix_project_route(proj_gate, oa, od, x2d, w_o, g_ffn, w_r_t, b_r, d_model):
    t = x2d.shape[0]
    tm = MIX_TM
    full = lambda shape: pl.BlockSpec(shape, lambda i: (0, 0))
    row = lambda: pl.BlockSpec((tm, d_model), lambda i: (i, 0))
    krow = lambda: pl.BlockSpec((TOP_K, tm), lambda i: (0, i))
    return pl.pallas_call(
        _mix_kernel,
        grid=(t // tm,),
        in_specs=[
            pl.BlockSpec((tm, d_model), lambda i: (i, 0)),
            pl.BlockSpec((tm, d_model), lambda i: (i, 1)),
            row(), row(), row(),
            full((d_model, d_model)), full((1, d_model)), full((N_EXPERTS, d_model)), full((N_EXPERTS, 1)),
        ],
        out_specs=[row(), pl.BlockSpec((tm, d_model // 2), lambda i: (i, 0)),
                   krow(), krow(), krow(), full((N_EXPERTS, 1))],
        out_shape=[
            jax.ShapeDtypeStruct((t, d_model), F32),
            jax.ShapeDtypeStruct((t, d_model // 2), jnp.int32),
            jax.ShapeDtypeStruct((TOP_K, t), jnp.int32),
            jax.ShapeDtypeStruct((TOP_K, t), F32),
            jax.ShapeDtypeStruct((TOP_K, t), jnp.int32),
            jax.ShapeDtypeStruct((N_EXPERTS, 1), jnp.int32),
        ],
        scratch_shapes=[pltpu.VMEM((N_EXPERTS, 1), F32)],
        compiler_params=_cparams(("arbitrary",)),
        name="merge_outproj_route",
    )(proj_gate, proj_gate, oa, od, x2d, w_o, g_ffn, w_r_t, b_r)


def _pack_halves(x):
    half = x.shape[1] // 2
    bits = pltpu.bitcast(x.astype(BF16).astype(F32), jnp.int32)
    return bits[:, :half] | lax.shift_right_logical(bits[:, half:], 16)


def _unpack_halves(p):
    hi = pltpu.bitcast(p & jnp.int32(-65536), F32)
    lo = pltpu.bitcast(lax.shift_left(p, 16), F32)
    return jnp.concatenate([hi, lo], axis=1)


def _expert_kernel(be_ref, nu_ref, x_ref, wup_ref, bup_ref, wdn_ref, bdn_ref, y_ref, wup_bf, wdn_bf):
    i = pl.program_id(0)
    d_ff = wdn_ref.shape[1]

    @pl.when(jnp.logical_or(i == 0, be_ref[i] != be_ref[jnp.maximum(i - 1, 0)]))
    def _():
        rr = lax.broadcasted_iota(jnp.int32, (2 * LANES, 2 * LANES), 0)
        cc = lax.broadcasted_iota(jnp.int32, (2 * LANES, 2 * LANES), 1)
        pick = jnp.where(cc < LANES, 2 * cc, 2 * (cc - LANES) + 1)
        perm = (rr == pick).astype(BF16)
        for g in range(wup_ref.shape[2] // (2 * LANES)):
            cs = slice(g * 2 * LANES, (g + 1) * 2 * LANES)
            wup_bf[:, cs] = jnp.dot(wup_ref[0, :, cs].astype(BF16), perm,
                                    preferred_element_type=F32).astype(BF16)
        wdn_bf[...] = wdn_ref[0].astype(BF16)

    @pl.when(i < nu_ref[0])
    def _():
        x = _unpack_halves(x_ref[...])
        hid = jnp.dot(x.astype(BF16), wup_bf[...], preferred_element_type=F32) + bup_ref[0]
        acts = []
        for g in range(hid.shape[1] // (2 * LANES)):
            glu = jnp.minimum(hid[:, g * 2 * LANES:g * 2 * LANES + LANES], SWIGLU_LIMIT)
            lin = jnp.clip(hid[:, g * 2 * LANES + LANES:(g + 1) * 2 * LANES], -SWIGLU_LIMIT, SWIGLU_LIMIT)
            acts.append(glu * jax.nn.sigmoid(SWIGLU_ALPHA * glu) * (lin + 1.0))
        act = jnp.concatenate(acts, axis=1)
        assert act.shape[1] == d_ff
        y = jnp.dot(act.astype(BF16), wdn_bf[...], preferred_element_type=F32) + bdn_ref[0]
        y_ref[...] = _pack_halves(y)

    @pl.when(i >= nu_ref[0])
    def _():
        y_ref[...] = jnp.zeros(y_ref.shape, y_ref.dtype)


def _experts(block_e, n_used, xs, w_up, b_up, w_down, b_down):
    n_rows, half = xs.shape
    d = w_up.shape[1]
    nb = n_rows // MOE_RB
    two_ff = w_up.shape[2]
    d_ff = w_down.shape[1]
    grid_spec = pltpu.PrefetchScalarGridSpec(
        num_scalar_prefetch=2,
        grid=(nb,),
        in_specs=[
            pl.BlockSpec((MOE_RB, half), lambda i, be, nu: (jnp.minimum(i, nu[0] - 1), 0)),
            pl.BlockSpec((1, d, two_ff), lambda i, be, nu: (be[i], 0, 0)),
            pl.BlockSpec((1, 1, two_ff), lambda i, be, nu: (be[i], 0, 0)),
            pl.BlockSpec((1, d_ff, d), lambda i, be, nu: (be[i], 0, 0)),
            pl.BlockSpec((1, 1, d), lambda i, be, nu: (be[i], 0, 0)),
        ],
        out_specs=pl.BlockSpec((MOE_RB, half), lambda i, be, nu: (i, 0)),
        scratch_shapes=[pltpu.VMEM((d, two_ff), BF16), pltpu.VMEM((d_ff, d), BF16)],
    )
    return pl.pallas_call(
        _expert_kernel,
        grid_spec=grid_spec,
        out_shape=jax.ShapeDtypeStruct((n_rows, half), jnp.int32),
        compiler_params=_cparams(("arbitrary",)),
        name="moe_experts",
    )(block_e, n_used, xs, w_up, b_up, w_down, b_down)


def _sc_invert_slots(dest_flat, n_rows):
    n_assign = dest_flat.shape[0]
    n_workers = SC_CORES * SC_SUBCORES
    rows_per_w = n_rows // n_workers
    chunk = SC_SCAN_CHUNK
    assert n_rows % n_workers == 0 and rows_per_w % SC_LANES == 0 and n_assign % chunk == 0
    mesh = plsc.VectorSubcoreMesh(core_axis_name="c", subcore_axis_name="s",
                                  num_cores=SC_CORES, num_subcores=SC_SUBCORES)

    def body(dest_hbm, out_hbm, dest_v, map_v):
        wid = lax.axis_index("s") * SC_CORES + lax.axis_index("c")
        base = wid * rows_per_w
        lanes = lax.broadcasted_iota(jnp.int32, (SC_LANES,), 0)

        @pl.loop(0, rows_per_w, step=SC_LANES)
        def _(r0):
            map_v[pl.ds(r0, SC_LANES)] = jnp.full((SC_LANES,), -1, jnp.int32)

        @pl.loop(0, n_assign // chunk)
        def _(ci):
            pltpu.sync_copy(dest_hbm.at[pl.ds(ci * chunk, chunk)], dest_v)

            @pl.loop(0, chunk, step=SC_LANES)
            def _(j):
                local = dest_v[pl.ds(j, SC_LANES)] - base
                mine = jnp.logical_and(local >= 0, local < rows_per_w)
                plsc.store_scatter(map_v, [jnp.where(mine, local, 0)], ci * chunk + j + lanes, mask=mine)

        pltpu.sync_copy(map_v, out_hbm.at[pl.ds(base, rows_per_w)])

    return pl.kernel(
        body,
        out_type=jax.ShapeDtypeStruct((n_rows,), jnp.int32),
        mesh=mesh,
        scratch_types=[pltpu.VMEM((chunk,), jnp.int32), pltpu.VMEM((rows_per_w,), jnp.int32)],
        compiler_params=pltpu.CompilerParams(needs_layout_passes=False),
        name="moe_slot_inverse",
    )(dest_flat)


def _sc_gather_rows(table, idx):
    n_idx = idx.shape[0]
    d = table.shape[1]
    n_workers = SC_CORES * SC_SUBCORES
    per_worker = n_idx // n_workers
    n_chunks = per_worker // SC_GATHER_ROWS
    assert n_idx % n_workers == 0 and per_worker % SC_GATHER_ROWS == 0
    mesh = plsc.VectorSubcoreMesh(core_axis_name="c", subcore_axis_name="s",
                                  num_cores=SC_CORES, num_subcores=SC_SUBCORES)

    assert n_chunks % 2 == 0

    def body(table_hbm, idx_hbm, out_hbm, idx_v, rows_a, rows_b, sem_a, sem_b):
        wid = lax.axis_index("s") * SC_CORES + lax.axis_index("c")
        base = wid * per_worker
        pltpu.sync_copy(idx_hbm.at[pl.ds(base, per_worker)], idx_v)

        def gather(ci, rows_v, sem):
            off = pl.multiple_of(ci * SC_GATHER_ROWS, SC_GATHER_ROWS)
            return pltpu.make_async_copy(table_hbm.at[idx_v.at[pl.ds(off, SC_GATHER_ROWS)]], rows_v, sem)

        def put(ci, rows_v):
            off = pl.multiple_of(ci * SC_GATHER_ROWS, SC_GATHER_ROWS)
            pltpu.sync_copy(rows_v, out_hbm.at[pl.ds(base + off, SC_GATHER_ROWS)])

        gather(0, rows_a, sem_a).start()

        @pl.loop(0, n_chunks, step=2)
        def _(ci):
            gather(ci + 1, rows_b, sem_b).start()
            gather(ci, rows_a, sem_a).wait()
            put(ci, rows_a)
            nxt = jnp.minimum(ci + 2, n_chunks - 1)
            gather(nxt, rows_a, sem_a).start()
            gather(ci + 1, rows_b, sem_b).wait()
            put(ci + 1, rows_b)

        gather(n_chunks - 1, rows_a, sem_a).wait()

    return pl.kernel(
        body,
        out_type=jax.ShapeDtypeStruct((n_idx, d), table.dtype),
        mesh=mesh,
        scratch_types=[
            pltpu.VMEM((per_worker,), jnp.int32),
            pltpu.VMEM((SC_GATHER_ROWS, d), table.dtype),
            pltpu.VMEM((SC_GATHER_ROWS, d), table.dtype),
            pltpu.SemaphoreType.DMA,
            pltpu.SemaphoreType.DMA,
        ],
        name="moe_slot_gather",
    )(table, idx)


def _combine_kernel(x1_ref, w_ref, y0_ref, y1_ref, y2_ref, y3_ref, o_ref):
    w = w_ref[...]
    out = x1_ref[...]
    for kk, y_ref in enumerate((y0_ref, y1_ref, y2_ref, y3_ref)):
        out = out + w[:, kk:kk + 1] * _unpack_halves(y_ref[...])
    o_ref[...] = out


def _combine(x1, w_tok, y_slots):
    t, d = x1.shape
    tc = COMB_TC
    nt = t // tc
    yspec = lambda kk: pl.BlockSpec((tc, d // 2), lambda i: (kk * nt + i, 0))
    return pl.pallas_call(
        _combine_kernel,
        grid=(nt,),
        in_specs=[
            pl.BlockSpec((tc, d), lambda i: (i, 0)),
            pl.BlockSpec((tc, TOP_K), lambda i: (i, 0)),
            yspec(0), yspec(1), yspec(2), yspec(3),
        ],
        out_specs=pl.BlockSpec((tc, d), lambda i: (i, 0)),
        out_shape=jax.ShapeDtypeStruct((t, d), F32),
        compiler_params=_cparams(("parallel",)),
        name="moe_combine",
    )(x1, w_tok, y_slots, y_slots, y_slots, y_slots)


def _moe(x1, h2, topi, topw, rank, counts, w_up, b_up, w_down, b_down):
    t, d = x1.shape
    n_assign = t * TOP_K
    nb = -(-n_assign // MOE_RB) + N_EXPERTS
    n_rows = nb * MOE_RB
    counts = counts[:, 0]
    padded = (counts + MOE_RB - 1) // MOE_RB * MOE_RB
    padded_end = jnp.cumsum(padded)
    padded_start = padded_end - padded
    expert_ids = jnp.arange(N_EXPERTS, dtype=jnp.int32)[:, None, None]
    start_of = jnp.sum(jnp.where(topi[None] == expert_ids, padded_start[:, None, None], 0), axis=0)
    dest = (start_of + rank).astype(jnp.int32)
    n_used = (padded_end[-1] // MOE_RB).astype(jnp.int32)
    blk = jnp.minimum(jnp.arange(nb, dtype=jnp.int32), n_used - 1)
    block_e = jnp.minimum(jnp.sum(padded_end[None, :] <= (blk * MOE_RB)[:, None], axis=1),
                          N_EXPERTS - 1).astype(jnp.int32)
    slot_of = _sc_invert_slots(dest.reshape(-1), n_rows)
    src_tok = jnp.where(slot_of < 0, jnp.arange(n_rows, dtype=jnp.int32), slot_of) % t

    xs = _sc_gather_rows(h2, src_tok)
    y_rows = _experts(block_e, n_used.reshape(1), xs, w_up, b_up, w_down, b_down)
    y_slots = _sc_gather_rows(y_rows, dest.reshape(-1))
    return _combine(x1, topw.T, y_slots)


def kernel(x, g_mix, w_in, b_gate, conv_w, a_log, dt_bias, g_delta_out, q_norm, k_norm, lambda_q1, lambda_k1, lambda_q2, lambda_k2, g_subln, rel_bias, w_o, g_ffn, w_router, b_router, w_up, b_up, w_down, b_down):
    bsz, seq, d = x.shape
    depth = g_mix.shape[0]
    n_heads = d // HEAD_DIM
    t = bsz * seq
    d_ff = w_down.shape[2]
    assert d % PROJ_TN == 0 and t % PROJ_TM == 0 and seq % GDN_TB == 0 and seq % ATT_BQ == 0
    assert t % MIX_TM == 0 and t % COMB_TC == 0 and n_heads % GDN_HG == 0
    assert (t * TOP_K) % MOE_RB == 0
    assert 2 * n_heads <= 2 * SUBLANES

    x2d = x.reshape(t, d)
    for l in range(depth):
        wl = w_in[l]
        c0 = 4 * d
        c1 = c0 + 2 * n_heads
        c2 = c1 + 2 * d
        c3 = c2 + d
        w_small = jnp.pad(wl[:, c0:c1], ((0, 0), (0, LANES - 2 * n_heads)))
        gm = g_mix[l].reshape(1, d)
        w_plain = jnp.concatenate([wl[:, :c0], wl[:, c2:c3]], axis=1).astype(BF16)
        proj_plain = _input_projection(x2d, gm, w_plain, jnp.zeros((1, 5 * d), F32), "plain")
        qk_gain = jnp.concatenate([jnp.tile(q_norm[l] * (DH_DIFF ** -0.5), 2 * n_heads),
                                   jnp.tile(k_norm[l], 2 * n_heads)]).reshape(1, 2 * d)
        proj_qk = _input_projection(x2d, gm, wl[:, c1:c2].astype(BF16), qk_gain, "qknorm")
        proj_gate = _input_projection(x2d, gm, wl[:, c3:].astype(BF16), b_gate[l].reshape(1, 2 * d), "gate")

        head_pad = jnp.zeros((LANES - 2 * n_heads,), F32)
        alog = jnp.concatenate([jnp.zeros((n_heads,), F32), a_log[l], head_pad])
        dtb = jnp.concatenate([jnp.zeros((n_heads,), F32), dt_bias[l], head_pad])
        rows_t = 2 * n_heads
        small, small_t = _small_projection(
            x2d, g_mix[l].reshape(1, d), w_small.astype(BF16), w_small[:, :rows_t].T.astype(BF16),
            alog.reshape(1, LANES), dtb.reshape(1, LANES),
            alog[:rows_t].reshape(rows_t, 1), dtb[:rows_t].reshape(rows_t, 1), n_heads)

        oa = _gated_delta(proj_plain, small, small_t, conv_w[l], g_delta_out[l].reshape(1, HEAD_DIM),
                          bsz, seq, n_heads, d)

        lam_init = 0.8 - 0.6 * math.exp(-0.3 * l)
        lam_params = jnp.stack([lambda_q1[l], lambda_k1[l], lambda_q2[l], lambda_k2[l]])
        od = _diff_attention(proj_qk, proj_plain, rel_bias, lam_params, g_subln[l].reshape(1, HEAD_DIM),
                             bsz, seq, n_heads, d, lam_init)

        x1, h2, topi, topw, rank, counts = _mix_project_route(
            proj_gate, oa, od, x2d, w_o[l].astype(BF16), g_ffn[l].reshape(1, d),
            w_router[l].T, b_router[l].reshape(N_EXPERTS, 1), d)

        b_up_l = b_up[l].reshape(N_EXPERTS, 2 * d_ff // (2 * LANES), LANES, 2)
        b_up_l = jnp.swapaxes(b_up_l, 2, 3).reshape(N_EXPERTS, 1, 2 * d_ff)
        x2d = _moe(x1, h2, topi, topw, rank, counts, w_up[l], b_up_l,
                   w_down[l], b_down[l].reshape(N_EXPERTS, 1, d))
    return x2d.reshape(bsz, seq, d)
```

```python
import functools
import math

import jax
import jax.numpy as jnp
from jax import lax
from jax.experimental import pallas as pl
from jax.experimental.pallas import tpu as pltpu
from jax.experimental.pallas import tpu_sc as plsc

F32 = jnp.float32
BF16 = jnp.bfloat16

HEAD_DIM = 128
DH_DIFF = HEAD_DIM // 2
CONV_WIDTH = 4
CHUNK = 64
N_BUCKETS = 32
MAX_DISTANCE = 128
N_EXPERTS = 32
TOP_K = 4
TOP_K_SHIFT = 2
SWIGLU_LIMIT = 7.0
SWIGLU_ALPHA = 1.702
EPS = 1e-6
NEG_BIG = -1e30

LANES = 128
SUBLANES = 8
VMEM_LIMIT = 56 * 1024 * 1024
SC_CORES = 2
SC_SUBCORES = 16
SC_LANES = 16
SC_GATHER_ROWS = 64
SC_SCAN_CHUNK = 4096

PROJ_TM = 2048
PROJ_TN = 1024
PROJ_CHUNK = 256
GDN_TB = 256
GDN_HG = 8
ATT_BQ = 512
ATT_BK = 512
MIX_TM = 512
MOE_RB = 256
COMB_TC = 512


def _cparams(sem):
    return pltpu.CompilerParams(dimension_semantics=sem, vmem_limit_bytes=VMEM_LIMIT)


def _bdot(a, b):
    return jnp.dot(a.astype(BF16), b.astype(BF16), preferred_element_type=F32)


def _bdot_nt(a, b):
    return lax.dot_general(a.astype(BF16), b.astype(BF16), (((1,), (1,)), ((), ())),
                           preferred_element_type=F32)


def _bdot_tn(a, b):
    return lax.dot_general(a.astype(BF16), b.astype(BF16), (((0,), (0,)), ((), ())),
                           preferred_element_type=F32)


def _proj_kernel(x_ref, g_ref, w_ref, aux_ref, o_ref, h_ref, *, mode):
    @pl.when(pl.program_id(1) == 0)
    def _():
        x = x_ref[...]
        ms = jnp.mean(x * x, axis=-1, keepdims=True)
        h_ref[...] = (x * lax.rsqrt(ms + EPS) * g_ref[...]).astype(BF16)

    h = h_ref[...]
    lo = lax.broadcasted_iota(jnp.int32, (1, LANES), 1) < DH_DIFF
    for c in range(PROJ_TN // PROJ_CHUNK):
        cs = slice(c * PROJ_CHUNK, (c + 1) * PROJ_CHUNK)
        acc = jnp.dot(h, w_ref[:, cs], preferred_element_type=F32)
        if mode == "plain":
            o_ref[:, cs] = acc.astype(o_ref.dtype)
        elif mode == "gate":
            o_ref[:, cs] = jax.nn.sigmoid(acc + aux_ref[:, cs]).astype(o_ref.dtype)
        else:
            for g in range(PROJ_CHUNK // LANES):
                sl = slice(c * PROJ_CHUNK + g * LANES, c * PROJ_CHUNK + (g + 1) * LANES)
                y = acc[:, g * LANES:(g + 1) * LANES]
                y2 = y * y
                s_lo = jnp.sum(jnp.where(lo, y2, 0.0), axis=-1, keepdims=True)
                s_hi = jnp.sum(jnp.where(lo, 0.0, y2), axis=-1, keepdims=True)
                r = jnp.where(lo, lax.rsqrt(s_lo / DH_DIFF + EPS), lax.rsqrt(s_hi / DH_DIFF + EPS))
                o_ref[:, sl] = (y * r * aux_ref[:, sl]).astype(o_ref.dtype)


def _input_projection(x2d, g_mix, w, aux, mode):
    t, d = x2d.shape
    n = w.shape[1]
    return pl.pallas_call(
        functools.partial(_proj_kernel, mode=mode),
        grid=(t // PROJ_TM, n // PROJ_TN),
        in_specs=[
            pl.BlockSpec((PROJ_TM, d), lambda i, j: (i, 0)),
            pl.BlockSpec((1, d), lambda i, j: (0, 0)),
            pl.BlockSpec((d, PROJ_TN), lambda i, j: (0, j)),
            pl.BlockSpec((1, PROJ_TN), lambda i, j: (0, j)),
        ],
        out_specs=pl.BlockSpec((PROJ_TM, PROJ_TN), lambda i, j: (i, j)),
        out_shape=jax.ShapeDtypeStruct((t, n), BF16),
        scratch_shapes=[pltpu.VMEM((PROJ_TM, d), BF16)],
        compiler_params=_cparams(("parallel", "arbitrary")),
        name="input_projection_" + mode,
    )(x2d, g_mix, w, aux)


def _small_proj_kernel(x_ref, g_ref, w_ref, wt_ref, alog_ref, dtb_ref, alog_t_ref, dtb_t_ref,
                       o_ref, ot_ref, *, n_heads):
    x = x_ref[...]
    ms = jnp.mean(x * x, axis=-1, keepdims=True)
    h = (x * lax.rsqrt(ms + EPS) * g_ref[...]).astype(BF16)

    def finish(acc, idx, alog, dtb):
        beta = jax.nn.sigmoid(acc)
        z = acc + dtb
        softplus = jnp.maximum(z, 0.0) + jnp.log1p(jnp.exp(-jnp.abs(z)))
        gdec = -jnp.exp(alog) * softplus
        return jnp.where(idx < n_heads, beta, jnp.where(idx < 2 * n_heads, gdec, 0.0))

    acc = jnp.dot(h, w_ref[...], preferred_element_type=F32)
    lane = lax.broadcasted_iota(jnp.int32, acc.shape, 1)
    o_ref[...] = finish(acc, lane, alog_ref[...], dtb_ref[...])
    acc_t = lax.dot_general(wt_ref[...], h, (((1,), (1,)), ((), ())),
                            preferred_element_type=F32)
    sub = lax.broadcasted_iota(jnp.int32, acc_t.shape, 0)
    ot_ref[...] = finish(acc_t, sub, alog_t_ref[...], dtb_t_ref[...])


def _small_projection(x2d, g_mix, w_small, w_small_t, alog, dtb, alog_t, dtb_t, n_heads):
    t, d = x2d.shape
    rows_t = w_small_t.shape[0]
    tm = PROJ_TM
    full = lambda shape: pl.BlockSpec(shape, lambda i: (0, 0))
    return pl.pallas_call(
        functools.partial(_small_proj_kernel, n_heads=n_heads),
        grid=(t // tm,),
        in_specs=[
            pl.BlockSpec((tm, d), lambda i: (i, 0)),
            full((1, d)), full((d, LANES)), full((rows_t, d)),
            full((1, LANES)), full((1, LANES)), full((rows_t, 1)), full((rows_t, 1)),
        ],
        out_specs=[pl.BlockSpec((tm, LANES), lambda i: (i, 0)),
                   pl.BlockSpec((rows_t, tm), lambda i: (0, i))],
        out_shape=[jax.ShapeDtypeStruct((t, LANES), F32),
                   jax.ShapeDtypeStruct((rows_t, t), F32)],
        compiler_params=_cparams(("parallel",)),
        name="beta_decay_projection",
    )(x2d, g_mix, w_small, w_small_t, alog, dtb, alog_t, dtb_t)


def _gdn_kernel(q_ref, k_ref, v_ref, z_ref, sm_ref, smt_ref, cwq_ref, cwk_ref, cwv_ref, gout_ref,
                o_ref, state_ref, qp_ref, kp_ref, vp_ref, vn_ref, *, n_heads):
    hg = pl.program_id(1)
    s = pl.program_id(2)
    tb = GDN_TB
    pad = SUBLANES
    width = GDN_HG * HEAD_DIM

    @pl.when(s == 0)
    def _():
        state_ref[...] = jnp.zeros_like(state_ref)
        for p_ref in (qp_ref, kp_ref, vp_ref):
            p_ref[0:pad, :] = jnp.zeros((pad, width), F32)

    def conv_silu(x_ref, p_ref, cw_ref):
        p_ref[pad:pad + tb, :] = x_ref[...].astype(F32)
        acc = cw_ref[CONV_WIDTH - 1:CONV_WIDTH, :] * p_ref[pad:pad + tb, :]
        for jj in range(CONV_WIDTH - 1):
            off = pad - (CONV_WIDTH - 1) + jj
            acc = acc + cw_ref[jj:jj + 1, :] * p_ref[off:off + tb, :]
        p_ref[0:pad, :] = p_ref[tb:tb + pad, :]
        return acc * jax.nn.sigmoid(acc)

    q_all = conv_silu(q_ref, qp_ref, cwq_ref)
    k_all = conv_silu(k_ref, kp_ref, cwk_ref)
    v_all = conv_silu(v_ref, vp_ref, cwv_ref)

    r = lax.broadcasted_iota(jnp.int32, (tb, tb), 0)
    c = lax.broadcasted_iota(jnp.int32, (tb, tb), 1)
    shift = int(math.log2(CHUNK))
    same = (r >> shift) == (c >> shift)
    incl = jnp.logical_and(same, c <= r)
    strict = jnp.logical_and(same, c < r)

    small = sm_ref[...]
    small_t = smt_ref[...]
    lane = lax.broadcasted_iota(jnp.int32, small.shape, 1)
    def split3(a):
        hi = a.astype(BF16)
        r1 = a - hi.astype(F32)
        mid = r1.astype(BF16)
        lo = (r1 - mid.astype(F32)).astype(BF16)
        return hi.astype(F32), mid.astype(F32), lo.astype(F32)

    part = 2 * n_heads
    s_hi, s_mid, s_lo = split3(small)
    small3 = jnp.where(lane < part, s_hi,
                       jnp.where(lane < 2 * part, pltpu.roll(s_mid, part, 1),
                                 jnp.where(lane < 3 * part, pltpu.roll(s_lo, 2 * part, 1), 0.0)))
    both = _bdot(jnp.concatenate([incl.astype(F32), same.astype(F32)], axis=0), small3)
    gcum = both[:tb]
    gtot = both[tb:]
    gcum_t = _bdot(jnp.concatenate(split3(small_t), axis=0),
                   jnp.logical_and(same, r <= c).astype(F32))
    sub3 = lax.broadcasted_iota(jnp.int32, gcum_t.shape, 0)

    heads = range(GDN_HG)
    hsl = [slice(hh * HEAD_DIM, (hh + 1) * HEAD_DIM) for hh in heads]
    qs = [q_all[:, hs] for hs in hsl]
    ks = [k_all[:, hs] for hs in hsl]
    vs = [v_all[:, hs] for hs in hsl]
    qs = [q * lax.rsqrt(jnp.sum(q * q, axis=-1, keepdims=True) + EPS) * (HEAD_DIM ** -0.5) for q in qs]
    ks = [k * lax.rsqrt(jnp.sum(k * k, axis=-1, keepdims=True) + EPS) for k in ks]

    def col_of(arr, idx):
        return jnp.sum(jnp.where(lane == idx, arr, 0.0), axis=-1, keepdims=True)

    def terms_of(pos, idx):
        return jnp.logical_or(pos == idx, jnp.logical_or(pos == idx + part, pos == idx + 2 * part))

    head_ids = [hg * GDN_HG + hh for hh in heads]
    betas = [col_of(small, hd) for hd in head_ids]
    gcs = [jnp.sum(jnp.where(terms_of(lane, hd + n_heads), gcum, 0.0), axis=-1, keepdims=True)
           for hd in head_ids]
    gls = [jnp.sum(jnp.where(terms_of(lane, hd + n_heads), gtot, 0.0), axis=-1, keepdims=True)
           for hd in head_ids]
    gc_rows = [jnp.sum(jnp.where(terms_of(sub3, hd + n_heads), gcum_t, 0.0), axis=0, keepdims=True)
               for hd in head_ids]

    decays = [jnp.where(incl, jnp.exp(jnp.minimum(gc - gr, 0.0)), 0.0) for gc, gr in zip(gcs, gc_rows)]
    kbs = [k * b for k, b in zip(ks, betas)]
    kks = [_bdot_nt(kb, k) for kb, k in zip(kbs, ks)]
    pws = [jnp.where(strict, -(kk * dc), 0.0) for kk, dc in zip(kks, decays)]
    n_chunks = tb // CHUNK
    cat_row = lax.broadcasted_iota(jnp.int32, (CHUNK, tb), 0)
    cat_lane = lax.broadcasted_iota(jnp.int32, (CHUNK, tb), 1)
    lane_chunk = cat_lane >> shift

    def block_diag(m_cat):
        return jnp.concatenate([jnp.where(lane_chunk == ci, m_cat, 0.0) for ci in range(n_chunks)], axis=0)

    def cat_of(m_bd):
        out = m_bd[0:CHUNK]
        for ci in range(1, n_chunks):
            out = out + m_bd[ci * CHUNK:(ci + 1) * CHUNK]
        return out

    pcats = [cat_of(pw) for pw in pws]
    eye_cat = ((cat_lane & (CHUNK - 1)) == cat_row).astype(F32)
    tcats = [eye_cat + pc for pc in pcats]
    pcats = [_bdot(pc, block_diag(pc)) for pc in pcats]
    n_levels = int(math.log2(CHUNK))
    for lev in range(1, n_levels):
        bds = [block_diag(pc) for pc in pcats]
        if lev < n_levels - 1:
            prods = [_bdot(jnp.concatenate([pc, tc], axis=0), bd) for pc, tc, bd in zip(pcats, tcats, bds)]
            pcats = [pr[:CHUNK] for pr in prods]
            tcats = [tc + pr[CHUNK:] for tc, pr in zip(tcats, prods)]
        else:
            tcats = [tc + _bdot(tc, bd) for tc, bd in zip(tcats, bds)]
    tmats = [block_diag(tc) for tc in tcats]
    egcs = [jnp.exp(gc) for gc in gcs]
    uws = [_bdot(tm, jnp.concatenate([v * b, kb * eg], axis=1))
           for tm, v, b, kb, eg in zip(tmats, vs, betas, kbs, egcs)]
    us = [uw[:, :HEAD_DIM] for uw in uws]
    ws = [uw[:, HEAD_DIM:] for uw in uws]
    qkm = [_bdot_nt(q, k) for q, k in zip(qs, ks)]
    qkm = [jnp.where(incl, x * dc, 0.0) for x, dc in zip(qkm, decays)]
    q_decs = [q * eg for q, eg in zip(qs, egcs)]
    k_ends = [k * jnp.exp(gl - gc) for k, gl, gc in zip(ks, gls, gcs)]

    for hh in heads:
        vn_ref[hh] = jnp.zeros((tb, HEAD_DIM), F32)
    outs = [[] for _ in heads]
    for ci in range(tb // CHUNK):
        cs = slice(ci * CHUNK, (ci + 1) * CHUNK)
        sts = [state_ref[hh] for hh in heads]
        ws_qs = [_bdot(jnp.concatenate([ws[hh][cs], q_decs[hh][cs]], axis=0), sts[hh]) for hh in heads]
        v_news = [us[hh][cs] - ws_qs[hh][:CHUNK] for hh in heads]
        for hh in heads:
            vn_ref[hh, cs, :] = v_news[hh]
        intra = [_bdot(qkm[hh][cs], vn_ref[hh]) for hh in heads]
        upd = [_bdot_tn(k_ends[hh][cs], v_news[hh]) for hh in heads]
        for hh in heads:
            outs[hh].append(ws_qs[hh][CHUNK:] + intra[hh])
            g_last = gls[hh][ci * CHUNK:ci * CHUNK + 1, :]
            state_ref[hh] = sts[hh] * jnp.exp(g_last) + upd[hh]
    for hh in heads:
        o = jnp.concatenate(outs[hh], axis=0)
        o = o * lax.rsqrt(jnp.mean(o * o, axis=-1, keepdims=True) + EPS) * gout_ref[...]
        zz = z_ref[:, hsl[hh]].astype(F32)
        o_ref[:, hsl[hh]] = (o * (zz * jax.nn.sigmoid(zz))).astype(o_ref.dtype)


def _gated_delta(big, small, small_t, conv_w, g_out, bsz, seq, n_heads, d_model):
    t = bsz * seq
    tb = GDN_TB
    ns = seq // tb
    width = GDN_HG * HEAD_DIM
    nhg = n_heads // GDN_HG
    blocks_per_group = d_model // width
    rows_t = small_t.shape[0]

    def colspec(group):
        return pl.BlockSpec((tb, width), lambda b, h, s: (b * ns + s, group * blocks_per_group + h))

    def cwspec(group):
        return pl.BlockSpec((CONV_WIDTH, width), lambda b, h, s: (0, group * blocks_per_group + h))

    return pl.pallas_call(
        functools.partial(_gdn_kernel, n_heads=n_heads),
        grid=(bsz, nhg, ns),
        in_specs=[
            colspec(0), colspec(1), colspec(2), colspec(3),
            pl.BlockSpec((tb, LANES), lambda b, h, s: (b * ns + s, 0)),
            pl.BlockSpec((rows_t, tb), lambda b, h, s: (0, b * ns + s)),
            cwspec(0), cwspec(1), cwspec(2),
            pl.BlockSpec((1, HEAD_DIM), lambda b, h, s: (0, 0)),
        ],
        out_specs=pl.BlockSpec((tb, width), lambda b, h, s: (b * ns + s, h)),
        out_shape=jax.ShapeDtypeStruct((t, d_model), BF16),
        scratch_shapes=[
            pltpu.VMEM((GDN_HG, HEAD_DIM, HEAD_DIM), F32),
            pltpu.VMEM((tb + SUBLANES, width), F32),
            pltpu.VMEM((tb + SUBLANES, width), F32),
            pltpu.VMEM((tb + SUBLANES, width), F32),
            pltpu.VMEM((GDN_HG, tb, HEAD_DIM), F32),
        ],
        compiler_params=_cparams(("parallel", "parallel", "arbitrary")),
        name="gated_delta",
    )(big, big, big, big, small, small_t, conv_w, conv_w, conv_w, g_out)


def _t5_bucket(n):
    max_exact = N_BUCKETS // 2
    nf = jnp.maximum(n, 1).astype(F32)
    large = max_exact + (jnp.log(nf / max_exact) / math.log(MAX_DISTANCE / max_exact)
                         * (N_BUCKETS - max_exact)).astype(jnp.int32)
    large = jnp.minimum(large, N_BUCKETS - 1)
    return jnp.where(n < max_exact, n, large)


def _attn_kernel(rb_ref, q_ref, k_ref, v_ref, lam_ref, gsub_ref, o_ref,
                 bias_ref, m_ref, acc_ref, sa_ref, sb_ref, *, lam_init):
    h = pl.program_id(0)
    b = pl.program_id(1)
    qi = pl.program_id(2)
    bq, bk = ATT_BQ, ATT_BK

    @pl.when(jnp.logical_and(b == 0, qi == 0))
    def _():
        i = lax.broadcasted_iota(jnp.int32, (bq, bk), 0)
        jj = lax.broadcasted_iota(jnp.int32, (bq, bk), 1)
        far = rb_ref[N_BUCKETS - 1, h]
        bias_ref[2] = jnp.zeros((bq, bk), F32)
        for slot in range(2):
            n = i - jj + slot * bk
            bucket = _t5_bucket(jnp.maximum(n, 0))
            bias = jnp.zeros((bq, bk), F32)
            for cc in range(N_BUCKETS):
                bias = jnp.where(bucket == cc, rb_ref[cc, h] - far, bias)
            if slot == 0:
                bias = jnp.where(n >= 0, bias, NEG_BIG)
            bias_ref[slot] = bias

    m_ref[...] = jnp.full(m_ref.shape, NEG_BIG, F32)
    acc_ref[...] = jnp.zeros(acc_ref.shape, F32)

    q = q_ref[...]
    lane = lax.broadcasted_iota(jnp.int32, q.shape, 1)
    zero = jnp.zeros_like(q)
    qs = jnp.concatenate([jnp.where(lane < DH_DIFF, q, zero), jnp.where(lane < DH_DIFF, zero, q)], axis=0)
    ones_col = (lax.broadcasted_iota(jnp.int32, (bk, HEAD_DIM), 1) == 0).astype(BF16)

    def scores(j):
        ks = pl.multiple_of(j * bk, bk)
        return lax.dot_general(qs, k_ref[pl.ds(ks, bk), :], (((1,), (1,)), ((), ())),
                               preferred_element_type=F32)

    def absorb(j, sc_ref, biased=True):
        ks = pl.multiple_of(j * bk, bk)
        v_ext = jnp.concatenate([v_ref[pl.ds(ks, bk), :], ones_col], axis=1)
        if biased:
            bias = bias_ref[jnp.minimum(qi - j, 2)]
            sc = jnp.concatenate([sc_ref[0:bq, :] + bias, sc_ref[bq:2 * bq, :] + bias], axis=0)
        else:
            sc = sc_ref[...]
        m_old = m_ref[...]
        m_new = jnp.maximum(m_old, jnp.max(sc, axis=-1, keepdims=True))
        p = jnp.exp(sc - m_new)
        acc_ref[...] = (jnp.exp(m_old - m_new) * acc_ref[...]
                        + jnp.dot(p.astype(BF16), v_ext, preferred_element_type=F32))
        m_ref[...] = m_new

    n_tiles = qi + 1
    sa_ref[...] = scores(0)

    def pair_body(jj, carry, biased):
        j0 = 2 * jj
        sb_ref[...] = scores(j0 + 1)
        absorb(j0, sa_ref, biased)
        sa_ref[...] = scores(jnp.minimum(j0 + 2, qi))
        absorb(j0 + 1, sb_ref, biased)
        return carry

    n_far_pairs = jnp.maximum(qi - 1, 0) // 2
    lax.fori_loop(0, n_far_pairs, functools.partial(pair_body, biased=False), 0)
    lax.fori_loop(n_far_pairs, n_tiles // 2, functools.partial(pair_body, biased=True), 0)

    @pl.when(n_tiles % 2 == 1)
    def _():
        absorb(qi, sa_ref)

    lam_p = lam_ref[...]
    s1 = jnp.sum(lam_p[0:1] * lam_p[1:2], axis=-1, keepdims=True)
    s2 = jnp.sum(lam_p[2:3] * lam_p[3:4], axis=-1, keepdims=True)
    lam = jnp.exp(s1) - jnp.exp(s2) + lam_init
    acc = acc_ref[...]
    num = acc[:, :HEAD_DIM]
    den = acc[:, HEAD_DIM:HEAD_DIM + 1]
    o = num[:bq] / den[:bq] - lam * (num[bq:] / den[bq:])
    o = o * lax.rsqrt(jnp.mean(o * o, axis=-1, keepdims=True) + EPS) * gsub_ref[...]
    o_ref[...] = (o * (1.0 - lam_init)).astype(o_ref.dtype)


def _diff_attention(proj_qk, proj_plain, rel_bias, lam_params, g_subln, bsz, seq, n_heads, d_model,
                    lam_init):
    t = bsz * seq
    nq = seq // ATT_BQ
    per = d_model // HEAD_DIM
    vcol = 4 * per
    return pl.pallas_call(
        functools.partial(_attn_kernel, lam_init=lam_init),
        grid=(n_heads, bsz, nq),
        in_specs=[
            pl.BlockSpec(memory_space=pltpu.SMEM),
            pl.BlockSpec((ATT_BQ, HEAD_DIM), lambda h, b, i: (b * nq + i, h)),
            pl.BlockSpec((seq, HEAD_DIM), lambda h, b, i: (b, per + h)),
            pl.BlockSpec((seq, HEAD_DIM), lambda h, b, i: (b, vcol + h)),
            pl.BlockSpec((4, DH_DIFF), lambda h, b, i: (0, 0)),
            pl.BlockSpec((1, HEAD_DIM), lambda h, b, i: (0, 0)),
        ],
        out_specs=pl.BlockSpec((ATT_BQ, HEAD_DIM), lambda h, b, i: (b * nq + i, h)),
        out_shape=jax.ShapeDtypeStruct((t, d_model), BF16),
        scratch_shapes=[
            pltpu.VMEM((3, ATT_BQ, ATT_BK), F32),
            pltpu.VMEM((2 * ATT_BQ, 1), F32),
            pltpu.VMEM((2 * ATT_BQ, 2 * HEAD_DIM), F32),
            pltpu.VMEM((2 * ATT_BQ, ATT_BK), F32),
            pltpu.VMEM((2 * ATT_BQ, ATT_BK), F32),
        ],
        compiler_params=_cparams(("arbitrary", "arbitrary", "arbitrary")),
        name="diff_attention",
    )(rel_bias, proj_qk, proj_qk, proj_plain, lam_params, g_subln)


def _mix_kernel(ga_ref, gb_ref, oa_ref, od_ref, x_ref, wo_ref, gffn_ref, wr_ref, br_ref,
                x1_ref, h2_ref, topi_ref, topw_ref, rank_ref, cnt_ref, carry_ref):
    i = pl.program_id(0)
    tm = MIX_TM

    @pl.when(i == 0)
    def _():
        carry_ref[...] = jnp.zeros_like(carry_ref)

    mix = (ga_ref[...].astype(F32) * oa_ref[...].astype(F32)
           + gb_ref[...].astype(F32) * od_ref[...].astype(F32))
    x1 = x_ref[...] + jnp.dot(mix.astype(BF16), wo_ref[...], preferred_element_type=F32)
    x1_ref[...] = x1
    h2 = x1 * lax.rsqrt(jnp.mean(x1 * x1, axis=-1, keepdims=True) + EPS) * gffn_ref[...]
    h2_ref[...] = _pack_halves(h2)

    logits = lax.dot_general(wr_ref[...], h2, (((1,), (1,)), ((), ())),
                             preferred_element_type=F32, precision=lax.Precision.HIGHEST) + br_ref[...]
    eidx = lax.broadcasted_iota(jnp.int32, logits.shape, 0).astype(F32)
    vals, hots = [], []
    cur = logits
    for kk in range(TOP_K):
        mx = jnp.max(cur, axis=0, keepdims=True)
        idx = jnp.min(jnp.where(cur == mx, eidx, float(N_EXPERTS)), axis=0, keepdims=True)
        hot = eidx == idx
        vals.append(mx)
        hots.append(hot)
        topi_ref[kk:kk + 1, :] = idx.astype(jnp.int32)
        cur = jnp.where(hot, -jnp.inf, cur)
    exps = [jnp.exp(vv - vals[0]) for vv in vals]
    denom = exps[0] + exps[1] + exps[2] + exps[3]
    for kk in range(TOP_K):
        topw_ref[kk:kk + 1, :] = exps[kk] / denom

    sel = hots[0]
    for kk in range(1, TOP_K):
        sel = jnp.logical_or(sel, hots[kk])
    sel_f = sel.astype(F32)
    r = lax.broadcasted_iota(jnp.int32, (tm, tm), 0)
    c = lax.broadcasted_iota(jnp.int32, (tm, tm), 1)
    before = _bdot(sel_f, (r < c).astype(F32)) + carry_ref[...]
    for kk in range(TOP_K):
        rank_ref[kk:kk + 1, :] = jnp.sum(jnp.where(hots[kk], before, 0.0), axis=0,
                                         keepdims=True).astype(jnp.int32)
    carry_ref[...] = carry_ref[...] + jnp.sum(sel_f, axis=-1, keepdims=True)
    cnt_ref[...] = carry_ref[...].astype(jnp.int32)


def _mix_project_route(proj_gate, oa, od, x2d, w_o, g_ffn, w_r_t, b_r, d_model):
    t = x2d.shape[0]
    tm = MIX_TM
    full = lambda shape: pl.BlockSpec(shape, lambda i: (0, 0))
    row = lambda: pl.BlockSpec((tm, d_model), lambda i: (i, 0))
    krow = lambda: pl.BlockSpec((TOP_K, tm), lambda i: (0, i))
    return pl.pallas_call(
        _mix_kernel,
        grid=(t // tm,),
        in_specs=[
            pl.BlockSpec((tm, d_model), lambda i: (i, 0)),
            pl.BlockSpec((tm, d_model), lambda i: (i, 1)),
            row(), row(), row(),
            full((d_model, d_model)), full((1, d_model)), full((N_EXPERTS, d_model)), full((N_EXPERTS, 1)),
        ],
        out_specs=[row(), pl.BlockSpec((tm, d_model // 2), lambda i: (i, 0)),
                   krow(), krow(), krow(), full((N_EXPERTS, 1))],
        out_shape=[
            jax.ShapeDtypeStruct((t, d_model), F32),
            jax.ShapeDtypeStruct((t, d_model // 2), jnp.int32),
            jax.ShapeDtypeStruct((TOP_K, t), jnp.int32),
            jax.ShapeDtypeStruct((TOP_K, t), F32),
            jax.ShapeDtypeStruct((TOP_K, t), jnp.int32),
            jax.ShapeDtypeStruct((N_EXPERTS, 1), jnp.int32),
        ],
        scratch_shapes=[pltpu.VMEM((N_EXPERTS, 1), F32)],
        compiler_params=_cparams(("arbitrary",)),
        name="merge_outproj_route",
    )(proj_gate, proj_gate, oa, od, x2d, w_o, g_ffn, w_r_t, b_r)


def _pack_halves(x):
    half = x.shape[1] // 2
    bits = pltpu.bitcast(x.astype(BF16).astype(F32), jnp.int32)
    return bits[:, :half] | lax.shift_right_logical(bits[:, half:], 16)


def _unpack_halves(p):
    hi = pltpu.bitcast(p & jnp.int32(-65536), F32)
    lo = pltpu.bitcast(lax.shift_left(p, 16), F32)
    return jnp.concatenate([hi, lo], axis=1)


def _expert_kernel(be_ref, nu_ref, x_ref, wup_ref, bup_ref, wdn_ref, bdn_ref, y_ref, wup_bf, wdn_bf):
    i = pl.program_id(0)
    d_ff = wdn_ref.shape[1]

    @pl.when(jnp.logical_or(i == 0, be_ref[i] != be_ref[jnp.maximum(i - 1, 0)]))
    def _():
        rr = lax.broadcasted_iota(jnp.int32, (2 * LANES, 2 * LANES), 0)
        cc = lax.broadcasted_iota(jnp.int32, (2 * LANES, 2 * LANES), 1)
        pick = jnp.where(cc < LANES, 2 * cc, 2 * (cc - LANES) + 1)
        perm = (rr == pick).astype(BF16)
        for g in range(wup_ref.shape[2] // (2 * LANES)):
            cs = slice(g * 2 * LANES, (g + 1) * 2 * LANES)
            wup_bf[:, cs] = jnp.dot(wup_ref[0, :, cs].astype(BF16), perm,
                                    preferred_element_type=F32).astype(BF16)
        wdn_bf[...] = wdn_ref[0].astype(BF16)

    @pl.when(i < nu_ref[0])
    def _():
        x = _unpack_halves(x_ref[...])
        hid = jnp.dot(x.astype(BF16), wup_bf[...], preferred_element_type=F32) + bup_ref[0]
        acts = []
        for g in range(hid.shape[1] // (2 * LANES)):
            glu = jnp.minimum(hid[:, g * 2 * LANES:g * 2 * LANES + LANES], SWIGLU_LIMIT)
            lin = jnp.clip(hid[:, g * 2 * LANES + LANES:(g + 1) * 2 * LANES], -SWIGLU_LIMIT, SWIGLU_LIMIT)
            acts.append(glu * jax.nn.sigmoid(SWIGLU_ALPHA * glu) * (lin + 1.0))
        act = jnp.concatenate(acts, axis=1)
        assert act.shape[1] == d_ff
        y = jnp.dot(act.astype(BF16), wdn_bf[...], preferred_element_type=F32) + bdn_ref[0]
        y_ref[...] = _pack_halves(y)

    @pl.when(i >= nu_ref[0])
    def _():
        y_ref[...] = jnp.zeros(y_ref.shape, y_ref.dtype)


def _experts(block_e, n_used, xs, w_up, b_up, w_down, b_down):
    n_rows, half = xs.shape
    d = w_up.shape[1]
    nb = n_rows // MOE_RB
    two_ff = w_up.shape[2]
    d_ff = w_down.shape[1]
    grid_spec = pltpu.PrefetchScalarGridSpec(
        num_scalar_prefetch=2,
        grid=(nb,),
        in_specs=[
            pl.BlockSpec((MOE_RB, half), lambda i, be, nu: (jnp.minimum(i, nu[0] - 1), 0)),
            pl.BlockSpec((1, d, two_ff), lambda i, be, nu: (be[i], 0, 0)),
            pl.BlockSpec((1, 1, two_ff), lambda i, be, nu: (be[i], 0, 0)),
            pl.BlockSpec((1, d_ff, d), lambda i, be, nu: (be[i], 0, 0)),
            pl.BlockSpec((1, 1, d), lambda i, be, nu: (be[i], 0, 0)),
        ],
        out_specs=pl.BlockSpec((MOE_RB, half), lambda i, be, nu: (i, 0)),
        scratch_shapes=[pltpu.VMEM((d, two_ff), BF16), pltpu.VMEM((d_ff, d), BF16)],
    )
    return pl.pallas_call(
        _expert_kernel,
        grid_spec=grid_spec,
        out_shape=jax.ShapeDtypeStruct((n_rows, half), jnp.int32),
        compiler_params=_cparams(("arbitrary",)),
        name="moe_experts",
    )(block_e, n_used, xs, w_up, b_up, w_down, b_down)


def _sc_invert_slots(dest_flat, n_rows):
    n_assign = dest_flat.shape[0]
    n_workers = SC_CORES * SC_SUBCORES
    rows_per_w = n_rows // n_workers
    chunk = SC_SCAN_CHUNK
    assert n_rows % n_workers == 0 and rows_per_w % SC_LANES == 0 and n_assign % chunk == 0
    mesh = plsc.VectorSubcoreMesh(core_axis_name="c", subcore_axis_name="s",
                                  num_cores=SC_CORES, num_subcores=SC_SUBCORES)

    def body(dest_hbm, out_hbm, dest_v, map_v):
        wid = lax.axis_index("s") * SC_CORES + lax.axis_index("c")
        base = wid * rows_per_w
        lanes = lax.broadcasted_iota(jnp.int32, (SC_LANES,), 0)

        @pl.loop(0, rows_per_w, step=SC_LANES)
        def _(r0):
            map_v[pl.ds(r0, SC_LANES)] = jnp.full((SC_LANES,), -1, jnp.int32)

        @pl.loop(0, n_assign // chunk)
        def _(ci):
            pltpu.sync_copy(dest_hbm.at[pl.ds(ci * chunk, chunk)], dest_v)

            @pl.loop(0, chunk, step=SC_LANES)
            def _(j):
                local = dest_v[pl.ds(j, SC_LANES)] - base
                mine = jnp.logical_and(local >= 0, local < rows_per_w)
                plsc.store_scatter(map_v, [jnp.where(mine, local, 0)], ci * chunk + j + lanes, mask=mine)

        pltpu.sync_copy(map_v, out_hbm.at[pl.ds(base, rows_per_w)])

    return pl.kernel(
        body,
        out_type=jax.ShapeDtypeStruct((n_rows,), jnp.int32),
        mesh=mesh,
        scratch_types=[pltpu.VMEM((chunk,), jnp.int32), pltpu.VMEM((rows_per_w,), jnp.int32)],
        compiler_params=pltpu.CompilerParams(needs_layout_passes=False),
        name="moe_slot_inverse",
    )(dest_flat)


def _sc_gather_rows(table, idx):
    n_idx = idx.shape[0]
    d = table.shape[1]
    n_workers = SC_CORES * SC_SUBCORES
    per_worker = n_idx // n_workers
    n_chunks = per_worker // SC_GATHER_ROWS
    assert n_idx % n_workers == 0 and per_worker % SC_GATHER_ROWS == 0
    mesh = plsc.VectorSubcoreMesh(core_axis_name="c", subcore_axis_name="s",
                                  num_cores=SC_CORES, num_subcores=SC_SUBCORES)

    assert n_chunks % 2 == 0

    def body(table_hbm, idx_hbm, out_hbm, idx_v, rows_a, rows_b, sem_a, sem_b):
        wid = lax.axis_index("s") * SC_CORES + lax.axis_index("c")
        base = wid * per_worker
        pltpu.sync_copy(idx_hbm.at[pl.ds(base, per_worker)], idx_v)

        def gather(ci, rows_v, sem):
            off = pl.multiple_of(ci * SC_GATHER_ROWS, SC_GATHER_ROWS)
            return pltpu.make_async_copy(table_hbm.at[idx_v.at[pl.ds(off, SC_GATHER_ROWS)]], rows_v, sem)

        def put(ci, rows_v):
            off = pl.multiple_of(ci * SC_GATHER_ROWS, SC_GATHER_ROWS)
            pltpu.sync_copy(rows_v, out_hbm.at[pl.ds(base + off, SC_GATHER_ROWS)])

        gather(0, rows_a, sem_a).start()

        @pl.loop(0, n_chunks, step=2)
        def _(ci):
            gather(ci + 1, rows_b, sem_b).start()
            gather(ci, rows_a, sem_a).wait()
            put(ci, rows_a)
            nxt = jnp.minimum(ci + 2, n_chunks - 1)
            gather(nxt, rows_a, sem_a).start()
            gather(ci + 1, rows_b, sem_b).wait()
            put(ci + 1, rows_b)

        gather(n_chunks - 1, rows_a, sem_a).wait()

    return pl.kernel(
        body,
        out_type=jax.ShapeDtypeStruct((n_idx, d), table.dtype),
        mesh=mesh,
        scratch_types=[
            pltpu.VMEM((per_worker,), jnp.int32),
            pltpu.VMEM((SC_GATHER_ROWS, d), table.dtype),
            pltpu.VMEM((SC_GATHER_ROWS, d), table.dtype),
            pltpu.SemaphoreType.DMA,
            pltpu.SemaphoreType.DMA,
        ],
        name="moe_slot_gather",
    )(table, idx)


def _combine_kernel(x1_ref, w_ref, y0_ref, y1_ref, y2_ref, y3_ref, o_ref):
    w = w_ref[...]
    out = x1_ref[...]
    for kk, y_ref in enumerate((y0_ref, y1_ref, y2_ref, y3_ref)):
        out = out + w[:, kk:kk + 1] * _unpack_halves(y_ref[...])
    o_ref[...] = out


def _combine(x1, w_tok, y_slots):
    t, d = x1.shape
    tc = COMB_TC
    nt = t // tc
    yspec = lambda kk: pl.BlockSpec((tc, d // 2), lambda i: (kk * nt + i, 0))
    return pl.pallas_call(
        _combine_kernel,
        grid=(nt,),
        in_specs=[
            pl.BlockSpec((tc, d), lambda i: (i, 0)),
            pl.BlockSpec((tc, TOP_K), lambda i: (i, 0)),
            yspec(0), yspec(1), yspec(2), yspec(3),
        ],
        out_specs=pl.BlockSpec((tc, d), lambda i: (i, 0)),
        out_shape=jax.ShapeDtypeStruct((t, d), F32),
        compiler_params=_cparams(("parallel",)),
        name="moe_combine",
    )(x1, w_tok, y_slots, y_slots, y_slots, y_slots)


def _moe(x1, h2, topi, topw, rank, counts, w_up, b_up, w_down, b_down):
    t, d = x1.shape
    n_assign = t * TOP_K
    nb = -(-n_assign // MOE_RB) + N_EXPERTS
    n_rows = nb * MOE_RB
    counts = counts[:, 0]
    padded = (counts + MOE_RB - 1) // MOE_RB * MOE_RB
    padded_end = jnp.cumsum(padded)
    padded_start = padded_end - padded
    expert_ids = jnp.arange(N_EXPERTS, dtype=jnp.int32)[:, None, None]
    start_of = jnp.sum(jnp.where(topi[None] == expert_ids, padded_start[:, None, None], 0), axis=0)
    dest = (start_of + rank).astype(jnp.int32)
    n_used = (padded_end[-1] // MOE_RB).astype(jnp.int32)
    blk = jnp.minimum(jnp.arange(nb, dtype=jnp.int32), n_used - 1)
    block_e = jnp.minimum(jnp.sum(padded_end[None, :] <= (blk * MOE_RB)[:, None], axis=1),
                          N_EXPERTS - 1).astype(jnp.int32)
    slot_of = _sc_invert_slots(dest.reshape(-1), n_rows)
    src_tok = jnp.where(slot_of < 0, jnp.arange(n_rows, dtype=jnp.int32), slot_of) % t

    xs = _sc_gather_rows(h2, src_tok)
    y_rows = _experts(block_e, n_used.reshape(1), xs, w_up, b_up, w_down, b_down)
    y_slots = _sc_gather_rows(y_rows, dest.reshape(-1))
    return _combine(x1, topw.T, y_slots)


def kernel(x, g_mix, w_in, b_gate, conv_w, a_log, dt_bias, g_delta_out, q_norm, k_norm, lambda_q1, lambda_k1, lambda_q2, lambda_k2, g_subln, rel_bias, w_o, g_ffn, w_router, b_router, w_up, b_up, w_down, b_down):
    bsz, seq, d = x.shape
    depth = g_mix.shape[0]
    n_heads = d // HEAD_DIM
    t = bsz * seq
    d_ff = w_down.shape[2]
    assert d % PROJ_TN == 0 and t % PROJ_TM == 0 and seq % GDN_TB == 0 and seq % ATT_BQ == 0
    assert t % MIX_TM == 0 and t % COMB_TC == 0 and n_heads % GDN_HG == 0
    assert (t * TOP_K) % MOE_RB == 0
    assert 2 * n_heads <= 2 * SUBLANES

    x2d = x.reshape(t, d)
    for l in range(depth):
        wl = w_in[l]
        c0 = 4 * d
        c1 = c0 + 2 * n_heads
        c2 = c1 + 2 * d
        c3 = c2 + d
        w_small = jnp.pad(wl[:, c0:c1], ((0, 0), (0, LANES - 2 * n_heads)))
        gm = g_mix[l].reshape(1, d)
        w_plain = jnp.concatenate([wl[:, :c0], wl[:, c2:c3]], axis=1).astype(BF16)
        proj_plain = _input_projection(x2d, gm, w_plain, jnp.zeros((1, 5 * d), F32), "plain")
        qk_gain = jnp.concatenate([jnp.tile(q_norm[l] * (DH_DIFF ** -0.5), 2 * n_heads),
                                   jnp.tile(k_norm[l], 2 * n_heads)]).reshape(1, 2 * d)
        proj_qk = _input_projection(x2d, gm, wl[:, c1:c2].astype(BF16), qk_gain, "qknorm")
        proj_gate = _input_projection(x2d, gm, wl[:, c3:].astype(BF16), b_gate[l].reshape(1, 2 * d), "gate")

        head_pad = jnp.zeros((LANES - 2 * n_heads,), F32)
        alog = jnp.concatenate([jnp.zeros((n_heads,), F32), a_log[l], head_pad])
        dtb = jnp.concatenate([jnp.zeros((n_heads,), F32), dt_bias[l], head_pad])
        rows_t = 2 * n_heads
        small, small_t = _small_projection(
            x2d, g_mix[l].reshape(1, d), w_small.astype(BF16), w_small[:, :rows_t].T.astype(BF16),
            alog.reshape(1, LANES), dtb.reshape(1, LANES),
            alog[:rows_t].reshape(rows_t, 1), dtb[:rows_t].reshape(rows_t, 1), n_heads)

        oa = _gated_delta(proj_plain, small, small_t, conv_w[l], g_delta_out[l].reshape(1, HEAD_DIM),
                          bsz, seq, n_heads, d)

        lam_init = 0.8 - 0.6 * math.exp(-0.3 * l)
        lam_params = jnp.stack([lambda_q1[l], lambda_k1[l], lambda_q2[l], lambda_k2[l]])
        od = _diff_attention(proj_qk, proj_plain, rel_bias, lam_params, g_subln[l].reshape(1, HEAD_DIM),
                             bsz, seq, n_heads, d, lam_init)

        x1, h2, topi, topw, rank, counts = _mix_project_route(
            proj_gate, oa, od, x2d, w_o[l].astype(BF16), g_ffn[l].reshape(1, d),
            w_router[l].T, b_router[l].reshape(N_EXPERTS, 1), d)

        b_up_l = b_up[l].reshape(N_EXPERTS, 2 * d_ff // (2 * LANES), LANES, 2)
        b_up_l = jnp.swapaxes(b_up_l, 2, 3).reshape(N_EXPERTS, 1, 2 * d_ff)
        x2d = _moe(x1, h2, topi, topw, rank, counts, w_up[l], b_up_l,
                   w_down[l], b_down[l].reshape(N_EXPERTS, 1, d))
    return x2d.reshape(bsz, seq, d)
```

```python
import functools
import math

import jax
import jax.numpy as jnp
from jax import lax
from jax.experimental import pallas as pl
from jax.experimental.pallas import tpu as pltpu
from jax.experimental.pallas import tpu_sc as plsc

F32 = jnp.float32
BF16 = jnp.bfloat16

HEAD_DIM = 128
DH_DIFF = HEAD_DIM // 2
CONV_WIDTH = 4
CHUNK = 64
N_BUCKETS = 32
MAX_DISTANCE = 128
N_EXPERTS = 32
TOP_K = 4
TOP_K_SHIFT = 2
SWIGLU_LIMIT = 7.0
SWIGLU_ALPHA = 1.702
EPS = 1e-6
NEG_BIG = -1e30

LANES = 128
SUBLANES = 8
VMEM_LIMIT = 56 * 1024 * 1024
SC_CORES = 2
SC_SUBCORES = 16
SC_LANES = 16
SC_GATHER_ROWS = 64
SC_SCAN_CHUNK = 4096

PROJ_TM = 2048
PROJ_TN = 1024
PROJ_CHUNK = 256
GDN_TB = 256
GDN_HG = 8
ATT_HG = 1
ATT_BQ = 512
ATT_BK = 512
MIX_TM = 512
MOE_RB = 512
COMB_TC = 512


def _cparams(sem):
    return pltpu.CompilerParams(dimension_semantics=sem, vmem_limit_bytes=VMEM_LIMIT)


def _bdot(a, b):
    return jnp.dot(a.astype(BF16), b.astype(BF16), preferred_element_type=F32)


def _bdot_nt(a, b):
    return lax.dot_general(a.astype(BF16), b.astype(BF16), (((1,), (1,)), ((), ())),
                           preferred_element_type=F32)


def _bdot_tn(a, b):
    return lax.dot_general(a.astype(BF16), b.astype(BF16), (((0,), (0,)), ((), ())),
                           preferred_element_type=F32)


def _proj_kernel(x_ref, g_ref, w_ref, aux_ref, o_ref, h_ref, *, mode):
    @pl.when(pl.program_id(1) == 0)
    def _():
        x = x_ref[...]
        ms = jnp.mean(x * x, axis=-1, keepdims=True)
        h_ref[...] = (x * lax.rsqrt(ms + EPS) * g_ref[...]).astype(BF16)

    h = h_ref[...]
    lo = lax.broadcasted_iota(jnp.int32, (1, LANES), 1) < DH_DIFF
    for c in range(PROJ_TN // PROJ_CHUNK):
        cs = slice(c * PROJ_CHUNK, (c + 1) * PROJ_CHUNK)
        acc = jnp.dot(h, w_ref[:, cs], preferred_element_type=F32)
        if mode == "plain":
            o_ref[:, cs] = acc.astype(o_ref.dtype)
        elif mode == "gate":
            o_ref[:, cs] = jax.nn.sigmoid(acc + aux_ref[:, cs]).astype(o_ref.dtype)
        else:
            for g in range(PROJ_CHUNK // LANES):
                sl = slice(c * PROJ_CHUNK + g * LANES, c * PROJ_CHUNK + (g + 1) * LANES)
                y = acc[:, g * LANES:(g + 1) * LANES]
                y2 = y * y
                s_lo = jnp.sum(jnp.where(lo, y2, 0.0), axis=-1, keepdims=True)
                s_hi = jnp.sum(jnp.where(lo, 0.0, y2), axis=-1, keepdims=True)
                r = jnp.where(lo, lax.rsqrt(s_lo / DH_DIFF + EPS), lax.rsqrt(s_hi / DH_DIFF + EPS))
                o_ref[:, sl] = (y * r * aux_ref[:, sl]).astype(o_ref.dtype)


def _input_projection(x2d, g_mix, w, aux, mode):
    t, d = x2d.shape
    n = w.shape[1]
    return pl.pallas_call(
        functools.partial(_proj_kernel, mode=mode),
        grid=(t // PROJ_TM, n // PROJ_TN),
        in_specs=[
            pl.BlockSpec((PROJ_TM, d), lambda i, j: (i, 0)),
            pl.BlockSpec((1, d), lambda i, j: (0, 0)),
            pl.BlockSpec((d, PROJ_TN), lambda i, j: (0, j)),
            pl.BlockSpec((1, PROJ_TN), lambda i, j: (0, j)),
        ],
        out_specs=pl.BlockSpec((PROJ_TM, PROJ_TN), lambda i, j: (i, j)),
        out_shape=jax.ShapeDtypeStruct((t, n), BF16),
        scratch_shapes=[pltpu.VMEM((PROJ_TM, d), BF16)],
        compiler_params=_cparams(("parallel", "arbitrary")),
        name="input_projection_" + mode,
    )(x2d, g_mix, w, aux)


def _small_proj_kernel(x_ref, g_ref, w_ref, wt_ref, alog_ref, dtb_ref, alog_t_ref, dtb_t_ref,
                       o_ref, ot_ref, *, n_heads):
    x = x_ref[...]
    ms = jnp.mean(x * x, axis=-1, keepdims=True)
    h = (x * lax.rsqrt(ms + EPS) * g_ref[...]).astype(BF16)

    def finish(acc, idx, alog, dtb):
        beta = jax.nn.sigmoid(acc)
        z = acc + dtb
        softplus = jnp.maximum(z, 0.0) + jnp.log1p(jnp.exp(-jnp.abs(z)))
        gdec = -jnp.exp(alog) * softplus
        return jnp.where(idx < n_heads, beta, jnp.where(idx < 2 * n_heads, gdec, 0.0))

    acc = jnp.dot(h, w_ref[...], preferred_element_type=F32)
    lane = lax.broadcasted_iota(jnp.int32, acc.shape, 1)
    o_ref[...] = finish(acc, lane, alog_ref[...], dtb_ref[...])
    acc_t = lax.dot_general(wt_ref[...], h, (((1,), (1,)), ((), ())),
                            preferred_element_type=F32)
    sub = lax.broadcasted_iota(jnp.int32, acc_t.shape, 0)
    ot_ref[...] = finish(acc_t, sub, alog_t_ref[...], dtb_t_ref[...])


def _small_projection(x2d, g_mix, w_small, w_small_t, alog, dtb, alog_t, dtb_t, n_heads):
    t, d = x2d.shape
    rows_t = w_small_t.shape[0]
    tm = PROJ_TM
    full = lambda shape: pl.BlockSpec(shape, lambda i: (0, 0))
    return pl.pallas_call(
        functools.partial(_small_proj_kernel, n_heads=n_heads),
        grid=(t // tm,),
        in_specs=[
            pl.BlockSpec((tm, d), lambda i: (i, 0)),
            full((1, d)), full((d, LANES)), full((rows_t, d)),
            full((1, LANES)), full((1, LANES)), full((rows_t, 1)), full((rows_t, 1)),
        ],
        out_specs=[pl.BlockSpec((tm, LANES), lambda i: (i, 0)),
                   pl.BlockSpec((rows_t, tm), lambda i: (0, i))],
        out_shape=[jax.ShapeDtypeStruct((t, LANES), F32),
                   jax.ShapeDtypeStruct((rows_t, t), F32)],
        compiler_params=_cparams(("parallel",)),
        name="beta_decay_projection",
    )(x2d, g_mix, w_small, w_small_t, alog, dtb, alog_t, dtb_t)


def _gdn_kernel(q_ref, k_ref, v_ref, z_ref, sm_ref, smt_ref, cwq_ref, cwk_ref, cwv_ref, gout_ref,
                o_ref, state_ref, qp_ref, kp_ref, vp_ref, vn_ref, *, n_heads):
    hg = pl.program_id(1)
    s = pl.program_id(2)
    tb = GDN_TB
    pad = SUBLANES
    width = GDN_HG * HEAD_DIM

    @pl.when(s == 0)
    def _():
        state_ref[...] = jnp.zeros_like(state_ref)
        for p_ref in (qp_ref, kp_ref, vp_ref):
            p_ref[0:pad, :] = jnp.zeros((pad, width), F32)

    r = lax.broadcasted_iota(jnp.int32, (tb, tb), 0)
    c = lax.broadcasted_iota(jnp.int32, (tb, tb), 1)
    delay_mat = jnp.concatenate([(r - c == dd).astype(BF16) for dd in range(1, CONV_WIDTH)], axis=0)

    def conv_silu(x_ref, p_ref, cw_ref):
        x = x_ref[...]
        xf = x.astype(F32)
        p_ref[pad:2 * pad, :] = xf[0:pad]
        delayed = jnp.dot(delay_mat, x, preferred_element_type=F32)
        acc = cw_ref[CONV_WIDTH - 1:CONV_WIDTH, :] * xf
        for dd in range(1, CONV_WIDTH):
            first = p_ref[pad - dd:2 * pad - dd, :]
            xd = jnp.concatenate([first, delayed[(dd - 1) * tb + pad:dd * tb]], axis=0)
            acc = acc + cw_ref[CONV_WIDTH - 1 - dd:CONV_WIDTH - dd, :] * xd
        p_ref[0:pad, :] = xf[tb - pad:tb]
        return acc * jax.nn.sigmoid(acc)

    q_all = conv_silu(q_ref, qp_ref, cwq_ref)
    k_all = conv_silu(k_ref, kp_ref, cwk_ref)
    v_all = conv_silu(v_ref, vp_ref, cwv_ref)

    shift = int(math.log2(CHUNK))
    same = (r >> shift) == (c >> shift)
    incl = jnp.logical_and(same, c <= r)
    strict = jnp.logical_and(same, c < r)

    small = sm_ref[...]
    small_t = smt_ref[...]
    lane = lax.broadcasted_iota(jnp.int32, small.shape, 1)
    def split3(a):
        hi = a.astype(BF16)
        r1 = a - hi.astype(F32)
        mid = r1.astype(BF16)
        lo = (r1 - mid.astype(F32)).astype(BF16)
        return hi.astype(F32), mid.astype(F32), lo.astype(F32)

    part = 2 * n_heads
    s_hi, s_mid, s_lo = split3(small)
    small3 = jnp.where(lane < part, s_hi,
                       jnp.where(lane < 2 * part, pltpu.roll(s_mid, part, 1),
                                 jnp.where(lane < 3 * part, pltpu.roll(s_lo, 2 * part, 1), 0.0)))
    both = _bdot(jnp.concatenate([incl.astype(F32), same.astype(F32)], axis=0), small3)
    gcum = both[:tb]
    gtot = both[tb:]
    gcum_t = _bdot(jnp.concatenate(split3(small_t), axis=0),
                   jnp.logical_and(same, r <= c).astype(F32))
    sub3 = lax.broadcasted_iota(jnp.int32, gcum_t.shape, 0)

    heads = range(GDN_HG)
    hsl = [slice(hh * HEAD_DIM, (hh + 1) * HEAD_DIM) for hh in heads]
    qs = [q_all[:, hs] for hs in hsl]
    ks = [k_all[:, hs] for hs in hsl]
    vs = [v_all[:, hs] for hs in hsl]
    qs = [q * lax.rsqrt(jnp.sum(q * q, axis=-1, keepdims=True) + EPS) * (HEAD_DIM ** -0.5) for q in qs]
    ks = [k * lax.rsqrt(jnp.sum(k * k, axis=-1, keepdims=True) + EPS) for k in ks]

    def col_of(arr, idx):
        return jnp.sum(jnp.where(lane == idx, arr, 0.0), axis=-1, keepdims=True)

    def terms_of(pos, idx):
        return jnp.logical_or(pos == idx, jnp.logical_or(pos == idx + part, pos == idx + 2 * part))

    head_ids = [hg * GDN_HG + hh for hh in heads]
    betas = [col_of(small, hd) for hd in head_ids]
    gcs = [jnp.sum(jnp.where(terms_of(lane, hd + n_heads), gcum, 0.0), axis=-1, keepdims=True)
           for hd in head_ids]
    gls = [jnp.sum(jnp.where(terms_of(lane, hd + n_heads), gtot, 0.0), axis=-1, keepdims=True)
           for hd in head_ids]
    gc_rows = [jnp.sum(jnp.where(terms_of(sub3, hd + n_heads), gcum_t, 0.0), axis=0, keepdims=True)
               for hd in head_ids]

    decays = [jnp.where(incl, jnp.exp(jnp.minimum(gc - gr, 0.0)), 0.0) for gc, gr in zip(gcs, gc_rows)]
    kbs = [k * b for k, b in zip(ks, betas)]
    kks = [_bdot_nt(kb, k) for kb, k in zip(kbs, ks)]
    pws = [jnp.where(strict, -(kk * dc), 0.0) for kk, dc in zip(kks, decays)]
    n_chunks = tb // CHUNK
    cat_row = lax.broadcasted_iota(jnp.int32, (CHUNK, tb), 0)
    cat_lane = lax.broadcasted_iota(jnp.int32, (CHUNK, tb), 1)
    lane_chunk = cat_lane >> shift

    def block_diag(m_cat):
        return jnp.concatenate([jnp.where(lane_chunk == ci, m_cat, 0.0) for ci in range(n_chunks)], axis=0)

    def cat_of(m_bd):
        out = m_bd[0:CHUNK]
        for ci in range(1, n_chunks):
            out = out + m_bd[ci * CHUNK:(ci + 1) * CHUNK]
        return out

    pcats = [cat_of(pw) for pw in pws]
    eye_cat = ((cat_lane & (CHUNK - 1)) == cat_row).astype(F32)
    tcats = [eye_cat + pc for pc in pcats]
    pcats = [_bdot(pc, block_diag(pc)) for pc in pcats]
    n_levels = int(math.log2(CHUNK))
    for lev in range(1, n_levels):
        bds = [block_diag(pc) for pc in pcats]
        if lev < n_levels - 1:
            prods = [_bdot(jnp.concatenate([pc, tc], axis=0), bd) for pc, tc, bd in zip(pcats, tcats, bds)]
            pcats = [pr[:CHUNK] for pr in prods]
            tcats = [tc + pr[CHUNK:] for tc, pr in zip(tcats, prods)]
        else:
            tcats = [tc + _bdot(tc, bd) for tc, bd in zip(tcats, bds)]
    tmats = [block_diag(tc) for tc in tcats]
    egcs = [jnp.exp(gc) for gc in gcs]
    uws = [_bdot(tm, jnp.concatenate([v * b, kb * eg], axis=1))
           for tm, v, b, kb, eg in zip(tmats, vs, betas, kbs, egcs)]
    us = [uw[:, :HEAD_DIM] for uw in uws]
    ws = [uw[:, HEAD_DIM:] for uw in uws]
    qkm = [_bdot_nt(q, k) for q, k in zip(qs, ks)]
    qkm = [jnp.where(incl, x * dc, 0.0) for x, dc in zip(qkm, decays)]
    q_decs = [q * eg for q, eg in zip(qs, egcs)]
    k_ends = [k * jnp.exp(gl - gc) for k, gl, gc in zip(ks, gls, gcs)]

    for hh in heads:
        vn_ref[hh] = jnp.zeros((tb, HEAD_DIM), F32)
    outs = [[] for _ in heads]
    for ci in range(tb // CHUNK):
        cs = slice(ci * CHUNK, (ci + 1) * CHUNK)
        sts = [state_ref[hh] for hh in heads]
        ws_qs = [_bdot(jnp.concatenate([ws[hh][cs], q_decs[hh][cs]], axis=0), sts[hh]) for hh in heads]
        v_news = [us[hh][cs] - ws_qs[hh][:CHUNK] for hh in heads]
        for hh in heads:
            vn_ref[hh, cs, :] = v_news[hh]
        intra = [_bdot(qkm[hh][cs], vn_ref[hh]) for hh in heads]
        upd = [_bdot_tn(k_ends[hh][cs], v_news[hh]) for hh in heads]
        for hh in heads:
            outs[hh].append(ws_qs[hh][CHUNK:] + intra[hh])
            g_last = gls[hh][ci * CHUNK:ci * CHUNK + 1, :]
            state_ref[hh] = sts[hh] * jnp.exp(g_last) + upd[hh]
    for hh in heads:
        o = jnp.concatenate(outs[hh], axis=0)
        o = o * lax.rsqrt(jnp.mean(o * o, axis=-1, keepdims=True) + EPS) * gout_ref[...]
        zz = z_ref[:, hsl[hh]].astype(F32)
        o_ref[:, hsl[hh]] = (o * (zz * jax.nn.sigmoid(zz))).astype(o_ref.dtype)


def _gated_delta(big, small, small_t, conv_w, g_out, bsz, seq, n_heads, d_model):
    t = bsz * seq
    tb = GDN_TB
    ns = seq // tb
    width = GDN_HG * HEAD_DIM
    nhg = n_heads // GDN_HG
    blocks_per_group = d_model // width
    rows_t = small_t.shape[0]

    def colspec(group):
        return pl.BlockSpec((tb, width), lambda b, h, s: (b * ns + s, group * blocks_per_group + h))

    def cwspec(group):
        return pl.BlockSpec((CONV_WIDTH, width), lambda b, h, s: (0, group * blocks_per_group + h))

    return pl.pallas_call(
        functools.partial(_gdn_kernel, n_heads=n_heads),
        grid=(bsz, nhg, ns),
        in_specs=[
            colspec(0), colspec(1), colspec(2), colspec(3),
            pl.BlockSpec((tb, LANES), lambda b, h, s: (b * ns + s, 0)),
            pl.BlockSpec((rows_t, tb), lambda b, h, s: (0, b * ns + s)),
            cwspec(0), cwspec(1), cwspec(2),
            pl.BlockSpec((1, HEAD_DIM), lambda b, h, s: (0, 0)),
        ],
        out_specs=pl.BlockSpec((tb, width), lambda b, h, s: (b * ns + s, h)),
        out_shape=jax.ShapeDtypeStruct((t, d_model), BF16),
        scratch_shapes=[
            pltpu.VMEM((GDN_HG, HEAD_DIM, HEAD_DIM), F32),
            pltpu.VMEM((2 * SUBLANES, width), F32),
            pltpu.VMEM((2 * SUBLANES, width), F32),
            pltpu.VMEM((2 * SUBLANES, width), F32),
            pltpu.VMEM((GDN_HG, tb, HEAD_DIM), F32),
        ],
        compiler_params=_cparams(("parallel", "parallel", "arbitrary")),
        name="gated_delta",
    )(big, big, big, big, small, small_t, conv_w, conv_w, conv_w, g_out)


def _t5_bucket(n):
    max_exact = N_BUCKETS // 2
    nf = jnp.maximum(n, 1).astype(F32)
    large = max_exact + (jnp.log(nf / max_exact) / math.log(MAX_DISTANCE / max_exact)
                         * (N_BUCKETS - max_exact)).astype(jnp.int32)
    large = jnp.minimum(large, N_BUCKETS - 1)
    return jnp.where(n < max_exact, n, large)


def _attn_kernel(rb_ref, q_ref, k_ref, v_ref, lam_ref, gsub_ref, o_ref,
                 bias_ref, m_ref, acc_ref, sa_ref, sb_ref, *, lam_init):
    hg = pl.program_id(0)
    b = pl.program_id(1)
    qi = pl.program_id(2)
    bq, bk = ATT_BQ, ATT_BK
    heads = range(ATT_HG)
    hsl = [slice(hh * HEAD_DIM, (hh + 1) * HEAD_DIM) for hh in heads]

    @pl.when(jnp.logical_and(b == 0, qi == 0))
    def _():
        i = lax.broadcasted_iota(jnp.int32, (bq, bk), 0)
        jj = lax.broadcasted_iota(jnp.int32, (bq, bk), 1)
        for hh in heads:
            head = hg * ATT_HG + hh
            far = rb_ref[N_BUCKETS - 1, head]
            bias_ref[hh, 2] = jnp.zeros((bq, bk), F32)
            for slot in range(2):
                n = i - jj + slot * bk
                bucket = _t5_bucket(jnp.maximum(n, 0))
                bias = jnp.zeros((bq, bk), F32)
                for cc in range(N_BUCKETS):
                    bias = jnp.where(bucket == cc, rb_ref[cc, head] - far, bias)
                if slot == 0:
                    bias = jnp.where(n >= 0, bias, NEG_BIG)
                bias_ref[hh, slot] = bias

    m_ref[...] = jnp.full(m_ref.shape, NEG_BIG, F32)
    acc_ref[...] = jnp.zeros(acc_ref.shape, F32)

    lane = lax.broadcasted_iota(jnp.int32, (bq, HEAD_DIM), 1)
    qs = []
    for hs in hsl:
        q = q_ref[:, hs]
        zero = jnp.zeros_like(q)
        qs.append(jnp.concatenate([jnp.where(lane < DH_DIFF, q, zero),
                                   jnp.where(lane < DH_DIFF, zero, q)], axis=0))
    ones_col = (lax.broadcasted_iota(jnp.int32, (bk, HEAD_DIM), 1) == 0).astype(BF16)

    def scores(j, s_ref):
        ks = pl.multiple_of(j * bk, bk)
        for hh in heads:
            s_ref[hh] = lax.dot_general(qs[hh], k_ref[pl.ds(ks, bk), hsl[hh]], (((1,), (1,)), ((), ())),
                                        preferred_element_type=F32)

    def absorb(j, s_ref, biased=True):
        ks = pl.multiple_of(j * bk, bk)
        v_exts = [jnp.concatenate([v_ref[pl.ds(ks, bk), hs], ones_col], axis=1) for hs in hsl]
        if biased:
            slot = jnp.minimum(qi - j, 2)
            scs = [jnp.concatenate([s_ref[hh, 0:bq, :] + bias_ref[hh, slot],
                                    s_ref[hh, bq:2 * bq, :] + bias_ref[hh, slot]], axis=0) for hh in heads]
        else:
            scs = [s_ref[hh] for hh in heads]
        m_olds = [m_ref[hh] for hh in heads]
        m_news = [jnp.maximum(mo, jnp.max(sc, axis=-1, keepdims=True)) for mo, sc in zip(m_olds, scs)]
        ps = [jnp.exp(sc - mn) for sc, mn in zip(scs, m_news)]
        pvs = [jnp.dot(p.astype(BF16), ve, preferred_element_type=F32) for p, ve in zip(ps, v_exts)]
        for hh in heads:
            acc_ref[hh] = jnp.exp(m_olds[hh] - m_news[hh]) * acc_ref[hh] + pvs[hh]
            m_ref[hh] = m_news[hh]

    n_tiles = qi + 1
    scores(0, sa_ref)

    def pair_body(jj, carry, biased):
        j0 = 2 * jj
        scores(j0 + 1, sb_ref)
        absorb(j0, sa_ref, biased)
        scores(jnp.minimum(j0 + 2, qi), sa_ref)
        absorb(j0 + 1, sb_ref, biased)
        return carry

    n_far_pairs = jnp.maximum(qi - 1, 0) // 2
    lax.fori_loop(0, n_far_pairs, functools.partial(pair_body, biased=False), 0)
    lax.fori_loop(n_far_pairs, n_tiles // 2, functools.partial(pair_body, biased=True), 0)

    @pl.when(n_tiles % 2 == 1)
    def _():
        absorb(qi, sa_ref)

    lam_p = lam_ref[...]
    s1 = jnp.sum(lam_p[0:1] * lam_p[1:2], axis=-1, keepdims=True)
    s2 = jnp.sum(lam_p[2:3] * lam_p[3:4], axis=-1, keepdims=True)
    lam = jnp.exp(s1) - jnp.exp(s2) + lam_init
    for hh in heads:
        acc = acc_ref[hh]
        num = acc[:, :HEAD_DIM]
        den = acc[:, HEAD_DIM:HEAD_DIM + 1]
        o = num[:bq] / den[:bq] - lam * (num[bq:] / den[bq:])
        o = o * lax.rsqrt(jnp.mean(o * o, axis=-1, keepdims=True) + EPS) * gsub_ref[...]
        o_ref[:, hsl[hh]] = (o * (1.0 - lam_init)).astype(o_ref.dtype)


def _diff_attention(proj_qk, proj_plain, rel_bias, lam_params, g_subln, bsz, seq, n_heads, d_model,
                    lam_init):
    t = bsz * seq
    nq = seq // ATT_BQ
    width = ATT_HG * HEAD_DIM
    per = d_model // width
    vcol = 4 * per
    return pl.pallas_call(
        functools.partial(_attn_kernel, lam_init=lam_init),
        grid=(n_heads // ATT_HG, bsz, nq),
        in_specs=[
            pl.BlockSpec(memory_space=pltpu.SMEM),
            pl.BlockSpec((ATT_BQ, width), lambda h, b, i: (b * nq + i, h)),
            pl.BlockSpec((seq, width), lambda h, b, i: (b, per + h)),
            pl.BlockSpec((seq, width), lambda h, b, i: (b, vcol + h)),
            pl.BlockSpec((4, DH_DIFF), lambda h, b, i: (0, 0)),
            pl.BlockSpec((1, HEAD_DIM), lambda h, b, i: (0, 0)),
        ],
        out_specs=pl.BlockSpec((ATT_BQ, width), lambda h, b, i: (b * nq + i, h)),
        out_shape=jax.ShapeDtypeStruct((t, d_model), BF16),
        scratch_shapes=[
            pltpu.VMEM((ATT_HG, 3, ATT_BQ, ATT_BK), F32),
            pltpu.VMEM((ATT_HG, 2 * ATT_BQ, 1), F32),
            pltpu.VMEM((ATT_HG, 2 * ATT_BQ, 2 * HEAD_DIM), F32),
            pltpu.VMEM((ATT_HG, 2 * ATT_BQ, ATT_BK), F32),
            pltpu.VMEM((ATT_HG, 2 * ATT_BQ, ATT_BK), F32),
        ],
        compiler_params=_cparams(("arbitrary", "arbitrary", "arbitrary")),
        name="diff_attention",
    )(rel_bias, proj_qk, proj_qk, proj_plain, lam_params, g_subln)


def _mix_kernel(ga_ref, gb_ref, oa_ref, od_ref, x_ref, wo_ref, gffn_ref, wr_ref, br_ref,
                x1_ref, h2_ref, topi_ref, topw_ref, rank_ref, cnt_ref, carry_ref):
    i = pl.program_id(0)
    tm = MIX_TM

    @pl.when(i == 0)
    def _():
        carry_ref[...] = jnp.zeros_like(carry_ref)

    mix = (ga_ref[...].astype(F32) * oa_ref[...].astype(F32)
           + gb_ref[...].astype(F32) * od_ref[...].astype(F32))
    x1 = x_ref[...] + jnp.dot(mix.astype(BF16), wo_ref[...], preferred_element_type=F32)
    x1_ref[...] = x1
    h2 = x1 * lax.rsqrt(jnp.mean(x1 * x1, axis=-1, keepdims=True) + EPS) * gffn_ref[...]
    h2_ref[...] = _pack_halves(h2)

    logits = lax.dot_general(wr_ref[...], h2, (((1,), (1,)), ((), ())),
                             preferred_element_type=F32, precision=lax.Precision.HIGHEST) + br_ref[...]
    eidx = lax.broadcasted_iota(jnp.int32, logits.shape, 0).astype(F32)
    vals, hots = [], []
    cur = logits
    for kk in range(TOP_K):
        mx = jnp.max(cur, axis=0, keepdims=True)
        idx = jnp.min(jnp.where(cur == mx, eidx, float(N_EXPERTS)), axis=0, keepdims=True)
        hot = eidx == idx
        vals.append(mx)
        hots.append(hot)
        topi_ref[kk:kk + 1, :] = idx.astype(jnp.int32)
        cur = jnp.where(hot, -jnp.inf, cur)
    exps = [jnp.exp(vv - vals[0]) for vv in vals]
    denom = exps[0] + exps[1] + exps[2] + exps[3]
    for kk in range(TOP_K):
        topw_ref[kk:kk + 1, :] = exps[kk] / denom

    sel = hots[0]
    for kk in range(1, TOP_K):
        sel = jnp.logical_or(sel, hots[kk])
    sel_f = sel.astype(F32)
    r = lax.broadcasted_iota(jnp.int32, (tm, tm), 0)
    c = lax.broadcasted_iota(jnp.int32, (tm, tm), 1)
    before = _bdot(sel_f, (r < c).astype(F32)) + carry_ref[...]
    for kk in range(TOP_K):
        rank_ref[kk:kk + 1, :] = jnp.sum(jnp.where(hots[kk], before, 0.0), axis=0,
                                         keepdims=True).astype(jnp.int32)
    carry_ref[...] = carry_ref[...] + jnp.sum(sel_f, axis=-1, keepdims=True)
    cnt_ref[...] = carry_ref[...].astype(jnp.int32)


def _mix_project_route(proj_gate, oa, od, x2d, w_o, g_ffn, w_r_t, b_r, d_model):
    t = x2d.shape[0]
    tm = MIX_TM
    full = lambda shape: pl.BlockSpec(shape, lambda i: (0, 0))
    row = lambda: pl.BlockSpec((tm, d_model), lambda i: (i, 0))
    krow = lambda: pl.BlockSpec((TOP_K, tm), lambda i: (0, i))
    return pl.pallas_call(
        _mix_kernel,
        grid=(t // tm,),
        in_specs=[
            pl.BlockSpec((tm, d_model), lambda i: (i, 0)),
            pl.BlockSpec((tm, d_model), lambda i: (i, 1)),
            row(), row(), row(),
            full((d_model, d_model)), full((1, d_model)), full((N_EXPERTS, d_model)), full((N_EXPERTS, 1)),
        ],
        out_specs=[row(), pl.BlockSpec((tm, d_model // 2), lambda i: (i, 0)),
                   krow(), krow(), krow(), full((N_EXPERTS, 1))],
        out_shape=[
            jax.ShapeDtypeStruct((t, d_model), F32),
            jax.ShapeDtypeStruct((t, d_model // 2), jnp.int32),
            jax.ShapeDtypeStruct((TOP_K, t), jnp.int32),
            jax.ShapeDtypeStruct((TOP_K, t), F32),
            jax.ShapeDtypeStruct((TOP_K, t), jnp.int32),
            jax.ShapeDtypeStruct((N_EXPERTS, 1), jnp.int32),
        ],
        scratch_shapes=[pltpu.VMEM((N_EXPERTS, 1), F32)],
        compiler_params=_cparams(("arbitrary",)),
        name="merge_outproj_route",
    )(proj_gate, proj_gate, oa, od, x2d, w_o, g_ffn, w_r_t, b_r)


def _pack_halves(x):
    half = x.shape[1] // 2
    bits = pltpu.bitcast(x.astype(BF16).astype(F32), jnp.int32)
    return bits[:, :half] | lax.shift_right_logical(bits[:, half:], 16)


def _unpack_halves(p):
    hi = pltpu.bitcast(p & jnp.int32(-65536), F32)
    lo = pltpu.bitcast(lax.shift_left(p, 16), F32)
    return jnp.concatenate([hi, lo], axis=1)


def _expert_kernel(be_ref, nu_ref, x_ref, wup_ref, bup_ref, wdn_ref, bdn_ref, y_ref, wup_bf, wdn_bf):
    i = pl.program_id(0)
    d_ff = wdn_ref.shape[1]

    @pl.when(jnp.logical_or(i == 0, be_ref[i] != be_ref[jnp.maximum(i - 1, 0)]))
    def _():
        rr = lax.broadcasted_iota(jnp.int32, (2 * LANES, 2 * LANES), 0)
        cc = lax.broadcasted_iota(jnp.int32, (2 * LANES, 2 * LANES), 1)
        pick = jnp.where(cc < LANES, 2 * cc, 2 * (cc - LANES) + 1)
        perm = (rr == pick).astype(BF16)
        for g in range(wup_ref.shape[2] // (2 * LANES)):
            cs = slice(g * 2 * LANES, (g + 1) * 2 * LANES)
            wup_bf[:, cs] = jnp.dot(wup_ref[0, :, cs].astype(BF16), perm,
                                    preferred_element_type=F32).astype(BF16)
        wdn_bf[...] = wdn_ref[0].astype(BF16)

    @pl.when(i < nu_ref[0])
    def _():
        x = _unpack_halves(x_ref[...])
        hid = jnp.dot(x.astype(BF16), wup_bf[...], preferred_element_type=F32) + bup_ref[0]
        acts = []
        for g in range(hid.shape[1] // (2 * LANES)):
            glu = jnp.minimum(hid[:, g * 2 * LANES:g * 2 * LANES + LANES], SWIGLU_LIMIT)
            lin = jnp.clip(hid[:, g * 2 * LANES + LANES:(g + 1) * 2 * LANES], -SWIGLU_LIMIT, SWIGLU_LIMIT)
            acts.append(glu * jax.nn.sigmoid(SWIGLU_ALPHA * glu) * (lin + 1.0))
        act = jnp.concatenate(acts, axis=1)
        assert act.shape[1] == d_ff
        y = jnp.dot(act.astype(BF16), wdn_bf[...], preferred_element_type=F32) + bdn_ref[0]
        y_ref[...] = _pack_halves(y)

    @pl.when(i >= nu_ref[0])
    def _():
        y_ref[...] = jnp.zeros(y_ref.shape, y_ref.dtype)


def _experts(block_e, n_used, xs, w_up, b_up, w_down, b_down):
    n_rows, half = xs.shape
    d = w_up.shape[1]
    nb = n_rows // MOE_RB
    two_ff = w_up.shape[2]
    d_ff = w_down.shape[1]
    grid_spec = pltpu.PrefetchScalarGridSpec(
        num_scalar_prefetch=2,
        grid=(nb,),
        in_specs=[
            pl.BlockSpec((MOE_RB, half), lambda i, be, nu: (jnp.minimum(i, nu[0] - 1), 0)),
            pl.BlockSpec((1, d, two_ff), lambda i, be, nu: (be[i], 0, 0)),
            pl.BlockSpec((1, 1, two_ff), lambda i, be, nu: (be[i], 0, 0)),
            pl.BlockSpec((1, d_ff, d), lambda i, be, nu: (be[i], 0, 0)),
            pl.BlockSpec((1, 1, d), lambda i, be, nu: (be[i], 0, 0)),
        ],
        out_specs=pl.BlockSpec((MOE_RB, half), lambda i, be, nu: (i, 0)),
        scratch_shapes=[pltpu.VMEM((d, two_ff), BF16), pltpu.VMEM((d_ff, d), BF16)],
    )
    return pl.pallas_call(
        _expert_kernel,
        grid_spec=grid_spec,
        out_shape=jax.ShapeDtypeStruct((n_rows, half), jnp.int32),
        compiler_params=_cparams(("arbitrary",)),
        name="moe_experts",
    )(block_e, n_used, xs, w_up, b_up, w_down, b_down)


def _sc_invert_slots(dest_flat, n_rows):
    n_assign = dest_flat.shape[0]
    n_workers = SC_CORES * SC_SUBCORES
    rows_per_w = n_rows // n_workers
    chunk = SC_SCAN_CHUNK
    assert n_rows % n_workers == 0 and rows_per_w % SC_LANES == 0 and n_assign % chunk == 0
    mesh = plsc.VectorSubcoreMesh(core_axis_name="c", subcore_axis_name="s",
                                  num_cores=SC_CORES, num_subcores=SC_SUBCORES)

    def body(dest_hbm, out_hbm, dest_v, map_v):
        wid = lax.axis_index("s") * SC_CORES + lax.axis_index("c")
        base = wid * rows_per_w
        lanes = lax.broadcasted_iota(jnp.int32, (SC_LANES,), 0)

        @pl.loop(0, rows_per_w, step=SC_LANES)
        def _(r0):
            map_v[pl.ds(r0, SC_LANES)] = jnp.full((SC_LANES,), -1, jnp.int32)

        @pl.loop(0, n_assign // chunk)
        def _(ci):
            pltpu.sync_copy(dest_hbm.at[pl.ds(ci * chunk, chunk)], dest_v)

            @pl.loop(0, chunk, step=SC_LANES)
            def _(j):
                local = dest_v[pl.ds(j, SC_LANES)] - base
                mine = jnp.logical_and(local >= 0, local < rows_per_w)
                plsc.store_scatter(map_v, [jnp.where(mine, local, 0)], ci * chunk + j + lanes, mask=mine)

        pltpu.sync_copy(map_v, out_hbm.at[pl.ds(base, rows_per_w)])

    return pl.kernel(
        body,
        out_type=jax.ShapeDtypeStruct((n_rows,), jnp.int32),
        mesh=mesh,
        scratch_types=[pltpu.VMEM((chunk,), jnp.int32), pltpu.VMEM((rows_per_w,), jnp.int32)],
        compiler_params=pltpu.CompilerParams(needs_layout_passes=False),
        name="moe_slot_inverse",
    )(dest_flat)


def _sc_gather_rows(table, idx):
    n_idx = idx.shape[0]
    d = table.shape[1]
    n_workers = SC_CORES * SC_SUBCORES
    per_worker = n_idx // n_workers
    n_chunks = per_worker // SC_GATHER_ROWS
    assert n_idx % n_workers == 0 and per_worker % SC_GATHER_ROWS == 0
    mesh = plsc.VectorSubcoreMesh(core_axis_name="c", subcore_axis_name="s",
                                  num_cores=SC_CORES, num_subcores=SC_SUBCORES)

    assert n_chunks % 2 == 0

    def body(table_hbm, idx_hbm, out_hbm, idx_v, rows_a, rows_b, sem_a, sem_b):
        wid = lax.axis_index("s") * SC_CORES + lax.axis_index("c")
        base = wid * per_worker
        pltpu.sync_copy(idx_hbm.at[pl.ds(base, per_worker)], idx_v)

        def gather(ci, rows_v, sem):
            off = pl.multiple_of(ci * SC_GATHER_ROWS, SC_GATHER_ROWS)
            return pltpu.make_async_copy(table_hbm.at[idx_v.at[pl.ds(off, SC_GATHER_ROWS)]], rows_v, sem)

        def put(ci, rows_v):
            off = pl.multiple_of(ci * SC_GATHER_ROWS, SC_GATHER_ROWS)
            pltpu.sync_copy(rows_v, out_hbm.at[pl.ds(base + off, SC_GATHER_ROWS)])

        gather(0, rows_a, sem_a).start()

        @pl.loop(0, n_chunks, step=2)
        def _(ci):
            gather(ci + 1, rows_b, sem_b).start()
            gather(ci, rows_a, sem_a).wait()
            put(ci, rows_a)
            nxt = jnp.minimum(ci + 2, n_chunks - 1)
            gather(nxt, rows_a, sem_a).start()
            gather(ci + 1, rows_b, sem_b).wait()
            put(ci + 1, rows_b)

        gather(n_chunks - 1, rows_a, sem_a).wait()

    return pl.kernel(
        body,
        out_type=jax.ShapeDtypeStruct((n_idx, d), table.dtype),
        mesh=mesh,
        scratch_types=[
            pltpu.VMEM((per_worker,), jnp.int32),
            pltpu.VMEM((SC_GATHER_ROWS, d), table.dtype),
            pltpu.VMEM((SC_GATHER_ROWS, d), table.dtype),
            pltpu.SemaphoreType.DMA,
            pltpu.SemaphoreType.DMA,
        ],
        name="moe_slot_gather",
    )(table, idx)


def _combine_kernel(x1_ref, w_ref, y0_ref, y1_ref, y2_ref, y3_ref, o_ref):
    w = w_ref[...]
    out = x1_ref[...]
    for kk, y_ref in enumerate((y0_ref, y1_ref, y2_ref, y3_ref)):
        out = out + w[:, kk:kk + 1] * _unpack_halves(y_ref[...])
    o_ref[...] = out


def _combine(x1, w_tok, y_slots):
    t, d = x1.shape
    tc = COMB_TC
    nt = t // tc
    yspec = lambda kk: pl.BlockSpec((tc, d // 2), lambda i: (kk * nt + i, 0))
    return pl.pallas_call(
        _combine_kernel,
        grid=(nt,),
        in_specs=[
            pl.BlockSpec((tc, d), lambda i: (i, 0)),
            pl.BlockSpec((tc, TOP_K), lambda i: (i, 0)),
            yspec(0), yspec(1), yspec(2), yspec(3),
        ],
        out_specs=pl.BlockSpec((tc, d), lambda i: (i, 0)),
        out_shape=jax.ShapeDtypeStruct((t, d), F32),
        compiler_params=_cparams(("parallel",)),
        name="moe_combine",
    )(x1, w_tok, y_slots, y_slots, y_slots, y_slots)


def _moe(x1, h2, topi, topw, rank, counts, w_up, b_up, w_down, b_down):
    t, d = x1.shape
    n_assign = t * TOP_K
    nb = -(-n_assign // MOE_RB) + N_EXPERTS
    n_rows = nb * MOE_RB
    counts = counts[:, 0]
    padded = (counts + MOE_RB - 1) // MOE_RB * MOE_RB
    padded_end = jnp.cumsum(padded)
    padded_start = padded_end - padded
    expert_ids = jnp.arange(N_EXPERTS, dtype=jnp.int32)[:, None, None]
    start_of = jnp.sum(jnp.where(topi[None] == expert_ids, padded_start[:, None, None], 0), axis=0)
    dest = (start_of + rank).astype(jnp.int32)
    n_used = (padded_end[-1] // MOE_RB).astype(jnp.int32)
    blk = jnp.minimum(jnp.arange(nb, dtype=jnp.int32), n_used - 1)
    block_e = jnp.minimum(jnp.sum(padded_end[None, :] <= (blk * MOE_RB)[:, None], axis=1),
                          N_EXPERTS - 1).astype(jnp.int32)
    slot_of = _sc_invert_slots(dest.reshape(-1), n_rows)
    src_tok = jnp.where(slot_of < 0, jnp.arange(n_rows, dtype=jnp.int32), slot_of) % t

    xs = _sc_gather_rows(h2, src_tok)
    y_rows = _experts(block_e, n_used.reshape(1), xs, w_up, b_up, w_down, b_down)
    y_slots = _sc_gather_rows(y_rows, dest.reshape(-1))
    return _combine(x1, topw.T, y_slots)


def kernel(x, g_mix, w_in, b_gate, conv_w, a_log, dt_bias, g_delta_out, q_norm, k_norm, lambda_q1, lambda_k1, lambda_q2, lambda_k2, g_subln, rel_bias, w_o, g_ffn, w_router, b_router, w_up, b_up, w_down, b_down):
    bsz, seq, d = x.shape
    depth = g_mix.shape[0]
    n_heads = d // HEAD_DIM
    t = bsz * seq
    d_ff = w_down.shape[2]
    assert d % PROJ_TN == 0 and t % PROJ_TM == 0 and seq % GDN_TB == 0 and seq % ATT_BQ == 0
    assert t % MIX_TM == 0 and t % COMB_TC == 0 and n_heads % GDN_HG == 0
    assert (t * TOP_K) % MOE_RB == 0
    assert 2 * n_heads <= 2 * SUBLANES

    x2d = x.reshape(t, d)
    for l in range(depth):
        wl = w_in[l]
        c0 = 4 * d
        c1 = c0 + 2 * n_heads
        c2 = c1 + 2 * d
        c3 = c2 + d
        w_small = jnp.pad(wl[:, c0:c1], ((0, 0), (0, LANES - 2 * n_heads)))
        gm = g_mix[l].reshape(1, d)
        w_plain = jnp.concatenate([wl[:, :c0], wl[:, c2:c3]], axis=1).astype(BF16)
        proj_plain = _input_projection(x2d, gm, w_plain, jnp.zeros((1, 5 * d), F32), "plain")
        qk_gain = jnp.concatenate([jnp.tile(q_norm[l] * (DH_DIFF ** -0.5), 2 * n_heads),
                                   jnp.tile(k_norm[l], 2 * n_heads)]).reshape(1, 2 * d)
        proj_qk = _input_projection(x2d, gm, wl[:, c1:c2].astype(BF16), qk_gain, "qknorm")
        proj_gate = _input_projection(x2d, gm, wl[:, c3:].astype(BF16), b_gate[l].reshape(1, 2 * d), "gate")

        head_pad = jnp.zeros((LANES - 2 * n_heads,), F32)
        alog = jnp.concatenate([jnp.zeros((n_heads,), F32), a_log[l], head_pad])
        dtb = jnp.concatenate([jnp.zeros((n_heads,), F32), dt_bias[l], head_pad])
        rows_t = 2 * n_heads
        small, small_t = _small_projection(
            x2d, g_mix[l].reshape(1, d), w_small.astype(BF16), w_small[:, :rows_t].T.astype(BF16),
            alog.reshape(1, LANES), dtb.reshape(1, LANES),
            alog[:rows_t].reshape(rows_t, 1), dtb[:rows_t].reshape(rows_t, 1), n_heads)

        oa = _gated_delta(proj_plain, small, small_t, conv_w[l], g_delta_out[l].reshape(1, HEAD_DIM),
                          bsz, seq, n_heads, d)

        lam_init = 0.8 - 0.6 * math.exp(-0.3 * l)
        lam_params = jnp.stack([lambda_q1[l], lambda_k1[l], lambda_q2[l], lambda_k2[l]])
        od = _diff_attention(proj_qk, proj_plain, rel_bias, lam_params, g_subln[l].reshape(1, HEAD_DIM),
                             bsz, seq, n_heads, d, lam_init)

        x1, h2, topi, topw, rank, counts = _mix_project_route(
            proj_gate, oa, od, x2d, w_o[l].astype(BF16), g_ffn[l].reshape(1, d),
            w_router[l].T, b_router[l].reshape(N_EXPERTS, 1), d)

        b_up_l = b_up[l].reshape(N_EXPERTS, 2 * d_ff // (2 * LANES), LANES, 2)
        b_up_l = jnp.swapaxes(b_up_l, 2, 3).reshape(N_EXPERTS, 1, 2 * d_ff)
        x2d = _moe(x1, h2, topi, topw, rank, counts, w_up[l], b_up_l,
                   w_down[l], b_down[l].reshape(N_EXPERTS, 1, d))
    return x2d.reshape(bsz, seq, d)
```

```python
import functools
import math

import jax
import jax.numpy as jnp
from jax import lax
from jax.experimental import pallas as pl
from jax.experimental.pallas import tpu as pltpu
from jax.experimental.pallas import tpu_sc as plsc

F32 = jnp.float32
BF16 = jnp.bfloat16

HEAD_DIM = 128
DH_DIFF = HEAD_DIM // 2
CONV_WIDTH = 4
CHUNK = 64
N_BUCKETS = 32
MAX_DISTANCE = 128
N_EXPERTS = 32
TOP_K = 4
SWIGLU_LIMIT = 7.0
SWIGLU_ALPHA = 1.702
EPS = 1e-6
NEG_BIG = -1e30

LANES = 128
SUBLANES = 8
VMEM_LIMIT = 56 * 1024 * 1024
SC_CORES = 2
SC_SUBCORES = 16
SC_LANES = 16
SC_GATHER_ROWS = 64
SC_SCAN_CHUNK = 4096

PROJ_TM = 2048
PROJ_TN = 1024
PROJ_CHUNK = 256
GDN_TB = 256
GDN_HG = 8
ATT_HG = 1
ATT_BQ = 512
ATT_BK = 512
MIX_TM = 512
MOE_RB = 512
COMB_TC = 512


def _cparams(sem):
    return pltpu.CompilerParams(dimension_semantics=sem, vmem_limit_bytes=VMEM_LIMIT)


def _bdot(a, b):
    return jnp.dot(a.astype(BF16), b.astype(BF16), preferred_element_type=F32)


def _bdot_nt(a, b):
    return lax.dot_general(a.astype(BF16), b.astype(BF16), (((1,), (1,)), ((), ())),
                           preferred_element_type=F32)


def _bdot_tn(a, b):
    return lax.dot_general(a.astype(BF16), b.astype(BF16), (((0,), (0,)), ((), ())),
                           preferred_element_type=F32)


def _proj_kernel(x_ref, g_ref, w_ref, aux_ref, o_ref, h_ref, *, mode):
    @pl.when(pl.program_id(1) == 0)
    def _():
        x = x_ref[...]
        ms = jnp.mean(x * x, axis=-1, keepdims=True)
        h_ref[...] = (x * lax.rsqrt(ms + EPS) * g_ref[...]).astype(BF16)

    h = h_ref[...]
    lo = lax.broadcasted_iota(jnp.int32, (1, LANES), 1) < DH_DIFF
    for c in range(PROJ_TN // PROJ_CHUNK):
        cs = slice(c * PROJ_CHUNK, (c + 1) * PROJ_CHUNK)
        acc = jnp.dot(h, w_ref[:, cs], preferred_element_type=F32)
        if mode == "plain":
            o_ref[:, cs] = acc.astype(o_ref.dtype)
        elif mode == "gate":
            o_ref[:, cs] = jax.nn.sigmoid(acc + aux_ref[:, cs]).astype(o_ref.dtype)
        else:
            for g in range(PROJ_CHUNK // LANES):
                sl = slice(c * PROJ_CHUNK + g * LANES, c * PROJ_CHUNK + (g + 1) * LANES)
                y = acc[:, g * LANES:(g + 1) * LANES]
                y2 = y * y
                s_lo = jnp.sum(jnp.where(lo, y2, 0.0), axis=-1, keepdims=True)
                s_hi = jnp.sum(jnp.where(lo, 0.0, y2), axis=-1, keepdims=True)
                r = jnp.where(lo, lax.rsqrt(s_lo / DH_DIFF + EPS), lax.rsqrt(s_hi / DH_DIFF + EPS))
                o_ref[:, sl] = (y * r * aux_ref[:, sl]).astype(o_ref.dtype)


def _input_projection(x2d, g_mix, w, aux, mode):
    t, d = x2d.shape
    n = w.shape[1]
    return pl.pallas_call(
        functools.partial(_proj_kernel, mode=mode),
        grid=(t // PROJ_TM, n // PROJ_TN),
        in_specs=[
            pl.BlockSpec((PROJ_TM, d), lambda i, j: (i, 0)),
            pl.BlockSpec((1, d), lambda i, j: (0, 0)),
            pl.BlockSpec((d, PROJ_TN), lambda i, j: (0, j)),
            pl.BlockSpec((1, PROJ_TN), lambda i, j: (0, j)),
        ],
        out_specs=pl.BlockSpec((PROJ_TM, PROJ_TN), lambda i, j: (i, j)),
        out_shape=jax.ShapeDtypeStruct((t, n), BF16),
        scratch_shapes=[pltpu.VMEM((PROJ_TM, d), BF16)],
        compiler_params=_cparams(("parallel", "arbitrary")),
        name="input_projection_" + mode,
    )(x2d, g_mix, w, aux)


def _small_proj_kernel(x_ref, g_ref, w_ref, wt_ref, alog_ref, dtb_ref, alog_t_ref, dtb_t_ref,
                       o_ref, ot_ref, *, n_heads):
    x = x_ref[...]
    ms = jnp.mean(x * x, axis=-1, keepdims=True)
    h = (x * lax.rsqrt(ms + EPS) * g_ref[...]).astype(BF16)

    def finish(acc, idx, alog, dtb):
        beta = jax.nn.sigmoid(acc)
        z = acc + dtb
        softplus = jnp.maximum(z, 0.0) + jnp.log1p(jnp.exp(-jnp.abs(z)))
        gdec = -jnp.exp(alog) * softplus
        return jnp.where(idx < n_heads, beta, jnp.where(idx < 2 * n_heads, gdec, 0.0))

    acc = jnp.dot(h, w_ref[...], preferred_element_type=F32)
    lane = lax.broadcasted_iota(jnp.int32, acc.shape, 1)
    o_ref[...] = finish(acc, lane, alog_ref[...], dtb_ref[...])
    acc_t = lax.dot_general(wt_ref[...], h, (((1,), (1,)), ((), ())),
                            preferred_element_type=F32)
    sub = lax.broadcasted_iota(jnp.int32, acc_t.shape, 0)
    ot_ref[...] = finish(acc_t, sub, alog_t_ref[...], dtb_t_ref[...])


def _small_projection(x2d, g_mix, w_small, w_small_t, alog, dtb, alog_t, dtb_t, n_heads):
    t, d = x2d.shape
    rows_t = w_small_t.shape[0]
    tm = PROJ_TM
    full = lambda shape: pl.BlockSpec(shape, lambda i: (0, 0))
    return pl.pallas_call(
        functools.partial(_small_proj_kernel, n_heads=n_heads),
        grid=(t // tm,),
        in_specs=[
            pl.BlockSpec((tm, d), lambda i: (i, 0)),
            full((1, d)), full((d, LANES)), full((rows_t, d)),
            full((1, LANES)), full((1, LANES)), full((rows_t, 1)), full((rows_t, 1)),
        ],
        out_specs=[pl.BlockSpec((tm, LANES), lambda i: (i, 0)),
                   pl.BlockSpec((rows_t, tm), lambda i: (0, i))],
        out_shape=[jax.ShapeDtypeStruct((t, LANES), F32),
                   jax.ShapeDtypeStruct((rows_t, t), F32)],
        compiler_params=_cparams(("parallel",)),
        name="beta_decay_projection",
    )(x2d, g_mix, w_small, w_small_t, alog, dtb, alog_t, dtb_t)


def _gdn_kernel(q_ref, k_ref, v_ref, z_ref, sm_ref, smt_ref, cwq_ref, cwk_ref, cwv_ref, gout_ref,
                o_ref, state_ref, qp_ref, kp_ref, vp_ref, vn_ref, *, n_heads):
    hg = pl.program_id(1)
    s = pl.program_id(2)
    tb = GDN_TB
    pad = SUBLANES
    width = GDN_HG * HEAD_DIM

    @pl.when(s == 0)
    def _():
        state_ref[...] = jnp.zeros_like(state_ref)
        for p_ref in (qp_ref, kp_ref, vp_ref):
            p_ref[0:pad, :] = jnp.zeros((pad, width), F32)

    r = lax.broadcasted_iota(jnp.int32, (tb, tb), 0)
    c = lax.broadcasted_iota(jnp.int32, (tb, tb), 1)
    delay_mat = jnp.concatenate([(r - c == dd).astype(BF16) for dd in range(1, CONV_WIDTH)], axis=0)

    def conv_silu(x_ref, p_ref, cw_ref):
        x = x_ref[...]
        xf = x.astype(F32)
        p_ref[pad:2 * pad, :] = xf[0:pad]
        delayed = jnp.dot(delay_mat, x, preferred_element_type=F32)
        acc = cw_ref[CONV_WIDTH - 1:CONV_WIDTH, :] * xf
        for dd in range(1, CONV_WIDTH):
            first = p_ref[pad - dd:2 * pad - dd, :]
            xd = jnp.concatenate([first, delayed[(dd - 1) * tb + pad:dd * tb]], axis=0)
            acc = acc + cw_ref[CONV_WIDTH - 1 - dd:CONV_WIDTH - dd, :] * xd
        p_ref[0:pad, :] = xf[tb - pad:tb]
        return acc * jax.nn.sigmoid(acc)

    q_all = conv_silu(q_ref, qp_ref, cwq_ref)
    k_all = conv_silu(k_ref, kp_ref, cwk_ref)
    v_all = conv_silu(v_ref, vp_ref, cwv_ref)

    shift = int(math.log2(CHUNK))
    same = (r >> shift) == (c >> shift)
    incl = jnp.logical_and(same, c <= r)
    strict = jnp.logical_and(same, c < r)

    small = sm_ref[...]
    small_t = smt_ref[...]
    lane = lax.broadcasted_iota(jnp.int32, small.shape, 1)
    def split3(a):
        hi = a.astype(BF16)
        r1 = a - hi.astype(F32)
        mid = r1.astype(BF16)
        lo = (r1 - mid.astype(F32)).astype(BF16)
        return hi.astype(F32), mid.astype(F32), lo.astype(F32)

    part = 2 * n_heads
    s_hi, s_mid, s_lo = split3(small)
    small3 = jnp.where(lane < part, s_hi,
                       jnp.where(lane < 2 * part, pltpu.roll(s_mid, part, 1),
                                 jnp.where(lane < 3 * part, pltpu.roll(s_lo, 2 * part, 1), 0.0)))
    both = _bdot(jnp.concatenate([incl.astype(F32), same.astype(F32)], axis=0), small3)
    gcum = both[:tb]
    gtot = both[tb:]
    gcum_t = _bdot(jnp.concatenate(split3(small_t), axis=0),
                   jnp.logical_and(same, r <= c).astype(F32))
    sub3 = lax.broadcasted_iota(jnp.int32, gcum_t.shape, 0)

    heads = range(GDN_HG)
    hsl = [slice(hh * HEAD_DIM, (hh + 1) * HEAD_DIM) for hh in heads]
    qs = [q_all[:, hs] for hs in hsl]
    ks = [k_all[:, hs] for hs in hsl]
    vs = [v_all[:, hs] for hs in hsl]
    qs = [q * lax.rsqrt(jnp.sum(q * q, axis=-1, keepdims=True) + EPS) * (HEAD_DIM ** -0.5) for q in qs]
    ks = [k * lax.rsqrt(jnp.sum(k * k, axis=-1, keepdims=True) + EPS) for k in ks]

    def col_of(arr, idx):
        return jnp.sum(jnp.where(lane == idx, arr, 0.0), axis=-1, keepdims=True)

    def terms_of(pos, idx):
        return jnp.logical_or(pos == idx, jnp.logical_or(pos == idx + part, pos == idx + 2 * part))

    head_ids = [hg * GDN_HG + hh for hh in heads]
    betas = [col_of(small, hd) for hd in head_ids]
    gcs = [jnp.sum(jnp.where(terms_of(lane, hd + n_heads), gcum, 0.0), axis=-1, keepdims=True)
           for hd in head_ids]
    gls = [jnp.sum(jnp.where(terms_of(lane, hd + n_heads), gtot, 0.0), axis=-1, keepdims=True)
           for hd in head_ids]
    gc_rows = [jnp.sum(jnp.where(terms_of(sub3, hd + n_heads), gcum_t, 0.0), axis=0, keepdims=True)
               for hd in head_ids]

    decays = [jnp.where(incl, jnp.exp(jnp.minimum(gc - gr, 0.0)), 0.0) for gc, gr in zip(gcs, gc_rows)]
    kbs = [k * b for k, b in zip(ks, betas)]
    kks = [_bdot_nt(kb, k) for kb, k in zip(kbs, ks)]
    pws = [jnp.where(strict, -(kk * dc), 0.0) for kk, dc in zip(kks, decays)]
    n_chunks = tb // CHUNK
    cat_row = lax.broadcasted_iota(jnp.int32, (CHUNK, tb), 0)
    cat_lane = lax.broadcasted_iota(jnp.int32, (CHUNK, tb), 1)
    lane_chunk = cat_lane >> shift

    def block_diag(m_cat):
        return jnp.concatenate([jnp.where(lane_chunk == ci, m_cat, 0.0) for ci in range(n_chunks)], axis=0)

    def cat_of(m_bd):
        out = m_bd[0:CHUNK]
        for ci in range(1, n_chunks):
            out = out + m_bd[ci * CHUNK:(ci + 1) * CHUNK]
        return out

    pcats = [cat_of(pw) for pw in pws]
    eye_cat = ((cat_lane & (CHUNK - 1)) == cat_row).astype(F32)
    tcats = [eye_cat + pc for pc in pcats]
    pcats = [_bdot(pc, block_diag(pc)) for pc in pcats]
    n_levels = int(math.log2(CHUNK))
    for lev in range(1, n_levels):
        bds = [block_diag(pc) for pc in pcats]
        if lev < n_levels - 1:
            prods = [_bdot(jnp.concatenate([pc, tc], axis=0), bd) for pc, tc, bd in zip(pcats, tcats, bds)]
            pcats = [pr[:CHUNK] for pr in prods]
            tcats = [tc + pr[CHUNK:] for tc, pr in zip(tcats, prods)]
        else:
            tcats = [tc + _bdot(tc, bd) for tc, bd in zip(tcats, bds)]
    tmats = [block_diag(tc) for tc in tcats]
    egcs = [jnp.exp(gc) for gc in gcs]
    uws = [_bdot(tm, jnp.concatenate([v * b, kb * eg], axis=1))
           for tm, v, b, kb, eg in zip(tmats, vs, betas, kbs, egcs)]
    us = [uw[:, :HEAD_DIM] for uw in uws]
    ws = [uw[:, HEAD_DIM:] for uw in uws]
    qkm = [_bdot_nt(q, k) for q, k in zip(qs, ks)]
    qkm = [jnp.where(incl, x * dc, 0.0) for x, dc in zip(qkm, decays)]
    q_decs = [q * eg for q, eg in zip(qs, egcs)]
    k_ends = [k * jnp.exp(gl - gc) for k, gl, gc in zip(ks, gls, gcs)]

    for hh in heads:
        vn_ref[hh] = jnp.zeros((tb, HEAD_DIM), F32)
    outs = [[] for _ in heads]
    for ci in range(tb // CHUNK):
        cs = slice(ci * CHUNK, (ci + 1) * CHUNK)
        sts = [state_ref[hh] for hh in heads]
        ws_qs = [_bdot(jnp.concatenate([ws[hh][cs], q_decs[hh][cs]], axis=0), sts[hh]) for hh in heads]
        v_news = [us[hh][cs] - ws_qs[hh][:CHUNK] for hh in heads]
        for hh in heads:
            vn_ref[hh, cs, :] = v_news[hh]
        intra = [_bdot(qkm[hh][cs], vn_ref[hh]) for hh in heads]
        upd = [_bdot_tn(k_ends[hh][cs], v_news[hh]) for hh in heads]
        for hh in heads:
            outs[hh].append(ws_qs[hh][CHUNK:] + intra[hh])
            g_last = gls[hh][ci * CHUNK:ci * CHUNK + 1, :]
            state_ref[hh] = sts[hh] * jnp.exp(g_last) + upd[hh]
    for hh in heads:
        o = jnp.concatenate(outs[hh], axis=0)
        o = o * lax.rsqrt(jnp.mean(o * o, axis=-1, keepdims=True) + EPS) * gout_ref[...]
        zz = z_ref[:, hsl[hh]].astype(F32)
        o_ref[:, hsl[hh]] = (o * (zz * jax.nn.sigmoid(zz))).astype(o_ref.dtype)


def _gated_delta(big, small, small_t, conv_w, g_out, bsz, seq, n_heads, d_model):
    t = bsz * seq
    tb = GDN_TB
    ns = seq // tb
    width = GDN_HG * HEAD_DIM
    nhg = n_heads // GDN_HG
    blocks_per_group = d_model // width
    rows_t = small_t.shape[0]

    def colspec(group):
        return pl.BlockSpec((tb, width), lambda b, h, s: (b * ns + s, group * blocks_per_group + h))

    def cwspec(group):
        return pl.BlockSpec((CONV_WIDTH, width), lambda b, h, s: (0, group * blocks_per_group + h))

    return pl.pallas_call(
        functools.partial(_gdn_kernel, n_heads=n_heads),
        grid=(bsz, nhg, ns),
        in_specs=[
            colspec(0), colspec(1), colspec(2), colspec(3),
            pl.BlockSpec((tb, LANES), lambda b, h, s: (b * ns + s, 0)),
            pl.BlockSpec((rows_t, tb), lambda b, h, s: (0, b * ns + s)),
            cwspec(0), cwspec(1), cwspec(2),
            pl.BlockSpec((1, HEAD_DIM), lambda b, h, s: (0, 0)),
        ],
        out_specs=pl.BlockSpec((tb, width), lambda b, h, s: (b * ns + s, h)),
        out_shape=jax.ShapeDtypeStruct((t, d_model), BF16),
        scratch_shapes=[
            pltpu.VMEM((GDN_HG, HEAD_DIM, HEAD_DIM), F32),
            pltpu.VMEM((2 * SUBLANES, width), F32),
            pltpu.VMEM((2 * SUBLANES, width), F32),
            pltpu.VMEM((2 * SUBLANES, width), F32),
            pltpu.VMEM((GDN_HG, tb, HEAD_DIM), F32),
        ],
        compiler_params=_cparams(("parallel", "parallel", "arbitrary")),
        name="gated_delta",
    )(big, big, big, big, small, small_t, conv_w, conv_w, conv_w, g_out)


def _t5_bucket(n):
    max_exact = N_BUCKETS // 2
    nf = jnp.maximum(n, 1).astype(F32)
    large = max_exact + (jnp.log(nf / max_exact) / math.log(MAX_DISTANCE / max_exact)
                         * (N_BUCKETS - max_exact)).astype(jnp.int32)
    large = jnp.minimum(large, N_BUCKETS - 1)
    return jnp.where(n < max_exact, n, large)


def _attn_kernel(rb_ref, q_ref, k_ref, v_ref, lam_ref, gsub_ref, o_ref,
                 bias_ref, m_ref, acc_ref, sa_ref, sb_ref, *, lam_init):
    hg = pl.program_id(0)
    b = pl.program_id(1)
    qi = pl.program_id(2)
    bq, bk = ATT_BQ, ATT_BK
    heads = range(ATT_HG)
    hsl = [slice(hh * HEAD_DIM, (hh + 1) * HEAD_DIM) for hh in heads]

    @pl.when(jnp.logical_and(b == 0, qi == 0))
    def _():
        blk = LANES
        i = lax.broadcasted_iota(jnp.int32, (blk, blk), 0)
        jj = lax.broadcasted_iota(jnp.int32, (blk, blk), 1)
        for hh in heads:
            head = hg * ATT_HG + hh
            far = rb_ref[N_BUCKETS - 1, head]

            def toeplitz(offset):
                bucket = _t5_bucket(jnp.maximum(i - jj + offset, 0))
                out = jnp.zeros((blk, blk), F32)
                for cc in range(N_BUCKETS):
                    out = jnp.where(bucket == cc, rb_ref[cc, head] - far, out)
                return out

            on_diag = jnp.where(i >= jj, toeplitz(0), NEG_BIG)
            next_diag = toeplitz(blk)
            kinds = {0: on_diag, 1: next_diag}
            bias_ref[hh, 2] = jnp.zeros((bq, bk), F32)
            for slot in range(2):
                for rr in range(bq // blk):
                    for cc in range(bk // blk):
                        delta = rr - cc + slot * (bk // blk)
                        if delta < 0:
                            tile = jnp.full((blk, blk), NEG_BIG, F32)
                        else:
                            tile = kinds.get(delta, jnp.zeros((blk, blk), F32))
                        bias_ref[hh, slot, rr * blk:(rr + 1) * blk, cc * blk:(cc + 1) * blk] = tile

    m_ref[...] = jnp.full(m_ref.shape, NEG_BIG, F32)
    acc_ref[...] = jnp.zeros(acc_ref.shape, F32)

    lane = lax.broadcasted_iota(jnp.int32, (bq, HEAD_DIM), 1)
    qs = []
    for hs in hsl:
        q = q_ref[:, hs]
        zero = jnp.zeros_like(q)
        qs.append(jnp.concatenate([jnp.where(lane < DH_DIFF, q, zero),
                                   jnp.where(lane < DH_DIFF, zero, q)], axis=0))
    ones_col = (lax.broadcasted_iota(jnp.int32, (bk, HEAD_DIM), 1) == 0).astype(BF16)

    def scores(j, s_ref):
        ks = pl.multiple_of(j * bk, bk)
        for hh in heads:
            s_ref[hh] = lax.dot_general(qs[hh], k_ref[pl.ds(ks, bk), hsl[hh]], (((1,), (1,)), ((), ())),
                                        preferred_element_type=F32)

    def absorb(j, s_ref, biased=True):
        ks = pl.multiple_of(j * bk, bk)
        v_exts = [jnp.concatenate([v_ref[pl.ds(ks, bk), hs], ones_col], axis=1) for hs in hsl]
        if biased:
            slot = jnp.minimum(qi - j, 2)
            scs = [jnp.concatenate([s_ref[hh, 0:bq, :] + bias_ref[hh, slot],
                                    s_ref[hh, bq:2 * bq, :] + bias_ref[hh, slot]], axis=0) for hh in heads]
        else:
            scs = [s_ref[hh] for hh in heads]
        m_olds = [m_ref[hh] for hh in heads]
        m_news = [jnp.maximum(mo, jnp.max(sc, axis=-1, keepdims=True)) for mo, sc in zip(m_olds, scs)]
        ps = [jnp.exp(sc - mn) for sc, mn in zip(scs, m_news)]
        pvs = [jnp.dot(p.astype(BF16), ve, preferred_element_type=F32) for p, ve in zip(ps, v_exts)]
        for hh in heads:
            acc_ref[hh] = jnp.exp(m_olds[hh] - m_news[hh]) * acc_ref[hh] + pvs[hh]
            m_ref[hh] = m_news[hh]

    n_tiles = qi + 1
    scores(0, sa_ref)

    def pair_body(jj, carry, biased):
        j0 = 2 * jj
        scores(j0 + 1, sb_ref)
        absorb(j0, sa_ref, biased)
        scores(jnp.minimum(j0 + 2, qi), sa_ref)
        absorb(j0 + 1, sb_ref, biased)
        return carry

    n_far_pairs = jnp.maximum(qi - 1, 0) // 2
    lax.fori_loop(0, n_far_pairs, functools.partial(pair_body, biased=False), 0)
    lax.fori_loop(n_far_pairs, n_tiles // 2, functools.partial(pair_body, biased=True), 0)

    @pl.when(n_tiles % 2 == 1)
    def _():
        absorb(qi, sa_ref)

    lam_p = lam_ref[...]
    s1 = jnp.sum(lam_p[0:1] * lam_p[1:2], axis=-1, keepdims=True)
    s2 = jnp.sum(lam_p[2:3] * lam_p[3:4], axis=-1, keepdims=True)
    lam = jnp.exp(s1) - jnp.exp(s2) + lam_init
    for hh in heads:
        acc = acc_ref[hh]
        num = acc[:, :HEAD_DIM]
        den = acc[:, HEAD_DIM:HEAD_DIM + 1]
        o = num[:bq] / den[:bq] - lam * (num[bq:] / den[bq:])
        o = o * lax.rsqrt(jnp.mean(o * o, axis=-1, keepdims=True) + EPS) * gsub_ref[...]
        o_ref[:, hsl[hh]] = (o * (1.0 - lam_init)).astype(o_ref.dtype)


def _diff_attention(proj_qk, proj_plain, rel_bias, lam_params, g_subln, bsz, seq, n_heads, d_model,
                    lam_init):
    t = bsz * seq
    nq = seq // ATT_BQ
    assert ATT_BQ == ATT_BK and MAX_DISTANCE <= LANES and n_heads % ATT_HG == 0
    width = ATT_HG * HEAD_DIM
    per = d_model // width
    vcol = 4 * per
    return pl.pallas_call(
        functools.partial(_attn_kernel, lam_init=lam_init),
        grid=(n_heads // ATT_HG, bsz, nq),
        in_specs=[
            pl.BlockSpec(memory_space=pltpu.SMEM),
            pl.BlockSpec((ATT_BQ, width), lambda h, b, i: (b * nq + i, h)),
            pl.BlockSpec((seq, width), lambda h, b, i: (b, per + h)),
            pl.BlockSpec((seq, width), lambda h, b, i: (b, vcol + h)),
            pl.BlockSpec((4, DH_DIFF), lambda h, b, i: (0, 0)),
            pl.BlockSpec((1, HEAD_DIM), lambda h, b, i: (0, 0)),
        ],
        out_specs=pl.BlockSpec((ATT_BQ, width), lambda h, b, i: (b * nq + i, h)),
        out_shape=jax.ShapeDtypeStruct((t, d_model), BF16),
        scratch_shapes=[
            pltpu.VMEM((ATT_HG, 3, ATT_BQ, ATT_BK), F32),
            pltpu.VMEM((ATT_HG, 2 * ATT_BQ, 1), F32),
            pltpu.VMEM((ATT_HG, 2 * ATT_BQ, 2 * HEAD_DIM), F32),
            pltpu.VMEM((ATT_HG, 2 * ATT_BQ, ATT_BK), F32),
            pltpu.VMEM((ATT_HG, 2 * ATT_BQ, ATT_BK), F32),
        ],
        compiler_params=_cparams(("arbitrary", "arbitrary", "arbitrary")),
        name="diff_attention",
    )(rel_bias, proj_qk, proj_qk, proj_plain, lam_params, g_subln)


def _mix_kernel(ga_ref, gb_ref, oa_ref, od_ref, x_ref, wo_ref, gffn_ref, wr_ref, br_ref,
                x1_ref, h2_ref, topi_ref, topw_ref, rank_ref, cnt_ref, carry_ref):
    i = pl.program_id(0)
    tm = MIX_TM

    @pl.when(i == 0)
    def _():
        carry_ref[...] = jnp.zeros_like(carry_ref)

    mix = (ga_ref[...].astype(F32) * oa_ref[...].astype(F32)
           + gb_ref[...].astype(F32) * od_ref[...].astype(F32))
    x1 = x_ref[...] + jnp.dot(mix.astype(BF16), wo_ref[...], preferred_element_type=F32)
    x1_ref[...] = x1
    h2 = x1 * lax.rsqrt(jnp.mean(x1 * x1, axis=-1, keepdims=True) + EPS) * gffn_ref[...]
    h2_ref[...] = _pack_halves(h2)

    logits = lax.dot_general(wr_ref[...], h2, (((1,), (1,)), ((), ())),
                             preferred_element_type=F32, precision=lax.Precision.HIGHEST) + br_ref[...]
    eidx = lax.broadcasted_iota(jnp.int32, logits.shape, 0).astype(F32)
    vals, hots = [], []
    cur = logits
    for kk in range(TOP_K):
        mx = jnp.max(cur, axis=0, keepdims=True)
        idx = jnp.min(jnp.where(cur == mx, eidx, float(N_EXPERTS)), axis=0, keepdims=True)
        hot = eidx == idx
        vals.append(mx)
        hots.append(hot)
        topi_ref[kk:kk + 1, :] = idx.astype(jnp.int32)
        cur = jnp.where(hot, -jnp.inf, cur)
    exps = [jnp.exp(vv - vals[0]) for vv in vals]
    denom = exps[0] + exps[1] + exps[2] + exps[3]
    for kk in range(TOP_K):
        topw_ref[kk:kk + 1, :] = exps[kk] / denom

    sel = hots[0]
    for kk in range(1, TOP_K):
        sel = jnp.logical_or(sel, hots[kk])
    sel_f = sel.astype(F32)
    r = lax.broadcasted_iota(jnp.int32, (tm, tm), 0)
    c = lax.broadcasted_iota(jnp.int32, (tm, tm), 1)
    before = _bdot(sel_f, (r < c).astype(F32)) + carry_ref[...]
    for kk in range(TOP_K):
        rank_ref[kk:kk + 1, :] = jnp.sum(jnp.where(hots[kk], before, 0.0), axis=0,
                                         keepdims=True).astype(jnp.int32)
    carry_ref[...] = carry_ref[...] + jnp.sum(sel_f, axis=-1, keepdims=True)
    cnt_ref[...] = carry_ref[...].astype(jnp.int32)


def _mix_project_route(proj_gate, oa, od, x2d, w_o, g_ffn, w_r_t, b_r, d_model):
    t = x2d.shape[0]
    tm = MIX_TM
    full = lambda shape: pl.BlockSpec(shape, lambda i: (0, 0))
    row = lambda: pl.BlockSpec((tm, d_model), lambda i: (i, 0))
    krow = lambda: pl.BlockSpec((TOP_K, tm), lambda i: (0, i))
    return pl.pallas_call(
        _mix_kernel,
        grid=(t // tm,),
        in_specs=[
            pl.BlockSpec((tm, d_model), lambda i: (i, 0)),
            pl.BlockSpec((tm, d_model), lambda i: (i, 1)),
            row(), row(), row(),
            full((d_model, d_model)), full((1, d_model)), full((N_EXPERTS, d_model)), full((N_EXPERTS, 1)),
        ],
        out_specs=[row(), pl.BlockSpec((tm, d_model // 2), lambda i: (i, 0)),
                   krow(), krow(), krow(), full((N_EXPERTS, 1))],
        out_shape=[
            jax.ShapeDtypeStruct((t, d_model), F32),
            jax.ShapeDtypeStruct((t, d_model // 2), jnp.int32),
            jax.ShapeDtypeStruct((TOP_K, t), jnp.int32),
            jax.ShapeDtypeStruct((TOP_K, t), F32),
            jax.ShapeDtypeStruct((TOP_K, t), jnp.int32),
            jax.ShapeDtypeStruct((N_EXPERTS, 1), jnp.int32),
        ],
        scratch_shapes=[pltpu.VMEM((N_EXPERTS, 1), F32)],
        compiler_params=_cparams(("arbitrary",)),
        name="merge_outproj_route",
    )(proj_gate, proj_gate, oa, od, x2d, w_o, g_ffn, w_r_t, b_r)


def _pack_halves(x):
    half = x.shape[1] // 2
    bits = pltpu.bitcast(x.astype(BF16).astype(F32), jnp.int32)
    return bits[:, :half] | lax.shift_right_logical(bits[:, half:], 16)


def _unpack_halves(p):
    hi = pltpu.bitcast(p & jnp.int32(-65536), F32)
    lo = pltpu.bitcast(lax.shift_left(p, 16), F32)
    return jnp.concatenate([hi, lo], axis=1)


def _expert_kernel(be_ref, nu_ref, x_ref, wup_ref, bup_ref, wdn_ref, bdn_ref, *rest):
    y_ref, wup_bf, wdn_bf = rest[-3:]
    i = pl.program_id(0)
    d_ff = wdn_ref.shape[1]

    @pl.when(jnp.logical_or(i == 0, be_ref[i] != be_ref[jnp.maximum(i - 1, 0)]))
    def _():
        rr = lax.broadcasted_iota(jnp.int32, (2 * LANES, 2 * LANES), 0)
        cc = lax.broadcasted_iota(jnp.int32, (2 * LANES, 2 * LANES), 1)
        pick = jnp.where(cc < LANES, 2 * cc, 2 * (cc - LANES) + 1)
        perm = (rr == pick).astype(BF16)
        for g in range(wup_ref.shape[2] // (2 * LANES)):
            cs = slice(g * 2 * LANES, (g + 1) * 2 * LANES)
            wup_bf[:, cs] = jnp.dot(wup_ref[0, :, cs].astype(BF16), perm,
                                    preferred_element_type=F32).astype(BF16)
        wdn_bf[...] = wdn_ref[0].astype(BF16)

    @pl.when(i < nu_ref[0])
    def _():
        x = _unpack_halves(x_ref[...])
        hid = jnp.dot(x.astype(BF16), wup_bf[...], preferred_element_type=F32) + bup_ref[0]
        acts = []
        for g in range(hid.shape[1] // (2 * LANES)):
            glu = jnp.minimum(hid[:, g * 2 * LANES:g * 2 * LANES + LANES], SWIGLU_LIMIT)
            lin = jnp.clip(hid[:, g * 2 * LANES + LANES:(g + 1) * 2 * LANES], -SWIGLU_LIMIT, SWIGLU_LIMIT)
            acts.append(glu * jax.nn.sigmoid(SWIGLU_ALPHA * glu) * (lin + 1.0))
        act = jnp.concatenate(acts, axis=1)
        assert act.shape[1] == d_ff
        y = jnp.dot(act.astype(BF16), wdn_bf[...], preferred_element_type=F32) + bdn_ref[0]
        y_ref[...] = _pack_halves(y)

    @pl.when(i >= nu_ref[0])
    def _():
        y_ref[...] = jnp.zeros(y_ref.shape, y_ref.dtype)


def _experts(block_e, n_used, xs, y_prev, first_block, n_rows_total, w_up, b_up, w_down, b_down):
    n_rows, half = xs.shape
    d = w_up.shape[1]
    nb = n_rows // MOE_RB
    two_ff = w_up.shape[2]
    d_ff = w_down.shape[1]
    in_specs = [
        pl.BlockSpec((MOE_RB, half), lambda i, be, nu: (jnp.maximum(jnp.minimum(i, nu[0] - 1), 0), 0)),
        pl.BlockSpec((1, d, two_ff), lambda i, be, nu: (be[i], 0, 0)),
        pl.BlockSpec((1, 1, two_ff), lambda i, be, nu: (be[i], 0, 0)),
        pl.BlockSpec((1, d_ff, d), lambda i, be, nu: (be[i], 0, 0)),
        pl.BlockSpec((1, 1, d), lambda i, be, nu: (be[i], 0, 0)),
    ]
    operands = [block_e, n_used, xs, w_up, b_up, w_down, b_down]
    aliases = {}
    if y_prev is not None:
        in_specs.append(pl.BlockSpec(memory_space=pl.ANY))
        aliases = {len(operands): 0}
        operands.append(y_prev)
    grid_spec = pltpu.PrefetchScalarGridSpec(
        num_scalar_prefetch=2,
        grid=(nb,),
        in_specs=in_specs,
        out_specs=pl.BlockSpec((MOE_RB, half), lambda i, be, nu: (first_block + i, 0)),
        scratch_shapes=[pltpu.VMEM((d, two_ff), BF16), pltpu.VMEM((d_ff, d), BF16)],
    )
    return pl.pallas_call(
        _expert_kernel,
        grid_spec=grid_spec,
        out_shape=jax.ShapeDtypeStruct((n_rows_total, half), jnp.int32),
        input_output_aliases=aliases,
        compiler_params=_cparams(("arbitrary",)),
        name="moe_experts",
    )(*operands)


def _sc_invert_slots(dest_flat, n_rows):
    n_assign = dest_flat.shape[0]
    n_workers = SC_CORES * SC_SUBCORES
    rows_per_w = n_rows // n_workers
    chunk = SC_SCAN_CHUNK
    assert n_rows % n_workers == 0 and rows_per_w % SC_LANES == 0 and n_assign % chunk == 0
    mesh = plsc.VectorSubcoreMesh(core_axis_name="c", subcore_axis_name="s",
                                  num_cores=SC_CORES, num_subcores=SC_SUBCORES)

    def body(dest_hbm, out_hbm, dest_v, map_v):
        wid = lax.axis_index("s") * SC_CORES + lax.axis_index("c")
        base = wid * rows_per_w
        lanes = lax.broadcasted_iota(jnp.int32, (SC_LANES,), 0)

        @pl.loop(0, rows_per_w, step=SC_LANES)
        def _(r0):
            map_v[pl.ds(r0, SC_LANES)] = jnp.full((SC_LANES,), -1, jnp.int32)

        @pl.loop(0, n_assign // chunk)
        def _(ci):
            pltpu.sync_copy(dest_hbm.at[pl.ds(ci * chunk, chunk)], dest_v)

            @pl.loop(0, chunk, step=SC_LANES)
            def _(j):
                local = dest_v[pl.ds(j, SC_LANES)] - base
                mine = jnp.logical_and(local >= 0, local < rows_per_w)
                plsc.store_scatter(map_v, [jnp.where(mine, local, 0)], ci * chunk + j + lanes, mask=mine)

        pltpu.sync_copy(map_v, out_hbm.at[pl.ds(base, rows_per_w)])

    return pl.kernel(
        body,
        out_type=jax.ShapeDtypeStruct((n_rows,), jnp.int32),
        mesh=mesh,
        scratch_types=[pltpu.VMEM((chunk,), jnp.int32), pltpu.VMEM((rows_per_w,), jnp.int32)],
        compiler_params=pltpu.CompilerParams(needs_layout_passes=False),
        name="moe_slot_inverse",
    )(dest_flat)


def _sc_gather_rows(table, idx):
    n_idx = idx.shape[0]
    d = table.shape[1]
    n_workers = SC_CORES * SC_SUBCORES
    per_worker = n_idx // n_workers
    n_chunks = per_worker // SC_GATHER_ROWS
    assert n_idx % n_workers == 0 and per_worker % SC_GATHER_ROWS == 0
    mesh = plsc.VectorSubcoreMesh(core_axis_name="c", subcore_axis_name="s",
                                  num_cores=SC_CORES, num_subcores=SC_SUBCORES)

    assert n_chunks % 2 == 0

    def body(table_hbm, idx_hbm, out_hbm, idx_v, rows_a, rows_b, sem_a, sem_b):
        wid = lax.axis_index("s") * SC_CORES + lax.axis_index("c")
        base = wid * per_worker
        pltpu.sync_copy(idx_hbm.at[pl.ds(base, per_worker)], idx_v)

        def gather(ci, rows_v, sem):
            off = pl.multiple_of(ci * SC_GATHER_ROWS, SC_GATHER_ROWS)
            return pltpu.make_async_copy(table_hbm.at[idx_v.at[pl.ds(off, SC_GATHER_ROWS)]], rows_v, sem)

        def put(ci, rows_v):
            off = pl.multiple_of(ci * SC_GATHER_ROWS, SC_GATHER_ROWS)
            pltpu.sync_copy(rows_v, out_hbm.at[pl.ds(base + off, SC_GATHER_ROWS)])

        gather(0, rows_a, sem_a).start()

        @pl.loop(0, n_chunks, step=2)
        def _(ci):
            gather(ci + 1, rows_b, sem_b).start()
            gather(ci, rows_a, sem_a).wait()
            put(ci, rows_a)
            nxt = jnp.minimum(ci + 2, n_chunks - 1)
            gather(nxt, rows_a, sem_a).start()
            gather(ci + 1, rows_b, sem_b).wait()
            put(ci + 1, rows_b)

        gather(n_chunks - 1, rows_a, sem_a).wait()

    return pl.kernel(
        body,
        out_type=jax.ShapeDtypeStruct((n_idx, d), table.dtype),
        mesh=mesh,
        scratch_types=[
            pltpu.VMEM((per_worker,), jnp.int32),
            pltpu.VMEM((SC_GATHER_ROWS, d), table.dtype),
            pltpu.VMEM((SC_GATHER_ROWS, d), table.dtype),
            pltpu.SemaphoreType.DMA,
            pltpu.SemaphoreType.DMA,
        ],
        name="moe_slot_gather",
    )(table, idx)


def _combine_kernel(x1_ref, w_ref, y0_ref, y1_ref, y2_ref, y3_ref, o_ref):
    w = w_ref[...]
    out = x1_ref[...]
    for kk, y_ref in enumerate((y0_ref, y1_ref, y2_ref, y3_ref)):
        out = out + w[:, kk:kk + 1] * _unpack_halves(y_ref[...])
    o_ref[...] = out


def _combine(x1, w_tok, y_slots):
    t, d = x1.shape
    tc = COMB_TC
    nt = t // tc
    yspec = lambda kk: pl.BlockSpec((tc, d // 2), lambda i: (kk * nt + i, 0))
    return pl.pallas_call(
        _combine_kernel,
        grid=(nt,),
        in_specs=[
            pl.BlockSpec((tc, d), lambda i: (i, 0)),
            pl.BlockSpec((tc, TOP_K), lambda i: (i, 0)),
            yspec(0), yspec(1), yspec(2), yspec(3),
        ],
        out_specs=pl.BlockSpec((tc, d), lambda i: (i, 0)),
        out_shape=jax.ShapeDtypeStruct((t, d), F32),
        compiler_params=_cparams(("parallel",)),
        name="moe_combine",
    )(x1, w_tok, y_slots, y_slots, y_slots, y_slots)


def _moe(x1, h2, topi, topw, rank, counts, w_up, b_up, w_down, b_down):
    t, d = x1.shape
    n_assign = t * TOP_K
    nb = -(-n_assign // MOE_RB) + N_EXPERTS
    n_rows = nb * MOE_RB
    counts = counts[:, 0]
    padded = (counts + MOE_RB - 1) // MOE_RB * MOE_RB
    padded_end = jnp.cumsum(padded)
    padded_start = padded_end - padded
    expert_ids = jnp.arange(N_EXPERTS, dtype=jnp.int32)[:, None, None]
    start_of = jnp.sum(jnp.where(topi[None] == expert_ids, padded_start[:, None, None], 0), axis=0)
    dest = (start_of + rank).astype(jnp.int32)
    n_used = (padded_end[-1] // MOE_RB).astype(jnp.int32)
    blk = jnp.minimum(jnp.arange(nb, dtype=jnp.int32), n_used - 1)
    block_e = jnp.minimum(jnp.sum(padded_end[None, :] <= (blk * MOE_RB)[:, None], axis=1),
                          N_EXPERTS - 1).astype(jnp.int32)
    slot_of = _sc_invert_slots(dest.reshape(-1), n_rows)
    src_tok = jnp.where(slot_of < 0, jnp.arange(n_rows, dtype=jnp.int32), slot_of) % t

    nb_a = nb // 2
    y_rows = None
    for first, n_blk in ((0, nb_a), (nb_a, nb - nb_a)):
        xs = _sc_gather_rows(h2, lax.slice(src_tok, (first * MOE_RB,), ((first + n_blk) * MOE_RB,)))
        used = jnp.clip(n_used - first, 0, n_blk).reshape(1)
        y_rows = _experts(lax.slice(block_e, (first,), (first + n_blk,)), used, xs, y_rows, first, n_rows,
                          w_up, b_up, w_down, b_down)
    y_slots = _sc_gather_rows(y_rows, dest.reshape(-1))
    return _combine(x1, topw.T, y_slots)


def kernel(x, g_mix, w_in, b_gate, conv_w, a_log, dt_bias, g_delta_out, q_norm, k_norm, lambda_q1, lambda_k1, lambda_q2, lambda_k2, g_subln, rel_bias, w_o, g_ffn, w_router, b_router, w_up, b_up, w_down, b_down):
    bsz, seq, d = x.shape
    depth = g_mix.shape[0]
    n_heads = d // HEAD_DIM
    t = bsz * seq
    d_ff = w_down.shape[2]
    assert d % PROJ_TN == 0 and t % PROJ_TM == 0 and seq % GDN_TB == 0 and seq % ATT_BQ == 0
    assert t % MIX_TM == 0 and t % COMB_TC == 0 and n_heads % GDN_HG == 0
    assert (t * TOP_K) % MOE_RB == 0
    assert 2 * n_heads <= 2 * SUBLANES

    x2d = x.reshape(t, d)
    for l in range(depth):
        wl = w_in[l]
        c0 = 4 * d
        c1 = c0 + 2 * n_heads
        c2 = c1 + 2 * d
        c3 = c2 + d
        w_small = jnp.pad(wl[:, c0:c1], ((0, 0), (0, LANES - 2 * n_heads)))
        gm = g_mix[l].reshape(1, d)
        w_plain = jnp.concatenate([wl[:, :c0], wl[:, c2:c3]], axis=1).astype(BF16)
        proj_plain = _input_projection(x2d, gm, w_plain, jnp.zeros((1, 5 * d), F32), "plain")
        qk_gain = jnp.concatenate([jnp.tile(q_norm[l] * (DH_DIFF ** -0.5), 2 * n_heads),
                                   jnp.tile(k_norm[l], 2 * n_heads)]).reshape(1, 2 * d)
        proj_qk = _input_projection(x2d, gm, wl[:, c1:c2].astype(BF16), qk_gain, "qknorm")
        proj_gate = _input_projection(x2d, gm, wl[:, c3:].astype(BF16), b_gate[l].reshape(1, 2 * d), "gate")

        head_pad = jnp.zeros((LANES - 2 * n_heads,), F32)
        alog = jnp.concatenate([jnp.zeros((n_heads,), F32), a_log[l], head_pad])
        dtb = jnp.concatenate([jnp.zeros((n_heads,), F32), dt_bias[l], head_pad])
        rows_t = 2 * n_heads
        small, small_t = _small_projection(
            x2d, g_mix[l].reshape(1, d), w_small.astype(BF16), w_small[:, :rows_t].T.astype(BF16),
            alog.reshape(1, LANES), dtb.reshape(1, LANES),
            alog[:rows_t].reshape(rows_t, 1), dtb[:rows_t].reshape(rows_t, 1), n_heads)

        oa = _gated_delta(proj_plain, small, small_t, conv_w[l], g_delta_out[l].reshape(1, HEAD_DIM),
                          bsz, seq, n_heads, d)

        lam_init = 0.8 - 0.6 * math.exp(-0.3 * l)
        lam_params = jnp.stack([lambda_q1[l], lambda_k1[l], lambda_q2[l], lambda_k2[l]])
        od = _diff_attention(proj_qk, proj_plain, rel_bias, lam_params, g_subln[l].reshape(1, HEAD_DIM),
                             bsz, seq, n_heads, d, lam_init)

        x1, h2, topi, topw, rank, counts = _mix_project_route(
            proj_gate, oa, od, x2d, w_o[l].astype(BF16), g_ffn[l].reshape(1, d),
            w_router[l].T, b_router[l].reshape(N_EXPERTS, 1), d)

        b_up_l = b_up[l].reshape(N_EXPERTS, 2 * d_ff // (2 * LANES), LANES, 2)
        b_up_l = jnp.swapaxes(b_up_l, 2, 3).reshape(N_EXPERTS, 1, 2 * d_ff)
        x2d = _moe(x1, h2, topi, topw, rank, counts, w_up[l], b_up_l,
                   w_down[l], b_down[l].reshape(N_EXPERTS, 1, d))
    return x2d.reshape(bsz, seq, d)
```

```python
import functools
import math

import jax
import jax.numpy as jnp
from jax import lax
from jax.experimental import pallas as pl
from jax.experimental.pallas import tpu as pltpu
from jax.experimental.pallas import tpu_sc as plsc

F32 = jnp.float32
BF16 = jnp.bfloat16

HEAD_DIM = 128
DH_DIFF = HEAD_DIM // 2
CONV_WIDTH = 4
CHUNK = 64
N_BUCKETS = 32
MAX_DISTANCE = 128
N_EXPERTS = 32
TOP_K = 4
SWIGLU_LIMIT = 7.0
SWIGLU_ALPHA = 1.702
EPS = 1e-6
NEG_BIG = -1e30

LANES = 128
SUBLANES = 8
VMEM_LIMIT = 56 * 1024 * 1024
SC_CORES = 2
SC_SUBCORES = 16
SC_LANES = 16
SC_GATHER_ROWS = 64
SC_SCAN_CHUNK = 4096

PROJ_TM = 2048
PROJ_TN = 1024
PROJ_CHUNK = 256
GDN_TB = 256
GDN_HG = 8
ATT_HG = 1
ATT_BQ = 512
ATT_BK = 512
MIX_TM = 512
MOE_RB = 512
COMB_TC = 512


def _cparams(sem):
    return pltpu.CompilerParams(dimension_semantics=sem, vmem_limit_bytes=VMEM_LIMIT)


def _bdot(a, b):
    return jnp.dot(a.astype(BF16), b.astype(BF16), preferred_element_type=F32)


def _bdot_nt(a, b):
    return lax.dot_general(a.astype(BF16), b.astype(BF16), (((1,), (1,)), ((), ())),
                           preferred_element_type=F32)


def _bdot_tn(a, b):
    return lax.dot_general(a.astype(BF16), b.astype(BF16), (((0,), (0,)), ((), ())),
                           preferred_element_type=F32)


def _proj_kernel(x_ref, g_ref, w_ref, aux_ref, o_ref, h_ref, *, mode):
    @pl.when(pl.program_id(1) == 0)
    def _():
        x = x_ref[...]
        ms = jnp.mean(x * x, axis=-1, keepdims=True)
        h_ref[...] = (x * lax.rsqrt(ms + EPS) * g_ref[...]).astype(BF16)

    h = h_ref[...]
    lo = lax.broadcasted_iota(jnp.int32, (1, LANES), 1) < DH_DIFF
    for c in range(PROJ_TN // PROJ_CHUNK):
        cs = slice(c * PROJ_CHUNK, (c + 1) * PROJ_CHUNK)
        acc = jnp.dot(h, w_ref[:, cs], preferred_element_type=F32)
        if mode == "plain":
            o_ref[:, cs] = acc.astype(o_ref.dtype)
        elif mode == "gate":
            o_ref[:, cs] = jax.nn.sigmoid(acc + aux_ref[:, cs]).astype(o_ref.dtype)
        else:
            for g in range(PROJ_CHUNK // LANES):
                sl = slice(c * PROJ_CHUNK + g * LANES, c * PROJ_CHUNK + (g + 1) * LANES)
                y = acc[:, g * LANES:(g + 1) * LANES]
                y2 = y * y
                s_lo = jnp.sum(jnp.where(lo, y2, 0.0), axis=-1, keepdims=True)
                s_hi = jnp.sum(jnp.where(lo, 0.0, y2), axis=-1, keepdims=True)
                r = jnp.where(lo, lax.rsqrt(s_lo / DH_DIFF + EPS), lax.rsqrt(s_hi / DH_DIFF + EPS))
                o_ref[:, sl] = (y * r * aux_ref[:, sl]).astype(o_ref.dtype)


def _input_projection(x2d, g_mix, w, aux, mode):
    t, d = x2d.shape
    n = w.shape[1]
    return pl.pallas_call(
        functools.partial(_proj_kernel, mode=mode),
        grid=(t // PROJ_TM, n // PROJ_TN),
        in_specs=[
            pl.BlockSpec((PROJ_TM, d), lambda i, j: (i, 0)),
            pl.BlockSpec((1, d), lambda i, j: (0, 0)),
            pl.BlockSpec((d, PROJ_TN), lambda i, j: (0, j)),
            pl.BlockSpec((1, PROJ_TN), lambda i, j: (0, j)),
        ],
        out_specs=pl.BlockSpec((PROJ_TM, PROJ_TN), lambda i, j: (i, j)),
        out_shape=jax.ShapeDtypeStruct((t, n), BF16),
        scratch_shapes=[pltpu.VMEM((PROJ_TM, d), BF16)],
        compiler_params=_cparams(("parallel", "arbitrary")),
        name="input_projection_" + mode,
    )(x2d, g_mix, w, aux)


def _small_proj_kernel(x_ref, g_ref, w_ref, wt_ref, alog_ref, dtb_ref, alog_t_ref, dtb_t_ref,
                       o_ref, ot_ref, *, n_heads):
    x = x_ref[...]
    ms = jnp.mean(x * x, axis=-1, keepdims=True)
    h = (x * lax.rsqrt(ms + EPS) * g_ref[...]).astype(BF16)

    def finish(acc, idx, alog, dtb):
        beta = jax.nn.sigmoid(acc)
        z = acc + dtb
        softplus = jnp.maximum(z, 0.0) + jnp.log1p(jnp.exp(-jnp.abs(z)))
        gdec = -jnp.exp(alog) * softplus
        return jnp.where(idx < n_heads, beta, jnp.where(idx < 2 * n_heads, gdec, 0.0))

    acc = jnp.dot(h, w_ref[...], preferred_element_type=F32)
    lane = lax.broadcasted_iota(jnp.int32, acc.shape, 1)
    o_ref[...] = finish(acc, lane, alog_ref[...], dtb_ref[...])
    acc_t = lax.dot_general(wt_ref[...], h, (((1,), (1,)), ((), ())),
                            preferred_element_type=F32)
    sub = lax.broadcasted_iota(jnp.int32, acc_t.shape, 0)
    ot_ref[...] = finish(acc_t, sub, alog_t_ref[...], dtb_t_ref[...])


def _small_projection(x2d, g_mix, w_small, w_small_t, alog, dtb, alog_t, dtb_t, n_heads):
    t, d = x2d.shape
    rows_t = w_small_t.shape[0]
    tm = PROJ_TM
    full = lambda shape: pl.BlockSpec(shape, lambda i: (0, 0))
    return pl.pallas_call(
        functools.partial(_small_proj_kernel, n_heads=n_heads),
        grid=(t // tm,),
        in_specs=[
            pl.BlockSpec((tm, d), lambda i: (i, 0)),
            full((1, d)), full((d, LANES)), full((rows_t, d)),
            full((1, LANES)), full((1, LANES)), full((rows_t, 1)), full((rows_t, 1)),
        ],
        out_specs=[pl.BlockSpec((tm, LANES), lambda i: (i, 0)),
                   pl.BlockSpec((rows_t, tm), lambda i: (0, i))],
        out_shape=[jax.ShapeDtypeStruct((t, LANES), F32),
                   jax.ShapeDtypeStruct((rows_t, t), F32)],
        compiler_params=_cparams(("parallel",)),
        name="beta_decay_projection",
    )(x2d, g_mix, w_small, w_small_t, alog, dtb, alog_t, dtb_t)


def _gdn_kernel(q_ref, k_ref, v_ref, z_ref, sm_ref, smt_ref, cwq_ref, cwk_ref, cwv_ref, gout_ref,
                o_ref, state_ref, qp_ref, kp_ref, vp_ref, vn_ref, *, n_heads):
    hg = pl.program_id(1)
    s = pl.program_id(2)
    tb = GDN_TB
    pad = SUBLANES
    width = GDN_HG * HEAD_DIM

    @pl.when(s == 0)
    def _():
        state_ref[...] = jnp.zeros_like(state_ref)
        for p_ref in (qp_ref, kp_ref, vp_ref):
            p_ref[0:pad, :] = jnp.zeros((pad, width), F32)

    r = lax.broadcasted_iota(jnp.int32, (tb, tb), 0)
    c = lax.broadcasted_iota(jnp.int32, (tb, tb), 1)
    delay_mat = jnp.concatenate([(r - c == dd).astype(BF16) for dd in range(1, CONV_WIDTH)], axis=0)

    def conv_silu(x_ref, p_ref, cw_ref):
        x = x_ref[...]
        xf = x.astype(F32)
        p_ref[pad:2 * pad, :] = xf[0:pad]
        delayed = jnp.dot(delay_mat, x, preferred_element_type=F32)
        acc = cw_ref[CONV_WIDTH - 1:CONV_WIDTH, :] * xf
        for dd in range(1, CONV_WIDTH):
            first = p_ref[pad - dd:2 * pad - dd, :]
            xd = jnp.concatenate([first, delayed[(dd - 1) * tb + pad:dd * tb]], axis=0)
            acc = acc + cw_ref[CONV_WIDTH - 1 - dd:CONV_WIDTH - dd, :] * xd
        p_ref[0:pad, :] = xf[tb - pad:tb]
        return acc * jax.nn.sigmoid(acc)

    q_all = conv_silu(q_ref, qp_ref, cwq_ref)
    k_all = conv_silu(k_ref, kp_ref, cwk_ref)
    v_all = conv_silu(v_ref, vp_ref, cwv_ref)

    shift = int(math.log2(CHUNK))
    same = (r >> shift) == (c >> shift)
    incl = jnp.logical_and(same, c <= r)
    strict = jnp.logical_and(same, c < r)

    small = sm_ref[...]
    small_t = smt_ref[...]
    lane = lax.broadcasted_iota(jnp.int32, small.shape, 1)
    def split3(a):
        hi = a.astype(BF16)
        r1 = a - hi.astype(F32)
        mid = r1.astype(BF16)
        lo = (r1 - mid.astype(F32)).astype(BF16)
        return hi.astype(F32), mid.astype(F32), lo.astype(F32)

    part = 2 * n_heads
    s_hi, s_mid, s_lo = split3(small)
    small3 = jnp.where(lane < part, s_hi,
                       jnp.where(lane < 2 * part, pltpu.roll(s_mid, part, 1),
                                 jnp.where(lane < 3 * part, pltpu.roll(s_lo, 2 * part, 1), 0.0)))
    both = _bdot(jnp.concatenate([incl.astype(F32), same.astype(F32)], axis=0), small3)
    gcum = both[:tb]
    gtot = both[tb:]
    gcum_t = _bdot(jnp.concatenate(split3(small_t), axis=0),
                   jnp.logical_and(same, r <= c).astype(F32))
    sub3 = lax.broadcasted_iota(jnp.int32, gcum_t.shape, 0)

    heads = range(GDN_HG)
    hsl = [slice(hh * HEAD_DIM, (hh + 1) * HEAD_DIM) for hh in heads]
    qs = [q_all[:, hs] for hs in hsl]
    ks = [k_all[:, hs] for hs in hsl]
    vs = [v_all[:, hs] for hs in hsl]
    qs = [q * lax.rsqrt(jnp.sum(q * q, axis=-1, keepdims=True) + EPS) * (HEAD_DIM ** -0.5) for q in qs]
    ks = [k * lax.rsqrt(jnp.sum(k * k, axis=-1, keepdims=True) + EPS) for k in ks]

    def col_of(arr, idx):
        return jnp.sum(jnp.where(lane == idx, arr, 0.0), axis=-1, keepdims=True)

    def terms_of(pos, idx):
        return jnp.logical_or(pos == idx, jnp.logical_or(pos == idx + part, pos == idx + 2 * part))

    head_ids = [hg * GDN_HG + hh for hh in heads]
    betas = [col_of(small, hd) for hd in head_ids]
    gcs = [jnp.sum(jnp.where(terms_of(lane, hd + n_heads), gcum, 0.0), axis=-1, keepdims=True)
           for hd in head_ids]
    gls = [jnp.sum(jnp.where(terms_of(lane, hd + n_heads), gtot, 0.0), axis=-1, keepdims=True)
           for hd in head_ids]
    gc_rows = [jnp.sum(jnp.where(terms_of(sub3, hd + n_heads), gcum_t, 0.0), axis=0, keepdims=True)
               for hd in head_ids]

    decays = [jnp.where(incl, jnp.exp(jnp.minimum(gc - gr, 0.0)), 0.0) for gc, gr in zip(gcs, gc_rows)]
    kbs = [k * b for k, b in zip(ks, betas)]
    kks = [_bdot_nt(kb, k) for kb, k in zip(kbs, ks)]
    pws = [jnp.where(strict, -(kk * dc), 0.0) for kk, dc in zip(kks, decays)]
    n_chunks = tb // CHUNK
    cat_row = lax.broadcasted_iota(jnp.int32, (CHUNK, tb), 0)
    cat_lane = lax.broadcasted_iota(jnp.int32, (CHUNK, tb), 1)
    lane_chunk = cat_lane >> shift

    def block_diag(m_cat):
        return jnp.concatenate([jnp.where(lane_chunk == ci, m_cat, 0.0) for ci in range(n_chunks)], axis=0)

    def cat_of(m_bd):
        out = m_bd[0:CHUNK]
        for ci in range(1, n_chunks):
            out = out + m_bd[ci * CHUNK:(ci + 1) * CHUNK]
        return out

    pcats = [cat_of(pw) for pw in pws]
    eye_cat = ((cat_lane & (CHUNK - 1)) == cat_row).astype(F32)
    tcats = [eye_cat + pc for pc in pcats]
    pcats = [_bdot(pc, block_diag(pc)) for pc in pcats]
    n_levels = int(math.log2(CHUNK))
    for lev in range(1, n_levels):
        bds = [block_diag(pc) for pc in pcats]
        if lev < n_levels - 1:
            prods = [_bdot(jnp.concatenate([pc, tc], axis=0), bd) for pc, tc, bd in zip(pcats, tcats, bds)]
            pcats = [pr[:CHUNK] for pr in prods]
            tcats = [tc + pr[CHUNK:] for tc, pr in zip(tcats, prods)]
        else:
            tcats = [tc + _bdot(tc, bd) for tc, bd in zip(tcats, bds)]
    tmats = [block_diag(tc) for tc in tcats]
    egcs = [jnp.exp(gc) for gc in gcs]
    uws = [_bdot(tm, jnp.concatenate([v * b, kb * eg], axis=1))
           for tm, v, b, kb, eg in zip(tmats, vs, betas, kbs, egcs)]
    us = [uw[:, :HEAD_DIM] for uw in uws]
    ws = [uw[:, HEAD_DIM:] for uw in uws]
    qkm = [_bdot_nt(q, k) for q, k in zip(qs, ks)]
    qkm = [jnp.where(incl, x * dc, 0.0) for x, dc in zip(qkm, decays)]
    q_decs = [q * eg for q, eg in zip(qs, egcs)]
    k_ends = [k * jnp.exp(gl - gc) for k, gl, gc in zip(ks, gls, gcs)]

    for hh in heads:
        vn_ref[hh] = jnp.zeros((tb, HEAD_DIM), F32)
    outs = [[] for _ in heads]
    for ci in range(tb // CHUNK):
        cs = slice(ci * CHUNK, (ci + 1) * CHUNK)
        sts = [state_ref[hh] for hh in heads]
        ws_qs = [_bdot(jnp.concatenate([ws[hh][cs], q_decs[hh][cs]], axis=0), sts[hh]) for hh in heads]
        v_news = [us[hh][cs] - ws_qs[hh][:CHUNK] for hh in heads]
        for hh in heads:
            vn_ref[hh, cs, :] = v_news[hh]
        intra = [_bdot(qkm[hh][cs], vn_ref[hh]) for hh in heads]
        upd = [_bdot_tn(k_ends[hh][cs], v_news[hh]) for hh in heads]
        for hh in heads:
            outs[hh].append(ws_qs[hh][CHUNK:] + intra[hh])
            g_last = gls[hh][ci * CHUNK:ci * CHUNK + 1, :]
            state_ref[hh] = sts[hh] * jnp.exp(g_last) + upd[hh]
    for hh in heads:
        o = jnp.concatenate(outs[hh], axis=0)
        o = o * lax.rsqrt(jnp.mean(o * o, axis=-1, keepdims=True) + EPS) * gout_ref[...]
        zz = z_ref[:, hsl[hh]].astype(F32)
        o_ref[:, hsl[hh]] = (o * (zz * jax.nn.sigmoid(zz))).astype(o_ref.dtype)


def _gated_delta(big, small, small_t, conv_w, g_out, bsz, seq, n_heads, d_model):
    t = bsz * seq
    tb = GDN_TB
    ns = seq // tb
    width = GDN_HG * HEAD_DIM
    nhg = n_heads // GDN_HG
    blocks_per_group = d_model // width
    rows_t = small_t.shape[0]

    def colspec(group):
        return pl.BlockSpec((tb, width), lambda b, h, s: (b * ns + s, group * blocks_per_group + h))

    def cwspec(group):
        return pl.BlockSpec((CONV_WIDTH, width), lambda b, h, s: (0, group * blocks_per_group + h))

    return pl.pallas_call(
        functools.partial(_gdn_kernel, n_heads=n_heads),
        grid=(bsz, nhg, ns),
        in_specs=[
            colspec(0), colspec(1), colspec(2), colspec(3),
            pl.BlockSpec((tb, LANES), lambda b, h, s: (b * ns + s, 0)),
            pl.BlockSpec((rows_t, tb), lambda b, h, s: (0, b * ns + s)),
            cwspec(0), cwspec(1), cwspec(2),
            pl.BlockSpec((1, HEAD_DIM), lambda b, h, s: (0, 0)),
        ],
        out_specs=pl.BlockSpec((tb, width), lambda b, h, s: (b * ns + s, h)),
        out_shape=jax.ShapeDtypeStruct((t, d_model), BF16),
        scratch_shapes=[
            pltpu.VMEM((GDN_HG, HEAD_DIM, HEAD_DIM), F32),
            pltpu.VMEM((2 * SUBLANES, width), F32),
            pltpu.VMEM((2 * SUBLANES, width), F32),
            pltpu.VMEM((2 * SUBLANES, width), F32),
            pltpu.VMEM((GDN_HG, tb, HEAD_DIM), F32),
        ],
        compiler_params=_cparams(("parallel", "parallel", "arbitrary")),
        name="gated_delta",
    )(big, big, big, big, small, small_t, conv_w, conv_w, conv_w, g_out)


def _t5_bucket(n):
    max_exact = N_BUCKETS // 2
    nf = jnp.maximum(n, 1).astype(F32)
    large = max_exact + (jnp.log(nf / max_exact) / math.log(MAX_DISTANCE / max_exact)
                         * (N_BUCKETS - max_exact)).astype(jnp.int32)
    large = jnp.minimum(large, N_BUCKETS - 1)
    return jnp.where(n < max_exact, n, large)


def _attn_kernel(rb_ref, q_ref, k_ref, v_ref, lam_ref, gsub_ref, o_ref,
                 bias_ref, m_ref, acc_ref, sa_ref, sb_ref, *, lam_init):
    hg = pl.program_id(0)
    b = pl.program_id(1)
    qi = pl.program_id(2)
    bq, bk = ATT_BQ, ATT_BK
    heads = range(ATT_HG)
    hsl = [slice(hh * HEAD_DIM, (hh + 1) * HEAD_DIM) for hh in heads]

    @pl.when(jnp.logical_and(b == 0, qi == 0))
    def _():
        blk = LANES
        i = lax.broadcasted_iota(jnp.int32, (blk, blk), 0)
        jj = lax.broadcasted_iota(jnp.int32, (blk, blk), 1)
        for hh in heads:
            head = hg * ATT_HG + hh
            far = rb_ref[N_BUCKETS - 1, head]

            def toeplitz(offset):
                bucket = _t5_bucket(jnp.maximum(i - jj + offset, 0))
                out = jnp.zeros((blk, blk), F32)
                for cc in range(N_BUCKETS):
                    out = jnp.where(bucket == cc, rb_ref[cc, head] - far, out)
                return out

            on_diag = jnp.where(i >= jj, toeplitz(0), NEG_BIG)
            next_diag = toeplitz(blk)
            kinds = {0: on_diag, 1: next_diag}
            bias_ref[hh, 2] = jnp.zeros((bq, bk), F32)
            for slot in range(2):
                for rr in range(bq // blk):
                    for cc in range(bk // blk):
                        delta = rr - cc + slot * (bk // blk)
                        if delta < 0:
                            tile = jnp.full((blk, blk), NEG_BIG, F32)
                        else:
                            tile = kinds.get(delta, jnp.zeros((blk, blk), F32))
                        bias_ref[hh, slot, rr * blk:(rr + 1) * blk, cc * blk:(cc + 1) * blk] = tile

    m_ref[...] = jnp.full(m_ref.shape, NEG_BIG, F32)
    acc_ref[...] = jnp.zeros(acc_ref.shape, F32)

    lane = lax.broadcasted_iota(jnp.int32, (bq, HEAD_DIM), 1)
    qs = []
    for hs in hsl:
        q = q_ref[:, hs]
        zero = jnp.zeros_like(q)
        qs.append(jnp.concatenate([jnp.where(lane < DH_DIFF, q, zero),
                                   jnp.where(lane < DH_DIFF, zero, q)], axis=0))
    ones_col = (lax.broadcasted_iota(jnp.int32, (bk, HEAD_DIM), 1) == 0).astype(BF16)

    def scores(j, s_ref):
        ks = pl.multiple_of(j * bk, bk)
        for hh in heads:
            s_ref[hh] = lax.dot_general(qs[hh], k_ref[pl.ds(ks, bk), hsl[hh]], (((1,), (1,)), ((), ())),
                                        preferred_element_type=F32)

    def absorb(j, s_ref, biased=True):
        ks = pl.multiple_of(j * bk, bk)
        v_exts = [jnp.concatenate([v_ref[pl.ds(ks, bk), hs], ones_col], axis=1) for hs in hsl]
        if biased:
            slot = jnp.minimum(qi - j, 2)
            scs = [jnp.concatenate([s_ref[hh, 0:bq, :] + bias_ref[hh, slot],
                                    s_ref[hh, bq:2 * bq, :] + bias_ref[hh, slot]], axis=0) for hh in heads]
        else:
            scs = [s_ref[hh] for hh in heads]
        m_olds = [m_ref[hh] for hh in heads]
        m_news = [jnp.maximum(mo, jnp.max(sc, axis=-1, keepdims=True)) for mo, sc in zip(m_olds, scs)]
        ps = [jnp.exp(sc - mn) for sc, mn in zip(scs, m_news)]
        pvs = [jnp.dot(p.astype(BF16), ve, preferred_element_type=F32) for p, ve in zip(ps, v_exts)]
        for hh in heads:
            acc_ref[hh] = jnp.exp(m_olds[hh] - m_news[hh]) * acc_ref[hh] + pvs[hh]
            m_ref[hh] = m_news[hh]

    n_tiles = qi + 1
    scores(0, sa_ref)

    def pair_body(jj, carry, biased):
        j0 = 2 * jj
        scores(j0 + 1, sb_ref)
        absorb(j0, sa_ref, biased)
        scores(jnp.minimum(j0 + 2, qi), sa_ref)
        absorb(j0 + 1, sb_ref, biased)
        return carry

    n_far_pairs = jnp.maximum(qi - 1, 0) // 2
    lax.fori_loop(0, n_far_pairs, functools.partial(pair_body, biased=False), 0)
    lax.fori_loop(n_far_pairs, n_tiles // 2, functools.partial(pair_body, biased=True), 0)

    @pl.when(n_tiles % 2 == 1)
    def _():
        absorb(qi, sa_ref)

    lam_p = lam_ref[...]
    s1 = jnp.sum(lam_p[0:1] * lam_p[1:2], axis=-1, keepdims=True)
    s2 = jnp.sum(lam_p[2:3] * lam_p[3:4], axis=-1, keepdims=True)
    lam = jnp.exp(s1) - jnp.exp(s2) + lam_init
    for hh in heads:
        acc = acc_ref[hh]
        num = acc[:, :HEAD_DIM]
        den = acc[:, HEAD_DIM:HEAD_DIM + 1]
        o = num[:bq] / den[:bq] - lam * (num[bq:] / den[bq:])
        o = o * lax.rsqrt(jnp.mean(o * o, axis=-1, keepdims=True) + EPS) * gsub_ref[...]
        o_ref[:, hsl[hh]] = (o * (1.0 - lam_init)).astype(o_ref.dtype)


def _diff_attention(proj_qk, proj_plain, rel_bias, lam_params, g_subln, bsz, seq, n_heads, d_model,
                    lam_init):
    t = bsz * seq
    nq = seq // ATT_BQ
    assert ATT_BQ == ATT_BK and MAX_DISTANCE <= LANES and n_heads % ATT_HG == 0
    width = ATT_HG * HEAD_DIM
    per = d_model // width
    vcol = 4 * per
    return pl.pallas_call(
        functools.partial(_attn_kernel, lam_init=lam_init),
        grid=(n_heads // ATT_HG, bsz, nq),
        in_specs=[
            pl.BlockSpec(memory_space=pltpu.SMEM),
            pl.BlockSpec((ATT_BQ, width), lambda h, b, i: (b * nq + i, h)),
            pl.BlockSpec((seq, width), lambda h, b, i: (b, per + h)),
            pl.BlockSpec((seq, width), lambda h, b, i: (b, vcol + h)),
            pl.BlockSpec((4, DH_DIFF), lambda h, b, i: (0, 0)),
            pl.BlockSpec((1, HEAD_DIM), lambda h, b, i: (0, 0)),
        ],
        out_specs=pl.BlockSpec((ATT_BQ, width), lambda h, b, i: (b * nq + i, h)),
        out_shape=jax.ShapeDtypeStruct((t, d_model), BF16),
        scratch_shapes=[
            pltpu.VMEM((ATT_HG, 3, ATT_BQ, ATT_BK), F32),
            pltpu.VMEM((ATT_HG, 2 * ATT_BQ, 1), F32),
            pltpu.VMEM((ATT_HG, 2 * ATT_BQ, 2 * HEAD_DIM), F32),
            pltpu.VMEM((ATT_HG, 2 * ATT_BQ, ATT_BK), F32),
            pltpu.VMEM((ATT_HG, 2 * ATT_BQ, ATT_BK), F32),
        ],
        compiler_params=_cparams(("arbitrary", "arbitrary", "arbitrary")),
        name="diff_attention",
    )(rel_bias, proj_qk, proj_qk, proj_plain, lam_params, g_subln)


def _mix_kernel(ga_ref, gb_ref, oa_ref, od_ref, x_ref, wo_ref, gffn_ref, wr_ref, br_ref,
                x1_ref, h2_ref, topi_ref, topw_ref, rank_ref, cnt_ref, carry_ref):
    i = pl.program_id(0)
    tm = MIX_TM

    @pl.when(i == 0)
    def _():
        carry_ref[...] = jnp.zeros_like(carry_ref)

    mix = (ga_ref[...].astype(F32) * oa_ref[...].astype(F32)
           + gb_ref[...].astype(F32) * od_ref[...].astype(F32))
    x1 = x_ref[...] + jnp.dot(mix.astype(BF16), wo_ref[...], preferred_element_type=F32)
    x1_ref[...] = x1
    h2 = x1 * lax.rsqrt(jnp.mean(x1 * x1, axis=-1, keepdims=True) + EPS) * gffn_ref[...]
    h2_ref[...] = _pack_halves(h2)

    logits = lax.dot_general(wr_ref[...], h2, (((1,), (1,)), ((), ())),
                             preferred_element_type=F32, precision=lax.Precision.HIGHEST) + br_ref[...]
    eidx = lax.broadcasted_iota(jnp.int32, logits.shape, 0).astype(F32)
    vals, hots = [], []
    cur = logits
    for kk in range(TOP_K):
        mx = jnp.max(cur, axis=0, keepdims=True)
        idx = jnp.min(jnp.where(cur == mx, eidx, float(N_EXPERTS)), axis=0, keepdims=True)
        hot = eidx == idx
        vals.append(mx)
        hots.append(hot)
        topi_ref[kk:kk + 1, :] = idx.astype(jnp.int32)
        cur = jnp.where(hot, -jnp.inf, cur)
    exps = [jnp.exp(vv - vals[0]) for vv in vals]
    denom = exps[0] + exps[1] + exps[2] + exps[3]
    for kk in range(TOP_K):
        topw_ref[kk:kk + 1, :] = exps[kk] / denom

    sel = hots[0]
    for kk in range(1, TOP_K):
        sel = jnp.logical_or(sel, hots[kk])
    sel_f = sel.astype(F32)
    r = lax.broadcasted_iota(jnp.int32, (tm, tm), 0)
    c = lax.broadcasted_iota(jnp.int32, (tm, tm), 1)
    before = _bdot(sel_f, (r < c).astype(F32)) + carry_ref[...]
    for kk in range(TOP_K):
        rank_ref[kk:kk + 1, :] = jnp.sum(jnp.where(hots[kk], before, 0.0), axis=0,
                                         keepdims=True).astype(jnp.int32)
    carry_ref[...] = carry_ref[...] + jnp.sum(sel_f, axis=-1, keepdims=True)
    cnt_ref[...] = carry_ref[...].astype(jnp.int32)


def _mix_project_route(proj_gate, oa, od, x2d, w_o, g_ffn, w_r_t, b_r, d_model):
    t = x2d.shape[0]
    tm = MIX_TM
    full = lambda shape: pl.BlockSpec(shape, lambda i: (0, 0))
    row = lambda: pl.BlockSpec((tm, d_model), lambda i: (i, 0))
    krow = lambda: pl.BlockSpec((TOP_K, tm), lambda i: (0, i))
    return pl.pallas_call(
        _mix_kernel,
        grid=(t // tm,),
        in_specs=[
            pl.BlockSpec((tm, d_model), lambda i: (i, 0)),
            pl.BlockSpec((tm, d_model), lambda i: (i, 1)),
            row(), row(), row(),
            full((d_model, d_model)), full((1, d_model)), full((N_EXPERTS, d_model)), full((N_EXPERTS, 1)),
        ],
        out_specs=[row(), pl.BlockSpec((tm, d_model // 2), lambda i: (i, 0)),
                   krow(), krow(), krow(), full((N_EXPERTS, 1))],
        out_shape=[
            jax.ShapeDtypeStruct((t, d_model), F32),
            jax.ShapeDtypeStruct((t, d_model // 2), jnp.int32),
            jax.ShapeDtypeStruct((TOP_K, t), jnp.int32),
            jax.ShapeDtypeStruct((TOP_K, t), F32),
            jax.ShapeDtypeStruct((TOP_K, t), jnp.int32),
            jax.ShapeDtypeStruct((N_EXPERTS, 1), jnp.int32),
        ],
        scratch_shapes=[pltpu.VMEM((N_EXPERTS, 1), F32)],
        compiler_params=_cparams(("arbitrary",)),
        name="merge_outproj_route",
    )(proj_gate, proj_gate, oa, od, x2d, w_o, g_ffn, w_r_t, b_r)


def _pack_halves(x):
    half = x.shape[1] // 2
    bits = pltpu.bitcast(x.astype(BF16).astype(F32), jnp.int32)
    return bits[:, :half] | lax.shift_right_logical(bits[:, half:], 16)


def _unpack_halves(p):
    hi = pltpu.bitcast(p & jnp.int32(-65536), F32)
    lo = pltpu.bitcast(lax.shift_left(p, 16), F32)
    return jnp.concatenate([hi, lo], axis=1)


def _expert_kernel(be_ref, nu_ref, x_ref, wup_ref, bup_ref, wdn_ref, bdn_ref, *rest):
    y_ref, wup_bf, wdn_bf = rest[-3:]
    i = pl.program_id(0)
    d_ff = wdn_ref.shape[1]

    @pl.when(jnp.logical_or(i == 0, be_ref[i] != be_ref[jnp.maximum(i - 1, 0)]))
    def _():
        rr = lax.broadcasted_iota(jnp.int32, (2 * LANES, 2 * LANES), 0)
        cc = lax.broadcasted_iota(jnp.int32, (2 * LANES, 2 * LANES), 1)
        pick = jnp.where(cc < LANES, 2 * cc, 2 * (cc - LANES) + 1)
        perm = (rr == pick).astype(BF16)
        for g in range(wup_ref.shape[2] // (2 * LANES)):
            cs = slice(g * 2 * LANES, (g + 1) * 2 * LANES)
            wup_bf[:, cs] = jnp.dot(wup_ref[0, :, cs].astype(BF16), perm,
                                    preferred_element_type=F32).astype(BF16)
        wdn_bf[...] = wdn_ref[0].astype(BF16)

    @pl.when(i < nu_ref[0])
    def _():
        x = _unpack_halves(x_ref[...])
        hid = jnp.dot(x.astype(BF16), wup_bf[...], preferred_element_type=F32) + bup_ref[0]
        acts = []
        for g in range(hid.shape[1] // (2 * LANES)):
            glu = jnp.minimum(hid[:, g * 2 * LANES:g * 2 * LANES + LANES], SWIGLU_LIMIT)
            lin = jnp.clip(hid[:, g * 2 * LANES + LANES:(g + 1) * 2 * LANES], -SWIGLU_LIMIT, SWIGLU_LIMIT)
            acts.append(glu * jax.nn.sigmoid(SWIGLU_ALPHA * glu) * (lin + 1.0))
        act = jnp.concatenate(acts, axis=1)
        assert act.shape[1] == d_ff
        y = jnp.dot(act.astype(BF16), wdn_bf[...], preferred_element_type=F32) + bdn_ref[0]
        y_ref[...] = _pack_halves(y)

    @pl.when(i >= nu_ref[0])
    def _():
        y_ref[...] = jnp.zeros(y_ref.shape, y_ref.dtype)


def _experts(block_e, n_used, xs, y_prev, first_block, n_rows_total, w_up, b_up, w_down, b_down):
    n_rows, half = xs.shape
    d = w_up.shape[1]
    nb = n_rows // MOE_RB
    two_ff = w_up.shape[2]
    d_ff = w_down.shape[1]
    in_specs = [
        pl.BlockSpec((MOE_RB, half), lambda i, be, nu: (jnp.maximum(jnp.minimum(i, nu[0] - 1), 0), 0)),
        pl.BlockSpec((1, d, two_ff), lambda i, be, nu: (be[i], 0, 0)),
        pl.BlockSpec((1, 1, two_ff), lambda i, be, nu: (be[i], 0, 0)),
        pl.BlockSpec((1, d_ff, d), lambda i, be, nu: (be[i], 0, 0)),
        pl.BlockSpec((1, 1, d), lambda i, be, nu: (be[i], 0, 0)),
    ]
    operands = [block_e, n_used, xs, w_up, b_up, w_down, b_down]
    aliases = {}
    if y_prev is not None:
        in_specs.append(pl.BlockSpec(memory_space=pl.ANY))
        aliases = {len(operands): 0}
        operands.append(y_prev)
    grid_spec = pltpu.PrefetchScalarGridSpec(
        num_scalar_prefetch=2,
        grid=(nb,),
        in_specs=in_specs,
        out_specs=pl.BlockSpec((MOE_RB, half), lambda i, be, nu: (first_block + i, 0)),
        scratch_shapes=[pltpu.VMEM((d, two_ff), BF16), pltpu.VMEM((d_ff, d), BF16)],
    )
    return pl.pallas_call(
        _expert_kernel,
        grid_spec=grid_spec,
        out_shape=jax.ShapeDtypeStruct((n_rows_total, half), jnp.int32),
        input_output_aliases=aliases,
        compiler_params=_cparams(("arbitrary",)),
        name="moe_experts",
    )(*operands)


def _sc_invert_slots(dest_flat, n_rows):
    n_assign = dest_flat.shape[0]
    n_workers = SC_CORES * SC_SUBCORES
    rows_per_w = n_rows // n_workers
    chunk = SC_SCAN_CHUNK
    assert n_rows % n_workers == 0 and rows_per_w % SC_LANES == 0 and n_assign % chunk == 0
    mesh = plsc.VectorSubcoreMesh(core_axis_name="c", subcore_axis_name="s",
                                  num_cores=SC_CORES, num_subcores=SC_SUBCORES)

    def body(dest_hbm, out_hbm, dest_v, map_v):
        wid = lax.axis_index("s") * SC_CORES + lax.axis_index("c")
        base = wid * rows_per_w
        lanes = lax.broadcasted_iota(jnp.int32, (SC_LANES,), 0)

        @pl.loop(0, rows_per_w, step=SC_LANES)
        def _(r0):
            map_v[pl.ds(r0, SC_LANES)] = jnp.full((SC_LANES,), -1, jnp.int32)

        @pl.loop(0, n_assign // chunk)
        def _(ci):
            pltpu.sync_copy(dest_hbm.at[pl.ds(ci * chunk, chunk)], dest_v)

            @pl.loop(0, chunk, step=SC_LANES)
            def _(j):
                local = dest_v[pl.ds(j, SC_LANES)] - base
                mine = jnp.logical_and(local >= 0, local < rows_per_w)
                plsc.store_scatter(map_v, [jnp.where(mine, local, 0)], ci * chunk + j + lanes, mask=mine)

        pltpu.sync_copy(map_v, out_hbm.at[pl.ds(base, rows_per_w)])

    return pl.kernel(
        body,
        out_type=jax.ShapeDtypeStruct((n_rows,), jnp.int32),
        mesh=mesh,
        scratch_types=[pltpu.VMEM((chunk,), jnp.int32), pltpu.VMEM((rows_per_w,), jnp.int32)],
        compiler_params=pltpu.CompilerParams(needs_layout_passes=False),
        name="moe_slot_inverse",
    )(dest_flat)


def _sc_gather_rows(table, idx):
    n_idx = idx.shape[0]
    d = table.shape[1]
    n_workers = SC_CORES * SC_SUBCORES
    per_worker = n_idx // n_workers
    n_chunks = per_worker // SC_GATHER_ROWS
    assert n_idx % n_workers == 0 and per_worker % SC_GATHER_ROWS == 0
    mesh = plsc.VectorSubcoreMesh(core_axis_name="c", subcore_axis_name="s",
                                  num_cores=SC_CORES, num_subcores=SC_SUBCORES)

    assert n_chunks % 2 == 0

    def body(table_hbm, idx_hbm, out_hbm, idx_v, rows_a, rows_b, sem_a, sem_b):
        wid = lax.axis_index("s") * SC_CORES + lax.axis_index("c")
        base = wid * per_worker
        pltpu.sync_copy(idx_hbm.at[pl.ds(base, per_worker)], idx_v)

        def gather(ci, rows_v, sem):
            off = pl.multiple_of(ci * SC_GATHER_ROWS, SC_GATHER_ROWS)
            return pltpu.make_async_copy(table_hbm.at[idx_v.at[pl.ds(off, SC_GATHER_ROWS)]], rows_v, sem)

        def put(ci, rows_v):
            off = pl.multiple_of(ci * SC_GATHER_ROWS, SC_GATHER_ROWS)
            pltpu.sync_copy(rows_v, out_hbm.at[pl.ds(base + off, SC_GATHER_ROWS)])

        gather(0, rows_a, sem_a).start()

        @pl.loop(0, n_chunks, step=2)
        def _(ci):
            gather(ci + 1, rows_b, sem_b).start()
            gather(ci, rows_a, sem_a).wait()
            put(ci, rows_a)
            nxt = jnp.minimum(ci + 2, n_chunks - 1)
            gather(nxt, rows_a, sem_a).start()
            gather(ci + 1, rows_b, sem_b).wait()
            put(ci + 1, rows_b)

        gather(n_chunks - 1, rows_a, sem_a).wait()

    return pl.kernel(
        body,
        out_type=jax.ShapeDtypeStruct((n_idx, d), table.dtype),
        mesh=mesh,
        scratch_types=[
            pltpu.VMEM((per_worker,), jnp.int32),
            pltpu.VMEM((SC_GATHER_ROWS, d), table.dtype),
            pltpu.VMEM((SC_GATHER_ROWS, d), table.dtype),
            pltpu.SemaphoreType.DMA,
            pltpu.SemaphoreType.DMA,
        ],
        name="moe_slot_gather",
    )(table, idx)


def _combine_kernel(x1_ref, w_ref, y0_ref, y1_ref, y2_ref, y3_ref, *rest):
    o_ref = rest[-1]
    w = w_ref[...]
    out = x1_ref[...]
    for kk, y_ref in enumerate((y0_ref, y1_ref, y2_ref, y3_ref)):
        out = out + w[:, kk:kk + 1] * _unpack_halves(y_ref[...])
    o_ref[...] = out


def _combine(x1, w_tok, y_slots, out_prev, first_tile):
    t, d = x1.shape
    tc = COMB_TC
    nt = y_slots.shape[0] // (TOP_K * tc)
    yspec = lambda kk: pl.BlockSpec((tc, d // 2), lambda i: (kk * nt + i, 0))
    in_specs = [
        pl.BlockSpec((tc, d), lambda i: (first_tile + i, 0)),
        pl.BlockSpec((tc, TOP_K), lambda i: (first_tile + i, 0)),
        yspec(0), yspec(1), yspec(2), yspec(3),
    ]
    operands = [x1, w_tok, y_slots, y_slots, y_slots, y_slots]
    aliases = {}
    if out_prev is not None:
        in_specs.append(pl.BlockSpec(memory_space=pl.ANY))
        aliases = {len(operands): 0}
        operands.append(out_prev)
    return pl.pallas_call(
        _combine_kernel,
        grid=(nt,),
        in_specs=in_specs,
        out_specs=pl.BlockSpec((tc, d), lambda i: (first_tile + i, 0)),
        out_shape=jax.ShapeDtypeStruct((t, d), F32),
        input_output_aliases=aliases,
        compiler_params=_cparams(("parallel",)),
        name="moe_combine",
    )(*operands)


def _moe(x1, h2, topi, topw, rank, counts, w_up, b_up, w_down, b_down):
    t, d = x1.shape
    n_assign = t * TOP_K
    nb = -(-n_assign // MOE_RB) + N_EXPERTS
    n_rows = nb * MOE_RB
    counts = counts[:, 0]
    padded = (counts + MOE_RB - 1) // MOE_RB * MOE_RB
    padded_end = jnp.cumsum(padded)
    padded_start = padded_end - padded
    expert_ids = jnp.arange(N_EXPERTS, dtype=jnp.int32)[:, None, None]
    start_of = jnp.sum(jnp.where(topi[None] == expert_ids, padded_start[:, None, None], 0), axis=0)
    dest = (start_of + rank).astype(jnp.int32)
    n_used = (padded_end[-1] // MOE_RB).astype(jnp.int32)
    blk = jnp.minimum(jnp.arange(nb, dtype=jnp.int32), n_used - 1)
    block_e = jnp.minimum(jnp.sum(padded_end[None, :] <= (blk * MOE_RB)[:, None], axis=1),
                          N_EXPERTS - 1).astype(jnp.int32)
    slot_of = _sc_invert_slots(dest.reshape(-1), n_rows)
    src_tok = jnp.where(slot_of < 0, jnp.arange(n_rows, dtype=jnp.int32), slot_of) % t

    nb_a = nb // 2
    y_rows = None
    for first, n_blk in ((0, nb_a), (nb_a, nb - nb_a)):
        xs = _sc_gather_rows(h2, lax.slice(src_tok, (first * MOE_RB,), ((first + n_blk) * MOE_RB,)))
        used = jnp.clip(n_used - first, 0, n_blk).reshape(1)
        y_rows = _experts(lax.slice(block_e, (first,), (first + n_blk,)), used, xs, y_rows, first, n_rows,
                          w_up, b_up, w_down, b_down)
    w_tok = topw.T
    t_half = t // 2
    out = None
    for part in range(2):
        idx = lax.slice(dest, (0, part * t_half), (TOP_K, (part + 1) * t_half)).reshape(-1)
        y_slots = _sc_gather_rows(y_rows, idx)
        out = _combine(x1, w_tok, y_slots, out, part * t_half // COMB_TC)
    return out


def kernel(x, g_mix, w_in, b_gate, conv_w, a_log, dt_bias, g_delta_out, q_norm, k_norm, lambda_q1, lambda_k1, lambda_q2, lambda_k2, g_subln, rel_bias, w_o, g_ffn, w_router, b_router, w_up, b_up, w_down, b_down):
    bsz, seq, d = x.shape
    depth = g_mix.shape[0]
    n_heads = d // HEAD_DIM
    t = bsz * seq
    d_ff = w_down.shape[2]
    assert d % PROJ_TN == 0 and t % PROJ_TM == 0 and seq % GDN_TB == 0 and seq % ATT_BQ == 0
    assert t % MIX_TM == 0 and t % COMB_TC == 0 and n_heads % GDN_HG == 0
    assert (t * TOP_K) % MOE_RB == 0
    assert 2 * n_heads <= 2 * SUBLANES

    x2d = x.reshape(t, d)
    for l in range(depth):
        wl = w_in[l]
        c0 = 4 * d
        c1 = c0 + 2 * n_heads
        c2 = c1 + 2 * d
        c3 = c2 + d
        w_small = jnp.pad(wl[:, c0:c1], ((0, 0), (0, LANES - 2 * n_heads)))
        gm = g_mix[l].reshape(1, d)
        w_plain = jnp.concatenate([wl[:, :c0], wl[:, c2:c3]], axis=1).astype(BF16)
        proj_plain = _input_projection(x2d, gm, w_plain, jnp.zeros((1, 5 * d), F32), "plain")
        qk_gain = jnp.concatenate([jnp.tile(q_norm[l] * (DH_DIFF ** -0.5), 2 * n_heads),
                                   jnp.tile(k_norm[l], 2 * n_heads)]).reshape(1, 2 * d)
        proj_qk = _input_projection(x2d, gm, wl[:, c1:c2].astype(BF16), qk_gain, "qknorm")
        proj_gate = _input_projection(x2d, gm, wl[:, c3:].astype(BF16), b_gate[l].reshape(1, 2 * d), "gate")

        head_pad = jnp.zeros((LANES - 2 * n_heads,), F32)
        alog = jnp.concatenate([jnp.zeros((n_heads,), F32), a_log[l], head_pad])
        dtb = jnp.concatenate([jnp.zeros((n_heads,), F32), dt_bias[l], head_pad])
        rows_t = 2 * n_heads
        small, small_t = _small_projection(
            x2d, g_mix[l].reshape(1, d), w_small.astype(BF16), w_small[:, :rows_t].T.astype(BF16),
            alog.reshape(1, LANES), dtb.reshape(1, LANES),
            alog[:rows_t].reshape(rows_t, 1), dtb[:rows_t].reshape(rows_t, 1), n_heads)

        oa = _gated_delta(proj_plain, small, small_t, conv_w[l], g_delta_out[l].reshape(1, HEAD_DIM),
                          bsz, seq, n_heads, d)

        lam_init = 0.8 - 0.6 * math.exp(-0.3 * l)
        lam_params = jnp.stack([lambda_q1[l], lambda_k1[l], lambda_q2[l], lambda_k2[l]])
        od = _diff_attention(proj_qk, proj_plain, rel_bias, lam_params, g_subln[l].reshape(1, HEAD_DIM),
                             bsz, seq, n_heads, d, lam_init)

        x1, h2, topi, topw, rank, counts = _mix_project_route(
            proj_gate, oa, od, x2d, w_o[l].astype(BF16), g_ffn[l].reshape(1, d),
            w_router[l].T, b_router[l].reshape(N_EXPERTS, 1), d)

        b_up_l = b_up[l].reshape(N_EXPERTS, 2 * d_ff // (2 * LANES), LANES, 2)
        b_up_l = jnp.swapaxes(b_up_l, 2, 3).reshape(N_EXPERTS, 1, 2 * d_ff)
        x2d = _moe(x1, h2, topi, topw, rank, counts, w_up[l], b_up_l,
                   w_down[l], b_down[l].reshape(N_EXPERTS, 1, d))
    return x2d.reshape(bsz, seq, d)
```

```python
import functools
import math

import jax
import jax.numpy as jnp
from jax import lax
from jax.experimental import pallas as pl
from jax.experimental.pallas import tpu as pltpu
from jax.experimental.pallas import tpu_sc as plsc

F32 = jnp.float32
BF16 = jnp.bfloat16

HEAD_DIM = 128
DH_DIFF = HEAD_DIM // 2
CONV_WIDTH = 4
CHUNK = 64
N_BUCKETS = 32
MAX_DISTANCE = 128
N_EXPERTS = 32
TOP_K = 4
SWIGLU_LIMIT = 7.0
SWIGLU_ALPHA = 1.702
EPS = 1e-6
NEG_BIG = -1e30

LANES = 128
SUBLANES = 8
VMEM_LIMIT = 56 * 1024 * 1024
SC_CORES = 2
SC_SUBCORES = 16
SC_LANES = 16
SC_GATHER_ROWS = 64
SC_SCAN_CHUNK = 4096

PROJ_TM = 2048
PROJ_TN = 1024
PROJ_CHUNK = 256
GDN_TB = 256
GDN_HG = 8
ATT_HG = 1
ATT_BQ = 512
ATT_BK = 512
MIX_TM = 1024
MIX_PARTS = 2
MOE_RB = 512
COMB_TC = 512


def _cparams(sem):
    return pltpu.CompilerParams(dimension_semantics=sem, vmem_limit_bytes=VMEM_LIMIT)


def _bdot(a, b):
    return jnp.dot(a.astype(BF16), b.astype(BF16), preferred_element_type=F32)


def _bdot_nt(a, b):
    return lax.dot_general(a.astype(BF16), b.astype(BF16), (((1,), (1,)), ((), ())),
                           preferred_element_type=F32)


def _bdot_tn(a, b):
    return lax.dot_general(a.astype(BF16), b.astype(BF16), (((0,), (0,)), ((), ())),
                           preferred_element_type=F32)


def _proj_kernel(x_ref, g_ref, w_ref, aux_ref, o_ref, h_ref, *, mode):
    @pl.when(pl.program_id(1) == 0)
    def _():
        x = x_ref[...]
        ms = jnp.mean(x * x, axis=-1, keepdims=True)
        h_ref[...] = (x * lax.rsqrt(ms + EPS) * g_ref[...]).astype(BF16)

    h = h_ref[...]
    lo = lax.broadcasted_iota(jnp.int32, (1, LANES), 1) < DH_DIFF
    for c in range(PROJ_TN // PROJ_CHUNK):
        cs = slice(c * PROJ_CHUNK, (c + 1) * PROJ_CHUNK)
        acc = jnp.dot(h, w_ref[:, cs], preferred_element_type=F32)
        if mode == "plain":
            o_ref[:, cs] = acc.astype(o_ref.dtype)
        elif mode == "gate":
            o_ref[:, cs] = jax.nn.sigmoid(acc + aux_ref[:, cs]).astype(o_ref.dtype)
        else:
            for g in range(PROJ_CHUNK // LANES):
                sl = slice(c * PROJ_CHUNK + g * LANES, c * PROJ_CHUNK + (g + 1) * LANES)
                y = acc[:, g * LANES:(g + 1) * LANES]
                y2 = y * y
                s_lo = jnp.sum(jnp.where(lo, y2, 0.0), axis=-1, keepdims=True)
                s_hi = jnp.sum(jnp.where(lo, 0.0, y2), axis=-1, keepdims=True)
                r = jnp.where(lo, lax.rsqrt(s_lo / DH_DIFF + EPS), lax.rsqrt(s_hi / DH_DIFF + EPS))
                o_ref[:, sl] = (y * r * aux_ref[:, sl]).astype(o_ref.dtype)


def _input_projection(x2d, g_mix, w, aux, mode):
    t, d = x2d.shape
    n = w.shape[1]
    return pl.pallas_call(
        functools.partial(_proj_kernel, mode=mode),
        grid=(t // PROJ_TM, n // PROJ_TN),
        in_specs=[
            pl.BlockSpec((PROJ_TM, d), lambda i, j: (i, 0)),
            pl.BlockSpec((1, d), lambda i, j: (0, 0)),
            pl.BlockSpec((d, PROJ_TN), lambda i, j: (0, j)),
            pl.BlockSpec((1, PROJ_TN), lambda i, j: (0, j)),
        ],
        out_specs=pl.BlockSpec((PROJ_TM, PROJ_TN), lambda i, j: (i, j)),
        out_shape=jax.ShapeDtypeStruct((t, n), BF16),
        scratch_shapes=[pltpu.VMEM((PROJ_TM, d), BF16)],
        compiler_params=_cparams(("parallel", "arbitrary")),
        name="input_projection_" + mode,
    )(x2d, g_mix, w, aux)


def _small_proj_kernel(x_ref, g_ref, w_ref, wt_ref, alog_ref, dtb_ref, alog_t_ref, dtb_t_ref,
                       o_ref, ot_ref, *, n_heads):
    x = x_ref[...]
    ms = jnp.mean(x * x, axis=-1, keepdims=True)
    h = (x * lax.rsqrt(ms + EPS) * g_ref[...]).astype(BF16)

    def finish(acc, idx, alog, dtb):
        beta = jax.nn.sigmoid(acc)
        z = acc + dtb
        softplus = jnp.maximum(z, 0.0) + jnp.log1p(jnp.exp(-jnp.abs(z)))
        gdec = -jnp.exp(alog) * softplus
        return jnp.where(idx < n_heads, beta, jnp.where(idx < 2 * n_heads, gdec, 0.0))

    acc = jnp.dot(h, w_ref[...], preferred_element_type=F32)
    lane = lax.broadcasted_iota(jnp.int32, acc.shape, 1)
    o_ref[...] = finish(acc, lane, alog_ref[...], dtb_ref[...])
    acc_t = lax.dot_general(wt_ref[...], h, (((1,), (1,)), ((), ())),
                            preferred_element_type=F32)
    sub = lax.broadcasted_iota(jnp.int32, acc_t.shape, 0)
    ot_ref[...] = finish(acc_t, sub, alog_t_ref[...], dtb_t_ref[...])


def _small_projection(x2d, g_mix, w_small, w_small_t, alog, dtb, alog_t, dtb_t, n_heads):
    t, d = x2d.shape
    rows_t = w_small_t.shape[0]
    tm = PROJ_TM
    full = lambda shape: pl.BlockSpec(shape, lambda i: (0, 0))
    return pl.pallas_call(
        functools.partial(_small_proj_kernel, n_heads=n_heads),
        grid=(t // tm,),
        in_specs=[
            pl.BlockSpec((tm, d), lambda i: (i, 0)),
            full((1, d)), full((d, LANES)), full((rows_t, d)),
            full((1, LANES)), full((1, LANES)), full((rows_t, 1)), full((rows_t, 1)),
        ],
        out_specs=[pl.BlockSpec((tm, LANES), lambda i: (i, 0)),
                   pl.BlockSpec((rows_t, tm), lambda i: (0, i))],
        out_shape=[jax.ShapeDtypeStruct((t, LANES), F32),
                   jax.ShapeDtypeStruct((rows_t, t), F32)],
        compiler_params=_cparams(("parallel",)),
        name="beta_decay_projection",
    )(x2d, g_mix, w_small, w_small_t, alog, dtb, alog_t, dtb_t)


def _gdn_kernel(q_ref, k_ref, v_ref, z_ref, sm_ref, smt_ref, cwq_ref, cwk_ref, cwv_ref, gout_ref,
                o_ref, state_ref, qp_ref, kp_ref, vp_ref, vn_ref, *, n_heads):
    hg = pl.program_id(1)
    s = pl.program_id(2)
    tb = GDN_TB
    pad = SUBLANES
    width = GDN_HG * HEAD_DIM

    @pl.when(s == 0)
    def _():
        state_ref[...] = jnp.zeros_like(state_ref)
        for p_ref in (qp_ref, kp_ref, vp_ref):
            p_ref[0:pad, :] = jnp.zeros((pad, width), F32)

    r = lax.broadcasted_iota(jnp.int32, (tb, tb), 0)
    c = lax.broadcasted_iota(jnp.int32, (tb, tb), 1)
    delay_mat = jnp.concatenate([(r - c == dd).astype(BF16) for dd in range(1, CONV_WIDTH)], axis=0)

    def conv_silu(x_ref, p_ref, cw_ref):
        x = x_ref[...]
        xf = x.astype(F32)
        p_ref[pad:2 * pad, :] = xf[0:pad]
        delayed = jnp.dot(delay_mat, x, preferred_element_type=F32)
        acc = cw_ref[CONV_WIDTH - 1:CONV_WIDTH, :] * xf
        for dd in range(1, CONV_WIDTH):
            first = p_ref[pad - dd:2 * pad - dd, :]
            xd = jnp.concatenate([first, delayed[(dd - 1) * tb + pad:dd * tb]], axis=0)
            acc = acc + cw_ref[CONV_WIDTH - 1 - dd:CONV_WIDTH - dd, :] * xd
        p_ref[0:pad, :] = xf[tb - pad:tb]
        return acc * jax.nn.sigmoid(acc)

    q_all = conv_silu(q_ref, qp_ref, cwq_ref)
    k_all = conv_silu(k_ref, kp_ref, cwk_ref)
    v_all = conv_silu(v_ref, vp_ref, cwv_ref)

    shift = int(math.log2(CHUNK))
    same = (r >> shift) == (c >> shift)
    incl = jnp.logical_and(same, c <= r)
    strict = jnp.logical_and(same, c < r)

    small = sm_ref[...]
    small_t = smt_ref[...]
    lane = lax.broadcasted_iota(jnp.int32, small.shape, 1)
    def split3(a):
        hi = a.astype(BF16)
        r1 = a - hi.astype(F32)
        mid = r1.astype(BF16)
        lo = (r1 - mid.astype(F32)).astype(BF16)
        return hi.astype(F32), mid.astype(F32), lo.astype(F32)

    part = 2 * n_heads
    s_hi, s_mid, s_lo = split3(small)
    small3 = jnp.where(lane < part, s_hi,
                       jnp.where(lane < 2 * part, pltpu.roll(s_mid, part, 1),
                                 jnp.where(lane < 3 * part, pltpu.roll(s_lo, 2 * part, 1), 0.0)))
    both = _bdot(jnp.concatenate([incl.astype(F32), same.astype(F32)], axis=0), small3)
    gcum = both[:tb]
    gtot = both[tb:]
    gcum_t = _bdot(jnp.concatenate(split3(small_t), axis=0),
                   jnp.logical_and(same, r <= c).astype(F32))
    sub3 = lax.broadcasted_iota(jnp.int32, gcum_t.shape, 0)

    heads = range(GDN_HG)
    hsl = [slice(hh * HEAD_DIM, (hh + 1) * HEAD_DIM) for hh in heads]
    qs = [q_all[:, hs] for hs in hsl]
    ks = [k_all[:, hs] for hs in hsl]
    vs = [v_all[:, hs] for hs in hsl]
    qs = [q * lax.rsqrt(jnp.sum(q * q, axis=-1, keepdims=True) + EPS) * (HEAD_DIM ** -0.5) for q in qs]
    ks = [k * lax.rsqrt(jnp.sum(k * k, axis=-1, keepdims=True) + EPS) for k in ks]

    def col_of(arr, idx):
        return jnp.sum(jnp.where(lane == idx, arr, 0.0), axis=-1, keepdims=True)

    def terms_of(pos, idx):
        return jnp.logical_or(pos == idx, jnp.logical_or(pos == idx + part, pos == idx + 2 * part))

    head_ids = [hg * GDN_HG + hh for hh in heads]
    betas = [col_of(small, hd) for hd in head_ids]
    gcs = [jnp.sum(jnp.where(terms_of(lane, hd + n_heads), gcum, 0.0), axis=-1, keepdims=True)
           for hd in head_ids]
    gls = [jnp.sum(jnp.where(terms_of(lane, hd + n_heads), gtot, 0.0), axis=-1, keepdims=True)
           for hd in head_ids]
    gc_rows = [jnp.sum(jnp.where(terms_of(sub3, hd + n_heads), gcum_t, 0.0), axis=0, keepdims=True)
               for hd in head_ids]

    decays = [jnp.where(incl, jnp.exp(jnp.minimum(gc - gr, 0.0)), 0.0) for gc, gr in zip(gcs, gc_rows)]
    kbs = [k * b for k, b in zip(ks, betas)]
    kks = [_bdot_nt(kb, k) for kb, k in zip(kbs, ks)]
    pws = [jnp.where(strict, -(kk * dc), 0.0) for kk, dc in zip(kks, decays)]
    n_chunks = tb // CHUNK
    cat_row = lax.broadcasted_iota(jnp.int32, (CHUNK, tb), 0)
    cat_lane = lax.broadcasted_iota(jnp.int32, (CHUNK, tb), 1)
    lane_chunk = cat_lane >> shift

    def block_diag(m_cat):
        return jnp.concatenate([jnp.where(lane_chunk == ci, m_cat, 0.0) for ci in range(n_chunks)], axis=0)

    def cat_of(m_bd):
        out = m_bd[0:CHUNK]
        for ci in range(1, n_chunks):
            out = out + m_bd[ci * CHUNK:(ci + 1) * CHUNK]
        return out

    pcats = [cat_of(pw) for pw in pws]
    eye_cat = ((cat_lane & (CHUNK - 1)) == cat_row).astype(F32)
    tcats = [eye_cat + pc for pc in pcats]
    pcats = [_bdot(pc, block_diag(pc)) for pc in pcats]
    n_levels = int(math.log2(CHUNK))
    for lev in range(1, n_levels):
        bds = [block_diag(pc) for pc in pcats]
        if lev < n_levels - 1:
            prods = [_bdot(jnp.concatenate([pc, tc], axis=0), bd) for pc, tc, bd in zip(pcats, tcats, bds)]
            pcats = [pr[:CHUNK] for pr in prods]
            tcats = [tc + pr[CHUNK:] for tc, pr in zip(tcats, prods)]
        else:
            tcats = [tc + _bdot(tc, bd) for tc, bd in zip(tcats, bds)]
    tmats = [block_diag(tc) for tc in tcats]
    egcs = [jnp.exp(gc) for gc in gcs]
    uws = [_bdot(tm, jnp.concatenate([v * b, kb * eg], axis=1))
           for tm, v, b, kb, eg in zip(tmats, vs, betas, kbs, egcs)]
    us = [uw[:, :HEAD_DIM] for uw in uws]
    ws = [uw[:, HEAD_DIM:] for uw in uws]
    qkm = [_bdot_nt(q, k) for q, k in zip(qs, ks)]
    qkm = [jnp.where(incl, x * dc, 0.0) for x, dc in zip(qkm, decays)]
    q_decs = [q * eg for q, eg in zip(qs, egcs)]
    k_ends = [k * jnp.exp(gl - gc) for k, gl, gc in zip(ks, gls, gcs)]

    for hh in heads:
        vn_ref[hh] = jnp.zeros((tb, HEAD_DIM), F32)
    outs = [[] for _ in heads]
    for ci in range(tb // CHUNK):
        cs = slice(ci * CHUNK, (ci + 1) * CHUNK)
        sts = [state_ref[hh] for hh in heads]
        ws_qs = [_bdot(jnp.concatenate([ws[hh][cs], q_decs[hh][cs]], axis=0), sts[hh]) for hh in heads]
        v_news = [us[hh][cs] - ws_qs[hh][:CHUNK] for hh in heads]
        for hh in heads:
            vn_ref[hh, cs, :] = v_news[hh]
        intra = [_bdot(qkm[hh][cs], vn_ref[hh]) for hh in heads]
        upd = [_bdot_tn(k_ends[hh][cs], v_news[hh]) for hh in heads]
        for hh in heads:
            outs[hh].append(ws_qs[hh][CHUNK:] + intra[hh])
            g_last = gls[hh][ci * CHUNK:ci * CHUNK + 1, :]
            state_ref[hh] = sts[hh] * jnp.exp(g_last) + upd[hh]
    for hh in heads:
        o = jnp.concatenate(outs[hh], axis=0)
        o = o * lax.rsqrt(jnp.mean(o * o, axis=-1, keepdims=True) + EPS) * gout_ref[...]
        zz = z_ref[:, hsl[hh]].astype(F32)
        o_ref[:, hsl[hh]] = (o * (zz * jax.nn.sigmoid(zz))).astype(o_ref.dtype)


def _gated_delta(big, small, small_t, conv_w, g_out, bsz, seq, n_heads, d_model):
    t = bsz * seq
    tb = GDN_TB
    ns = seq // tb
    width = GDN_HG * HEAD_DIM
    nhg = n_heads // GDN_HG
    blocks_per_group = d_model // width
    rows_t = small_t.shape[0]

    def colspec(group):
        return pl.BlockSpec((tb, width), lambda b, h, s: (b * ns + s, group * blocks_per_group + h))

    def cwspec(group):
        return pl.BlockSpec((CONV_WIDTH, width), lambda b, h, s: (0, group * blocks_per_group + h))

    return pl.pallas_call(
        functools.partial(_gdn_kernel, n_heads=n_heads),
        grid=(bsz, nhg, ns),
        in_specs=[
            colspec(0), colspec(1), colspec(2), colspec(3),
            pl.BlockSpec((tb, LANES), lambda b, h, s: (b * ns + s, 0)),
            pl.BlockSpec((rows_t, tb), lambda b, h, s: (0, b * ns + s)),
            cwspec(0), cwspec(1), cwspec(2),
            pl.BlockSpec((1, HEAD_DIM), lambda b, h, s: (0, 0)),
        ],
        out_specs=pl.BlockSpec((tb, width), lambda b, h, s: (b * ns + s, h)),
        out_shape=jax.ShapeDtypeStruct((t, d_model), BF16),
        scratch_shapes=[
            pltpu.VMEM((GDN_HG, HEAD_DIM, HEAD_DIM), F32),
            pltpu.VMEM((2 * SUBLANES, width), F32),
            pltpu.VMEM((2 * SUBLANES, width), F32),
            pltpu.VMEM((2 * SUBLANES, width), F32),
            pltpu.VMEM((GDN_HG, tb, HEAD_DIM), F32),
        ],
        compiler_params=_cparams(("parallel", "parallel", "arbitrary")),
        name="gated_delta",
    )(big, big, big, big, small, small_t, conv_w, conv_w, conv_w, g_out)


def _t5_bucket(n):
    max_exact = N_BUCKETS // 2
    nf = jnp.maximum(n, 1).astype(F32)
    large = max_exact + (jnp.log(nf / max_exact) / math.log(MAX_DISTANCE / max_exact)
                         * (N_BUCKETS - max_exact)).astype(jnp.int32)
    large = jnp.minimum(large, N_BUCKETS - 1)
    return jnp.where(n < max_exact, n, large)


def _attn_kernel(rb_ref, q_ref, k_ref, v_ref, lam_ref, gsub_ref, o_ref,
                 bias_ref, m_ref, acc_ref, sa_ref, sb_ref, *, lam_init):
    hg = pl.program_id(0)
    b = pl.program_id(1)
    qi = pl.program_id(2)
    bq, bk = ATT_BQ, ATT_BK
    heads = range(ATT_HG)
    hsl = [slice(hh * HEAD_DIM, (hh + 1) * HEAD_DIM) for hh in heads]

    @pl.when(jnp.logical_and(b == 0, qi == 0))
    def _():
        blk = LANES
        i = lax.broadcasted_iota(jnp.int32, (blk, blk), 0)
        jj = lax.broadcasted_iota(jnp.int32, (blk, blk), 1)
        for hh in heads:
            head = hg * ATT_HG + hh
            far = rb_ref[N_BUCKETS - 1, head]

            def toeplitz(offset):
                bucket = _t5_bucket(jnp.maximum(i - jj + offset, 0))
                out = jnp.zeros((blk, blk), F32)
                for cc in range(N_BUCKETS):
                    out = jnp.where(bucket == cc, rb_ref[cc, head] - far, out)
                return out

            on_diag = jnp.where(i >= jj, toeplitz(0), NEG_BIG)
            next_diag = toeplitz(blk)
            kinds = {0: on_diag, 1: next_diag}
            bias_ref[hh, 2] = jnp.zeros((bq, bk), F32)
            for slot in range(2):
                for rr in range(bq // blk):
                    for cc in range(bk // blk):
                        delta = rr - cc + slot * (bk // blk)
                        if delta < 0:
                            tile = jnp.full((blk, blk), NEG_BIG, F32)
                        else:
                            tile = kinds.get(delta, jnp.zeros((blk, blk), F32))
                        bias_ref[hh, slot, rr * blk:(rr + 1) * blk, cc * blk:(cc + 1) * blk] = tile

    m_ref[...] = jnp.full(m_ref.shape, NEG_BIG, F32)
    acc_ref[...] = jnp.zeros(acc_ref.shape, F32)

    lane = lax.broadcasted_iota(jnp.int32, (bq, HEAD_DIM), 1)
    qs = []
    for hs in hsl:
        q = q_ref[:, hs]
        zero = jnp.zeros_like(q)
        qs.append(jnp.concatenate([jnp.where(lane < DH_DIFF, q, zero),
                                   jnp.where(lane < DH_DIFF, zero, q)], axis=0))
    ones_col = (lax.broadcasted_iota(jnp.int32, (bk, HEAD_DIM), 1) == 0).astype(BF16)

    def scores(j, s_ref):
        ks = pl.multiple_of(j * bk, bk)
        for hh in heads:
            s_ref[hh] = lax.dot_general(qs[hh], k_ref[pl.ds(ks, bk), hsl[hh]], (((1,), (1,)), ((), ())),
                                        preferred_element_type=F32)

    def absorb(j, s_ref, biased=True):
        ks = pl.multiple_of(j * bk, bk)
        v_exts = [jnp.concatenate([v_ref[pl.ds(ks, bk), hs], ones_col], axis=1) for hs in hsl]
        if biased:
            slot = jnp.minimum(qi - j, 2)
            scs = [jnp.concatenate([s_ref[hh, 0:bq, :] + bias_ref[hh, slot],
                                    s_ref[hh, bq:2 * bq, :] + bias_ref[hh, slot]], axis=0) for hh in heads]
        else:
            scs = [s_ref[hh] for hh in heads]
        m_olds = [m_ref[hh] for hh in heads]
        m_news = [jnp.maximum(mo, jnp.max(sc, axis=-1, keepdims=True)) for mo, sc in zip(m_olds, scs)]
        ps = [jnp.exp(sc - mn) for sc, mn in zip(scs, m_news)]
        pvs = [jnp.dot(p.astype(BF16), ve, preferred_element_type=F32) for p, ve in zip(ps, v_exts)]
        for hh in heads:
            acc_ref[hh] = jnp.exp(m_olds[hh] - m_news[hh]) * acc_ref[hh] + pvs[hh]
            m_ref[hh] = m_news[hh]

    n_tiles = qi + 1
    scores(0, sa_ref)

    def pair_body(jj, carry, biased):
        j0 = 2 * jj
        scores(j0 + 1, sb_ref)
        absorb(j0, sa_ref, biased)
        scores(jnp.minimum(j0 + 2, qi), sa_ref)
        absorb(j0 + 1, sb_ref, biased)
        return carry

    n_far_pairs = jnp.maximum(qi - 1, 0) // 2
    lax.fori_loop(0, n_far_pairs, functools.partial(pair_body, biased=False), 0)
    lax.fori_loop(n_far_pairs, n_tiles // 2, functools.partial(pair_body, biased=True), 0)

    @pl.when(n_tiles % 2 == 1)
    def _():
        absorb(qi, sa_ref)

    lam_p = lam_ref[...]
    s1 = jnp.sum(lam_p[0:1] * lam_p[1:2], axis=-1, keepdims=True)
    s2 = jnp.sum(lam_p[2:3] * lam_p[3:4], axis=-1, keepdims=True)
    lam = jnp.exp(s1) - jnp.exp(s2) + lam_init
    for hh in heads:
        acc = acc_ref[hh]
        num = acc[:, :HEAD_DIM]
        den = acc[:, HEAD_DIM:HEAD_DIM + 1]
        o = num[:bq] / den[:bq] - lam * (num[bq:] / den[bq:])
        o = o * lax.rsqrt(jnp.mean(o * o, axis=-1, keepdims=True) + EPS) * gsub_ref[...]
        o_ref[:, hsl[hh]] = (o * (1.0 - lam_init)).astype(o_ref.dtype)


def _diff_attention(proj_qk, proj_plain, rel_bias, lam_params, g_subln, bsz, seq, n_heads, d_model,
                    lam_init):
    t = bsz * seq
    nq = seq // ATT_BQ
    assert ATT_BQ == ATT_BK and MAX_DISTANCE <= LANES and n_heads % ATT_HG == 0
    width = ATT_HG * HEAD_DIM
    per = d_model // width
    vcol = 4 * per
    return pl.pallas_call(
        functools.partial(_attn_kernel, lam_init=lam_init),
        grid=(n_heads // ATT_HG, bsz, nq),
        in_specs=[
            pl.BlockSpec(memory_space=pltpu.SMEM),
            pl.BlockSpec((ATT_BQ, width), lambda h, b, i: (b * nq + i, h)),
            pl.BlockSpec((seq, width), lambda h, b, i: (b, per + h)),
            pl.BlockSpec((seq, width), lambda h, b, i: (b, vcol + h)),
            pl.BlockSpec((4, DH_DIFF), lambda h, b, i: (0, 0)),
            pl.BlockSpec((1, HEAD_DIM), lambda h, b, i: (0, 0)),
        ],
        out_specs=pl.BlockSpec((ATT_BQ, width), lambda h, b, i: (b * nq + i, h)),
        out_shape=jax.ShapeDtypeStruct((t, d_model), BF16),
        scratch_shapes=[
            pltpu.VMEM((ATT_HG, 3, ATT_BQ, ATT_BK), F32),
            pltpu.VMEM((ATT_HG, 2 * ATT_BQ, 1), F32),
            pltpu.VMEM((ATT_HG, 2 * ATT_BQ, 2 * HEAD_DIM), F32),
            pltpu.VMEM((ATT_HG, 2 * ATT_BQ, ATT_BK), F32),
            pltpu.VMEM((ATT_HG, 2 * ATT_BQ, ATT_BK), F32),
        ],
        compiler_params=_cparams(("arbitrary", "arbitrary", "arbitrary")),
        name="diff_attention",
    )(rel_bias, proj_qk, proj_qk, proj_plain, lam_params, g_subln)


def _mix_kernel(ga_ref, gb_ref, oa_ref, od_ref, x_ref, wo_ref, gffn_ref, wr_ref, br_ref,
                x1_ref, h2_ref, topi_ref, topw_ref, rank_ref, cnt_ref, carry_ref):
    i = pl.program_id(0)
    tm = MIX_TM

    @pl.when(i == 0)
    def _():
        carry_ref[...] = jnp.zeros_like(carry_ref)

    tp = tm // MIX_PARTS
    parts = range(MIX_PARTS)
    rows = [slice(pp * tp, (pp + 1) * tp) for pp in parts]
    mixes = [ga_ref[rs, :] * oa_ref[rs, :] + gb_ref[rs, :] * od_ref[rs, :] for rs in rows]
    x1s = [x_ref[rs, :] + jnp.dot(mx, wo_ref[...], preferred_element_type=F32) for rs, mx in zip(rows, mixes)]
    for rs, x1 in zip(rows, x1s):
        x1_ref[rs, :] = x1
    h2s = [x1 * lax.rsqrt(jnp.mean(x1 * x1, axis=-1, keepdims=True) + EPS) * gffn_ref[...] for x1 in x1s]
    for rs, h2 in zip(rows, h2s):
        h2_ref[rs, :] = _pack_halves(h2)

    curs = [lax.dot_general(wr_ref[...], h2, (((1,), (1,)), ((), ())), preferred_element_type=F32,
                            precision=lax.Precision.HIGHEST) + br_ref[...] for h2 in h2s]
    eidx = lax.broadcasted_iota(jnp.int32, curs[0].shape, 0).astype(F32)
    vals = [[] for _ in parts]
    hots = [[] for _ in parts]
    for kk in range(TOP_K):
        mxs = [jnp.max(cur, axis=0, keepdims=True) for cur in curs]
        idxs = [jnp.min(jnp.where(cur == mx, eidx, float(N_EXPERTS)), axis=0, keepdims=True)
                for cur, mx in zip(curs, mxs)]
        for pp in parts:
            hot = eidx == idxs[pp]
            vals[pp].append(mxs[pp])
            hots[pp].append(hot)
            topi_ref[kk:kk + 1, rows[pp]] = idxs[pp].astype(jnp.int32)
            curs[pp] = jnp.where(hot, -jnp.inf, curs[pp])
    for pp in parts:
        exps = [jnp.exp(vv - vals[pp][0]) for vv in vals[pp]]
        denom = exps[0] + exps[1] + exps[2] + exps[3]
        for kk in range(TOP_K):
            topw_ref[kk:kk + 1, rows[pp]] = exps[kk] / denom

    r = lax.broadcasted_iota(jnp.int32, (tp, tp), 0)
    c = lax.broadcasted_iota(jnp.int32, (tp, tp), 1)
    earlier = (r < c).astype(F32)
    sel_fs = []
    for pp in parts:
        sel = hots[pp][0]
        for kk in range(1, TOP_K):
            sel = jnp.logical_or(sel, hots[pp][kk])
        sel_fs.append(sel.astype(F32))
    within = [_bdot(sf, earlier) for sf in sel_fs]
    totals = [jnp.sum(sf, axis=-1, keepdims=True) for sf in sel_fs]
    run = carry_ref[...]
    for pp in parts:
        before = within[pp] + run
        for kk in range(TOP_K):
            rank_ref[kk:kk + 1, rows[pp]] = jnp.sum(jnp.where(hots[pp][kk], before, 0.0), axis=0,
                                                    keepdims=True).astype(jnp.int32)
        run = run + totals[pp]
    carry_ref[...] = run
    cnt_ref[...] = run.astype(jnp.int32)


def _mix_project_route(proj_gate, oa, od, x2d, w_o, g_ffn, w_r_t, b_r, d_model):
    t = x2d.shape[0]
    tm = MIX_TM
    full = lambda shape: pl.BlockSpec(shape, lambda i: (0, 0))
    row = lambda: pl.BlockSpec((tm, d_model), lambda i: (i, 0))
    krow = lambda: pl.BlockSpec((TOP_K, tm), lambda i: (0, i))
    return pl.pallas_call(
        _mix_kernel,
        grid=(t // tm,),
        in_specs=[
            pl.BlockSpec((tm, d_model), lambda i: (i, 0)),
            pl.BlockSpec((tm, d_model), lambda i: (i, 1)),
            row(), row(), row(),
            full((d_model, d_model)), full((1, d_model)), full((N_EXPERTS, d_model)), full((N_EXPERTS, 1)),
        ],
        out_specs=[row(), pl.BlockSpec((tm, d_model // 2), lambda i: (i, 0)),
                   krow(), krow(), krow(), full((N_EXPERTS, 1))],
        out_shape=[
            jax.ShapeDtypeStruct((t, d_model), F32),
            jax.ShapeDtypeStruct((t, d_model // 2), jnp.int32),
            jax.ShapeDtypeStruct((TOP_K, t), jnp.int32),
            jax.ShapeDtypeStruct((TOP_K, t), F32),
            jax.ShapeDtypeStruct((TOP_K, t), jnp.int32),
            jax.ShapeDtypeStruct((N_EXPERTS, 1), jnp.int32),
        ],
        scratch_shapes=[pltpu.VMEM((N_EXPERTS, 1), F32)],
        compiler_params=_cparams(("arbitrary",)),
        name="merge_outproj_route",
    )(proj_gate, proj_gate, oa, od, x2d, w_o, g_ffn, w_r_t, b_r)


def _pack_halves(x):
    half = x.shape[1] // 2
    bits = pltpu.bitcast(x.astype(BF16).astype(F32), jnp.int32)
    return bits[:, :half] | lax.shift_right_logical(bits[:, half:], 16)


def _unpack_halves(p):
    hi = pltpu.bitcast(p & jnp.int32(-65536), F32)
    lo = pltpu.bitcast(lax.shift_left(p, 16), F32)
    return jnp.concatenate([hi, lo], axis=1)


def _expert_kernel(be_ref, nu_ref, x_ref, wup_ref, bup_ref, wdn_ref, bdn_ref, *rest):
    y_ref, wup_bf, wdn_bf = rest[-3:]
    i = pl.program_id(0)
    d_ff = wdn_ref.shape[1]

    @pl.when(jnp.logical_or(i == 0, be_ref[i] != be_ref[jnp.maximum(i - 1, 0)]))
    def _():
        rr = lax.broadcasted_iota(jnp.int32, (2 * LANES, 2 * LANES), 0)
        cc = lax.broadcasted_iota(jnp.int32, (2 * LANES, 2 * LANES), 1)
        pick = jnp.where(cc < LANES, 2 * cc, 2 * (cc - LANES) + 1)
        perm = (rr == pick).astype(BF16)
        for g in range(wup_ref.shape[2] // (2 * LANES)):
            cs = slice(g * 2 * LANES, (g + 1) * 2 * LANES)
            wup_bf[:, cs] = jnp.dot(wup_ref[0, :, cs].astype(BF16), perm,
                                    preferred_element_type=F32).astype(BF16)
        wdn_bf[...] = wdn_ref[0].astype(BF16)

    @pl.when(i < nu_ref[0])
    def _():
        x = _unpack_halves(x_ref[...])
        hid = jnp.dot(x.astype(BF16), wup_bf[...], preferred_element_type=F32) + bup_ref[0]
        acts = []
        for g in range(hid.shape[1] // (2 * LANES)):
            glu = jnp.minimum(hid[:, g * 2 * LANES:g * 2 * LANES + LANES], SWIGLU_LIMIT)
            lin = jnp.clip(hid[:, g * 2 * LANES + LANES:(g + 1) * 2 * LANES], -SWIGLU_LIMIT, SWIGLU_LIMIT)
            acts.append(glu * jax.nn.sigmoid(SWIGLU_ALPHA * glu) * (lin + 1.0))
        act = jnp.concatenate(acts, axis=1)
        assert act.shape[1] == d_ff
        y = jnp.dot(act.astype(BF16), wdn_bf[...], preferred_element_type=F32) + bdn_ref[0]
        y_ref[...] = _pack_halves(y)

    @pl.when(i >= nu_ref[0])
    def _():
        y_ref[...] = jnp.zeros(y_ref.shape, y_ref.dtype)


def _experts(block_e, n_used, xs, y_prev, first_block, n_rows_total, w_up, b_up, w_down, b_down):
    n_rows, half = xs.shape
    d = w_up.shape[1]
    nb = n_rows // MOE_RB
    two_ff = w_up.shape[2]
    d_ff = w_down.shape[1]
    in_specs = [
        pl.BlockSpec((MOE_RB, half), lambda i, be, nu: (jnp.maximum(jnp.minimum(i, nu[0] - 1), 0), 0)),
        pl.BlockSpec((1, d, two_ff), lambda i, be, nu: (be[i], 0, 0)),
        pl.BlockSpec((1, 1, two_ff), lambda i, be, nu: (be[i], 0, 0)),
        pl.BlockSpec((1, d_ff, d), lambda i, be, nu: (be[i], 0, 0)),
        pl.BlockSpec((1, 1, d), lambda i, be, nu: (be[i], 0, 0)),
    ]
    operands = [block_e, n_used, xs, w_up, b_up, w_down, b_down]
    aliases = {}
    if y_prev is not None:
        in_specs.append(pl.BlockSpec(memory_space=pl.ANY))
        aliases = {len(operands): 0}
        operands.append(y_prev)
    grid_spec = pltpu.PrefetchScalarGridSpec(
        num_scalar_prefetch=2,
        grid=(nb,),
        in_specs=in_specs,
        out_specs=pl.BlockSpec((MOE_RB, half), lambda i, be, nu: (first_block + i, 0)),
        scratch_shapes=[pltpu.VMEM((d, two_ff), BF16), pltpu.VMEM((d_ff, d), BF16)],
    )
    return pl.pallas_call(
        _expert_kernel,
        grid_spec=grid_spec,
        out_shape=jax.ShapeDtypeStruct((n_rows_total, half), jnp.int32),
        input_output_aliases=aliases,
        compiler_params=_cparams(("arbitrary",)),
        name="moe_experts",
    )(*operands)


def _sc_invert_slots(dest_flat, n_rows):
    n_assign = dest_flat.shape[0]
    n_workers = SC_CORES * SC_SUBCORES
    rows_per_w = n_rows // n_workers
    chunk = SC_SCAN_CHUNK
    assert n_rows % n_workers == 0 and rows_per_w % SC_LANES == 0 and n_assign % chunk == 0
    mesh = plsc.VectorSubcoreMesh(core_axis_name="c", subcore_axis_name="s",
                                  num_cores=SC_CORES, num_subcores=SC_SUBCORES)

    def body(dest_hbm, out_hbm, dest_v, map_v):
        wid = lax.axis_index("s") * SC_CORES + lax.axis_index("c")
        base = wid * rows_per_w
        lanes = lax.broadcasted_iota(jnp.int32, (SC_LANES,), 0)

        @pl.loop(0, rows_per_w, step=SC_LANES)
        def _(r0):
            map_v[pl.ds(r0, SC_LANES)] = jnp.full((SC_LANES,), -1, jnp.int32)

        @pl.loop(0, n_assign // chunk)
        def _(ci):
            pltpu.sync_copy(dest_hbm.at[pl.ds(ci * chunk, chunk)], dest_v)

            @pl.loop(0, chunk, step=SC_LANES)
            def _(j):
                local = dest_v[pl.ds(j, SC_LANES)] - base
                mine = jnp.logical_and(local >= 0, local < rows_per_w)
                plsc.store_scatter(map_v, [jnp.where(mine, local, 0)], ci * chunk + j + lanes, mask=mine)

        pltpu.sync_copy(map_v, out_hbm.at[pl.ds(base, rows_per_w)])

    return pl.kernel(
        body,
        out_type=jax.ShapeDtypeStruct((n_rows,), jnp.int32),
        mesh=mesh,
        scratch_types=[pltpu.VMEM((chunk,), jnp.int32), pltpu.VMEM((rows_per_w,), jnp.int32)],
        compiler_params=pltpu.CompilerParams(needs_layout_passes=False),
        name="moe_slot_inverse",
    )(dest_flat)


def _sc_gather_rows(table, idx):
    n_idx = idx.shape[0]
    d = table.shape[1]
    n_workers = SC_CORES * SC_SUBCORES
    per_worker = n_idx // n_workers
    n_chunks = per_worker // SC_GATHER_ROWS
    assert n_idx % n_workers == 0 and per_worker % SC_GATHER_ROWS == 0
    mesh = plsc.VectorSubcoreMesh(core_axis_name="c", subcore_axis_name="s",
                                  num_cores=SC_CORES, num_subcores=SC_SUBCORES)

    assert n_chunks % 2 == 0

    def body(table_hbm, idx_hbm, out_hbm, idx_v, rows_a, rows_b, sem_a, sem_b):
        wid = lax.axis_index("s") * SC_CORES + lax.axis_index("c")
        base = wid * per_worker
        pltpu.sync_copy(idx_hbm.at[pl.ds(base, per_worker)], idx_v)

        def gather(ci, rows_v, sem):
            off = pl.multiple_of(ci * SC_GATHER_ROWS, SC_GATHER_ROWS)
            return pltpu.make_async_copy(table_hbm.at[idx_v.at[pl.ds(off, SC_GATHER_ROWS)]], rows_v, sem)

        def put(ci, rows_v):
            off = pl.multiple_of(ci * SC_GATHER_ROWS, SC_GATHER_ROWS)
            pltpu.sync_copy(rows_v, out_hbm.at[pl.ds(base + off, SC_GATHER_ROWS)])

        gather(0, rows_a, sem_a).start()

        @pl.loop(0, n_chunks, step=2)
        def _(ci):
            gather(ci + 1, rows_b, sem_b).start()
            gather(ci, rows_a, sem_a).wait()
            put(ci, rows_a)
            nxt = jnp.minimum(ci + 2, n_chunks - 1)
            gather(nxt, rows_a, sem_a).start()
            gather(ci + 1, rows_b, sem_b).wait()
            put(ci + 1, rows_b)

        gather(n_chunks - 1, rows_a, sem_a).wait()

    return pl.kernel(
        body,
        out_type=jax.ShapeDtypeStruct((n_idx, d), table.dtype),
        mesh=mesh,
        scratch_types=[
            pltpu.VMEM((per_worker,), jnp.int32),
            pltpu.VMEM((SC_GATHER_ROWS, d), table.dtype),
            pltpu.VMEM((SC_GATHER_ROWS, d), table.dtype),
            pltpu.SemaphoreType.DMA,
            pltpu.SemaphoreType.DMA,
        ],
        name="moe_slot_gather",
    )(table, idx)


def _combine_kernel(x1_ref, w_ref, y0_ref, y1_ref, y2_ref, y3_ref, o_ref):
    w = w_ref[...]
    out = x1_ref[...]
    for kk, y_ref in enumerate((y0_ref, y1_ref, y2_ref, y3_ref)):
        out = out + w[:, kk:kk + 1] * _unpack_halves(y_ref[...])
    o_ref[...] = out


def _combine(x1, w_tok, y_slots):
    t, d = x1.shape
    tc = COMB_TC
    nt = t // tc
    yspec = lambda kk: pl.BlockSpec((tc, d // 2), lambda i: (kk * nt + i, 0))
    return pl.pallas_call(
        _combine_kernel,
        grid=(nt,),
        in_specs=[
            pl.BlockSpec((tc, d), lambda i: (i, 0)),
            pl.BlockSpec((tc, TOP_K), lambda i: (i, 0)),
            yspec(0), yspec(1), yspec(2), yspec(3),
        ],
        out_specs=pl.BlockSpec((tc, d), lambda i: (i, 0)),
        out_shape=jax.ShapeDtypeStruct((t, d), F32),
        compiler_params=_cparams(("parallel",)),
        name="moe_combine",
    )(x1, w_tok, y_slots, y_slots, y_slots, y_slots)


def _moe(x1, h2, topi, topw, rank, counts, w_up, b_up, w_down, b_down):
    t, d = x1.shape
    n_assign = t * TOP_K
    nb = -(-n_assign // MOE_RB) + N_EXPERTS
    n_rows = nb * MOE_RB
    counts = counts[:, 0]
    padded = (counts + MOE_RB - 1) // MOE_RB * MOE_RB
    padded_end = jnp.cumsum(padded)
    padded_start = padded_end - padded
    expert_ids = jnp.arange(N_EXPERTS, dtype=jnp.int32)[:, None, None]
    start_of = jnp.sum(jnp.where(topi[None] == expert_ids, padded_start[:, None, None], 0), axis=0)
    dest = (start_of + rank).astype(jnp.int32)
    n_used = (padded_end[-1] // MOE_RB).astype(jnp.int32)
    blk = jnp.minimum(jnp.arange(nb, dtype=jnp.int32), n_used - 1)
    block_e = jnp.minimum(jnp.sum(padded_end[None, :] <= (blk * MOE_RB)[:, None], axis=1),
                          N_EXPERTS - 1).astype(jnp.int32)
    slot_of = _sc_invert_slots(dest.reshape(-1), n_rows)
    src_tok = jnp.where(slot_of < 0, jnp.arange(n_rows, dtype=jnp.int32), slot_of) % t

    nb_a = nb // 2
    y_rows = None
    for first, n_blk in ((0, nb_a), (nb_a, nb - nb_a)):
        xs = _sc_gather_rows(h2, lax.slice(src_tok, (first * MOE_RB,), ((first + n_blk) * MOE_RB,)))
        used = jnp.clip(n_used - first, 0, n_blk).reshape(1)
        y_rows = _experts(lax.slice(block_e, (first,), (first + n_blk,)), used, xs, y_rows, first, n_rows,
                          w_up, b_up, w_down, b_down)
    y_slots = _sc_gather_rows(y_rows, dest.reshape(-1))
    return _combine(x1, topw.T, y_slots)


def kernel(x, g_mix, w_in, b_gate, conv_w, a_log, dt_bias, g_delta_out, q_norm, k_norm, lambda_q1, lambda_k1, lambda_q2, lambda_k2, g_subln, rel_bias, w_o, g_ffn, w_router, b_router, w_up, b_up, w_down, b_down):
    bsz, seq, d = x.shape
    depth = g_mix.shape[0]
    n_heads = d // HEAD_DIM
    t = bsz * seq
    d_ff = w_down.shape[2]
    assert d % PROJ_TN == 0 and t % PROJ_TM == 0 and seq % GDN_TB == 0 and seq % ATT_BQ == 0
    assert t % MIX_TM == 0 and t % COMB_TC == 0 and n_heads % GDN_HG == 0
    assert (t * TOP_K) % MOE_RB == 0
    assert 2 * n_heads <= 2 * SUBLANES

    x2d = x.reshape(t, d)
    for l in range(depth):
        wl = w_in[l]
        c0 = 4 * d
        c1 = c0 + 2 * n_heads
        c2 = c1 + 2 * d
        c3 = c2 + d
        w_small = jnp.pad(wl[:, c0:c1], ((0, 0), (0, LANES - 2 * n_heads)))
        gm = g_mix[l].reshape(1, d)
        w_plain = jnp.concatenate([wl[:, :c0], wl[:, c2:c3]], axis=1).astype(BF16)
        proj_plain = _input_projection(x2d, gm, w_plain, jnp.zeros((1, 5 * d), F32), "plain")
        qk_gain = jnp.concatenate([jnp.tile(q_norm[l] * (DH_DIFF ** -0.5), 2 * n_heads),
                                   jnp.tile(k_norm[l], 2 * n_heads)]).reshape(1, 2 * d)
        proj_qk = _input_projection(x2d, gm, wl[:, c1:c2].astype(BF16), qk_gain, "qknorm")
        proj_gate = _input_projection(x2d, gm, wl[:, c3:].astype(BF16), b_gate[l].reshape(1, 2 * d), "gate")

        head_pad = jnp.zeros((LANES - 2 * n_heads,), F32)
        alog = jnp.concatenate([jnp.zeros((n_heads,), F32), a_log[l], head_pad])
        dtb = jnp.concatenate([jnp.zeros((n_heads,), F32), dt_bias[l], head_pad])
        rows_t = 2 * n_heads
        small, small_t = _small_projection(
            x2d, g_mix[l].reshape(1, d), w_small.astype(BF16), w_small[:, :rows_t].T.astype(BF16),
            alog.reshape(1, LANES), dtb.reshape(1, LANES),
            alog[:rows_t].reshape(rows_t, 1), dtb[:rows_t].reshape(rows_t, 1), n_heads)

        oa = _gated_delta(proj_plain, small, small_t, conv_w[l], g_delta_out[l].reshape(1, HEAD_DIM),
                          bsz, seq, n_heads, d)

        lam_init = 0.8 - 0.6 * math.exp(-0.3 * l)
        lam_params = jnp.stack([lambda_q1[l], lambda_k1[l], lambda_q2[l], lambda_k2[l]])
        od = _diff_attention(proj_qk, proj_plain, rel_bias, lam_params, g_subln[l].reshape(1, HEAD_DIM),
                             bsz, seq, n_heads, d, lam_init)

        x1, h2, topi, topw, rank, counts = _mix_project_route(
            proj_gate, oa, od, x2d, w_o[l].astype(BF16), g_ffn[l].reshape(1, d),
            w_router[l].T, b_router[l].reshape(N_EXPERTS, 1), d)

        b_up_l = b_up[l].reshape(N_EXPERTS, 2 * d_ff // (2 * LANES), LANES, 2)
        b_up_l = jnp.swapaxes(b_up_l, 2, 3).reshape(N_EXPERTS, 1, 2 * d_ff)
        x2d = _moe(x1, h2, topi, topw, rank, counts, w_up[l], b_up_l,
                   w_down[l], b_down[l].reshape(N_EXPERTS, 1, d))
    return x2d.reshape(bsz, seq, d)
```

```python
import functools
import math

import jax
import jax.numpy as jnp
from jax import lax
from jax.experimental import pallas as pl
from jax.experimental.pallas import tpu as pltpu
from jax.experimental.pallas import tpu_sc as plsc

F32 = jnp.float32
BF16 = jnp.bfloat16

HEAD_DIM = 128
DH_DIFF = HEAD_DIM // 2
CONV_WIDTH = 4
CHUNK = 64
N_BUCKETS = 32
MAX_DISTANCE = 128
N_EXPERTS = 32
TOP_K = 4
SWIGLU_LIMIT = 7.0
SWIGLU_ALPHA = 1.702
EPS = 1e-6
NEG_BIG = -1e30

LANES = 128
SUBLANES = 8
VMEM_LIMIT = 56 * 1024 * 1024
SC_CORES = 2
SC_SUBCORES = 16
SC_LANES = 16
SC_GATHER_ROWS = 64
SC_SCAN_CHUNK = 4096

PROJ_TM = 2048
PROJ_TN = 1024
PROJ_CHUNK = 256
GDN_TB = 256
GDN_HG = 8
ATT_HG = 1
ATT_BQ = 512
ATT_BK = 512
MIX_TM = 1024
MIX_PARTS = 2
MOE_RB = 512
COMB_TC = 512


def _cparams(sem):
    return pltpu.CompilerParams(dimension_semantics=sem, vmem_limit_bytes=VMEM_LIMIT)


def _sigmoid(x):
    return 0.5 * jnp.tanh(0.5 * x) + 0.5


def _bdot(a, b):
    return jnp.dot(a.astype(BF16), b.astype(BF16), preferred_element_type=F32)


def _bdot_nt(a, b):
    return lax.dot_general(a.astype(BF16), b.astype(BF16), (((1,), (1,)), ((), ())),
                           preferred_element_type=F32)


def _bdot_tn(a, b):
    return lax.dot_general(a.astype(BF16), b.astype(BF16), (((0,), (0,)), ((), ())),
                           preferred_element_type=F32)


def _beta_decay(acc, idx, alog, dtb, n_heads):
    beta = _sigmoid(acc)
    z = acc + dtb
    softplus = jnp.maximum(z, 0.0) + jnp.log1p(jnp.exp(-jnp.abs(z)))
    gdec = -jnp.exp(alog) * softplus
    return jnp.where(idx < n_heads, beta, jnp.where(idx < 2 * n_heads, gdec, 0.0))


def _proj_kernel(x_ref, g_ref, w_ref, aux_ref, *rest, mode, n_heads):
    o_ref, h_ref = rest[-2:] if n_heads is None else (rest[6], rest[-1])

    @pl.when(pl.program_id(1) == 0)
    def _():
        x = x_ref[...]
        ms = jnp.mean(x * x, axis=-1, keepdims=True)
        h_ref[...] = (x * lax.rsqrt(ms + EPS) * g_ref[...]).astype(BF16)
        if n_heads is not None:
            ws_ref, wst_ref, alog_ref, dtb_ref, alog_t_ref, dtb_t_ref, _, os_ref, ost_ref, _ = rest
            hb = h_ref[...]
            acc = jnp.dot(hb, ws_ref[...], preferred_element_type=F32)
            lane = lax.broadcasted_iota(jnp.int32, acc.shape, 1)
            os_ref[...] = _beta_decay(acc, lane, alog_ref[...], dtb_ref[...], n_heads)
            acc_t = lax.dot_general(wst_ref[...], hb, (((1,), (1,)), ((), ())),
                                    preferred_element_type=F32)
            sub = lax.broadcasted_iota(jnp.int32, acc_t.shape, 0)
            ost_ref[...] = _beta_decay(acc_t, sub, alog_t_ref[...], dtb_t_ref[...], n_heads)

    h = h_ref[...]
    lo = lax.broadcasted_iota(jnp.int32, (1, LANES), 1) < DH_DIFF
    for c in range(PROJ_TN // PROJ_CHUNK):
        cs = slice(c * PROJ_CHUNK, (c + 1) * PROJ_CHUNK)
        acc = jnp.dot(h, w_ref[:, cs], preferred_element_type=F32)
        if mode == "plain":
            o_ref[:, cs] = acc.astype(o_ref.dtype)
        elif mode == "gate":
            o_ref[:, cs] = _sigmoid(acc + aux_ref[:, cs]).astype(o_ref.dtype)
        else:
            for g in range(PROJ_CHUNK // LANES):
                sl = slice(c * PROJ_CHUNK + g * LANES, c * PROJ_CHUNK + (g + 1) * LANES)
                y = acc[:, g * LANES:(g + 1) * LANES]
                y2 = y * y
                s_lo = jnp.sum(jnp.where(lo, y2, 0.0), axis=-1, keepdims=True)
                s_hi = jnp.sum(jnp.where(lo, 0.0, y2), axis=-1, keepdims=True)
                r = jnp.where(lo, lax.rsqrt(s_lo / DH_DIFF + EPS), lax.rsqrt(s_hi / DH_DIFF + EPS))
                o_ref[:, sl] = (y * r * aux_ref[:, sl]).astype(o_ref.dtype)


def _input_projection(x2d, g_mix, w, aux, mode, beta_decay=None):
    t, d = x2d.shape
    n = w.shape[1]
    full = lambda shape: pl.BlockSpec(shape, lambda i, j: (0, 0))
    in_specs = [
        pl.BlockSpec((PROJ_TM, d), lambda i, j: (i, 0)),
        full((1, d)),
        pl.BlockSpec((d, PROJ_TN), lambda i, j: (0, j)),
        pl.BlockSpec((1, PROJ_TN), lambda i, j: (0, j)),
    ]
    operands = [x2d, g_mix, w, aux]
    out_specs = [pl.BlockSpec((PROJ_TM, PROJ_TN), lambda i, j: (i, j))]
    out_shape = [jax.ShapeDtypeStruct((t, n), BF16)]
    n_heads = None
    if beta_decay is not None:
        n_heads = beta_decay[-1]
        rows_t = 2 * n_heads
        in_specs += [full((d, LANES)), full((rows_t, d)), full((1, LANES)), full((1, LANES)),
                     full((rows_t, 1)), full((rows_t, 1))]
        operands += list(beta_decay[:-1])
        out_specs += [pl.BlockSpec((PROJ_TM, LANES), lambda i, j: (i, 0)),
                      pl.BlockSpec((rows_t, PROJ_TM), lambda i, j: (0, i))]
        out_shape += [jax.ShapeDtypeStruct((t, LANES), F32), jax.ShapeDtypeStruct((rows_t, t), F32)]
    out = pl.pallas_call(
        functools.partial(_proj_kernel, mode=mode, n_heads=n_heads),
        grid=(t // PROJ_TM, n // PROJ_TN),
        in_specs=in_specs,
        out_specs=out_specs,
        out_shape=out_shape,
        scratch_shapes=[pltpu.VMEM((PROJ_TM, d), BF16)],
        compiler_params=_cparams(("parallel", "arbitrary")),
        name="input_projection_" + mode,
    )(*operands)
    return out[0] if beta_decay is None else out


def _gdn_kernel(q_ref, k_ref, v_ref, z_ref, sm_ref, smt_ref, cwq_ref, cwk_ref, cwv_ref, gout_ref,
                o_ref, state_ref, qp_ref, kp_ref, vp_ref, vn_ref, *, n_heads):
    hg = pl.program_id(1)
    s = pl.program_id(2)
    tb = GDN_TB
    pad = SUBLANES
    width = GDN_HG * HEAD_DIM

    @pl.when(s == 0)
    def _():
        state_ref[...] = jnp.zeros_like(state_ref)
        for p_ref in (qp_ref, kp_ref, vp_ref):
            p_ref[0:pad, :] = jnp.zeros((pad, width), F32)

    r = lax.broadcasted_iota(jnp.int32, (tb, tb), 0)
    c = lax.broadcasted_iota(jnp.int32, (tb, tb), 1)
    delay_mat = jnp.concatenate([(r - c == dd).astype(BF16) for dd in range(1, CONV_WIDTH)], axis=0)

    def conv_silu(x_ref, p_ref, cw_ref):
        x = x_ref[...]
        xf = x.astype(F32)
        p_ref[pad:2 * pad, :] = xf[0:pad]
        delayed = jnp.dot(delay_mat, x, preferred_element_type=F32)
        acc = cw_ref[CONV_WIDTH - 1:CONV_WIDTH, :] * xf
        for dd in range(1, CONV_WIDTH):
            first = p_ref[pad - dd:2 * pad - dd, :]
            xd = jnp.concatenate([first, delayed[(dd - 1) * tb + pad:dd * tb]], axis=0)
            acc = acc + cw_ref[CONV_WIDTH - 1 - dd:CONV_WIDTH - dd, :] * xd
        p_ref[0:pad, :] = xf[tb - pad:tb]
        return acc * _sigmoid(acc)

    q_all = conv_silu(q_ref, qp_ref, cwq_ref)
    k_all = conv_silu(k_ref, kp_ref, cwk_ref)
    v_all = conv_silu(v_ref, vp_ref, cwv_ref)

    shift = int(math.log2(CHUNK))
    same = (r >> shift) == (c >> shift)
    incl = jnp.logical_and(same, c <= r)
    strict = jnp.logical_and(same, c < r)

    small = sm_ref[...]
    small_t = smt_ref[...]
    lane = lax.broadcasted_iota(jnp.int32, small.shape, 1)
    def split3(a):
        hi = a.astype(BF16)
        r1 = a - hi.astype(F32)
        mid = r1.astype(BF16)
        lo = (r1 - mid.astype(F32)).astype(BF16)
        return hi.astype(F32), mid.astype(F32), lo.astype(F32)

    part = 2 * n_heads
    s_hi, s_mid, s_lo = split3(small)
    small3 = jnp.where(lane < part, s_hi,
                       jnp.where(lane < 2 * part, pltpu.roll(s_mid, part, 1),
                                 jnp.where(lane < 3 * part, pltpu.roll(s_lo, 2 * part, 1), 0.0)))
    both = _bdot(jnp.concatenate([incl.astype(F32), same.astype(F32)], axis=0), small3)
    gcum = both[:tb]
    gtot = both[tb:]
    gcum_t = _bdot(jnp.concatenate(split3(small_t), axis=0),
                   jnp.logical_and(same, r <= c).astype(F32))
    sub3 = lax.broadcasted_iota(jnp.int32, gcum_t.shape, 0)

    heads = range(GDN_HG)
    hsl = [slice(hh * HEAD_DIM, (hh + 1) * HEAD_DIM) for hh in heads]
    qs = [q_all[:, hs] for hs in hsl]
    ks = [k_all[:, hs] for hs in hsl]
    vs = [v_all[:, hs] for hs in hsl]
    qs = [q * lax.rsqrt(jnp.sum(q * q, axis=-1, keepdims=True) + EPS) * (HEAD_DIM ** -0.5) for q in qs]
    ks = [k * lax.rsqrt(jnp.sum(k * k, axis=-1, keepdims=True) + EPS) for k in ks]

    def col_of(arr, idx):
        return jnp.sum(jnp.where(lane == idx, arr, 0.0), axis=-1, keepdims=True)

    def terms_of(pos, idx):
        return jnp.logical_or(pos == idx, jnp.logical_or(pos == idx + part, pos == idx + 2 * part))

    head_ids = [hg * GDN_HG + hh for hh in heads]
    betas = [col_of(small, hd) for hd in head_ids]
    gcs = [jnp.sum(jnp.where(terms_of(lane, hd + n_heads), gcum, 0.0), axis=-1, keepdims=True)
           for hd in head_ids]
    gls = [jnp.sum(jnp.where(terms_of(lane, hd + n_heads), gtot, 0.0), axis=-1, keepdims=True)
           for hd in head_ids]
    gc_rows = [jnp.sum(jnp.where(terms_of(sub3, hd + n_heads), gcum_t, 0.0), axis=0, keepdims=True)
               for hd in head_ids]

    decays = [jnp.where(incl, jnp.exp(jnp.minimum(gc - gr, 0.0)), 0.0) for gc, gr in zip(gcs, gc_rows)]
    kbs = [k * b for k, b in zip(ks, betas)]
    kks = [_bdot_nt(kb, k) for kb, k in zip(kbs, ks)]
    pws = [jnp.where(strict, -(kk * dc), 0.0) for kk, dc in zip(kks, decays)]
    n_chunks = tb // CHUNK
    cat_row = lax.broadcasted_iota(jnp.int32, (CHUNK, tb), 0)
    cat_lane = lax.broadcasted_iota(jnp.int32, (CHUNK, tb), 1)
    lane_chunk = cat_lane >> shift

    def block_diag(m_cat):
        return jnp.concatenate([jnp.where(lane_chunk == ci, m_cat, 0.0) for ci in range(n_chunks)], axis=0)

    def cat_of(m_bd):
        out = m_bd[0:CHUNK]
        for ci in range(1, n_chunks):
            out = out + m_bd[ci * CHUNK:(ci + 1) * CHUNK]
        return out

    pcats = [cat_of(pw) for pw in pws]
    eye_cat = ((cat_lane & (CHUNK - 1)) == cat_row).astype(F32)
    tcats = [eye_cat + pc for pc in pcats]
    pcats = [_bdot(pc, block_diag(pc)) for pc in pcats]
    n_levels = int(math.log2(CHUNK))
    for lev in range(1, n_levels):
        bds = [block_diag(pc) for pc in pcats]
        if lev < n_levels - 1:
            prods = [_bdot(jnp.concatenate([pc, tc], axis=0), bd) for pc, tc, bd in zip(pcats, tcats, bds)]
            pcats = [pr[:CHUNK] for pr in prods]
            tcats = [tc + pr[CHUNK:] for tc, pr in zip(tcats, prods)]
        else:
            tcats = [tc + _bdot(tc, bd) for tc, bd in zip(tcats, bds)]
    tmats = [block_diag(tc) for tc in tcats]
    egcs = [jnp.exp(gc) for gc in gcs]
    uws = [_bdot(tm, jnp.concatenate([v * b, kb * eg], axis=1))
           for tm, v, b, kb, eg in zip(tmats, vs, betas, kbs, egcs)]
    us = [uw[:, :HEAD_DIM] for uw in uws]
    ws = [uw[:, HEAD_DIM:] for uw in uws]
    qkm = [_bdot_nt(q, k) for q, k in zip(qs, ks)]
    qkm = [jnp.where(incl, x * dc, 0.0) for x, dc in zip(qkm, decays)]
    q_decs = [q * eg for q, eg in zip(qs, egcs)]
    k_ends = [k * jnp.exp(gl - gc) for k, gl, gc in zip(ks, gls, gcs)]

    for hh in heads:
        vn_ref[hh] = jnp.zeros((tb, HEAD_DIM), F32)
    outs = [[] for _ in heads]
    for ci in range(tb // CHUNK):
        cs = slice(ci * CHUNK, (ci + 1) * CHUNK)
        sts = [state_ref[hh] for hh in heads]
        ws_qs = [_bdot(jnp.concatenate([ws[hh][cs], q_decs[hh][cs]], axis=0), sts[hh]) for hh in heads]
        v_news = [us[hh][cs] - ws_qs[hh][:CHUNK] for hh in heads]
        for hh in heads:
            vn_ref[hh, cs, :] = v_news[hh]
        intra = [_bdot(qkm[hh][cs], vn_ref[hh]) for hh in heads]
        upd = [_bdot_tn(k_ends[hh][cs], v_news[hh]) for hh in heads]
        for hh in heads:
            outs[hh].append(ws_qs[hh][CHUNK:] + intra[hh])
            g_last = gls[hh][ci * CHUNK:ci * CHUNK + 1, :]
            state_ref[hh] = sts[hh] * jnp.exp(g_last) + upd[hh]
    for hh in heads:
        o = jnp.concatenate(outs[hh], axis=0)
        o = o * lax.rsqrt(jnp.mean(o * o, axis=-1, keepdims=True) + EPS) * gout_ref[...]
        zz = z_ref[:, hsl[hh]].astype(F32)
        o_ref[:, hsl[hh]] = (o * (zz * _sigmoid(zz))).astype(o_ref.dtype)


def _gated_delta(big, small, small_t, conv_w, g_out, bsz, seq, n_heads, d_model):
    t = bsz * seq
    tb = GDN_TB
    ns = seq // tb
    width = GDN_HG * HEAD_DIM
    nhg = n_heads // GDN_HG
    blocks_per_group = d_model // width
    rows_t = small_t.shape[0]

    def colspec(group):
        return pl.BlockSpec((tb, width), lambda b, h, s: (b * ns + s, group * blocks_per_group + h))

    def cwspec(group):
        return pl.BlockSpec((CONV_WIDTH, width), lambda b, h, s: (0, group * blocks_per_group + h))

    return pl.pallas_call(
        functools.partial(_gdn_kernel, n_heads=n_heads),
        grid=(bsz, nhg, ns),
        in_specs=[
            colspec(0), colspec(1), colspec(2), colspec(3),
            pl.BlockSpec((tb, LANES), lambda b, h, s: (b * ns + s, 0)),
            pl.BlockSpec((rows_t, tb), lambda b, h, s: (0, b * ns + s)),
            cwspec(0), cwspec(1), cwspec(2),
            pl.BlockSpec((1, HEAD_DIM), lambda b, h, s: (0, 0)),
        ],
        out_specs=pl.BlockSpec((tb, width), lambda b, h, s: (b * ns + s, h)),
        out_shape=jax.ShapeDtypeStruct((t, d_model), BF16),
        scratch_shapes=[
            pltpu.VMEM((GDN_HG, HEAD_DIM, HEAD_DIM), F32),
            pltpu.VMEM((2 * SUBLANES, width), F32),
            pltpu.VMEM((2 * SUBLANES, width), F32),
            pltpu.VMEM((2 * SUBLANES, width), F32),
            pltpu.VMEM((GDN_HG, tb, HEAD_DIM), F32),
        ],
        compiler_params=_cparams(("parallel", "parallel", "arbitrary")),
        name="gated_delta",
    )(big, big, big, big, small, small_t, conv_w, conv_w, conv_w, g_out)


def _t5_bucket(n):
    max_exact = N_BUCKETS // 2
    nf = jnp.maximum(n, 1).astype(F32)
    large = max_exact + (jnp.log(nf / max_exact) / math.log(MAX_DISTANCE / max_exact)
                         * (N_BUCKETS - max_exact)).astype(jnp.int32)
    large = jnp.minimum(large, N_BUCKETS - 1)
    return jnp.where(n < max_exact, n, large)


def _attn_kernel(rb_ref, q_ref, k_ref, v_ref, lam_ref, gsub_ref, o_ref,
                 bias_ref, m_ref, acc_ref, sa_ref, sb_ref, *, lam_init):
    hg = pl.program_id(0)
    b = pl.program_id(1)
    qi = pl.program_id(2)
    bq, bk = ATT_BQ, ATT_BK
    heads = range(ATT_HG)
    hsl = [slice(hh * HEAD_DIM, (hh + 1) * HEAD_DIM) for hh in heads]

    @pl.when(jnp.logical_and(b == 0, qi == 0))
    def _():
        blk = LANES
        i = lax.broadcasted_iota(jnp.int32, (blk, blk), 0)
        jj = lax.broadcasted_iota(jnp.int32, (blk, blk), 1)
        for hh in heads:
            head = hg * ATT_HG + hh
            far = rb_ref[N_BUCKETS - 1, head]

            def toeplitz(offset):
                bucket = _t5_bucket(jnp.maximum(i - jj + offset, 0))
                out = jnp.zeros((blk, blk), F32)
                for cc in range(N_BUCKETS):
                    out = jnp.where(bucket == cc, rb_ref[cc, head] - far, out)
                return out

            on_diag = jnp.where(i >= jj, toeplitz(0), NEG_BIG)
            next_diag = toeplitz(blk)
            kinds = {0: on_diag, 1: next_diag}
            bias_ref[hh, 2] = jnp.zeros((bq, bk), F32)
            for slot in range(2):
                for rr in range(bq // blk):
                    for cc in range(bk // blk):
                        delta = rr - cc + slot * (bk // blk)
                        if delta < 0:
                            tile = jnp.full((blk, blk), NEG_BIG, F32)
                        else:
                            tile = kinds.get(delta, jnp.zeros((blk, blk), F32))
                        bias_ref[hh, slot, rr * blk:(rr + 1) * blk, cc * blk:(cc + 1) * blk] = tile

    m_ref[...] = jnp.full(m_ref.shape, NEG_BIG, F32)
    acc_ref[...] = jnp.zeros(acc_ref.shape, F32)

    lane = lax.broadcasted_iota(jnp.int32, (bq, HEAD_DIM), 1)
    qs = []
    for hs in hsl:
        q = q_ref[:, hs]
        zero = jnp.zeros_like(q)
        qs.append(jnp.concatenate([jnp.where(lane < DH_DIFF, q, zero),
                                   jnp.where(lane < DH_DIFF, zero, q)], axis=0))
    ones_col = (lax.broadcasted_iota(jnp.int32, (bk, HEAD_DIM), 1) == 0).astype(BF16)

    def scores(j, s_ref):
        ks = pl.multiple_of(j * bk, bk)
        for hh in heads:
            s_ref[hh] = lax.dot_general(qs[hh], k_ref[pl.ds(ks, bk), hsl[hh]], (((1,), (1,)), ((), ())),
                                        preferred_element_type=F32)

    def absorb(j, s_ref, biased=True):
        ks = pl.multiple_of(j * bk, bk)
        v_exts = [jnp.concatenate([v_ref[pl.ds(ks, bk), hs], ones_col], axis=1) for hs in hsl]
        if biased:
            slot = jnp.minimum(qi - j, 2)
            scs = [jnp.concatenate([s_ref[hh, 0:bq, :] + bias_ref[hh, slot],
                                    s_ref[hh, bq:2 * bq, :] + bias_ref[hh, slot]], axis=0) for hh in heads]
        else:
            scs = [s_ref[hh] for hh in heads]
        m_olds = [m_ref[hh] for hh in heads]
        m_news = [jnp.maximum(mo, jnp.max(sc, axis=-1, keepdims=True)) for mo, sc in zip(m_olds, scs)]
        ps = [jnp.exp(sc - mn) for sc, mn in zip(scs, m_news)]
        pvs = [jnp.dot(p.astype(BF16), ve, preferred_element_type=F32) for p, ve in zip(ps, v_exts)]
        for hh in heads:
            acc_ref[hh] = jnp.exp(m_olds[hh] - m_news[hh]) * acc_ref[hh] + pvs[hh]
            m_ref[hh] = m_news[hh]

    n_tiles = qi + 1
    scores(0, sa_ref)

    def pair_body(jj, carry, biased):
        j0 = 2 * jj
        scores(j0 + 1, sb_ref)
        absorb(j0, sa_ref, biased)
        scores(jnp.minimum(j0 + 2, qi), sa_ref)
        absorb(j0 + 1, sb_ref, biased)
        return carry

    n_far_pairs = jnp.maximum(qi - 1, 0) // 2
    lax.fori_loop(0, n_far_pairs, functools.partial(pair_body, biased=False), 0)
    lax.fori_loop(n_far_pairs, n_tiles // 2, functools.partial(pair_body, biased=True), 0)

    @pl.when(n_tiles % 2 == 1)
    def _():
        absorb(qi, sa_ref)

    lam_p = lam_ref[...]
    s1 = jnp.sum(lam_p[0:1] * lam_p[1:2], axis=-1, keepdims=True)
    s2 = jnp.sum(lam_p[2:3] * lam_p[3:4], axis=-1, keepdims=True)
    lam = jnp.exp(s1) - jnp.exp(s2) + lam_init
    for hh in heads:
        acc = acc_ref[hh]
        num = acc[:, :HEAD_DIM]
        den = acc[:, HEAD_DIM:HEAD_DIM + 1]
        o = num[:bq] / den[:bq] - lam * (num[bq:] / den[bq:])
        o = o * lax.rsqrt(jnp.mean(o * o, axis=-1, keepdims=True) + EPS) * gsub_ref[...]
        o_ref[:, hsl[hh]] = (o * (1.0 - lam_init)).astype(o_ref.dtype)


def _diff_attention(proj_qk, proj_plain, rel_bias, lam_params, g_subln, bsz, seq, n_heads, d_model,
                    lam_init):
    t = bsz * seq
    nq = seq // ATT_BQ
    assert ATT_BQ == ATT_BK and MAX_DISTANCE <= LANES and n_heads % ATT_HG == 0
    width = ATT_HG * HEAD_DIM
    per = d_model // width
    vcol = 4 * per
    return pl.pallas_call(
        functools.partial(_attn_kernel, lam_init=lam_init),
        grid=(n_heads // ATT_HG, bsz, nq),
        in_specs=[
            pl.BlockSpec(memory_space=pltpu.SMEM),
            pl.BlockSpec((ATT_BQ, width), lambda h, b, i: (b * nq + i, h)),
            pl.BlockSpec((seq, width), lambda h, b, i: (b, per + h)),
            pl.BlockSpec((seq, width), lambda h, b, i: (b, vcol + h)),
            pl.BlockSpec((4, DH_DIFF), lambda h, b, i: (0, 0)),
            pl.BlockSpec((1, HEAD_DIM), lambda h, b, i: (0, 0)),
        ],
        out_specs=pl.BlockSpec((ATT_BQ, width), lambda h, b, i: (b * nq + i, h)),
        out_shape=jax.ShapeDtypeStruct((t, d_model), BF16),
        scratch_shapes=[
            pltpu.VMEM((ATT_HG, 3, ATT_BQ, ATT_BK), F32),
            pltpu.VMEM((ATT_HG, 2 * ATT_BQ, 1), F32),
            pltpu.VMEM((ATT_HG, 2 * ATT_BQ, 2 * HEAD_DIM), F32),
            pltpu.VMEM((ATT_HG, 2 * ATT_BQ, ATT_BK), F32),
            pltpu.VMEM((ATT_HG, 2 * ATT_BQ, ATT_BK), F32),
        ],
        compiler_params=_cparams(("arbitrary", "arbitrary", "arbitrary")),
        name="diff_attention",
    )(rel_bias, proj_qk, proj_qk, proj_plain, lam_params, g_subln)


def _mix_kernel(ga_ref, gb_ref, oa_ref, od_ref, x_ref, wo_ref, gffn_ref, wr_ref, br_ref,
                x1_ref, h2_ref, topi_ref, topw_ref, rank_ref, cnt_ref, carry_ref):
    i = pl.program_id(0)
    tm = MIX_TM

    @pl.when(i == 0)
    def _():
        carry_ref[...] = jnp.zeros_like(carry_ref)

    tp = tm // MIX_PARTS
    parts = range(MIX_PARTS)
    rows = [slice(pp * tp, (pp + 1) * tp) for pp in parts]
    mixes = [ga_ref[rs, :] * oa_ref[rs, :] + gb_ref[rs, :] * od_ref[rs, :] for rs in rows]
    x1s = [x_ref[rs, :] + jnp.dot(mx, wo_ref[...], preferred_element_type=F32) for rs, mx in zip(rows, mixes)]
    for rs, x1 in zip(rows, x1s):
        x1_ref[rs, :] = x1
    h2s = [x1 * lax.rsqrt(jnp.mean(x1 * x1, axis=-1, keepdims=True) + EPS) * gffn_ref[...] for x1 in x1s]
    for rs, h2 in zip(rows, h2s):
        h2_ref[rs, :] = _pack_halves(h2)

    curs = [lax.dot_general(wr_ref[...], h2, (((1,), (1,)), ((), ())), preferred_element_type=F32,
                            precision=lax.Precision.HIGHEST) + br_ref[...] for h2 in h2s]
    eidx = lax.broadcasted_iota(jnp.int32, curs[0].shape, 0).astype(F32)
    vals = [[] for _ in parts]
    hots = [[] for _ in parts]
    for kk in range(TOP_K):
        mxs = [jnp.max(cur, axis=0, keepdims=True) for cur in curs]
        idxs = [jnp.min(jnp.where(cur == mx, eidx, float(N_EXPERTS)), axis=0, keepdims=True)
                for cur, mx in zip(curs, mxs)]
        for pp in parts:
            hot = eidx == idxs[pp]
            vals[pp].append(mxs[pp])
            hots[pp].append(hot)
            topi_ref[kk:kk + 1, rows[pp]] = idxs[pp].astype(jnp.int32)
            curs[pp] = jnp.where(hot, -jnp.inf, curs[pp])
    for pp in parts:
        exps = [jnp.exp(vv - vals[pp][0]) for vv in vals[pp]]
        denom = exps[0] + exps[1] + exps[2] + exps[3]
        for kk in range(TOP_K):
            topw_ref[kk:kk + 1, rows[pp]] = exps[kk] / denom

    r = lax.broadcasted_iota(jnp.int32, (tp, tp), 0)
    c = lax.broadcasted_iota(jnp.int32, (tp, tp), 1)
    earlier = (r < c).astype(F32)
    sel_fs = []
    for pp in parts:
        sel = hots[pp][0]
        for kk in range(1, TOP_K):
            sel = jnp.logical_or(sel, hots[pp][kk])
        sel_fs.append(sel.astype(F32))
    within = [_bdot(sf, earlier) for sf in sel_fs]
    totals = [jnp.sum(sf, axis=-1, keepdims=True) for sf in sel_fs]
    run = carry_ref[...]
    for pp in parts:
        before = within[pp] + run
        for kk in range(TOP_K):
            rank_ref[kk:kk + 1, rows[pp]] = jnp.sum(jnp.where(hots[pp][kk], before, 0.0), axis=0,
                                                    keepdims=True).astype(jnp.int32)
        run = run + totals[pp]
    carry_ref[...] = run
    cnt_ref[...] = run.astype(jnp.int32)


def _mix_project_route(proj_gate, oa, od, x2d, w_o, g_ffn, w_r_t, b_r, d_model):
    t = x2d.shape[0]
    tm = MIX_TM
    full = lambda shape: pl.BlockSpec(shape, lambda i: (0, 0))
    row = lambda: pl.BlockSpec((tm, d_model), lambda i: (i, 0))
    krow = lambda: pl.BlockSpec((TOP_K, tm), lambda i: (0, i))
    return pl.pallas_call(
        _mix_kernel,
        grid=(t // tm,),
        in_specs=[
            pl.BlockSpec((tm, d_model), lambda i: (i, 0)),
            pl.BlockSpec((tm, d_model), lambda i: (i, 1)),
            row(), row(), row(),
            full((d_model, d_model)), full((1, d_model)), full((N_EXPERTS, d_model)), full((N_EXPERTS, 1)),
        ],
        out_specs=[row(), pl.BlockSpec((tm, d_model // 2), lambda i: (i, 0)),
                   krow(), krow(), krow(), full((N_EXPERTS, 1))],
        out_shape=[
            jax.ShapeDtypeStruct((t, d_model), F32),
            jax.ShapeDtypeStruct((t, d_model // 2), jnp.int32),
            jax.ShapeDtypeStruct((TOP_K, t), jnp.int32),
            jax.ShapeDtypeStruct((TOP_K, t), F32),
            jax.ShapeDtypeStruct((TOP_K, t), jnp.int32),
            jax.ShapeDtypeStruct((N_EXPERTS, 1), jnp.int32),
        ],
        scratch_shapes=[pltpu.VMEM((N_EXPERTS, 1), F32)],
        compiler_params=_cparams(("arbitrary",)),
        name="merge_outproj_route",
    )(proj_gate, proj_gate, oa, od, x2d, w_o, g_ffn, w_r_t, b_r)


def _pack_halves(x):
    half = x.shape[1] // 2
    bits = pltpu.bitcast(x.astype(BF16).astype(F32), jnp.int32)
    return bits[:, :half] | lax.shift_right_logical(bits[:, half:], 16)


def _unpack_halves(p):
    hi = pltpu.bitcast(p & jnp.int32(-65536), F32)
    lo = pltpu.bitcast(lax.shift_left(p, 16), F32)
    return jnp.concatenate([hi, lo], axis=1)


def _expert_kernel(be_ref, nu_ref, x_ref, wup_ref, bup_ref, wdn_ref, bdn_ref, *rest):
    y_ref, wup_bf, wdn_bf = rest[-3:]
    i = pl.program_id(0)
    d_ff = wdn_ref.shape[1]

    @pl.when(jnp.logical_or(i == 0, be_ref[i] != be_ref[jnp.maximum(i - 1, 0)]))
    def _():
        rr = lax.broadcasted_iota(jnp.int32, (2 * LANES, 2 * LANES), 0)
        cc = lax.broadcasted_iota(jnp.int32, (2 * LANES, 2 * LANES), 1)
        pick = jnp.where(cc < LANES, 2 * cc, 2 * (cc - LANES) + 1)
        perm = (rr == pick).astype(BF16)
        for g in range(wup_ref.shape[2] // (2 * LANES)):
            cs = slice(g * 2 * LANES, (g + 1) * 2 * LANES)
            wup_bf[:, cs] = jnp.dot(wup_ref[0, :, cs].astype(BF16), perm,
                                    preferred_element_type=F32).astype(BF16)
        wdn_bf[...] = wdn_ref[0].astype(BF16)

    @pl.when(i < nu_ref[0])
    def _():
        x = _unpack_halves(x_ref[...])
        hid = jnp.dot(x.astype(BF16), wup_bf[...], preferred_element_type=F32) + bup_ref[0]
        acts = []
        for g in range(hid.shape[1] // (2 * LANES)):
            glu = jnp.minimum(hid[:, g * 2 * LANES:g * 2 * LANES + LANES], SWIGLU_LIMIT)
            lin = jnp.clip(hid[:, g * 2 * LANES + LANES:(g + 1) * 2 * LANES], -SWIGLU_LIMIT, SWIGLU_LIMIT)
            acts.append(glu * _sigmoid(SWIGLU_ALPHA * glu) * (lin + 1.0))
        act = jnp.concatenate(acts, axis=1)
        assert act.shape[1] == d_ff
        y = jnp.dot(act.astype(BF16), wdn_bf[...], preferred_element_type=F32) + bdn_ref[0]
        y_ref[...] = _pack_halves(y)

    @pl.when(i >= nu_ref[0])
    def _():
        y_ref[...] = jnp.zeros(y_ref.shape, y_ref.dtype)


def _experts(block_e, n_used, xs, y_prev, first_block, n_rows_total, w_up, b_up, w_down, b_down):
    n_rows, half = xs.shape
    d = w_up.shape[1]
    nb = n_rows // MOE_RB
    two_ff = w_up.shape[2]
    d_ff = w_down.shape[1]
    in_specs = [
        pl.BlockSpec((MOE_RB, half), lambda i, be, nu: (jnp.maximum(jnp.minimum(i, nu[0] - 1), 0), 0)),
        pl.BlockSpec((1, d, two_ff), lambda i, be, nu: (be[i], 0, 0)),
        pl.BlockSpec((1, 1, two_ff), lambda i, be, nu: (be[i], 0, 0)),
        pl.BlockSpec((1, d_ff, d), lambda i, be, nu: (be[i], 0, 0)),
        pl.BlockSpec((1, 1, d), lambda i, be, nu: (be[i], 0, 0)),
    ]
    operands = [block_e, n_used, xs, w_up, b_up, w_down, b_down]
    aliases = {}
    if y_prev is not None:
        in_specs.append(pl.BlockSpec(memory_space=pl.ANY))
        aliases = {len(operands): 0}
        operands.append(y_prev)
    grid_spec = pltpu.PrefetchScalarGridSpec(
        num_scalar_prefetch=2,
        grid=(nb,),
        in_specs=in_specs,
        out_specs=pl.BlockSpec((MOE_RB, half), lambda i, be, nu: (first_block + i, 0)),
        scratch_shapes=[pltpu.VMEM((d, two_ff), BF16), pltpu.VMEM((d_ff, d), BF16)],
    )
    return pl.pallas_call(
        _expert_kernel,
        grid_spec=grid_spec,
        out_shape=jax.ShapeDtypeStruct((n_rows_total, half), jnp.int32),
        input_output_aliases=aliases,
        compiler_params=_cparams(("arbitrary",)),
        name="moe_experts",
    )(*operands)


def _sc_invert_slots(dest_flat, n_rows):
    n_assign = dest_flat.shape[0]
    n_workers = SC_CORES * SC_SUBCORES
    rows_per_w = n_rows // n_workers
    chunk = SC_SCAN_CHUNK
    assert n_rows % n_workers == 0 and rows_per_w % SC_LANES == 0 and n_assign % chunk == 0
    mesh = plsc.VectorSubcoreMesh(core_axis_name="c", subcore_axis_name="s",
                                  num_cores=SC_CORES, num_subcores=SC_SUBCORES)

    def body(dest_hbm, out_hbm, dest_v, map_v):
        wid = lax.axis_index("s") * SC_CORES + lax.axis_index("c")
        base = wid * rows_per_w
        lanes = lax.broadcasted_iota(jnp.int32, (SC_LANES,), 0)

        @pl.loop(0, rows_per_w, step=SC_LANES)
        def _(r0):
            map_v[pl.ds(r0, SC_LANES)] = jnp.full((SC_LANES,), -1, jnp.int32)

        @pl.loop(0, n_assign // chunk)
        def _(ci):
            pltpu.sync_copy(dest_hbm.at[pl.ds(ci * chunk, chunk)], dest_v)

            @pl.loop(0, chunk, step=SC_LANES)
            def _(j):
                local = dest_v[pl.ds(j, SC_LANES)] - base
                mine = jnp.logical_and(local >= 0, local < rows_per_w)
                plsc.store_scatter(map_v, [jnp.where(mine, local, 0)], ci * chunk + j + lanes, mask=mine)

        pltpu.sync_copy(map_v, out_hbm.at[pl.ds(base, rows_per_w)])

    return pl.kernel(
        body,
        out_type=jax.ShapeDtypeStruct((n_rows,), jnp.int32),
        mesh=mesh,
        scratch_types=[pltpu.VMEM((chunk,), jnp.int32), pltpu.VMEM((rows_per_w,), jnp.int32)],
        compiler_params=pltpu.CompilerParams(needs_layout_passes=False),
        name="moe_slot_inverse",
    )(dest_flat)


def _sc_gather_rows(table, idx):
    n_idx = idx.shape[0]
    d = table.shape[1]
    n_workers = SC_CORES * SC_SUBCORES
    per_worker = n_idx // n_workers
    n_chunks = per_worker // SC_GATHER_ROWS
    assert n_idx % n_workers == 0 and per_worker % SC_GATHER_ROWS == 0
    mesh = plsc.VectorSubcoreMesh(core_axis_name="c", subcore_axis_name="s",
                                  num_cores=SC_CORES, num_subcores=SC_SUBCORES)

    assert n_chunks % 2 == 0

    def body(table_hbm, idx_hbm, out_hbm, idx_v, rows_a, rows_b, sem_a, sem_b):
        wid = lax.axis_index("s") * SC_CORES + lax.axis_index("c")
        base = wid * per_worker
        pltpu.sync_copy(idx_hbm.at[pl.ds(base, per_worker)], idx_v)

        def gather(ci, rows_v, sem):
            off = pl.multiple_of(ci * SC_GATHER_ROWS, SC_GATHER_ROWS)
            return pltpu.make_async_copy(table_hbm.at[idx_v.at[pl.ds(off, SC_GATHER_ROWS)]], rows_v, sem)

        def put(ci, rows_v):
            off = pl.multiple_of(ci * SC_GATHER_ROWS, SC_GATHER_ROWS)
            pltpu.sync_copy(rows_v, out_hbm.at[pl.ds(base + off, SC_GATHER_ROWS)])

        gather(0, rows_a, sem_a).start()

        @pl.loop(0, n_chunks, step=2)
        def _(ci):
            gather(ci + 1, rows_b, sem_b).start()
            gather(ci, rows_a, sem_a).wait()
            put(ci, rows_a)
            nxt = jnp.minimum(ci + 2, n_chunks - 1)
            gather(nxt, rows_a, sem_a).start()
            gather(ci + 1, rows_b, sem_b).wait()
            put(ci + 1, rows_b)

        gather(n_chunks - 1, rows_a, sem_a).wait()

    return pl.kernel(
        body,
        out_type=jax.ShapeDtypeStruct((n_idx, d), table.dtype),
        mesh=mesh,
        scratch_types=[
            pltpu.VMEM((per_worker,), jnp.int32),
            pltpu.VMEM((SC_GATHER_ROWS, d), table.dtype),
            pltpu.VMEM((SC_GATHER_ROWS, d), table.dtype),
            pltpu.SemaphoreType.DMA,
            pltpu.SemaphoreType.DMA,
        ],
        name="moe_slot_gather",
    )(table, idx)


def _combine_kernel(x1_ref, w_ref, y0_ref, y1_ref, y2_ref, y3_ref, o_ref):
    w = w_ref[...]
    out = x1_ref[...]
    for kk, y_ref in enumerate((y0_ref, y1_ref, y2_ref, y3_ref)):
        out = out + w[:, kk:kk + 1] * _unpack_halves(y_ref[...])
    o_ref[...] = out


def _combine(x1, w_tok, y_slots):
    t, d = x1.shape
    tc = COMB_TC
    nt = t // tc
    yspec = lambda kk: pl.BlockSpec((tc, d // 2), lambda i: (kk * nt + i, 0))
    return pl.pallas_call(
        _combine_kernel,
        grid=(nt,),
        in_specs=[
            pl.BlockSpec((tc, d), lambda i: (i, 0)),
            pl.BlockSpec((tc, TOP_K), lambda i: (i, 0)),
            yspec(0), yspec(1), yspec(2), yspec(3),
        ],
        out_specs=pl.BlockSpec((tc, d), lambda i: (i, 0)),
        out_shape=jax.ShapeDtypeStruct((t, d), F32),
        compiler_params=_cparams(("parallel",)),
        name="moe_combine",
    )(x1, w_tok, y_slots, y_slots, y_slots, y_slots)


def _moe(x1, h2, topi, topw, rank, counts, w_up, b_up, w_down, b_down):
    t, d = x1.shape
    n_assign = t * TOP_K
    nb = -(-n_assign // MOE_RB) + N_EXPERTS
    n_rows = nb * MOE_RB
    counts = counts[:, 0]
    padded = (counts + MOE_RB - 1) // MOE_RB * MOE_RB
    padded_end = jnp.cumsum(padded)
    padded_start = padded_end - padded
    expert_ids = jnp.arange(N_EXPERTS, dtype=jnp.int32)[:, None, None]
    start_of = jnp.sum(jnp.where(topi[None] == expert_ids, padded_start[:, None, None], 0), axis=0)
    dest = (start_of + rank).astype(jnp.int32)
    n_used = (padded_end[-1] // MOE_RB).astype(jnp.int32)
    blk = jnp.minimum(jnp.arange(nb, dtype=jnp.int32), n_used - 1)
    block_e = jnp.minimum(jnp.sum(padded_end[None, :] <= (blk * MOE_RB)[:, None], axis=1),
                          N_EXPERTS - 1).astype(jnp.int32)
    slot_of = _sc_invert_slots(dest.reshape(-1), n_rows)
    src_tok = jnp.where(slot_of < 0, jnp.arange(n_rows, dtype=jnp.int32), slot_of) % t

    nb_a = nb // 2
    y_rows = None
    for first, n_blk in ((0, nb_a), (nb_a, nb - nb_a)):
        xs = _sc_gather_rows(h2, lax.slice(src_tok, (first * MOE_RB,), ((first + n_blk) * MOE_RB,)))
        used = jnp.clip(n_used - first, 0, n_blk).reshape(1)
        y_rows = _experts(lax.slice(block_e, (first,), (first + n_blk,)), used, xs, y_rows, first, n_rows,
                          w_up, b_up, w_down, b_down)
    y_slots = _sc_gather_rows(y_rows, dest.reshape(-1))
    return _combine(x1, topw.T, y_slots)


def kernel(x, g_mix, w_in, b_gate, conv_w, a_log, dt_bias, g_delta_out, q_norm, k_norm, lambda_q1, lambda_k1, lambda_q2, lambda_k2, g_subln, rel_bias, w_o, g_ffn, w_router, b_router, w_up, b_up, w_down, b_down):
    bsz, seq, d = x.shape
    depth = g_mix.shape[0]
    n_heads = d // HEAD_DIM
    t = bsz * seq
    d_ff = w_down.shape[2]
    assert d % PROJ_TN == 0 and t % PROJ_TM == 0 and seq % GDN_TB == 0 and seq % ATT_BQ == 0
    assert t % MIX_TM == 0 and t % COMB_TC == 0 and n_heads % GDN_HG == 0
    assert (t * TOP_K) % MOE_RB == 0
    assert 2 * n_heads <= 2 * SUBLANES

    x2d = x.reshape(t, d)
    for l in range(depth):
        wl = w_in[l]
        c0 = 4 * d
        c1 = c0 + 2 * n_heads
        c2 = c1 + 2 * d
        c3 = c2 + d
        w_small = jnp.pad(wl[:, c0:c1], ((0, 0), (0, LANES - 2 * n_heads)))
        gm = g_mix[l].reshape(1, d)
        head_pad = jnp.zeros((LANES - 2 * n_heads,), F32)
        alog = jnp.concatenate([jnp.zeros((n_heads,), F32), a_log[l], head_pad])
        dtb = jnp.concatenate([jnp.zeros((n_heads,), F32), dt_bias[l], head_pad])
        rows_t = 2 * n_heads
        beta_decay = (w_small.astype(BF16), w_small[:, :rows_t].T.astype(BF16),
                      alog.reshape(1, LANES), dtb.reshape(1, LANES),
                      alog[:rows_t].reshape(rows_t, 1), dtb[:rows_t].reshape(rows_t, 1), n_heads)
        w_plain = jnp.concatenate([wl[:, :c0], wl[:, c2:c3]], axis=1).astype(BF16)
        proj_plain, small, small_t = _input_projection(x2d, gm, w_plain, jnp.zeros((1, 5 * d), F32), "plain",
                                                       beta_decay)
        qk_gain = jnp.concatenate([jnp.tile(q_norm[l] * (DH_DIFF ** -0.5), 2 * n_heads),
                                   jnp.tile(k_norm[l], 2 * n_heads)]).reshape(1, 2 * d)
        proj_qk = _input_projection(x2d, gm, wl[:, c1:c2].astype(BF16), qk_gain, "qknorm")
        proj_gate = _input_projection(x2d, gm, wl[:, c3:].astype(BF16), b_gate[l].reshape(1, 2 * d), "gate")

        oa = _gated_delta(proj_plain, small, small_t, conv_w[l], g_delta_out[l].reshape(1, HEAD_DIM),
                          bsz, seq, n_heads, d)

        lam_init = 0.8 - 0.6 * math.exp(-0.3 * l)
        lam_params = jnp.stack([lambda_q1[l], lambda_k1[l], lambda_q2[l], lambda_k2[l]])
        od = _diff_attention(proj_qk, proj_plain, rel_bias, lam_params, g_subln[l].reshape(1, HEAD_DIM),
                             bsz, seq, n_heads, d, lam_init)

        x1, h2, topi, topw, rank, counts = _mix_project_route(
            proj_gate, oa, od, x2d, w_o[l].astype(BF16), g_ffn[l].reshape(1, d),
            w_router[l].T, b_router[l].reshape(N_EXPERTS, 1), d)

        b_up_l = b_up[l].reshape(N_EXPERTS, 2 * d_ff // (2 * LANES), LANES, 2)
        b_up_l = jnp.swapaxes(b_up_l, 2, 3).reshape(N_EXPERTS, 1, 2 * d_ff)
        x2d = _moe(x1, h2, topi, topw, rank, counts, w_up[l], b_up_l,
                   w_down[l], b_down[l].reshape(N_EXPERTS, 1, d))
    return x2d.reshape(bsz, seq, d)
```

```python
import functools
import math

import jax
import jax.numpy as jnp
from jax import lax
from jax.experimental import pallas as pl
from jax.experimental.pallas import tpu as pltpu
from jax.experimental.pallas import tpu_sc as plsc

F32 = jnp.float32
BF16 = jnp.bfloat16

HEAD_DIM = 128
DH_DIFF = HEAD_DIM // 2
CONV_WIDTH = 4
CHUNK = 64
N_BUCKETS = 32
MAX_DISTANCE = 128
N_EXPERTS = 32
TOP_K = 4
SWIGLU_LIMIT = 7.0
SWIGLU_ALPHA = 1.702
EPS = 1e-6
NEG_BIG = -1e30

LANES = 128
SUBLANES = 8
VMEM_LIMIT = 56 * 1024 * 1024
SC_CORES = 2
SC_SUBCORES = 16
SC_LANES = 16
SC_GATHER_ROWS = 64
SC_SCAN_CHUNK = 4096

PROJ_TM = 2048
PROJ_TN = 1024
PROJ_CHUNK = 256
GDN_TB = 256
GDN_HG = 8
ATT_HG = 1
ATT_BQ = 512
ATT_BK = 512
MIX_TM = 1024
MIX_PARTS = 2
MOE_RB = 512
COMB_TC = 512


def _cparams(sem):
    return pltpu.CompilerParams(dimension_semantics=sem, vmem_limit_bytes=VMEM_LIMIT)


def _sigmoid(x):
    return 0.5 * jnp.tanh(0.5 * x) + 0.5


def _bdot(a, b):
    return jnp.dot(a.astype(BF16), b.astype(BF16), preferred_element_type=F32)


def _bdot_nt(a, b):
    return lax.dot_general(a.astype(BF16), b.astype(BF16), (((1,), (1,)), ((), ())),
                           preferred_element_type=F32)


def _bdot_tn(a, b):
    return lax.dot_general(a.astype(BF16), b.astype(BF16), (((0,), (0,)), ((), ())),
                           preferred_element_type=F32)


def _beta_decay(acc, idx, alog, dtb, n_heads):
    beta = _sigmoid(acc)
    z = acc + dtb
    softplus = jnp.maximum(z, 0.0) + jnp.log1p(jnp.exp(-jnp.abs(z)))
    gdec = -jnp.exp(alog) * softplus
    return jnp.where(idx < n_heads, beta, jnp.where(idx < 2 * n_heads, gdec, 0.0))


def _proj_kernel(x_ref, g_ref, w_ref, aux_ref, *rest, mode, n_heads):
    o_ref, h_ref = rest[-2:] if n_heads is None else (rest[6], rest[-1])

    @pl.when(pl.program_id(1) == 0)
    def _():
        x = x_ref[...]
        ms = jnp.mean(x * x, axis=-1, keepdims=True)
        h_ref[...] = (x * lax.rsqrt(ms + EPS) * g_ref[...]).astype(BF16)
        if n_heads is not None:
            ws_ref, wst_ref, alog_ref, dtb_ref, alog_t_ref, dtb_t_ref, _, os_ref, ost_ref, _ = rest
            hb = h_ref[...]
            acc = jnp.dot(hb, ws_ref[...], preferred_element_type=F32)
            lane = lax.broadcasted_iota(jnp.int32, acc.shape, 1)
            os_ref[...] = _beta_decay(acc, lane, alog_ref[...], dtb_ref[...], n_heads)
            acc_t = lax.dot_general(wst_ref[...], hb, (((1,), (1,)), ((), ())),
                                    preferred_element_type=F32)
            sub = lax.broadcasted_iota(jnp.int32, acc_t.shape, 0)
            ost_ref[...] = _beta_decay(acc_t, sub, alog_t_ref[...], dtb_t_ref[...], n_heads)

    h = h_ref[...]
    lo = lax.broadcasted_iota(jnp.int32, (1, LANES), 1) < DH_DIFF
    for c in range(PROJ_TN // PROJ_CHUNK):
        cs = slice(c * PROJ_CHUNK, (c + 1) * PROJ_CHUNK)
        acc = jnp.dot(h, w_ref[:, cs], preferred_element_type=F32)
        if mode == "plain":
            o_ref[:, cs] = acc.astype(o_ref.dtype)
        elif mode == "gate":
            o_ref[:, cs] = _sigmoid(acc + aux_ref[:, cs]).astype(o_ref.dtype)
        else:
            for g in range(PROJ_CHUNK // LANES):
                sl = slice(c * PROJ_CHUNK + g * LANES, c * PROJ_CHUNK + (g + 1) * LANES)
                y = acc[:, g * LANES:(g + 1) * LANES]
                y2 = y * y
                s_lo = jnp.sum(jnp.where(lo, y2, 0.0), axis=-1, keepdims=True)
                s_hi = jnp.sum(jnp.where(lo, 0.0, y2), axis=-1, keepdims=True)
                r = jnp.where(lo, lax.rsqrt(s_lo / DH_DIFF + EPS), lax.rsqrt(s_hi / DH_DIFF + EPS))
                o_ref[:, sl] = (y * r * aux_ref[:, sl]).astype(o_ref.dtype)


def _input_projection(x2d, g_mix, w, aux, mode, beta_decay=None):
    t, d = x2d.shape
    n = w.shape[1]
    full = lambda shape: pl.BlockSpec(shape, lambda i, j: (0, 0))
    in_specs = [
        pl.BlockSpec((PROJ_TM, d), lambda i, j: (i, 0)),
        full((1, d)),
        pl.BlockSpec((d, PROJ_TN), lambda i, j: (0, j)),
        pl.BlockSpec((1, PROJ_TN), lambda i, j: (0, j)),
    ]
    operands = [x2d, g_mix, w, aux]
    out_specs = [pl.BlockSpec((PROJ_TM, PROJ_TN), lambda i, j: (i, j))]
    out_shape = [jax.ShapeDtypeStruct((t, n), BF16)]
    n_heads = None
    if beta_decay is not None:
        n_heads = beta_decay[-1]
        rows_t = 2 * n_heads
        in_specs += [full((d, LANES)), full((rows_t, d)), full((1, LANES)), full((1, LANES)),
                     full((rows_t, 1)), full((rows_t, 1))]
        operands += list(beta_decay[:-1])
        out_specs += [pl.BlockSpec((PROJ_TM, LANES), lambda i, j: (i, 0)),
                      pl.BlockSpec((rows_t, PROJ_TM), lambda i, j: (0, i))]
        out_shape += [jax.ShapeDtypeStruct((t, LANES), F32), jax.ShapeDtypeStruct((rows_t, t), F32)]
    out = pl.pallas_call(
        functools.partial(_proj_kernel, mode=mode, n_heads=n_heads),
        grid=(t // PROJ_TM, n // PROJ_TN),
        in_specs=in_specs,
        out_specs=out_specs,
        out_shape=out_shape,
        scratch_shapes=[pltpu.VMEM((PROJ_TM, d), BF16)],
        compiler_params=_cparams(("parallel", "arbitrary")),
        name="input_projection_" + mode,
    )(*operands)
    return out[0] if beta_decay is None else out


def _gdn_kernel(q_ref, k_ref, v_ref, z_ref, sm_ref, smt_ref, cwq_ref, cwk_ref, cwv_ref, gout_ref,
                o_ref, state_ref, qp_ref, kp_ref, vp_ref, vn_ref, *, n_heads):
    hg = pl.program_id(1)
    s = pl.program_id(2)
    tb = GDN_TB
    pad = SUBLANES
    width = GDN_HG * HEAD_DIM

    @pl.when(s == 0)
    def _():
        state_ref[...] = jnp.zeros_like(state_ref)
        for p_ref in (qp_ref, kp_ref, vp_ref):
            p_ref[0:pad, :] = jnp.zeros((pad, width), F32)

    r = lax.broadcasted_iota(jnp.int32, (tb, tb), 0)
    c = lax.broadcasted_iota(jnp.int32, (tb, tb), 1)
    delay_mat = jnp.concatenate([(r - c == dd).astype(BF16) for dd in range(1, CONV_WIDTH)], axis=0)

    def conv_silu(x_ref, p_ref, cw_ref):
        x = x_ref[...]
        xf = x.astype(F32)
        p_ref[pad:2 * pad, :] = xf[0:pad]
        delayed = jnp.dot(delay_mat, x, preferred_element_type=F32)
        acc = cw_ref[CONV_WIDTH - 1:CONV_WIDTH, :] * xf
        for dd in range(1, CONV_WIDTH):
            first = p_ref[pad - dd:2 * pad - dd, :]
            xd = jnp.concatenate([first, delayed[(dd - 1) * tb + pad:dd * tb]], axis=0)
            acc = acc + cw_ref[CONV_WIDTH - 1 - dd:CONV_WIDTH - dd, :] * xd
        p_ref[0:pad, :] = xf[tb - pad:tb]
        return acc * _sigmoid(acc)

    q_all = conv_silu(q_ref, qp_ref, cwq_ref)
    k_all = conv_silu(k_ref, kp_ref, cwk_ref)
    v_all = conv_silu(v_ref, vp_ref, cwv_ref)

    shift = int(math.log2(CHUNK))
    same = (r >> shift) == (c >> shift)
    incl = jnp.logical_and(same, c <= r)
    strict = jnp.logical_and(same, c < r)

    small = sm_ref[...]
    small_t = smt_ref[...]
    lane = lax.broadcasted_iota(jnp.int32, small.shape, 1)
    def split3(a):
        hi = a.astype(BF16)
        r1 = a - hi.astype(F32)
        mid = r1.astype(BF16)
        lo = (r1 - mid.astype(F32)).astype(BF16)
        return hi.astype(F32), mid.astype(F32), lo.astype(F32)

    part = 2 * n_heads
    s_hi, s_mid, s_lo = split3(small)
    small3 = jnp.where(lane < part, s_hi,
                       jnp.where(lane < 2 * part, pltpu.roll(s_mid, part, 1),
                                 jnp.where(lane < 3 * part, pltpu.roll(s_lo, 2 * part, 1), 0.0)))
    both = _bdot(jnp.concatenate([incl.astype(F32), same.astype(F32)], axis=0), small3)
    gcum = both[:tb]
    gtot = both[tb:]
    gcum_t = _bdot(jnp.concatenate(split3(small_t), axis=0),
                   jnp.logical_and(same, r <= c).astype(F32))
    sub3 = lax.broadcasted_iota(jnp.int32, gcum_t.shape, 0)

    heads = range(GDN_HG)
    hsl = [slice(hh * HEAD_DIM, (hh + 1) * HEAD_DIM) for hh in heads]
    qs = [q_all[:, hs] for hs in hsl]
    ks = [k_all[:, hs] for hs in hsl]
    vs = [v_all[:, hs] for hs in hsl]
    qs = [q * lax.rsqrt(jnp.sum(q * q, axis=-1, keepdims=True) + EPS) * (HEAD_DIM ** -0.5) for q in qs]
    ks = [k * lax.rsqrt(jnp.sum(k * k, axis=-1, keepdims=True) + EPS) for k in ks]

    def col_of(arr, idx):
        return jnp.sum(jnp.where(lane == idx, arr, 0.0), axis=-1, keepdims=True)

    def terms_of(pos, idx):
        return jnp.logical_or(pos == idx, jnp.logical_or(pos == idx + part, pos == idx + 2 * part))

    head_ids = [hg * GDN_HG + hh for hh in heads]
    betas = [col_of(small, hd) for hd in head_ids]
    gcs = [jnp.sum(jnp.where(terms_of(lane, hd + n_heads), gcum, 0.0), axis=-1, keepdims=True)
           for hd in head_ids]
    gls = [jnp.sum(jnp.where(terms_of(lane, hd + n_heads), gtot, 0.0), axis=-1, keepdims=True)
           for hd in head_ids]
    gc_rows = [jnp.sum(jnp.where(terms_of(sub3, hd + n_heads), gcum_t, 0.0), axis=0, keepdims=True)
               for hd in head_ids]

    decays = [jnp.where(incl, jnp.exp(jnp.minimum(gc - gr, 0.0)), 0.0) for gc, gr in zip(gcs, gc_rows)]
    kbs = [k * b for k, b in zip(ks, betas)]
    kks = [_bdot_nt(kb, k) for kb, k in zip(kbs, ks)]
    pws = [jnp.where(strict, -(kk * dc), 0.0) for kk, dc in zip(kks, decays)]
    n_chunks = tb // CHUNK
    cat_row = lax.broadcasted_iota(jnp.int32, (CHUNK, tb), 0)
    cat_lane = lax.broadcasted_iota(jnp.int32, (CHUNK, tb), 1)
    lane_chunk = cat_lane >> shift

    def block_diag(m_cat):
        return jnp.concatenate([jnp.where(lane_chunk == ci, m_cat, 0.0) for ci in range(n_chunks)], axis=0)

    def cat_of(m_bd):
        out = m_bd[0:CHUNK]
        for ci in range(1, n_chunks):
            out = out + m_bd[ci * CHUNK:(ci + 1) * CHUNK]
        return out

    pcats = [cat_of(pw) for pw in pws]
    eye_cat = ((cat_lane & (CHUNK - 1)) == cat_row).astype(F32)
    tcats = [eye_cat + pc for pc in pcats]
    pcats = [_bdot(pc, block_diag(pc)) for pc in pcats]
    n_levels = int(math.log2(CHUNK))
    for lev in range(1, n_levels):
        bds = [block_diag(pc) for pc in pcats]
        if lev < n_levels - 1:
            prods = [_bdot(jnp.concatenate([pc, tc], axis=0), bd) for pc, tc, bd in zip(pcats, tcats, bds)]
            pcats = [pr[:CHUNK] for pr in prods]
            tcats = [tc + pr[CHUNK:] for tc, pr in zip(tcats, prods)]
        else:
            tcats = [tc + _bdot(tc, bd) for tc, bd in zip(tcats, bds)]
    tmats = [block_diag(tc) for tc in tcats]
    egcs = [jnp.exp(gc) for gc in gcs]
    uws = [_bdot(tm, jnp.concatenate([v * b, kb * eg], axis=1))
           for tm, v, b, kb, eg in zip(tmats, vs, betas, kbs, egcs)]
    us = [uw[:, :HEAD_DIM] for uw in uws]
    ws = [uw[:, HEAD_DIM:] for uw in uws]
    qkm = [_bdot_nt(q, k) for q, k in zip(qs, ks)]
    qkm = [jnp.where(incl, x * dc, 0.0) for x, dc in zip(qkm, decays)]
    q_decs = [q * eg for q, eg in zip(qs, egcs)]
    k_ends = [k * jnp.exp(gl - gc) for k, gl, gc in zip(ks, gls, gcs)]

    for hh in heads:
        vn_ref[hh] = jnp.zeros((tb, HEAD_DIM), F32)
    outs = [[] for _ in heads]
    for ci in range(tb // CHUNK):
        cs = slice(ci * CHUNK, (ci + 1) * CHUNK)
        sts = [state_ref[hh] for hh in heads]
        ws_qs = [_bdot(jnp.concatenate([ws[hh][cs], q_decs[hh][cs]], axis=0), sts[hh]) for hh in heads]
        v_news = [us[hh][cs] - ws_qs[hh][:CHUNK] for hh in heads]
        for hh in heads:
            vn_ref[hh, cs, :] = v_news[hh]
        intra = [_bdot(qkm[hh][cs], vn_ref[hh]) for hh in heads]
        upd = [_bdot_tn(k_ends[hh][cs], v_news[hh]) for hh in heads]
        for hh in heads:
            outs[hh].append(ws_qs[hh][CHUNK:] + intra[hh])
            g_last = gls[hh][ci * CHUNK:ci * CHUNK + 1, :]
            state_ref[hh] = sts[hh] * jnp.exp(g_last) + upd[hh]
    for hh in heads:
        o = jnp.concatenate(outs[hh], axis=0)
        o = o * lax.rsqrt(jnp.mean(o * o, axis=-1, keepdims=True) + EPS) * gout_ref[...]
        zz = z_ref[:, hsl[hh]].astype(F32)
        o_ref[:, hsl[hh]] = (o * (zz * _sigmoid(zz))).astype(o_ref.dtype)


def _gated_delta(big, small, small_t, conv_w, g_out, bsz, seq, n_heads, d_model):
    t = bsz * seq
    tb = GDN_TB
    ns = seq // tb
    width = GDN_HG * HEAD_DIM
    nhg = n_heads // GDN_HG
    blocks_per_group = d_model // width
    rows_t = small_t.shape[0]

    def colspec(group):
        return pl.BlockSpec((tb, width), lambda b, h, s: (b * ns + s, group * blocks_per_group + h))

    def cwspec(group):
        return pl.BlockSpec((CONV_WIDTH, width), lambda b, h, s: (0, group * blocks_per_group + h))

    return pl.pallas_call(
        functools.partial(_gdn_kernel, n_heads=n_heads),
        grid=(bsz, nhg, ns),
        in_specs=[
            colspec(0), colspec(1), colspec(2), colspec(3),
            pl.BlockSpec((tb, LANES), lambda b, h, s: (b * ns + s, 0)),
            pl.BlockSpec((rows_t, tb), lambda b, h, s: (0, b * ns + s)),
            cwspec(0), cwspec(1), cwspec(2),
            pl.BlockSpec((1, HEAD_DIM), lambda b, h, s: (0, 0)),
        ],
        out_specs=pl.BlockSpec((tb, width), lambda b, h, s: (b * ns + s, h)),
        out_shape=jax.ShapeDtypeStruct((t, d_model), BF16),
        scratch_shapes=[
            pltpu.VMEM((GDN_HG, HEAD_DIM, HEAD_DIM), F32),
            pltpu.VMEM((2 * SUBLANES, width), F32),
            pltpu.VMEM((2 * SUBLANES, width), F32),
            pltpu.VMEM((2 * SUBLANES, width), F32),
            pltpu.VMEM((GDN_HG, tb, HEAD_DIM), F32),
        ],
        compiler_params=_cparams(("parallel", "parallel", "arbitrary")),
        name="gated_delta",
    )(big, big, big, big, small, small_t, conv_w, conv_w, conv_w, g_out)


def _t5_bucket(n):
    max_exact = N_BUCKETS // 2
    nf = jnp.maximum(n, 1).astype(F32)
    large = max_exact + (jnp.log(nf / max_exact) / math.log(MAX_DISTANCE / max_exact)
                         * (N_BUCKETS - max_exact)).astype(jnp.int32)
    large = jnp.minimum(large, N_BUCKETS - 1)
    return jnp.where(n < max_exact, n, large)


def _attn_kernel(rb_ref, q_ref, k_ref, v_ref, lam_ref, gsub_ref, o_ref,
                 bias_ref, m_ref, acc_ref, sa_ref, sb_ref, qs_ref, *, lam_init):
    hg = pl.program_id(0)
    b = pl.program_id(1)
    bq, bk = ATT_BQ, ATT_BK
    heads = range(ATT_HG)
    hsl = [slice(hh * HEAD_DIM, (hh + 1) * HEAD_DIM) for hh in heads]

    @pl.when(b == 0)
    def _():
        blk = LANES
        i = lax.broadcasted_iota(jnp.int32, (blk, blk), 0)
        jj = lax.broadcasted_iota(jnp.int32, (blk, blk), 1)
        for hh in heads:
            head = hg * ATT_HG + hh
            far = rb_ref[N_BUCKETS - 1, head]

            def toeplitz(offset):
                bucket = _t5_bucket(jnp.maximum(i - jj + offset, 0))
                out = jnp.zeros((blk, blk), F32)
                for cc in range(N_BUCKETS):
                    out = jnp.where(bucket == cc, rb_ref[cc, head] - far, out)
                return out

            on_diag = jnp.where(i >= jj, toeplitz(0), NEG_BIG)
            next_diag = toeplitz(blk)
            kinds = {0: on_diag, 1: next_diag}
            bias_ref[hh, 2] = jnp.zeros((bq, bk), F32)
            for slot in range(2):
                for rr in range(bq // blk):
                    for cc in range(bk // blk):
                        delta = rr - cc + slot * (bk // blk)
                        if delta < 0:
                            tile = jnp.full((blk, blk), NEG_BIG, F32)
                        else:
                            tile = kinds.get(delta, jnp.zeros((blk, blk), F32))
                        bias_ref[hh, slot, rr * blk:(rr + 1) * blk, cc * blk:(cc + 1) * blk] = tile

    lane = lax.broadcasted_iota(jnp.int32, (bq, HEAD_DIM), 1)
    ones_col = (lax.broadcasted_iota(jnp.int32, (bk, HEAD_DIM), 1) == 0).astype(BF16)
    lam_p = lam_ref[...]
    s1 = jnp.sum(lam_p[0:1] * lam_p[1:2], axis=-1, keepdims=True)
    s2 = jnp.sum(lam_p[2:3] * lam_p[3:4], axis=-1, keepdims=True)
    lam = jnp.exp(s1) - jnp.exp(s2) + lam_init

    def query_block(qi, outer):
        rows = pl.ds(pl.multiple_of(qi * bq, bq), bq)
        m_ref[...] = jnp.full(m_ref.shape, NEG_BIG, F32)
        acc_ref[...] = jnp.zeros(acc_ref.shape, F32)
        for hh in heads:
            q = q_ref[rows, hsl[hh]]
            zero = jnp.zeros_like(q)
            qs_ref[hh, 0:bq, :] = jnp.where(lane < DH_DIFF, q, zero)
            qs_ref[hh, bq:2 * bq, :] = jnp.where(lane < DH_DIFF, zero, q)

        def scores(j, s_ref):
            ks = pl.multiple_of(j * bk, bk)
            for hh in heads:
                s_ref[hh] = lax.dot_general(qs_ref[hh], k_ref[pl.ds(ks, bk), hsl[hh]],
                                            (((1,), (1,)), ((), ())), preferred_element_type=F32)

        def absorb(j, s_ref, biased=True):
            ks = pl.multiple_of(j * bk, bk)
            v_exts = [jnp.concatenate([v_ref[pl.ds(ks, bk), hs], ones_col], axis=1) for hs in hsl]
            if biased:
                slot = jnp.minimum(qi - j, 2)
                scs = [jnp.concatenate([s_ref[hh, 0:bq, :] + bias_ref[hh, slot],
                                        s_ref[hh, bq:2 * bq, :] + bias_ref[hh, slot]], axis=0)
                       for hh in heads]
            else:
                scs = [s_ref[hh] for hh in heads]
            m_olds = [m_ref[hh] for hh in heads]
            m_news = [jnp.maximum(mo, jnp.max(sc, axis=-1, keepdims=True)) for mo, sc in zip(m_olds, scs)]
            ps = [jnp.exp(sc - mn) for sc, mn in zip(scs, m_news)]
            pvs = [jnp.dot(p.astype(BF16), ve, preferred_element_type=F32) for p, ve in zip(ps, v_exts)]
            for hh in heads:
                acc_ref[hh] = jnp.exp(m_olds[hh] - m_news[hh]) * acc_ref[hh] + pvs[hh]
                m_ref[hh] = m_news[hh]

        n_tiles = qi + 1
        scores(0, sa_ref)

        def pair_body(jj, carry, biased):
            j0 = 2 * jj
            scores(j0 + 1, sb_ref)
            absorb(j0, sa_ref, biased)
            scores(jnp.minimum(j0 + 2, qi), sa_ref)
            absorb(j0 + 1, sb_ref, biased)
            return carry

        n_far_pairs = jnp.maximum(qi - 1, 0) // 2
        lax.fori_loop(0, n_far_pairs, functools.partial(pair_body, biased=False), 0)
        lax.fori_loop(n_far_pairs, n_tiles // 2, functools.partial(pair_body, biased=True), 0)

        @pl.when(n_tiles % 2 == 1)
        def _():
            absorb(qi, sa_ref)

        for hh in heads:
            acc = acc_ref[hh]
            num = acc[:, :HEAD_DIM]
            den = acc[:, HEAD_DIM:HEAD_DIM + 1]
            o = num[:bq] / den[:bq] - lam * (num[bq:] / den[bq:])
            o = o * lax.rsqrt(jnp.mean(o * o, axis=-1, keepdims=True) + EPS) * gsub_ref[...]
            o_ref[rows, hsl[hh]] = (o * (1.0 - lam_init)).astype(o_ref.dtype)
        return outer

    lax.fori_loop(0, q_ref.shape[0] // bq, query_block, 0)


def _diff_attention(proj_qk, proj_plain, rel_bias, lam_params, g_subln, bsz, seq, n_heads, d_model,
                    lam_init):
    t = bsz * seq
    nq = seq // ATT_BQ
    assert ATT_BQ == ATT_BK and MAX_DISTANCE <= LANES and n_heads % ATT_HG == 0
    width = ATT_HG * HEAD_DIM
    per = d_model // width
    vcol = 4 * per
    return pl.pallas_call(
        functools.partial(_attn_kernel, lam_init=lam_init),
        grid=(n_heads // ATT_HG, bsz),
        in_specs=[
            pl.BlockSpec(memory_space=pltpu.SMEM),
            pl.BlockSpec((seq, width), lambda h, b: (b, h)),
            pl.BlockSpec((seq, width), lambda h, b: (b, per + h)),
            pl.BlockSpec((seq, width), lambda h, b: (b, vcol + h)),
            pl.BlockSpec((4, DH_DIFF), lambda h, b: (0, 0)),
            pl.BlockSpec((1, HEAD_DIM), lambda h, b: (0, 0)),
        ],
        out_specs=pl.BlockSpec((seq, width), lambda h, b: (b, h)),
        out_shape=jax.ShapeDtypeStruct((t, d_model), BF16),
        scratch_shapes=[
            pltpu.VMEM((ATT_HG, 3, ATT_BQ, ATT_BK), F32),
            pltpu.VMEM((ATT_HG, 2 * ATT_BQ, 1), F32),
            pltpu.VMEM((ATT_HG, 2 * ATT_BQ, 2 * HEAD_DIM), F32),
            pltpu.VMEM((ATT_HG, 2 * ATT_BQ, ATT_BK), F32),
            pltpu.VMEM((ATT_HG, 2 * ATT_BQ, ATT_BK), F32),
            pltpu.VMEM((ATT_HG, 2 * ATT_BQ, HEAD_DIM), BF16),
        ],
        compiler_params=_cparams(("arbitrary", "arbitrary")),
        name="diff_attention",
    )(rel_bias, proj_qk, proj_qk, proj_plain, lam_params, g_subln)


def _mix_kernel(ga_ref, gb_ref, oa_ref, od_ref, x_ref, wo_ref, gffn_ref, wr_ref, br_ref,
                x1_ref, h2_ref, topi_ref, topw_ref, rank_ref, cnt_ref, carry_ref):
    i = pl.program_id(0)
    tm = MIX_TM

    @pl.when(i == 0)
    def _():
        carry_ref[...] = jnp.zeros_like(carry_ref)

    tp = tm // MIX_PARTS
    parts = range(MIX_PARTS)
    rows = [slice(pp * tp, (pp + 1) * tp) for pp in parts]
    mixes = [ga_ref[rs, :] * oa_ref[rs, :] + gb_ref[rs, :] * od_ref[rs, :] for rs in rows]
    x1s = [x_ref[rs, :] + jnp.dot(mx, wo_ref[...], preferred_element_type=F32) for rs, mx in zip(rows, mixes)]
    for rs, x1 in zip(rows, x1s):
        x1_ref[rs, :] = x1
    h2s = [x1 * lax.rsqrt(jnp.mean(x1 * x1, axis=-1, keepdims=True) + EPS) * gffn_ref[...] for x1 in x1s]
    for rs, h2 in zip(rows, h2s):
        h2_ref[rs, :] = _pack_halves(h2)

    curs = [lax.dot_general(wr_ref[...], h2, (((1,), (1,)), ((), ())), preferred_element_type=F32,
                            precision=lax.Precision.HIGHEST) + br_ref[...] for h2 in h2s]
    eidx = lax.broadcasted_iota(jnp.int32, curs[0].shape, 0).astype(F32)
    vals = [[] for _ in parts]
    hots = [[] for _ in parts]
    for kk in range(TOP_K):
        mxs = [jnp.max(cur, axis=0, keepdims=True) for cur in curs]
        idxs = [jnp.min(jnp.where(cur == mx, eidx, float(N_EXPERTS)), axis=0, keepdims=True)
                for cur, mx in zip(curs, mxs)]
        for pp in parts:
            hot = eidx == idxs[pp]
            vals[pp].append(mxs[pp])
            hots[pp].append(hot)
            topi_ref[kk:kk + 1, rows[pp]] = idxs[pp].astype(jnp.int32)
            curs[pp] = jnp.where(hot, -jnp.inf, curs[pp])
    for pp in parts:
        exps = [jnp.exp(vv - vals[pp][0]) for vv in vals[pp]]
        denom = exps[0] + exps[1] + exps[2] + exps[3]
        for kk in range(TOP_K):
            topw_ref[kk:kk + 1, rows[pp]] = exps[kk] / denom

    r = lax.broadcasted_iota(jnp.int32, (tp, tp), 0)
    c = lax.broadcasted_iota(jnp.int32, (tp, tp), 1)
    earlier = (r < c).astype(F32)
    sel_fs = []
    for pp in parts:
        sel = hots[pp][0]
        for kk in range(1, TOP_K):
            sel = jnp.logical_or(sel, hots[pp][kk])
        sel_fs.append(sel.astype(F32))
    within = [_bdot(sf, earlier) for sf in sel_fs]
    totals = [jnp.sum(sf, axis=-1, keepdims=True) for sf in sel_fs]
    run = carry_ref[...]
    for pp in parts:
        before = within[pp] + run
        for kk in range(TOP_K):
            rank_ref[kk:kk + 1, rows[pp]] = jnp.sum(jnp.where(hots[pp][kk], before, 0.0), axis=0,
                                                    keepdims=True).astype(jnp.int32)
        run = run + totals[pp]
    carry_ref[...] = run
    cnt_ref[...] = run.astype(jnp.int32)


def _mix_project_route(proj_gate, oa, od, x2d, w_o, g_ffn, w_r_t, b_r, d_model):
    t = x2d.shape[0]
    tm = MIX_TM
    full = lambda shape: pl.BlockSpec(shape, lambda i: (0, 0))
    row = lambda: pl.BlockSpec((tm, d_model), lambda i: (i, 0))
    krow = lambda: pl.BlockSpec((TOP_K, tm), lambda i: (0, i))
    return pl.pallas_call(
        _mix_kernel,
        grid=(t // tm,),
        in_specs=[
            pl.BlockSpec((tm, d_model), lambda i: (i, 0)),
            pl.BlockSpec((tm, d_model), lambda i: (i, 1)),
            row(), row(), row(),
            full((d_model, d_model)), full((1, d_model)), full((N_EXPERTS, d_model)), full((N_EXPERTS, 1)),
        ],
        out_specs=[row(), pl.BlockSpec((tm, d_model // 2), lambda i: (i, 0)),
                   krow(), krow(), krow(), full((N_EXPERTS, 1))],
        out_shape=[
            jax.ShapeDtypeStruct((t, d_model), F32),
            jax.ShapeDtypeStruct((t, d_model // 2), jnp.int32),
            jax.ShapeDtypeStruct((TOP_K, t), jnp.int32),
            jax.ShapeDtypeStruct((TOP_K, t), F32),
            jax.ShapeDtypeStruct((TOP_K, t), jnp.int32),
            jax.ShapeDtypeStruct((N_EXPERTS, 1), jnp.int32),
        ],
        scratch_shapes=[pltpu.VMEM((N_EXPERTS, 1), F32)],
        compiler_params=_cparams(("arbitrary",)),
        name="merge_outproj_route",
    )(proj_gate, proj_gate, oa, od, x2d, w_o, g_ffn, w_r_t, b_r)


def _pack_halves(x):
    half = x.shape[1] // 2
    bits = pltpu.bitcast(x.astype(BF16).astype(F32), jnp.int32)
    return bits[:, :half] | lax.shift_right_logical(bits[:, half:], 16)


def _unpack_halves(p):
    hi = pltpu.bitcast(p & jnp.int32(-65536), F32)
    lo = pltpu.bitcast(lax.shift_left(p, 16), F32)
    return jnp.concatenate([hi, lo], axis=1)


def _expert_kernel(be_ref, nu_ref, x_ref, wup_ref, bup_ref, wdn_ref, bdn_ref, *rest):
    y_ref, wup_bf, wdn_bf = rest[-3:]
    i = pl.program_id(0)
    d_ff = wdn_ref.shape[1]

    @pl.when(jnp.logical_or(i == 0, be_ref[i] != be_ref[jnp.maximum(i - 1, 0)]))
    def _():
        rr = lax.broadcasted_iota(jnp.int32, (2 * LANES, 2 * LANES), 0)
        cc = lax.broadcasted_iota(jnp.int32, (2 * LANES, 2 * LANES), 1)
        pick = jnp.where(cc < LANES, 2 * cc, 2 * (cc - LANES) + 1)
        perm = (rr == pick).astype(BF16)
        for g in range(wup_ref.shape[2] // (2 * LANES)):
            cs = slice(g * 2 * LANES, (g + 1) * 2 * LANES)
            wup_bf[:, cs] = jnp.dot(wup_ref[0, :, cs].astype(BF16), perm,
                                    preferred_element_type=F32).astype(BF16)
        wdn_bf[...] = wdn_ref[0].astype(BF16)

    @pl.when(i < nu_ref[0])
    def _():
        x = _unpack_halves(x_ref[...])
        hid = jnp.dot(x.astype(BF16), wup_bf[...], preferred_element_type=F32) + bup_ref[0]
        acts = []
        for g in range(hid.shape[1] // (2 * LANES)):
            glu = jnp.minimum(hid[:, g * 2 * LANES:g * 2 * LANES + LANES], SWIGLU_LIMIT)
            lin = jnp.clip(hid[:, g * 2 * LANES + LANES:(g + 1) * 2 * LANES], -SWIGLU_LIMIT, SWIGLU_LIMIT)
            acts.append(glu * _sigmoid(SWIGLU_ALPHA * glu) * (lin + 1.0))
        act = jnp.concatenate(acts, axis=1)
        assert act.shape[1] == d_ff
        y = jnp.dot(act.astype(BF16), wdn_bf[...], preferred_element_type=F32) + bdn_ref[0]
        y_ref[...] = _pack_halves(y)

    @pl.when(i >= nu_ref[0])
    def _():
        y_ref[...] = jnp.zeros(y_ref.shape, y_ref.dtype)


def _experts(block_e, n_used, xs, y_prev, first_block, n_rows_total, w_up, b_up, w_down, b_down):
    n_rows, half = xs.shape
    d = w_up.shape[1]
    nb = n_rows // MOE_RB
    two_ff = w_up.shape[2]
    d_ff = w_down.shape[1]
    in_specs = [
        pl.BlockSpec((MOE_RB, half), lambda i, be, nu: (jnp.maximum(jnp.minimum(i, nu[0] - 1), 0), 0)),
        pl.BlockSpec((1, d, two_ff), lambda i, be, nu: (be[i], 0, 0)),
        pl.BlockSpec((1, 1, two_ff), lambda i, be, nu: (be[i], 0, 0)),
        pl.BlockSpec((1, d_ff, d), lambda i, be, nu: (be[i], 0, 0)),
        pl.BlockSpec((1, 1, d), lambda i, be, nu: (be[i], 0, 0)),
    ]
    operands = [block_e, n_used, xs, w_up, b_up, w_down, b_down]
    aliases = {}
    if y_prev is not None:
        in_specs.append(pl.BlockSpec(memory_space=pl.ANY))
        aliases = {len(operands): 0}
        operands.append(y_prev)
    grid_spec = pltpu.PrefetchScalarGridSpec(
        num_scalar_prefetch=2,
        grid=(nb,),
        in_specs=in_specs,
        out_specs=pl.BlockSpec((MOE_RB, half), lambda i, be, nu: (first_block + i, 0)),
        scratch_shapes=[pltpu.VMEM((d, two_ff), BF16), pltpu.VMEM((d_ff, d), BF16)],
    )
    return pl.pallas_call(
        _expert_kernel,
        grid_spec=grid_spec,
        out_shape=jax.ShapeDtypeStruct((n_rows_total, half), jnp.int32),
        input_output_aliases=aliases,
        compiler_params=_cparams(("arbitrary",)),
        name="moe_experts",
    )(*operands)


def _sc_invert_slots(dest_flat, n_rows):
    n_assign = dest_flat.shape[0]
    n_workers = SC_CORES * SC_SUBCORES
    rows_per_w = n_rows // n_workers
    chunk = SC_SCAN_CHUNK
    assert n_rows % n_workers == 0 and rows_per_w % SC_LANES == 0 and n_assign % chunk == 0
    mesh = plsc.VectorSubcoreMesh(core_axis_name="c", subcore_axis_name="s",
                                  num_cores=SC_CORES, num_subcores=SC_SUBCORES)

    def body(dest_hbm, out_hbm, dest_v, map_v):
        wid = lax.axis_index("s") * SC_CORES + lax.axis_index("c")
        base = wid * rows_per_w
        lanes = lax.broadcasted_iota(jnp.int32, (SC_LANES,), 0)

        @pl.loop(0, rows_per_w, step=SC_LANES)
        def _(r0):
            map_v[pl.ds(r0, SC_LANES)] = jnp.full((SC_LANES,), -1, jnp.int32)

        @pl.loop(0, n_assign // chunk)
        def _(ci):
            pltpu.sync_copy(dest_hbm.at[pl.ds(ci * chunk, chunk)], dest_v)

            @pl.loop(0, chunk, step=SC_LANES)
            def _(j):
                local = dest_v[pl.ds(j, SC_LANES)] - base
                mine = jnp.logical_and(local >= 0, local < rows_per_w)
                plsc.store_scatter(map_v, [jnp.where(mine, local, 0)], ci * chunk + j + lanes, mask=mine)

        pltpu.sync_copy(map_v, out_hbm.at[pl.ds(base, rows_per_w)])

    return pl.kernel(
        body,
        out_type=jax.ShapeDtypeStruct((n_rows,), jnp.int32),
        mesh=mesh,
        scratch_types=[pltpu.VMEM((chunk,), jnp.int32), pltpu.VMEM((rows_per_w,), jnp.int32)],
        compiler_params=pltpu.CompilerParams(needs_layout_passes=False),
        name="moe_slot_inverse",
    )(dest_flat)


def _sc_gather_rows(table, idx):
    n_idx = idx.shape[0]
    d = table.shape[1]
    n_workers = SC_CORES * SC_SUBCORES
    per_worker = n_idx // n_workers
    n_chunks = per_worker // SC_GATHER_ROWS
    assert n_idx % n_workers == 0 and per_worker % SC_GATHER_ROWS == 0
    mesh = plsc.VectorSubcoreMesh(core_axis_name="c", subcore_axis_name="s",
                                  num_cores=SC_CORES, num_subcores=SC_SUBCORES)

    assert n_chunks % 2 == 0

    def body(table_hbm, idx_hbm, out_hbm, idx_v, rows_a, rows_b, sem_a, sem_b):
        wid = lax.axis_index("s") * SC_CORES + lax.axis_index("c")
        base = wid * per_worker
        pltpu.sync_copy(idx_hbm.at[pl.ds(base, per_worker)], idx_v)

        def gather(ci, rows_v, sem):
            off = pl.multiple_of(ci * SC_GATHER_ROWS, SC_GATHER_ROWS)
            return pltpu.make_async_copy(table_hbm.at[idx_v.at[pl.ds(off, SC_GATHER_ROWS)]], rows_v, sem)

        def put(ci, rows_v):
            off = pl.multiple_of(ci * SC_GATHER_ROWS, SC_GATHER_ROWS)
            pltpu.sync_copy(rows_v, out_hbm.at[pl.ds(base + off, SC_GATHER_ROWS)])

        gather(0, rows_a, sem_a).start()

        @pl.loop(0, n_chunks, step=2)
        def _(ci):
            gather(ci + 1, rows_b, sem_b).start()
            gather(ci, rows_a, sem_a).wait()
            put(ci, rows_a)
            nxt = jnp.minimum(ci + 2, n_chunks - 1)
            gather(nxt, rows_a, sem_a).start()
            gather(ci + 1, rows_b, sem_b).wait()
            put(ci + 1, rows_b)

        gather(n_chunks - 1, rows_a, sem_a).wait()

    return pl.kernel(
        body,
        out_type=jax.ShapeDtypeStruct((n_idx, d), table.dtype),
        mesh=mesh,
        scratch_types=[
            pltpu.VMEM((per_worker,), jnp.int32),
            pltpu.VMEM((SC_GATHER_ROWS, d), table.dtype),
            pltpu.VMEM((SC_GATHER_ROWS, d), table.dtype),
            pltpu.SemaphoreType.DMA,
            pltpu.SemaphoreType.DMA,
        ],
        name="moe_slot_gather",
    )(table, idx)


def _combine_kernel(x1_ref, w_ref, y0_ref, y1_ref, y2_ref, y3_ref, o_ref):
    w = w_ref[...]
    out = x1_ref[...]
    for kk, y_ref in enumerate((y0_ref, y1_ref, y2_ref, y3_ref)):
        out = out + w[:, kk:kk + 1] * _unpack_halves(y_ref[...])
    o_ref[...] = out


def _combine(x1, w_tok, y_slots):
    t, d = x1.shape
    tc = COMB_TC
    nt = t // tc
    yspec = lambda kk: pl.BlockSpec((tc, d // 2), lambda i: (kk * nt + i, 0))
    return pl.pallas_call(
        _combine_kernel,
        grid=(nt,),
        in_specs=[
            pl.BlockSpec((tc, d), lambda i: (i, 0)),
            pl.BlockSpec((tc, TOP_K), lambda i: (i, 0)),
            yspec(0), yspec(1), yspec(2), yspec(3),
        ],
        out_specs=pl.BlockSpec((tc, d), lambda i: (i, 0)),
        out_shape=jax.ShapeDtypeStruct((t, d), F32),
        compiler_params=_cparams(("parallel",)),
        name="moe_combine",
    )(x1, w_tok, y_slots, y_slots, y_slots, y_slots)


def _moe(x1, h2, topi, topw, rank, counts, w_up, b_up, w_down, b_down):
    t, d = x1.shape
    n_assign = t * TOP_K
    nb = -(-n_assign // MOE_RB) + N_EXPERTS
    n_rows = nb * MOE_RB
    counts = counts[:, 0]
    padded = (counts + MOE_RB - 1) // MOE_RB * MOE_RB
    padded_end = jnp.cumsum(padded)
    padded_start = padded_end - padded
    expert_ids = jnp.arange(N_EXPERTS, dtype=jnp.int32)[:, None, None]
    start_of = jnp.sum(jnp.where(topi[None] == expert_ids, padded_start[:, None, None], 0), axis=0)
    dest = (start_of + rank).astype(jnp.int32)
    n_used = (padded_end[-1] // MOE_RB).astype(jnp.int32)
    blk = jnp.minimum(jnp.arange(nb, dtype=jnp.int32), n_used - 1)
    block_e = jnp.minimum(jnp.sum(padded_end[None, :] <= (blk * MOE_RB)[:, None], axis=1),
                          N_EXPERTS - 1).astype(jnp.int32)
    slot_of = _sc_invert_slots(dest.reshape(-1), n_rows)
    src_tok = jnp.where(slot_of < 0, jnp.arange(n_rows, dtype=jnp.int32), slot_of) % t

    nb_a = nb // 2
    y_rows = None
    for first, n_blk in ((0, nb_a), (nb_a, nb - nb_a)):
        xs = _sc_gather_rows(h2, lax.slice(src_tok, (first * MOE_RB,), ((first + n_blk) * MOE_RB,)))
        used = jnp.clip(n_used - first, 0, n_blk).reshape(1)
        y_rows = _experts(lax.slice(block_e, (first,), (first + n_blk,)), used, xs, y_rows, first, n_rows,
                          w_up, b_up, w_down, b_down)
    y_slots = _sc_gather_rows(y_rows, dest.reshape(-1))
    return _combine(x1, topw.T, y_slots)


def kernel(x, g_mix, w_in, b_gate, conv_w, a_log, dt_bias, g_delta_out, q_norm, k_norm, lambda_q1, lambda_k1, lambda_q2, lambda_k2, g_subln, rel_bias, w_o, g_ffn, w_router, b_router, w_up, b_up, w_down, b_down):
    bsz, seq, d = x.shape
    depth = g_mix.shape[0]
    n_heads = d // HEAD_DIM
    t = bsz * seq
    d_ff = w_down.shape[2]
    assert d % PROJ_TN == 0 and t % PROJ_TM == 0 and seq % GDN_TB == 0 and seq % ATT_BQ == 0
    assert t % MIX_TM == 0 and t % COMB_TC == 0 and n_heads % GDN_HG == 0
    assert (t * TOP_K) % MOE_RB == 0
    assert 2 * n_heads <= 2 * SUBLANES

    x2d = x.reshape(t, d)
    for l in range(depth):
        wl = w_in[l]
        c0 = 4 * d
        c1 = c0 + 2 * n_heads
        c2 = c1 + 2 * d
        c3 = c2 + d
        w_small = jnp.pad(wl[:, c0:c1], ((0, 0), (0, LANES - 2 * n_heads)))
        gm = g_mix[l].reshape(1, d)
        head_pad = jnp.zeros((LANES - 2 * n_heads,), F32)
        alog = jnp.concatenate([jnp.zeros((n_heads,), F32), a_log[l], head_pad])
        dtb = jnp.concatenate([jnp.zeros((n_heads,), F32), dt_bias[l], head_pad])
        rows_t = 2 * n_heads
        beta_decay = (w_small.astype(BF16), w_small[:, :rows_t].T.astype(BF16),
                      alog.reshape(1, LANES), dtb.reshape(1, LANES),
                      alog[:rows_t].reshape(rows_t, 1), dtb[:rows_t].reshape(rows_t, 1), n_heads)
        w_plain = jnp.concatenate([wl[:, :c0], wl[:, c2:c3]], axis=1).astype(BF16)
        proj_plain, small, small_t = _input_projection(x2d, gm, w_plain, jnp.zeros((1, 5 * d), F32), "plain",
                                                       beta_decay)
        qk_gain = jnp.concatenate([jnp.tile(q_norm[l] * (DH_DIFF ** -0.5), 2 * n_heads),
                                   jnp.tile(k_norm[l], 2 * n_heads)]).reshape(1, 2 * d)
        proj_qk = _input_projection(x2d, gm, wl[:, c1:c2].astype(BF16), qk_gain, "qknorm")
        proj_gate = _input_projection(x2d, gm, wl[:, c3:].astype(BF16), b_gate[l].reshape(1, 2 * d), "gate")

        oa = _gated_delta(proj_plain, small, small_t, conv_w[l], g_delta_out[l].reshape(1, HEAD_DIM),
                          bsz, seq, n_heads, d)

        lam_init = 0.8 - 0.6 * math.exp(-0.3 * l)
        lam_params = jnp.stack([lambda_q1[l], lambda_k1[l], lambda_q2[l], lambda_k2[l]])
        od = _diff_attention(proj_qk, proj_plain, rel_bias, lam_params, g_subln[l].reshape(1, HEAD_DIM),
                             bsz, seq, n_heads, d, lam_init)

        x1, h2, topi, topw, rank, counts = _mix_project_route(
            proj_gate, oa, od, x2d, w_o[l].astype(BF16), g_ffn[l].reshape(1, d),
            w_router[l].T, b_router[l].reshape(N_EXPERTS, 1), d)

        b_up_l = b_up[l].reshape(N_EXPERTS, 2 * d_ff // (2 * LANES), LANES, 2)
        b_up_l = jnp.swapaxes(b_up_l, 2, 3).reshape(N_EXPERTS, 1, 2 * d_ff)
        x2d = _moe(x1, h2, topi, topw, rank, counts, w_up[l], b_up_l,
                   w_down[l], b_down[l].reshape(N_EXPERTS, 1, d))
    return x2d.reshape(bsz, seq, d)
```

```python
import functools
import math

import jax
import jax.numpy as jnp
from jax import lax
from jax.experimental import pallas as pl
from jax.experimental.pallas import tpu as pltpu
from jax.experimental.pallas import tpu_sc as plsc

F32 = jnp.float32
BF16 = jnp.bfloat16

HEAD_DIM = 128
DH_DIFF = HEAD_DIM // 2
CONV_WIDTH = 4
CHUNK = 64
N_BUCKETS = 32
MAX_DISTANCE = 128
N_EXPERTS = 32
TOP_K = 4
SWIGLU_LIMIT = 7.0
SWIGLU_ALPHA = 1.702
EPS = 1e-6
NEG_BIG = -1e30

LANES = 128
SUBLANES = 8
VMEM_LIMIT = 56 * 1024 * 1024
SC_CORES = 2
SC_SUBCORES = 16
SC_LANES = 16
SC_GATHER_ROWS = 64
SC_SCAN_CHUNK = 4096

PROJ_TM = 2048
PROJ_TN = 1024
PROJ_CHUNK = 256
GDN_TB = 256
GDN_HG = 8
ATT_HG = 1
ATT_BQ = 512
ATT_BK = 512
MIX_TM = 1024
MIX_PARTS = 2
MOE_RB = 512
COMB_TC = 512


def _cparams(sem):
    return pltpu.CompilerParams(dimension_semantics=sem, vmem_limit_bytes=VMEM_LIMIT)


def _sigmoid(x):
    return 0.5 * jnp.tanh(0.5 * x) + 0.5


def _bdot(a, b):
    return jnp.dot(a.astype(BF16), b.astype(BF16), preferred_element_type=F32)


def _bdot_nt(a, b):
    return lax.dot_general(a.astype(BF16), b.astype(BF16), (((1,), (1,)), ((), ())),
                           preferred_element_type=F32)


def _bdot_tn(a, b):
    return lax.dot_general(a.astype(BF16), b.astype(BF16), (((0,), (0,)), ((), ())),
                           preferred_element_type=F32)


def _beta_decay(acc, idx, alog, dtb, n_heads):
    beta = _sigmoid(acc)
    z = acc + dtb
    softplus = jnp.maximum(z, 0.0) + jnp.log1p(jnp.exp(-jnp.abs(z)))
    gdec = -jnp.exp(alog) * softplus
    return jnp.where(idx < n_heads, beta, jnp.where(idx < 2 * n_heads, gdec, 0.0))


def _proj_kernel(x_ref, g_ref, w_ref, aux_ref, *rest, mode, n_heads):
    o_ref, h_ref = rest[-2:] if n_heads is None else (rest[6], rest[-1])

    @pl.when(pl.program_id(1) == 0)
    def _():
        x = x_ref[...]
        ms = jnp.mean(x * x, axis=-1, keepdims=True)
        h_ref[...] = (x * lax.rsqrt(ms + EPS) * g_ref[...]).astype(BF16)
        if n_heads is not None:
            ws_ref, wst_ref, alog_ref, dtb_ref, alog_t_ref, dtb_t_ref, _, os_ref, ost_ref, _ = rest
            hb = h_ref[...]
            acc = jnp.dot(hb, ws_ref[...], preferred_element_type=F32)
            lane = lax.broadcasted_iota(jnp.int32, acc.shape, 1)
            os_ref[...] = _beta_decay(acc, lane, alog_ref[...], dtb_ref[...], n_heads)
            acc_t = lax.dot_general(wst_ref[...], hb, (((1,), (1,)), ((), ())),
                                    preferred_element_type=F32)
            sub = lax.broadcasted_iota(jnp.int32, acc_t.shape, 0)
            ost_ref[...] = _beta_decay(acc_t, sub, alog_t_ref[...], dtb_t_ref[...], n_heads)

    h = h_ref[...]
    lo = lax.broadcasted_iota(jnp.int32, (1, LANES), 1) < DH_DIFF
    for c in range(PROJ_TN // PROJ_CHUNK):
        cs = slice(c * PROJ_CHUNK, (c + 1) * PROJ_CHUNK)
        acc = jnp.dot(h, w_ref[:, cs], preferred_element_type=F32)
        if mode == "plain":
            o_ref[:, cs] = acc.astype(o_ref.dtype)
        elif mode == "gate":
            o_ref[:, cs] = _sigmoid(acc + aux_ref[:, cs]).astype(o_ref.dtype)
        else:
            for g in range(PROJ_CHUNK // LANES):
                sl = slice(c * PROJ_CHUNK + g * LANES, c * PROJ_CHUNK + (g + 1) * LANES)
                y = acc[:, g * LANES:(g + 1) * LANES]
                y2 = y * y
                s_lo = jnp.sum(jnp.where(lo, y2, 0.0), axis=-1, keepdims=True)
                s_hi = jnp.sum(jnp.where(lo, 0.0, y2), axis=-1, keepdims=True)
                r = jnp.where(lo, lax.rsqrt(s_lo / DH_DIFF + EPS), lax.rsqrt(s_hi / DH_DIFF + EPS))
                o_ref[:, sl] = (y * r * aux_ref[:, sl]).astype(o_ref.dtype)


def _input_projection(x2d, g_mix, w, aux, mode, beta_decay=None):
    t, d = x2d.shape
    n = w.shape[1]
    full = lambda shape: pl.BlockSpec(shape, lambda i, j: (0, 0))
    in_specs = [
        pl.BlockSpec((PROJ_TM, d), lambda i, j: (i, 0)),
        full((1, d)),
        pl.BlockSpec((d, PROJ_TN), lambda i, j: (0, j)),
        pl.BlockSpec((1, PROJ_TN), lambda i, j: (0, j)),
    ]
    operands = [x2d, g_mix, w, aux]
    out_specs = [pl.BlockSpec((PROJ_TM, PROJ_TN), lambda i, j: (i, j))]
    out_shape = [jax.ShapeDtypeStruct((t, n), BF16)]
    n_heads = None
    if beta_decay is not None:
        n_heads = beta_decay[-1]
        rows_t = 2 * n_heads
        in_specs += [full((d, LANES)), full((rows_t, d)), full((1, LANES)), full((1, LANES)),
                     full((rows_t, 1)), full((rows_t, 1))]
        operands += list(beta_decay[:-1])
        out_specs += [pl.BlockSpec((PROJ_TM, LANES), lambda i, j: (i, 0)),
                      pl.BlockSpec((rows_t, PROJ_TM), lambda i, j: (0, i))]
        out_shape += [jax.ShapeDtypeStruct((t, LANES), F32), jax.ShapeDtypeStruct((rows_t, t), F32)]
    out = pl.pallas_call(
        functools.partial(_proj_kernel, mode=mode, n_heads=n_heads),
        grid=(t // PROJ_TM, n // PROJ_TN),
        in_specs=in_specs,
        out_specs=out_specs,
        out_shape=out_shape,
        scratch_shapes=[pltpu.VMEM((PROJ_TM, d), BF16)],
        compiler_params=_cparams(("parallel", "arbitrary")),
        name="input_projection_" + mode,
    )(*operands)
    return out[0] if beta_decay is None else out


def _gdn_kernel(q_ref, k_ref, v_ref, z_ref, sm_ref, smt_ref, cwq_ref, cwk_ref, cwv_ref, gout_ref,
                o_ref, state_ref, qp_ref, kp_ref, vp_ref, vn_ref, *, n_heads):
    hg = pl.program_id(1)
    s = pl.program_id(2)
    tb = GDN_TB
    pad = SUBLANES
    width = GDN_HG * HEAD_DIM

    @pl.when(s == 0)
    def _():
        state_ref[...] = jnp.zeros_like(state_ref)
        for p_ref in (qp_ref, kp_ref, vp_ref):
            p_ref[0:pad, :] = jnp.zeros((pad, width), F32)

    r = lax.broadcasted_iota(jnp.int32, (tb, tb), 0)
    c = lax.broadcasted_iota(jnp.int32, (tb, tb), 1)
    delay_mat = jnp.concatenate([(r - c == dd).astype(BF16) for dd in range(1, CONV_WIDTH)], axis=0)

    def conv_silu(x_ref, p_ref, cw_ref):
        x = x_ref[...]
        xf = x.astype(F32)
        p_ref[pad:2 * pad, :] = xf[0:pad]
        delayed = jnp.dot(delay_mat, x, preferred_element_type=F32)
        acc = cw_ref[CONV_WIDTH - 1:CONV_WIDTH, :] * xf
        for dd in range(1, CONV_WIDTH):
            first = p_ref[pad - dd:2 * pad - dd, :]
            xd = jnp.concatenate([first, delayed[(dd - 1) * tb + pad:dd * tb]], axis=0)
            acc = acc + cw_ref[CONV_WIDTH - 1 - dd:CONV_WIDTH - dd, :] * xd
        p_ref[0:pad, :] = xf[tb - pad:tb]
        return acc * _sigmoid(acc)

    q_all = conv_silu(q_ref, qp_ref, cwq_ref)
    k_all = conv_silu(k_ref, kp_ref, cwk_ref)
    v_all = conv_silu(v_ref, vp_ref, cwv_ref)

    shift = int(math.log2(CHUNK))
    same = (r >> shift) == (c >> shift)
    incl = jnp.logical_and(same, c <= r)
    strict = jnp.logical_and(same, c < r)

    small = sm_ref[...]
    small_t = smt_ref[...]
    lane = lax.broadcasted_iota(jnp.int32, small.shape, 1)
    def split3(a):
        hi = a.astype(BF16)
        r1 = a - hi.astype(F32)
        mid = r1.astype(BF16)
        lo = (r1 - mid.astype(F32)).astype(BF16)
        return hi.astype(F32), mid.astype(F32), lo.astype(F32)

    part = 2 * n_heads
    s_hi, s_mid, s_lo = split3(small)
    small3 = jnp.where(lane < part, s_hi,
                       jnp.where(lane < 2 * part, pltpu.roll(s_mid, part, 1),
                                 jnp.where(lane < 3 * part, pltpu.roll(s_lo, 2 * part, 1), 0.0)))
    both = _bdot(jnp.concatenate([incl.astype(F32), same.astype(F32)], axis=0), small3)
    gcum = both[:tb]
    gtot = both[tb:]
    gcum_t = _bdot(jnp.concatenate(split3(small_t), axis=0),
                   jnp.logical_and(same, r <= c).astype(F32))
    sub3 = lax.broadcasted_iota(jnp.int32, gcum_t.shape, 0)

    heads = range(GDN_HG)
    hsl = [slice(hh * HEAD_DIM, (hh + 1) * HEAD_DIM) for hh in heads]
    qs = [q_all[:, hs] for hs in hsl]
    ks = [k_all[:, hs] for hs in hsl]
    vs = [v_all[:, hs] for hs in hsl]
    qs = [q * lax.rsqrt(jnp.sum(q * q, axis=-1, keepdims=True) + EPS) * (HEAD_DIM ** -0.5) for q in qs]
    ks = [k * lax.rsqrt(jnp.sum(k * k, axis=-1, keepdims=True) + EPS) for k in ks]

    def col_of(arr, idx):
        return jnp.sum(jnp.where(lane == idx, arr, 0.0), axis=-1, keepdims=True)

    def terms_of(pos, idx):
        return jnp.logical_or(pos == idx, jnp.logical_or(pos == idx + part, pos == idx + 2 * part))

    head_ids = [hg * GDN_HG + hh for hh in heads]
    betas = [col_of(small, hd) for hd in head_ids]
    gcs = [jnp.sum(jnp.where(terms_of(lane, hd + n_heads), gcum, 0.0), axis=-1, keepdims=True)
           for hd in head_ids]
    gls = [jnp.sum(jnp.where(terms_of(lane, hd + n_heads), gtot, 0.0), axis=-1, keepdims=True)
           for hd in head_ids]
    gc_rows = [jnp.sum(jnp.where(terms_of(sub3, hd + n_heads), gcum_t, 0.0), axis=0, keepdims=True)
               for hd in head_ids]

    decays = [jnp.where(incl, jnp.exp(jnp.minimum(gc - gr, 0.0)), 0.0) for gc, gr in zip(gcs, gc_rows)]
    kbs = [k * b for k, b in zip(ks, betas)]
    kks = [_bdot_nt(kb, k) for kb, k in zip(kbs, ks)]
    pws = [jnp.where(strict, -(kk * dc), 0.0) for kk, dc in zip(kks, decays)]
    n_chunks = tb // CHUNK
    cat_row = lax.broadcasted_iota(jnp.int32, (CHUNK, tb), 0)
    cat_lane = lax.broadcasted_iota(jnp.int32, (CHUNK, tb), 1)
    lane_chunk = cat_lane >> shift

    def block_diag(m_cat):
        return jnp.concatenate([jnp.where(lane_chunk == ci, m_cat, 0.0) for ci in range(n_chunks)], axis=0)

    def cat_of(m_bd):
        out = m_bd[0:CHUNK]
        for ci in range(1, n_chunks):
            out = out + m_bd[ci * CHUNK:(ci + 1) * CHUNK]
        return out

    pcats = [cat_of(pw) for pw in pws]
    eye_cat = ((cat_lane & (CHUNK - 1)) == cat_row).astype(F32)
    tcats = [eye_cat + pc for pc in pcats]
    pcats = [_bdot(pc, block_diag(pc)) for pc in pcats]
    n_levels = int(math.log2(CHUNK))
    for lev in range(1, n_levels):
        bds = [block_diag(pc) for pc in pcats]
        if lev < n_levels - 1:
            prods = [_bdot(jnp.concatenate([pc, tc], axis=0), bd) for pc, tc, bd in zip(pcats, tcats, bds)]
            pcats = [pr[:CHUNK] for pr in prods]
            tcats = [tc + pr[CHUNK:] for tc, pr in zip(tcats, prods)]
        else:
            tcats = [tc + _bdot(tc, bd) for tc, bd in zip(tcats, bds)]
    tmats = [block_diag(tc) for tc in tcats]
    egcs = [jnp.exp(gc) for gc in gcs]
    uws = [_bdot(tm, jnp.concatenate([v * b, kb * eg], axis=1))
           for tm, v, b, kb, eg in zip(tmats, vs, betas, kbs, egcs)]
    us = [uw[:, :HEAD_DIM] for uw in uws]
    ws = [uw[:, HEAD_DIM:] for uw in uws]
    qkm = [_bdot_nt(q, k) for q, k in zip(qs, ks)]
    qkm = [jnp.where(incl, x * dc, 0.0) for x, dc in zip(qkm, decays)]
    q_decs = [q * eg for q, eg in zip(qs, egcs)]
    k_ends = [k * jnp.exp(gl - gc) for k, gl, gc in zip(ks, gls, gcs)]

    for hh in heads:
        vn_ref[hh] = jnp.zeros((tb, HEAD_DIM), F32)
    outs = [[] for _ in heads]
    for ci in range(tb // CHUNK):
        cs = slice(ci * CHUNK, (ci + 1) * CHUNK)
        sts = [state_ref[hh] for hh in heads]
        ws_qs = [_bdot(jnp.concatenate([ws[hh][cs], q_decs[hh][cs]], axis=0), sts[hh]) for hh in heads]
        v_news = [us[hh][cs] - ws_qs[hh][:CHUNK] for hh in heads]
        for hh in heads:
            vn_ref[hh, cs, :] = v_news[hh]
        intra = [_bdot(qkm[hh][cs], vn_ref[hh]) for hh in heads]
        upd = [_bdot_tn(k_ends[hh][cs], v_news[hh]) for hh in heads]
        for hh in heads:
            outs[hh].append(ws_qs[hh][CHUNK:] + intra[hh])
            g_last = gls[hh][ci * CHUNK:ci * CHUNK + 1, :]
            state_ref[hh] = sts[hh] * jnp.exp(g_last) + upd[hh]
    for hh in heads:
        o = jnp.concatenate(outs[hh], axis=0)
        o = o * lax.rsqrt(jnp.mean(o * o, axis=-1, keepdims=True) + EPS) * gout_ref[...]
        zz = z_ref[:, hsl[hh]].astype(F32)
        o_ref[:, hsl[hh]] = (o * (zz * _sigmoid(zz))).astype(o_ref.dtype)


def _gated_delta(big, small, small_t, conv_w, g_out, bsz, seq, n_heads, d_model):
    t = bsz * seq
    tb = GDN_TB
    ns = seq // tb
    width = GDN_HG * HEAD_DIM
    nhg = n_heads // GDN_HG
    blocks_per_group = d_model // width
    rows_t = small_t.shape[0]

    def colspec(group):
        return pl.BlockSpec((tb, width), lambda b, h, s: (b * ns + s, group * blocks_per_group + h))

    def cwspec(group):
        return pl.BlockSpec((CONV_WIDTH, width), lambda b, h, s: (0, group * blocks_per_group + h))

    return pl.pallas_call(
        functools.partial(_gdn_kernel, n_heads=n_heads),
        grid=(bsz, nhg, ns),
        in_specs=[
            colspec(0), colspec(1), colspec(2), colspec(3),
            pl.BlockSpec((tb, LANES), lambda b, h, s: (b * ns + s, 0)),
            pl.BlockSpec((rows_t, tb), lambda b, h, s: (0, b * ns + s)),
            cwspec(0), cwspec(1), cwspec(2),
            pl.BlockSpec((1, HEAD_DIM), lambda b, h, s: (0, 0)),
        ],
        out_specs=pl.BlockSpec((tb, width), lambda b, h, s: (b * ns + s, h)),
        out_shape=jax.ShapeDtypeStruct((t, d_model), BF16),
        scratch_shapes=[
            pltpu.VMEM((GDN_HG, HEAD_DIM, HEAD_DIM), F32),
            pltpu.VMEM((2 * SUBLANES, width), F32),
            pltpu.VMEM((2 * SUBLANES, width), F32),
            pltpu.VMEM((2 * SUBLANES, width), F32),
            pltpu.VMEM((GDN_HG, tb, HEAD_DIM), F32),
        ],
        compiler_params=_cparams(("parallel", "parallel", "arbitrary")),
        name="gated_delta",
    )(big, big, big, big, small, small_t, conv_w, conv_w, conv_w, g_out)


def _t5_bucket(n):
    max_exact = N_BUCKETS // 2
    nf = jnp.maximum(n, 1).astype(F32)
    large = max_exact + (jnp.log(nf / max_exact) / math.log(MAX_DISTANCE / max_exact)
                         * (N_BUCKETS - max_exact)).astype(jnp.int32)
    large = jnp.minimum(large, N_BUCKETS - 1)
    return jnp.where(n < max_exact, n, large)


def _attn_kernel(rb_ref, q_ref, k_ref, v_ref, lam_ref, gsub_ref, o_ref,
                 bias_ref, m_ref, acc_ref, sa_ref, sb_ref, qs_ref, *, lam_init):
    hg = pl.program_id(0)
    b = pl.program_id(1)
    bq, bk = ATT_BQ, ATT_BK
    heads = range(ATT_HG)
    hsl = [slice(hh * HEAD_DIM, (hh + 1) * HEAD_DIM) for hh in heads]

    @pl.when(b == 0)
    def _():
        blk = LANES
        i = lax.broadcasted_iota(jnp.int32, (blk, blk), 0)
        jj = lax.broadcasted_iota(jnp.int32, (blk, blk), 1)
        for hh in heads:
            head = hg * ATT_HG + hh
            far = rb_ref[N_BUCKETS - 1, head]

            def toeplitz(offset):
                bucket = _t5_bucket(jnp.maximum(i - jj + offset, 0))
                out = jnp.zeros((blk, blk), F32)
                for cc in range(N_BUCKETS):
                    out = jnp.where(bucket == cc, rb_ref[cc, head] - far, out)
                return out

            on_diag = jnp.where(i >= jj, toeplitz(0), NEG_BIG)
            next_diag = toeplitz(blk)
            kinds = {0: on_diag, 1: next_diag}
            bias_ref[hh, 2] = jnp.zeros((bq, bk), F32)
            for slot in range(2):
                for rr in range(bq // blk):
                    for cc in range(bk // blk):
                        delta = rr - cc + slot * (bk // blk)
                        if delta < 0:
                            tile = jnp.full((blk, blk), NEG_BIG, F32)
                        else:
                            tile = kinds.get(delta, jnp.zeros((blk, blk), F32))
                        bias_ref[hh, slot, rr * blk:(rr + 1) * blk, cc * blk:(cc + 1) * blk] = tile

    lane = lax.broadcasted_iota(jnp.int32, (bq, HEAD_DIM), 1)
    ones_col = (lax.broadcasted_iota(jnp.int32, (bk, HEAD_DIM), 1) == 0).astype(BF16)
    lam_p = lam_ref[...]
    s1 = jnp.sum(lam_p[0:1] * lam_p[1:2], axis=-1, keepdims=True)
    s2 = jnp.sum(lam_p[2:3] * lam_p[3:4], axis=-1, keepdims=True)
    lam = jnp.exp(s1) - jnp.exp(s2) + lam_init

    n_q = q_ref.shape[0] // bq

    def stack_q(qb):
        rows = pl.ds(pl.multiple_of(qb * bq, bq), bq)
        for hh in heads:
            q = q_ref[rows, hsl[hh]]
            zero = jnp.zeros_like(q)
            qs_ref[hh, 0:bq, :] = jnp.where(lane < DH_DIFF, q, zero)
            qs_ref[hh, bq:2 * bq, :] = jnp.where(lane < DH_DIFF, zero, q)

    def scores(j, s_ref):
        ks = pl.multiple_of(j * bk, bk)
        for hh in heads:
            s_ref[hh] = lax.dot_general(qs_ref[hh], k_ref[pl.ds(ks, bk), hsl[hh]],
                                        (((1,), (1,)), ((), ())), preferred_element_type=F32)

    stack_q(0)
    scores(0, sa_ref)

    def query_block(qi, outer):
        rows = pl.ds(pl.multiple_of(qi * bq, bq), bq)
        m_ref[...] = jnp.full(m_ref.shape, NEG_BIG, F32)
        acc_ref[...] = jnp.zeros(acc_ref.shape, F32)

        def absorb(j, s_ref, biased=True):
            ks = pl.multiple_of(j * bk, bk)
            v_exts = [jnp.concatenate([v_ref[pl.ds(ks, bk), hs], ones_col], axis=1) for hs in hsl]
            if biased:
                slot = jnp.minimum(qi - j, 2)
                scs = [jnp.concatenate([s_ref[hh, 0:bq, :] + bias_ref[hh, slot],
                                        s_ref[hh, bq:2 * bq, :] + bias_ref[hh, slot]], axis=0)
                       for hh in heads]
            else:
                scs = [s_ref[hh] for hh in heads]
            m_olds = [m_ref[hh] for hh in heads]
            m_news = [jnp.maximum(mo, jnp.max(sc, axis=-1, keepdims=True)) for mo, sc in zip(m_olds, scs)]
            ps = [jnp.exp(sc - mn) for sc, mn in zip(scs, m_news)]
            pvs = [jnp.dot(p.astype(BF16), ve, preferred_element_type=F32) for p, ve in zip(ps, v_exts)]
            for hh in heads:
                acc_ref[hh] = jnp.exp(m_olds[hh] - m_news[hh]) * acc_ref[hh] + pvs[hh]
                m_ref[hh] = m_news[hh]

        n_tiles = qi + 1

        def pair_body(jj, carry, biased):
            j0 = 2 * jj
            scores(j0 + 1, sb_ref)
            absorb(j0, sa_ref, biased)
            scores(jnp.minimum(j0 + 2, qi), sa_ref)
            absorb(j0 + 1, sb_ref, biased)
            return carry

        n_far_pairs = jnp.maximum(qi - 1, 0) // 2
        lax.fori_loop(0, n_far_pairs, functools.partial(pair_body, biased=False), 0)
        lax.fori_loop(n_far_pairs, n_tiles // 2, functools.partial(pair_body, biased=True), 0)

        @pl.when(n_tiles % 2 == 1)
        def _():
            absorb(qi, sa_ref)

        stack_q(jnp.minimum(qi + 1, n_q - 1))
        scores(0, sa_ref)
        for hh in heads:
            acc = acc_ref[hh]
            num = acc[:, :HEAD_DIM]
            den = acc[:, HEAD_DIM:HEAD_DIM + 1]
            o = num[:bq] / den[:bq] - lam * (num[bq:] / den[bq:])
            o = o * lax.rsqrt(jnp.mean(o * o, axis=-1, keepdims=True) + EPS) * gsub_ref[...]
            o_ref[rows, hsl[hh]] = (o * (1.0 - lam_init)).astype(o_ref.dtype)
        return outer

    lax.fori_loop(0, n_q, query_block, 0)


def _diff_attention(proj_qk, proj_plain, rel_bias, lam_params, g_subln, bsz, seq, n_heads, d_model,
                    lam_init):
    t = bsz * seq
    assert ATT_BQ == ATT_BK and MAX_DISTANCE <= LANES and n_heads % ATT_HG == 0 and seq % ATT_BQ == 0
    width = ATT_HG * HEAD_DIM
    per = d_model // width
    vcol = 4 * per
    return pl.pallas_call(
        functools.partial(_attn_kernel, lam_init=lam_init),
        grid=(n_heads // ATT_HG, bsz),
        in_specs=[
            pl.BlockSpec(memory_space=pltpu.SMEM),
            pl.BlockSpec((seq, width), lambda h, b: (b, h)),
            pl.BlockSpec((seq, width), lambda h, b: (b, per + h)),
            pl.BlockSpec((seq, width), lambda h, b: (b, vcol + h)),
            pl.BlockSpec((4, DH_DIFF), lambda h, b: (0, 0)),
            pl.BlockSpec((1, HEAD_DIM), lambda h, b: (0, 0)),
        ],
        out_specs=pl.BlockSpec((seq, width), lambda h, b: (b, h)),
        out_shape=jax.ShapeDtypeStruct((t, d_model), BF16),
        scratch_shapes=[
            pltpu.VMEM((ATT_HG, 3, ATT_BQ, ATT_BK), F32),
            pltpu.VMEM((ATT_HG, 2 * ATT_BQ, 1), F32),
            pltpu.VMEM((ATT_HG, 2 * ATT_BQ, 2 * HEAD_DIM), F32),
            pltpu.VMEM((ATT_HG, 2 * ATT_BQ, ATT_BK), F32),
            pltpu.VMEM((ATT_HG, 2 * ATT_BQ, ATT_BK), F32),
            pltpu.VMEM((ATT_HG, 2 * ATT_BQ, HEAD_DIM), BF16),
        ],
        compiler_params=_cparams(("arbitrary", "arbitrary")),
        name="diff_attention",
    )(rel_bias, proj_qk, proj_qk, proj_plain, lam_params, g_subln)


def _mix_kernel(ga_ref, gb_ref, oa_ref, od_ref, x_ref, wo_ref, gffn_ref, wr_ref, br_ref,
                x1_ref, h2_ref, topi_ref, topw_ref, rank_ref, cnt_ref, carry_ref):
    i = pl.program_id(0)
    tm = MIX_TM

    @pl.when(i == 0)
    def _():
        carry_ref[...] = jnp.zeros_like(carry_ref)

    tp = tm // MIX_PARTS
    parts = range(MIX_PARTS)
    rows = [slice(pp * tp, (pp + 1) * tp) for pp in parts]
    mixes = [ga_ref[rs, :] * oa_ref[rs, :] + gb_ref[rs, :] * od_ref[rs, :] for rs in rows]
    x1s = [x_ref[rs, :] + jnp.dot(mx, wo_ref[...], preferred_element_type=F32) for rs, mx in zip(rows, mixes)]
    for rs, x1 in zip(rows, x1s):
        x1_ref[rs, :] = x1
    h2s = [x1 * lax.rsqrt(jnp.mean(x1 * x1, axis=-1, keepdims=True) + EPS) * gffn_ref[...] for x1 in x1s]
    for rs, h2 in zip(rows, h2s):
        h2_ref[rs, :] = _pack_halves(h2)

    curs = [lax.dot_general(wr_ref[...], h2, (((1,), (1,)), ((), ())), preferred_element_type=F32,
                            precision=lax.Precision.HIGHEST) + br_ref[...] for h2 in h2s]
    eidx = lax.broadcasted_iota(jnp.int32, curs[0].shape, 0).astype(F32)
    vals = [[] for _ in parts]
    hots = [[] for _ in parts]
    for kk in range(TOP_K):
        mxs = [jnp.max(cur, axis=0, keepdims=True) for cur in curs]
        idxs = [jnp.min(jnp.where(cur == mx, eidx, float(N_EXPERTS)), axis=0, keepdims=True)
                for cur, mx in zip(curs, mxs)]
        for pp in parts:
            hot = eidx == idxs[pp]
            vals[pp].append(mxs[pp])
            hots[pp].append(hot)
            topi_ref[kk:kk + 1, rows[pp]] = idxs[pp].astype(jnp.int32)
            curs[pp] = jnp.where(hot, -jnp.inf, curs[pp])
    for pp in parts:
        exps = [jnp.exp(vv - vals[pp][0]) for vv in vals[pp]]
        denom = exps[0] + exps[1] + exps[2] + exps[3]
        for kk in range(TOP_K):
            topw_ref[kk:kk + 1, rows[pp]] = exps[kk] / denom

    r = lax.broadcasted_iota(jnp.int32, (tp, tp), 0)
    c = lax.broadcasted_iota(jnp.int32, (tp, tp), 1)
    earlier = (r < c).astype(F32)
    sel_fs = []
    for pp in parts:
        sel = hots[pp][0]
        for kk in range(1, TOP_K):
            sel = jnp.logical_or(sel, hots[pp][kk])
        sel_fs.append(sel.astype(F32))
    within = [_bdot(sf, earlier) for sf in sel_fs]
    totals = [jnp.sum(sf, axis=-1, keepdims=True) for sf in sel_fs]
    run = carry_ref[...]
    for pp in parts:
        before = within[pp] + run
        for kk in range(TOP_K):
            rank_ref[kk:kk + 1, rows[pp]] = jnp.sum(jnp.where(hots[pp][kk], before, 0.0), axis=0,
                                                    keepdims=True).astype(jnp.int32)
        run = run + totals[pp]
    carry_ref[...] = run
    cnt_ref[...] = run.astype(jnp.int32)


def _mix_project_route(proj_gate, oa, od, x2d, w_o, g_ffn, w_r_t, b_r, d_model):
    t = x2d.shape[0]
    tm = MIX_TM
    full = lambda shape: pl.BlockSpec(shape, lambda i: (0, 0))
    row = lambda: pl.BlockSpec((tm, d_model), lambda i: (i, 0))
    krow = lambda: pl.BlockSpec((TOP_K, tm), lambda i: (0, i))
    return pl.pallas_call(
        _mix_kernel,
        grid=(t // tm,),
        in_specs=[
            pl.BlockSpec((tm, d_model), lambda i: (i, 0)),
            pl.BlockSpec((tm, d_model), lambda i: (i, 1)),
            row(), row(), row(),
            full((d_model, d_model)), full((1, d_model)), full((N_EXPERTS, d_model)), full((N_EXPERTS, 1)),
        ],
        out_specs=[row(), pl.BlockSpec((tm, d_model // 2), lambda i: (i, 0)),
                   krow(), krow(), krow(), full((N_EXPERTS, 1))],
        out_shape=[
            jax.ShapeDtypeStruct((t, d_model), F32),
            jax.ShapeDtypeStruct((t, d_model // 2), jnp.int32),
            jax.ShapeDtypeStruct((TOP_K, t), jnp.int32),
            jax.ShapeDtypeStruct((TOP_K, t), F32),
            jax.ShapeDtypeStruct((TOP_K, t), jnp.int32),
            jax.ShapeDtypeStruct((N_EXPERTS, 1), jnp.int32),
        ],
        scratch_shapes=[pltpu.VMEM((N_EXPERTS, 1), F32)],
        compiler_params=_cparams(("arbitrary",)),
        name="merge_outproj_route",
    )(proj_gate, proj_gate, oa, od, x2d, w_o, g_ffn, w_r_t, b_r)


def _pack_halves(x):
    half = x.shape[1] // 2
    bits = pltpu.bitcast(x.astype(BF16).astype(F32), jnp.int32)
    return bits[:, :half] | lax.shift_right_logical(bits[:, half:], 16)


def _unpack_halves(p):
    hi = pltpu.bitcast(p & jnp.int32(-65536), F32)
    lo = pltpu.bitcast(lax.shift_left(p, 16), F32)
    return jnp.concatenate([hi, lo], axis=1)


def _expert_kernel(be_ref, nu_ref, x_ref, wup_ref, bup_ref, wdn_ref, bdn_ref, *rest):
    y_ref, wup_bf, wdn_bf = rest[-3:]
    i = pl.program_id(0)
    d_ff = wdn_ref.shape[1]

    @pl.when(jnp.logical_or(i == 0, be_ref[i] != be_ref[jnp.maximum(i - 1, 0)]))
    def _():
        rr = lax.broadcasted_iota(jnp.int32, (2 * LANES, 2 * LANES), 0)
        cc = lax.broadcasted_iota(jnp.int32, (2 * LANES, 2 * LANES), 1)
        pick = jnp.where(cc < LANES, 2 * cc, 2 * (cc - LANES) + 1)
        perm = (rr == pick).astype(BF16)
        for g in range(wup_ref.shape[2] // (2 * LANES)):
            cs = slice(g * 2 * LANES, (g + 1) * 2 * LANES)
            wup_bf[:, cs] = jnp.dot(wup_ref[0, :, cs].astype(BF16), perm,
                                    preferred_element_type=F32).astype(BF16)
        wdn_bf[...] = wdn_ref[0].astype(BF16)

    @pl.when(i < nu_ref[0])
    def _():
        x = _unpack_halves(x_ref[...])
        hid = jnp.dot(x.astype(BF16), wup_bf[...], preferred_element_type=F32) + bup_ref[0]
        acts = []
        for g in range(hid.shape[1] // (2 * LANES)):
            glu = jnp.minimum(hid[:, g * 2 * LANES:g * 2 * LANES + LANES], SWIGLU_LIMIT)
            lin = jnp.clip(hid[:, g * 2 * LANES + LANES:(g + 1) * 2 * LANES], -SWIGLU_LIMIT, SWIGLU_LIMIT)
            acts.append(glu * _sigmoid(SWIGLU_ALPHA * glu) * (lin + 1.0))
        act = jnp.concatenate(acts, axis=1)
        assert act.shape[1] == d_ff
        y = jnp.dot(act.astype(BF16), wdn_bf[...], preferred_element_type=F32) + bdn_ref[0]
        y_ref[...] = _pack_halves(y)

    @pl.when(i >= nu_ref[0])
    def _():
        y_ref[...] = jnp.zeros(y_ref.shape, y_ref.dtype)


def _experts(block_e, n_used, xs, y_prev, first_block, n_rows_total, w_up, b_up, w_down, b_down):
    n_rows, half = xs.shape
    d = w_up.shape[1]
    nb = n_rows // MOE_RB
    two_ff = w_up.shape[2]
    d_ff = w_down.shape[1]
    in_specs = [
        pl.BlockSpec((MOE_RB, half), lambda i, be, nu: (jnp.maximum(jnp.minimum(i, nu[0] - 1), 0), 0)),
        pl.BlockSpec((1, d, two_ff), lambda i, be, nu: (be[i], 0, 0)),
        pl.BlockSpec((1, 1, two_ff), lambda i, be, nu: (be[i], 0, 0)),
        pl.BlockSpec((1, d_ff, d), lambda i, be, nu: (be[i], 0, 0)),
        pl.BlockSpec((1, 1, d), lambda i, be, nu: (be[i], 0, 0)),
    ]
    operands = [block_e, n_used, xs, w_up, b_up, w_down, b_down]
    aliases = {}
    if y_prev is not None:
        in_specs.append(pl.BlockSpec(memory_space=pl.ANY))
        aliases = {len(operands): 0}
        operands.append(y_prev)
    grid_spec = pltpu.PrefetchScalarGridSpec(
        num_scalar_prefetch=2,
        grid=(nb,),
        in_specs=in_specs,
        out_specs=pl.BlockSpec((MOE_RB, half), lambda i, be, nu: (first_block + i, 0)),
        scratch_shapes=[pltpu.VMEM((d, two_ff), BF16), pltpu.VMEM((d_ff, d), BF16)],
    )
    return pl.pallas_call(
        _expert_kernel,
        grid_spec=grid_spec,
        out_shape=jax.ShapeDtypeStruct((n_rows_total, half), jnp.int32),
        input_output_aliases=aliases,
        compiler_params=_cparams(("arbitrary",)),
        name="moe_experts",
    )(*operands)


def _sc_invert_slots(dest_flat, n_rows):
    n_assign = dest_flat.shape[0]
    n_workers = SC_CORES * SC_SUBCORES
    rows_per_w = n_rows // n_workers
    chunk = SC_SCAN_CHUNK
    assert n_rows % n_workers == 0 and rows_per_w % SC_LANES == 0 and n_assign % chunk == 0
    mesh = plsc.VectorSubcoreMesh(core_axis_name="c", subcore_axis_name="s",
                                  num_cores=SC_CORES, num_subcores=SC_SUBCORES)

    def body(dest_hbm, out_hbm, dest_v, map_v):
        wid = lax.axis_index("s") * SC_CORES + lax.axis_index("c")
        base = wid * rows_per_w
        lanes = lax.broadcasted_iota(jnp.int32, (SC_LANES,), 0)

        @pl.loop(0, rows_per_w, step=SC_LANES)
        def _(r0):
            map_v[pl.ds(r0, SC_LANES)] = jnp.full((SC_LANES,), -1, jnp.int32)

        @pl.loop(0, n_assign // chunk)
        def _(ci):
            pltpu.sync_copy(dest_hbm.at[pl.ds(ci * chunk, chunk)], dest_v)

            @pl.loop(0, chunk, step=SC_LANES)
            def _(j):
                local = dest_v[pl.ds(j, SC_LANES)] - base
                mine = jnp.logical_and(local >= 0, local < rows_per_w)
                plsc.store_scatter(map_v, [jnp.where(mine, local, 0)], ci * chunk + j + lanes, mask=mine)

        pltpu.sync_copy(map_v, out_hbm.at[pl.ds(base, rows_per_w)])

    return pl.kernel(
        body,
        out_type=jax.ShapeDtypeStruct((n_rows,), jnp.int32),
        mesh=mesh,
        scratch_types=[pltpu.VMEM((chunk,), jnp.int32), pltpu.VMEM((rows_per_w,), jnp.int32)],
        compiler_params=pltpu.CompilerParams(needs_layout_passes=False),
        name="moe_slot_inverse",
    )(dest_flat)


def _sc_gather_rows(table, idx):
    n_idx = idx.shape[0]
    d = table.shape[1]
    n_workers = SC_CORES * SC_SUBCORES
    per_worker = n_idx // n_workers
    n_chunks = per_worker // SC_GATHER_ROWS
    assert n_idx % n_workers == 0 and per_worker % SC_GATHER_ROWS == 0
    mesh = plsc.VectorSubcoreMesh(core_axis_name="c", subcore_axis_name="s",
                                  num_cores=SC_CORES, num_subcores=SC_SUBCORES)

    assert n_chunks % 2 == 0

    def body(table_hbm, idx_hbm, out_hbm, idx_v, rows_a, rows_b, sem_a, sem_b):
        wid = lax.axis_index("s") * SC_CORES + lax.axis_index("c")
        base = wid * per_worker
        pltpu.sync_copy(idx_hbm.at[pl.ds(base, per_worker)], idx_v)

        def gather(ci, rows_v, sem):
            off = pl.multiple_of(ci * SC_GATHER_ROWS, SC_GATHER_ROWS)
            return pltpu.make_async_copy(table_hbm.at[idx_v.at[pl.ds(off, SC_GATHER_ROWS)]], rows_v, sem)

        def put(ci, rows_v):
            off = pl.multiple_of(ci * SC_GATHER_ROWS, SC_GATHER_ROWS)
            pltpu.sync_copy(rows_v, out_hbm.at[pl.ds(base + off, SC_GATHER_ROWS)])

        gather(0, rows_a, sem_a).start()

        @pl.loop(0, n_chunks, step=2)
        def _(ci):
            gather(ci + 1, rows_b, sem_b).start()
            gather(ci, rows_a, sem_a).wait()
            put(ci, rows_a)
            nxt = jnp.minimum(ci + 2, n_chunks - 1)
            gather(nxt, rows_a, sem_a).start()
            gather(ci + 1, rows_b, sem_b).wait()
            put(ci + 1, rows_b)

        gather(n_chunks - 1, rows_a, sem_a).wait()

    return pl.kernel(
        body,
        out_type=jax.ShapeDtypeStruct((n_idx, d), table.dtype),
        mesh=mesh,
        scratch_types=[
            pltpu.VMEM((per_worker,), jnp.int32),
            pltpu.VMEM((SC_GATHER_ROWS, d), table.dtype),
            pltpu.VMEM((SC_GATHER_ROWS, d), table.dtype),
            pltpu.SemaphoreType.DMA,
            pltpu.SemaphoreType.DMA,
        ],
        name="moe_slot_gather",
    )(table, idx)


def _combine_kernel(x1_ref, w_ref, y0_ref, y1_ref, y2_ref, y3_ref, o_ref):
    w = w_ref[...]
    out = x1_ref[...]
    for kk, y_ref in enumerate((y0_ref, y1_ref, y2_ref, y3_ref)):
        out = out + w[:, kk:kk + 1] * _unpack_halves(y_ref[...])
    o_ref[...] = out


def _combine(x1, w_tok, y_slots):
    t, d = x1.shape
    tc = COMB_TC
    nt = t // tc
    yspec = lambda kk: pl.BlockSpec((tc, d // 2), lambda i: (kk * nt + i, 0))
    return pl.pallas_call(
        _combine_kernel,
        grid=(nt,),
        in_specs=[
            pl.BlockSpec((tc, d), lambda i: (i, 0)),
            pl.BlockSpec((tc, TOP_K), lambda i: (i, 0)),
            yspec(0), yspec(1), yspec(2), yspec(3),
        ],
        out_specs=pl.BlockSpec((tc, d), lambda i: (i, 0)),
        out_shape=jax.ShapeDtypeStruct((t, d), F32),
        compiler_params=_cparams(("parallel",)),
        name="moe_combine",
    )(x1, w_tok, y_slots, y_slots, y_slots, y_slots)


def _moe(x1, h2, topi, topw, rank, counts, w_up, b_up, w_down, b_down):
    t, d = x1.shape
    n_assign = t * TOP_K
    nb = -(-n_assign // MOE_RB) + N_EXPERTS
    n_rows = nb * MOE_RB
    counts = counts[:, 0]
    padded = (counts + MOE_RB - 1) // MOE_RB * MOE_RB
    padded_end = jnp.cumsum(padded)
    padded_start = padded_end - padded
    expert_ids = jnp.arange(N_EXPERTS, dtype=jnp.int32)[:, None, None]
    start_of = jnp.sum(jnp.where(topi[None] == expert_ids, padded_start[:, None, None], 0), axis=0)
    dest = (start_of + rank).astype(jnp.int32)
    n_used = (padded_end[-1] // MOE_RB).astype(jnp.int32)
    blk = jnp.minimum(jnp.arange(nb, dtype=jnp.int32), n_used - 1)
    block_e = jnp.minimum(jnp.sum(padded_end[None, :] <= (blk * MOE_RB)[:, None], axis=1),
                          N_EXPERTS - 1).astype(jnp.int32)
    slot_of = _sc_invert_slots(dest.reshape(-1), n_rows)
    src_tok = jnp.where(slot_of < 0, jnp.arange(n_rows, dtype=jnp.int32), slot_of) % t

    nb_a = nb // 2
    y_rows = None
    for first, n_blk in ((0, nb_a), (nb_a, nb - nb_a)):
        xs = _sc_gather_rows(h2, lax.slice(src_tok, (first * MOE_RB,), ((first + n_blk) * MOE_RB,)))
        used = jnp.clip(n_used - first, 0, n_blk).reshape(1)
        y_rows = _experts(lax.slice(block_e, (first,), (first + n_blk,)), used, xs, y_rows, first, n_rows,
                          w_up, b_up, w_down, b_down)
    y_slots = _sc_gather_rows(y_rows, dest.reshape(-1))
    return _combine(x1, topw.T, y_slots)


def kernel(x, g_mix, w_in, b_gate, conv_w, a_log, dt_bias, g_delta_out, q_norm, k_norm, lambda_q1, lambda_k1, lambda_q2, lambda_k2, g_subln, rel_bias, w_o, g_ffn, w_router, b_router, w_up, b_up, w_down, b_down):
    bsz, seq, d = x.shape
    depth = g_mix.shape[0]
    n_heads = d // HEAD_DIM
    t = bsz * seq
    d_ff = w_down.shape[2]
    assert d % PROJ_TN == 0 and t % PROJ_TM == 0 and seq % GDN_TB == 0 and seq % ATT_BQ == 0
    assert t % MIX_TM == 0 and t % COMB_TC == 0 and n_heads % GDN_HG == 0
    assert (t * TOP_K) % MOE_RB == 0
    assert 2 * n_heads <= 2 * SUBLANES

    x2d = x.reshape(t, d)
    for l in range(depth):
        wl = w_in[l]
        c0 = 4 * d
        c1 = c0 + 2 * n_heads
        c2 = c1 + 2 * d
        c3 = c2 + d
        w_small = jnp.pad(wl[:, c0:c1], ((0, 0), (0, LANES - 2 * n_heads)))
        gm = g_mix[l].reshape(1, d)
        head_pad = jnp.zeros((LANES - 2 * n_heads,), F32)
        alog = jnp.concatenate([jnp.zeros((n_heads,), F32), a_log[l], head_pad])
        dtb = jnp.concatenate([jnp.zeros((n_heads,), F32), dt_bias[l], head_pad])
        rows_t = 2 * n_heads
        beta_decay = (w_small.astype(BF16), w_small[:, :rows_t].T.astype(BF16),
                      alog.reshape(1, LANES), dtb.reshape(1, LANES),
                      alog[:rows_t].reshape(rows_t, 1), dtb[:rows_t].reshape(rows_t, 1), n_heads)
        w_plain = jnp.concatenate([wl[:, :c0], wl[:, c2:c3]], axis=1).astype(BF16)
        proj_plain, small, small_t = _input_projection(x2d, gm, w_plain, jnp.zeros((1, 5 * d), F32), "plain",
                                                       beta_decay)
        qk_gain = jnp.concatenate([jnp.tile(q_norm[l] * (DH_DIFF ** -0.5), 2 * n_heads),
                                   jnp.tile(k_norm[l], 2 * n_heads)]).reshape(1, 2 * d)
        proj_qk = _input_projection(x2d, gm, wl[:, c1:c2].astype(BF16), qk_gain, "qknorm")
        proj_gate = _input_projection(x2d, gm, wl[:, c3:].astype(BF16), b_gate[l].reshape(1, 2 * d), "gate")

        oa = _gated_delta(proj_plain, small, small_t, conv_w[l], g_delta_out[l].reshape(1, HEAD_DIM),
                          bsz, seq, n_heads, d)

        lam_init = 0.8 - 0.6 * math.exp(-0.3 * l)
        lam_params = jnp.stack([lambda_q1[l], lambda_k1[l], lambda_q2[l], lambda_k2[l]])
        od = _diff_attention(proj_qk, proj_plain, rel_bias, lam_params, g_subln[l].reshape(1, HEAD_DIM),
                             bsz, seq, n_heads, d, lam_init)

        x1, h2, topi, topw, rank, counts = _mix_project_route(
            proj_gate, oa, od, x2d, w_o[l].astype(BF16), g_ffn[l].reshape(1, d),
            w_router[l].T, b_router[l].reshape(N_EXPERTS, 1), d)

        b_up_l = b_up[l].reshape(N_EXPERTS, 2 * d_ff // (2 * LANES), LANES, 2)
        b_up_l = jnp.swapaxes(b_up_l, 2, 3).reshape(N_EXPERTS, 1, 2 * d_ff)
        x2d = _moe(x1, h2, topi, topw, rank, counts, w_up[l], b_up_l,
                   w_down[l], b_down[l].reshape(N_EXPERTS, 1, d))
    return x2d.reshape(bsz, seq, d)
```

```python
import functools
import math

import jax
import jax.numpy as jnp
from jax import lax
from jax.experimental import pallas as pl
from jax.experimental.pallas import tpu as pltpu
from jax.experimental.pallas import tpu_sc as plsc

F32 = jnp.float32
BF16 = jnp.bfloat16

HEAD_DIM = 128
DH_DIFF = HEAD_DIM // 2
CONV_WIDTH = 4
CHUNK = 64
N_BUCKETS = 32
MAX_DISTANCE = 128
N_EXPERTS = 32
TOP_K = 4
SWIGLU_LIMIT = 7.0
SWIGLU_ALPHA = 1.702
EPS = 1e-6
NEG_BIG = -1e30

LANES = 128
SUBLANES = 8
VMEM_LIMIT = 56 * 1024 * 1024
SC_CORES = 2
SC_SUBCORES = 16
SC_LANES = 16
SC_GATHER_ROWS = 64
SC_SCAN_CHUNK = 4096

PROJ_TM = 2048
PROJ_TN = 1024
PROJ_CHUNK = 256
GDN_TB = 256
GDN_HG = 8
ATT_HG = 1
ATT_BQ = 512
ATT_BK = 512
MIX_TM = 1024
MIX_PARTS = 2
MOE_RB = 512
COMB_TC = 512


def _cparams(sem):
    return pltpu.CompilerParams(dimension_semantics=sem, vmem_limit_bytes=VMEM_LIMIT)


def _sigmoid(x):
    return 0.5 * jnp.tanh(0.5 * x) + 0.5


def _bdot(a, b):
    return jnp.dot(a.astype(BF16), b.astype(BF16), preferred_element_type=F32)


def _bdot_nt(a, b):
    return lax.dot_general(a.astype(BF16), b.astype(BF16), (((1,), (1,)), ((), ())),
                           preferred_element_type=F32)


def _bdot_tn(a, b):
    return lax.dot_general(a.astype(BF16), b.astype(BF16), (((0,), (0,)), ((), ())),
                           preferred_element_type=F32)


def _beta_decay(acc, idx, alog, dtb, n_heads):
    beta = _sigmoid(acc)
    z = acc + dtb
    softplus = jnp.maximum(z, 0.0) + jnp.log1p(jnp.exp(-jnp.abs(z)))
    gdec = -jnp.exp(alog) * softplus
    return jnp.where(idx < n_heads, beta, jnp.where(idx < 2 * n_heads, gdec, 0.0))


def _proj_kernel(x_ref, g_ref, w_ref, aux_ref, *rest, mode, n_heads):
    o_ref, h_ref = rest[-2:] if n_heads is None else (rest[6], rest[-1])

    @pl.when(pl.program_id(1) == 0)
    def _():
        x = x_ref[...]
        ms = jnp.mean(x * x, axis=-1, keepdims=True)
        h_ref[...] = (x * lax.rsqrt(ms + EPS) * g_ref[...]).astype(BF16)
        if n_heads is not None:
            ws_ref, wst_ref, alog_ref, dtb_ref, alog_t_ref, dtb_t_ref, _, os_ref, ost_ref, _ = rest
            hb = h_ref[...]
            acc = jnp.dot(hb, ws_ref[...], preferred_element_type=F32)
            lane = lax.broadcasted_iota(jnp.int32, acc.shape, 1)
            os_ref[...] = _beta_decay(acc, lane, alog_ref[...], dtb_ref[...], n_heads)
            acc_t = lax.dot_general(wst_ref[...], hb, (((1,), (1,)), ((), ())),
                                    preferred_element_type=F32)
            sub = lax.broadcasted_iota(jnp.int32, acc_t.shape, 0)
            ost_ref[...] = _beta_decay(acc_t, sub, alog_t_ref[...], dtb_t_ref[...], n_heads)

    h = h_ref[...]
    lo = lax.broadcasted_iota(jnp.int32, (1, LANES), 1) < DH_DIFF
    for c in range(PROJ_TN // PROJ_CHUNK):
        cs = slice(c * PROJ_CHUNK, (c + 1) * PROJ_CHUNK)
        acc = jnp.dot(h, w_ref[:, cs], preferred_element_type=F32)
        if mode == "plain":
            o_ref[:, cs] = acc.astype(o_ref.dtype)
        elif mode == "gate":
            o_ref[:, cs] = _sigmoid(acc + aux_ref[:, cs]).astype(o_ref.dtype)
        else:
            for g in range(PROJ_CHUNK // LANES):
                sl = slice(c * PROJ_CHUNK + g * LANES, c * PROJ_CHUNK + (g + 1) * LANES)
                y = acc[:, g * LANES:(g + 1) * LANES]
                y2 = y * y
                s_lo = jnp.sum(jnp.where(lo, y2, 0.0), axis=-1, keepdims=True)
                s_hi = jnp.sum(jnp.where(lo, 0.0, y2), axis=-1, keepdims=True)
                r = jnp.where(lo, lax.rsqrt(s_lo / DH_DIFF + EPS), lax.rsqrt(s_hi / DH_DIFF + EPS))
                o_ref[:, sl] = (y * r * aux_ref[:, sl]).astype(o_ref.dtype)


def _input_projection(x2d, g_mix, w, aux, mode, beta_decay=None):
    t, d = x2d.shape
    n = w.shape[1]
    full = lambda shape: pl.BlockSpec(shape, lambda i, j: (0, 0))
    in_specs = [
        pl.BlockSpec((PROJ_TM, d), lambda i, j: (i, 0)),
        full((1, d)),
        pl.BlockSpec((d, PROJ_TN), lambda i, j: (0, j)),
        pl.BlockSpec((1, PROJ_TN), lambda i, j: (0, j)),
    ]
    operands = [x2d, g_mix, w, aux]
    out_specs = [pl.BlockSpec((PROJ_TM, PROJ_TN), lambda i, j: (i, j))]
    out_shape = [jax.ShapeDtypeStruct((t, n), BF16)]
    n_heads = None
    if beta_decay is not None:
        n_heads = beta_decay[-1]
        rows_t = 2 * n_heads
        in_specs += [full((d, LANES)), full((rows_t, d)), full((1, LANES)), full((1, LANES)),
                     full((rows_t, 1)), full((rows_t, 1))]
        operands += list(beta_decay[:-1])
        out_specs += [pl.BlockSpec((PROJ_TM, LANES), lambda i, j: (i, 0)),
                      pl.BlockSpec((rows_t, PROJ_TM), lambda i, j: (0, i))]
        out_shape += [jax.ShapeDtypeStruct((t, LANES), F32), jax.ShapeDtypeStruct((rows_t, t), F32)]
    out = pl.pallas_call(
        functools.partial(_proj_kernel, mode=mode, n_heads=n_heads),
        grid=(t // PROJ_TM, n // PROJ_TN),
        in_specs=in_specs,
        out_specs=out_specs,
        out_shape=out_shape,
        scratch_shapes=[pltpu.VMEM((PROJ_TM, d), BF16)],
        compiler_params=_cparams(("parallel", "arbitrary")),
        name="input_projection_" + mode,
    )(*operands)
    return out[0] if beta_decay is None else out


def _gdn_kernel(q_ref, k_ref, v_ref, z_ref, sm_ref, smt_ref, cwq_ref, cwk_ref, cwv_ref, gout_ref,
                o_ref, state_ref, qp_ref, kp_ref, vp_ref, vn_ref, *, n_heads):
    hg = pl.program_id(1)
    s = pl.program_id(2)
    tb = GDN_TB
    pad = SUBLANES
    width = GDN_HG * HEAD_DIM

    @pl.when(s == 0)
    def _():
        state_ref[...] = jnp.zeros_like(state_ref)
        for p_ref in (qp_ref, kp_ref, vp_ref):
            p_ref[0:pad, :] = jnp.zeros((pad, width), F32)

    r = lax.broadcasted_iota(jnp.int32, (tb, tb), 0)
    c = lax.broadcasted_iota(jnp.int32, (tb, tb), 1)
    delay_mat = jnp.concatenate([(r - c == dd).astype(BF16) for dd in range(1, CONV_WIDTH)], axis=0)

    def conv_silu(x_ref, p_ref, cw_ref):
        x = x_ref[...]
        xf = x.astype(F32)
        p_ref[pad:2 * pad, :] = xf[0:pad]
        delayed = jnp.dot(delay_mat, x, preferred_element_type=F32)
        acc = cw_ref[CONV_WIDTH - 1:CONV_WIDTH, :] * xf
        for dd in range(1, CONV_WIDTH):
            first = p_ref[pad - dd:2 * pad - dd, :]
            xd = jnp.concatenate([first, delayed[(dd - 1) * tb + pad:dd * tb]], axis=0)
            acc = acc + cw_ref[CONV_WIDTH - 1 - dd:CONV_WIDTH - dd, :] * xd
        p_ref[0:pad, :] = xf[tb - pad:tb]
        return acc * _sigmoid(acc)

    q_all = conv_silu(q_ref, qp_ref, cwq_ref)
    k_all = conv_silu(k_ref, kp_ref, cwk_ref)
    v_all = conv_silu(v_ref, vp_ref, cwv_ref)

    shift = int(math.log2(CHUNK))
    same = (r >> shift) == (c >> shift)
    incl = jnp.logical_and(same, c <= r)
    strict = jnp.logical_and(same, c < r)

    small = sm_ref[...]
    small_t = smt_ref[...]
    lane = lax.broadcasted_iota(jnp.int32, small.shape, 1)
    def split3(a):
        hi = a.astype(BF16)
        r1 = a - hi.astype(F32)
        mid = r1.astype(BF16)
        lo = (r1 - mid.astype(F32)).astype(BF16)
        return hi.astype(F32), mid.astype(F32), lo.astype(F32)

    part = 2 * n_heads
    s_hi, s_mid, s_lo = split3(small)
    small3 = jnp.where(lane < part, s_hi,
                       jnp.where(lane < 2 * part, pltpu.roll(s_mid, part, 1),
                                 jnp.where(lane < 3 * part, pltpu.roll(s_lo, 2 * part, 1), 0.0)))
    both = _bdot(jnp.concatenate([incl.astype(F32), same.astype(F32)], axis=0), small3)
    gcum = both[:tb]
    gtot = both[tb:]
    gcum_t = _bdot(jnp.concatenate(split3(small_t), axis=0),
                   jnp.logical_and(same, r <= c).astype(F32))
    sub3 = lax.broadcasted_iota(jnp.int32, gcum_t.shape, 0)

    heads = range(GDN_HG)
    hsl = [slice(hh * HEAD_DIM, (hh + 1) * HEAD_DIM) for hh in heads]
    qs = [q_all[:, hs] for hs in hsl]
    ks = [k_all[:, hs] for hs in hsl]
    vs = [v_all[:, hs] for hs in hsl]
    qs = [q * lax.rsqrt(jnp.sum(q * q, axis=-1, keepdims=True) + EPS) * (HEAD_DIM ** -0.5) for q in qs]
    ks = [k * lax.rsqrt(jnp.sum(k * k, axis=-1, keepdims=True) + EPS) for k in ks]

    def col_of(arr, idx):
        return jnp.sum(jnp.where(lane == idx, arr, 0.0), axis=-1, keepdims=True)

    def terms_of(pos, idx):
        return jnp.logical_or(pos == idx, jnp.logical_or(pos == idx + part, pos == idx + 2 * part))

    head_ids = [hg * GDN_HG + hh for hh in heads]
    betas = [col_of(small, hd) for hd in head_ids]
    gcs = [jnp.sum(jnp.where(terms_of(lane, hd + n_heads), gcum, 0.0), axis=-1, keepdims=True)
           for hd in head_ids]
    gls = [jnp.sum(jnp.where(terms_of(lane, hd + n_heads), gtot, 0.0), axis=-1, keepdims=True)
           for hd in head_ids]
    gc_rows = [jnp.sum(jnp.where(terms_of(sub3, hd + n_heads), gcum_t, 0.0), axis=0, keepdims=True)
               for hd in head_ids]

    decays = [jnp.where(incl, jnp.exp(jnp.minimum(gc - gr, 0.0)), 0.0) for gc, gr in zip(gcs, gc_rows)]
    kbs = [k * b for k, b in zip(ks, betas)]
    kks = [_bdot_nt(kb, k) for kb, k in zip(kbs, ks)]
    pws = [jnp.where(strict, -(kk * dc), 0.0) for kk, dc in zip(kks, decays)]
    n_chunks = tb // CHUNK
    cat_row = lax.broadcasted_iota(jnp.int32, (CHUNK, tb), 0)
    cat_lane = lax.broadcasted_iota(jnp.int32, (CHUNK, tb), 1)
    lane_chunk = cat_lane >> shift

    def block_diag(m_cat):
        return jnp.concatenate([jnp.where(lane_chunk == ci, m_cat, 0.0) for ci in range(n_chunks)], axis=0)

    def cat_of(m_bd):
        out = m_bd[0:CHUNK]
        for ci in range(1, n_chunks):
            out = out + m_bd[ci * CHUNK:(ci + 1) * CHUNK]
        return out

    pcats = [cat_of(pw) for pw in pws]
    eye_cat = ((cat_lane & (CHUNK - 1)) == cat_row).astype(F32)
    tcats = [eye_cat + pc for pc in pcats]
    pcats = [_bdot(pc, block_diag(pc)) for pc in pcats]
    n_levels = int(math.log2(CHUNK))
    for lev in range(1, n_levels):
        bds = [block_diag(pc) for pc in pcats]
        if lev < n_levels - 1:
            prods = [_bdot(jnp.concatenate([pc, tc], axis=0), bd) for pc, tc, bd in zip(pcats, tcats, bds)]
            pcats = [pr[:CHUNK] for pr in prods]
            tcats = [tc + pr[CHUNK:] for tc, pr in zip(tcats, prods)]
        else:
            tcats = [tc + _bdot(tc, bd) for tc, bd in zip(tcats, bds)]
    tmats = [block_diag(tc) for tc in tcats]
    egcs = [jnp.exp(gc) for gc in gcs]
    uws = [_bdot(tm, jnp.concatenate([v * b, kb * eg], axis=1))
           for tm, v, b, kb, eg in zip(tmats, vs, betas, kbs, egcs)]
    us = [uw[:, :HEAD_DIM] for uw in uws]
    ws = [uw[:, HEAD_DIM:] for uw in uws]
    qkm = [_bdot_nt(q, k) for q, k in zip(qs, ks)]
    qkm = [jnp.where(incl, x * dc, 0.0) for x, dc in zip(qkm, decays)]
    q_decs = [q * eg for q, eg in zip(qs, egcs)]
    k_ends = [k * jnp.exp(gl - gc) for k, gl, gc in zip(ks, gls, gcs)]

    for hh in heads:
        vn_ref[hh] = jnp.zeros((tb, HEAD_DIM), F32)
    outs = [[] for _ in heads]
    for ci in range(tb // CHUNK):
        cs = slice(ci * CHUNK, (ci + 1) * CHUNK)
        sts = [state_ref[hh] for hh in heads]
        ws_qs = [_bdot(jnp.concatenate([ws[hh][cs], q_decs[hh][cs]], axis=0), sts[hh]) for hh in heads]
        v_news = [us[hh][cs] - ws_qs[hh][:CHUNK] for hh in heads]
        for hh in heads:
            vn_ref[hh, cs, :] = v_news[hh]
        intra = [_bdot(qkm[hh][cs], vn_ref[hh]) for hh in heads]
        upd = [_bdot_tn(k_ends[hh][cs], v_news[hh]) for hh in heads]
        for hh in heads:
            outs[hh].append(ws_qs[hh][CHUNK:] + intra[hh])
            g_last = gls[hh][ci * CHUNK:ci * CHUNK + 1, :]
            state_ref[hh] = sts[hh] * jnp.exp(g_last) + upd[hh]
    for hh in heads:
        o = jnp.concatenate(outs[hh], axis=0)
        o = o * lax.rsqrt(jnp.mean(o * o, axis=-1, keepdims=True) + EPS) * gout_ref[...]
        zz = z_ref[:, hsl[hh]].astype(F32)
        o_ref[:, hsl[hh]] = (o * (zz * _sigmoid(zz))).astype(o_ref.dtype)


def _gated_delta(big, small, small_t, conv_w, g_out, bsz, seq, n_heads, d_model):
    t = bsz * seq
    tb = GDN_TB
    ns = seq // tb
    width = GDN_HG * HEAD_DIM
    nhg = n_heads // GDN_HG
    blocks_per_group = d_model // width
    rows_t = small_t.shape[0]

    def colspec(group):
        return pl.BlockSpec((tb, width), lambda b, h, s: (b * ns + s, group * blocks_per_group + h))

    def cwspec(group):
        return pl.BlockSpec((CONV_WIDTH, width), lambda b, h, s: (0, group * blocks_per_group + h))

    return pl.pallas_call(
        functools.partial(_gdn_kernel, n_heads=n_heads),
        grid=(bsz, nhg, ns),
        in_specs=[
            colspec(0), colspec(1), colspec(2), colspec(3),
            pl.BlockSpec((tb, LANES), lambda b, h, s: (b * ns + s, 0)),
            pl.BlockSpec((rows_t, tb), lambda b, h, s: (0, b * ns + s)),
            cwspec(0), cwspec(1), cwspec(2),
            pl.BlockSpec((1, HEAD_DIM), lambda b, h, s: (0, 0)),
        ],
        out_specs=pl.BlockSpec((tb, width), lambda b, h, s: (b * ns + s, h)),
        out_shape=jax.ShapeDtypeStruct((t, d_model), BF16),
        scratch_shapes=[
            pltpu.VMEM((GDN_HG, HEAD_DIM, HEAD_DIM), F32),
            pltpu.VMEM((2 * SUBLANES, width), F32),
            pltpu.VMEM((2 * SUBLANES, width), F32),
            pltpu.VMEM((2 * SUBLANES, width), F32),
            pltpu.VMEM((GDN_HG, tb, HEAD_DIM), F32),
        ],
        compiler_params=_cparams(("parallel", "parallel", "arbitrary")),
        name="gated_delta",
    )(big, big, big, big, small, small_t, conv_w, conv_w, conv_w, g_out)


def _t5_bucket(n):
    max_exact = N_BUCKETS // 2
    nf = jnp.maximum(n, 1).astype(F32)
    large = max_exact + (jnp.log(nf / max_exact) / math.log(MAX_DISTANCE / max_exact)
                         * (N_BUCKETS - max_exact)).astype(jnp.int32)
    large = jnp.minimum(large, N_BUCKETS - 1)
    return jnp.where(n < max_exact, n, large)


def _attn_kernel(rb_ref, q_ref, k_ref, v_ref, lam_ref, gsub_ref, o_ref,
                 bias_ref, m_ref, acc_ref, sa_ref, sb_ref, qs_ref, *, lam_init):
    hg = pl.program_id(0)
    b = pl.program_id(1)
    bq, bk = ATT_BQ, ATT_BK
    heads = range(ATT_HG)
    hsl = [slice(hh * HEAD_DIM, (hh + 1) * HEAD_DIM) for hh in heads]

    @pl.when(b == 0)
    def _():
        blk = LANES
        i = lax.broadcasted_iota(jnp.int32, (blk, blk), 0)
        jj = lax.broadcasted_iota(jnp.int32, (blk, blk), 1)
        for hh in heads:
            head = hg * ATT_HG + hh
            far = rb_ref[N_BUCKETS - 1, head]

            def toeplitz(offset):
                bucket = _t5_bucket(jnp.maximum(i - jj + offset, 0))
                out = jnp.zeros((blk, blk), F32)
                for cc in range(N_BUCKETS):
                    out = jnp.where(bucket == cc, rb_ref[cc, head] - far, out)
                return out

            on_diag = jnp.where(i >= jj, toeplitz(0), NEG_BIG)
            next_diag = toeplitz(blk)
            kinds = {0: on_diag, 1: next_diag}
            bias_ref[hh, 2] = jnp.zeros((bq, bk), F32)
            for slot in range(2):
                for rr in range(bq // blk):
                    for cc in range(bk // blk):
                        delta = rr - cc + slot * (bk // blk)
                        if delta < 0:
                            tile = jnp.full((blk, blk), NEG_BIG, F32)
                        else:
                            tile = kinds.get(delta, jnp.zeros((blk, blk), F32))
                        bias_ref[hh, slot, rr * blk:(rr + 1) * blk, cc * blk:(cc + 1) * blk] = tile

    lane = lax.broadcasted_iota(jnp.int32, (bq, HEAD_DIM), 1)
    ones_col = (lax.broadcasted_iota(jnp.int32, (bk, HEAD_DIM), 1) == 0).astype(BF16)
    lam_p = lam_ref[...]
    s1 = jnp.sum(lam_p[0:1] * lam_p[1:2], axis=-1, keepdims=True)
    s2 = jnp.sum(lam_p[2:3] * lam_p[3:4], axis=-1, keepdims=True)
    lam = jnp.exp(s1) - jnp.exp(s2) + lam_init

    n_q = q_ref.shape[0] // bq

    def stack_q(qb):
        rows = pl.ds(pl.multiple_of(qb * bq, bq), bq)
        for hh in heads:
            q = q_ref[rows, hsl[hh]]
            zero = jnp.zeros_like(q)
            qs_ref[hh, 0:bq, :] = jnp.where(lane < DH_DIFF, q, zero)
            qs_ref[hh, bq:2 * bq, :] = jnp.where(lane < DH_DIFF, zero, q)

    def scores(j, s_ref):
        ks = pl.multiple_of(j * bk, bk)
        for hh in heads:
            s_ref[hh] = lax.dot_general(qs_ref[hh], k_ref[pl.ds(ks, bk), hsl[hh]],
                                        (((1,), (1,)), ((), ())), preferred_element_type=F32)

    stack_q(0)
    scores(0, sa_ref)

    def query_block(qi, outer):
        rows = pl.ds(pl.multiple_of(qi * bq, bq), bq)
        m_ref[...] = jnp.full(m_ref.shape, NEG_BIG, F32)
        acc_ref[...] = jnp.zeros(acc_ref.shape, F32)

        def absorb(j, s_ref, biased=True):
            ks = pl.multiple_of(j * bk, bk)
            v_exts = [jnp.concatenate([v_ref[pl.ds(ks, bk), hs], ones_col], axis=1) for hs in hsl]
            if biased:
                slot = jnp.minimum(qi - j, 2)
                scs = [jnp.concatenate([s_ref[hh, 0:bq, :] + bias_ref[hh, slot],
                                        s_ref[hh, bq:2 * bq, :] + bias_ref[hh, slot]], axis=0)
                       for hh in heads]
            else:
                scs = [s_ref[hh] for hh in heads]
            m_olds = [m_ref[hh] for hh in heads]
            m_news = [jnp.maximum(mo, jnp.max(sc, axis=-1, keepdims=True)) for mo, sc in zip(m_olds, scs)]
            ps = [jnp.exp(sc - mn) for sc, mn in zip(scs, m_news)]
            pvs = [jnp.dot(p.astype(BF16), ve, preferred_element_type=F32) for p, ve in zip(ps, v_exts)]
            for hh in heads:
                acc_ref[hh] = jnp.exp(m_olds[hh] - m_news[hh]) * acc_ref[hh] + pvs[hh]
                m_ref[hh] = m_news[hh]

        n_tiles = qi + 1

        def pair_body(jj, carry, biased):
            j0 = 2 * jj
            scores(j0 + 1, sb_ref)
            absorb(j0, sa_ref, biased)
            scores(jnp.minimum(j0 + 2, qi), sa_ref)
            absorb(j0 + 1, sb_ref, biased)
            return carry

        n_far_pairs = jnp.maximum(qi - 1, 0) // 2
        lax.fori_loop(0, n_far_pairs, functools.partial(pair_body, biased=False), 0)
        lax.fori_loop(n_far_pairs, n_tiles // 2, functools.partial(pair_body, biased=True), 0)

        @pl.when(n_tiles % 2 == 1)
        def _():
            absorb(qi, sa_ref)

        stack_q(jnp.minimum(qi + 1, n_q - 1))
        scores(0, sa_ref)
        for hh in heads:
            acc = acc_ref[hh]
            num = acc[:, :HEAD_DIM]
            den = acc[:, HEAD_DIM:HEAD_DIM + 1]
            o = num[:bq] / den[:bq] - lam * (num[bq:] / den[bq:])
            o = o * lax.rsqrt(jnp.mean(o * o, axis=-1, keepdims=True) + EPS) * gsub_ref[...]
            o_ref[rows, hsl[hh]] = (o * (1.0 - lam_init)).astype(o_ref.dtype)
        return outer

    lax.fori_loop(0, n_q, query_block, 0)


def _diff_attention(proj_qk, proj_plain, rel_bias, lam_params, g_subln, bsz, seq, n_heads, d_model,
                    lam_init):
    t = bsz * seq
    assert ATT_BQ == ATT_BK and MAX_DISTANCE <= LANES and n_heads % ATT_HG == 0 and seq % ATT_BQ == 0
    width = ATT_HG * HEAD_DIM
    per = d_model // width
    vcol = 4 * per
    return pl.pallas_call(
        functools.partial(_attn_kernel, lam_init=lam_init),
        grid=(n_heads // ATT_HG, bsz),
        in_specs=[
            pl.BlockSpec(memory_space=pltpu.SMEM),
            pl.BlockSpec((seq, width), lambda h, b: (b, h)),
            pl.BlockSpec((seq, width), lambda h, b: (b, per + h)),
            pl.BlockSpec((seq, width), lambda h, b: (b, vcol + h)),
            pl.BlockSpec((4, DH_DIFF), lambda h, b: (0, 0)),
            pl.BlockSpec((1, HEAD_DIM), lambda h, b: (0, 0)),
        ],
        out_specs=pl.BlockSpec((seq, width), lambda h, b: (b, h)),
        out_shape=jax.ShapeDtypeStruct((t, d_model), BF16),
        scratch_shapes=[
            pltpu.VMEM((ATT_HG, 3, ATT_BQ, ATT_BK), F32),
            pltpu.VMEM((ATT_HG, 2 * ATT_BQ, 1), F32),
            pltpu.VMEM((ATT_HG, 2 * ATT_BQ, 2 * HEAD_DIM), F32),
            pltpu.VMEM((ATT_HG, 2 * ATT_BQ, ATT_BK), F32),
            pltpu.VMEM((ATT_HG, 2 * ATT_BQ, ATT_BK), F32),
            pltpu.VMEM((ATT_HG, 2 * ATT_BQ, HEAD_DIM), BF16),
        ],
        compiler_params=_cparams(("arbitrary", "arbitrary")),
        name="diff_attention",
    )(rel_bias, proj_qk, proj_qk, proj_plain, lam_params, g_subln)


def _mix_kernel(ga_ref, gb_ref, oa_ref, od_ref, x_ref, wo_ref, gffn_ref, wr_ref, br_ref,
                x1_ref, h2_ref, topi_ref, topw_ref, rank_ref, cnt_ref, carry_ref):
    i = pl.program_id(0)
    tm = MIX_TM

    @pl.when(i == 0)
    def _():
        carry_ref[...] = jnp.zeros_like(carry_ref)

    tp = tm // MIX_PARTS
    parts = range(MIX_PARTS)
    rows = [slice(pp * tp, (pp + 1) * tp) for pp in parts]
    mixes = [ga_ref[rs, :] * oa_ref[rs, :] + gb_ref[rs, :] * od_ref[rs, :] for rs in rows]
    x1s = [x_ref[rs, :] + jnp.dot(mx, wo_ref[...], preferred_element_type=F32) for rs, mx in zip(rows, mixes)]
    for rs, x1 in zip(rows, x1s):
        x1_ref[rs, :] = x1
    h2s = [x1 * lax.rsqrt(jnp.mean(x1 * x1, axis=-1, keepdims=True) + EPS) * gffn_ref[...] for x1 in x1s]
    for rs, h2 in zip(rows, h2s):
        h2_ref[rs, :] = _pack_halves(h2)

    curs = [lax.dot_general(wr_ref[...], h2, (((1,), (1,)), ((), ())), preferred_element_type=F32,
                            precision=lax.Precision.HIGHEST) + br_ref[...] for h2 in h2s]
    eidx = lax.broadcasted_iota(jnp.int32, curs[0].shape, 0).astype(F32)
    vals = [[] for _ in parts]
    hots = [[] for _ in parts]
    for kk in range(TOP_K):
        mxs = [jnp.max(cur, axis=0, keepdims=True) for cur in curs]
        idxs = [jnp.min(jnp.where(cur == mx, eidx, float(N_EXPERTS)), axis=0, keepdims=True)
                for cur, mx in zip(curs, mxs)]
        for pp in parts:
            hot = eidx == idxs[pp]
            vals[pp].append(mxs[pp])
            hots[pp].append(hot)
            topi_ref[kk:kk + 1, rows[pp]] = idxs[pp].astype(jnp.int32)
            curs[pp] = jnp.where(hot, -jnp.inf, curs[pp])
    for pp in parts:
        exps = [jnp.exp(vv - vals[pp][0]) for vv in vals[pp]]
        denom = exps[0] + exps[1] + exps[2] + exps[3]
        for kk in range(TOP_K):
            topw_ref[kk:kk + 1, rows[pp]] = exps[kk] / denom

    r = lax.broadcasted_iota(jnp.int32, (tp, tp), 0)
    c = lax.broadcasted_iota(jnp.int32, (tp, tp), 1)
    earlier = (r < c).astype(F32)
    sel_fs = []
    for pp in parts:
        sel = hots[pp][0]
        for kk in range(1, TOP_K):
            sel = jnp.logical_or(sel, hots[pp][kk])
        sel_fs.append(sel.astype(F32))
    within = [_bdot(sf, earlier) for sf in sel_fs]
    totals = [jnp.sum(sf, axis=-1, keepdims=True) for sf in sel_fs]
    run = carry_ref[...]
    for pp in parts:
        before = within[pp] + run
        for kk in range(TOP_K):
            rank_ref[kk:kk + 1, rows[pp]] = jnp.sum(jnp.where(hots[pp][kk], before, 0.0), axis=0,
                                                    keepdims=True).astype(jnp.int32)
        run = run + totals[pp]
    carry_ref[...] = run
    cnt_ref[...] = run.astype(jnp.int32)


def _mix_project_route(proj_gate, oa, od, x2d, w_o, g_ffn, w_r_t, b_r, d_model):
    t = x2d.shape[0]
    tm = MIX_TM
    full = lambda shape: pl.BlockSpec(shape, lambda i: (0, 0))
    row = lambda: pl.BlockSpec((tm, d_model), lambda i: (i, 0))
    krow = lambda: pl.BlockSpec((TOP_K, tm), lambda i: (0, i))
    return pl.pallas_call(
        _mix_kernel,
        grid=(t // tm,),
        in_specs=[
            pl.BlockSpec((tm, d_model), lambda i: (i, 0)),
            pl.BlockSpec((tm, d_model), lambda i: (i, 1)),
            row(), row(), row(),
            full((d_model, d_model)), full((1, d_model)), full((N_EXPERTS, d_model)), full((N_EXPERTS, 1)),
        ],
        out_specs=[row(), pl.BlockSpec((tm, d_model // 2), lambda i: (i, 0)),
                   krow(), krow(), krow(), full((N_EXPERTS, 1))],
        out_shape=[
            jax.ShapeDtypeStruct((t, d_model), F32),
            jax.ShapeDtypeStruct((t, d_model // 2), jnp.int32),
            jax.ShapeDtypeStruct((TOP_K, t), jnp.int32),
            jax.ShapeDtypeStruct((TOP_K, t), F32),
            jax.ShapeDtypeStruct((TOP_K, t), jnp.int32),
            jax.ShapeDtypeStruct((N_EXPERTS, 1), jnp.int32),
        ],
        scratch_shapes=[pltpu.VMEM((N_EXPERTS, 1), F32)],
        compiler_params=_cparams(("arbitrary",)),
        name="merge_outproj_route",
    )(proj_gate, proj_gate, oa, od, x2d, w_o, g_ffn, w_r_t, b_r)


def _pack_halves(x):
    half = x.shape[1] // 2
    bits = pltpu.bitcast(x.astype(BF16).astype(F32), jnp.int32)
    return bits[:, :half] | lax.shift_right_logical(bits[:, half:], 16)


def _unpack_halves(p):
    hi = pltpu.bitcast(p & jnp.int32(-65536), F32)
    lo = pltpu.bitcast(lax.shift_left(p, 16), F32)
    return jnp.concatenate([hi, lo], axis=1)


def _expert_kernel(be_ref, nu_ref, x_ref, wup_ref, bup_ref, wdn_ref, bdn_ref, *rest):
    y_ref, wup_bf, wdn_bf = rest[-3:]
    i = pl.program_id(0)
    d_ff = wdn_ref.shape[1]

    @pl.when(jnp.logical_or(i == 0, be_ref[i] != be_ref[jnp.maximum(i - 1, 0)]))
    def _():
        rr = lax.broadcasted_iota(jnp.int32, (2 * LANES, 2 * LANES), 0)
        cc = lax.broadcasted_iota(jnp.int32, (2 * LANES, 2 * LANES), 1)
        pick = jnp.where(cc < LANES, 2 * cc, 2 * (cc - LANES) + 1)
        perm = (rr == pick).astype(BF16)
        for g in range(wup_ref.shape[2] // (2 * LANES)):
            cs = slice(g * 2 * LANES, (g + 1) * 2 * LANES)
            wup_bf[:, cs] = jnp.dot(wup_ref[0, :, cs].astype(BF16), perm,
                                    preferred_element_type=F32).astype(BF16)
        wdn_bf[...] = wdn_ref[0].astype(BF16)

    @pl.when(i < nu_ref[0])
    def _():
        x = _unpack_halves(x_ref[...])
        hid = jnp.dot(x.astype(BF16), wup_bf[...], preferred_element_type=F32) + bup_ref[0]
        acts = []
        for g in range(hid.shape[1] // (2 * LANES)):
            glu = jnp.minimum(hid[:, g * 2 * LANES:g * 2 * LANES + LANES], SWIGLU_LIMIT)
            lin = jnp.clip(hid[:, g * 2 * LANES + LANES:(g + 1) * 2 * LANES], -SWIGLU_LIMIT, SWIGLU_LIMIT)
            acts.append(glu * _sigmoid(SWIGLU_ALPHA * glu) * (lin + 1.0))
        act = jnp.concatenate(acts, axis=1)
        assert act.shape[1] == d_ff
        y = jnp.dot(act.astype(BF16), wdn_bf[...], preferred_element_type=F32) + bdn_ref[0]
        y_ref[...] = _pack_halves(y)

    @pl.when(i >= nu_ref[0])
    def _():
        y_ref[...] = jnp.zeros(y_ref.shape, y_ref.dtype)


def _experts(block_e, n_used, xs, y_prev, first_block, n_rows_total, w_up, b_up, w_down, b_down):
    n_rows, half = xs.shape
    d = w_up.shape[1]
    nb = n_rows // MOE_RB
    two_ff = w_up.shape[2]
    d_ff = w_down.shape[1]
    in_specs = [
        pl.BlockSpec((MOE_RB, half), lambda i, be, nu: (jnp.maximum(jnp.minimum(i, nu[0] - 1), 0), 0)),
        pl.BlockSpec((1, d, two_ff), lambda i, be, nu: (be[i], 0, 0)),
        pl.BlockSpec((1, 1, two_ff), lambda i, be, nu: (be[i], 0, 0)),
        pl.BlockSpec((1, d_ff, d), lambda i, be, nu: (be[i], 0, 0)),
        pl.BlockSpec((1, 1, d), lambda i, be, nu: (be[i], 0, 0)),
    ]
    operands = [block_e, n_used, xs, w_up, b_up, w_down, b_down]
    aliases = {}
    if y_prev is not None:
        in_specs.append(pl.BlockSpec(memory_space=pl.ANY))
        aliases = {len(operands): 0}
        operands.append(y_prev)
    grid_spec = pltpu.PrefetchScalarGridSpec(
        num_scalar_prefetch=2,
        grid=(nb,),
        in_specs=in_specs,
        out_specs=pl.BlockSpec((MOE_RB, half), lambda i, be, nu: (first_block + i, 0)),
        scratch_shapes=[pltpu.VMEM((d, two_ff), BF16), pltpu.VMEM((d_ff, d), BF16)],
    )
    return pl.pallas_call(
        _expert_kernel,
        grid_spec=grid_spec,
        out_shape=jax.ShapeDtypeStruct((n_rows_total, half), jnp.int32),
        input_output_aliases=aliases,
        compiler_params=_cparams(("arbitrary",)),
        name="moe_experts",
    )(*operands)


def _sc_invert_slots(dest_flat, n_rows):
    n_assign = dest_flat.shape[0]
    n_workers = SC_CORES * SC_SUBCORES
    rows_per_w = n_rows // n_workers
    chunk = SC_SCAN_CHUNK
    assert n_rows % n_workers == 0 and rows_per_w % SC_LANES == 0 and n_assign % chunk == 0
    mesh = plsc.VectorSubcoreMesh(core_axis_name="c", subcore_axis_name="s",
                                  num_cores=SC_CORES, num_subcores=SC_SUBCORES)

    def body(dest_hbm, out_hbm, dest_v, map_v):
        wid = lax.axis_index("s") * SC_CORES + lax.axis_index("c")
        base = wid * rows_per_w
        lanes = lax.broadcasted_iota(jnp.int32, (SC_LANES,), 0)

        @pl.loop(0, rows_per_w, step=SC_LANES)
        def _(r0):
            map_v[pl.ds(r0, SC_LANES)] = jnp.full((SC_LANES,), -1, jnp.int32)

        @pl.loop(0, n_assign // chunk)
        def _(ci):
            pltpu.sync_copy(dest_hbm.at[pl.ds(ci * chunk, chunk)], dest_v)

            @pl.loop(0, chunk, step=SC_LANES)
            def _(j):
                local = dest_v[pl.ds(j, SC_LANES)] - base
                mine = jnp.logical_and(local >= 0, local < rows_per_w)
                plsc.store_scatter(map_v, [jnp.where(mine, local, 0)], ci * chunk + j + lanes, mask=mine)

        pltpu.sync_copy(map_v, out_hbm.at[pl.ds(base, rows_per_w)])

    return pl.kernel(
        body,
        out_type=jax.ShapeDtypeStruct((n_rows,), jnp.int32),
        mesh=mesh,
        scratch_types=[pltpu.VMEM((chunk,), jnp.int32), pltpu.VMEM((rows_per_w,), jnp.int32)],
        compiler_params=pltpu.CompilerParams(needs_layout_passes=False),
        name="moe_slot_inverse",
    )(dest_flat)


def _sc_gather_rows(table, idx):
    n_idx = idx.shape[0]
    d = table.shape[1]
    n_workers = SC_CORES * SC_SUBCORES
    per_worker = n_idx // n_workers
    n_chunks = per_worker // SC_GATHER_ROWS
    assert n_idx % n_workers == 0 and per_worker % SC_GATHER_ROWS == 0
    mesh = plsc.VectorSubcoreMesh(core_axis_name="c", subcore_axis_name="s",
                                  num_cores=SC_CORES, num_subcores=SC_SUBCORES)

    assert n_chunks % 2 == 0

    def body(table_hbm, idx_hbm, out_hbm, idx_v, rows_a, rows_b, sem_a, sem_b):
        wid = lax.axis_index("s") * SC_CORES + lax.axis_index("c")
        base = wid * per_worker
        pltpu.sync_copy(idx_hbm.at[pl.ds(base, per_worker)], idx_v)

        def gather(ci, rows_v, sem):
            off = pl.multiple_of(ci * SC_GATHER_ROWS, SC_GATHER_ROWS)
            return pltpu.make_async_copy(table_hbm.at[idx_v.at[pl.ds(off, SC_GATHER_ROWS)]], rows_v, sem)

        def put(ci, rows_v):
            off = pl.multiple_of(ci * SC_GATHER_ROWS, SC_GATHER_ROWS)
            pltpu.sync_copy(rows_v, out_hbm.at[pl.ds(base + off, SC_GATHER_ROWS)])

        gather(0, rows_a, sem_a).start()

        @pl.loop(0, n_chunks, step=2)
        def _(ci):
            gather(ci + 1, rows_b, sem_b).start()
            gather(ci, rows_a, sem_a).wait()
            put(ci, rows_a)
            nxt = jnp.minimum(ci + 2, n_chunks - 1)
            gather(nxt, rows_a, sem_a).start()
            gather(ci + 1, rows_b, sem_b).wait()
            put(ci + 1, rows_b)

        gather(n_chunks - 1, rows_a, sem_a).wait()

    return pl.kernel(
        body,
        out_type=jax.ShapeDtypeStruct((n_idx, d), table.dtype),
        mesh=mesh,
        scratch_types=[
            pltpu.VMEM((per_worker,), jnp.int32),
            pltpu.VMEM((SC_GATHER_ROWS, d), table.dtype),
            pltpu.VMEM((SC_GATHER_ROWS, d), table.dtype),
            pltpu.SemaphoreType.DMA,
            pltpu.SemaphoreType.DMA,
        ],
        name="moe_slot_gather",
    )(table, idx)


def _combine_kernel(x1_ref, w_ref, y0_ref, y1_ref, y2_ref, y3_ref, o_ref):
    w = w_ref[...]
    out = x1_ref[...]
    for kk, y_ref in enumerate((y0_ref, y1_ref, y2_ref, y3_ref)):
        out = out + w[:, kk:kk + 1] * _unpack_halves(y_ref[...])
    o_ref[...] = out


def _combine(x1, w_tok, y_slots):
    t, d = x1.shape
    tc = COMB_TC
    nt = t // tc
    yspec = lambda kk: pl.BlockSpec((tc, d // 2), lambda i: (kk * nt + i, 0))
    return pl.pallas_call(
        _combine_kernel,
        grid=(nt,),
        in_specs=[
            pl.BlockSpec((tc, d), lambda i: (i, 0)),
            pl.BlockSpec((tc, TOP_K), lambda i: (i, 0)),
            yspec(0), yspec(1), yspec(2), yspec(3),
        ],
        out_specs=pl.BlockSpec((tc, d), lambda i: (i, 0)),
        out_shape=jax.ShapeDtypeStruct((t, d), F32),
        compiler_params=_cparams(("parallel",)),
        name="moe_combine",
    )(x1, w_tok, y_slots, y_slots, y_slots, y_slots)


def _moe(x1, h2, topi, topw, rank, counts, w_up, b_up, w_down, b_down):
    t, d = x1.shape
    n_assign = t * TOP_K
    nb = -(-n_assign // MOE_RB) + N_EXPERTS
    n_rows = nb * MOE_RB
    counts = counts[:, 0]
    padded = (counts + MOE_RB - 1) // MOE_RB * MOE_RB
    padded_end = jnp.cumsum(padded)
    padded_start = padded_end - padded
    expert_ids = jnp.arange(N_EXPERTS, dtype=jnp.int32)[:, None, None]
    start_of = jnp.sum(jnp.where(topi[None] == expert_ids, padded_start[:, None, None], 0), axis=0)
    dest = (start_of + rank).astype(jnp.int32)
    n_used = (padded_end[-1] // MOE_RB).astype(jnp.int32)
    blk = jnp.minimum(jnp.arange(nb, dtype=jnp.int32), n_used - 1)
    block_e = jnp.minimum(jnp.sum(padded_end[None, :] <= (blk * MOE_RB)[:, None], axis=1),
                          N_EXPERTS - 1).astype(jnp.int32)
    slot_of = _sc_invert_slots(dest.reshape(-1), n_rows)
    src_tok = jnp.where(slot_of < 0, jnp.arange(n_rows, dtype=jnp.int32), slot_of) % t

    nb_a = nb // 4
    y_rows = None
    for first, n_blk in ((0, nb_a), (nb_a, nb - nb_a)):
        xs = _sc_gather_rows(h2, lax.slice(src_tok, (first * MOE_RB,), ((first + n_blk) * MOE_RB,)))
        used = jnp.clip(n_used - first, 0, n_blk).reshape(1)
        y_rows = _experts(lax.slice(block_e, (first,), (first + n_blk,)), used, xs, y_rows, first, n_rows,
                          w_up, b_up, w_down, b_down)
    y_slots = _sc_gather_rows(y_rows, dest.reshape(-1))
    return _combine(x1, topw.T, y_slots)


def kernel(x, g_mix, w_in, b_gate, conv_w, a_log, dt_bias, g_delta_out, q_norm, k_norm, lambda_q1, lambda_k1, lambda_q2, lambda_k2, g_subln, rel_bias, w_o, g_ffn, w_router, b_router, w_up, b_up, w_down, b_down):
    bsz, seq, d = x.shape
    depth = g_mix.shape[0]
    n_heads = d // HEAD_DIM
    t = bsz * seq
    d_ff = w_down.shape[2]
    assert d % PROJ_TN == 0 and t % PROJ_TM == 0 and seq % GDN_TB == 0 and seq % ATT_BQ == 0
    assert t % MIX_TM == 0 and t % COMB_TC == 0 and n_heads % GDN_HG == 0
    assert (t * TOP_K) % MOE_RB == 0
    assert 2 * n_heads <= 2 * SUBLANES

    x2d = x.reshape(t, d)
    for l in range(depth):
        wl = w_in[l]
        c0 = 4 * d
        c1 = c0 + 2 * n_heads
        c2 = c1 + 2 * d
        c3 = c2 + d
        w_small = jnp.pad(wl[:, c0:c1], ((0, 0), (0, LANES - 2 * n_heads)))
        gm = g_mix[l].reshape(1, d)
        head_pad = jnp.zeros((LANES - 2 * n_heads,), F32)
        alog = jnp.concatenate([jnp.zeros((n_heads,), F32), a_log[l], head_pad])
        dtb = jnp.concatenate([jnp.zeros((n_heads,), F32), dt_bias[l], head_pad])
        rows_t = 2 * n_heads
        beta_decay = (w_small.astype(BF16), w_small[:, :rows_t].T.astype(BF16),
                      alog.reshape(1, LANES), dtb.reshape(1, LANES),
                      alog[:rows_t].reshape(rows_t, 1), dtb[:rows_t].reshape(rows_t, 1), n_heads)
        w_plain = jnp.concatenate([wl[:, :c0], wl[:, c2:c3]], axis=1).astype(BF16)
        proj_plain, small, small_t = _input_projection(x2d, gm, w_plain, jnp.zeros((1, 5 * d), F32), "plain",
                                                       beta_decay)
        qk_gain = jnp.concatenate([jnp.tile(q_norm[l] * (DH_DIFF ** -0.5), 2 * n_heads),
                                   jnp.tile(k_norm[l], 2 * n_heads)]).reshape(1, 2 * d)
        proj_qk = _input_projection(x2d, gm, wl[:, c1:c2].astype(BF16), qk_gain, "qknorm")
        proj_gate = _input_projection(x2d, gm, wl[:, c3:].astype(BF16), b_gate[l].reshape(1, 2 * d), "gate")

        oa = _gated_delta(proj_plain, small, small_t, conv_w[l], g_delta_out[l].reshape(1, HEAD_DIM),
                          bsz, seq, n_heads, d)

        lam_init = 0.8 - 0.6 * math.exp(-0.3 * l)
        lam_params = jnp.stack([lambda_q1[l], lambda_k1[l], lambda_q2[l], lambda_k2[l]])
        od = _diff_attention(proj_qk, proj_plain, rel_bias, lam_params, g_subln[l].reshape(1, HEAD_DIM),
                             bsz, seq, n_heads, d, lam_init)

        x1, h2, topi, topw, rank, counts = _mix_project_route(
            proj_gate, oa, od, x2d, w_o[l].astype(BF16), g_ffn[l].reshape(1, d),
            w_router[l].T, b_router[l].reshape(N_EXPERTS, 1), d)

        b_up_l = b_up[l].reshape(N_EXPERTS, 2 * d_ff // (2 * LANES), LANES, 2)
        b_up_l = jnp.swapaxes(b_up_l, 2, 3).reshape(N_EXPERTS, 1, 2 * d_ff)
        x2d = _moe(x1, h2, topi, topw, rank, counts, w_up[l], b_up_l,
                   w_down[l], b_down[l].reshape(N_EXPERTS, 1, d))
    return x2d.reshape(bsz, seq, d)
```

```python
import functools
import math

import jax
import jax.numpy as jnp
from jax import lax
from jax.experimental import pallas as pl
from jax.experimental.pallas import tpu as pltpu
from jax.experimental.pallas import tpu_sc as plsc

F32 = jnp.float32
BF16 = jnp.bfloat16

HEAD_DIM = 128
DH_DIFF = HEAD_DIM // 2
CONV_WIDTH = 4
CHUNK = 64
N_BUCKETS = 32
MAX_DISTANCE = 128
N_EXPERTS = 32
TOP_K = 4
SWIGLU_LIMIT = 7.0
SWIGLU_ALPHA = 1.702
EPS = 1e-6
NEG_BIG = -1e30

LANES = 128
SUBLANES = 8
VMEM_LIMIT = 56 * 1024 * 1024
SC_CORES = 2
SC_SUBCORES = 16
SC_LANES = 16
SC_GATHER_ROWS = 64
SC_SCAN_CHUNK = 4096

PROJ_TM = 1024
PROJ_TN = 1024
PROJ_CHUNK = 256
GDN_TB = 256
GDN_HG = 8
ATT_HG = 1
ATT_BQ = 512
ATT_BK = 512
MIX_TM = 1024
MIX_PARTS = 2
MOE_RB = 512
COMB_TC = 512


def _cparams(sem):
    return pltpu.CompilerParams(dimension_semantics=sem, vmem_limit_bytes=VMEM_LIMIT)


def _sigmoid(x):
    return 0.5 * jnp.tanh(0.5 * x) + 0.5


def _bdot(a, b):
    return jnp.dot(a.astype(BF16), b.astype(BF16), preferred_element_type=F32)


def _bdot_nt(a, b):
    return lax.dot_general(a.astype(BF16), b.astype(BF16), (((1,), (1,)), ((), ())),
                           preferred_element_type=F32)


def _bdot_tn(a, b):
    return lax.dot_general(a.astype(BF16), b.astype(BF16), (((0,), (0,)), ((), ())),
                           preferred_element_type=F32)


def _beta_decay(acc, idx, alog, dtb, n_heads):
    beta = _sigmoid(acc)
    z = acc + dtb
    softplus = jnp.maximum(z, 0.0) + jnp.log1p(jnp.exp(-jnp.abs(z)))
    gdec = -jnp.exp(alog) * softplus
    return jnp.where(idx < n_heads, beta, jnp.where(idx < 2 * n_heads, gdec, 0.0))


def _proj_kernel(x_ref, g_ref, w_ref, aux_ref, ws_ref, wst_ref, alog_ref, dtb_ref, alog_t_ref, dtb_t_ref,
                 op_ref, oq_ref, og_ref, os_ref, ost_ref, h_ref, *, n_plain, n_qk, n_heads):
    j = pl.program_id(1)

    @pl.when(j == 0)
    def _():
        x = x_ref[...]
        ms = jnp.mean(x * x, axis=-1, keepdims=True)
        h_ref[...] = (x * lax.rsqrt(ms + EPS) * g_ref[...]).astype(BF16)
        hb = h_ref[...]
        acc = jnp.dot(hb, ws_ref[...], preferred_element_type=F32)
        lane = lax.broadcasted_iota(jnp.int32, acc.shape, 1)
        os_ref[...] = _beta_decay(acc, lane, alog_ref[...], dtb_ref[...], n_heads)
        acc_t = lax.dot_general(wst_ref[...], hb, (((1,), (1,)), ((), ())),
                                preferred_element_type=F32)
        sub = lax.broadcasted_iota(jnp.int32, acc_t.shape, 0)
        ost_ref[...] = _beta_decay(acc_t, sub, alog_t_ref[...], dtb_t_ref[...], n_heads)

    def tile(o_ref, mode):
        h = h_ref[...]
        lo = lax.broadcasted_iota(jnp.int32, (1, LANES), 1) < DH_DIFF
        for c in range(PROJ_TN // PROJ_CHUNK):
            cs = slice(c * PROJ_CHUNK, (c + 1) * PROJ_CHUNK)
            acc = jnp.dot(h, w_ref[:, cs], preferred_element_type=F32)
            if mode == "plain":
                o_ref[:, cs] = acc.astype(o_ref.dtype)
            elif mode == "gate":
                o_ref[:, cs] = _sigmoid(acc + aux_ref[:, cs]).astype(o_ref.dtype)
            else:
                for g in range(PROJ_CHUNK // LANES):
                    sl = slice(c * PROJ_CHUNK + g * LANES, c * PROJ_CHUNK + (g + 1) * LANES)
                    y = acc[:, g * LANES:(g + 1) * LANES]
                    y2 = y * y
                    s_lo = jnp.sum(jnp.where(lo, y2, 0.0), axis=-1, keepdims=True)
                    s_hi = jnp.sum(jnp.where(lo, 0.0, y2), axis=-1, keepdims=True)
                    r = jnp.where(lo, lax.rsqrt(s_lo / DH_DIFF + EPS), lax.rsqrt(s_hi / DH_DIFF + EPS))
                    o_ref[:, sl] = (y * r * aux_ref[:, sl]).astype(o_ref.dtype)

    @pl.when(j < n_plain)
    def _():
        tile(op_ref, "plain")

    @pl.when(jnp.logical_and(j >= n_plain, j < n_plain + n_qk))
    def _():
        tile(oq_ref, "qknorm")

    @pl.when(j >= n_plain + n_qk)
    def _():
        tile(og_ref, "gate")


def _input_projection(x2d, g_mix, w, aux, n_plain, n_qk, n_gate, beta_decay):
    t, d = x2d.shape
    n_heads = beta_decay[-1]
    rows_t = 2 * n_heads
    full = lambda shape: pl.BlockSpec(shape, lambda i, j: (0, 0))
    group = lambda first, count: pl.BlockSpec(
        (PROJ_TM, PROJ_TN), lambda i, j: (i, jnp.clip(j - first, 0, count - 1)))
    return pl.pallas_call(
        functools.partial(_proj_kernel, n_plain=n_plain, n_qk=n_qk, n_heads=n_heads),
        grid=(t // PROJ_TM, n_plain + n_qk + n_gate),
        in_specs=[
            pl.BlockSpec((PROJ_TM, d), lambda i, j: (i, 0)),
            full((1, d)),
            pl.BlockSpec((d, PROJ_TN), lambda i, j: (0, j)),
            pl.BlockSpec((1, PROJ_TN), lambda i, j: (0, j)),
            full((d, LANES)), full((rows_t, d)), full((1, LANES)), full((1, LANES)),
            full((rows_t, 1)), full((rows_t, 1)),
        ],
        out_specs=[
            group(0, n_plain), group(n_plain, n_qk), group(n_plain + n_qk, n_gate),
            pl.BlockSpec((PROJ_TM, LANES), lambda i, j: (i, 0)),
            pl.BlockSpec((rows_t, PROJ_TM), lambda i, j: (0, i)),
        ],
        out_shape=[
            jax.ShapeDtypeStruct((t, n_plain * PROJ_TN), BF16),
            jax.ShapeDtypeStruct((t, n_qk * PROJ_TN), BF16),
            jax.ShapeDtypeStruct((t, n_gate * PROJ_TN), BF16),
            jax.ShapeDtypeStruct((t, LANES), F32),
            jax.ShapeDtypeStruct((rows_t, t), F32),
        ],
        scratch_shapes=[pltpu.VMEM((PROJ_TM, d), BF16)],
        compiler_params=_cparams(("parallel", "arbitrary")),
        name="input_projection",
    )(x2d, g_mix, w, aux, *beta_decay[:-1])


def _gdn_kernel(q_ref, k_ref, v_ref, z_ref, sm_ref, smt_ref, cwq_ref, cwk_ref, cwv_ref, gout_ref,
                o_ref, state_ref, qp_ref, kp_ref, vp_ref, vn_ref, *, n_heads):
    hg = pl.program_id(1)
    s = pl.program_id(2)
    tb = GDN_TB
    pad = SUBLANES
    width = GDN_HG * HEAD_DIM

    @pl.when(s == 0)
    def _():
        state_ref[...] = jnp.zeros_like(state_ref)
        for p_ref in (qp_ref, kp_ref, vp_ref):
            p_ref[0:pad, :] = jnp.zeros((pad, width), F32)

    r = lax.broadcasted_iota(jnp.int32, (tb, tb), 0)
    c = lax.broadcasted_iota(jnp.int32, (tb, tb), 1)
    delay_mat = jnp.concatenate([(r - c == dd).astype(BF16) for dd in range(1, CONV_WIDTH)], axis=0)

    def conv_silu(x_ref, p_ref, cw_ref):
        x = x_ref[...]
        xf = x.astype(F32)
        p_ref[pad:2 * pad, :] = xf[0:pad]
        delayed = jnp.dot(delay_mat, x, preferred_element_type=F32)
        acc = cw_ref[CONV_WIDTH - 1:CONV_WIDTH, :] * xf
        for dd in range(1, CONV_WIDTH):
            first = p_ref[pad - dd:2 * pad - dd, :]
            xd = jnp.concatenate([first, delayed[(dd - 1) * tb + pad:dd * tb]], axis=0)
            acc = acc + cw_ref[CONV_WIDTH - 1 - dd:CONV_WIDTH - dd, :] * xd
        p_ref[0:pad, :] = xf[tb - pad:tb]
        return acc * _sigmoid(acc)

    q_all = conv_silu(q_ref, qp_ref, cwq_ref)
    k_all = conv_silu(k_ref, kp_ref, cwk_ref)
    v_all = conv_silu(v_ref, vp_ref, cwv_ref)

    shift = int(math.log2(CHUNK))
    same = (r >> shift) == (c >> shift)
    incl = jnp.logical_and(same, c <= r)
    strict = jnp.logical_and(same, c < r)

    small = sm_ref[...]
    small_t = smt_ref[...]
    lane = lax.broadcasted_iota(jnp.int32, small.shape, 1)
    def split3(a):
        hi = a.astype(BF16)
        r1 = a - hi.astype(F32)
        mid = r1.astype(BF16)
        lo = (r1 - mid.astype(F32)).astype(BF16)
        return hi.astype(F32), mid.astype(F32), lo.astype(F32)

    part = 2 * n_heads
    s_hi, s_mid, s_lo = split3(small)
    small3 = jnp.where(lane < part, s_hi,
                       jnp.where(lane < 2 * part, pltpu.roll(s_mid, part, 1),
                                 jnp.where(lane < 3 * part, pltpu.roll(s_lo, 2 * part, 1), 0.0)))
    both = _bdot(jnp.concatenate([incl.astype(F32), same.astype(F32)], axis=0), small3)
    gcum = both[:tb]
    gtot = both[tb:]
    gcum_t = _bdot(jnp.concatenate(split3(small_t), axis=0),
                   jnp.logical_and(same, r <= c).astype(F32))
    sub3 = lax.broadcasted_iota(jnp.int32, gcum_t.shape, 0)

    heads = range(GDN_HG)
    hsl = [slice(hh * HEAD_DIM, (hh + 1) * HEAD_DIM) for hh in heads]
    qs = [q_all[:, hs] for hs in hsl]
    ks = [k_all[:, hs] for hs in hsl]
    vs = [v_all[:, hs] for hs in hsl]
    qs = [q * lax.rsqrt(jnp.sum(q * q, axis=-1, keepdims=True) + EPS) * (HEAD_DIM ** -0.5) for q in qs]
    ks = [k * lax.rsqrt(jnp.sum(k * k, axis=-1, keepdims=True) + EPS) for k in ks]

    def col_of(arr, idx):
        return jnp.sum(jnp.where(lane == idx, arr, 0.0), axis=-1, keepdims=True)

    def terms_of(pos, idx):
        return jnp.logical_or(pos == idx, jnp.logical_or(pos == idx + part, pos == idx + 2 * part))

    head_ids = [hg * GDN_HG + hh for hh in heads]
    betas = [col_of(small, hd) for hd in head_ids]
    gcs = [jnp.sum(jnp.where(terms_of(lane, hd + n_heads), gcum, 0.0), axis=-1, keepdims=True)
           for hd in head_ids]
    gls = [jnp.sum(jnp.where(terms_of(lane, hd + n_heads), gtot, 0.0), axis=-1, keepdims=True)
           for hd in head_ids]
    gc_rows = [jnp.sum(jnp.where(terms_of(sub3, hd + n_heads), gcum_t, 0.0), axis=0, keepdims=True)
               for hd in head_ids]

    decays = [jnp.where(incl, jnp.exp(jnp.minimum(gc - gr, 0.0)), 0.0) for gc, gr in zip(gcs, gc_rows)]
    kbs = [k * b for k, b in zip(ks, betas)]
    kks = [_bdot_nt(kb, k) for kb, k in zip(kbs, ks)]
    pws = [jnp.where(strict, -(kk * dc), 0.0) for kk, dc in zip(kks, decays)]
    n_chunks = tb // CHUNK
    cat_row = lax.broadcasted_iota(jnp.int32, (CHUNK, tb), 0)
    cat_lane = lax.broadcasted_iota(jnp.int32, (CHUNK, tb), 1)
    lane_chunk = cat_lane >> shift

    def block_diag(m_cat):
        return jnp.concatenate([jnp.where(lane_chunk == ci, m_cat, 0.0) for ci in range(n_chunks)], axis=0)

    def cat_of(m_bd):
        out = m_bd[0:CHUNK]
        for ci in range(1, n_chunks):
            out = out + m_bd[ci * CHUNK:(ci + 1) * CHUNK]
        return out

    pcats = [cat_of(pw) for pw in pws]
    eye_cat = ((cat_lane & (CHUNK - 1)) == cat_row).astype(F32)
    tcats = [eye_cat + pc for pc in pcats]
    pcats = [_bdot(pc, block_diag(pc)) for pc in pcats]
    n_levels = int(math.log2(CHUNK))
    for lev in range(1, n_levels):
        bds = [block_diag(pc) for pc in pcats]
        if lev < n_levels - 1:
            prods = [_bdot(jnp.concatenate([pc, tc], axis=0), bd) for pc, tc, bd in zip(pcats, tcats, bds)]
            pcats = [pr[:CHUNK] for pr in prods]
            tcats = [tc + pr[CHUNK:] for tc, pr in zip(tcats, prods)]
        else:
            tcats = [tc + _bdot(tc, bd) for tc, bd in zip(tcats, bds)]
    tmats = [block_diag(tc) for tc in tcats]
    egcs = [jnp.exp(gc) for gc in gcs]
    uws = [_bdot(tm, jnp.concatenate([v * b, kb * eg], axis=1))
           for tm, v, b, kb, eg in zip(tmats, vs, betas, kbs, egcs)]
    us = [uw[:, :HEAD_DIM] for uw in uws]
    ws = [uw[:, HEAD_DIM:] for uw in uws]
    qkm = [_bdot_nt(q, k) for q, k in zip(qs, ks)]
    qkm = [jnp.where(incl, x * dc, 0.0) for x, dc in zip(qkm, decays)]
    q_decs = [q * eg for q, eg in zip(qs, egcs)]
    k_ends = [k * jnp.exp(gl - gc) for k, gl, gc in zip(ks, gls, gcs)]

    for hh in heads:
        vn_ref[hh] = jnp.zeros((tb, HEAD_DIM), F32)
    outs = [[] for _ in heads]
    for ci in range(tb // CHUNK):
        cs = slice(ci * CHUNK, (ci + 1) * CHUNK)
        sts = [state_ref[hh] for hh in heads]
        ws_qs = [_bdot(jnp.concatenate([ws[hh][cs], q_decs[hh][cs]], axis=0), sts[hh]) for hh in heads]
        v_news = [us[hh][cs] - ws_qs[hh][:CHUNK] for hh in heads]
        for hh in heads:
            vn_ref[hh, cs, :] = v_news[hh]
        intra = [_bdot(qkm[hh][cs], vn_ref[hh]) for hh in heads]
        upd = [_bdot_tn(k_ends[hh][cs], v_news[hh]) for hh in heads]
        for hh in heads:
            outs[hh].append(ws_qs[hh][CHUNK:] + intra[hh])
            g_last = gls[hh][ci * CHUNK:ci * CHUNK + 1, :]
            state_ref[hh] = sts[hh] * jnp.exp(g_last) + upd[hh]
    for hh in heads:
        o = jnp.concatenate(outs[hh], axis=0)
        o = o * lax.rsqrt(jnp.mean(o * o, axis=-1, keepdims=True) + EPS) * gout_ref[...]
        zz = z_ref[:, hsl[hh]].astype(F32)
        o_ref[:, hsl[hh]] = (o * (zz * _sigmoid(zz))).astype(o_ref.dtype)


def _gated_delta(big, small, small_t, conv_w, g_out, bsz, seq, n_heads, d_model):
    t = bsz * seq
    tb = GDN_TB
    ns = seq // tb
    width = GDN_HG * HEAD_DIM
    nhg = n_heads // GDN_HG
    blocks_per_group = d_model // width
    rows_t = small_t.shape[0]

    def colspec(group):
        return pl.BlockSpec((tb, width), lambda b, h, s: (b * ns + s, group * blocks_per_group + h))

    def cwspec(group):
        return pl.BlockSpec((CONV_WIDTH, width), lambda b, h, s: (0, group * blocks_per_group + h))

    return pl.pallas_call(
        functools.partial(_gdn_kernel, n_heads=n_heads),
        grid=(bsz, nhg, ns),
        in_specs=[
            colspec(0), colspec(1), colspec(2), colspec(3),
            pl.BlockSpec((tb, LANES), lambda b, h, s: (b * ns + s, 0)),
            pl.BlockSpec((rows_t, tb), lambda b, h, s: (0, b * ns + s)),
            cwspec(0), cwspec(1), cwspec(2),
            pl.BlockSpec((1, HEAD_DIM), lambda b, h, s: (0, 0)),
        ],
        out_specs=pl.BlockSpec((tb, width), lambda b, h, s: (b * ns + s, h)),
        out_shape=jax.ShapeDtypeStruct((t, d_model), BF16),
        scratch_shapes=[
            pltpu.VMEM((GDN_HG, HEAD_DIM, HEAD_DIM), F32),
            pltpu.VMEM((2 * SUBLANES, width), F32),
            pltpu.VMEM((2 * SUBLANES, width), F32),
            pltpu.VMEM((2 * SUBLANES, width), F32),
            pltpu.VMEM((GDN_HG, tb, HEAD_DIM), F32),
        ],
        compiler_params=_cparams(("parallel", "parallel", "arbitrary")),
        name="gated_delta",
    )(big, big, big, big, small, small_t, conv_w, conv_w, conv_w, g_out)


def _t5_bucket(n):
    max_exact = N_BUCKETS // 2
    nf = jnp.maximum(n, 1).astype(F32)
    large = max_exact + (jnp.log(nf / max_exact) / math.log(MAX_DISTANCE / max_exact)
                         * (N_BUCKETS - max_exact)).astype(jnp.int32)
    large = jnp.minimum(large, N_BUCKETS - 1)
    return jnp.where(n < max_exact, n, large)


def _attn_kernel(rb_ref, q_ref, k_ref, v_ref, lam_ref, gsub_ref, o_ref,
                 bias_ref, m_ref, acc_ref, sa_ref, sb_ref, qs_ref, *, lam_init):
    hg = pl.program_id(0)
    b = pl.program_id(1)
    bq, bk = ATT_BQ, ATT_BK
    heads = range(ATT_HG)
    hsl = [slice(hh * HEAD_DIM, (hh + 1) * HEAD_DIM) for hh in heads]

    @pl.when(b == 0)
    def _():
        blk = LANES
        i = lax.broadcasted_iota(jnp.int32, (blk, blk), 0)
        jj = lax.broadcasted_iota(jnp.int32, (blk, blk), 1)
        for hh in heads:
            head = hg * ATT_HG + hh
            far = rb_ref[N_BUCKETS - 1, head]

            def toeplitz(offset):
                bucket = _t5_bucket(jnp.maximum(i - jj + offset, 0))
                out = jnp.zeros((blk, blk), F32)
                for cc in range(N_BUCKETS):
                    out = jnp.where(bucket == cc, rb_ref[cc, head] - far, out)
                return out

            on_diag = jnp.where(i >= jj, toeplitz(0), NEG_BIG)
            next_diag = toeplitz(blk)
            kinds = {0: on_diag, 1: next_diag}
            bias_ref[hh, 2] = jnp.zeros((bq, bk), F32)
            for slot in range(2):
                for rr in range(bq // blk):
                    for cc in range(bk // blk):
                        delta = rr - cc + slot * (bk // blk)
                        if delta < 0:
                            tile = jnp.full((blk, blk), NEG_BIG, F32)
                        else:
                            tile = kinds.get(delta, jnp.zeros((blk, blk), F32))
                        bias_ref[hh, slot, rr * blk:(rr + 1) * blk, cc * blk:(cc + 1) * blk] = tile

    lane = lax.broadcasted_iota(jnp.int32, (bq, HEAD_DIM), 1)
    ones_col = (lax.broadcasted_iota(jnp.int32, (bk, HEAD_DIM), 1) == 0).astype(BF16)
    lam_p = lam_ref[...]
    s1 = jnp.sum(lam_p[0:1] * lam_p[1:2], axis=-1, keepdims=True)
    s2 = jnp.sum(lam_p[2:3] * lam_p[3:4], axis=-1, keepdims=True)
    lam = jnp.exp(s1) - jnp.exp(s2) + lam_init

    n_q = q_ref.shape[0] // bq

    def stack_q(qb):
        rows = pl.ds(pl.multiple_of(qb * bq, bq), bq)
        for hh in heads:
            q = q_ref[rows, hsl[hh]]
            zero = jnp.zeros_like(q)
            qs_ref[hh, 0:bq, :] = jnp.where(lane < DH_DIFF, q, zero)
            qs_ref[hh, bq:2 * bq, :] = jnp.where(lane < DH_DIFF, zero, q)

    def scores(j, s_ref):
        ks = pl.multiple_of(j * bk, bk)
        for hh in heads:
            s_ref[hh] = lax.dot_general(qs_ref[hh], k_ref[pl.ds(ks, bk), hsl[hh]],
                                        (((1,), (1,)), ((), ())), preferred_element_type=F32)

    stack_q(0)
    scores(0, sa_ref)

    def query_block(qi, outer):
        rows = pl.ds(pl.multiple_of(qi * bq, bq), bq)
        m_ref[...] = jnp.full(m_ref.shape, NEG_BIG, F32)
        acc_ref[...] = jnp.zeros(acc_ref.shape, F32)

        def absorb(j, s_ref, biased=True):
            ks = pl.multiple_of(j * bk, bk)
            v_exts = [jnp.concatenate([v_ref[pl.ds(ks, bk), hs], ones_col], axis=1) for hs in hsl]
            if biased:
                slot = jnp.minimum(qi - j, 2)
                scs = [jnp.concatenate([s_ref[hh, 0:bq, :] + bias_ref[hh, slot],
                                        s_ref[hh, bq:2 * bq, :] + bias_ref[hh, slot]], axis=0)
                       for hh in heads]
            else:
                scs = [s_ref[hh] for hh in heads]
            m_olds = [m_ref[hh] for hh in heads]
            m_news = [jnp.maximum(mo, jnp.max(sc, axis=-1, keepdims=True)) for mo, sc in zip(m_olds, scs)]
            ps = [jnp.exp(sc - mn) for sc, mn in zip(scs, m_news)]
            pvs = [jnp.dot(p.astype(BF16), ve, preferred_element_type=F32) for p, ve in zip(ps, v_exts)]
            for hh in heads:
                acc_ref[hh] = jnp.exp(m_olds[hh] - m_news[hh]) * acc_ref[hh] + pvs[hh]
                m_ref[hh] = m_news[hh]

        n_tiles = qi + 1

        def pair_body(jj, carry, biased):
            j0 = 2 * jj
            scores(j0 + 1, sb_ref)
            absorb(j0, sa_ref, biased)
            scores(jnp.minimum(j0 + 2, qi), sa_ref)
            absorb(j0 + 1, sb_ref, biased)
            return carry

        n_far_pairs = jnp.maximum(qi - 1, 0) // 2
        lax.fori_loop(0, n_far_pairs, functools.partial(pair_body, biased=False), 0)
        lax.fori_loop(n_far_pairs, n_tiles // 2, functools.partial(pair_body, biased=True), 0)

        @pl.when(n_tiles % 2 == 1)
        def _():
            absorb(qi, sa_ref)

        stack_q(jnp.minimum(qi + 1, n_q - 1))
        scores(0, sa_ref)
        for hh in heads:
            acc = acc_ref[hh]
            num = acc[:, :HEAD_DIM]
            den = acc[:, HEAD_DIM:HEAD_DIM + 1]
            o = num[:bq] / den[:bq] - lam * (num[bq:] / den[bq:])
            o = o * lax.rsqrt(jnp.mean(o * o, axis=-1, keepdims=True) + EPS) * gsub_ref[...]
            o_ref[rows, hsl[hh]] = (o * (1.0 - lam_init)).astype(o_ref.dtype)
        return outer

    lax.fori_loop(0, n_q, query_block, 0)


def _diff_attention(proj_qk, proj_plain, rel_bias, lam_params, g_subln, bsz, seq, n_heads, d_model,
                    lam_init):
    t = bsz * seq
    assert ATT_BQ == ATT_BK and MAX_DISTANCE <= LANES and n_heads % ATT_HG == 0 and seq % ATT_BQ == 0
    width = ATT_HG * HEAD_DIM
    per = d_model // width
    vcol = 4 * per
    return pl.pallas_call(
        functools.partial(_attn_kernel, lam_init=lam_init),
        grid=(n_heads // ATT_HG, bsz),
        in_specs=[
            pl.BlockSpec(memory_space=pltpu.SMEM),
            pl.BlockSpec((seq, width), lambda h, b: (b, h)),
            pl.BlockSpec((seq, width), lambda h, b: (b, per + h)),
            pl.BlockSpec((seq, width), lambda h, b: (b, vcol + h)),
            pl.BlockSpec((4, DH_DIFF), lambda h, b: (0, 0)),
            pl.BlockSpec((1, HEAD_DIM), lambda h, b: (0, 0)),
        ],
        out_specs=pl.BlockSpec((seq, width), lambda h, b: (b, h)),
        out_shape=jax.ShapeDtypeStruct((t, d_model), BF16),
        scratch_shapes=[
            pltpu.VMEM((ATT_HG, 3, ATT_BQ, ATT_BK), F32),
            pltpu.VMEM((ATT_HG, 2 * ATT_BQ, 1), F32),
            pltpu.VMEM((ATT_HG, 2 * ATT_BQ, 2 * HEAD_DIM), F32),
            pltpu.VMEM((ATT_HG, 2 * ATT_BQ, ATT_BK), F32),
            pltpu.VMEM((ATT_HG, 2 * ATT_BQ, ATT_BK), F32),
            pltpu.VMEM((ATT_HG, 2 * ATT_BQ, HEAD_DIM), BF16),
        ],
        compiler_params=_cparams(("arbitrary", "arbitrary")),
        name="diff_attention",
    )(rel_bias, proj_qk, proj_qk, proj_plain, lam_params, g_subln)


def _mix_kernel(ga_ref, gb_ref, oa_ref, od_ref, x_ref, wo_ref, gffn_ref, wr_ref, br_ref,
                x1_ref, h2_ref, topi_ref, topw_ref, rank_ref, cnt_ref, carry_ref):
    i = pl.program_id(0)
    tm = MIX_TM

    @pl.when(i == 0)
    def _():
        carry_ref[...] = jnp.zeros_like(carry_ref)

    tp = tm // MIX_PARTS
    parts = range(MIX_PARTS)
    rows = [slice(pp * tp, (pp + 1) * tp) for pp in parts]
    mixes = [ga_ref[rs, :] * oa_ref[rs, :] + gb_ref[rs, :] * od_ref[rs, :] for rs in rows]
    x1s = [x_ref[rs, :] + jnp.dot(mx, wo_ref[...], preferred_element_type=F32) for rs, mx in zip(rows, mixes)]
    for rs, x1 in zip(rows, x1s):
        x1_ref[rs, :] = x1
    h2s = [x1 * lax.rsqrt(jnp.mean(x1 * x1, axis=-1, keepdims=True) + EPS) * gffn_ref[...] for x1 in x1s]
    for rs, h2 in zip(rows, h2s):
        h2_ref[rs, :] = _pack_halves(h2)

    curs = [lax.dot_general(wr_ref[...], h2, (((1,), (1,)), ((), ())), preferred_element_type=F32,
                            precision=lax.Precision.HIGHEST) + br_ref[...] for h2 in h2s]
    eidx = lax.broadcasted_iota(jnp.int32, curs[0].shape, 0).astype(F32)
    vals = [[] for _ in parts]
    hots = [[] for _ in parts]
    for kk in range(TOP_K):
        mxs = [jnp.max(cur, axis=0, keepdims=True) for cur in curs]
        idxs = [jnp.min(jnp.where(cur == mx, eidx, float(N_EXPERTS)), axis=0, keepdims=True)
                for cur, mx in zip(curs, mxs)]
        for pp in parts:
            hot = eidx == idxs[pp]
            vals[pp].append(mxs[pp])
            hots[pp].append(hot)
            topi_ref[kk:kk + 1, rows[pp]] = idxs[pp].astype(jnp.int32)
            curs[pp] = jnp.where(hot, -jnp.inf, curs[pp])
    for pp in parts:
        exps = [jnp.exp(vv - vals[pp][0]) for vv in vals[pp]]
        denom = exps[0] + exps[1] + exps[2] + exps[3]
        for kk in range(TOP_K):
            topw_ref[kk:kk + 1, rows[pp]] = exps[kk] / denom

    r = lax.broadcasted_iota(jnp.int32, (tp, tp), 0)
    c = lax.broadcasted_iota(jnp.int32, (tp, tp), 1)
    earlier = (r < c).astype(F32)
    sel_fs = []
    for pp in parts:
        sel = hots[pp][0]
        for kk in range(1, TOP_K):
            sel = jnp.logical_or(sel, hots[pp][kk])
        sel_fs.append(sel.astype(F32))
    within = [_bdot(sf, earlier) for sf in sel_fs]
    totals = [jnp.sum(sf, axis=-1, keepdims=True) for sf in sel_fs]
    run = carry_ref[...]
    for pp in parts:
        before = within[pp] + run
        for kk in range(TOP_K):
            rank_ref[kk:kk + 1, rows[pp]] = jnp.sum(jnp.where(hots[pp][kk], before, 0.0), axis=0,
                                                    keepdims=True).astype(jnp.int32)
        run = run + totals[pp]
    carry_ref[...] = run
    cnt_ref[...] = run.astype(jnp.int32)


def _mix_project_route(proj_gate, oa, od, x2d, w_o, g_ffn, w_r_t, b_r, d_model):
    t = x2d.shape[0]
    tm = MIX_TM
    full = lambda shape: pl.BlockSpec(shape, lambda i: (0, 0))
    row = lambda: pl.BlockSpec((tm, d_model), lambda i: (i, 0))
    krow = lambda: pl.BlockSpec((TOP_K, tm), lambda i: (0, i))
    return pl.pallas_call(
        _mix_kernel,
        grid=(t // tm,),
        in_specs=[
            pl.BlockSpec((tm, d_model), lambda i: (i, 0)),
            pl.BlockSpec((tm, d_model), lambda i: (i, 1)),
            row(), row(), row(),
            full((d_model, d_model)), full((1, d_model)), full((N_EXPERTS, d_model)), full((N_EXPERTS, 1)),
        ],
        out_specs=[row(), pl.BlockSpec((tm, d_model // 2), lambda i: (i, 0)),
                   krow(), krow(), krow(), full((N_EXPERTS, 1))],
        out_shape=[
            jax.ShapeDtypeStruct((t, d_model), F32),
            jax.ShapeDtypeStruct((t, d_model // 2), jnp.int32),
            jax.ShapeDtypeStruct((TOP_K, t), jnp.int32),
            jax.ShapeDtypeStruct((TOP_K, t), F32),
            jax.ShapeDtypeStruct((TOP_K, t), jnp.int32),
            jax.ShapeDtypeStruct((N_EXPERTS, 1), jnp.int32),
        ],
        scratch_shapes=[pltpu.VMEM((N_EXPERTS, 1), F32)],
        compiler_params=_cparams(("arbitrary",)),
        name="merge_outproj_route",
    )(proj_gate, proj_gate, oa, od, x2d, w_o, g_ffn, w_r_t, b_r)


def _pack_halves(x):
    half = x.shape[1] // 2
    bits = pltpu.bitcast(x.astype(BF16).astype(F32), jnp.int32)
    return bits[:, :half] | lax.shift_right_logical(bits[:, half:], 16)


def _unpack_halves(p):
    hi = pltpu.bitcast(p & jnp.int32(-65536), F32)
    lo = pltpu.bitcast(lax.shift_left(p, 16), F32)
    return jnp.concatenate([hi, lo], axis=1)


def _expert_kernel(be_ref, nu_ref, x_ref, wup_ref, bup_ref, wdn_ref, bdn_ref, *rest):
    y_ref, wup_bf, wdn_bf = rest[-3:]
    i = pl.program_id(0)
    d_ff = wdn_ref.shape[1]

    @pl.when(jnp.logical_or(i == 0, be_ref[i] != be_ref[jnp.maximum(i - 1, 0)]))
    def _():
        rr = lax.broadcasted_iota(jnp.int32, (2 * LANES, 2 * LANES), 0)
        cc = lax.broadcasted_iota(jnp.int32, (2 * LANES, 2 * LANES), 1)
        pick = jnp.where(cc < LANES, 2 * cc, 2 * (cc - LANES) + 1)
        perm = (rr == pick).astype(BF16)
        for g in range(wup_ref.shape[2] // (2 * LANES)):
            cs = slice(g * 2 * LANES, (g + 1) * 2 * LANES)
            wup_bf[:, cs] = jnp.dot(wup_ref[0, :, cs].astype(BF16), perm,
                                    preferred_element_type=F32).astype(BF16)
        wdn_bf[...] = wdn_ref[0].astype(BF16)

    @pl.when(i < nu_ref[0])
    def _():
        x = _unpack_halves(x_ref[...])
        hid = jnp.dot(x.astype(BF16), wup_bf[...], preferred_element_type=F32) + bup_ref[0]
        acts = []
        for g in range(hid.shape[1] // (2 * LANES)):
            glu = jnp.minimum(hid[:, g * 2 * LANES:g * 2 * LANES + LANES], SWIGLU_LIMIT)
            lin = jnp.clip(hid[:, g * 2 * LANES + LANES:(g + 1) * 2 * LANES], -SWIGLU_LIMIT, SWIGLU_LIMIT)
            acts.append(glu * _sigmoid(SWIGLU_ALPHA * glu) * (lin + 1.0))
        act = jnp.concatenate(acts, axis=1)
        assert act.shape[1] == d_ff
        y = jnp.dot(act.astype(BF16), wdn_bf[...], preferred_element_type=F32) + bdn_ref[0]
        y_ref[...] = _pack_halves(y)

    @pl.when(i >= nu_ref[0])
    def _():
        y_ref[...] = jnp.zeros(y_ref.shape, y_ref.dtype)


def _experts(block_e, n_used, xs, y_prev, first_block, n_rows_total, w_up, b_up, w_down, b_down):
    n_rows, half = xs.shape
    d = w_up.shape[1]
    nb = n_rows // MOE_RB
    two_ff = w_up.shape[2]
    d_ff = w_down.shape[1]
    in_specs = [
        pl.BlockSpec((MOE_RB, half), lambda i, be, nu: (jnp.maximum(jnp.minimum(i, nu[0] - 1), 0), 0)),
        pl.BlockSpec((1, d, two_ff), lambda i, be, nu: (be[i], 0, 0)),
        pl.BlockSpec((1, 1, two_ff), lambda i, be, nu: (be[i], 0, 0)),
        pl.BlockSpec((1, d_ff, d), lambda i, be, nu: (be[i], 0, 0)),
        pl.BlockSpec((1, 1, d), lambda i, be, nu: (be[i], 0, 0)),
    ]
    operands = [block_e, n_used, xs, w_up, b_up, w_down, b_down]
    aliases = {}
    if y_prev is not None:
        in_specs.append(pl.BlockSpec(memory_space=pl.ANY))
        aliases = {len(operands): 0}
        operands.append(y_prev)
    grid_spec = pltpu.PrefetchScalarGridSpec(
        num_scalar_prefetch=2,
        grid=(nb,),
        in_specs=in_specs,
        out_specs=pl.BlockSpec((MOE_RB, half), lambda i, be, nu: (first_block + i, 0)),
        scratch_shapes=[pltpu.VMEM((d, two_ff), BF16), pltpu.VMEM((d_ff, d), BF16)],
    )
    return pl.pallas_call(
        _expert_kernel,
        grid_spec=grid_spec,
        out_shape=jax.ShapeDtypeStruct((n_rows_total, half), jnp.int32),
        input_output_aliases=aliases,
        compiler_params=_cparams(("arbitrary",)),
        name="moe_experts",
    )(*operands)


def _sc_invert_slots(dest_flat, n_rows):
    n_assign = dest_flat.shape[0]
    n_workers = SC_CORES * SC_SUBCORES
    rows_per_w = n_rows // n_workers
    chunk = SC_SCAN_CHUNK
    assert n_rows % n_workers == 0 and rows_per_w % SC_LANES == 0 and n_assign % chunk == 0
    mesh = plsc.VectorSubcoreMesh(core_axis_name="c", subcore_axis_name="s",
                                  num_cores=SC_CORES, num_subcores=SC_SUBCORES)

    def body(dest_hbm, out_hbm, dest_v, map_v):
        wid = lax.axis_index("s") * SC_CORES + lax.axis_index("c")
        base = wid * rows_per_w
        lanes = lax.broadcasted_iota(jnp.int32, (SC_LANES,), 0)

        @pl.loop(0, rows_per_w, step=SC_LANES)
        def _(r0):
            map_v[pl.ds(r0, SC_LANES)] = jnp.full((SC_LANES,), -1, jnp.int32)

        @pl.loop(0, n_assign // chunk)
        def _(ci):
            pltpu.sync_copy(dest_hbm.at[pl.ds(ci * chunk, chunk)], dest_v)

            @pl.loop(0, chunk, step=SC_LANES)
            def _(j):
                local = dest_v[pl.ds(j, SC_LANES)] - base
                mine = jnp.logical_and(local >= 0, local < rows_per_w)
                plsc.store_scatter(map_v, [jnp.where(mine, local, 0)], ci * chunk + j + lanes, mask=mine)

        pltpu.sync_copy(map_v, out_hbm.at[pl.ds(base, rows_per_w)])

    return pl.kernel(
        body,
        out_type=jax.ShapeDtypeStruct((n_rows,), jnp.int32),
        mesh=mesh,
        scratch_types=[pltpu.VMEM((chunk,), jnp.int32), pltpu.VMEM((rows_per_w,), jnp.int32)],
        compiler_params=pltpu.CompilerParams(needs_layout_passes=False),
        name="moe_slot_inverse",
    )(dest_flat)


def _sc_gather_rows(table, idx):
    n_idx = idx.shape[0]
    d = table.shape[1]
    n_workers = SC_CORES * SC_SUBCORES
    per_worker = n_idx // n_workers
    n_chunks = per_worker // SC_GATHER_ROWS
    assert n_idx % n_workers == 0 and per_worker % SC_GATHER_ROWS == 0
    mesh = plsc.VectorSubcoreMesh(core_axis_name="c", subcore_axis_name="s",
                                  num_cores=SC_CORES, num_subcores=SC_SUBCORES)

    assert n_chunks % 2 == 0

    def body(table_hbm, idx_hbm, out_hbm, idx_v, rows_a, rows_b, sem_a, sem_b):
        wid = lax.axis_index("s") * SC_CORES + lax.axis_index("c")
        base = wid * per_worker
        pltpu.sync_copy(idx_hbm.at[pl.ds(base, per_worker)], idx_v)

        def gather(ci, rows_v, sem):
            off = pl.multiple_of(ci * SC_GATHER_ROWS, SC_GATHER_ROWS)
            return pltpu.make_async_copy(table_hbm.at[idx_v.at[pl.ds(off, SC_GATHER_ROWS)]], rows_v, sem)

        def put(ci, rows_v):
            off = pl.multiple_of(ci * SC_GATHER_ROWS, SC_GATHER_ROWS)
            pltpu.sync_copy(rows_v, out_hbm.at[pl.ds(base + off, SC_GATHER_ROWS)])

        gather(0, rows_a, sem_a).start()

        @pl.loop(0, n_chunks, step=2)
        def _(ci):
            gather(ci + 1, rows_b, sem_b).start()
            gather(ci, rows_a, sem_a).wait()
            put(ci, rows_a)
            nxt = jnp.minimum(ci + 2, n_chunks - 1)
            gather(nxt, rows_a, sem_a).start()
            gather(ci + 1, rows_b, sem_b).wait()
            put(ci + 1, rows_b)

        gather(n_chunks - 1, rows_a, sem_a).wait()

    return pl.kernel(
        body,
        out_type=jax.ShapeDtypeStruct((n_idx, d), table.dtype),
        mesh=mesh,
        scratch_types=[
            pltpu.VMEM((per_worker,), jnp.int32),
            pltpu.VMEM((SC_GATHER_ROWS, d), table.dtype),
            pltpu.VMEM((SC_GATHER_ROWS, d), table.dtype),
            pltpu.SemaphoreType.DMA,
            pltpu.SemaphoreType.DMA,
        ],
        name="moe_slot_gather",
    )(table, idx)


def _combine_kernel(x1_ref, w_ref, y0_ref, y1_ref, y2_ref, y3_ref, o_ref):
    w = w_ref[...]
    out = x1_ref[...]
    for kk, y_ref in enumerate((y0_ref, y1_ref, y2_ref, y3_ref)):
        out = out + w[:, kk:kk + 1] * _unpack_halves(y_ref[...])
    o_ref[...] = out


def _combine(x1, w_tok, y_slots):
    t, d = x1.shape
    tc = COMB_TC
    nt = t // tc
    yspec = lambda kk: pl.BlockSpec((tc, d // 2), lambda i: (kk * nt + i, 0))
    return pl.pallas_call(
        _combine_kernel,
        grid=(nt,),
        in_specs=[
            pl.BlockSpec((tc, d), lambda i: (i, 0)),
            pl.BlockSpec((tc, TOP_K), lambda i: (i, 0)),
            yspec(0), yspec(1), yspec(2), yspec(3),
        ],
        out_specs=pl.BlockSpec((tc, d), lambda i: (i, 0)),
        out_shape=jax.ShapeDtypeStruct((t, d), F32),
        compiler_params=_cparams(("parallel",)),
        name="moe_combine",
    )(x1, w_tok, y_slots, y_slots, y_slots, y_slots)


def _moe(x1, h2, topi, topw, rank, counts, w_up, b_up, w_down, b_down):
    t, d = x1.shape
    n_assign = t * TOP_K
    nb = -(-n_assign // MOE_RB) + N_EXPERTS
    n_rows = nb * MOE_RB
    counts = counts[:, 0]
    padded = (counts + MOE_RB - 1) // MOE_RB * MOE_RB
    padded_end = jnp.cumsum(padded)
    padded_start = padded_end - padded
    expert_ids = jnp.arange(N_EXPERTS, dtype=jnp.int32)[:, None, None]
    start_of = jnp.sum(jnp.where(topi[None] == expert_ids, padded_start[:, None, None], 0), axis=0)
    dest = (start_of + rank).astype(jnp.int32)
    n_used = (padded_end[-1] // MOE_RB).astype(jnp.int32)
    blk = jnp.minimum(jnp.arange(nb, dtype=jnp.int32), n_used - 1)
    block_e = jnp.minimum(jnp.sum(padded_end[None, :] <= (blk * MOE_RB)[:, None], axis=1),
                          N_EXPERTS - 1).astype(jnp.int32)
    slot_of = _sc_invert_slots(dest.reshape(-1), n_rows)
    src_tok = jnp.where(slot_of < 0, jnp.arange(n_rows, dtype=jnp.int32), slot_of) % t

    nb_a = nb // 4
    y_rows = None
    for first, n_blk in ((0, nb_a), (nb_a, nb - nb_a)):
        xs = _sc_gather_rows(h2, lax.slice(src_tok, (first * MOE_RB,), ((first + n_blk) * MOE_RB,)))
        used = jnp.clip(n_used - first, 0, n_blk).reshape(1)
        y_rows = _experts(lax.slice(block_e, (first,), (first + n_blk,)), used, xs, y_rows, first, n_rows,
                          w_up, b_up, w_down, b_down)
    y_slots = _sc_gather_rows(y_rows, dest.reshape(-1))
    return _combine(x1, topw.T, y_slots)


def kernel(x, g_mix, w_in, b_gate, conv_w, a_log, dt_bias, g_delta_out, q_norm, k_norm, lambda_q1, lambda_k1, lambda_q2, lambda_k2, g_subln, rel_bias, w_o, g_ffn, w_router, b_router, w_up, b_up, w_down, b_down):
    bsz, seq, d = x.shape
    depth = g_mix.shape[0]
    n_heads = d // HEAD_DIM
    t = bsz * seq
    d_ff = w_down.shape[2]
    assert d % PROJ_TN == 0 and t % PROJ_TM == 0 and seq % GDN_TB == 0 and seq % ATT_BQ == 0
    assert t % MIX_TM == 0 and t % COMB_TC == 0 and n_heads % GDN_HG == 0
    assert (t * TOP_K) % MOE_RB == 0
    assert 2 * n_heads <= 2 * SUBLANES

    x2d = x.reshape(t, d)
    for l in range(depth):
        wl = w_in[l]
        c0 = 4 * d
        c1 = c0 + 2 * n_heads
        c2 = c1 + 2 * d
        c3 = c2 + d
        w_small = jnp.pad(wl[:, c0:c1], ((0, 0), (0, LANES - 2 * n_heads)))
        gm = g_mix[l].reshape(1, d)
        head_pad = jnp.zeros((LANES - 2 * n_heads,), F32)
        alog = jnp.concatenate([jnp.zeros((n_heads,), F32), a_log[l], head_pad])
        dtb = jnp.concatenate([jnp.zeros((n_heads,), F32), dt_bias[l], head_pad])
        rows_t = 2 * n_heads
        beta_decay = (w_small.astype(BF16), w_small[:, :rows_t].T.astype(BF16),
                      alog.reshape(1, LANES), dtb.reshape(1, LANES),
                      alog[:rows_t].reshape(rows_t, 1), dtb[:rows_t].reshape(rows_t, 1), n_heads)
        w_all = jnp.concatenate([wl[:, :c0], wl[:, c2:c3], wl[:, c1:c2], wl[:, c3:]], axis=1).astype(BF16)
        aux = jnp.concatenate([jnp.zeros((5 * d,), F32),
                               jnp.tile(q_norm[l] * (DH_DIFF ** -0.5), 2 * n_heads),
                               jnp.tile(k_norm[l], 2 * n_heads), b_gate[l].reshape(2 * d)]).reshape(1, 9 * d)
        proj_plain, proj_qk, proj_gate, small, small_t = _input_projection(
            x2d, gm, w_all, aux, 5 * d // PROJ_TN, 2 * d // PROJ_TN, 2 * d // PROJ_TN, beta_decay)

        oa = _gated_delta(proj_plain, small, small_t, conv_w[l], g_delta_out[l].reshape(1, HEAD_DIM),
                          bsz, seq, n_heads, d)

        lam_init = 0.8 - 0.6 * math.exp(-0.3 * l)
        lam_params = jnp.stack([lambda_q1[l], lambda_k1[l], lambda_q2[l], lambda_k2[l]])
        od = _diff_attention(proj_qk, proj_plain, rel_bias, lam_params, g_subln[l].reshape(1, HEAD_DIM),
                             bsz, seq, n_heads, d, lam_init)

        x1, h2, topi, topw, rank, counts = _mix_project_route(
            proj_gate, oa, od, x2d, w_o[l].astype(BF16), g_ffn[l].reshape(1, d),
            w_router[l].T, b_router[l].reshape(N_EXPERTS, 1), d)

        b_up_l = b_up[l].reshape(N_EXPERTS, 2 * d_ff // (2 * LANES), LANES, 2)
        b_up_l = jnp.swapaxes(b_up_l, 2, 3).reshape(N_EXPERTS, 1, 2 * d_ff)
        x2d = _moe(x1, h2, topi, topw, rank, counts, w_up[l], b_up_l,
                   w_down[l], b_down[l].reshape(N_EXPERTS, 1, d))
    return x2d.reshape(bsz, seq, d)
```

```python
import functools
import math

import jax
import jax.numpy as jnp
from jax import lax
from jax.experimental import pallas as pl
from jax.experimental.pallas import tpu as pltpu
from jax.experimental.pallas import tpu_sc as plsc

F32 = jnp.float32
BF16 = jnp.bfloat16

HEAD_DIM = 128
DH_DIFF = HEAD_DIM // 2
CONV_WIDTH = 4
CHUNK = 64
N_BUCKETS = 32
MAX_DISTANCE = 128
N_EXPERTS = 32
TOP_K = 4
SWIGLU_LIMIT = 7.0
SWIGLU_ALPHA = 1.702
EPS = 1e-6
NEG_BIG = -1e30

LANES = 128
SUBLANES = 8
VMEM_LIMIT = 56 * 1024 * 1024
SC_CORES = 2
SC_SUBCORES = 16
SC_LANES = 16
SC_GATHER_ROWS = 64
SC_SCAN_CHUNK = 4096

PROJ_TM = 2048
PROJ_TN = 1024
PROJ_CHUNK = 256
GDN_TB = 256
GDN_HG = 8
ATT_HG = 1
ATT_BQ = 512
ATT_BK = 512
MIX_TM = 1024
MIX_PARTS = 2
MOE_RB = 512
COMB_TC = 512


def _cparams(sem):
    return pltpu.CompilerParams(dimension_semantics=sem, vmem_limit_bytes=VMEM_LIMIT)


def _sigmoid(x):
    return 0.5 * jnp.tanh(0.5 * x) + 0.5


def _bdot(a, b):
    return jnp.dot(a.astype(BF16), b.astype(BF16), preferred_element_type=F32)


def _bdot_nt(a, b):
    return lax.dot_general(a.astype(BF16), b.astype(BF16), (((1,), (1,)), ((), ())),
                           preferred_element_type=F32)


def _bdot_tn(a, b):
    return lax.dot_general(a.astype(BF16), b.astype(BF16), (((0,), (0,)), ((), ())),
                           preferred_element_type=F32)


def _beta_decay(acc, idx, alog, dtb, n_heads):
    beta = _sigmoid(acc)
    z = acc + dtb
    softplus = jnp.maximum(z, 0.0) + jnp.log1p(jnp.exp(-jnp.abs(z)))
    gdec = -jnp.exp(alog) * softplus
    return jnp.where(idx < n_heads, beta, jnp.where(idx < 2 * n_heads, gdec, 0.0))


def _proj_kernel(x_ref, g_ref, w_ref, aux_ref, *rest, mode, n_heads):
    o_ref, h_ref = rest[-2:] if n_heads is None else (rest[6], rest[-1])

    @pl.when(pl.program_id(1) == 0)
    def _():
        x = x_ref[...]
        ms = jnp.mean(x * x, axis=-1, keepdims=True)
        h_ref[...] = (x * lax.rsqrt(ms + EPS) * g_ref[...]).astype(BF16)
        if n_heads is not None:
            ws_ref, wst_ref, alog_ref, dtb_ref, alog_t_ref, dtb_t_ref, _, os_ref, ost_ref, _ = rest
            hb = h_ref[...]
            acc = jnp.dot(hb, ws_ref[...], preferred_element_type=F32)
            lane = lax.broadcasted_iota(jnp.int32, acc.shape, 1)
            os_ref[...] = _beta_decay(acc, lane, alog_ref[...], dtb_ref[...], n_heads)
            acc_t = lax.dot_general(wst_ref[...], hb, (((1,), (1,)), ((), ())),
                                    preferred_element_type=F32)
            sub = lax.broadcasted_iota(jnp.int32, acc_t.shape, 0)
            ost_ref[...] = _beta_decay(acc_t, sub, alog_t_ref[...], dtb_t_ref[...], n_heads)

    h = h_ref[...]
    lo = lax.broadcasted_iota(jnp.int32, (1, LANES), 1) < DH_DIFF
    for c in range(PROJ_TN // PROJ_CHUNK):
        cs = slice(c * PROJ_CHUNK, (c + 1) * PROJ_CHUNK)
        acc = jnp.dot(h, w_ref[:, cs], preferred_element_type=F32)
        if mode == "plain":
            o_ref[:, cs] = acc.astype(o_ref.dtype)
        elif mode == "gate":
            o_ref[:, cs] = _sigmoid(acc + aux_ref[:, cs]).astype(o_ref.dtype)
        else:
            for g in range(PROJ_CHUNK // LANES):
                sl = slice(c * PROJ_CHUNK + g * LANES, c * PROJ_CHUNK + (g + 1) * LANES)
                y = acc[:, g * LANES:(g + 1) * LANES]
                y2 = y * y
                s_lo = jnp.sum(jnp.where(lo, y2, 0.0), axis=-1, keepdims=True)
                s_hi = jnp.sum(jnp.where(lo, 0.0, y2), axis=-1, keepdims=True)
                r = jnp.where(lo, lax.rsqrt(s_lo / DH_DIFF + EPS), lax.rsqrt(s_hi / DH_DIFF + EPS))
                o_ref[:, sl] = (y * r * aux_ref[:, sl]).astype(o_ref.dtype)


def _input_projection(x2d, g_mix, w, aux, mode, beta_decay=None):
    t, d = x2d.shape
    n = w.shape[1]
    full = lambda shape: pl.BlockSpec(shape, lambda i, j: (0, 0))
    in_specs = [
        pl.BlockSpec((PROJ_TM, d), lambda i, j: (i, 0)),
        full((1, d)),
        pl.BlockSpec((d, PROJ_TN), lambda i, j: (0, j)),
        pl.BlockSpec((1, PROJ_TN), lambda i, j: (0, j)),
    ]
    operands = [x2d, g_mix, w, aux]
    out_specs = [pl.BlockSpec((PROJ_TM, PROJ_TN), lambda i, j: (i, j))]
    out_shape = [jax.ShapeDtypeStruct((t, n), BF16)]
    n_heads = None
    if beta_decay is not None:
        n_heads = beta_decay[-1]
        rows_t = 2 * n_heads
        in_specs += [full((d, LANES)), full((rows_t, d)), full((1, LANES)), full((1, LANES)),
                     full((rows_t, 1)), full((rows_t, 1))]
        operands += list(beta_decay[:-1])
        out_specs += [pl.BlockSpec((PROJ_TM, LANES), lambda i, j: (i, 0)),
                      pl.BlockSpec((rows_t, PROJ_TM), lambda i, j: (0, i))]
        out_shape += [jax.ShapeDtypeStruct((t, LANES), F32), jax.ShapeDtypeStruct((rows_t, t), F32)]
    out = pl.pallas_call(
        functools.partial(_proj_kernel, mode=mode, n_heads=n_heads),
        grid=(t // PROJ_TM, n // PROJ_TN),
        in_specs=in_specs,
        out_specs=out_specs,
        out_shape=out_shape,
        scratch_shapes=[pltpu.VMEM((PROJ_TM, d), BF16)],
        compiler_params=_cparams(("parallel", "arbitrary")),
        name="input_projection_" + mode,
    )(*operands)
    return out[0] if beta_decay is None else out


def _gdn_kernel(q_ref, k_ref, v_ref, z_ref, sm_ref, smt_ref, cwq_ref, cwk_ref, cwv_ref, gout_ref,
                o_ref, state_ref, qp_ref, kp_ref, vp_ref, vn_ref, *, n_heads):
    hg = pl.program_id(1)
    s = pl.program_id(2)
    tb = GDN_TB
    pad = SUBLANES
    width = GDN_HG * HEAD_DIM

    @pl.when(s == 0)
    def _():
        state_ref[...] = jnp.zeros_like(state_ref)
        for p_ref in (qp_ref, kp_ref, vp_ref):
            p_ref[0:pad, :] = jnp.zeros((pad, width), F32)

    r = lax.broadcasted_iota(jnp.int32, (tb, tb), 0)
    c = lax.broadcasted_iota(jnp.int32, (tb, tb), 1)
    delay_mat = jnp.concatenate([(r - c == dd).astype(BF16) for dd in range(1, CONV_WIDTH)], axis=0)

    def conv_silu(x_ref, p_ref, cw_ref):
        x = x_ref[...]
        xf = x.astype(F32)
        p_ref[pad:2 * pad, :] = xf[0:pad]
        delayed = jnp.dot(delay_mat, x, preferred_element_type=F32)
        acc = cw_ref[CONV_WIDTH - 1:CONV_WIDTH, :] * xf
        for dd in range(1, CONV_WIDTH):
            first = p_ref[pad - dd:2 * pad - dd, :]
            xd = jnp.concatenate([first, delayed[(dd - 1) * tb + pad:dd * tb]], axis=0)
            acc = acc + cw_ref[CONV_WIDTH - 1 - dd:CONV_WIDTH - dd, :] * xd
        p_ref[0:pad, :] = xf[tb - pad:tb]
        return acc * _sigmoid(acc)

    q_all = conv_silu(q_ref, qp_ref, cwq_ref)
    k_all = conv_silu(k_ref, kp_ref, cwk_ref)
    v_all = conv_silu(v_ref, vp_ref, cwv_ref)

    shift = int(math.log2(CHUNK))
    same = (r >> shift) == (c >> shift)
    incl = jnp.logical_and(same, c <= r)
    strict = jnp.logical_and(same, c < r)

    small = sm_ref[...]
    small_t = smt_ref[...]
    lane = lax.broadcasted_iota(jnp.int32, small.shape, 1)
    def split3(a):
        hi = a.astype(BF16)
        r1 = a - hi.astype(F32)
        mid = r1.astype(BF16)
        lo = (r1 - mid.astype(F32)).astype(BF16)
        return hi.astype(F32), mid.astype(F32), lo.astype(F32)

    part = 2 * n_heads
    s_hi, s_mid, s_lo = split3(small)
    small3 = jnp.where(lane < part, s_hi,
                       jnp.where(lane < 2 * part, pltpu.roll(s_mid, part, 1),
                                 jnp.where(lane < 3 * part, pltpu.roll(s_lo, 2 * part, 1), 0.0)))
    both = _bdot(jnp.concatenate([incl.astype(F32), same.astype(F32)], axis=0), small3)
    gcum = both[:tb]
    gtot = both[tb:]
    gcum_t = _bdot(jnp.concatenate(split3(small_t), axis=0),
                   jnp.logical_and(same, r <= c).astype(F32))
    sub3 = lax.broadcasted_iota(jnp.int32, gcum_t.shape, 0)

    heads = range(GDN_HG)
    hsl = [slice(hh * HEAD_DIM, (hh + 1) * HEAD_DIM) for hh in heads]
    qs = [q_all[:, hs] for hs in hsl]
    ks = [k_all[:, hs] for hs in hsl]
    vs = [v_all[:, hs] for hs in hsl]
    qs = [q * lax.rsqrt(jnp.sum(q * q, axis=-1, keepdims=True) + EPS) * (HEAD_DIM ** -0.5) for q in qs]
    ks = [k * lax.rsqrt(jnp.sum(k * k, axis=-1, keepdims=True) + EPS) for k in ks]

    def col_of(arr, idx):
        return jnp.sum(jnp.where(lane == idx, arr, 0.0), axis=-1, keepdims=True)

    def terms_of(pos, idx):
        return jnp.logical_or(pos == idx, jnp.logical_or(pos == idx + part, pos == idx + 2 * part))

    head_ids = [hg * GDN_HG + hh for hh in heads]
    betas = [col_of(small, hd) for hd in head_ids]
    gcs = [jnp.sum(jnp.where(terms_of(lane, hd + n_heads), gcum, 0.0), axis=-1, keepdims=True)
           for hd in head_ids]
    gls = [jnp.sum(jnp.where(terms_of(lane, hd + n_heads), gtot, 0.0), axis=-1, keepdims=True)
           for hd in head_ids]
    gc_rows = [jnp.sum(jnp.where(terms_of(sub3, hd + n_heads), gcum_t, 0.0), axis=0, keepdims=True)
               for hd in head_ids]

    decays = [jnp.where(incl, jnp.exp(jnp.minimum(gc - gr, 0.0)), 0.0) for gc, gr in zip(gcs, gc_rows)]
    kbs = [k * b for k, b in zip(ks, betas)]
    kks = [_bdot_nt(kb, k) for kb, k in zip(kbs, ks)]
    pws = [jnp.where(strict, -(kk * dc), 0.0) for kk, dc in zip(kks, decays)]
    n_chunks = tb // CHUNK
    cat_row = lax.broadcasted_iota(jnp.int32, (CHUNK, tb), 0)
    cat_lane = lax.broadcasted_iota(jnp.int32, (CHUNK, tb), 1)
    lane_chunk = cat_lane >> shift

    def block_diag(m_cat):
        return jnp.concatenate([jnp.where(lane_chunk == ci, m_cat, 0.0) for ci in range(n_chunks)], axis=0)

    def cat_of(m_bd):
        out = m_bd[0:CHUNK]
        for ci in range(1, n_chunks):
            out = out + m_bd[ci * CHUNK:(ci + 1) * CHUNK]
        return out

    pcats = [cat_of(pw) for pw in pws]
    eye_cat = ((cat_lane & (CHUNK - 1)) == cat_row).astype(F32)
    tcats = [eye_cat + pc for pc in pcats]
    pcats = [_bdot(pc, block_diag(pc)) for pc in pcats]
    n_levels = int(math.log2(CHUNK))
    for lev in range(1, n_levels):
        bds = [block_diag(pc) for pc in pcats]
        if lev < n_levels - 1:
            prods = [_bdot(jnp.concatenate([pc, tc], axis=0), bd) for pc, tc, bd in zip(pcats, tcats, bds)]
            pcats = [pr[:CHUNK] for pr in prods]
            tcats = [tc + pr[CHUNK:] for tc, pr in zip(tcats, prods)]
        else:
            tcats = [tc + _bdot(tc, bd) for tc, bd in zip(tcats, bds)]
    tmats = [block_diag(tc) for tc in tcats]
    egcs = [jnp.exp(gc) for gc in gcs]
    uws = [_bdot(tm, jnp.concatenate([v * b, kb * eg], axis=1))
           for tm, v, b, kb, eg in zip(tmats, vs, betas, kbs, egcs)]
    us = [uw[:, :HEAD_DIM] for uw in uws]
    ws = [uw[:, HEAD_DIM:] for uw in uws]
    qkm = [_bdot_nt(q, k) for q, k in zip(qs, ks)]
    qkm = [jnp.where(incl, x * dc, 0.0) for x, dc in zip(qkm, decays)]
    q_decs = [q * eg for q, eg in zip(qs, egcs)]
    k_ends = [k * jnp.exp(gl - gc) for k, gl, gc in zip(ks, gls, gcs)]

    for hh in heads:
        vn_ref[hh] = jnp.zeros((tb, HEAD_DIM), F32)
    outs = [[] for _ in heads]
    for ci in range(tb // CHUNK):
        cs = slice(ci * CHUNK, (ci + 1) * CHUNK)
        sts = [state_ref[hh] for hh in heads]
        ws_qs = [_bdot(jnp.concatenate([ws[hh][cs], q_decs[hh][cs]], axis=0), sts[hh]) for hh in heads]
        v_news = [us[hh][cs] - ws_qs[hh][:CHUNK] for hh in heads]
        for hh in heads:
            vn_ref[hh, cs, :] = v_news[hh]
        intra = [_bdot(qkm[hh][cs], vn_ref[hh]) for hh in heads]
        upd = [_bdot_tn(k_ends[hh][cs], v_news[hh]) for hh in heads]
        for hh in heads:
            outs[hh].append(ws_qs[hh][CHUNK:] + intra[hh])
            g_last = gls[hh][ci * CHUNK:ci * CHUNK + 1, :]
            state_ref[hh] = sts[hh] * jnp.exp(g_last) + upd[hh]
    for hh in heads:
        o = jnp.concatenate(outs[hh], axis=0)
        o = o * lax.rsqrt(jnp.mean(o * o, axis=-1, keepdims=True) + EPS) * gout_ref[...]
        zz = z_ref[:, hsl[hh]].astype(F32)
        o_ref[:, hsl[hh]] = (o * (zz * _sigmoid(zz))).astype(o_ref.dtype)


def _gated_delta(big, small, small_t, conv_w, g_out, bsz, seq, n_heads, d_model):
    t = bsz * seq
    tb = GDN_TB
    ns = seq // tb
    width = GDN_HG * HEAD_DIM
    nhg = n_heads // GDN_HG
    blocks_per_group = d_model // width
    rows_t = small_t.shape[0]

    def colspec(group):
        return pl.BlockSpec((tb, width), lambda b, h, s: (b * ns + s, group * blocks_per_group + h))

    def cwspec(group):
        return pl.BlockSpec((CONV_WIDTH, width), lambda b, h, s: (0, group * blocks_per_group + h))

    return pl.pallas_call(
        functools.partial(_gdn_kernel, n_heads=n_heads),
        grid=(bsz, nhg, ns),
        in_specs=[
            colspec(0), colspec(1), colspec(2), colspec(3),
            pl.BlockSpec((tb, LANES), lambda b, h, s: (b * ns + s, 0)),
            pl.BlockSpec((rows_t, tb), lambda b, h, s: (0, b * ns + s)),
            cwspec(0), cwspec(1), cwspec(2),
            pl.BlockSpec((1, HEAD_DIM), lambda b, h, s: (0, 0)),
        ],
        out_specs=pl.BlockSpec((tb, width), lambda b, h, s: (b * ns + s, h)),
        out_shape=jax.ShapeDtypeStruct((t, d_model), BF16),
        scratch_shapes=[
            pltpu.VMEM((GDN_HG, HEAD_DIM, HEAD_DIM), F32),
            pltpu.VMEM((2 * SUBLANES, width), F32),
            pltpu.VMEM((2 * SUBLANES, width), F32),
            pltpu.VMEM((2 * SUBLANES, width), F32),
            pltpu.VMEM((GDN_HG, tb, HEAD_DIM), F32),
        ],
        compiler_params=_cparams(("parallel", "parallel", "arbitrary")),
        name="gated_delta",
    )(big, big, big, big, small, small_t, conv_w, conv_w, conv_w, g_out)


def _t5_bucket(n):
    max_exact = N_BUCKETS // 2
    nf = jnp.maximum(n, 1).astype(F32)
    large = max_exact + (jnp.log(nf / max_exact) / math.log(MAX_DISTANCE / max_exact)
                         * (N_BUCKETS - max_exact)).astype(jnp.int32)
    large = jnp.minimum(large, N_BUCKETS - 1)
    return jnp.where(n < max_exact, n, large)


def _attn_kernel(rb_ref, q_ref, k_ref, v_ref, lam_ref, gsub_ref, o_ref,
                 bias_ref, m_ref, acc_ref, sa_ref, sb_ref, qs_ref, *, lam_init):
    hg = pl.program_id(0)
    b = pl.program_id(1)
    bq, bk = ATT_BQ, ATT_BK
    heads = range(ATT_HG)
    hsl = [slice(hh * HEAD_DIM, (hh + 1) * HEAD_DIM) for hh in heads]

    @pl.when(b == 0)
    def _():
        blk = LANES
        i = lax.broadcasted_iota(jnp.int32, (blk, blk), 0)
        jj = lax.broadcasted_iota(jnp.int32, (blk, blk), 1)
        for hh in heads:
            head = hg * ATT_HG + hh
            far = rb_ref[N_BUCKETS - 1, head]

            def toeplitz(offset):
                bucket = _t5_bucket(jnp.maximum(i - jj + offset, 0))
                out = jnp.zeros((blk, blk), F32)
                for cc in range(N_BUCKETS):
                    out = jnp.where(bucket == cc, rb_ref[cc, head] - far, out)
                return out

            on_diag = jnp.where(i >= jj, toeplitz(0), NEG_BIG)
            next_diag = toeplitz(blk)
            kinds = {0: on_diag, 1: next_diag}
            bias_ref[hh, 2] = jnp.zeros((bq, bk), F32)
            for slot in range(2):
                for rr in range(bq // blk):
                    for cc in range(bk // blk):
                        delta = rr - cc + slot * (bk // blk)
                        if delta < 0:
                            tile = jnp.full((blk, blk), NEG_BIG, F32)
                        else:
                            tile = kinds.get(delta, jnp.zeros((blk, blk), F32))
                        bias_ref[hh, slot, rr * blk:(rr + 1) * blk, cc * blk:(cc + 1) * blk] = tile

    lane = lax.broadcasted_iota(jnp.int32, (bq, HEAD_DIM), 1)
    ones_col = (lax.broadcasted_iota(jnp.int32, (bk, HEAD_DIM), 1) == 0).astype(BF16)
    lam_p = lam_ref[...]
    s1 = jnp.sum(lam_p[0:1] * lam_p[1:2], axis=-1, keepdims=True)
    s2 = jnp.sum(lam_p[2:3] * lam_p[3:4], axis=-1, keepdims=True)
    lam = jnp.exp(s1) - jnp.exp(s2) + lam_init

    n_q = q_ref.shape[0] // bq

    def stack_q(qb):
        rows = pl.ds(pl.multiple_of(qb * bq, bq), bq)
        for hh in heads:
            q = q_ref[rows, hsl[hh]]
            zero = jnp.zeros_like(q)
            qs_ref[hh, 0:bq, :] = jnp.where(lane < DH_DIFF, q, zero)
            qs_ref[hh, bq:2 * bq, :] = jnp.where(lane < DH_DIFF, zero, q)

    def scores(j, s_ref):
        ks = pl.multiple_of(j * bk, bk)
        for hh in heads:
            s_ref[hh] = lax.dot_general(qs_ref[hh], k_ref[pl.ds(ks, bk), hsl[hh]],
                                        (((1,), (1,)), ((), ())), preferred_element_type=F32)

    stack_q(0)
    scores(0, sa_ref)

    def query_block(qi, outer):
        rows = pl.ds(pl.multiple_of(qi * bq, bq), bq)
        m_ref[...] = jnp.full(m_ref.shape, NEG_BIG, F32)
        acc_ref[...] = jnp.zeros(acc_ref.shape, F32)

        def absorb(j, s_ref, biased=True):
            ks = pl.multiple_of(j * bk, bk)
            v_exts = [jnp.concatenate([v_ref[pl.ds(ks, bk), hs], ones_col], axis=1) for hs in hsl]
            if biased:
                slot = jnp.minimum(qi - j, 2)
                scs = [jnp.concatenate([s_ref[hh, 0:bq, :] + bias_ref[hh, slot],
                                        s_ref[hh, bq:2 * bq, :] + bias_ref[hh, slot]], axis=0)
                       for hh in heads]
            else:
                scs = [s_ref[hh] for hh in heads]
            m_olds = [m_ref[hh] for hh in heads]
            m_news = [jnp.maximum(mo, jnp.max(sc, axis=-1, keepdims=True)) for mo, sc in zip(m_olds, scs)]
            ps = [jnp.exp(sc - mn) for sc, mn in zip(scs, m_news)]
            pvs = [jnp.dot(p.astype(BF16), ve, preferred_element_type=F32) for p, ve in zip(ps, v_exts)]
            for hh in heads:
                acc_ref[hh] = jnp.exp(m_olds[hh] - m_news[hh]) * acc_ref[hh] + pvs[hh]
                m_ref[hh] = m_news[hh]

        n_tiles = qi + 1

        def pair_body(jj, carry, biased):
            j0 = 2 * jj
            scores(j0 + 1, sb_ref)
            absorb(j0, sa_ref, biased)
            scores(jnp.minimum(j0 + 2, qi), sa_ref)
            absorb(j0 + 1, sb_ref, biased)
            return carry

        n_far_pairs = jnp.maximum(qi - 1, 0) // 2
        lax.fori_loop(0, n_far_pairs, functools.partial(pair_body, biased=False), 0)
        lax.fori_loop(n_far_pairs, n_tiles // 2, functools.partial(pair_body, biased=True), 0)

        @pl.when(n_tiles % 2 == 1)
        def _():
            absorb(qi, sa_ref)

        stack_q(jnp.minimum(qi + 1, n_q - 1))
        scores(0, sa_ref)
        for hh in heads:
            acc = acc_ref[hh]
            num = acc[:, :HEAD_DIM]
            den = acc[:, HEAD_DIM:HEAD_DIM + 1]
            o = num[:bq] / den[:bq] - lam * (num[bq:] / den[bq:])
            o = o * lax.rsqrt(jnp.mean(o * o, axis=-1, keepdims=True) + EPS) * gsub_ref[...]
            o_ref[rows, hsl[hh]] = (o * (1.0 - lam_init)).astype(o_ref.dtype)
        return outer

    lax.fori_loop(0, n_q, query_block, 0)


def _diff_attention(proj_qk, proj_plain, rel_bias, lam_params, g_subln, bsz, seq, n_heads, d_model,
                    lam_init):
    t = bsz * seq
    assert ATT_BQ == ATT_BK and MAX_DISTANCE <= LANES and n_heads % ATT_HG == 0 and seq % ATT_BQ == 0
    width = ATT_HG * HEAD_DIM
    per = d_model // width
    vcol = 4 * per
    return pl.pallas_call(
        functools.partial(_attn_kernel, lam_init=lam_init),
        grid=(n_heads // ATT_HG, bsz),
        in_specs=[
            pl.BlockSpec(memory_space=pltpu.SMEM),
            pl.BlockSpec((seq, width), lambda h, b: (b, h)),
            pl.BlockSpec((seq, width), lambda h, b: (b, per + h)),
            pl.BlockSpec((seq, width), lambda h, b: (b, vcol + h)),
            pl.BlockSpec((4, DH_DIFF), lambda h, b: (0, 0)),
            pl.BlockSpec((1, HEAD_DIM), lambda h, b: (0, 0)),
        ],
        out_specs=pl.BlockSpec((seq, width), lambda h, b: (b, h)),
        out_shape=jax.ShapeDtypeStruct((t, d_model), BF16),
        scratch_shapes=[
            pltpu.VMEM((ATT_HG, 3, ATT_BQ, ATT_BK), F32),
            pltpu.VMEM((ATT_HG, 2 * ATT_BQ, 1), F32),
            pltpu.VMEM((ATT_HG, 2 * ATT_BQ, 2 * HEAD_DIM), F32),
            pltpu.VMEM((ATT_HG, 2 * ATT_BQ, ATT_BK), F32),
            pltpu.VMEM((ATT_HG, 2 * ATT_BQ, ATT_BK), F32),
            pltpu.VMEM((ATT_HG, 2 * ATT_BQ, HEAD_DIM), BF16),
        ],
        compiler_params=_cparams(("arbitrary", "arbitrary")),
        name="diff_attention",
    )(rel_bias, proj_qk, proj_qk, proj_plain, lam_params, g_subln)


def _mix_kernel(ga_ref, gb_ref, oa_ref, od_ref, x_ref, wo_ref, gffn_ref, wr_ref, br_ref,
                x1_ref, h2_ref, topi_ref, topw_ref, rank_ref, cnt_ref, carry_ref):
    i = pl.program_id(0)
    tm = MIX_TM

    @pl.when(i == 0)
    def _():
        carry_ref[...] = jnp.zeros_like(carry_ref)

    tp = tm // MIX_PARTS
    parts = range(MIX_PARTS)
    rows = [slice(pp * tp, (pp + 1) * tp) for pp in parts]
    mixes = [ga_ref[rs, :] * oa_ref[rs, :] + gb_ref[rs, :] * od_ref[rs, :] for rs in rows]
    x1s = [x_ref[rs, :] + jnp.dot(mx, wo_ref[...], preferred_element_type=F32) for rs, mx in zip(rows, mixes)]
    for rs, x1 in zip(rows, x1s):
        x1_ref[rs, :] = x1
    h2s = [x1 * lax.rsqrt(jnp.mean(x1 * x1, axis=-1, keepdims=True) + EPS) * gffn_ref[...] for x1 in x1s]
    for rs, h2 in zip(rows, h2s):
        h2_ref[rs, :] = _pack_halves(h2)

    curs = [lax.dot_general(wr_ref[...], h2, (((1,), (1,)), ((), ())), preferred_element_type=F32,
                            precision=lax.Precision.HIGHEST) + br_ref[...] for h2 in h2s]
    eidx = lax.broadcasted_iota(jnp.int32, curs[0].shape, 0).astype(F32)
    vals = [[] for _ in parts]
    hots = [[] for _ in parts]
    for kk in range(TOP_K):
        mxs = [jnp.max(cur, axis=0, keepdims=True) for cur in curs]
        idxs = [jnp.min(jnp.where(cur == mx, eidx, float(N_EXPERTS)), axis=0, keepdims=True)
                for cur, mx in zip(curs, mxs)]
        for pp in parts:
            hot = eidx == idxs[pp]
            vals[pp].append(mxs[pp])
            hots[pp].append(hot)
            topi_ref[kk:kk + 1, rows[pp]] = idxs[pp].astype(jnp.int32)
            curs[pp] = jnp.where(hot, -jnp.inf, curs[pp])
    for pp in parts:
        exps = [jnp.exp(vv - vals[pp][0]) for vv in vals[pp]]
        denom = exps[0] + exps[1] + exps[2] + exps[3]
        for kk in range(TOP_K):
            topw_ref[kk:kk + 1, rows[pp]] = exps[kk] / denom

    r = lax.broadcasted_iota(jnp.int32, (tp, tp), 0)
    c = lax.broadcasted_iota(jnp.int32, (tp, tp), 1)
    earlier = (r < c).astype(F32)
    sel_fs = []
    for pp in parts:
        sel = hots[pp][0]
        for kk in range(1, TOP_K):
            sel = jnp.logical_or(sel, hots[pp][kk])
        sel_fs.append(sel.astype(F32))
    within = [_bdot(sf, earlier) for sf in sel_fs]
    totals = [jnp.sum(sf, axis=-1, keepdims=True) for sf in sel_fs]
    run = carry_ref[...]
    for pp in parts:
        before = within[pp] + run
        for kk in range(TOP_K):
            rank_ref[kk:kk + 1, rows[pp]] = jnp.sum(jnp.where(hots[pp][kk], before, 0.0), axis=0,
                                                    keepdims=True).astype(jnp.int32)
        run = run + totals[pp]
    carry_ref[...] = run
    cnt_ref[...] = run.astype(jnp.int32)


def _mix_project_route(proj_gate, oa, od, x2d, w_o, g_ffn, w_r_t, b_r, d_model):
    t = x2d.shape[0]
    tm = MIX_TM
    full = lambda shape: pl.BlockSpec(shape, lambda i: (0, 0))
    row = lambda: pl.BlockSpec((tm, d_model), lambda i: (i, 0))
    krow = lambda: pl.BlockSpec((TOP_K, tm), lambda i: (0, i))
    return pl.pallas_call(
        _mix_kernel,
        grid=(t // tm,),
        in_specs=[
            pl.BlockSpec((tm, d_model), lambda i: (i, 0)),
            pl.BlockSpec((tm, d_model), lambda i: (i, 1)),
            row(), row(), row(),
            full((d_model, d_model)), full((1, d_model)), full((N_EXPERTS, d_model)), full((N_EXPERTS, 1)),
        ],
        out_specs=[row(), pl.BlockSpec((tm, d_model // 2), lambda i: (i, 0)),
                   krow(), krow(), krow(), full((N_EXPERTS, 1))],
        out_shape=[
            jax.ShapeDtypeStruct((t, d_model), F32),
            jax.ShapeDtypeStruct((t, d_model // 2), jnp.int32),
            jax.ShapeDtypeStruct((TOP_K, t), jnp.int32),
            jax.ShapeDtypeStruct((TOP_K, t), F32),
            jax.ShapeDtypeStruct((TOP_K, t), jnp.int32),
            jax.ShapeDtypeStruct((N_EXPERTS, 1), jnp.int32),
        ],
        scratch_shapes=[pltpu.VMEM((N_EXPERTS, 1), F32)],
        compiler_params=_cparams(("arbitrary",)),
        name="merge_outproj_route",
    )(proj_gate, proj_gate, oa, od, x2d, w_o, g_ffn, w_r_t, b_r)


def _pack_halves(x):
    half = x.shape[1] // 2
    bits = pltpu.bitcast(x.astype(BF16).astype(F32), jnp.int32)
    return bits[:, :half] | lax.shift_right_logical(bits[:, half:], 16)


def _unpack_halves(p):
    hi = pltpu.bitcast(p & jnp.int32(-65536), F32)
    lo = pltpu.bitcast(lax.shift_left(p, 16), F32)
    return jnp.concatenate([hi, lo], axis=1)


def _expert_kernel(be_ref, nu_ref, ne_ref, x_ref, wup_hbm, bup_ref, wdn_hbm, bdn_ref, *rest):
    y_ref, wup_bf, wdn_bf, wup_f, wdn_f, sem = rest[-6:]
    i = pl.program_id(0)
    d_ff = wdn_f.shape[0]

    def weight_copies(e):
        return (pltpu.make_async_copy(wup_hbm.at[e], wup_f, sem.at[0]),
                pltpu.make_async_copy(wdn_hbm.at[e], wdn_f, sem.at[1]))

    @pl.when(i == 0)
    def _():
        for cp in weight_copies(be_ref[0]):
            cp.start()

    @pl.when(jnp.logical_or(i == 0, be_ref[i] != be_ref[jnp.maximum(i - 1, 0)]))
    def _():
        for cp in weight_copies(be_ref[i]):
            cp.wait()
        rr = lax.broadcasted_iota(jnp.int32, (2 * LANES, 2 * LANES), 0)
        cc = lax.broadcasted_iota(jnp.int32, (2 * LANES, 2 * LANES), 1)
        pick = jnp.where(cc < LANES, 2 * cc, 2 * (cc - LANES) + 1)
        perm = (rr == pick).astype(BF16)
        for g in range(wup_f.shape[1] // (2 * LANES)):
            cs = slice(g * 2 * LANES, (g + 1) * 2 * LANES)
            wup_bf[:, cs] = jnp.dot(wup_f[:, cs].astype(BF16), perm,
                                    preferred_element_type=F32).astype(BF16)
        wdn_bf[...] = wdn_f[...].astype(BF16)

        @pl.when(ne_ref[i] >= 0)
        def _():
            for cp in weight_copies(ne_ref[i]):
                cp.start()

    @pl.when(i < nu_ref[0])
    def _():
        x = _unpack_halves(x_ref[...])
        hid = jnp.dot(x.astype(BF16), wup_bf[...], preferred_element_type=F32) + bup_ref[0]
        acts = []
        for g in range(hid.shape[1] // (2 * LANES)):
            glu = jnp.minimum(hid[:, g * 2 * LANES:g * 2 * LANES + LANES], SWIGLU_LIMIT)
            lin = jnp.clip(hid[:, g * 2 * LANES + LANES:(g + 1) * 2 * LANES], -SWIGLU_LIMIT, SWIGLU_LIMIT)
            acts.append(glu * _sigmoid(SWIGLU_ALPHA * glu) * (lin + 1.0))
        act = jnp.concatenate(acts, axis=1)
        assert act.shape[1] == d_ff
        y = jnp.dot(act.astype(BF16), wdn_bf[...], preferred_element_type=F32) + bdn_ref[0]
        y_ref[...] = _pack_halves(y)

    @pl.when(i >= nu_ref[0])
    def _():
        y_ref[...] = jnp.zeros(y_ref.shape, y_ref.dtype)


def _experts(block_e, n_used, xs, y_prev, first_block, n_rows_total, w_up, b_up, w_down, b_down):
    n_rows, half = xs.shape
    d = w_up.shape[1]
    nb = n_rows // MOE_RB
    two_ff = w_up.shape[2]
    d_ff = w_down.shape[1]
    blk = jnp.arange(nb, dtype=jnp.int32)
    later = jnp.logical_and(blk[None, :] > blk[:, None], block_e[None, :] != block_e[:, None])
    next_e = jnp.min(jnp.where(later, block_e[None, :], N_EXPERTS), axis=1)
    next_e = jnp.where(next_e == N_EXPERTS, -1, next_e).astype(jnp.int32)
    in_specs = [
        pl.BlockSpec((MOE_RB, half), lambda i, be, nu, ne: (jnp.maximum(jnp.minimum(i, nu[0] - 1), 0), 0)),
        pl.BlockSpec(memory_space=pl.ANY),
        pl.BlockSpec((1, 1, two_ff), lambda i, be, nu, ne: (be[i], 0, 0)),
        pl.BlockSpec(memory_space=pl.ANY),
        pl.BlockSpec((1, 1, d), lambda i, be, nu, ne: (be[i], 0, 0)),
    ]
    operands = [block_e, n_used, next_e, xs, w_up, b_up, w_down, b_down]
    aliases = {}
    if y_prev is not None:
        in_specs.append(pl.BlockSpec(memory_space=pl.ANY))
        aliases = {len(operands): 0}
        operands.append(y_prev)
    grid_spec = pltpu.PrefetchScalarGridSpec(
        num_scalar_prefetch=3,
        grid=(nb,),
        in_specs=in_specs,
        out_specs=pl.BlockSpec((MOE_RB, half), lambda i, be, nu, ne: (first_block + i, 0)),
        scratch_shapes=[pltpu.VMEM((d, two_ff), BF16), pltpu.VMEM((d_ff, d), BF16),
                        pltpu.VMEM((d, two_ff), F32), pltpu.VMEM((d_ff, d), F32),
                        pltpu.SemaphoreType.DMA((2,))],
    )
    return pl.pallas_call(
        _expert_kernel,
        grid_spec=grid_spec,
        out_shape=jax.ShapeDtypeStruct((n_rows_total, half), jnp.int32),
        input_output_aliases=aliases,
        compiler_params=_cparams(("arbitrary",)),
        name="moe_experts",
    )(*operands)


def _sc_invert_slots(dest_flat, n_rows):
    n_assign = dest_flat.shape[0]
    n_workers = SC_CORES * SC_SUBCORES
    rows_per_w = n_rows // n_workers
    chunk = SC_SCAN_CHUNK
    assert n_rows % n_workers == 0 and rows_per_w % SC_LANES == 0 and n_assign % chunk == 0
    mesh = plsc.VectorSubcoreMesh(core_axis_name="c", subcore_axis_name="s",
                                  num_cores=SC_CORES, num_subcores=SC_SUBCORES)

    def body(dest_hbm, out_hbm, dest_v, map_v):
        wid = lax.axis_index("s") * SC_CORES + lax.axis_index("c")
        base = wid * rows_per_w
        lanes = lax.broadcasted_iota(jnp.int32, (SC_LANES,), 0)

        @pl.loop(0, rows_per_w, step=SC_LANES)
        def _(r0):
            map_v[pl.ds(r0, SC_LANES)] = jnp.full((SC_LANES,), -1, jnp.int32)

        @pl.loop(0, n_assign // chunk)
        def _(ci):
            pltpu.sync_copy(dest_hbm.at[pl.ds(ci * chunk, chunk)], dest_v)

            @pl.loop(0, chunk, step=SC_LANES)
            def _(j):
                local = dest_v[pl.ds(j, SC_LANES)] - base
                mine = jnp.logical_and(local >= 0, local < rows_per_w)
                plsc.store_scatter(map_v, [jnp.where(mine, local, 0)], ci * chunk + j + lanes, mask=mine)

        pltpu.sync_copy(map_v, out_hbm.at[pl.ds(base, rows_per_w)])

    return pl.kernel(
        body,
        out_type=jax.ShapeDtypeStruct((n_rows,), jnp.int32),
        mesh=mesh,
        scratch_types=[pltpu.VMEM((chunk,), jnp.int32), pltpu.VMEM((rows_per_w,), jnp.int32)],
        compiler_params=pltpu.CompilerParams(needs_layout_passes=False),
        name="moe_slot_inverse",
    )(dest_flat)


def _sc_gather_rows(table, idx):
    n_idx = idx.shape[0]
    d = table.shape[1]
    n_workers = SC_CORES * SC_SUBCORES
    per_worker = n_idx // n_workers
    n_chunks = per_worker // SC_GATHER_ROWS
    assert n_idx % n_workers == 0 and per_worker % SC_GATHER_ROWS == 0
    mesh = plsc.VectorSubcoreMesh(core_axis_name="c", subcore_axis_name="s",
                                  num_cores=SC_CORES, num_subcores=SC_SUBCORES)

    assert n_chunks % 2 == 0

    def body(table_hbm, idx_hbm, out_hbm, idx_v, rows_a, rows_b, sem_a, sem_b):
        wid = lax.axis_index("s") * SC_CORES + lax.axis_index("c")
        base = wid * per_worker
        pltpu.sync_copy(idx_hbm.at[pl.ds(base, per_worker)], idx_v)

        def gather(ci, rows_v, sem):
            off = pl.multiple_of(ci * SC_GATHER_ROWS, SC_GATHER_ROWS)
            return pltpu.make_async_copy(table_hbm.at[idx_v.at[pl.ds(off, SC_GATHER_ROWS)]], rows_v, sem)

        def put(ci, rows_v):
            off = pl.multiple_of(ci * SC_GATHER_ROWS, SC_GATHER_ROWS)
            pltpu.sync_copy(rows_v, out_hbm.at[pl.ds(base + off, SC_GATHER_ROWS)])

        gather(0, rows_a, sem_a).start()

        @pl.loop(0, n_chunks, step=2)
        def _(ci):
            gather(ci + 1, rows_b, sem_b).start()
            gather(ci, rows_a, sem_a).wait()
            put(ci, rows_a)
            nxt = jnp.minimum(ci + 2, n_chunks - 1)
            gather(nxt, rows_a, sem_a).start()
            gather(ci + 1, rows_b, sem_b).wait()
            put(ci + 1, rows_b)

        gather(n_chunks - 1, rows_a, sem_a).wait()

    return pl.kernel(
        body,
        out_type=jax.ShapeDtypeStruct((n_idx, d), table.dtype),
        mesh=mesh,
        scratch_types=[
            pltpu.VMEM((per_worker,), jnp.int32),
            pltpu.VMEM((SC_GATHER_ROWS, d), table.dtype),
            pltpu.VMEM((SC_GATHER_ROWS, d), table.dtype),
            pltpu.SemaphoreType.DMA,
            pltpu.SemaphoreType.DMA,
        ],
        name="moe_slot_gather",
    )(table, idx)


def _combine_kernel(x1_ref, w_ref, y0_ref, y1_ref, y2_ref, y3_ref, o_ref):
    w = w_ref[...]
    out = x1_ref[...]
    for kk, y_ref in enumerate((y0_ref, y1_ref, y2_ref, y3_ref)):
        out = out + w[:, kk:kk + 1] * _unpack_halves(y_ref[...])
    o_ref[...] = out


def _combine(x1, w_tok, y_slots):
    t, d = x1.shape
    tc = COMB_TC
    nt = t // tc
    yspec = lambda kk: pl.BlockSpec((tc, d // 2), lambda i: (kk * nt + i, 0))
    return pl.pallas_call(
        _combine_kernel,
        grid=(nt,),
        in_specs=[
            pl.BlockSpec((tc, d), lambda i: (i, 0)),
            pl.BlockSpec((tc, TOP_K), lambda i: (i, 0)),
            yspec(0), yspec(1), yspec(2), yspec(3),
        ],
        out_specs=pl.BlockSpec((tc, d), lambda i: (i, 0)),
        out_shape=jax.ShapeDtypeStruct((t, d), F32),
        compiler_params=_cparams(("parallel",)),
        name="moe_combine",
    )(x1, w_tok, y_slots, y_slots, y_slots, y_slots)


def _moe(x1, h2, topi, topw, rank, counts, w_up, b_up, w_down, b_down):
    t, d = x1.shape
    n_assign = t * TOP_K
    nb = -(-n_assign // MOE_RB) + N_EXPERTS
    n_rows = nb * MOE_RB
    counts = counts[:, 0]
    padded = (counts + MOE_RB - 1) // MOE_RB * MOE_RB
    padded_end = jnp.cumsum(padded)
    padded_start = padded_end - padded
    expert_ids = jnp.arange(N_EXPERTS, dtype=jnp.int32)[:, None, None]
    start_of = jnp.sum(jnp.where(topi[None] == expert_ids, padded_start[:, None, None], 0), axis=0)
    dest = (start_of + rank).astype(jnp.int32)
    n_used = (padded_end[-1] // MOE_RB).astype(jnp.int32)
    blk = jnp.minimum(jnp.arange(nb, dtype=jnp.int32), n_used - 1)
    block_e = jnp.minimum(jnp.sum(padded_end[None, :] <= (blk * MOE_RB)[:, None], axis=1),
                          N_EXPERTS - 1).astype(jnp.int32)
    slot_of = _sc_invert_slots(dest.reshape(-1), n_rows)
    src_tok = jnp.where(slot_of < 0, jnp.arange(n_rows, dtype=jnp.int32), slot_of) % t

    nb_a = nb // 4
    y_rows = None
    for first, n_blk in ((0, nb_a), (nb_a, nb - nb_a)):
        xs = _sc_gather_rows(h2, lax.slice(src_tok, (first * MOE_RB,), ((first + n_blk) * MOE_RB,)))
        used = jnp.clip(n_used - first, 0, n_blk).reshape(1)
        y_rows = _experts(lax.slice(block_e, (first,), (first + n_blk,)), used, xs, y_rows, first, n_rows,
                          w_up, b_up, w_down, b_down)
    y_slots = _sc_gather_rows(y_rows, dest.reshape(-1))
    return _combine(x1, topw.T, y_slots)


def kernel(x, g_mix, w_in, b_gate, conv_w, a_log, dt_bias, g_delta_out, q_norm, k_norm, lambda_q1, lambda_k1, lambda_q2, lambda_k2, g_subln, rel_bias, w_o, g_ffn, w_router, b_router, w_up, b_up, w_down, b_down):
    bsz, seq, d = x.shape
    depth = g_mix.shape[0]
    n_heads = d // HEAD_DIM
    t = bsz * seq
    d_ff = w_down.shape[2]
    assert d % PROJ_TN == 0 and t % PROJ_TM == 0 and seq % GDN_TB == 0 and seq % ATT_BQ == 0
    assert t % MIX_TM == 0 and t % COMB_TC == 0 and n_heads % GDN_HG == 0
    assert (t * TOP_K) % MOE_RB == 0
    assert 2 * n_heads <= 2 * SUBLANES

    x2d = x.reshape(t, d)
    for l in range(depth):
        wl = w_in[l]
        c0 = 4 * d
        c1 = c0 + 2 * n_heads
        c2 = c1 + 2 * d
        c3 = c2 + d
        w_small = jnp.pad(wl[:, c0:c1], ((0, 0), (0, LANES - 2 * n_heads)))
        gm = g_mix[l].reshape(1, d)
        head_pad = jnp.zeros((LANES - 2 * n_heads,), F32)
        alog = jnp.concatenate([jnp.zeros((n_heads,), F32), a_log[l], head_pad])
        dtb = jnp.concatenate([jnp.zeros((n_heads,), F32), dt_bias[l], head_pad])
        rows_t = 2 * n_heads
        beta_decay = (w_small.astype(BF16), w_small[:, :rows_t].T.astype(BF16),
                      alog.reshape(1, LANES), dtb.reshape(1, LANES),
                      alog[:rows_t].reshape(rows_t, 1), dtb[:rows_t].reshape(rows_t, 1), n_heads)
        w_plain = jnp.concatenate([wl[:, :c0], wl[:, c2:c3]], axis=1).astype(BF16)
        proj_plain, small, small_t = _input_projection(x2d, gm, w_plain, jnp.zeros((1, 5 * d), F32), "plain",
                                                       beta_decay)
        qk_gain = jnp.concatenate([jnp.tile(q_norm[l] * (DH_DIFF ** -0.5), 2 * n_heads),
                                   jnp.tile(k_norm[l], 2 * n_heads)]).reshape(1, 2 * d)
        proj_qk = _input_projection(x2d, gm, wl[:, c1:c2].astype(BF16), qk_gain, "qknorm")
        proj_gate = _input_projection(x2d, gm, wl[:, c3:].astype(BF16), b_gate[l].reshape(1, 2 * d), "gate")

        oa = _gated_delta(proj_plain, small, small_t, conv_w[l], g_delta_out[l].reshape(1, HEAD_DIM),
                          bsz, seq, n_heads, d)

        lam_init = 0.8 - 0.6 * math.exp(-0.3 * l)
        lam_params = jnp.stack([lambda_q1[l], lambda_k1[l], lambda_q2[l], lambda_k2[l]])
        od = _diff_attention(proj_qk, proj_plain, rel_bias, lam_params, g_subln[l].reshape(1, HEAD_DIM),
                             bsz, seq, n_heads, d, lam_init)

        x1, h2, topi, topw, rank, counts = _mix_project_route(
            proj_gate, oa, od, x2d, w_o[l].astype(BF16), g_ffn[l].reshape(1, d),
            w_router[l].T, b_router[l].reshape(N_EXPERTS, 1), d)

        b_up_l = b_up[l].reshape(N_EXPERTS, 2 * d_ff // (2 * LANES), LANES, 2)
        b_up_l = jnp.swapaxes(b_up_l, 2, 3).reshape(N_EXPERTS, 1, 2 * d_ff)
        x2d = _moe(x1, h2, topi, topw, rank, counts, w_up[l], b_up_l,
                   w_down[l], b_down[l].reshape(N_EXPERTS, 1, d))
    return x2d.reshape(bsz, seq, d)
```

```python
import functools
import math

import jax
import jax.numpy as jnp
from jax import lax
from jax.experimental import pallas as pl
from jax.experimental.pallas import tpu as pltpu
from jax.experimental.pallas import tpu_sc as plsc

F32 = jnp.float32
BF16 = jnp.bfloat16

HEAD_DIM = 128
DH_DIFF = HEAD_DIM // 2
CONV_WIDTH = 4
CHUNK = 64
N_BUCKETS = 32
MAX_DISTANCE = 128
N_EXPERTS = 32
TOP_K = 4
SWIGLU_LIMIT = 7.0
SWIGLU_ALPHA = 1.702
EPS = 1e-6
NEG_BIG = -1e30

LANES = 128
SUBLANES = 8
VMEM_LIMIT = 56 * 1024 * 1024
SC_CORES = 2
SC_SUBCORES = 16
SC_LANES = 16
SC_GATHER_ROWS = 64
SC_SCAN_CHUNK = 4096
SC_SCAN_UNROLL = 4

PROJ_TM = 2048
PROJ_TN = 1024
PROJ_CHUNK = 256
GDN_TB = 256
GDN_HG = 8
ATT_HG = 1
ATT_BQ = 512
ATT_BK = 512
MIX_TM = 1024
MIX_PARTS = 2
MOE_RB = 512
COMB_TC = 512


def _cparams(sem):
    return pltpu.CompilerParams(dimension_semantics=sem, vmem_limit_bytes=VMEM_LIMIT)


def _sigmoid(x):
    return 0.5 * jnp.tanh(0.5 * x) + 0.5


def _bdot(a, b):
    return jnp.dot(a.astype(BF16), b.astype(BF16), preferred_element_type=F32)


def _bdot_nt(a, b):
    return lax.dot_general(a.astype(BF16), b.astype(BF16), (((1,), (1,)), ((), ())),
                           preferred_element_type=F32)


def _bdot_tn(a, b):
    return lax.dot_general(a.astype(BF16), b.astype(BF16), (((0,), (0,)), ((), ())),
                           preferred_element_type=F32)


def _beta_decay(acc, idx, alog, dtb, n_heads):
    beta = _sigmoid(acc)
    z = acc + dtb
    softplus = jnp.maximum(z, 0.0) + jnp.log1p(jnp.exp(-jnp.abs(z)))
    gdec = -jnp.exp(alog) * softplus
    return jnp.where(idx < n_heads, beta, jnp.where(idx < 2 * n_heads, gdec, 0.0))


def _proj_kernel(x_ref, g_ref, w_ref, aux_ref, *rest, mode, n_heads):
    o_ref, h_ref = rest[-2:] if n_heads is None else (rest[6], rest[-1])

    @pl.when(pl.program_id(1) == 0)
    def _():
        x = x_ref[...]
        ms = jnp.mean(x * x, axis=-1, keepdims=True)
        h_ref[...] = (x * lax.rsqrt(ms + EPS) * g_ref[...]).astype(BF16)
        if n_heads is not None:
            ws_ref, wst_ref, alog_ref, dtb_ref, alog_t_ref, dtb_t_ref, _, os_ref, ost_ref, _ = rest
            hb = h_ref[...]
            acc = jnp.dot(hb, ws_ref[...], preferred_element_type=F32)
            lane = lax.broadcasted_iota(jnp.int32, acc.shape, 1)
            os_ref[...] = _beta_decay(acc, lane, alog_ref[...], dtb_ref[...], n_heads)
            acc_t = lax.dot_general(wst_ref[...], hb, (((1,), (1,)), ((), ())),
                                    preferred_element_type=F32)
            sub = lax.broadcasted_iota(jnp.int32, acc_t.shape, 0)
            ost_ref[...] = _beta_decay(acc_t, sub, alog_t_ref[...], dtb_t_ref[...], n_heads)

    h = h_ref[...]
    lo = lax.broadcasted_iota(jnp.int32, (1, LANES), 1) < DH_DIFF
    for c in range(PROJ_TN // PROJ_CHUNK):
        cs = slice(c * PROJ_CHUNK, (c + 1) * PROJ_CHUNK)
        acc = jnp.dot(h, w_ref[:, cs], preferred_element_type=F32)
        if mode == "plain":
            o_ref[:, cs] = acc.astype(o_ref.dtype)
        elif mode == "gate":
            o_ref[:, cs] = _sigmoid(acc + aux_ref[:, cs]).astype(o_ref.dtype)
        else:
            for g in range(PROJ_CHUNK // LANES):
                sl = slice(c * PROJ_CHUNK + g * LANES, c * PROJ_CHUNK + (g + 1) * LANES)
                y = acc[:, g * LANES:(g + 1) * LANES]
                y2 = y * y
                s_lo = jnp.sum(jnp.where(lo, y2, 0.0), axis=-1, keepdims=True)
                s_hi = jnp.sum(jnp.where(lo, 0.0, y2), axis=-1, keepdims=True)
                r = jnp.where(lo, lax.rsqrt(s_lo / DH_DIFF + EPS), lax.rsqrt(s_hi / DH_DIFF + EPS))
                o_ref[:, sl] = (y * r * aux_ref[:, sl]).astype(o_ref.dtype)


def _input_projection(x2d, g_mix, w, aux, mode, beta_decay=None):
    t, d = x2d.shape
    n = w.shape[1]
    full = lambda shape: pl.BlockSpec(shape, lambda i, j: (0, 0))
    in_specs = [
        pl.BlockSpec((PROJ_TM, d), lambda i, j: (i, 0)),
        full((1, d)),
        pl.BlockSpec((d, PROJ_TN), lambda i, j: (0, j)),
        pl.BlockSpec((1, PROJ_TN), lambda i, j: (0, j)),
    ]
    operands = [x2d, g_mix, w, aux]
    out_specs = [pl.BlockSpec((PROJ_TM, PROJ_TN), lambda i, j: (i, j))]
    out_shape = [jax.ShapeDtypeStruct((t, n), BF16)]
    n_heads = None
    if beta_decay is not None:
        n_heads = beta_decay[-1]
        rows_t = 2 * n_heads
        in_specs += [full((d, LANES)), full((rows_t, d)), full((1, LANES)), full((1, LANES)),
                     full((rows_t, 1)), full((rows_t, 1))]
        operands += list(beta_decay[:-1])
        out_specs += [pl.BlockSpec((PROJ_TM, LANES), lambda i, j: (i, 0)),
                      pl.BlockSpec((rows_t, PROJ_TM), lambda i, j: (0, i))]
        out_shape += [jax.ShapeDtypeStruct((t, LANES), F32), jax.ShapeDtypeStruct((rows_t, t), F32)]
    out = pl.pallas_call(
        functools.partial(_proj_kernel, mode=mode, n_heads=n_heads),
        grid=(t // PROJ_TM, n // PROJ_TN),
        in_specs=in_specs,
        out_specs=out_specs,
        out_shape=out_shape,
        scratch_shapes=[pltpu.VMEM((PROJ_TM, d), BF16)],
        compiler_params=_cparams(("parallel", "arbitrary")),
        name="input_projection_" + mode,
    )(*operands)
    return out[0] if beta_decay is None else out


def _gdn_kernel(q_ref, k_ref, v_ref, z_ref, sm_ref, smt_ref, cwq_ref, cwk_ref, cwv_ref, gout_ref,
                o_ref, state_ref, qp_ref, kp_ref, vp_ref, vn_ref, *, n_heads):
    hg = pl.program_id(1)
    s = pl.program_id(2)
    tb = GDN_TB
    pad = SUBLANES
    width = GDN_HG * HEAD_DIM

    @pl.when(s == 0)
    def _():
        state_ref[...] = jnp.zeros_like(state_ref)
        for p_ref in (qp_ref, kp_ref, vp_ref):
            p_ref[0:pad, :] = jnp.zeros((pad, width), F32)

    r = lax.broadcasted_iota(jnp.int32, (tb, tb), 0)
    c = lax.broadcasted_iota(jnp.int32, (tb, tb), 1)
    delay_mat = jnp.concatenate([(r - c == dd).astype(BF16) for dd in range(1, CONV_WIDTH)], axis=0)

    def conv_silu(x_ref, p_ref, cw_ref):
        x = x_ref[...]
        xf = x.astype(F32)
        p_ref[pad:2 * pad, :] = xf[0:pad]
        delayed = jnp.dot(delay_mat, x, preferred_element_type=F32)
        acc = cw_ref[CONV_WIDTH - 1:CONV_WIDTH, :] * xf
        for dd in range(1, CONV_WIDTH):
            first = p_ref[pad - dd:2 * pad - dd, :]
            xd = jnp.concatenate([first, delayed[(dd - 1) * tb + pad:dd * tb]], axis=0)
            acc = acc + cw_ref[CONV_WIDTH - 1 - dd:CONV_WIDTH - dd, :] * xd
        p_ref[0:pad, :] = xf[tb - pad:tb]
        return acc * _sigmoid(acc)

    q_all = conv_silu(q_ref, qp_ref, cwq_ref)
    k_all = conv_silu(k_ref, kp_ref, cwk_ref)
    v_all = conv_silu(v_ref, vp_ref, cwv_ref)

    shift = int(math.log2(CHUNK))
    same = (r >> shift) == (c >> shift)
    incl = jnp.logical_and(same, c <= r)
    strict = jnp.logical_and(same, c < r)

    small = sm_ref[...]
    small_t = smt_ref[...]
    lane = lax.broadcasted_iota(jnp.int32, small.shape, 1)
    def split3(a):
        hi = a.astype(BF16)
        r1 = a - hi.astype(F32)
        mid = r1.astype(BF16)
        lo = (r1 - mid.astype(F32)).astype(BF16)
        return hi.astype(F32), mid.astype(F32), lo.astype(F32)

    part = 2 * n_heads
    s_hi, s_mid, s_lo = split3(small)
    small3 = jnp.where(lane < part, s_hi,
                       jnp.where(lane < 2 * part, pltpu.roll(s_mid, part, 1),
                                 jnp.where(lane < 3 * part, pltpu.roll(s_lo, 2 * part, 1), 0.0)))
    both = _bdot(jnp.concatenate([incl.astype(F32), same.astype(F32)], axis=0), small3)
    gcum = both[:tb]
    gtot = both[tb:]
    gcum_t = _bdot(jnp.concatenate(split3(small_t), axis=0),
                   jnp.logical_and(same, r <= c).astype(F32))
    sub3 = lax.broadcasted_iota(jnp.int32, gcum_t.shape, 0)

    heads = range(GDN_HG)
    hsl = [slice(hh * HEAD_DIM, (hh + 1) * HEAD_DIM) for hh in heads]
    qs = [q_all[:, hs] for hs in hsl]
    ks = [k_all[:, hs] for hs in hsl]
    vs = [v_all[:, hs] for hs in hsl]
    qs = [q * lax.rsqrt(jnp.sum(q * q, axis=-1, keepdims=True) + EPS) * (HEAD_DIM ** -0.5) for q in qs]
    ks = [k * lax.rsqrt(jnp.sum(k * k, axis=-1, keepdims=True) + EPS) for k in ks]

    def col_of(arr, idx):
        return jnp.sum(jnp.where(lane == idx, arr, 0.0), axis=-1, keepdims=True)

    def terms_of(pos, idx):
        return jnp.logical_or(pos == idx, jnp.logical_or(pos == idx + part, pos == idx + 2 * part))

    head_ids = [hg * GDN_HG + hh for hh in heads]
    betas = [col_of(small, hd) for hd in head_ids]
    gcs = [jnp.sum(jnp.where(terms_of(lane, hd + n_heads), gcum, 0.0), axis=-1, keepdims=True)
           for hd in head_ids]
    gls = [jnp.sum(jnp.where(terms_of(lane, hd + n_heads), gtot, 0.0), axis=-1, keepdims=True)
           for hd in head_ids]
    gc_rows = [jnp.sum(jnp.where(terms_of(sub3, hd + n_heads), gcum_t, 0.0), axis=0, keepdims=True)
               for hd in head_ids]

    decays = [jnp.where(incl, jnp.exp(jnp.minimum(gc - gr, 0.0)), 0.0) for gc, gr in zip(gcs, gc_rows)]
    kbs = [k * b for k, b in zip(ks, betas)]
    kks = [_bdot_nt(kb, k) for kb, k in zip(kbs, ks)]
    pws = [jnp.where(strict, -(kk * dc), 0.0) for kk, dc in zip(kks, decays)]
    n_chunks = tb // CHUNK
    cat_row = lax.broadcasted_iota(jnp.int32, (CHUNK, tb), 0)
    cat_lane = lax.broadcasted_iota(jnp.int32, (CHUNK, tb), 1)
    lane_chunk = cat_lane >> shift

    def block_diag(m_cat):
        return jnp.concatenate([jnp.where(lane_chunk == ci, m_cat, 0.0) for ci in range(n_chunks)], axis=0)

    def cat_of(m_bd):
        out = m_bd[0:CHUNK]
        for ci in range(1, n_chunks):
            out = out + m_bd[ci * CHUNK:(ci + 1) * CHUNK]
        return out

    pcats = [cat_of(pw) for pw in pws]
    eye_cat = ((cat_lane & (CHUNK - 1)) == cat_row).astype(F32)
    tcats = [eye_cat + pc for pc in pcats]
    pcats = [_bdot(pc, block_diag(pc)) for pc in pcats]
    n_levels = int(math.log2(CHUNK))
    for lev in range(1, n_levels):
        bds = [block_diag(pc) for pc in pcats]
        if lev < n_levels - 1:
            prods = [_bdot(jnp.concatenate([pc, tc], axis=0), bd) for pc, tc, bd in zip(pcats, tcats, bds)]
            pcats = [pr[:CHUNK] for pr in prods]
            tcats = [tc + pr[CHUNK:] for tc, pr in zip(tcats, prods)]
        else:
            tcats = [tc + _bdot(tc, bd) for tc, bd in zip(tcats, bds)]
    tmats = [block_diag(tc) for tc in tcats]
    egcs = [jnp.exp(gc) for gc in gcs]
    uws = [_bdot(tm, jnp.concatenate([v * b, kb * eg], axis=1))
           for tm, v, b, kb, eg in zip(tmats, vs, betas, kbs, egcs)]
    us = [uw[:, :HEAD_DIM] for uw in uws]
    ws = [uw[:, HEAD_DIM:] for uw in uws]
    qkm = [_bdot_nt(q, k) for q, k in zip(qs, ks)]
    qkm = [jnp.where(incl, x * dc, 0.0) for x, dc in zip(qkm, decays)]
    q_decs = [q * eg for q, eg in zip(qs, egcs)]
    k_ends = [k * jnp.exp(gl - gc) for k, gl, gc in zip(ks, gls, gcs)]

    for hh in heads:
        vn_ref[hh] = jnp.zeros((tb, HEAD_DIM), F32)
    outs = [[] for _ in heads]
    for ci in range(tb // CHUNK):
        cs = slice(ci * CHUNK, (ci + 1) * CHUNK)
        sts = [state_ref[hh] for hh in heads]
        ws_qs = [_bdot(jnp.concatenate([ws[hh][cs], q_decs[hh][cs]], axis=0), sts[hh]) for hh in heads]
        v_news = [us[hh][cs] - ws_qs[hh][:CHUNK] for hh in heads]
        for hh in heads:
            vn_ref[hh, cs, :] = v_news[hh]
        intra = [_bdot(qkm[hh][cs], vn_ref[hh]) for hh in heads]
        upd = [_bdot_tn(k_ends[hh][cs], v_news[hh]) for hh in heads]
        for hh in heads:
            outs[hh].append(ws_qs[hh][CHUNK:] + intra[hh])
            g_last = gls[hh][ci * CHUNK:ci * CHUNK + 1, :]
            state_ref[hh] = sts[hh] * jnp.exp(g_last) + upd[hh]
    for hh in heads:
        o = jnp.concatenate(outs[hh], axis=0)
        o = o * lax.rsqrt(jnp.mean(o * o, axis=-1, keepdims=True) + EPS) * gout_ref[...]
        zz = z_ref[:, hsl[hh]].astype(F32)
        o_ref[:, hsl[hh]] = (o * (zz * _sigmoid(zz))).astype(o_ref.dtype)


def _gated_delta(big, small, small_t, conv_w, g_out, bsz, seq, n_heads, d_model):
    t = bsz * seq
    tb = GDN_TB
    ns = seq // tb
    width = GDN_HG * HEAD_DIM
    nhg = n_heads // GDN_HG
    blocks_per_group = d_model // width
    rows_t = small_t.shape[0]

    def colspec(group):
        return pl.BlockSpec((tb, width), lambda b, h, s: (b * ns + s, group * blocks_per_group + h))

    def cwspec(group):
        return pl.BlockSpec((CONV_WIDTH, width), lambda b, h, s: (0, group * blocks_per_group + h))

    return pl.pallas_call(
        functools.partial(_gdn_kernel, n_heads=n_heads),
        grid=(bsz, nhg, ns),
        in_specs=[
            colspec(0), colspec(1), colspec(2), colspec(3),
            pl.BlockSpec((tb, LANES), lambda b, h, s: (b * ns + s, 0)),
            pl.BlockSpec((rows_t, tb), lambda b, h, s: (0, b * ns + s)),
            cwspec(0), cwspec(1), cwspec(2),
            pl.BlockSpec((1, HEAD_DIM), lambda b, h, s: (0, 0)),
        ],
        out_specs=pl.BlockSpec((tb, width), lambda b, h, s: (b * ns + s, h)),
        out_shape=jax.ShapeDtypeStruct((t, d_model), BF16),
        scratch_shapes=[
            pltpu.VMEM((GDN_HG, HEAD_DIM, HEAD_DIM), F32),
            pltpu.VMEM((2 * SUBLANES, width), F32),
            pltpu.VMEM((2 * SUBLANES, width), F32),
            pltpu.VMEM((2 * SUBLANES, width), F32),
            pltpu.VMEM((GDN_HG, tb, HEAD_DIM), F32),
        ],
        compiler_params=_cparams(("parallel", "parallel", "arbitrary")),
        name="gated_delta",
    )(big, big, big, big, small, small_t, conv_w, conv_w, conv_w, g_out)


def _t5_bucket(n):
    max_exact = N_BUCKETS // 2
    nf = jnp.maximum(n, 1).astype(F32)
    large = max_exact + (jnp.log(nf / max_exact) / math.log(MAX_DISTANCE / max_exact)
                         * (N_BUCKETS - max_exact)).astype(jnp.int32)
    large = jnp.minimum(large, N_BUCKETS - 1)
    return jnp.where(n < max_exact, n, large)


def _attn_kernel(rb_ref, q_ref, k_ref, v_ref, lam_ref, gsub_ref, o_ref,
                 bias_ref, m_ref, acc_ref, sa_ref, sb_ref, qs_ref, *, lam_init):
    hg = pl.program_id(0)
    b = pl.program_id(1)
    bq, bk = ATT_BQ, ATT_BK
    heads = range(ATT_HG)
    hsl = [slice(hh * HEAD_DIM, (hh + 1) * HEAD_DIM) for hh in heads]

    @pl.when(b == 0)
    def _():
        blk = LANES
        i = lax.broadcasted_iota(jnp.int32, (blk, blk), 0)
        jj = lax.broadcasted_iota(jnp.int32, (blk, blk), 1)
        for hh in heads:
            head = hg * ATT_HG + hh
            far = rb_ref[N_BUCKETS - 1, head]

            def toeplitz(offset):
                bucket = _t5_bucket(jnp.maximum(i - jj + offset, 0))
                out = jnp.zeros((blk, blk), F32)
                for cc in range(N_BUCKETS):
                    out = jnp.where(bucket == cc, rb_ref[cc, head] - far, out)
                return out

            on_diag = jnp.where(i >= jj, toeplitz(0), NEG_BIG)
            next_diag = toeplitz(blk)
            kinds = {0: on_diag, 1: next_diag}
            bias_ref[hh, 2] = jnp.zeros((bq, bk), F32)
            for slot in range(2):
                for rr in range(bq // blk):
                    for cc in range(bk // blk):
                        delta = rr - cc + slot * (bk // blk)
                        if delta < 0:
                            tile = jnp.full((blk, blk), NEG_BIG, F32)
                        else:
                            tile = kinds.get(delta, jnp.zeros((blk, blk), F32))
                        bias_ref[hh, slot, rr * blk:(rr + 1) * blk, cc * blk:(cc + 1) * blk] = tile

    lane = lax.broadcasted_iota(jnp.int32, (bq, HEAD_DIM), 1)
    ones_col = (lax.broadcasted_iota(jnp.int32, (bk, HEAD_DIM), 1) == 0).astype(BF16)
    lam_p = lam_ref[...]
    s1 = jnp.sum(lam_p[0:1] * lam_p[1:2], axis=-1, keepdims=True)
    s2 = jnp.sum(lam_p[2:3] * lam_p[3:4], axis=-1, keepdims=True)
    lam = jnp.exp(s1) - jnp.exp(s2) + lam_init

    n_q = q_ref.shape[0] // bq

    def stack_q(qb):
        rows = pl.ds(pl.multiple_of(qb * bq, bq), bq)
        for hh in heads:
            q = q_ref[rows, hsl[hh]]
            zero = jnp.zeros_like(q)
            qs_ref[hh, 0:bq, :] = jnp.where(lane < DH_DIFF, q, zero)
            qs_ref[hh, bq:2 * bq, :] = jnp.where(lane < DH_DIFF, zero, q)

    def scores(j, s_ref):
        ks = pl.multiple_of(j * bk, bk)
        for hh in heads:
            s_ref[hh] = lax.dot_general(qs_ref[hh], k_ref[pl.ds(ks, bk), hsl[hh]],
                                        (((1,), (1,)), ((), ())), preferred_element_type=F32)

    stack_q(0)
    scores(0, sa_ref)

    def query_block(qi, outer):
        rows = pl.ds(pl.multiple_of(qi * bq, bq), bq)
        m_ref[...] = jnp.full(m_ref.shape, NEG_BIG, F32)
        acc_ref[...] = jnp.zeros(acc_ref.shape, F32)

        def absorb(j, s_ref, biased=True):
            ks = pl.multiple_of(j * bk, bk)
            v_exts = [jnp.concatenate([v_ref[pl.ds(ks, bk), hs], ones_col], axis=1) for hs in hsl]
            if biased:
                slot = jnp.minimum(qi - j, 2)
                scs = [jnp.concatenate([s_ref[hh, 0:bq, :] + bias_ref[hh, slot],
                                        s_ref[hh, bq:2 * bq, :] + bias_ref[hh, slot]], axis=0)
                       for hh in heads]
            else:
                scs = [s_ref[hh] for hh in heads]
            m_olds = [m_ref[hh] for hh in heads]
            m_news = [jnp.maximum(mo, jnp.max(sc, axis=-1, keepdims=True)) for mo, sc in zip(m_olds, scs)]
            ps = [jnp.exp(sc - mn) for sc, mn in zip(scs, m_news)]
            pvs = [jnp.dot(p.astype(BF16), ve, preferred_element_type=F32) for p, ve in zip(ps, v_exts)]
            for hh in heads:
                acc_ref[hh] = jnp.exp(m_olds[hh] - m_news[hh]) * acc_ref[hh] + pvs[hh]
                m_ref[hh] = m_news[hh]

        n_tiles = qi + 1

        def pair_body(jj, carry, biased):
            j0 = 2 * jj
            scores(j0 + 1, sb_ref)
            absorb(j0, sa_ref, biased)
            scores(jnp.minimum(j0 + 2, qi), sa_ref)
            absorb(j0 + 1, sb_ref, biased)
            return carry

        n_far_pairs = jnp.maximum(qi - 1, 0) // 2
        lax.fori_loop(0, n_far_pairs, functools.partial(pair_body, biased=False), 0)
        lax.fori_loop(n_far_pairs, n_tiles // 2, functools.partial(pair_body, biased=True), 0)

        @pl.when(n_tiles % 2 == 1)
        def _():
            absorb(qi, sa_ref)

        stack_q(jnp.minimum(qi + 1, n_q - 1))
        scores(0, sa_ref)
        for hh in heads:
            acc = acc_ref[hh]
            num = acc[:, :HEAD_DIM]
            den = acc[:, HEAD_DIM:HEAD_DIM + 1]
            o = num[:bq] / den[:bq] - lam * (num[bq:] / den[bq:])
            o = o * lax.rsqrt(jnp.mean(o * o, axis=-1, keepdims=True) + EPS) * gsub_ref[...]
            o_ref[rows, hsl[hh]] = (o * (1.0 - lam_init)).astype(o_ref.dtype)
        return outer

    lax.fori_loop(0, n_q, query_block, 0)


def _diff_attention(proj_qk, proj_plain, rel_bias, lam_params, g_subln, bsz, seq, n_heads, d_model,
                    lam_init):
    t = bsz * seq
    assert ATT_BQ == ATT_BK and MAX_DISTANCE <= LANES and n_heads % ATT_HG == 0 and seq % ATT_BQ == 0
    width = ATT_HG * HEAD_DIM
    per = d_model // width
    vcol = 4 * per
    return pl.pallas_call(
        functools.partial(_attn_kernel, lam_init=lam_init),
        grid=(n_heads // ATT_HG, bsz),
        in_specs=[
            pl.BlockSpec(memory_space=pltpu.SMEM),
            pl.BlockSpec((seq, width), lambda h, b: (b, h)),
            pl.BlockSpec((seq, width), lambda h, b: (b, per + h)),
            pl.BlockSpec((seq, width), lambda h, b: (b, vcol + h)),
            pl.BlockSpec((4, DH_DIFF), lambda h, b: (0, 0)),
            pl.BlockSpec((1, HEAD_DIM), lambda h, b: (0, 0)),
        ],
        out_specs=pl.BlockSpec((seq, width), lambda h, b: (b, h)),
        out_shape=jax.ShapeDtypeStruct((t, d_model), BF16),
        scratch_shapes=[
            pltpu.VMEM((ATT_HG, 3, ATT_BQ, ATT_BK), F32),
            pltpu.VMEM((ATT_HG, 2 * ATT_BQ, 1), F32),
            pltpu.VMEM((ATT_HG, 2 * ATT_BQ, 2 * HEAD_DIM), F32),
            pltpu.VMEM((ATT_HG, 2 * ATT_BQ, ATT_BK), F32),
            pltpu.VMEM((ATT_HG, 2 * ATT_BQ, ATT_BK), F32),
            pltpu.VMEM((ATT_HG, 2 * ATT_BQ, HEAD_DIM), BF16),
        ],
        compiler_params=_cparams(("arbitrary", "arbitrary")),
        name="diff_attention",
    )(rel_bias, proj_qk, proj_qk, proj_plain, lam_params, g_subln)


def _mix_kernel(ga_ref, gb_ref, oa_ref, od_ref, x_ref, wo_ref, gffn_ref, wr_ref, br_ref,
                x1_ref, h2_ref, topi_ref, topw_ref, rank_ref, cnt_ref, carry_ref):
    i = pl.program_id(0)
    tm = MIX_TM

    @pl.when(i == 0)
    def _():
        carry_ref[...] = jnp.zeros_like(carry_ref)

    tp = tm // MIX_PARTS
    parts = range(MIX_PARTS)
    rows = [slice(pp * tp, (pp + 1) * tp) for pp in parts]
    mixes = [ga_ref[rs, :] * oa_ref[rs, :] + gb_ref[rs, :] * od_ref[rs, :] for rs in rows]
    x1s = [x_ref[rs, :] + jnp.dot(mx, wo_ref[...], preferred_element_type=F32) for rs, mx in zip(rows, mixes)]
    for rs, x1 in zip(rows, x1s):
        x1_ref[rs, :] = x1
    h2s = [x1 * lax.rsqrt(jnp.mean(x1 * x1, axis=-1, keepdims=True) + EPS) * gffn_ref[...] for x1 in x1s]
    for rs, h2 in zip(rows, h2s):
        h2_ref[rs, :] = _pack_halves(h2)

    curs = [lax.dot_general(wr_ref[...], h2, (((1,), (1,)), ((), ())), preferred_element_type=F32,
                            precision=lax.Precision.HIGHEST) + br_ref[...] for h2 in h2s]
    eidx = lax.broadcasted_iota(jnp.int32, curs[0].shape, 0).astype(F32)
    vals = [[] for _ in parts]
    hots = [[] for _ in parts]
    for kk in range(TOP_K):
        mxs = [jnp.max(cur, axis=0, keepdims=True) for cur in curs]
        idxs = [jnp.min(jnp.where(cur == mx, eidx, float(N_EXPERTS)), axis=0, keepdims=True)
                for cur, mx in zip(curs, mxs)]
        for pp in parts:
            hot = eidx == idxs[pp]
            vals[pp].append(mxs[pp])
            hots[pp].append(hot)
            topi_ref[kk:kk + 1, rows[pp]] = idxs[pp].astype(jnp.int32)
            curs[pp] = jnp.where(hot, -jnp.inf, curs[pp])
    for pp in parts:
        exps = [jnp.exp(vv - vals[pp][0]) for vv in vals[pp]]
        denom = exps[0] + exps[1] + exps[2] + exps[3]
        for kk in range(TOP_K):
            topw_ref[kk:kk + 1, rows[pp]] = exps[kk] / denom

    r = lax.broadcasted_iota(jnp.int32, (tp, tp), 0)
    c = lax.broadcasted_iota(jnp.int32, (tp, tp), 1)
    earlier = (r < c).astype(F32)
    sel_fs = []
    for pp in parts:
        sel = hots[pp][0]
        for kk in range(1, TOP_K):
            sel = jnp.logical_or(sel, hots[pp][kk])
        sel_fs.append(sel.astype(F32))
    within = [_bdot(sf, earlier) for sf in sel_fs]
    totals = [jnp.sum(sf, axis=-1, keepdims=True) for sf in sel_fs]
    run = carry_ref[...]
    for pp in parts:
        before = within[pp] + run
        for kk in range(TOP_K):
            rank_ref[kk:kk + 1, rows[pp]] = jnp.sum(jnp.where(hots[pp][kk], before, 0.0), axis=0,
                                                    keepdims=True).astype(jnp.int32)
        run = run + totals[pp]
    carry_ref[...] = run
    cnt_ref[...] = run.astype(jnp.int32)


def _mix_project_route(proj_gate, oa, od, x2d, w_o, g_ffn, w_r_t, b_r, d_model):
    t = x2d.shape[0]
    tm = MIX_TM
    full = lambda shape: pl.BlockSpec(shape, lambda i: (0, 0))
    row = lambda: pl.BlockSpec((tm, d_model), lambda i: (i, 0))
    krow = lambda: pl.BlockSpec((TOP_K, tm), lambda i: (0, i))
    return pl.pallas_call(
        _mix_kernel,
        grid=(t // tm,),
        in_specs=[
            pl.BlockSpec((tm, d_model), lambda i: (i, 0)),
            pl.BlockSpec((tm, d_model), lambda i: (i, 1)),
            row(), row(), row(),
            full((d_model, d_model)), full((1, d_model)), full((N_EXPERTS, d_model)), full((N_EXPERTS, 1)),
        ],
        out_specs=[row(), pl.BlockSpec((tm, d_model // 2), lambda i: (i, 0)),
                   krow(), krow(), krow(), full((N_EXPERTS, 1))],
        out_shape=[
            jax.ShapeDtypeStruct((t, d_model), F32),
            jax.ShapeDtypeStruct((t, d_model // 2), jnp.int32),
            jax.ShapeDtypeStruct((TOP_K, t), jnp.int32),
            jax.ShapeDtypeStruct((TOP_K, t), F32),
            jax.ShapeDtypeStruct((TOP_K, t), jnp.int32),
            jax.ShapeDtypeStruct((N_EXPERTS, 1), jnp.int32),
        ],
        scratch_shapes=[pltpu.VMEM((N_EXPERTS, 1), F32)],
        compiler_params=_cparams(("arbitrary",)),
        name="merge_outproj_route",
    )(proj_gate, proj_gate, oa, od, x2d, w_o, g_ffn, w_r_t, b_r)


def _pack_halves(x):
    half = x.shape[1] // 2
    bits = pltpu.bitcast(x.astype(BF16).astype(F32), jnp.int32)
    return bits[:, :half] | lax.shift_right_logical(bits[:, half:], 16)


def _unpack_halves(p):
    hi = pltpu.bitcast(p & jnp.int32(-65536), F32)
    lo = pltpu.bitcast(lax.shift_left(p, 16), F32)
    return jnp.concatenate([hi, lo], axis=1)


def _expert_kernel(be_ref, nu_ref, ne_ref, x_ref, wup_hbm, bup_ref, wdn_hbm, bdn_ref, *rest):
    y_ref, wup_bf, wdn_bf, wup_f, wdn_f, sem = rest[-6:]
    i = pl.program_id(0)
    d_ff = wdn_f.shape[0]

    def weight_copies(e):
        return (pltpu.make_async_copy(wup_hbm.at[e], wup_f, sem.at[0]),
                pltpu.make_async_copy(wdn_hbm.at[e], wdn_f, sem.at[1]))

    @pl.when(i == 0)
    def _():
        for cp in weight_copies(be_ref[0]):
            cp.start()

    @pl.when(jnp.logical_or(i == 0, be_ref[i] != be_ref[jnp.maximum(i - 1, 0)]))
    def _():
        for cp in weight_copies(be_ref[i]):
            cp.wait()
        rr = lax.broadcasted_iota(jnp.int32, (2 * LANES, 2 * LANES), 0)
        cc = lax.broadcasted_iota(jnp.int32, (2 * LANES, 2 * LANES), 1)
        pick = jnp.where(cc < LANES, 2 * cc, 2 * (cc - LANES) + 1)
        perm = (rr == pick).astype(BF16)
        for g in range(wup_f.shape[1] // (2 * LANES)):
            cs = slice(g * 2 * LANES, (g + 1) * 2 * LANES)
            wup_bf[:, cs] = jnp.dot(wup_f[:, cs].astype(BF16), perm,
                                    preferred_element_type=F32).astype(BF16)
        wdn_bf[...] = wdn_f[...].astype(BF16)

        @pl.when(ne_ref[i] >= 0)
        def _():
            for cp in weight_copies(ne_ref[i]):
                cp.start()

    @pl.when(i < nu_ref[0])
    def _():
        x = _unpack_halves(x_ref[...])
        hid = jnp.dot(x.astype(BF16), wup_bf[...], preferred_element_type=F32) + bup_ref[0]
        acts = []
        for g in range(hid.shape[1] // (2 * LANES)):
            glu = jnp.minimum(hid[:, g * 2 * LANES:g * 2 * LANES + LANES], SWIGLU_LIMIT)
            lin = jnp.clip(hid[:, g * 2 * LANES + LANES:(g + 1) * 2 * LANES], -SWIGLU_LIMIT, SWIGLU_LIMIT)
            acts.append(glu * _sigmoid(SWIGLU_ALPHA * glu) * (lin + 1.0))
        act = jnp.concatenate(acts, axis=1)
        assert act.shape[1] == d_ff
        y = jnp.dot(act.astype(BF16), wdn_bf[...], preferred_element_type=F32) + bdn_ref[0]
        y_ref[...] = _pack_halves(y)

    @pl.when(i >= nu_ref[0])
    def _():
        y_ref[...] = jnp.zeros(y_ref.shape, y_ref.dtype)


def _experts(block_e, n_used, xs, y_prev, first_block, n_rows_total, w_up, b_up, w_down, b_down):
    n_rows, half = xs.shape
    d = w_up.shape[1]
    nb = n_rows // MOE_RB
    two_ff = w_up.shape[2]
    d_ff = w_down.shape[1]
    blk = jnp.arange(nb, dtype=jnp.int32)
    later = jnp.logical_and(blk[None, :] > blk[:, None], block_e[None, :] != block_e[:, None])
    next_e = jnp.min(jnp.where(later, block_e[None, :], N_EXPERTS), axis=1)
    next_e = jnp.where(next_e == N_EXPERTS, -1, next_e).astype(jnp.int32)
    in_specs = [
        pl.BlockSpec((MOE_RB, half), lambda i, be, nu, ne: (jnp.maximum(jnp.minimum(i, nu[0] - 1), 0), 0)),
        pl.BlockSpec(memory_space=pl.ANY),
        pl.BlockSpec((1, 1, two_ff), lambda i, be, nu, ne: (be[i], 0, 0)),
        pl.BlockSpec(memory_space=pl.ANY),
        pl.BlockSpec((1, 1, d), lambda i, be, nu, ne: (be[i], 0, 0)),
    ]
    operands = [block_e, n_used, next_e, xs, w_up, b_up, w_down, b_down]
    aliases = {}
    if y_prev is not None:
        in_specs.append(pl.BlockSpec(memory_space=pl.ANY))
        aliases = {len(operands): 0}
        operands.append(y_prev)
    grid_spec = pltpu.PrefetchScalarGridSpec(
        num_scalar_prefetch=3,
        grid=(nb,),
        in_specs=in_specs,
        out_specs=pl.BlockSpec((MOE_RB, half), lambda i, be, nu, ne: (first_block + i, 0)),
        scratch_shapes=[pltpu.VMEM((d, two_ff), BF16), pltpu.VMEM((d_ff, d), BF16),
                        pltpu.VMEM((d, two_ff), F32), pltpu.VMEM((d_ff, d), F32),
                        pltpu.SemaphoreType.DMA((2,))],
    )
    return pl.pallas_call(
        _expert_kernel,
        grid_spec=grid_spec,
        out_shape=jax.ShapeDtypeStruct((n_rows_total, half), jnp.int32),
        input_output_aliases=aliases,
        compiler_params=_cparams(("arbitrary",)),
        name="moe_experts",
    )(*operands)


def _sc_invert_slots(dest_flat, n_rows):
    n_assign = dest_flat.shape[0]
    n_workers = SC_CORES * SC_SUBCORES
    rows_per_w = n_rows // n_workers
    chunk = SC_SCAN_CHUNK
    assert n_rows % n_workers == 0 and rows_per_w % SC_LANES == 0 and n_assign % chunk == 0
    mesh = plsc.VectorSubcoreMesh(core_axis_name="c", subcore_axis_name="s",
                                  num_cores=SC_CORES, num_subcores=SC_SUBCORES)

    n_chunks = n_assign // chunk
    assert n_chunks % 2 == 0 and chunk % (SC_SCAN_UNROLL * SC_LANES) == 0

    def body(dest_hbm, out_hbm, dest_a, dest_b, map_v, sem_a, sem_b):
        wid = lax.axis_index("s") * SC_CORES + lax.axis_index("c")
        base = wid * rows_per_w
        lanes = lax.broadcasted_iota(jnp.int32, (SC_LANES,), 0)

        def fetch(ci, dest_v, sem):
            off = pl.multiple_of(ci * chunk, chunk)
            return pltpu.make_async_copy(dest_hbm.at[pl.ds(off, chunk)], dest_v, sem)

        def scan(ci, dest_v):
            @pl.loop(0, chunk, step=SC_SCAN_UNROLL * SC_LANES)
            def _(j0):
                for u in range(SC_SCAN_UNROLL):
                    j = j0 + u * SC_LANES
                    local = dest_v[pl.ds(j, SC_LANES)] - base
                    mine = jnp.logical_and(local >= 0, local < rows_per_w)
                    plsc.store_scatter(map_v, [jnp.where(mine, local, 0)], ci * chunk + j + lanes, mask=mine)

        fetch(0, dest_a, sem_a).start()

        @pl.loop(0, rows_per_w, step=SC_LANES)
        def _(r0):
            map_v[pl.ds(r0, SC_LANES)] = jnp.full((SC_LANES,), -1, jnp.int32)

        @pl.loop(0, n_chunks, step=2)
        def _(ci):
            fetch(ci + 1, dest_b, sem_b).start()
            fetch(ci, dest_a, sem_a).wait()
            scan(ci, dest_a)
            nxt = jnp.minimum(ci + 2, n_chunks - 1)
            fetch(nxt, dest_a, sem_a).start()
            fetch(ci + 1, dest_b, sem_b).wait()
            scan(ci + 1, dest_b)

        fetch(n_chunks - 1, dest_a, sem_a).wait()
        pltpu.sync_copy(map_v, out_hbm.at[pl.ds(base, rows_per_w)])

    return pl.kernel(
        body,
        out_type=jax.ShapeDtypeStruct((n_rows,), jnp.int32),
        mesh=mesh,
        scratch_types=[pltpu.VMEM((chunk,), jnp.int32), pltpu.VMEM((chunk,), jnp.int32),
                       pltpu.VMEM((rows_per_w,), jnp.int32),
                       pltpu.SemaphoreType.DMA, pltpu.SemaphoreType.DMA],
        compiler_params=pltpu.CompilerParams(needs_layout_passes=False),
        name="moe_slot_inverse",
    )(dest_flat)


def _sc_gather_rows(table, idx):
    n_idx = idx.shape[0]
    d = table.shape[1]
    n_workers = SC_CORES * SC_SUBCORES
    per_worker = n_idx // n_workers
    n_chunks = per_worker // SC_GATHER_ROWS
    assert n_idx % n_workers == 0 and per_worker % SC_GATHER_ROWS == 0
    mesh = plsc.VectorSubcoreMesh(core_axis_name="c", subcore_axis_name="s",
                                  num_cores=SC_CORES, num_subcores=SC_SUBCORES)

    assert n_chunks % 2 == 0

    def body(table_hbm, idx_hbm, out_hbm, idx_v, rows_a, rows_b, sem_a, sem_b):
        wid = lax.axis_index("s") * SC_CORES + lax.axis_index("c")
        base = wid * per_worker
        pltpu.sync_copy(idx_hbm.at[pl.ds(base, per_worker)], idx_v)

        def gather(ci, rows_v, sem):
            off = pl.multiple_of(ci * SC_GATHER_ROWS, SC_GATHER_ROWS)
            return pltpu.make_async_copy(table_hbm.at[idx_v.at[pl.ds(off, SC_GATHER_ROWS)]], rows_v, sem)

        def put(ci, rows_v):
            off = pl.multiple_of(ci * SC_GATHER_ROWS, SC_GATHER_ROWS)
            pltpu.sync_copy(rows_v, out_hbm.at[pl.ds(base + off, SC_GATHER_ROWS)])

        gather(0, rows_a, sem_a).start()

        @pl.loop(0, n_chunks, step=2)
        def _(ci):
            gather(ci + 1, rows_b, sem_b).start()
            gather(ci, rows_a, sem_a).wait()
            put(ci, rows_a)
            nxt = jnp.minimum(ci + 2, n_chunks - 1)
            gather(nxt, rows_a, sem_a).start()
            gather(ci + 1, rows_b, sem_b).wait()
            put(ci + 1, rows_b)

        gather(n_chunks - 1, rows_a, sem_a).wait()

    return pl.kernel(
        body,
        out_type=jax.ShapeDtypeStruct((n_idx, d), table.dtype),
        mesh=mesh,
        scratch_types=[
            pltpu.VMEM((per_worker,), jnp.int32),
            pltpu.VMEM((SC_GATHER_ROWS, d), table.dtype),
            pltpu.VMEM((SC_GATHER_ROWS, d), table.dtype),
            pltpu.SemaphoreType.DMA,
            pltpu.SemaphoreType.DMA,
        ],
        name="moe_slot_gather",
    )(table, idx)


def _combine_kernel(x1_ref, w_ref, y0_ref, y1_ref, y2_ref, y3_ref, o_ref):
    w = w_ref[...]
    out = x1_ref[...]
    for kk, y_ref in enumerate((y0_ref, y1_ref, y2_ref, y3_ref)):
        out = out + w[:, kk:kk + 1] * _unpack_halves(y_ref[...])
    o_ref[...] = out


def _combine(x1, w_tok, y_slots):
    t, d = x1.shape
    tc = COMB_TC
    nt = t // tc
    yspec = lambda kk: pl.BlockSpec((tc, d // 2), lambda i: (kk * nt + i, 0))
    return pl.pallas_call(
        _combine_kernel,
        grid=(nt,),
        in_specs=[
            pl.BlockSpec((tc, d), lambda i: (i, 0)),
            pl.BlockSpec((tc, TOP_K), lambda i: (i, 0)),
            yspec(0), yspec(1), yspec(2), yspec(3),
        ],
        out_specs=pl.BlockSpec((tc, d), lambda i: (i, 0)),
        out_shape=jax.ShapeDtypeStruct((t, d), F32),
        compiler_params=_cparams(("parallel",)),
        name="moe_combine",
    )(x1, w_tok, y_slots, y_slots, y_slots, y_slots)


def _moe(x1, h2, topi, topw, rank, counts, w_up, b_up, w_down, b_down):
    t, d = x1.shape
    n_assign = t * TOP_K
    nb = -(-n_assign // MOE_RB) + N_EXPERTS
    n_rows = nb * MOE_RB
    counts = counts[:, 0]
    padded = (counts + MOE_RB - 1) // MOE_RB * MOE_RB
    padded_end = jnp.cumsum(padded)
    padded_start = padded_end - padded
    expert_ids = jnp.arange(N_EXPERTS, dtype=jnp.int32)[:, None, None]
    start_of = jnp.sum(jnp.where(topi[None] == expert_ids, padded_start[:, None, None], 0), axis=0)
    dest = (start_of + rank).astype(jnp.int32)
    n_used = (padded_end[-1] // MOE_RB).astype(jnp.int32)
    blk = jnp.minimum(jnp.arange(nb, dtype=jnp.int32), n_used - 1)
    block_e = jnp.minimum(jnp.sum(padded_end[None, :] <= (blk * MOE_RB)[:, None], axis=1),
                          N_EXPERTS - 1).astype(jnp.int32)
    slot_of = _sc_invert_slots(dest.reshape(-1), n_rows)
    src_tok = jnp.where(slot_of < 0, jnp.arange(n_rows, dtype=jnp.int32), slot_of) % t

    nb_a = nb // 4
    y_rows = None
    for first, n_blk in ((0, nb_a), (nb_a, nb - nb_a)):
        xs = _sc_gather_rows(h2, lax.slice(src_tok, (first * MOE_RB,), ((first + n_blk) * MOE_RB,)))
        used = jnp.clip(n_used - first, 0, n_blk).reshape(1)
        y_rows = _experts(lax.slice(block_e, (first,), (first + n_blk,)), used, xs, y_rows, first, n_rows,
                          w_up, b_up, w_down, b_down)
    y_slots = _sc_gather_rows(y_rows, dest.reshape(-1))
    return _combine(x1, topw.T, y_slots)


def kernel(x, g_mix, w_in, b_gate, conv_w, a_log, dt_bias, g_delta_out, q_norm, k_norm, lambda_q1, lambda_k1, lambda_q2, lambda_k2, g_subln, rel_bias, w_o, g_ffn, w_router, b_router, w_up, b_up, w_down, b_down):
    bsz, seq, d = x.shape
    depth = g_mix.shape[0]
    n_heads = d // HEAD_DIM
    t = bsz * seq
    d_ff = w_down.shape[2]
    assert d % PROJ_TN == 0 and t % PROJ_TM == 0 and seq % GDN_TB == 0 and seq % ATT_BQ == 0
    assert t % MIX_TM == 0 and t % COMB_TC == 0 and n_heads % GDN_HG == 0
    assert (t * TOP_K) % MOE_RB == 0
    assert 2 * n_heads <= 2 * SUBLANES

    x2d = x.reshape(t, d)
    for l in range(depth):
        wl = w_in[l]
        c0 = 4 * d
        c1 = c0 + 2 * n_heads
        c2 = c1 + 2 * d
        c3 = c2 + d
        w_small = jnp.pad(wl[:, c0:c1], ((0, 0), (0, LANES - 2 * n_heads)))
        gm = g_mix[l].reshape(1, d)
        head_pad = jnp.zeros((LANES - 2 * n_heads,), F32)
        alog = jnp.concatenate([jnp.zeros((n_heads,), F32), a_log[l], head_pad])
        dtb = jnp.concatenate([jnp.zeros((n_heads,), F32), dt_bias[l], head_pad])
        rows_t = 2 * n_heads
        beta_decay = (w_small.astype(BF16), w_small[:, :rows_t].T.astype(BF16),
                      alog.reshape(1, LANES), dtb.reshape(1, LANES),
                      alog[:rows_t].reshape(rows_t, 1), dtb[:rows_t].reshape(rows_t, 1), n_heads)
        w_plain = jnp.concatenate([wl[:, :c0], wl[:, c2:c3]], axis=1).astype(BF16)
        proj_plain, small, small_t = _input_projection(x2d, gm, w_plain, jnp.zeros((1, 5 * d), F32), "plain",
                                                       beta_decay)
        qk_gain = jnp.concatenate([jnp.tile(q_norm[l] * (DH_DIFF ** -0.5), 2 * n_heads),
                                   jnp.tile(k_norm[l], 2 * n_heads)]).reshape(1, 2 * d)
        proj_qk = _input_projection(x2d, gm, wl[:, c1:c2].astype(BF16), qk_gain, "qknorm")
        proj_gate = _input_projection(x2d, gm, wl[:, c3:].astype(BF16), b_gate[l].reshape(1, 2 * d), "gate")

        oa = _gated_delta(proj_plain, small, small_t, conv_w[l], g_delta_out[l].reshape(1, HEAD_DIM),
                          bsz, seq, n_heads, d)

        lam_init = 0.8 - 0.6 * math.exp(-0.3 * l)
        lam_params = jnp.stack([lambda_q1[l], lambda_k1[l], lambda_q2[l], lambda_k2[l]])
        od = _diff_attention(proj_qk, proj_plain, rel_bias, lam_params, g_subln[l].reshape(1, HEAD_DIM),
                             bsz, seq, n_heads, d, lam_init)

        x1, h2, topi, topw, rank, counts = _mix_project_route(
            proj_gate, oa, od, x2d, w_o[l].astype(BF16), g_ffn[l].reshape(1, d),
            w_router[l].T, b_router[l].reshape(N_EXPERTS, 1), d)

        b_up_l = b_up[l].reshape(N_EXPERTS, 2 * d_ff // (2 * LANES), LANES, 2)
        b_up_l = jnp.swapaxes(b_up_l, 2, 3).reshape(N_EXPERTS, 1, 2 * d_ff)
        x2d = _moe(x1, h2, topi, topw, rank, counts, w_up[l], b_up_l,
                   w_down[l], b_down[l].reshape(N_EXPERTS, 1, d))
    return x2d.reshape(bsz, seq, d)
```

```python
import functools
import math

import jax
import jax.numpy as jnp
from jax import lax
from jax.experimental import pallas as pl
from jax.experimental.pallas import tpu as pltpu
from jax.experimental.pallas import tpu_sc as plsc

F32 = jnp.float32
BF16 = jnp.bfloat16

HEAD_DIM = 128
DH_DIFF = HEAD_DIM // 2
CONV_WIDTH = 4
CHUNK = 64
N_BUCKETS = 32
MAX_DISTANCE = 128
N_EXPERTS = 32
TOP_K = 4
SWIGLU_LIMIT = 7.0
SWIGLU_ALPHA = 1.702
EPS = 1e-6
NEG_BIG = -1e30

LANES = 128
SUBLANES = 8
VMEM_LIMIT = 56 * 1024 * 1024
SC_CORES = 2
SC_SUBCORES = 16
SC_LANES = 16
SC_GATHER_ROWS = 64
SC_SCAN_CHUNK = 4096
SC_SCAN_UNROLL = 4

PROJ_TM = 2048
PROJ_TN = 1024
PROJ_CHUNK = 256
GDN_TB = 256
GDN_HG = 8
ATT_HG = 1
ATT_BQ = 512
ATT_BK = 512
MIX_TM = 1024
MIX_PARTS = 2
MOE_RB = 512
COMB_TC = 512


def _cparams(sem):
    return pltpu.CompilerParams(dimension_semantics=sem, vmem_limit_bytes=VMEM_LIMIT)


def _sigmoid(x):
    return 0.5 * jnp.tanh(0.5 * x) + 0.5


def _bdot(a, b):
    return jnp.dot(a.astype(BF16), b.astype(BF16), preferred_element_type=F32)


def _bdot_nt(a, b):
    return lax.dot_general(a.astype(BF16), b.astype(BF16), (((1,), (1,)), ((), ())),
                           preferred_element_type=F32)


def _bdot_tn(a, b):
    return lax.dot_general(a.astype(BF16), b.astype(BF16), (((0,), (0,)), ((), ())),
                           preferred_element_type=F32)


def _beta_decay(acc, idx, alog, dtb, n_heads):
    beta = _sigmoid(acc)
    z = acc + dtb
    softplus = jnp.maximum(z, 0.0) + jnp.log1p(jnp.exp(-jnp.abs(z)))
    gdec = -jnp.exp(alog) * softplus
    return jnp.where(idx < n_heads, beta, jnp.where(idx < 2 * n_heads, gdec, 0.0))


def _proj_kernel(x_ref, g_ref, w_ref, aux_ref, *rest, mode, n_heads):
    o_ref, h_ref = rest[-2:] if n_heads is None else (rest[6], rest[-1])

    @pl.when(pl.program_id(1) == 0)
    def _():
        x = x_ref[...]
        ms = jnp.mean(x * x, axis=-1, keepdims=True)
        h_ref[...] = (x * lax.rsqrt(ms + EPS) * g_ref[...]).astype(BF16)
        if n_heads is not None:
            ws_ref, wst_ref, alog_ref, dtb_ref, alog_t_ref, dtb_t_ref, _, os_ref, ost_ref, _ = rest
            hb = h_ref[...]
            acc = jnp.dot(hb, ws_ref[...], preferred_element_type=F32)
            lane = lax.broadcasted_iota(jnp.int32, acc.shape, 1)
            os_ref[...] = _beta_decay(acc, lane, alog_ref[...], dtb_ref[...], n_heads)
            acc_t = lax.dot_general(wst_ref[...], hb, (((1,), (1,)), ((), ())),
                                    preferred_element_type=F32)
            sub = lax.broadcasted_iota(jnp.int32, acc_t.shape, 0)
            ost_ref[...] = _beta_decay(acc_t, sub, alog_t_ref[...], dtb_t_ref[...], n_heads)

    h = h_ref[...]
    lo = lax.broadcasted_iota(jnp.int32, (1, LANES), 1) < DH_DIFF
    for c in range(PROJ_TN // PROJ_CHUNK):
        cs = slice(c * PROJ_CHUNK, (c + 1) * PROJ_CHUNK)
        acc = jnp.dot(h, w_ref[:, cs], preferred_element_type=F32)
        if mode == "plain":
            o_ref[:, cs] = acc.astype(o_ref.dtype)
        elif mode == "gate":
            o_ref[:, cs] = _sigmoid(acc + aux_ref[:, cs]).astype(o_ref.dtype)
        else:
            for g in range(PROJ_CHUNK // LANES):
                sl = slice(c * PROJ_CHUNK + g * LANES, c * PROJ_CHUNK + (g + 1) * LANES)
                y = acc[:, g * LANES:(g + 1) * LANES]
                y2 = y * y
                s_lo = jnp.sum(jnp.where(lo, y2, 0.0), axis=-1, keepdims=True)
                s_hi = jnp.sum(jnp.where(lo, 0.0, y2), axis=-1, keepdims=True)
                r = jnp.where(lo, lax.rsqrt(s_lo / DH_DIFF + EPS), lax.rsqrt(s_hi / DH_DIFF + EPS))
                o_ref[:, sl] = (y * r * aux_ref[:, sl]).astype(o_ref.dtype)


def _input_projection(x2d, g_mix, w, aux, mode, beta_decay=None):
    t, d = x2d.shape
    n = w.shape[1]
    full = lambda shape: pl.BlockSpec(shape, lambda i, j: (0, 0))
    in_specs = [
        pl.BlockSpec((PROJ_TM, d), lambda i, j: (i, 0)),
        full((1, d)),
        pl.BlockSpec((d, PROJ_TN), lambda i, j: (0, j)),
        pl.BlockSpec((1, PROJ_TN), lambda i, j: (0, j)),
    ]
    operands = [x2d, g_mix, w, aux]
    out_specs = [pl.BlockSpec((PROJ_TM, PROJ_TN), lambda i, j: (i, j))]
    out_shape = [jax.ShapeDtypeStruct((t, n), BF16)]
    n_heads = None
    if beta_decay is not None:
        n_heads = beta_decay[-1]
        rows_t = 2 * n_heads
        in_specs += [full((d, LANES)), full((rows_t, d)), full((1, LANES)), full((1, LANES)),
                     full((rows_t, 1)), full((rows_t, 1))]
        operands += list(beta_decay[:-1])
        out_specs += [pl.BlockSpec((PROJ_TM, LANES), lambda i, j: (i, 0)),
                      pl.BlockSpec((rows_t, PROJ_TM), lambda i, j: (0, i))]
        out_shape += [jax.ShapeDtypeStruct((t, LANES), F32), jax.ShapeDtypeStruct((rows_t, t), F32)]
    out = pl.pallas_call(
        functools.partial(_proj_kernel, mode=mode, n_heads=n_heads),
        grid=(t // PROJ_TM, n // PROJ_TN),
        in_specs=in_specs,
        out_specs=out_specs,
        out_shape=out_shape,
        scratch_shapes=[pltpu.VMEM((PROJ_TM, d), BF16)],
        compiler_params=_cparams(("parallel", "arbitrary")),
        name="input_projection_" + mode,
    )(*operands)
    return out[0] if beta_decay is None else out


def _gdn_kernel(q_ref, k_ref, v_ref, z_ref, sm_ref, smt_ref, cwq_ref, cwk_ref, cwv_ref, gout_ref,
                o_ref, state_ref, qp_ref, kp_ref, vp_ref, vn_ref, *, n_heads):
    hg = pl.program_id(1)
    s = pl.program_id(2)
    tb = GDN_TB
    pad = SUBLANES
    width = GDN_HG * HEAD_DIM

    @pl.when(s == 0)
    def _():
        state_ref[...] = jnp.zeros_like(state_ref)
        for p_ref in (qp_ref, kp_ref, vp_ref):
            p_ref[0:pad, :] = jnp.zeros((pad, width), F32)

    r = lax.broadcasted_iota(jnp.int32, (tb, tb), 0)
    c = lax.broadcasted_iota(jnp.int32, (tb, tb), 1)
    delay_mat = jnp.concatenate([(r - c == dd).astype(BF16) for dd in range(1, CONV_WIDTH)], axis=0)

    def conv_silu(x_ref, p_ref, cw_ref):
        x = x_ref[...]
        xf = x.astype(F32)
        p_ref[pad:2 * pad, :] = xf[0:pad]
        delayed = jnp.dot(delay_mat, x, preferred_element_type=F32)
        acc = cw_ref[CONV_WIDTH - 1:CONV_WIDTH, :] * xf
        for dd in range(1, CONV_WIDTH):
            first = p_ref[pad - dd:2 * pad - dd, :]
            xd = jnp.concatenate([first, delayed[(dd - 1) * tb + pad:dd * tb]], axis=0)
            acc = acc + cw_ref[CONV_WIDTH - 1 - dd:CONV_WIDTH - dd, :] * xd
        p_ref[0:pad, :] = xf[tb - pad:tb]
        return acc * _sigmoid(acc)

    q_all = conv_silu(q_ref, qp_ref, cwq_ref)
    k_all = conv_silu(k_ref, kp_ref, cwk_ref)
    v_all = conv_silu(v_ref, vp_ref, cwv_ref)

    shift = int(math.log2(CHUNK))
    same = (r >> shift) == (c >> shift)
    incl = jnp.logical_and(same, c <= r)
    strict = jnp.logical_and(same, c < r)

    small = sm_ref[...]
    small_t = smt_ref[...]
    lane = lax.broadcasted_iota(jnp.int32, small.shape, 1)
    def split3(a):
        hi = a.astype(BF16)
        r1 = a - hi.astype(F32)
        mid = r1.astype(BF16)
        lo = (r1 - mid.astype(F32)).astype(BF16)
        return hi.astype(F32), mid.astype(F32), lo.astype(F32)

    part = 2 * n_heads
    s_hi, s_mid, s_lo = split3(small)
    small3 = jnp.where(lane < part, s_hi,
                       jnp.where(lane < 2 * part, pltpu.roll(s_mid, part, 1),
                                 jnp.where(lane < 3 * part, pltpu.roll(s_lo, 2 * part, 1), 0.0)))
    both = _bdot(jnp.concatenate([incl.astype(F32), same.astype(F32)], axis=0), small3)
    gcum = both[:tb]
    gtot = both[tb:]
    gcum_t = _bdot(jnp.concatenate(split3(small_t), axis=0),
                   jnp.logical_and(same, r <= c).astype(F32))
    sub3 = lax.broadcasted_iota(jnp.int32, gcum_t.shape, 0)

    heads = range(GDN_HG)
    hsl = [slice(hh * HEAD_DIM, (hh + 1) * HEAD_DIM) for hh in heads]
    qs = [q_all[:, hs] for hs in hsl]
    ks = [k_all[:, hs] for hs in hsl]
    vs = [v_all[:, hs] for hs in hsl]
    qs = [q * lax.rsqrt(jnp.sum(q * q, axis=-1, keepdims=True) + EPS) * (HEAD_DIM ** -0.5) for q in qs]
    ks = [k * lax.rsqrt(jnp.sum(k * k, axis=-1, keepdims=True) + EPS) for k in ks]

    def col_of(arr, idx):
        return jnp.sum(jnp.where(lane == idx, arr, 0.0), axis=-1, keepdims=True)

    def terms_of(pos, idx):
        return jnp.logical_or(pos == idx, jnp.logical_or(pos == idx + part, pos == idx + 2 * part))

    head_ids = [hg * GDN_HG + hh for hh in heads]
    betas = [col_of(small, hd) for hd in head_ids]
    gcs = [jnp.sum(jnp.where(terms_of(lane, hd + n_heads), gcum, 0.0), axis=-1, keepdims=True)
           for hd in head_ids]
    gls = [jnp.sum(jnp.where(terms_of(lane, hd + n_heads), gtot, 0.0), axis=-1, keepdims=True)
           for hd in head_ids]
    gc_rows = [jnp.sum(jnp.where(terms_of(sub3, hd + n_heads), gcum_t, 0.0), axis=0, keepdims=True)
               for hd in head_ids]

    decays = [jnp.where(incl, jnp.exp(jnp.minimum(gc - gr, 0.0)), 0.0) for gc, gr in zip(gcs, gc_rows)]
    kbs = [k * b for k, b in zip(ks, betas)]
    kks = [_bdot_nt(kb, k) for kb, k in zip(kbs, ks)]
    pws = [jnp.where(strict, -(kk * dc), 0.0) for kk, dc in zip(kks, decays)]
    n_chunks = tb // CHUNK
    cat_row = lax.broadcasted_iota(jnp.int32, (CHUNK, tb), 0)
    cat_lane = lax.broadcasted_iota(jnp.int32, (CHUNK, tb), 1)
    lane_chunk = cat_lane >> shift

    def block_diag(m_cat):
        return jnp.concatenate([jnp.where(lane_chunk == ci, m_cat, 0.0) for ci in range(n_chunks)], axis=0)

    def cat_of(m_bd):
        out = m_bd[0:CHUNK]
        for ci in range(1, n_chunks):
            out = out + m_bd[ci * CHUNK:(ci + 1) * CHUNK]
        return out

    pcats = [cat_of(pw) for pw in pws]
    eye_cat = ((cat_lane & (CHUNK - 1)) == cat_row).astype(F32)
    tcats = [eye_cat + pc for pc in pcats]
    pcats = [_bdot(pc, block_diag(pc)) for pc in pcats]
    n_levels = int(math.log2(CHUNK))
    for lev in range(1, n_levels):
        bds = [block_diag(pc) for pc in pcats]
        if lev < n_levels - 1:
            prods = [_bdot(jnp.concatenate([pc, tc], axis=0), bd) for pc, tc, bd in zip(pcats, tcats, bds)]
            pcats = [pr[:CHUNK] for pr in prods]
            tcats = [tc + pr[CHUNK:] for tc, pr in zip(tcats, prods)]
        else:
            tcats = [tc + _bdot(tc, bd) for tc, bd in zip(tcats, bds)]
    tmats = [block_diag(tc) for tc in tcats]
    egcs = [jnp.exp(gc) for gc in gcs]
    uws = [_bdot(tm, jnp.concatenate([v * b, kb * eg], axis=1))
           for tm, v, b, kb, eg in zip(tmats, vs, betas, kbs, egcs)]
    us = [uw[:, :HEAD_DIM] for uw in uws]
    ws = [uw[:, HEAD_DIM:] for uw in uws]
    qkm = [_bdot_nt(q, k) for q, k in zip(qs, ks)]
    qkm = [jnp.where(incl, x * dc, 0.0) for x, dc in zip(qkm, decays)]
    q_decs = [q * eg for q, eg in zip(qs, egcs)]
    k_ends = [k * jnp.exp(gl - gc) for k, gl, gc in zip(ks, gls, gcs)]

    for hh in heads:
        vn_ref[hh] = jnp.zeros((tb, HEAD_DIM), F32)
    outs = [[] for _ in heads]
    for ci in range(tb // CHUNK):
        cs = slice(ci * CHUNK, (ci + 1) * CHUNK)
        sts = [state_ref[hh] for hh in heads]
        ws_qs = [_bdot(jnp.concatenate([ws[hh][cs], q_decs[hh][cs]], axis=0), sts[hh]) for hh in heads]
        v_news = [us[hh][cs] - ws_qs[hh][:CHUNK] for hh in heads]
        for hh in heads:
            vn_ref[hh, cs, :] = v_news[hh]
        intra = [_bdot(qkm[hh][cs], vn_ref[hh]) for hh in heads]
        upd = [_bdot_tn(k_ends[hh][cs], v_news[hh]) for hh in heads]
        for hh in heads:
            outs[hh].append(ws_qs[hh][CHUNK:] + intra[hh])
            g_last = gls[hh][ci * CHUNK:ci * CHUNK + 1, :]
            state_ref[hh] = sts[hh] * jnp.exp(g_last) + upd[hh]
    for hh in heads:
        o = jnp.concatenate(outs[hh], axis=0)
        o = o * lax.rsqrt(jnp.mean(o * o, axis=-1, keepdims=True) + EPS) * gout_ref[...]
        zz = z_ref[:, hsl[hh]].astype(F32)
        o_ref[:, hsl[hh]] = (o * (zz * _sigmoid(zz))).astype(o_ref.dtype)


def _gated_delta(big, small, small_t, conv_w, g_out, bsz, seq, n_heads, d_model):
    t = bsz * seq
    tb = GDN_TB
    ns = seq // tb
    width = GDN_HG * HEAD_DIM
    nhg = n_heads // GDN_HG
    blocks_per_group = d_model // width
    rows_t = small_t.shape[0]

    def colspec(group):
        return pl.BlockSpec((tb, width), lambda b, h, s: (b * ns + s, group * blocks_per_group + h))

    def cwspec(group):
        return pl.BlockSpec((CONV_WIDTH, width), lambda b, h, s: (0, group * blocks_per_group + h))

    return pl.pallas_call(
        functools.partial(_gdn_kernel, n_heads=n_heads),
        grid=(bsz, nhg, ns),
        in_specs=[
            colspec(0), colspec(1), colspec(2), colspec(3),
            pl.BlockSpec((tb, LANES), lambda b, h, s: (b * ns + s, 0)),
            pl.BlockSpec((rows_t, tb), lambda b, h, s: (0, b * ns + s)),
            cwspec(0), cwspec(1), cwspec(2),
            pl.BlockSpec((1, HEAD_DIM), lambda b, h, s: (0, 0)),
        ],
        out_specs=pl.BlockSpec((tb, width), lambda b, h, s: (b * ns + s, h)),
        out_shape=jax.ShapeDtypeStruct((t, d_model), BF16),
        scratch_shapes=[
            pltpu.VMEM((GDN_HG, HEAD_DIM, HEAD_DIM), F32),
            pltpu.VMEM((2 * SUBLANES, width), F32),
            pltpu.VMEM((2 * SUBLANES, width), F32),
            pltpu.VMEM((2 * SUBLANES, width), F32),
            pltpu.VMEM((GDN_HG, tb, HEAD_DIM), F32),
        ],
        compiler_params=_cparams(("parallel", "parallel", "arbitrary")),
        name="gated_delta",
    )(big, big, big, big, small, small_t, conv_w, conv_w, conv_w, g_out)


def _t5_bucket(n):
    max_exact = N_BUCKETS // 2
    nf = jnp.maximum(n, 1).astype(F32)
    large = max_exact + (jnp.log(nf / max_exact) / math.log(MAX_DISTANCE / max_exact)
                         * (N_BUCKETS - max_exact)).astype(jnp.int32)
    large = jnp.minimum(large, N_BUCKETS - 1)
    return jnp.where(n < max_exact, n, large)


def _attn_kernel(rb_ref, q_ref, k_ref, v_ref, lam_ref, gsub_ref, o_ref,
                 bias_ref, m_ref, acc_ref, sa_ref, sb_ref, qs_ref, *, lam_init):
    hg = pl.program_id(0)
    b = pl.program_id(1)
    bq, bk = ATT_BQ, ATT_BK
    heads = range(ATT_HG)
    hsl = [slice(hh * HEAD_DIM, (hh + 1) * HEAD_DIM) for hh in heads]

    @pl.when(b == 0)
    def _():
        blk = LANES
        i = lax.broadcasted_iota(jnp.int32, (blk, blk), 0)
        jj = lax.broadcasted_iota(jnp.int32, (blk, blk), 1)
        for hh in heads:
            head = hg * ATT_HG + hh
            far = rb_ref[N_BUCKETS - 1, head]

            def toeplitz(offset):
                bucket = _t5_bucket(jnp.maximum(i - jj + offset, 0))
                out = jnp.zeros((blk, blk), F32)
                for cc in range(N_BUCKETS):
                    out = jnp.where(bucket == cc, rb_ref[cc, head] - far, out)
                return out

            on_diag = jnp.where(i >= jj, toeplitz(0), NEG_BIG)
            next_diag = toeplitz(blk)
            kinds = {0: on_diag, 1: next_diag}
            bias_ref[hh, 2] = jnp.zeros((bq, bk), F32)
            for slot in range(2):
                for rr in range(bq // blk):
                    for cc in range(bk // blk):
                        delta = rr - cc + slot * (bk // blk)
                        if delta < 0:
                            tile = jnp.full((blk, blk), NEG_BIG, F32)
                        else:
                            tile = kinds.get(delta, jnp.zeros((blk, blk), F32))
                        bias_ref[hh, slot, rr * blk:(rr + 1) * blk, cc * blk:(cc + 1) * blk] = tile

    lane = lax.broadcasted_iota(jnp.int32, (bq, HEAD_DIM), 1)
    ones_col = (lax.broadcasted_iota(jnp.int32, (bk, HEAD_DIM), 1) == 0).astype(BF16)
    lam_p = lam_ref[...]
    s1 = jnp.sum(lam_p[0:1] * lam_p[1:2], axis=-1, keepdims=True)
    s2 = jnp.sum(lam_p[2:3] * lam_p[3:4], axis=-1, keepdims=True)
    lam = jnp.exp(s1) - jnp.exp(s2) + lam_init

    n_q = q_ref.shape[0] // bq

    def stack_q(qb):
        rows = pl.ds(pl.multiple_of(qb * bq, bq), bq)
        for hh in heads:
            q = q_ref[rows, hsl[hh]]
            zero = jnp.zeros_like(q)
            qs_ref[hh, 0:bq, :] = jnp.where(lane < DH_DIFF, q, zero)
            qs_ref[hh, bq:2 * bq, :] = jnp.where(lane < DH_DIFF, zero, q)

    def scores(j, s_ref):
        ks = pl.multiple_of(j * bk, bk)
        for hh in heads:
            s_ref[hh] = lax.dot_general(qs_ref[hh], k_ref[pl.ds(ks, bk), hsl[hh]],
                                        (((1,), (1,)), ((), ())), preferred_element_type=F32)

    stack_q(0)
    scores(0, sa_ref)

    def query_block(qi, outer):
        rows = pl.ds(pl.multiple_of(qi * bq, bq), bq)
        m_ref[...] = jnp.full(m_ref.shape, NEG_BIG, F32)
        acc_ref[...] = jnp.zeros(acc_ref.shape, F32)

        def absorb(j, s_ref, biased=True):
            ks = pl.multiple_of(j * bk, bk)
            v_exts = [jnp.concatenate([v_ref[pl.ds(ks, bk), hs], ones_col], axis=1) for hs in hsl]
            if biased:
                slot = jnp.minimum(qi - j, 2)
                scs = [jnp.concatenate([s_ref[hh, 0:bq, :] + bias_ref[hh, slot],
                                        s_ref[hh, bq:2 * bq, :] + bias_ref[hh, slot]], axis=0)
                       for hh in heads]
            else:
                scs = [s_ref[hh] for hh in heads]
            m_olds = [m_ref[hh] for hh in heads]
            m_news = [jnp.maximum(mo, jnp.max(sc, axis=-1, keepdims=True)) for mo, sc in zip(m_olds, scs)]
            ps = [jnp.exp(sc - mn) for sc, mn in zip(scs, m_news)]
            pvs = [jnp.dot(p.astype(BF16), ve, preferred_element_type=F32) for p, ve in zip(ps, v_exts)]
            for hh in heads:
                acc_ref[hh] = jnp.exp(m_olds[hh] - m_news[hh]) * acc_ref[hh] + pvs[hh]
                m_ref[hh] = m_news[hh]

        n_tiles = qi + 1

        def pair_body(jj, carry, biased):
            j0 = 2 * jj
            scores(j0 + 1, sb_ref)
            absorb(j0, sa_ref, biased)
            scores(jnp.minimum(j0 + 2, qi), sa_ref)
            absorb(j0 + 1, sb_ref, biased)
            return carry

        n_far_pairs = jnp.maximum(qi - 1, 0) // 2
        lax.fori_loop(0, n_far_pairs, functools.partial(pair_body, biased=False), 0)
        lax.fori_loop(n_far_pairs, n_tiles // 2, functools.partial(pair_body, biased=True), 0)

        @pl.when(n_tiles % 2 == 1)
        def _():
            absorb(qi, sa_ref)

        stack_q(jnp.minimum(qi + 1, n_q - 1))
        scores(0, sa_ref)
        for hh in heads:
            acc = acc_ref[hh]
            num = acc[:, :HEAD_DIM]
            den = acc[:, HEAD_DIM:HEAD_DIM + 1]
            o = num[:bq] / den[:bq] - lam * (num[bq:] / den[bq:])
            o = o * lax.rsqrt(jnp.mean(o * o, axis=-1, keepdims=True) + EPS) * gsub_ref[...]
            o_ref[rows, hsl[hh]] = (o * (1.0 - lam_init)).astype(o_ref.dtype)
        return outer

    lax.fori_loop(0, n_q, query_block, 0)


def _diff_attention(proj_qk, proj_plain, rel_bias, lam_params, g_subln, bsz, seq, n_heads, d_model,
                    lam_init):
    t = bsz * seq
    assert ATT_BQ == ATT_BK and MAX_DISTANCE <= LANES and n_heads % ATT_HG == 0 and seq % ATT_BQ == 0
    width = ATT_HG * HEAD_DIM
    per = d_model // width
    vcol = 4 * per
    return pl.pallas_call(
        functools.partial(_attn_kernel, lam_init=lam_init),
        grid=(n_heads // ATT_HG, bsz),
        in_specs=[
            pl.BlockSpec(memory_space=pltpu.SMEM),
            pl.BlockSpec((seq, width), lambda h, b: (b, h)),
            pl.BlockSpec((seq, width), lambda h, b: (b, per + h)),
            pl.BlockSpec((seq, width), lambda h, b: (b, vcol + h)),
            pl.BlockSpec((4, DH_DIFF), lambda h, b: (0, 0)),
            pl.BlockSpec((1, HEAD_DIM), lambda h, b: (0, 0)),
        ],
        out_specs=pl.BlockSpec((seq, width), lambda h, b: (b, h)),
        out_shape=jax.ShapeDtypeStruct((t, d_model), BF16),
        scratch_shapes=[
            pltpu.VMEM((ATT_HG, 3, ATT_BQ, ATT_BK), F32),
            pltpu.VMEM((ATT_HG, 2 * ATT_BQ, 1), F32),
            pltpu.VMEM((ATT_HG, 2 * ATT_BQ, 2 * HEAD_DIM), F32),
            pltpu.VMEM((ATT_HG, 2 * ATT_BQ, ATT_BK), F32),
            pltpu.VMEM((ATT_HG, 2 * ATT_BQ, ATT_BK), F32),
            pltpu.VMEM((ATT_HG, 2 * ATT_BQ, HEAD_DIM), BF16),
        ],
        compiler_params=_cparams(("arbitrary", "arbitrary")),
        name="diff_attention",
    )(rel_bias, proj_qk, proj_qk, proj_plain, lam_params, g_subln)


def _mix_kernel(ga_ref, gb_ref, oa_ref, od_ref, x_ref, wo_ref, gffn_ref, wr_ref, br_ref,
                x1_ref, h2_ref, topi_ref, topw_ref, rank_ref, cnt_ref, carry_ref):
    i = pl.program_id(0)
    tm = MIX_TM

    @pl.when(i == 0)
    def _():
        carry_ref[...] = jnp.zeros_like(carry_ref)

    tp = tm // MIX_PARTS
    parts = range(MIX_PARTS)
    rows = [slice(pp * tp, (pp + 1) * tp) for pp in parts]
    mixes = [ga_ref[rs, :] * oa_ref[rs, :] + gb_ref[rs, :] * od_ref[rs, :] for rs in rows]
    x1s = [x_ref[rs, :] + jnp.dot(mx, wo_ref[...], preferred_element_type=F32) for rs, mx in zip(rows, mixes)]
    for rs, x1 in zip(rows, x1s):
        x1_ref[rs, :] = x1
    h2s = [x1 * lax.rsqrt(jnp.mean(x1 * x1, axis=-1, keepdims=True) + EPS) * gffn_ref[...] for x1 in x1s]
    for rs, h2 in zip(rows, h2s):
        h2_ref[rs, :] = _pack_halves(h2)

    curs = [_bdot_nt(wr_ref[...], h2) + br_ref[...] for h2 in h2s]
    eidx = lax.broadcasted_iota(jnp.int32, curs[0].shape, 0).astype(F32)
    vals = [[] for _ in parts]
    hots = [[] for _ in parts]
    for kk in range(TOP_K):
        mxs = [jnp.max(cur, axis=0, keepdims=True) for cur in curs]
        idxs = [jnp.min(jnp.where(cur == mx, eidx, float(N_EXPERTS)), axis=0, keepdims=True)
                for cur, mx in zip(curs, mxs)]
        for pp in parts:
            hot = eidx == idxs[pp]
            vals[pp].append(mxs[pp])
            hots[pp].append(hot)
            topi_ref[kk:kk + 1, rows[pp]] = idxs[pp].astype(jnp.int32)
            curs[pp] = jnp.where(hot, -jnp.inf, curs[pp])
    for pp in parts:
        exps = [jnp.exp(vv - vals[pp][0]) for vv in vals[pp]]
        denom = exps[0] + exps[1] + exps[2] + exps[3]
        for kk in range(TOP_K):
            topw_ref[kk:kk + 1, rows[pp]] = exps[kk] / denom

    r = lax.broadcasted_iota(jnp.int32, (tp, tp), 0)
    c = lax.broadcasted_iota(jnp.int32, (tp, tp), 1)
    earlier = (r < c).astype(F32)
    sel_fs = []
    for pp in parts:
        sel = hots[pp][0]
        for kk in range(1, TOP_K):
            sel = jnp.logical_or(sel, hots[pp][kk])
        sel_fs.append(sel.astype(F32))
    within = [_bdot(sf, earlier) for sf in sel_fs]
    totals = [jnp.sum(sf, axis=-1, keepdims=True) for sf in sel_fs]
    run = carry_ref[...]
    for pp in parts:
        before = within[pp] + run
        for kk in range(TOP_K):
            rank_ref[kk:kk + 1, rows[pp]] = jnp.sum(jnp.where(hots[pp][kk], before, 0.0), axis=0,
                                                    keepdims=True).astype(jnp.int32)
        run = run + totals[pp]
    carry_ref[...] = run
    cnt_ref[...] = run.astype(jnp.int32)


def _mix_project_route(proj_gate, oa, od, x2d, w_o, g_ffn, w_r_t, b_r, d_model):
    t = x2d.shape[0]
    tm = MIX_TM
    full = lambda shape: pl.BlockSpec(shape, lambda i: (0, 0))
    row = lambda: pl.BlockSpec((tm, d_model), lambda i: (i, 0))
    krow = lambda: pl.BlockSpec((TOP_K, tm), lambda i: (0, i))
    return pl.pallas_call(
        _mix_kernel,
        grid=(t // tm,),
        in_specs=[
            pl.BlockSpec((tm, d_model), lambda i: (i, 0)),
            pl.BlockSpec((tm, d_model), lambda i: (i, 1)),
            row(), row(), row(),
            full((d_model, d_model)), full((1, d_model)), full((N_EXPERTS, d_model)), full((N_EXPERTS, 1)),
        ],
        out_specs=[row(), pl.BlockSpec((tm, d_model // 2), lambda i: (i, 0)),
                   krow(), krow(), krow(), full((N_EXPERTS, 1))],
        out_shape=[
            jax.ShapeDtypeStruct((t, d_model), F32),
            jax.ShapeDtypeStruct((t, d_model // 2), jnp.int32),
            jax.ShapeDtypeStruct((TOP_K, t), jnp.int32),
            jax.ShapeDtypeStruct((TOP_K, t), F32),
            jax.ShapeDtypeStruct((TOP_K, t), jnp.int32),
            jax.ShapeDtypeStruct((N_EXPERTS, 1), jnp.int32),
        ],
        scratch_shapes=[pltpu.VMEM((N_EXPERTS, 1), F32)],
        compiler_params=_cparams(("arbitrary",)),
        name="merge_outproj_route",
    )(proj_gate, proj_gate, oa, od, x2d, w_o, g_ffn, w_r_t, b_r)


def _pack_halves(x):
    half = x.shape[1] // 2
    bits = pltpu.bitcast(x.astype(BF16).astype(F32), jnp.int32)
    return bits[:, :half] | lax.shift_right_logical(bits[:, half:], 16)


def _unpack_halves(p):
    hi = pltpu.bitcast(p & jnp.int32(-65536), F32)
    lo = pltpu.bitcast(lax.shift_left(p, 16), F32)
    return jnp.concatenate([hi, lo], axis=1)


def _expert_kernel(be_ref, nu_ref, ne_ref, x_ref, wup_hbm, bup_ref, wdn_hbm, bdn_ref, *rest):
    y_ref, wup_bf, wdn_bf, wup_f, wdn_f, sem = rest[-6:]
    i = pl.program_id(0)
    d_ff = wdn_f.shape[0]

    def weight_copies(e):
        return (pltpu.make_async_copy(wup_hbm.at[e], wup_f, sem.at[0]),
                pltpu.make_async_copy(wdn_hbm.at[e], wdn_f, sem.at[1]))

    @pl.when(i == 0)
    def _():
        for cp in weight_copies(be_ref[0]):
            cp.start()

    @pl.when(jnp.logical_or(i == 0, be_ref[i] != be_ref[jnp.maximum(i - 1, 0)]))
    def _():
        for cp in weight_copies(be_ref[i]):
            cp.wait()
        rr = lax.broadcasted_iota(jnp.int32, (2 * LANES, 2 * LANES), 0)
        cc = lax.broadcasted_iota(jnp.int32, (2 * LANES, 2 * LANES), 1)
        pick = jnp.where(cc < LANES, 2 * cc, 2 * (cc - LANES) + 1)
        perm = (rr == pick).astype(BF16)
        for g in range(wup_f.shape[1] // (2 * LANES)):
            cs = slice(g * 2 * LANES, (g + 1) * 2 * LANES)
            wup_bf[:, cs] = jnp.dot(wup_f[:, cs].astype(BF16), perm,
                                    preferred_element_type=F32).astype(BF16)
        wdn_bf[...] = wdn_f[...].astype(BF16)

        @pl.when(ne_ref[i] >= 0)
        def _():
            for cp in weight_copies(ne_ref[i]):
                cp.start()

    @pl.when(i < nu_ref[0])
    def _():
        x = _unpack_halves(x_ref[...])
        hid = jnp.dot(x.astype(BF16), wup_bf[...], preferred_element_type=F32) + bup_ref[0]
        acts = []
        for g in range(hid.shape[1] // (2 * LANES)):
            glu = jnp.minimum(hid[:, g * 2 * LANES:g * 2 * LANES + LANES], SWIGLU_LIMIT)
            lin = jnp.clip(hid[:, g * 2 * LANES + LANES:(g + 1) * 2 * LANES], -SWIGLU_LIMIT, SWIGLU_LIMIT)
            acts.append(glu * _sigmoid(SWIGLU_ALPHA * glu) * (lin + 1.0))
        act = jnp.concatenate(acts, axis=1)
        assert act.shape[1] == d_ff
        y = jnp.dot(act.astype(BF16), wdn_bf[...], preferred_element_type=F32) + bdn_ref[0]
        y_ref[...] = _pack_halves(y)

    @pl.when(i >= nu_ref[0])
    def _():
        y_ref[...] = jnp.zeros(y_ref.shape, y_ref.dtype)


def _experts(block_e, n_used, xs, y_prev, first_block, n_rows_total, w_up, b_up, w_down, b_down):
    n_rows, half = xs.shape
    d = w_up.shape[1]
    nb = n_rows // MOE_RB
    two_ff = w_up.shape[2]
    d_ff = w_down.shape[1]
    blk = jnp.arange(nb, dtype=jnp.int32)
    later = jnp.logical_and(blk[None, :] > blk[:, None], block_e[None, :] != block_e[:, None])
    next_e = jnp.min(jnp.where(later, block_e[None, :], N_EXPERTS), axis=1)
    next_e = jnp.where(next_e == N_EXPERTS, -1, next_e).astype(jnp.int32)
    in_specs = [
        pl.BlockSpec((MOE_RB, half), lambda i, be, nu, ne: (jnp.maximum(jnp.minimum(i, nu[0] - 1), 0), 0)),
        pl.BlockSpec(memory_space=pl.ANY),
        pl.BlockSpec((1, 1, two_ff), lambda i, be, nu, ne: (be[i], 0, 0)),
        pl.BlockSpec(memory_space=pl.ANY),
        pl.BlockSpec((1, 1, d), lambda i, be, nu, ne: (be[i], 0, 0)),
    ]
    operands = [block_e, n_used, next_e, xs, w_up, b_up, w_down, b_down]
    aliases = {}
    if y_prev is not None:
        in_specs.append(pl.BlockSpec(memory_space=pl.ANY))
        aliases = {len(operands): 0}
        operands.append(y_prev)
    grid_spec = pltpu.PrefetchScalarGridSpec(
        num_scalar_prefetch=3,
        grid=(nb,),
        in_specs=in_specs,
        out_specs=pl.BlockSpec((MOE_RB, half), lambda i, be, nu, ne: (first_block + i, 0)),
        scratch_shapes=[pltpu.VMEM((d, two_ff), BF16), pltpu.VMEM((d_ff, d), BF16),
                        pltpu.VMEM((d, two_ff), F32), pltpu.VMEM((d_ff, d), F32),
                        pltpu.SemaphoreType.DMA((2,))],
    )
    return pl.pallas_call(
        _expert_kernel,
        grid_spec=grid_spec,
        out_shape=jax.ShapeDtypeStruct((n_rows_total, half), jnp.int32),
        input_output_aliases=aliases,
        compiler_params=_cparams(("arbitrary",)),
        name="moe_experts",
    )(*operands)


def _sc_invert_slots(dest_flat, n_rows):
    n_assign = dest_flat.shape[0]
    n_workers = SC_CORES * SC_SUBCORES
    rows_per_w = n_rows // n_workers
    chunk = SC_SCAN_CHUNK
    assert n_rows % n_workers == 0 and rows_per_w % SC_LANES == 0 and n_assign % chunk == 0
    mesh = plsc.VectorSubcoreMesh(core_axis_name="c", subcore_axis_name="s",
                                  num_cores=SC_CORES, num_subcores=SC_SUBCORES)

    n_chunks = n_assign // chunk
    assert n_chunks % 2 == 0 and chunk % (SC_SCAN_UNROLL * SC_LANES) == 0

    def body(dest_hbm, out_hbm, dest_a, dest_b, map_v, sem_a, sem_b):
        wid = lax.axis_index("s") * SC_CORES + lax.axis_index("c")
        base = wid * rows_per_w
        lanes = lax.broadcasted_iota(jnp.int32, (SC_LANES,), 0)

        def fetch(ci, dest_v, sem):
            off = pl.multiple_of(ci * chunk, chunk)
            return pltpu.make_async_copy(dest_hbm.at[pl.ds(off, chunk)], dest_v, sem)

        def scan(ci, dest_v):
            @pl.loop(0, chunk, step=SC_SCAN_UNROLL * SC_LANES)
            def _(j0):
                for u in range(SC_SCAN_UNROLL):
                    j = j0 + u * SC_LANES
                    local = dest_v[pl.ds(j, SC_LANES)] - base
                    mine = jnp.logical_and(local >= 0, local < rows_per_w)
                    plsc.store_scatter(map_v, [jnp.where(mine, local, 0)], ci * chunk + j + lanes, mask=mine)

        fetch(0, dest_a, sem_a).start()

        @pl.loop(0, rows_per_w, step=SC_LANES)
        def _(r0):
            map_v[pl.ds(r0, SC_LANES)] = jnp.full((SC_LANES,), -1, jnp.int32)

        @pl.loop(0, n_chunks, step=2)
        def _(ci):
            fetch(ci + 1, dest_b, sem_b).start()
            fetch(ci, dest_a, sem_a).wait()
            scan(ci, dest_a)
            nxt = jnp.minimum(ci + 2, n_chunks - 1)
            fetch(nxt, dest_a, sem_a).start()
            fetch(ci + 1, dest_b, sem_b).wait()
            scan(ci + 1, dest_b)

        fetch(n_chunks - 1, dest_a, sem_a).wait()
        pltpu.sync_copy(map_v, out_hbm.at[pl.ds(base, rows_per_w)])

    return pl.kernel(
        body,
        out_type=jax.ShapeDtypeStruct((n_rows,), jnp.int32),
        mesh=mesh,
        scratch_types=[pltpu.VMEM((chunk,), jnp.int32), pltpu.VMEM((chunk,), jnp.int32),
                       pltpu.VMEM((rows_per_w,), jnp.int32),
                       pltpu.SemaphoreType.DMA, pltpu.SemaphoreType.DMA],
        compiler_params=pltpu.CompilerParams(needs_layout_passes=False),
        name="moe_slot_inverse",
    )(dest_flat)


def _sc_gather_rows(table, idx):
    n_idx = idx.shape[0]
    d = table.shape[1]
    n_workers = SC_CORES * SC_SUBCORES
    per_worker = n_idx // n_workers
    n_chunks = per_worker // SC_GATHER_ROWS
    assert n_idx % n_workers == 0 and per_worker % SC_GATHER_ROWS == 0
    mesh = plsc.VectorSubcoreMesh(core_axis_name="c", subcore_axis_name="s",
                                  num_cores=SC_CORES, num_subcores=SC_SUBCORES)

    assert n_chunks % 2 == 0

    def body(table_hbm, idx_hbm, out_hbm, idx_v, rows_a, rows_b, sem_a, sem_b):
        wid = lax.axis_index("s") * SC_CORES + lax.axis_index("c")
        base = wid * per_worker
        pltpu.sync_copy(idx_hbm.at[pl.ds(base, per_worker)], idx_v)

        def gather(ci, rows_v, sem):
            off = pl.multiple_of(ci * SC_GATHER_ROWS, SC_GATHER_ROWS)
            return pltpu.make_async_copy(table_hbm.at[idx_v.at[pl.ds(off, SC_GATHER_ROWS)]], rows_v, sem)

        def put(ci, rows_v):
            off = pl.multiple_of(ci * SC_GATHER_ROWS, SC_GATHER_ROWS)
            pltpu.sync_copy(rows_v, out_hbm.at[pl.ds(base + off, SC_GATHER_ROWS)])

        gather(0, rows_a, sem_a).start()

        @pl.loop(0, n_chunks, step=2)
        def _(ci):
            gather(ci + 1, rows_b, sem_b).start()
            gather(ci, rows_a, sem_a).wait()
            put(ci, rows_a)
            nxt = jnp.minimum(ci + 2, n_chunks - 1)
            gather(nxt, rows_a, sem_a).start()
            gather(ci + 1, rows_b, sem_b).wait()
            put(ci + 1, rows_b)

        gather(n_chunks - 1, rows_a, sem_a).wait()

    return pl.kernel(
        body,
        out_type=jax.ShapeDtypeStruct((n_idx, d), table.dtype),
        mesh=mesh,
        scratch_types=[
            pltpu.VMEM((per_worker,), jnp.int32),
            pltpu.VMEM((SC_GATHER_ROWS, d), table.dtype),
            pltpu.VMEM((SC_GATHER_ROWS, d), table.dtype),
            pltpu.SemaphoreType.DMA,
            pltpu.SemaphoreType.DMA,
        ],
        name="moe_slot_gather",
    )(table, idx)


def _combine_kernel(x1_ref, w_ref, y0_ref, y1_ref, y2_ref, y3_ref, o_ref):
    w = w_ref[...]
    out = x1_ref[...]
    for kk, y_ref in enumerate((y0_ref, y1_ref, y2_ref, y3_ref)):
        out = out + w[:, kk:kk + 1] * _unpack_halves(y_ref[...])
    o_ref[...] = out


def _combine(x1, w_tok, y_slots):
    t, d = x1.shape
    tc = COMB_TC
    nt = t // tc
    yspec = lambda kk: pl.BlockSpec((tc, d // 2), lambda i: (kk * nt + i, 0))
    return pl.pallas_call(
        _combine_kernel,
        grid=(nt,),
        in_specs=[
            pl.BlockSpec((tc, d), lambda i: (i, 0)),
            pl.BlockSpec((tc, TOP_K), lambda i: (i, 0)),
            yspec(0), yspec(1), yspec(2), yspec(3),
        ],
        out_specs=pl.BlockSpec((tc, d), lambda i: (i, 0)),
        out_shape=jax.ShapeDtypeStruct((t, d), F32),
        compiler_params=_cparams(("parallel",)),
        name="moe_combine",
    )(x1, w_tok, y_slots, y_slots, y_slots, y_slots)


def _moe(x1, h2, topi, topw, rank, counts, w_up, b_up, w_down, b_down):
    t, d = x1.shape
    n_assign = t * TOP_K
    nb = -(-n_assign // MOE_RB) + N_EXPERTS
    n_rows = nb * MOE_RB
    counts = counts[:, 0]
    padded = (counts + MOE_RB - 1) // MOE_RB * MOE_RB
    padded_end = jnp.cumsum(padded)
    padded_start = padded_end - padded
    expert_ids = jnp.arange(N_EXPERTS, dtype=jnp.int32)[:, None, None]
    start_of = jnp.sum(jnp.where(topi[None] == expert_ids, padded_start[:, None, None], 0), axis=0)
    dest = (start_of + rank).astype(jnp.int32)
    n_used = (padded_end[-1] // MOE_RB).astype(jnp.int32)
    blk = jnp.minimum(jnp.arange(nb, dtype=jnp.int32), n_used - 1)
    block_e = jnp.minimum(jnp.sum(padded_end[None, :] <= (blk * MOE_RB)[:, None], axis=1),
                          N_EXPERTS - 1).astype(jnp.int32)
    slot_of = _sc_invert_slots(dest.reshape(-1), n_rows)
    src_tok = jnp.where(slot_of < 0, jnp.arange(n_rows, dtype=jnp.int32), slot_of) % t

    nb_a = nb // 4
    y_rows = None
    for first, n_blk in ((0, nb_a), (nb_a, nb - nb_a)):
        xs = _sc_gather_rows(h2, lax.slice(src_tok, (first * MOE_RB,), ((first + n_blk) * MOE_RB,)))
        used = jnp.clip(n_used - first, 0, n_blk).reshape(1)
        y_rows = _experts(lax.slice(block_e, (first,), (first + n_blk,)), used, xs, y_rows, first, n_rows,
                          w_up, b_up, w_down, b_down)
    y_slots = _sc_gather_rows(y_rows, dest.reshape(-1))
    return _combine(x1, topw.T, y_slots)


def kernel(x, g_mix, w_in, b_gate, conv_w, a_log, dt_bias, g_delta_out, q_norm, k_norm, lambda_q1, lambda_k1, lambda_q2, lambda_k2, g_subln, rel_bias, w_o, g_ffn, w_router, b_router, w_up, b_up, w_down, b_down):
    bsz, seq, d = x.shape
    depth = g_mix.shape[0]
    n_heads = d // HEAD_DIM
    t = bsz * seq
    d_ff = w_down.shape[2]
    assert d % PROJ_TN == 0 and t % PROJ_TM == 0 and seq % GDN_TB == 0 and seq % ATT_BQ == 0
    assert t % MIX_TM == 0 and t % COMB_TC == 0 and n_heads % GDN_HG == 0
    assert (t * TOP_K) % MOE_RB == 0
    assert 2 * n_heads <= 2 * SUBLANES

    x2d = x.reshape(t, d)
    for l in range(depth):
        wl = w_in[l]
        c0 = 4 * d
        c1 = c0 + 2 * n_heads
        c2 = c1 + 2 * d
        c3 = c2 + d
        w_small = jnp.pad(wl[:, c0:c1], ((0, 0), (0, LANES - 2 * n_heads)))
        gm = g_mix[l].reshape(1, d)
        head_pad = jnp.zeros((LANES - 2 * n_heads,), F32)
        alog = jnp.concatenate([jnp.zeros((n_heads,), F32), a_log[l], head_pad])
        dtb = jnp.concatenate([jnp.zeros((n_heads,), F32), dt_bias[l], head_pad])
        rows_t = 2 * n_heads
        beta_decay = (w_small.astype(BF16), w_small[:, :rows_t].T.astype(BF16),
                      alog.reshape(1, LANES), dtb.reshape(1, LANES),
                      alog[:rows_t].reshape(rows_t, 1), dtb[:rows_t].reshape(rows_t, 1), n_heads)
        w_plain = jnp.concatenate([wl[:, :c0], wl[:, c2:c3]], axis=1).astype(BF16)
        proj_plain, small, small_t = _input_projection(x2d, gm, w_plain, jnp.zeros((1, 5 * d), F32), "plain",
                                                       beta_decay)
        qk_gain = jnp.concatenate([jnp.tile(q_norm[l] * (DH_DIFF ** -0.5), 2 * n_heads),
                                   jnp.tile(k_norm[l], 2 * n_heads)]).reshape(1, 2 * d)
        proj_qk = _input_projection(x2d, gm, wl[:, c1:c2].astype(BF16), qk_gain, "qknorm")
        proj_gate = _input_projection(x2d, gm, wl[:, c3:].astype(BF16), b_gate[l].reshape(1, 2 * d), "gate")

        oa = _gated_delta(proj_plain, small, small_t, conv_w[l], g_delta_out[l].reshape(1, HEAD_DIM),
                          bsz, seq, n_heads, d)

        lam_init = 0.8 - 0.6 * math.exp(-0.3 * l)
        lam_params = jnp.stack([lambda_q1[l], lambda_k1[l], lambda_q2[l], lambda_k2[l]])
        od = _diff_attention(proj_qk, proj_plain, rel_bias, lam_params, g_subln[l].reshape(1, HEAD_DIM),
                             bsz, seq, n_heads, d, lam_init)

        x1, h2, topi, topw, rank, counts = _mix_project_route(
            proj_gate, oa, od, x2d, w_o[l].astype(BF16), g_ffn[l].reshape(1, d),
            w_router[l].T, b_router[l].reshape(N_EXPERTS, 1), d)

        b_up_l = b_up[l].reshape(N_EXPERTS, 2 * d_ff // (2 * LANES), LANES, 2)
        b_up_l = jnp.swapaxes(b_up_l, 2, 3).reshape(N_EXPERTS, 1, 2 * d_ff)
        x2d = _moe(x1, h2, topi, topw, rank, counts, w_up[l], b_up_l,
                   w_down[l], b_down[l].reshape(N_EXPERTS, 1, d))
    return x2d.reshape(bsz, seq, d)
```

```python
import functools
import math

import jax
import jax.numpy as jnp
from jax import lax
from jax.experimental import pallas as pl
from jax.experimental.pallas import tpu as pltpu
from jax.experimental.pallas import tpu_sc as plsc

F32 = jnp.float32
BF16 = jnp.bfloat16

HEAD_DIM = 128
DH_DIFF = HEAD_DIM // 2
CONV_WIDTH = 4
CHUNK = 64
N_BUCKETS = 32
MAX_DISTANCE = 128
N_EXPERTS = 32
TOP_K = 4
SWIGLU_LIMIT = 7.0
SWIGLU_ALPHA = 1.702
EPS = 1e-6
NEG_BIG = -1e30

LANES = 128
SUBLANES = 8
VMEM_LIMIT = 56 * 1024 * 1024
SC_CORES = 2
SC_SUBCORES = 16
SC_LANES = 16
SC_GATHER_ROWS = 64
SC_SCAN_CHUNK = 4096
SC_SCAN_UNROLL = 4

PROJ_TM = 2048
PROJ_TN = 1024
PROJ_CHUNK = 256
GDN_TB = 256
GDN_HG = 8
ATT_HG = 1
ATT_BQ = 512
ATT_BK = 512
MIX_TM = 1024
MIX_PARTS = 2
MOE_RB = 512
COMB_TC = 512


def _cparams(sem):
    return pltpu.CompilerParams(dimension_semantics=sem, vmem_limit_bytes=VMEM_LIMIT)


def _sigmoid(x):
    return 0.5 * jnp.tanh(0.5 * x) + 0.5


def _bdot(a, b):
    return jnp.dot(a.astype(BF16), b.astype(BF16), preferred_element_type=F32)


def _bdot_nt(a, b):
    return lax.dot_general(a.astype(BF16), b.astype(BF16), (((1,), (1,)), ((), ())),
                           preferred_element_type=F32)


def _bdot_tn(a, b):
    return lax.dot_general(a.astype(BF16), b.astype(BF16), (((0,), (0,)), ((), ())),
                           preferred_element_type=F32)


def _beta_decay(acc, idx, alog, dtb, n_heads):
    beta = _sigmoid(acc)
    z = acc + dtb
    softplus = jnp.maximum(z, 0.0) + jnp.log1p(jnp.exp(-jnp.abs(z)))
    gdec = -jnp.exp(alog) * softplus
    return jnp.where(idx < n_heads, beta, jnp.where(idx < 2 * n_heads, gdec, 0.0))


def _proj_kernel(x_ref, g_ref, w_ref, aux_ref, *rest, mode, n_heads):
    o_ref, h_ref = rest[-2:] if n_heads is None else (rest[6], rest[-1])

    @pl.when(pl.program_id(1) == 0)
    def _():
        x = x_ref[...]
        ms = jnp.mean(x * x, axis=-1, keepdims=True)
        h_ref[...] = (x * lax.rsqrt(ms + EPS) * g_ref[...]).astype(BF16)
        if n_heads is not None:
            ws_ref, wst_ref, alog_ref, dtb_ref, alog_t_ref, dtb_t_ref, _, os_ref, ost_ref, _ = rest
            hb = h_ref[...]
            acc = jnp.dot(hb, ws_ref[...], preferred_element_type=F32)
            lane = lax.broadcasted_iota(jnp.int32, acc.shape, 1)
            os_ref[...] = _beta_decay(acc, lane, alog_ref[...], dtb_ref[...], n_heads)
            acc_t = lax.dot_general(wst_ref[...], hb, (((1,), (1,)), ((), ())),
                                    preferred_element_type=F32)
            sub = lax.broadcasted_iota(jnp.int32, acc_t.shape, 0)
            ost_ref[...] = _beta_decay(acc_t, sub, alog_t_ref[...], dtb_t_ref[...], n_heads)

    h = h_ref[...]
    lo = lax.broadcasted_iota(jnp.int32, (1, LANES), 1) < DH_DIFF
    for c in range(PROJ_TN // PROJ_CHUNK):
        cs = slice(c * PROJ_CHUNK, (c + 1) * PROJ_CHUNK)
        acc = jnp.dot(h, w_ref[:, cs], preferred_element_type=F32)
        if mode == "plain":
            o_ref[:, cs] = acc.astype(o_ref.dtype)
        elif mode == "gate":
            o_ref[:, cs] = _sigmoid(acc + aux_ref[:, cs]).astype(o_ref.dtype)
        else:
            for g in range(PROJ_CHUNK // LANES):
                sl = slice(c * PROJ_CHUNK + g * LANES, c * PROJ_CHUNK + (g + 1) * LANES)
                y = acc[:, g * LANES:(g + 1) * LANES]
                y2 = y * y
                s_lo = jnp.sum(jnp.where(lo, y2, 0.0), axis=-1, keepdims=True)
                s_hi = jnp.sum(jnp.where(lo, 0.0, y2), axis=-1, keepdims=True)
                r = jnp.where(lo, lax.rsqrt(s_lo / DH_DIFF + EPS), lax.rsqrt(s_hi / DH_DIFF + EPS))
                o_ref[:, sl] = (y * r * aux_ref[:, sl]).astype(o_ref.dtype)


def _input_projection(x2d, g_mix, w, aux, mode, beta_decay=None):
    t, d = x2d.shape
    n = w.shape[1]
    full = lambda shape: pl.BlockSpec(shape, lambda i, j: (0, 0))
    in_specs = [
        pl.BlockSpec((PROJ_TM, d), lambda i, j: (i, 0)),
        full((1, d)),
        pl.BlockSpec((d, PROJ_TN), lambda i, j: (0, j)),
        pl.BlockSpec((1, PROJ_TN), lambda i, j: (0, j)),
    ]
    operands = [x2d, g_mix, w, aux]
    out_specs = [pl.BlockSpec((PROJ_TM, PROJ_TN), lambda i, j: (i, j))]
    out_shape = [jax.ShapeDtypeStruct((t, n), BF16)]
    n_heads = None
    if beta_decay is not None:
        n_heads = beta_decay[-1]
        rows_t = 2 * n_heads
        in_specs += [full((d, LANES)), full((rows_t, d)), full((1, LANES)), full((1, LANES)),
                     full((rows_t, 1)), full((rows_t, 1))]
        operands += list(beta_decay[:-1])
        out_specs += [pl.BlockSpec((PROJ_TM, LANES), lambda i, j: (i, 0)),
                      pl.BlockSpec((rows_t, PROJ_TM), lambda i, j: (0, i))]
        out_shape += [jax.ShapeDtypeStruct((t, LANES), F32), jax.ShapeDtypeStruct((rows_t, t), F32)]
    out = pl.pallas_call(
        functools.partial(_proj_kernel, mode=mode, n_heads=n_heads),
        grid=(t // PROJ_TM, n // PROJ_TN),
        in_specs=in_specs,
        out_specs=out_specs,
        out_shape=out_shape,
        scratch_shapes=[pltpu.VMEM((PROJ_TM, d), BF16)],
        compiler_params=_cparams(("parallel", "arbitrary")),
        name="input_projection_" + mode,
    )(*operands)
    return out[0] if beta_decay is None else out


def _gdn_kernel(q_ref, k_ref, v_ref, z_ref, sm_ref, smt_ref, cwq_ref, cwk_ref, cwv_ref, gout_ref,
                o_ref, state_ref, qp_ref, kp_ref, vp_ref, vn_ref, *, n_heads):
    hg = pl.program_id(1)
    s = pl.program_id(2)
    tb = GDN_TB
    pad = SUBLANES
    width = GDN_HG * HEAD_DIM

    @pl.when(s == 0)
    def _():
        state_ref[...] = jnp.zeros_like(state_ref)
        for p_ref in (qp_ref, kp_ref, vp_ref):
            p_ref[0:pad, :] = jnp.zeros((pad, width), F32)

    r = lax.broadcasted_iota(jnp.int32, (tb, tb), 0)
    c = lax.broadcasted_iota(jnp.int32, (tb, tb), 1)
    delay_mat = jnp.concatenate([(r - c == dd).astype(BF16) for dd in range(1, CONV_WIDTH)], axis=0)

    def conv_silu(x_ref, p_ref, cw_ref):
        x = x_ref[...]
        xf = x.astype(F32)
        p_ref[pad:2 * pad, :] = xf[0:pad]
        delayed = jnp.dot(delay_mat, x, preferred_element_type=F32)
        acc = cw_ref[CONV_WIDTH - 1:CONV_WIDTH, :] * xf
        for dd in range(1, CONV_WIDTH):
            first = p_ref[pad - dd:2 * pad - dd, :]
            xd = jnp.concatenate([first, delayed[(dd - 1) * tb + pad:dd * tb]], axis=0)
            acc = acc + cw_ref[CONV_WIDTH - 1 - dd:CONV_WIDTH - dd, :] * xd
        p_ref[0:pad, :] = xf[tb - pad:tb]
        return acc * _sigmoid(acc)

    q_all = conv_silu(q_ref, qp_ref, cwq_ref)
    k_all = conv_silu(k_ref, kp_ref, cwk_ref)
    v_all = conv_silu(v_ref, vp_ref, cwv_ref)

    shift = int(math.log2(CHUNK))
    same = (r >> shift) == (c >> shift)
    incl = jnp.logical_and(same, c <= r)
    strict = jnp.logical_and(same, c < r)

    small = sm_ref[...]
    small_t = smt_ref[...]
    lane = lax.broadcasted_iota(jnp.int32, small.shape, 1)
    def split3(a):
        hi = a.astype(BF16)
        r1 = a - hi.astype(F32)
        mid = r1.astype(BF16)
        lo = (r1 - mid.astype(F32)).astype(BF16)
        return hi.astype(F32), mid.astype(F32), lo.astype(F32)

    part = 2 * n_heads
    s_hi, s_mid, s_lo = split3(small)
    small3 = jnp.where(lane < part, s_hi,
                       jnp.where(lane < 2 * part, pltpu.roll(s_mid, part, 1),
                                 jnp.where(lane < 3 * part, pltpu.roll(s_lo, 2 * part, 1), 0.0)))
    both = _bdot(jnp.concatenate([incl.astype(F32), same.astype(F32)], axis=0), small3)
    gcum = both[:tb]
    gtot = both[tb:]
    gcum_t = _bdot(jnp.concatenate(split3(small_t), axis=0),
                   jnp.logical_and(same, r <= c).astype(F32))
    sub3 = lax.broadcasted_iota(jnp.int32, gcum_t.shape, 0)

    heads = range(GDN_HG)
    hsl = [slice(hh * HEAD_DIM, (hh + 1) * HEAD_DIM) for hh in heads]
    qs = [q_all[:, hs] for hs in hsl]
    ks = [k_all[:, hs] for hs in hsl]
    vs = [v_all[:, hs] for hs in hsl]
    qs = [q * lax.rsqrt(jnp.sum(q * q, axis=-1, keepdims=True) + EPS) * (HEAD_DIM ** -0.5) for q in qs]
    ks = [k * lax.rsqrt(jnp.sum(k * k, axis=-1, keepdims=True) + EPS) for k in ks]

    def col_of(arr, idx):
        return jnp.sum(jnp.where(lane == idx, arr, 0.0), axis=-1, keepdims=True)

    def terms_of(pos, idx):
        return jnp.logical_or(pos == idx, jnp.logical_or(pos == idx + part, pos == idx + 2 * part))

    head_ids = [hg * GDN_HG + hh for hh in heads]
    betas = [col_of(small, hd) for hd in head_ids]
    gcs = [jnp.sum(jnp.where(terms_of(lane, hd + n_heads), gcum, 0.0), axis=-1, keepdims=True)
           for hd in head_ids]
    gls = [jnp.sum(jnp.where(terms_of(lane, hd + n_heads), gtot, 0.0), axis=-1, keepdims=True)
           for hd in head_ids]
    gc_rows = [jnp.sum(jnp.where(terms_of(sub3, hd + n_heads), gcum_t, 0.0), axis=0, keepdims=True)
               for hd in head_ids]

    decays = [jnp.where(incl, jnp.exp(jnp.minimum(gc - gr, 0.0)), 0.0) for gc, gr in zip(gcs, gc_rows)]
    kbs = [k * b for k, b in zip(ks, betas)]
    kks = [_bdot_nt(kb, k) for kb, k in zip(kbs, ks)]
    pws = [jnp.where(strict, -(kk * dc), 0.0) for kk, dc in zip(kks, decays)]
    n_chunks = tb // CHUNK
    cat_row = lax.broadcasted_iota(jnp.int32, (CHUNK, tb), 0)
    cat_lane = lax.broadcasted_iota(jnp.int32, (CHUNK, tb), 1)
    lane_chunk = cat_lane >> shift

    def block_diag(m_cat):
        return jnp.concatenate([jnp.where(lane_chunk == ci, m_cat, 0.0) for ci in range(n_chunks)], axis=0)

    def cat_of(m_bd):
        out = m_bd[0:CHUNK]
        for ci in range(1, n_chunks):
            out = out + m_bd[ci * CHUNK:(ci + 1) * CHUNK]
        return out

    pcats = [cat_of(pw) for pw in pws]
    eye_cat = ((cat_lane & (CHUNK - 1)) == cat_row).astype(F32)
    tcats = [eye_cat + pc for pc in pcats]
    pcats = [_bdot(pc, block_diag(pc)) for pc in pcats]
    n_levels = int(math.log2(CHUNK))
    for lev in range(1, n_levels):
        bds = [block_diag(pc) for pc in pcats]
        if lev < n_levels - 1:
            prods = [_bdot(jnp.concatenate([pc, tc], axis=0), bd) for pc, tc, bd in zip(pcats, tcats, bds)]
            pcats = [pr[:CHUNK] for pr in prods]
            tcats = [tc + pr[CHUNK:] for tc, pr in zip(tcats, prods)]
        else:
            tcats = [tc + _bdot(tc, bd) for tc, bd in zip(tcats, bds)]
    tmats = [block_diag(tc) for tc in tcats]
    egcs = [jnp.exp(gc) for gc in gcs]
    uws = [_bdot(tm, jnp.concatenate([v * b, kb * eg], axis=1))
           for tm, v, b, kb, eg in zip(tmats, vs, betas, kbs, egcs)]
    us = [uw[:, :HEAD_DIM] for uw in uws]
    ws = [uw[:, HEAD_DIM:] for uw in uws]
    qkm = [_bdot_nt(q, k) for q, k in zip(qs, ks)]
    qkm = [jnp.where(incl, x * dc, 0.0) for x, dc in zip(qkm, decays)]
    q_decs = [q * eg for q, eg in zip(qs, egcs)]
    k_ends = [k * jnp.exp(gl - gc) for k, gl, gc in zip(ks, gls, gcs)]

    for hh in heads:
        vn_ref[hh] = jnp.zeros((tb, HEAD_DIM), F32)
    outs = [[] for _ in heads]
    for ci in range(tb // CHUNK):
        cs = slice(ci * CHUNK, (ci + 1) * CHUNK)
        sts = [state_ref[hh] for hh in heads]
        ws_qs = [_bdot(jnp.concatenate([ws[hh][cs], q_decs[hh][cs]], axis=0), sts[hh]) for hh in heads]
        v_news = [us[hh][cs] - ws_qs[hh][:CHUNK] for hh in heads]
        for hh in heads:
            vn_ref[hh, cs, :] = v_news[hh]
        intra = [_bdot(qkm[hh][cs], vn_ref[hh]) for hh in heads]
        upd = [_bdot_tn(k_ends[hh][cs], v_news[hh]) for hh in heads]
        for hh in heads:
            outs[hh].append(ws_qs[hh][CHUNK:] + intra[hh])
            g_last = gls[hh][ci * CHUNK:ci * CHUNK + 1, :]
            state_ref[hh] = sts[hh] * jnp.exp(g_last) + upd[hh]
    for hh in heads:
        o = jnp.concatenate(outs[hh], axis=0)
        o = o * lax.rsqrt(jnp.mean(o * o, axis=-1, keepdims=True) + EPS) * gout_ref[...]
        zz = z_ref[:, hsl[hh]].astype(F32)
        o_ref[:, hsl[hh]] = (o * (zz * _sigmoid(zz))).astype(o_ref.dtype)


def _gated_delta(big, small, small_t, conv_w, g_out, bsz, seq, n_heads, d_model):
    t = bsz * seq
    tb = GDN_TB
    ns = seq // tb
    width = GDN_HG * HEAD_DIM
    nhg = n_heads // GDN_HG
    blocks_per_group = d_model // width
    rows_t = small_t.shape[0]

    def colspec(group):
        return pl.BlockSpec((tb, width), lambda b, h, s: (b * ns + s, group * blocks_per_group + h))

    def cwspec(group):
        return pl.BlockSpec((CONV_WIDTH, width), lambda b, h, s: (0, group * blocks_per_group + h))

    return pl.pallas_call(
        functools.partial(_gdn_kernel, n_heads=n_heads),
        grid=(bsz, nhg, ns),
        in_specs=[
            colspec(0), colspec(1), colspec(2), colspec(3),
            pl.BlockSpec((tb, LANES), lambda b, h, s: (b * ns + s, 0)),
            pl.BlockSpec((rows_t, tb), lambda b, h, s: (0, b * ns + s)),
            cwspec(0), cwspec(1), cwspec(2),
            pl.BlockSpec((1, HEAD_DIM), lambda b, h, s: (0, 0)),
        ],
        out_specs=pl.BlockSpec((tb, width), lambda b, h, s: (b * ns + s, h)),
        out_shape=jax.ShapeDtypeStruct((t, d_model), BF16),
        scratch_shapes=[
            pltpu.VMEM((GDN_HG, HEAD_DIM, HEAD_DIM), F32),
            pltpu.VMEM((2 * SUBLANES, width), F32),
            pltpu.VMEM((2 * SUBLANES, width), F32),
            pltpu.VMEM((2 * SUBLANES, width), F32),
            pltpu.VMEM((GDN_HG, tb, HEAD_DIM), F32),
        ],
        compiler_params=_cparams(("parallel", "parallel", "arbitrary")),
        name="gated_delta",
    )(big, big, big, big, small, small_t, conv_w, conv_w, conv_w, g_out)


def _t5_bucket(n):
    max_exact = N_BUCKETS // 2
    nf = jnp.maximum(n, 1).astype(F32)
    large = max_exact + (jnp.log(nf / max_exact) / math.log(MAX_DISTANCE / max_exact)
                         * (N_BUCKETS - max_exact)).astype(jnp.int32)
    large = jnp.minimum(large, N_BUCKETS - 1)
    return jnp.where(n < max_exact, n, large)


def _attn_kernel(rb_ref, q_ref, k_ref, v_ref, lam_ref, gsub_ref, o_ref,
                 bias_ref, m_ref, acc_ref, sa_ref, sb_ref, qs_ref, *, lam_init):
    hg = pl.program_id(0)
    b = pl.program_id(1)
    bq, bk = ATT_BQ, ATT_BK
    heads = range(ATT_HG)
    hsl = [slice(hh * HEAD_DIM, (hh + 1) * HEAD_DIM) for hh in heads]

    @pl.when(b == 0)
    def _():
        blk = LANES
        i = lax.broadcasted_iota(jnp.int32, (blk, blk), 0)
        jj = lax.broadcasted_iota(jnp.int32, (blk, blk), 1)
        for hh in heads:
            head = hg * ATT_HG + hh
            far = rb_ref[N_BUCKETS - 1, head]

            def toeplitz(offset):
                bucket = _t5_bucket(jnp.maximum(i - jj + offset, 0))
                out = jnp.zeros((blk, blk), F32)
                for cc in range(N_BUCKETS):
                    out = jnp.where(bucket == cc, rb_ref[cc, head] - far, out)
                return out

            on_diag = jnp.where(i >= jj, toeplitz(0), NEG_BIG)
            next_diag = toeplitz(blk)
            kinds = {0: on_diag, 1: next_diag}
            bias_ref[hh, 2] = jnp.zeros((bq, bk), F32)
            for slot in range(2):
                for rr in range(bq // blk):
                    for cc in range(bk // blk):
                        delta = rr - cc + slot * (bk // blk)
                        if delta < 0:
                            tile = jnp.full((blk, blk), NEG_BIG, F32)
                        else:
                            tile = kinds.get(delta, jnp.zeros((blk, blk), F32))
                        bias_ref[hh, slot, rr * blk:(rr + 1) * blk, cc * blk:(cc + 1) * blk] = tile

    lane = lax.broadcasted_iota(jnp.int32, (bq, HEAD_DIM), 1)
    ones_col = (lax.broadcasted_iota(jnp.int32, (bk, HEAD_DIM), 1) == 0).astype(BF16)
    lam_p = lam_ref[...]
    s1 = jnp.sum(lam_p[0:1] * lam_p[1:2], axis=-1, keepdims=True)
    s2 = jnp.sum(lam_p[2:3] * lam_p[3:4], axis=-1, keepdims=True)
    lam = jnp.exp(s1) - jnp.exp(s2) + lam_init

    n_q = q_ref.shape[0] // bq

    def stack_q(qb):
        rows = pl.ds(pl.multiple_of(qb * bq, bq), bq)
        for hh in heads:
            q = q_ref[rows, hsl[hh]]
            zero = jnp.zeros_like(q)
            qs_ref[hh, 0:bq, :] = jnp.where(lane < DH_DIFF, q, zero)
            qs_ref[hh, bq:2 * bq, :] = jnp.where(lane < DH_DIFF, zero, q)

    def scores(j, s_ref):
        ks = pl.multiple_of(j * bk, bk)
        for hh in heads:
            s_ref[hh] = lax.dot_general(qs_ref[hh], k_ref[pl.ds(ks, bk), hsl[hh]],
                                        (((1,), (1,)), ((), ())), preferred_element_type=F32)

    stack_q(0)
    scores(0, sa_ref)

    def query_block(qi, outer):
        rows = pl.ds(pl.multiple_of(qi * bq, bq), bq)
        m_ref[...] = jnp.full(m_ref.shape, NEG_BIG, F32)
        acc_ref[...] = jnp.zeros(acc_ref.shape, F32)

        def absorb(j, s_ref, biased=True):
            ks = pl.multiple_of(j * bk, bk)
            v_exts = [jnp.concatenate([v_ref[pl.ds(ks, bk), hs], ones_col], axis=1) for hs in hsl]
            if biased:
                slot = jnp.minimum(qi - j, 2)
                scs = [jnp.concatenate([s_ref[hh, 0:bq, :] + bias_ref[hh, slot],
                                        s_ref[hh, bq:2 * bq, :] + bias_ref[hh, slot]], axis=0)
                       for hh in heads]
            else:
                scs = [s_ref[hh] for hh in heads]
            m_olds = [m_ref[hh] for hh in heads]
            m_news = [jnp.maximum(mo, jnp.max(sc, axis=-1, keepdims=True)) for mo, sc in zip(m_olds, scs)]
            ps = [jnp.exp(sc - mn) for sc, mn in zip(scs, m_news)]
            pvs = [jnp.dot(p.astype(BF16), ve, preferred_element_type=F32) for p, ve in zip(ps, v_exts)]
            for hh in heads:
                acc_ref[hh] = jnp.exp(m_olds[hh] - m_news[hh]) * acc_ref[hh] + pvs[hh]
                m_ref[hh] = m_news[hh]

        n_tiles = qi + 1

        def pair_body(jj, carry, biased):
            j0 = 2 * jj
            scores(j0 + 1, sb_ref)
            absorb(j0, sa_ref, biased)
            scores(jnp.minimum(j0 + 2, qi), sa_ref)
            absorb(j0 + 1, sb_ref, biased)
            return carry

        n_far_pairs = jnp.maximum(qi - 1, 0) // 2
        lax.fori_loop(0, n_far_pairs, functools.partial(pair_body, biased=False), 0)
        lax.fori_loop(n_far_pairs, n_tiles // 2, functools.partial(pair_body, biased=True), 0)

        @pl.when(n_tiles % 2 == 1)
        def _():
            absorb(qi, sa_ref)

        stack_q(jnp.minimum(qi + 1, n_q - 1))
        scores(0, sa_ref)
        for hh in heads:
            acc = acc_ref[hh]
            num = acc[:, :HEAD_DIM]
            den = acc[:, HEAD_DIM:HEAD_DIM + 1]
            o = num[:bq] / den[:bq] - lam * (num[bq:] / den[bq:])
            o = o * lax.rsqrt(jnp.mean(o * o, axis=-1, keepdims=True) + EPS) * gsub_ref[...]
            o_ref[rows, hsl[hh]] = (o * (1.0 - lam_init)).astype(o_ref.dtype)
        return outer

    lax.fori_loop(0, n_q, query_block, 0)


def _diff_attention(proj_qk, proj_plain, rel_bias, lam_params, g_subln, bsz, seq, n_heads, d_model,
                    lam_init):
    t = bsz * seq
    assert ATT_BQ == ATT_BK and MAX_DISTANCE <= LANES and n_heads % ATT_HG == 0 and seq % ATT_BQ == 0
    width = ATT_HG * HEAD_DIM
    per = d_model // width
    vcol = 4 * per
    return pl.pallas_call(
        functools.partial(_attn_kernel, lam_init=lam_init),
        grid=(n_heads // ATT_HG, bsz),
        in_specs=[
            pl.BlockSpec(memory_space=pltpu.SMEM),
            pl.BlockSpec((seq, width), lambda h, b: (b, h)),
            pl.BlockSpec((seq, width), lambda h, b: (b, per + h)),
            pl.BlockSpec((seq, width), lambda h, b: (b, vcol + h)),
            pl.BlockSpec((4, DH_DIFF), lambda h, b: (0, 0)),
            pl.BlockSpec((1, HEAD_DIM), lambda h, b: (0, 0)),
        ],
        out_specs=pl.BlockSpec((seq, width), lambda h, b: (b, h)),
        out_shape=jax.ShapeDtypeStruct((t, d_model), BF16),
        scratch_shapes=[
            pltpu.VMEM((ATT_HG, 3, ATT_BQ, ATT_BK), F32),
            pltpu.VMEM((ATT_HG, 2 * ATT_BQ, 1), F32),
            pltpu.VMEM((ATT_HG, 2 * ATT_BQ, 2 * HEAD_DIM), F32),
            pltpu.VMEM((ATT_HG, 2 * ATT_BQ, ATT_BK), F32),
            pltpu.VMEM((ATT_HG, 2 * ATT_BQ, ATT_BK), F32),
            pltpu.VMEM((ATT_HG, 2 * ATT_BQ, HEAD_DIM), BF16),
        ],
        compiler_params=_cparams(("arbitrary", "arbitrary")),
        name="diff_attention",
    )(rel_bias, proj_qk, proj_qk, proj_plain, lam_params, g_subln)


def _mix_kernel(wg_ref, bg_ref, gmix_ref, oa_ref, od_ref, x_ref, wo_ref, gffn_ref, wr_ref, br_ref,
                x1_ref, h2_ref, topi_ref, topw_ref, rank_ref, cnt_ref, carry_ref):
    i = pl.program_id(0)
    tm = MIX_TM
    d = x_ref.shape[1]

    def gated_mix(rs):
        x = x_ref[rs, :]
        h = (x * lax.rsqrt(jnp.mean(x * x, axis=-1, keepdims=True) + EPS) * gmix_ref[...]).astype(BF16)
        cols = []
        for c in range(d // PROJ_CHUNK):
            cs = slice(c * PROJ_CHUNK, (c + 1) * PROJ_CHUNK)
            cb = slice(d + c * PROJ_CHUNK, d + (c + 1) * PROJ_CHUNK)
            ga = _sigmoid(jnp.dot(h, wg_ref[:, cs], preferred_element_type=F32) + bg_ref[:, cs]).astype(BF16)
            gb = _sigmoid(jnp.dot(h, wg_ref[:, cb], preferred_element_type=F32) + bg_ref[:, cb]).astype(BF16)
            cols.append(ga * oa_ref[rs, cs] + gb * od_ref[rs, cs])
        return jnp.concatenate(cols, axis=1)

    @pl.when(i == 0)
    def _():
        carry_ref[...] = jnp.zeros_like(carry_ref)

    tp = tm // MIX_PARTS
    parts = range(MIX_PARTS)
    rows = [slice(pp * tp, (pp + 1) * tp) for pp in parts]
    mixes = [gated_mix(rs) for rs in rows]
    x1s = [x_ref[rs, :] + jnp.dot(mx, wo_ref[...], preferred_element_type=F32) for rs, mx in zip(rows, mixes)]
    for rs, x1 in zip(rows, x1s):
        x1_ref[rs, :] = x1
    h2s = [x1 * lax.rsqrt(jnp.mean(x1 * x1, axis=-1, keepdims=True) + EPS) * gffn_ref[...] for x1 in x1s]
    for rs, h2 in zip(rows, h2s):
        h2_ref[rs, :] = _pack_halves(h2)

    curs = [_bdot_nt(wr_ref[...], h2) + br_ref[...] for h2 in h2s]
    eidx = lax.broadcasted_iota(jnp.int32, curs[0].shape, 0).astype(F32)
    vals = [[] for _ in parts]
    hots = [[] for _ in parts]
    for kk in range(TOP_K):
        mxs = [jnp.max(cur, axis=0, keepdims=True) for cur in curs]
        idxs = [jnp.min(jnp.where(cur == mx, eidx, float(N_EXPERTS)), axis=0, keepdims=True)
                for cur, mx in zip(curs, mxs)]
        for pp in parts:
            hot = eidx == idxs[pp]
            vals[pp].append(mxs[pp])
            hots[pp].append(hot)
            topi_ref[kk:kk + 1, rows[pp]] = idxs[pp].astype(jnp.int32)
            curs[pp] = jnp.where(hot, -jnp.inf, curs[pp])
    for pp in parts:
        exps = [jnp.exp(vv - vals[pp][0]) for vv in vals[pp]]
        denom = exps[0] + exps[1] + exps[2] + exps[3]
        for kk in range(TOP_K):
            topw_ref[kk:kk + 1, rows[pp]] = exps[kk] / denom

    r = lax.broadcasted_iota(jnp.int32, (tp, tp), 0)
    c = lax.broadcasted_iota(jnp.int32, (tp, tp), 1)
    earlier = (r < c).astype(F32)
    sel_fs = []
    for pp in parts:
        sel = hots[pp][0]
        for kk in range(1, TOP_K):
            sel = jnp.logical_or(sel, hots[pp][kk])
        sel_fs.append(sel.astype(F32))
    within = [_bdot(sf, earlier) for sf in sel_fs]
    totals = [jnp.sum(sf, axis=-1, keepdims=True) for sf in sel_fs]
    run = carry_ref[...]
    for pp in parts:
        before = within[pp] + run
        for kk in range(TOP_K):
            rank_ref[kk:kk + 1, rows[pp]] = jnp.sum(jnp.where(hots[pp][kk], before, 0.0), axis=0,
                                                    keepdims=True).astype(jnp.int32)
        run = run + totals[pp]
    carry_ref[...] = run
    cnt_ref[...] = run.astype(jnp.int32)


def _mix_project_route(w_gate, b_gate, g_mix, oa, od, x2d, w_o, g_ffn, w_r_t, b_r, d_model):
    t = x2d.shape[0]
    tm = MIX_TM
    full = lambda shape: pl.BlockSpec(shape, lambda i: (0, 0))
    row = lambda: pl.BlockSpec((tm, d_model), lambda i: (i, 0))
    krow = lambda: pl.BlockSpec((TOP_K, tm), lambda i: (0, i))
    return pl.pallas_call(
        _mix_kernel,
        grid=(t // tm,),
        in_specs=[
            full((d_model, 2 * d_model)), full((1, 2 * d_model)), full((1, d_model)),
            row(), row(), row(),
            full((d_model, d_model)), full((1, d_model)), full((N_EXPERTS, d_model)), full((N_EXPERTS, 1)),
        ],
        out_specs=[row(), pl.BlockSpec((tm, d_model // 2), lambda i: (i, 0)),
                   krow(), krow(), krow(), full((N_EXPERTS, 1))],
        out_shape=[
            jax.ShapeDtypeStruct((t, d_model), F32),
            jax.ShapeDtypeStruct((t, d_model // 2), jnp.int32),
            jax.ShapeDtypeStruct((TOP_K, t), jnp.int32),
            jax.ShapeDtypeStruct((TOP_K, t), F32),
            jax.ShapeDtypeStruct((TOP_K, t), jnp.int32),
            jax.ShapeDtypeStruct((N_EXPERTS, 1), jnp.int32),
        ],
        scratch_shapes=[pltpu.VMEM((N_EXPERTS, 1), F32)],
        compiler_params=_cparams(("arbitrary",)),
        name="merge_outproj_route",
    )(w_gate, b_gate, g_mix, oa, od, x2d, w_o, g_ffn, w_r_t, b_r)


def _pack_halves(x):
    half = x.shape[1] // 2
    bits = pltpu.bitcast(x.astype(BF16).astype(F32), jnp.int32)
    return bits[:, :half] | lax.shift_right_logical(bits[:, half:], 16)


def _unpack_halves(p):
    hi = pltpu.bitcast(p & jnp.int32(-65536), F32)
    lo = pltpu.bitcast(lax.shift_left(p, 16), F32)
    return jnp.concatenate([hi, lo], axis=1)


def _expert_kernel(be_ref, nu_ref, ne_ref, x_ref, wup_hbm, bup_ref, wdn_hbm, bdn_ref, *rest):
    y_ref, wup_bf, wdn_bf, wup_f, wdn_f, sem = rest[-6:]
    i = pl.program_id(0)
    d_ff = wdn_f.shape[0]

    def weight_copies(e):
        return (pltpu.make_async_copy(wup_hbm.at[e], wup_f, sem.at[0]),
                pltpu.make_async_copy(wdn_hbm.at[e], wdn_f, sem.at[1]))

    @pl.when(i == 0)
    def _():
        for cp in weight_copies(be_ref[0]):
            cp.start()

    @pl.when(jnp.logical_or(i == 0, be_ref[i] != be_ref[jnp.maximum(i - 1, 0)]))
    def _():
        for cp in weight_copies(be_ref[i]):
            cp.wait()
        rr = lax.broadcasted_iota(jnp.int32, (2 * LANES, 2 * LANES), 0)
        cc = lax.broadcasted_iota(jnp.int32, (2 * LANES, 2 * LANES), 1)
        pick = jnp.where(cc < LANES, 2 * cc, 2 * (cc - LANES) + 1)
        perm = (rr == pick).astype(BF16)
        for g in range(wup_f.shape[1] // (2 * LANES)):
            cs = slice(g * 2 * LANES, (g + 1) * 2 * LANES)
            wup_bf[:, cs] = jnp.dot(wup_f[:, cs].astype(BF16), perm,
                                    preferred_element_type=F32).astype(BF16)
        wdn_bf[...] = wdn_f[...].astype(BF16)

        @pl.when(ne_ref[i] >= 0)
        def _():
            for cp in weight_copies(ne_ref[i]):
                cp.start()

    @pl.when(i < nu_ref[0])
    def _():
        x = _unpack_halves(x_ref[...])
        hid = jnp.dot(x.astype(BF16), wup_bf[...], preferred_element_type=F32) + bup_ref[0]
        acts = []
        for g in range(hid.shape[1] // (2 * LANES)):
            glu = jnp.minimum(hid[:, g * 2 * LANES:g * 2 * LANES + LANES], SWIGLU_LIMIT)
            lin = jnp.clip(hid[:, g * 2 * LANES + LANES:(g + 1) * 2 * LANES], -SWIGLU_LIMIT, SWIGLU_LIMIT)
            acts.append(glu * _sigmoid(SWIGLU_ALPHA * glu) * (lin + 1.0))
        act = jnp.concatenate(acts, axis=1)
        assert act.shape[1] == d_ff
        y = jnp.dot(act.astype(BF16), wdn_bf[...], preferred_element_type=F32) + bdn_ref[0]
        y_ref[...] = _pack_halves(y)

    @pl.when(i >= nu_ref[0])
    def _():
        y_ref[...] = jnp.zeros(y_ref.shape, y_ref.dtype)


def _experts(block_e, n_used, xs, y_prev, first_block, n_rows_total, w_up, b_up, w_down, b_down):
    n_rows, half = xs.shape
    d = w_up.shape[1]
    nb = n_rows // MOE_RB
    two_ff = w_up.shape[2]
    d_ff = w_down.shape[1]
    blk = jnp.arange(nb, dtype=jnp.int32)
    later = jnp.logical_and(blk[None, :] > blk[:, None], block_e[None, :] != block_e[:, None])
    next_e = jnp.min(jnp.where(later, block_e[None, :], N_EXPERTS), axis=1)
    next_e = jnp.where(next_e == N_EXPERTS, -1, next_e).astype(jnp.int32)
    in_specs = [
        pl.BlockSpec((MOE_RB, half), lambda i, be, nu, ne: (jnp.maximum(jnp.minimum(i, nu[0] - 1), 0), 0)),
        pl.BlockSpec(memory_space=pl.ANY),
        pl.BlockSpec((1, 1, two_ff), lambda i, be, nu, ne: (be[i], 0, 0)),
        pl.BlockSpec(memory_space=pl.ANY),
        pl.BlockSpec((1, 1, d), lambda i, be, nu, ne: (be[i], 0, 0)),
    ]
    operands = [block_e, n_used, next_e, xs, w_up, b_up, w_down, b_down]
    aliases = {}
    if y_prev is not None:
        in_specs.append(pl.BlockSpec(memory_space=pl.ANY))
        aliases = {len(operands): 0}
        operands.append(y_prev)
    grid_spec = pltpu.PrefetchScalarGridSpec(
        num_scalar_prefetch=3,
        grid=(nb,),
        in_specs=in_specs,
        out_specs=pl.BlockSpec((MOE_RB, half), lambda i, be, nu, ne: (first_block + i, 0)),
        scratch_shapes=[pltpu.VMEM((d, two_ff), BF16), pltpu.VMEM((d_ff, d), BF16),
                        pltpu.VMEM((d, two_ff), F32), pltpu.VMEM((d_ff, d), F32),
                        pltpu.SemaphoreType.DMA((2,))],
    )
    return pl.pallas_call(
        _expert_kernel,
        grid_spec=grid_spec,
        out_shape=jax.ShapeDtypeStruct((n_rows_total, half), jnp.int32),
        input_output_aliases=aliases,
        compiler_params=_cparams(("arbitrary",)),
        name="moe_experts",
    )(*operands)


def _sc_invert_slots(dest_flat, n_rows):
    n_assign = dest_flat.shape[0]
    n_workers = SC_CORES * SC_SUBCORES
    rows_per_w = n_rows // n_workers
    chunk = SC_SCAN_CHUNK
    assert n_rows % n_workers == 0 and rows_per_w % SC_LANES == 0 and n_assign % chunk == 0
    mesh = plsc.VectorSubcoreMesh(core_axis_name="c", subcore_axis_name="s",
                                  num_cores=SC_CORES, num_subcores=SC_SUBCORES)

    n_chunks = n_assign // chunk
    assert n_chunks % 2 == 0 and chunk % (SC_SCAN_UNROLL * SC_LANES) == 0

    def body(dest_hbm, out_hbm, dest_a, dest_b, map_v, sem_a, sem_b):
        wid = lax.axis_index("s") * SC_CORES + lax.axis_index("c")
        base = wid * rows_per_w
        lanes = lax.broadcasted_iota(jnp.int32, (SC_LANES,), 0)

        def fetch(ci, dest_v, sem):
            off = pl.multiple_of(ci * chunk, chunk)
            return pltpu.make_async_copy(dest_hbm.at[pl.ds(off, chunk)], dest_v, sem)

        def scan(ci, dest_v):
            @pl.loop(0, chunk, step=SC_SCAN_UNROLL * SC_LANES)
            def _(j0):
                for u in range(SC_SCAN_UNROLL):
                    j = j0 + u * SC_LANES
                    local = dest_v[pl.ds(j, SC_LANES)] - base
                    mine = jnp.logical_and(local >= 0, local < rows_per_w)
                    plsc.store_scatter(map_v, [jnp.where(mine, local, 0)], ci * chunk + j + lanes, mask=mine)

        fetch(0, dest_a, sem_a).start()

        @pl.loop(0, rows_per_w, step=SC_LANES)
        def _(r0):
            map_v[pl.ds(r0, SC_LANES)] = jnp.full((SC_LANES,), -1, jnp.int32)

        @pl.loop(0, n_chunks, step=2)
        def _(ci):
            fetch(ci + 1, dest_b, sem_b).start()
            fetch(ci, dest_a, sem_a).wait()
            scan(ci, dest_a)
            nxt = jnp.minimum(ci + 2, n_chunks - 1)
            fetch(nxt, dest_a, sem_a).start()
            fetch(ci + 1, dest_b, sem_b).wait()
            scan(ci + 1, dest_b)

        fetch(n_chunks - 1, dest_a, sem_a).wait()
        pltpu.sync_copy(map_v, out_hbm.at[pl.ds(base, rows_per_w)])

    return pl.kernel(
        body,
        out_type=jax.ShapeDtypeStruct((n_rows,), jnp.int32),
        mesh=mesh,
        scratch_types=[pltpu.VMEM((chunk,), jnp.int32), pltpu.VMEM((chunk,), jnp.int32),
                       pltpu.VMEM((rows_per_w,), jnp.int32),
                       pltpu.SemaphoreType.DMA, pltpu.SemaphoreType.DMA],
        compiler_params=pltpu.CompilerParams(needs_layout_passes=False),
        name="moe_slot_inverse",
    )(dest_flat)


def _sc_gather_rows(table, idx):
    n_idx = idx.shape[0]
    d = table.shape[1]
    n_workers = SC_CORES * SC_SUBCORES
    per_worker = n_idx // n_workers
    n_chunks = per_worker // SC_GATHER_ROWS
    assert n_idx % n_workers == 0 and per_worker % SC_GATHER_ROWS == 0
    mesh = plsc.VectorSubcoreMesh(core_axis_name="c", subcore_axis_name="s",
                                  num_cores=SC_CORES, num_subcores=SC_SUBCORES)

    assert n_chunks % 2 == 0

    def body(table_hbm, idx_hbm, out_hbm, idx_v, rows_a, rows_b, sem_a, sem_b):
        wid = lax.axis_index("s") * SC_CORES + lax.axis_index("c")
        base = wid * per_worker
        pltpu.sync_copy(idx_hbm.at[pl.ds(base, per_worker)], idx_v)

        def gather(ci, rows_v, sem):
            off = pl.multiple_of(ci * SC_GATHER_ROWS, SC_GATHER_ROWS)
            return pltpu.make_async_copy(table_hbm.at[idx_v.at[pl.ds(off, SC_GATHER_ROWS)]], rows_v, sem)

        def put(ci, rows_v):
            off = pl.multiple_of(ci * SC_GATHER_ROWS, SC_GATHER_ROWS)
            pltpu.sync_copy(rows_v, out_hbm.at[pl.ds(base + off, SC_GATHER_ROWS)])

        gather(0, rows_a, sem_a).start()

        @pl.loop(0, n_chunks, step=2)
        def _(ci):
            gather(ci + 1, rows_b, sem_b).start()
            gather(ci, rows_a, sem_a).wait()
            put(ci, rows_a)
            nxt = jnp.minimum(ci + 2, n_chunks - 1)
            gather(nxt, rows_a, sem_a).start()
            gather(ci + 1, rows_b, sem_b).wait()
            put(ci + 1, rows_b)

        gather(n_chunks - 1, rows_a, sem_a).wait()

    return pl.kernel(
        body,
        out_type=jax.ShapeDtypeStruct((n_idx, d), table.dtype),
        mesh=mesh,
        scratch_types=[
            pltpu.VMEM((per_worker,), jnp.int32),
            pltpu.VMEM((SC_GATHER_ROWS, d), table.dtype),
            pltpu.VMEM((SC_GATHER_ROWS, d), table.dtype),
            pltpu.SemaphoreType.DMA,
            pltpu.SemaphoreType.DMA,
        ],
        name="moe_slot_gather",
    )(table, idx)


def _combine_kernel(x1_ref, w_ref, y0_ref, y1_ref, y2_ref, y3_ref, o_ref):
    w = w_ref[...]
    out = x1_ref[...]
    for kk, y_ref in enumerate((y0_ref, y1_ref, y2_ref, y3_ref)):
        out = out + w[:, kk:kk + 1] * _unpack_halves(y_ref[...])
    o_ref[...] = out


def _combine(x1, w_tok, y_slots):
    t, d = x1.shape
    tc = COMB_TC
    nt = t // tc
    yspec = lambda kk: pl.BlockSpec((tc, d // 2), lambda i: (kk * nt + i, 0))
    return pl.pallas_call(
        _combine_kernel,
        grid=(nt,),
        in_specs=[
            pl.BlockSpec((tc, d), lambda i: (i, 0)),
            pl.BlockSpec((tc, TOP_K), lambda i: (i, 0)),
            yspec(0), yspec(1), yspec(2), yspec(3),
        ],
        out_specs=pl.BlockSpec((tc, d), lambda i: (i, 0)),
        out_shape=jax.ShapeDtypeStruct((t, d), F32),
        compiler_params=_cparams(("parallel",)),
        name="moe_combine",
    )(x1, w_tok, y_slots, y_slots, y_slots, y_slots)


def _moe(x1, h2, topi, topw, rank, counts, w_up, b_up, w_down, b_down):
    t, d = x1.shape
    n_assign = t * TOP_K
    nb = -(-n_assign // MOE_RB) + N_EXPERTS
    n_rows = nb * MOE_RB
    counts = counts[:, 0]
    padded = (counts + MOE_RB - 1) // MOE_RB * MOE_RB
    padded_end = jnp.cumsum(padded)
    padded_start = padded_end - padded
    expert_ids = jnp.arange(N_EXPERTS, dtype=jnp.int32)[:, None, None]
    start_of = jnp.sum(jnp.where(topi[None] == expert_ids, padded_start[:, None, None], 0), axis=0)
    dest = (start_of + rank).astype(jnp.int32)
    n_used = (padded_end[-1] // MOE_RB).astype(jnp.int32)
    blk = jnp.minimum(jnp.arange(nb, dtype=jnp.int32), n_used - 1)
    block_e = jnp.minimum(jnp.sum(padded_end[None, :] <= (blk * MOE_RB)[:, None], axis=1),
                          N_EXPERTS - 1).astype(jnp.int32)
    slot_of = _sc_invert_slots(dest.reshape(-1), n_rows)
    src_tok = jnp.where(slot_of < 0, jnp.arange(n_rows, dtype=jnp.int32), slot_of) % t

    nb_a = nb // 4
    y_rows = None
    for first, n_blk in ((0, nb_a), (nb_a, nb - nb_a)):
        xs = _sc_gather_rows(h2, lax.slice(src_tok, (first * MOE_RB,), ((first + n_blk) * MOE_RB,)))
        used = jnp.clip(n_used - first, 0, n_blk).reshape(1)
        y_rows = _experts(lax.slice(block_e, (first,), (first + n_blk,)), used, xs, y_rows, first, n_rows,
                          w_up, b_up, w_down, b_down)
    y_slots = _sc_gather_rows(y_rows, dest.reshape(-1))
    return _combine(x1, topw.T, y_slots)


def kernel(x, g_mix, w_in, b_gate, conv_w, a_log, dt_bias, g_delta_out, q_norm, k_norm, lambda_q1, lambda_k1, lambda_q2, lambda_k2, g_subln, rel_bias, w_o, g_ffn, w_router, b_router, w_up, b_up, w_down, b_down):
    bsz, seq, d = x.shape
    depth = g_mix.shape[0]
    n_heads = d // HEAD_DIM
    t = bsz * seq
    d_ff = w_down.shape[2]
    assert d % PROJ_TN == 0 and t % PROJ_TM == 0 and seq % GDN_TB == 0 and seq % ATT_BQ == 0
    assert t % MIX_TM == 0 and t % COMB_TC == 0 and n_heads % GDN_HG == 0
    assert (t * TOP_K) % MOE_RB == 0
    assert 2 * n_heads <= 2 * SUBLANES

    x2d = x.reshape(t, d)
    for l in range(depth):
        wl = w_in[l]
        c0 = 4 * d
        c1 = c0 + 2 * n_heads
        c2 = c1 + 2 * d
        c3 = c2 + d
        w_small = jnp.pad(wl[:, c0:c1], ((0, 0), (0, LANES - 2 * n_heads)))
        gm = g_mix[l].reshape(1, d)
        head_pad = jnp.zeros((LANES - 2 * n_heads,), F32)
        alog = jnp.concatenate([jnp.zeros((n_heads,), F32), a_log[l], head_pad])
        dtb = jnp.concatenate([jnp.zeros((n_heads,), F32), dt_bias[l], head_pad])
        rows_t = 2 * n_heads
        beta_decay = (w_small.astype(BF16), w_small[:, :rows_t].T.astype(BF16),
                      alog.reshape(1, LANES), dtb.reshape(1, LANES),
                      alog[:rows_t].reshape(rows_t, 1), dtb[:rows_t].reshape(rows_t, 1), n_heads)
        w_plain = jnp.concatenate([wl[:, :c0], wl[:, c2:c3]], axis=1).astype(BF16)
        proj_plain, small, small_t = _input_projection(x2d, gm, w_plain, jnp.zeros((1, 5 * d), F32), "plain",
                                                       beta_decay)
        qk_gain = jnp.concatenate([jnp.tile(q_norm[l] * (DH_DIFF ** -0.5), 2 * n_heads),
                                   jnp.tile(k_norm[l], 2 * n_heads)]).reshape(1, 2 * d)
        proj_qk = _input_projection(x2d, gm, wl[:, c1:c2].astype(BF16), qk_gain, "qknorm")

        oa = _gated_delta(proj_plain, small, small_t, conv_w[l], g_delta_out[l].reshape(1, HEAD_DIM),
                          bsz, seq, n_heads, d)

        lam_init = 0.8 - 0.6 * math.exp(-0.3 * l)
        lam_params = jnp.stack([lambda_q1[l], lambda_k1[l], lambda_q2[l], lambda_k2[l]])
        od = _diff_attention(proj_qk, proj_plain, rel_bias, lam_params, g_subln[l].reshape(1, HEAD_DIM),
                             bsz, seq, n_heads, d, lam_init)

        x1, h2, topi, topw, rank, counts = _mix_project_route(
            wl[:, c3:].astype(BF16), b_gate[l].reshape(1, 2 * d), gm, oa, od, x2d, w_o[l].astype(BF16),
            g_ffn[l].reshape(1, d),
            w_router[l].T, b_router[l].reshape(N_EXPERTS, 1), d)

        b_up_l = b_up[l].reshape(N_EXPERTS, 2 * d_ff // (2 * LANES), LANES, 2)
        b_up_l = jnp.swapaxes(b_up_l, 2, 3).reshape(N_EXPERTS, 1, 2 * d_ff)
        x2d = _moe(x1, h2, topi, topw, rank, counts, w_up[l], b_up_l,
                   w_down[l], b_down[l].reshape(N_EXPERTS, 1, d))
    return x2d.reshape(bsz, seq, d)
```

```python
import functools
import math

import jax
import jax.numpy as jnp
from jax import lax
from jax.experimental import pallas as pl
from jax.experimental.pallas import tpu as pltpu
from jax.experimental.pallas import tpu_sc as plsc

F32 = jnp.float32
BF16 = jnp.bfloat16

HEAD_DIM = 128
DH_DIFF = HEAD_DIM // 2
CONV_WIDTH = 4
CHUNK = 64
N_BUCKETS = 32
MAX_DISTANCE = 128
N_EXPERTS = 32
TOP_K = 4
SWIGLU_LIMIT = 7.0
SWIGLU_ALPHA = 1.702
EPS = 1e-6
NEG_BIG = -1e30

LANES = 128
SUBLANES = 8
VMEM_LIMIT = 56 * 1024 * 1024
SC_CORES = 2
SC_SUBCORES = 16
SC_LANES = 16
SC_GATHER_ROWS = 64
SC_SCAN_CHUNK = 4096
SC_SCAN_UNROLL = 4

PROJ_TM = 2048
PROJ_TN = 1024
PROJ_CHUNK = 256
GDN_TB = 256
GDN_HG = 8
ATT_HG = 1
ATT_BQ = 512
ATT_BK = 512
MIX_TM = 1024
MIX_PARTS = 2
MOE_RB = 512
COMB_TC = 512


def _cparams(sem):
    return pltpu.CompilerParams(dimension_semantics=sem, vmem_limit_bytes=VMEM_LIMIT)


def _sigmoid(x):
    return 0.5 * jnp.tanh(0.5 * x) + 0.5


def _bdot(a, b):
    return jnp.dot(a.astype(BF16), b.astype(BF16), preferred_element_type=F32)


def _bdot_nt(a, b):
    return lax.dot_general(a.astype(BF16), b.astype(BF16), (((1,), (1,)), ((), ())),
                           preferred_element_type=F32)


def _bdot_tn(a, b):
    return lax.dot_general(a.astype(BF16), b.astype(BF16), (((0,), (0,)), ((), ())),
                           preferred_element_type=F32)


def _beta_decay(acc, idx, alog, dtb, n_heads):
    beta = _sigmoid(acc)
    z = acc + dtb
    softplus = jnp.maximum(z, 0.0) + jnp.log1p(jnp.exp(-jnp.abs(z)))
    gdec = -jnp.exp(alog) * softplus
    return jnp.where(idx < n_heads, beta, jnp.where(idx < 2 * n_heads, gdec, 0.0))


def _proj_kernel(x_ref, g_ref, w_ref, aux_ref, *rest, mode, n_heads):
    o_ref, h_ref = rest[-2:] if n_heads is None else (rest[6], rest[-1])

    @pl.when(pl.program_id(1) == 0)
    def _():
        x = x_ref[...]
        ms = jnp.mean(x * x, axis=-1, keepdims=True)
        h_ref[...] = (x * lax.rsqrt(ms + EPS) * g_ref[...]).astype(BF16)
        if n_heads is not None:
            ws_ref, wst_ref, alog_ref, dtb_ref, alog_t_ref, dtb_t_ref, _, os_ref, ost_ref, _ = rest
            hb = h_ref[...]
            acc = jnp.dot(hb, ws_ref[...], preferred_element_type=F32)
            lane = lax.broadcasted_iota(jnp.int32, acc.shape, 1)
            os_ref[...] = _beta_decay(acc, lane, alog_ref[...], dtb_ref[...], n_heads)
            acc_t = lax.dot_general(wst_ref[...], hb, (((1,), (1,)), ((), ())),
                                    preferred_element_type=F32)
            sub = lax.broadcasted_iota(jnp.int32, acc_t.shape, 0)
            ost_ref[...] = _beta_decay(acc_t, sub, alog_t_ref[...], dtb_t_ref[...], n_heads)

    h = h_ref[...]
    lo = lax.broadcasted_iota(jnp.int32, (1, LANES), 1) < DH_DIFF
    for c in range(PROJ_TN // PROJ_CHUNK):
        cs = slice(c * PROJ_CHUNK, (c + 1) * PROJ_CHUNK)
        acc = jnp.dot(h, w_ref[:, cs], preferred_element_type=F32)
        if mode == "plain":
            o_ref[:, cs] = acc.astype(o_ref.dtype)
        else:
            for g in range(PROJ_CHUNK // LANES):
                sl = slice(c * PROJ_CHUNK + g * LANES, c * PROJ_CHUNK + (g + 1) * LANES)
                y = acc[:, g * LANES:(g + 1) * LANES]
                y2 = y * y
                s_lo = jnp.sum(jnp.where(lo, y2, 0.0), axis=-1, keepdims=True)
                s_hi = jnp.sum(jnp.where(lo, 0.0, y2), axis=-1, keepdims=True)
                r = jnp.where(lo, lax.rsqrt(s_lo / DH_DIFF + EPS), lax.rsqrt(s_hi / DH_DIFF + EPS))
                o_ref[:, sl] = (y * r * aux_ref[:, sl]).astype(o_ref.dtype)


def _input_projection(x2d, g_mix, w, aux, mode, beta_decay=None):
    t, d = x2d.shape
    n = w.shape[1]
    full = lambda shape: pl.BlockSpec(shape, lambda i, j: (0, 0))
    in_specs = [
        pl.BlockSpec((PROJ_TM, d), lambda i, j: (i, 0)),
        full((1, d)),
        pl.BlockSpec((d, PROJ_TN), lambda i, j: (0, j)),
        pl.BlockSpec((1, PROJ_TN), lambda i, j: (0, j)),
    ]
    operands = [x2d, g_mix, w, aux]
    out_specs = [pl.BlockSpec((PROJ_TM, PROJ_TN), lambda i, j: (i, j))]
    out_shape = [jax.ShapeDtypeStruct((t, n), BF16)]
    n_heads = None
    if beta_decay is not None:
        n_heads = beta_decay[-1]
        rows_t = 2 * n_heads
        in_specs += [full((d, LANES)), full((rows_t, d)), full((1, LANES)), full((1, LANES)),
                     full((rows_t, 1)), full((rows_t, 1))]
        operands += list(beta_decay[:-1])
        out_specs += [pl.BlockSpec((PROJ_TM, LANES), lambda i, j: (i, 0)),
                      pl.BlockSpec((rows_t, PROJ_TM), lambda i, j: (0, i))]
        out_shape += [jax.ShapeDtypeStruct((t, LANES), F32), jax.ShapeDtypeStruct((rows_t, t), F32)]
    out = pl.pallas_call(
        functools.partial(_proj_kernel, mode=mode, n_heads=n_heads),
        grid=(t // PROJ_TM, n // PROJ_TN),
        in_specs=in_specs,
        out_specs=out_specs,
        out_shape=out_shape,
        scratch_shapes=[pltpu.VMEM((PROJ_TM, d), BF16)],
        compiler_params=_cparams(("parallel", "arbitrary")),
        name="input_projection_" + mode,
    )(*operands)
    return out[0] if beta_decay is None else out


def _gdn_kernel(q_ref, k_ref, v_ref, z_ref, sm_ref, smt_ref, cwq_ref, cwk_ref, cwv_ref, gout_ref,
                o_ref, state_ref, qp_ref, kp_ref, vp_ref, vn_ref, *, n_heads):
    hg = pl.program_id(1)
    s = pl.program_id(2)
    tb = GDN_TB
    pad = SUBLANES
    width = GDN_HG * HEAD_DIM

    @pl.when(s == 0)
    def _():
        state_ref[...] = jnp.zeros_like(state_ref)
        for p_ref in (qp_ref, kp_ref, vp_ref):
            p_ref[0:pad, :] = jnp.zeros((pad, width), F32)

    r = lax.broadcasted_iota(jnp.int32, (tb, tb), 0)
    c = lax.broadcasted_iota(jnp.int32, (tb, tb), 1)
    delay_mat = jnp.concatenate([(r - c == dd).astype(BF16) for dd in range(1, CONV_WIDTH)], axis=0)

    def conv_silu(x_ref, p_ref, cw_ref):
        x = x_ref[...]
        xf = x.astype(F32)
        p_ref[pad:2 * pad, :] = xf[0:pad]
        delayed = jnp.dot(delay_mat, x, preferred_element_type=F32)
        acc = cw_ref[CONV_WIDTH - 1:CONV_WIDTH, :] * xf
        for dd in range(1, CONV_WIDTH):
            first = p_ref[pad - dd:2 * pad - dd, :]
            xd = jnp.concatenate([first, delayed[(dd - 1) * tb + pad:dd * tb]], axis=0)
            acc = acc + cw_ref[CONV_WIDTH - 1 - dd:CONV_WIDTH - dd, :] * xd
        p_ref[0:pad, :] = xf[tb - pad:tb]
        return acc * _sigmoid(acc)

    q_all = conv_silu(q_ref, qp_ref, cwq_ref)
    k_all = conv_silu(k_ref, kp_ref, cwk_ref)
    v_all = conv_silu(v_ref, vp_ref, cwv_ref)

    shift = int(math.log2(CHUNK))
    same = (r >> shift) == (c >> shift)
    incl = jnp.logical_and(same, c <= r)
    strict = jnp.logical_and(same, c < r)

    small = sm_ref[...]
    small_t = smt_ref[...]
    lane = lax.broadcasted_iota(jnp.int32, small.shape, 1)
    def split3(a):
        hi = a.astype(BF16)
        r1 = a - hi.astype(F32)
        mid = r1.astype(BF16)
        lo = (r1 - mid.astype(F32)).astype(BF16)
        return hi.astype(F32), mid.astype(F32), lo.astype(F32)

    part = 2 * n_heads
    s_hi, s_mid, s_lo = split3(small)
    small3 = jnp.where(lane < part, s_hi,
                       jnp.where(lane < 2 * part, pltpu.roll(s_mid, part, 1),
                                 jnp.where(lane < 3 * part, pltpu.roll(s_lo, 2 * part, 1), 0.0)))
    both = _bdot(jnp.concatenate([incl.astype(F32), same.astype(F32)], axis=0), small3)
    gcum = both[:tb]
    gtot = both[tb:]
    gcum_t = _bdot(jnp.concatenate(split3(small_t), axis=0),
                   jnp.logical_and(same, r <= c).astype(F32))
    sub3 = lax.broadcasted_iota(jnp.int32, gcum_t.shape, 0)

    heads = range(GDN_HG)
    hsl = [slice(hh * HEAD_DIM, (hh + 1) * HEAD_DIM) for hh in heads]
    qs = [q_all[:, hs] for hs in hsl]
    ks = [k_all[:, hs] for hs in hsl]
    vs = [v_all[:, hs] for hs in hsl]
    qs = [q * lax.rsqrt(jnp.sum(q * q, axis=-1, keepdims=True) + EPS) * (HEAD_DIM ** -0.5) for q in qs]
    ks = [k * lax.rsqrt(jnp.sum(k * k, axis=-1, keepdims=True) + EPS) for k in ks]

    def col_of(arr, idx):
        return jnp.sum(jnp.where(lane == idx, arr, 0.0), axis=-1, keepdims=True)

    def terms_of(pos, idx):
        return jnp.logical_or(pos == idx, jnp.logical_or(pos == idx + part, pos == idx + 2 * part))

    head_ids = [hg * GDN_HG + hh for hh in heads]
    betas = [col_of(small, hd) for hd in head_ids]
    gcs = [jnp.sum(jnp.where(terms_of(lane, hd + n_heads), gcum, 0.0), axis=-1, keepdims=True)
           for hd in head_ids]
    gls = [jnp.sum(jnp.where(terms_of(lane, hd + n_heads), gtot, 0.0), axis=-1, keepdims=True)
           for hd in head_ids]
    gc_rows = [jnp.sum(jnp.where(terms_of(sub3, hd + n_heads), gcum_t, 0.0), axis=0, keepdims=True)
               for hd in head_ids]

    decays = [jnp.where(incl, jnp.exp(jnp.minimum(gc - gr, 0.0)), 0.0) for gc, gr in zip(gcs, gc_rows)]
    kbs = [k * b for k, b in zip(ks, betas)]
    kks = [_bdot_nt(kb, k) for kb, k in zip(kbs, ks)]
    pws = [jnp.where(strict, -(kk * dc), 0.0) for kk, dc in zip(kks, decays)]
    n_chunks = tb // CHUNK
    cat_row = lax.broadcasted_iota(jnp.int32, (CHUNK, tb), 0)
    cat_lane = lax.broadcasted_iota(jnp.int32, (CHUNK, tb), 1)
    lane_chunk = cat_lane >> shift

    def block_diag(m_cat):
        return jnp.concatenate([jnp.where(lane_chunk == ci, m_cat, 0.0) for ci in range(n_chunks)], axis=0)

    def cat_of(m_bd):
        out = m_bd[0:CHUNK]
        for ci in range(1, n_chunks):
            out = out + m_bd[ci * CHUNK:(ci + 1) * CHUNK]
        return out

    pcats = [cat_of(pw) for pw in pws]
    eye_cat = ((cat_lane & (CHUNK - 1)) == cat_row).astype(F32)
    tcats = [eye_cat + pc for pc in pcats]
    pcats = [_bdot(pc, block_diag(pc)) for pc in pcats]
    n_levels = int(math.log2(CHUNK))
    for lev in range(1, n_levels):
        bds = [block_diag(pc) for pc in pcats]
        if lev < n_levels - 1:
            prods = [_bdot(jnp.concatenate([pc, tc], axis=0), bd) for pc, tc, bd in zip(pcats, tcats, bds)]
            pcats = [pr[:CHUNK] for pr in prods]
            tcats = [tc + pr[CHUNK:] for tc, pr in zip(tcats, prods)]
        else:
            tcats = [tc + _bdot(tc, bd) for tc, bd in zip(tcats, bds)]
    tmats = [block_diag(tc) for tc in tcats]
    egcs = [jnp.exp(gc) for gc in gcs]
    uws = [_bdot(tm, jnp.concatenate([v * b, kb * eg], axis=1))
           for tm, v, b, kb, eg in zip(tmats, vs, betas, kbs, egcs)]
    us = [uw[:, :HEAD_DIM] for uw in uws]
    ws = [uw[:, HEAD_DIM:] for uw in uws]
    qkm = [_bdot_nt(q, k) for q, k in zip(qs, ks)]
    qkm = [jnp.where(incl, x * dc, 0.0) for x, dc in zip(qkm, decays)]
    q_decs = [q * eg for q, eg in zip(qs, egcs)]
    k_ends = [k * jnp.exp(gl - gc) for k, gl, gc in zip(ks, gls, gcs)]

    for hh in heads:
        vn_ref[hh] = jnp.zeros((tb, HEAD_DIM), F32)
    outs = [[] for _ in heads]
    for ci in range(tb // CHUNK):
        cs = slice(ci * CHUNK, (ci + 1) * CHUNK)
        sts = [state_ref[hh] for hh in heads]
        ws_qs = [_bdot(jnp.concatenate([ws[hh][cs], q_decs[hh][cs]], axis=0), sts[hh]) for hh in heads]
        v_news = [us[hh][cs] - ws_qs[hh][:CHUNK] for hh in heads]
        for hh in heads:
            vn_ref[hh, cs, :] = v_news[hh]
        intra = [_bdot(qkm[hh][cs], vn_ref[hh]) for hh in heads]
        upd = [_bdot_tn(k_ends[hh][cs], v_news[hh]) for hh in heads]
        for hh in heads:
            outs[hh].append(ws_qs[hh][CHUNK:] + intra[hh])
            g_last = gls[hh][ci * CHUNK:ci * CHUNK + 1, :]
            state_ref[hh] = sts[hh] * jnp.exp(g_last) + upd[hh]
    for hh in heads:
        o = jnp.concatenate(outs[hh], axis=0)
        o = o * lax.rsqrt(jnp.mean(o * o, axis=-1, keepdims=True) + EPS) * gout_ref[...]
        zz = z_ref[:, hsl[hh]].astype(F32)
        o_ref[:, hsl[hh]] = (o * (zz * _sigmoid(zz))).astype(o_ref.dtype)


def _gated_delta(big, small, small_t, conv_w, g_out, bsz, seq, n_heads, d_model):
    t = bsz * seq
    tb = GDN_TB
    ns = seq // tb
    width = GDN_HG * HEAD_DIM
    nhg = n_heads // GDN_HG
    blocks_per_group = d_model // width
    rows_t = small_t.shape[0]

    def colspec(group):
        return pl.BlockSpec((tb, width), lambda b, h, s: (b * ns + s, group * blocks_per_group + h))

    def cwspec(group):
        return pl.BlockSpec((CONV_WIDTH, width), lambda b, h, s: (0, group * blocks_per_group + h))

    return pl.pallas_call(
        functools.partial(_gdn_kernel, n_heads=n_heads),
        grid=(bsz, nhg, ns),
        in_specs=[
            colspec(0), colspec(1), colspec(2), colspec(3),
            pl.BlockSpec((tb, LANES), lambda b, h, s: (b * ns + s, 0)),
            pl.BlockSpec((rows_t, tb), lambda b, h, s: (0, b * ns + s)),
            cwspec(0), cwspec(1), cwspec(2),
            pl.BlockSpec((1, HEAD_DIM), lambda b, h, s: (0, 0)),
        ],
        out_specs=pl.BlockSpec((tb, width), lambda b, h, s: (b * ns + s, h)),
        out_shape=jax.ShapeDtypeStruct((t, d_model), BF16),
        scratch_shapes=[
            pltpu.VMEM((GDN_HG, HEAD_DIM, HEAD_DIM), F32),
            pltpu.VMEM((2 * SUBLANES, width), F32),
            pltpu.VMEM((2 * SUBLANES, width), F32),
            pltpu.VMEM((2 * SUBLANES, width), F32),
            pltpu.VMEM((GDN_HG, tb, HEAD_DIM), F32),
        ],
        compiler_params=_cparams(("parallel", "parallel", "arbitrary")),
        name="gated_delta",
    )(big, big, big, big, small, small_t, conv_w, conv_w, conv_w, g_out)


def _t5_bucket(n):
    max_exact = N_BUCKETS // 2
    nf = jnp.maximum(n, 1).astype(F32)
    large = max_exact + (jnp.log(nf / max_exact) / math.log(MAX_DISTANCE / max_exact)
                         * (N_BUCKETS - max_exact)).astype(jnp.int32)
    large = jnp.minimum(large, N_BUCKETS - 1)
    return jnp.where(n < max_exact, n, large)


def _attn_kernel(rb_ref, q_ref, k_ref, v_ref, lam_ref, gsub_ref, o_ref,
                 bias_ref, m_ref, acc_ref, sa_ref, sb_ref, qs_ref, *, lam_init):
    hg = pl.program_id(0)
    b = pl.program_id(1)
    bq, bk = ATT_BQ, ATT_BK
    heads = range(ATT_HG)
    hsl = [slice(hh * HEAD_DIM, (hh + 1) * HEAD_DIM) for hh in heads]

    @pl.when(b == 0)
    def _():
        blk = LANES
        i = lax.broadcasted_iota(jnp.int32, (blk, blk), 0)
        jj = lax.broadcasted_iota(jnp.int32, (blk, blk), 1)
        for hh in heads:
            head = hg * ATT_HG + hh
            far = rb_ref[N_BUCKETS - 1, head]

            def toeplitz(offset):
                bucket = _t5_bucket(jnp.maximum(i - jj + offset, 0))
                out = jnp.zeros((blk, blk), F32)
                for cc in range(N_BUCKETS):
                    out = jnp.where(bucket == cc, rb_ref[cc, head] - far, out)
                return out

            on_diag = jnp.where(i >= jj, toeplitz(0), NEG_BIG)
            next_diag = toeplitz(blk)
            kinds = {0: on_diag, 1: next_diag}
            bias_ref[hh, 2] = jnp.zeros((bq, bk), F32)
            for slot in range(2):
                for rr in range(bq // blk):
                    for cc in range(bk // blk):
                        delta = rr - cc + slot * (bk // blk)
                        if delta < 0:
                            tile = jnp.full((blk, blk), NEG_BIG, F32)
                        else:
                            tile = kinds.get(delta, jnp.zeros((blk, blk), F32))
                        bias_ref[hh, slot, rr * blk:(rr + 1) * blk, cc * blk:(cc + 1) * blk] = tile

    lane = lax.broadcasted_iota(jnp.int32, (bq, HEAD_DIM), 1)
    ones_col = (lax.broadcasted_iota(jnp.int32, (bk, HEAD_DIM), 1) == 0).astype(BF16)
    lam_p = lam_ref[...]
    s1 = jnp.sum(lam_p[0:1] * lam_p[1:2], axis=-1, keepdims=True)
    s2 = jnp.sum(lam_p[2:3] * lam_p[3:4], axis=-1, keepdims=True)
    lam = jnp.exp(s1) - jnp.exp(s2) + lam_init

    n_q = q_ref.shape[0] // bq

    def stack_q(qb):
        rows = pl.ds(pl.multiple_of(qb * bq, bq), bq)
        for hh in heads:
            q = q_ref[rows, hsl[hh]]
            zero = jnp.zeros_like(q)
            qs_ref[hh, 0:bq, :] = jnp.where(lane < DH_DIFF, q, zero)
            qs_ref[hh, bq:2 * bq, :] = jnp.where(lane < DH_DIFF, zero, q)

    def scores(j, s_ref):
        ks = pl.multiple_of(j * bk, bk)
        for hh in heads:
            s_ref[hh] = lax.dot_general(qs_ref[hh], k_ref[pl.ds(ks, bk), hsl[hh]],
                                        (((1,), (1,)), ((), ())), preferred_element_type=F32)

    stack_q(0)
    scores(0, sa_ref)

    def query_block(qi, outer):
        rows = pl.ds(pl.multiple_of(qi * bq, bq), bq)
        m_ref[...] = jnp.full(m_ref.shape, NEG_BIG, F32)
        acc_ref[...] = jnp.zeros(acc_ref.shape, F32)

        def absorb(j, s_ref, biased=True):
            ks = pl.multiple_of(j * bk, bk)
            v_exts = [jnp.concatenate([v_ref[pl.ds(ks, bk), hs], ones_col], axis=1) for hs in hsl]
            if biased:
                slot = jnp.minimum(qi - j, 2)
                scs = [jnp.concatenate([s_ref[hh, 0:bq, :] + bias_ref[hh, slot],
                                        s_ref[hh, bq:2 * bq, :] + bias_ref[hh, slot]], axis=0)
                       for hh in heads]
            else:
                scs = [s_ref[hh] for hh in heads]
            m_olds = [m_ref[hh] for hh in heads]
            m_news = [jnp.maximum(mo, jnp.max(sc, axis=-1, keepdims=True)) for mo, sc in zip(m_olds, scs)]
            ps = [jnp.exp(sc - mn) for sc, mn in zip(scs, m_news)]
            pvs = [jnp.dot(p.astype(BF16), ve, preferred_element_type=F32) for p, ve in zip(ps, v_exts)]
            for hh in heads:
                acc_ref[hh] = jnp.exp(m_olds[hh] - m_news[hh]) * acc_ref[hh] + pvs[hh]
                m_ref[hh] = m_news[hh]

        n_tiles = qi + 1

        def pair_body(jj, carry, biased):
            j0 = 2 * jj
            scores(j0 + 1, sb_ref)
            absorb(j0, sa_ref, biased)
            scores(jnp.minimum(j0 + 2, qi), sa_ref)
            absorb(j0 + 1, sb_ref, biased)
            return carry

        n_far_pairs = jnp.maximum(qi - 1, 0) // 2
        lax.fori_loop(0, n_far_pairs, functools.partial(pair_body, biased=False), 0)
        lax.fori_loop(n_far_pairs, n_tiles // 2, functools.partial(pair_body, biased=True), 0)

        @pl.when(n_tiles % 2 == 1)
        def _():
            absorb(qi, sa_ref)

        stack_q(jnp.minimum(qi + 1, n_q - 1))
        scores(0, sa_ref)
        for hh in heads:
            acc = acc_ref[hh]
            num = acc[:, :HEAD_DIM]
            den = acc[:, HEAD_DIM:HEAD_DIM + 1]
            o = num[:bq] / den[:bq] - lam * (num[bq:] / den[bq:])
            o = o * lax.rsqrt(jnp.mean(o * o, axis=-1, keepdims=True) + EPS) * gsub_ref[...]
            o_ref[rows, hsl[hh]] = (o * (1.0 - lam_init)).astype(o_ref.dtype)
        return outer

    lax.fori_loop(0, n_q, query_block, 0)


def _diff_attention(proj_qk, proj_plain, rel_bias, lam_params, g_subln, bsz, seq, n_heads, d_model,
                    lam_init):
    t = bsz * seq
    assert ATT_BQ == ATT_BK and MAX_DISTANCE <= LANES and n_heads % ATT_HG == 0 and seq % ATT_BQ == 0
    width = ATT_HG * HEAD_DIM
    per = d_model // width
    vcol = 4 * per
    return pl.pallas_call(
        functools.partial(_attn_kernel, lam_init=lam_init),
        grid=(n_heads // ATT_HG, bsz),
        in_specs=[
            pl.BlockSpec(memory_space=pltpu.SMEM),
            pl.BlockSpec((seq, width), lambda h, b: (b, h)),
            pl.BlockSpec((seq, width), lambda h, b: (b, per + h)),
            pl.BlockSpec((seq, width), lambda h, b: (b, vcol + h)),
            pl.BlockSpec((4, DH_DIFF), lambda h, b: (0, 0)),
            pl.BlockSpec((1, HEAD_DIM), lambda h, b: (0, 0)),
        ],
        out_specs=pl.BlockSpec((seq, width), lambda h, b: (b, h)),
        out_shape=jax.ShapeDtypeStruct((t, d_model), BF16),
        scratch_shapes=[
            pltpu.VMEM((ATT_HG, 3, ATT_BQ, ATT_BK), F32),
            pltpu.VMEM((ATT_HG, 2 * ATT_BQ, 1), F32),
            pltpu.VMEM((ATT_HG, 2 * ATT_BQ, 2 * HEAD_DIM), F32),
            pltpu.VMEM((ATT_HG, 2 * ATT_BQ, ATT_BK), F32),
            pltpu.VMEM((ATT_HG, 2 * ATT_BQ, ATT_BK), F32),
            pltpu.VMEM((ATT_HG, 2 * ATT_BQ, HEAD_DIM), BF16),
        ],
        compiler_params=_cparams(("arbitrary", "arbitrary")),
        name="diff_attention",
    )(rel_bias, proj_qk, proj_qk, proj_plain, lam_params, g_subln)


def _mix_kernel(wg_ref, bg_ref, gmix_ref, oa_ref, od_ref, x_ref, wo_ref, gffn_ref, wr_ref, br_ref,
                x1_ref, h2_ref, topi_ref, topw_ref, rank_ref, cnt_ref, carry_ref):
    i = pl.program_id(0)
    tm = MIX_TM
    d = x_ref.shape[1]

    def gated_mix(rs):
        x = x_ref[rs, :]
        h = (x * lax.rsqrt(jnp.mean(x * x, axis=-1, keepdims=True) + EPS) * gmix_ref[...]).astype(BF16)
        cols = []
        for c in range(d // PROJ_CHUNK):
            cs = slice(c * PROJ_CHUNK, (c + 1) * PROJ_CHUNK)
            cb = slice(d + c * PROJ_CHUNK, d + (c + 1) * PROJ_CHUNK)
            ga = _sigmoid(jnp.dot(h, wg_ref[:, cs], preferred_element_type=F32) + bg_ref[:, cs]).astype(BF16)
            gb = _sigmoid(jnp.dot(h, wg_ref[:, cb], preferred_element_type=F32) + bg_ref[:, cb]).astype(BF16)
            cols.append(ga * oa_ref[rs, cs] + gb * od_ref[rs, cs])
        return jnp.concatenate(cols, axis=1)

    @pl.when(i == 0)
    def _():
        carry_ref[...] = jnp.zeros_like(carry_ref)

    tp = tm // MIX_PARTS
    parts = range(MIX_PARTS)
    rows = [slice(pp * tp, (pp + 1) * tp) for pp in parts]
    mixes = [gated_mix(rs) for rs in rows]
    x1s = [x_ref[rs, :] + jnp.dot(mx, wo_ref[...], preferred_element_type=F32) for rs, mx in zip(rows, mixes)]
    for rs, x1 in zip(rows, x1s):
        x1_ref[rs, :] = x1
    h2s = [x1 * lax.rsqrt(jnp.mean(x1 * x1, axis=-1, keepdims=True) + EPS) * gffn_ref[...] for x1 in x1s]
    for rs, h2 in zip(rows, h2s):
        h2_ref[rs, :] = _pack_halves(h2)

    curs = [_bdot_nt(wr_ref[...], h2) + br_ref[...] for h2 in h2s]
    eidx = lax.broadcasted_iota(jnp.int32, curs[0].shape, 0).astype(F32)
    vals = [[] for _ in parts]
    hots = [[] for _ in parts]
    for kk in range(TOP_K):
        mxs = [jnp.max(cur, axis=0, keepdims=True) for cur in curs]
        idxs = [jnp.min(jnp.where(cur == mx, eidx, float(N_EXPERTS)), axis=0, keepdims=True)
                for cur, mx in zip(curs, mxs)]
        for pp in parts:
            hot = eidx == idxs[pp]
            vals[pp].append(mxs[pp])
            hots[pp].append(hot)
            topi_ref[kk:kk + 1, rows[pp]] = idxs[pp].astype(jnp.int32)
            curs[pp] = jnp.where(hot, -jnp.inf, curs[pp])
    for pp in parts:
        exps = [jnp.exp(vv - vals[pp][0]) for vv in vals[pp]]
        denom = exps[0] + exps[1] + exps[2] + exps[3]
        for kk in range(TOP_K):
            topw_ref[kk:kk + 1, rows[pp]] = exps[kk] / denom

    r = lax.broadcasted_iota(jnp.int32, (tp, tp), 0)
    c = lax.broadcasted_iota(jnp.int32, (tp, tp), 1)
    earlier = (r < c).astype(F32)
    sel_fs = []
    for pp in parts:
        sel = hots[pp][0]
        for kk in range(1, TOP_K):
            sel = jnp.logical_or(sel, hots[pp][kk])
        sel_fs.append(sel.astype(F32))
    within = [_bdot(sf, earlier) for sf in sel_fs]
    totals = [jnp.sum(sf, axis=-1, keepdims=True) for sf in sel_fs]
    run = carry_ref[...]
    for pp in parts:
        before = within[pp] + run
        for kk in range(TOP_K):
            rank_ref[kk:kk + 1, rows[pp]] = jnp.sum(jnp.where(hots[pp][kk], before, 0.0), axis=0,
                                                    keepdims=True).astype(jnp.int32)
        run = run + totals[pp]
    carry_ref[...] = run
    cnt_ref[...] = run.astype(jnp.int32)


def _mix_project_route(w_gate, b_gate, g_mix, oa, od, x2d, w_o, g_ffn, w_r_t, b_r, d_model):
    t = x2d.shape[0]
    tm = MIX_TM
    full = lambda shape: pl.BlockSpec(shape, lambda i: (0, 0))
    row = lambda: pl.BlockSpec((tm, d_model), lambda i: (i, 0))
    krow = lambda: pl.BlockSpec((TOP_K, tm), lambda i: (0, i))
    return pl.pallas_call(
        _mix_kernel,
        grid=(t // tm,),
        in_specs=[
            full((d_model, 2 * d_model)), full((1, 2 * d_model)), full((1, d_model)),
            row(), row(), row(),
            full((d_model, d_model)), full((1, d_model)), full((N_EXPERTS, d_model)), full((N_EXPERTS, 1)),
        ],
        out_specs=[row(), pl.BlockSpec((tm, d_model // 2), lambda i: (i, 0)),
                   krow(), krow(), krow(), full((N_EXPERTS, 1))],
        out_shape=[
            jax.ShapeDtypeStruct((t, d_model), F32),
            jax.ShapeDtypeStruct((t, d_model // 2), jnp.int32),
            jax.ShapeDtypeStruct((TOP_K, t), jnp.int32),
            jax.ShapeDtypeStruct((TOP_K, t), F32),
            jax.ShapeDtypeStruct((TOP_K, t), jnp.int32),
            jax.ShapeDtypeStruct((N_EXPERTS, 1), jnp.int32),
        ],
        scratch_shapes=[pltpu.VMEM((N_EXPERTS, 1), F32)],
        compiler_params=_cparams(("arbitrary",)),
        name="merge_outproj_route",
    )(w_gate, b_gate, g_mix, oa, od, x2d, w_o, g_ffn, w_r_t, b_r)


def _pack_halves(x):
    half = x.shape[1] // 2
    bits = pltpu.bitcast(x.astype(BF16).astype(F32), jnp.int32)
    return bits[:, :half] | lax.shift_right_logical(bits[:, half:], 16)


def _unpack_halves(p):
    hi = pltpu.bitcast(p & jnp.int32(-65536), F32)
    lo = pltpu.bitcast(lax.shift_left(p, 16), F32)
    return jnp.concatenate([hi, lo], axis=1)


def _expert_kernel(be_ref, nu_ref, ne_ref, x_ref, wup_hbm, bup_ref, wdn_hbm, bdn_ref, *rest):
    y_ref, wup_bf, wdn_bf, wup_f, wdn_f, sem = rest[-6:]
    i = pl.program_id(0)
    d_ff = wdn_f.shape[0]

    def weight_copies(e):
        return (pltpu.make_async_copy(wup_hbm.at[e], wup_f, sem.at[0]),
                pltpu.make_async_copy(wdn_hbm.at[e], wdn_f, sem.at[1]))

    @pl.when(i == 0)
    def _():
        for cp in weight_copies(be_ref[0]):
            cp.start()

    @pl.when(jnp.logical_or(i == 0, be_ref[i] != be_ref[jnp.maximum(i - 1, 0)]))
    def _():
        for cp in weight_copies(be_ref[i]):
            cp.wait()
        rr = lax.broadcasted_iota(jnp.int32, (2 * LANES, 2 * LANES), 0)
        cc = lax.broadcasted_iota(jnp.int32, (2 * LANES, 2 * LANES), 1)
        pick = jnp.where(cc < LANES, 2 * cc, 2 * (cc - LANES) + 1)
        perm = (rr == pick).astype(BF16)
        for g in range(wup_f.shape[1] // (2 * LANES)):
            cs = slice(g * 2 * LANES, (g + 1) * 2 * LANES)
            wup_bf[:, cs] = jnp.dot(wup_f[:, cs].astype(BF16), perm,
                                    preferred_element_type=F32).astype(BF16)
        wdn_bf[...] = wdn_f[...].astype(BF16)

        @pl.when(ne_ref[i] >= 0)
        def _():
            for cp in weight_copies(ne_ref[i]):
                cp.start()

    @pl.when(i < nu_ref[0])
    def _():
        x = _unpack_halves(x_ref[...])
        hid = jnp.dot(x.astype(BF16), wup_bf[...], preferred_element_type=F32) + bup_ref[0]
        acts = []
        for g in range(hid.shape[1] // (2 * LANES)):
            glu = jnp.minimum(hid[:, g * 2 * LANES:g * 2 * LANES + LANES], SWIGLU_LIMIT)
            lin = jnp.clip(hid[:, g * 2 * LANES + LANES:(g + 1) * 2 * LANES], -SWIGLU_LIMIT, SWIGLU_LIMIT)
            acts.append(glu * _sigmoid(SWIGLU_ALPHA * glu) * (lin + 1.0))
        act = jnp.concatenate(acts, axis=1)
        assert act.shape[1] == d_ff
        y = jnp.dot(act.astype(BF16), wdn_bf[...], preferred_element_type=F32) + bdn_ref[0]
        y_ref[...] = _pack_halves(y)

    @pl.when(i >= nu_ref[0])
    def _():
        y_ref[...] = jnp.zeros(y_ref.shape, y_ref.dtype)


def _experts(block_e, n_used, xs, y_prev, first_block, n_rows_total, w_up, b_up, w_down, b_down):
    n_rows, half = xs.shape
    d = w_up.shape[1]
    nb = n_rows // MOE_RB
    two_ff = w_up.shape[2]
    d_ff = w_down.shape[1]
    blk = jnp.arange(nb, dtype=jnp.int32)
    later = jnp.logical_and(blk[None, :] > blk[:, None], block_e[None, :] != block_e[:, None])
    next_e = jnp.min(jnp.where(later, block_e[None, :], N_EXPERTS), axis=1)
    next_e = jnp.where(next_e == N_EXPERTS, -1, next_e).astype(jnp.int32)
    in_specs = [
        pl.BlockSpec((MOE_RB, half), lambda i, be, nu, ne: (jnp.maximum(jnp.minimum(i, nu[0] - 1), 0), 0)),
        pl.BlockSpec(memory_space=pl.ANY),
        pl.BlockSpec((1, 1, two_ff), lambda i, be, nu, ne: (be[i], 0, 0)),
        pl.BlockSpec(memory_space=pl.ANY),
        pl.BlockSpec((1, 1, d), lambda i, be, nu, ne: (be[i], 0, 0)),
    ]
    operands = [block_e, n_used, next_e, xs, w_up, b_up, w_down, b_down]
    aliases = {}
    if y_prev is not None:
        in_specs.append(pl.BlockSpec(memory_space=pl.ANY))
        aliases = {len(operands): 0}
        operands.append(y_prev)
    grid_spec = pltpu.PrefetchScalarGridSpec(
        num_scalar_prefetch=3,
        grid=(nb,),
        in_specs=in_specs,
        out_specs=pl.BlockSpec((MOE_RB, half), lambda i, be, nu, ne: (first_block + i, 0)),
        scratch_shapes=[pltpu.VMEM((d, two_ff), BF16), pltpu.VMEM((d_ff, d), BF16),
                        pltpu.VMEM((d, two_ff), F32), pltpu.VMEM((d_ff, d), F32),
                        pltpu.SemaphoreType.DMA((2,))],
    )
    return pl.pallas_call(
        _expert_kernel,
        grid_spec=grid_spec,
        out_shape=jax.ShapeDtypeStruct((n_rows_total, half), jnp.int32),
        input_output_aliases=aliases,
        compiler_params=_cparams(("arbitrary",)),
        name="moe_experts",
    )(*operands)


def _sc_invert_slots(dest_flat, n_rows):
    n_assign = dest_flat.shape[0]
    n_workers = SC_CORES * SC_SUBCORES
    rows_per_w = n_rows // n_workers
    chunk = SC_SCAN_CHUNK
    assert n_rows % n_workers == 0 and rows_per_w % SC_LANES == 0 and n_assign % chunk == 0
    mesh = plsc.VectorSubcoreMesh(core_axis_name="c", subcore_axis_name="s",
                                  num_cores=SC_CORES, num_subcores=SC_SUBCORES)

    n_chunks = n_assign // chunk
    assert n_chunks % 2 == 0 and chunk % (SC_SCAN_UNROLL * SC_LANES) == 0

    def body(dest_hbm, out_hbm, dest_a, dest_b, map_v, sem_a, sem_b):
        wid = lax.axis_index("s") * SC_CORES + lax.axis_index("c")
        base = wid * rows_per_w
        lanes = lax.broadcasted_iota(jnp.int32, (SC_LANES,), 0)

        def fetch(ci, dest_v, sem):
            off = pl.multiple_of(ci * chunk, chunk)
            return pltpu.make_async_copy(dest_hbm.at[pl.ds(off, chunk)], dest_v, sem)

        def scan(ci, dest_v):
            @pl.loop(0, chunk, step=SC_SCAN_UNROLL * SC_LANES)
            def _(j0):
                for u in range(SC_SCAN_UNROLL):
                    j = j0 + u * SC_LANES
                    local = dest_v[pl.ds(j, SC_LANES)] - base
                    mine = jnp.logical_and(local >= 0, local < rows_per_w)
                    plsc.store_scatter(map_v, [jnp.where(mine, local, 0)], ci * chunk + j + lanes, mask=mine)

        fetch(0, dest_a, sem_a).start()

        @pl.loop(0, rows_per_w, step=SC_LANES)
        def _(r0):
            map_v[pl.ds(r0, SC_LANES)] = jnp.full((SC_LANES,), -1, jnp.int32)

        @pl.loop(0, n_chunks, step=2)
        def _(ci):
            fetch(ci + 1, dest_b, sem_b).start()
            fetch(ci, dest_a, sem_a).wait()
            scan(ci, dest_a)
            nxt = jnp.minimum(ci + 2, n_chunks - 1)
            fetch(nxt, dest_a, sem_a).start()
            fetch(ci + 1, dest_b, sem_b).wait()
            scan(ci + 1, dest_b)

        fetch(n_chunks - 1, dest_a, sem_a).wait()
        pltpu.sync_copy(map_v, out_hbm.at[pl.ds(base, rows_per_w)])

    return pl.kernel(
        body,
        out_type=jax.ShapeDtypeStruct((n_rows,), jnp.int32),
        mesh=mesh,
        scratch_types=[pltpu.VMEM((chunk,), jnp.int32), pltpu.VMEM((chunk,), jnp.int32),
                       pltpu.VMEM((rows_per_w,), jnp.int32),
                       pltpu.SemaphoreType.DMA, pltpu.SemaphoreType.DMA],
        compiler_params=pltpu.CompilerParams(needs_layout_passes=False),
        name="moe_slot_inverse",
    )(dest_flat)


def _sc_gather_rows(table, idx):
    n_idx = idx.shape[0]
    d = table.shape[1]
    n_workers = SC_CORES * SC_SUBCORES
    per_worker = n_idx // n_workers
    n_chunks = per_worker // SC_GATHER_ROWS
    assert n_idx % n_workers == 0 and per_worker % SC_GATHER_ROWS == 0
    mesh = plsc.VectorSubcoreMesh(core_axis_name="c", subcore_axis_name="s",
                                  num_cores=SC_CORES, num_subcores=SC_SUBCORES)

    assert n_chunks % 2 == 0

    def body(table_hbm, idx_hbm, out_hbm, idx_v, rows_a, rows_b, sem_a, sem_b):
        wid = lax.axis_index("s") * SC_CORES + lax.axis_index("c")
        base = wid * per_worker
        pltpu.sync_copy(idx_hbm.at[pl.ds(base, per_worker)], idx_v)

        def gather(ci, rows_v, sem):
            off = pl.multiple_of(ci * SC_GATHER_ROWS, SC_GATHER_ROWS)
            return pltpu.make_async_copy(table_hbm.at[idx_v.at[pl.ds(off, SC_GATHER_ROWS)]], rows_v, sem)

        def put(ci, rows_v):
            off = pl.multiple_of(ci * SC_GATHER_ROWS, SC_GATHER_ROWS)
            pltpu.sync_copy(rows_v, out_hbm.at[pl.ds(base + off, SC_GATHER_ROWS)])

        gather(0, rows_a, sem_a).start()

        @pl.loop(0, n_chunks, step=2)
        def _(ci):
            gather(ci + 1, rows_b, sem_b).start()
            gather(ci, rows_a, sem_a).wait()
            put(ci, rows_a)
            nxt = jnp.minimum(ci + 2, n_chunks - 1)
            gather(nxt, rows_a, sem_a).start()
            gather(ci + 1, rows_b, sem_b).wait()
            put(ci + 1, rows_b)

        gather(n_chunks - 1, rows_a, sem_a).wait()

    return pl.kernel(
        body,
        out_type=jax.ShapeDtypeStruct((n_idx, d), table.dtype),
        mesh=mesh,
        scratch_types=[
            pltpu.VMEM((per_worker,), jnp.int32),
            pltpu.VMEM((SC_GATHER_ROWS, d), table.dtype),
            pltpu.VMEM((SC_GATHER_ROWS, d), table.dtype),
            pltpu.SemaphoreType.DMA,
            pltpu.SemaphoreType.DMA,
        ],
        name="moe_slot_gather",
    )(table, idx)


def _combine_kernel(x1_ref, w_ref, y0_ref, y1_ref, y2_ref, y3_ref, o_ref):
    w = w_ref[...]
    out = x1_ref[...]
    for kk, y_ref in enumerate((y0_ref, y1_ref, y2_ref, y3_ref)):
        out = out + w[:, kk:kk + 1] * _unpack_halves(y_ref[...])
    o_ref[...] = out


def _combine(x1, w_tok, y_slots):
    t, d = x1.shape
    tc = COMB_TC
    nt = t // tc
    yspec = lambda kk: pl.BlockSpec((tc, d // 2), lambda i: (kk * nt + i, 0))
    return pl.pallas_call(
        _combine_kernel,
        grid=(nt,),
        in_specs=[
            pl.BlockSpec((tc, d), lambda i: (i, 0)),
            pl.BlockSpec((tc, TOP_K), lambda i: (i, 0)),
            yspec(0), yspec(1), yspec(2), yspec(3),
        ],
        out_specs=pl.BlockSpec((tc, d), lambda i: (i, 0)),
        out_shape=jax.ShapeDtypeStruct((t, d), F32),
        compiler_params=_cparams(("parallel",)),
        name="moe_combine",
    )(x1, w_tok, y_slots, y_slots, y_slots, y_slots)


def _moe(x1, h2, topi, topw, rank, counts, w_up, b_up, w_down, b_down):
    t, d = x1.shape
    n_assign = t * TOP_K
    nb = -(-n_assign // MOE_RB) + N_EXPERTS
    n_rows = nb * MOE_RB
    counts = counts[:, 0]
    padded = (counts + MOE_RB - 1) // MOE_RB * MOE_RB
    padded_end = jnp.cumsum(padded)
    padded_start = padded_end - padded
    expert_ids = jnp.arange(N_EXPERTS, dtype=jnp.int32)[:, None, None]
    start_of = jnp.sum(jnp.where(topi[None] == expert_ids, padded_start[:, None, None], 0), axis=0)
    dest = (start_of + rank).astype(jnp.int32)
    n_used = (padded_end[-1] // MOE_RB).astype(jnp.int32)
    blk = jnp.minimum(jnp.arange(nb, dtype=jnp.int32), n_used - 1)
    block_e = jnp.minimum(jnp.sum(padded_end[None, :] <= (blk * MOE_RB)[:, None], axis=1),
                          N_EXPERTS - 1).astype(jnp.int32)
    slot_of = _sc_invert_slots(dest.reshape(-1), n_rows)
    src_tok = jnp.where(slot_of < 0, jnp.arange(n_rows, dtype=jnp.int32), slot_of) % t

    nb_a = nb // 4
    y_rows = None
    for first, n_blk in ((0, nb_a), (nb_a, nb - nb_a)):
        xs = _sc_gather_rows(h2, lax.slice(src_tok, (first * MOE_RB,), ((first + n_blk) * MOE_RB,)))
        used = jnp.clip(n_used - first, 0, n_blk).reshape(1)
        y_rows = _experts(lax.slice(block_e, (first,), (first + n_blk,)), used, xs, y_rows, first, n_rows,
                          w_up, b_up, w_down, b_down)
    y_slots = _sc_gather_rows(y_rows, dest.reshape(-1))
    return _combine(x1, topw.T, y_slots)


def kernel(x, g_mix, w_in, b_gate, conv_w, a_log, dt_bias, g_delta_out, q_norm, k_norm, lambda_q1, lambda_k1, lambda_q2, lambda_k2, g_subln, rel_bias, w_o, g_ffn, w_router, b_router, w_up, b_up, w_down, b_down):
    bsz, seq, d = x.shape
    depth = g_mix.shape[0]
    n_heads = d // HEAD_DIM
    t = bsz * seq
    d_ff = w_down.shape[2]
    assert d % PROJ_TN == 0 and t % PROJ_TM == 0 and seq % GDN_TB == 0 and seq % ATT_BQ == 0
    assert t % MIX_TM == 0 and t % COMB_TC == 0 and n_heads % GDN_HG == 0
    assert (t * TOP_K) % MOE_RB == 0
    assert 2 * n_heads <= 2 * SUBLANES

    x2d = x.reshape(t, d)
    for l in range(depth):
        wl = w_in[l]
        c0 = 4 * d
        c1 = c0 + 2 * n_heads
        c2 = c1 + 2 * d
        c3 = c2 + d
        w_small = jnp.pad(wl[:, c0:c1], ((0, 0), (0, LANES - 2 * n_heads)))
        gm = g_mix[l].reshape(1, d)
        head_pad = jnp.zeros((LANES - 2 * n_heads,), F32)
        alog = jnp.concatenate([jnp.zeros((n_heads,), F32), a_log[l], head_pad])
        dtb = jnp.concatenate([jnp.zeros((n_heads,), F32), dt_bias[l], head_pad])
        rows_t = 2 * n_heads
        beta_decay = (w_small.astype(BF16), w_small[:, :rows_t].T.astype(BF16),
                      alog.reshape(1, LANES), dtb.reshape(1, LANES),
                      alog[:rows_t].reshape(rows_t, 1), dtb[:rows_t].reshape(rows_t, 1), n_heads)
        w_plain = jnp.concatenate([wl[:, :c0], wl[:, c2:c3]], axis=1).astype(BF16)
        proj_plain, small, small_t = _input_projection(x2d, gm, w_plain, jnp.zeros((1, 5 * d), F32), "plain",
                                                       beta_decay)
        qk_gain = jnp.concatenate([jnp.tile(q_norm[l] * (DH_DIFF ** -0.5), 2 * n_heads),
                                   jnp.tile(k_norm[l], 2 * n_heads)]).reshape(1, 2 * d)
        proj_qk = _input_projection(x2d, gm, wl[:, c1:c2].astype(BF16), qk_gain, "qknorm")

        oa = _gated_delta(proj_plain, small, small_t, conv_w[l], g_delta_out[l].reshape(1, HEAD_DIM),
                          bsz, seq, n_heads, d)

        lam_init = 0.8 - 0.6 * math.exp(-0.3 * l)
        lam_params = jnp.stack([lambda_q1[l], lambda_k1[l], lambda_q2[l], lambda_k2[l]])
        od = _diff_attention(proj_qk, proj_plain, rel_bias, lam_params, g_subln[l].reshape(1, HEAD_DIM),
                             bsz, seq, n_heads, d, lam_init)

        x1, h2, topi, topw, rank, counts = _mix_project_route(
            wl[:, c3:].astype(BF16), b_gate[l].reshape(1, 2 * d), gm, oa, od, x2d, w_o[l].astype(BF16),
            g_ffn[l].reshape(1, d),
            w_router[l].T, b_router[l].reshape(N_EXPERTS, 1), d)

        b_up_l = b_up[l].reshape(N_EXPERTS, 2 * d_ff // (2 * LANES), LANES, 2)
        b_up_l = jnp.swapaxes(b_up_l, 2, 3).reshape(N_EXPERTS, 1, 2 * d_ff)
        x2d = _moe(x1, h2, topi, topw, rank, counts, w_up[l], b_up_l,
                   w_down[l], b_down[l].reshape(N_EXPERTS, 1, d))
    return x2d.reshape(bsz, seq, d)
```
